```python
import math
import jax, jax.numpy as jnp
from jax import lax
import numpy as np

D_MODEL = 2048
BATCH = 8
SEQ = 8192
DEPTH = 1

D_MIX = D_MODEL
D_LRU = D_MIX // 2
D_SC = D_MIX - D_LRU
N_LRU_HEADS = 8
LRU_HEAD_DIM = D_LRU // N_LRU_HEADS
N_SC_HEADS = 8
LRU_CONV_WIDTH = 4
SC_CONV_WIDTH = 3
LRU_C = 8.0
D_FF = 5632
FFN_RESIDUAL_SCALE = 0.5
NORM_EPS = 1e-6
D_IN_PROJ = 2 * D_LRU + 3 * D_SC

kernel_name = "hawk_shortconv_macaron_hybrid"


def rms_norm(x, gain):
    xf = x.astype(jnp.float32)
    var = jnp.mean(xf * xf, axis=-1, keepdims=True)
    return (xf * lax.rsqrt(var + NORM_EPS) * gain.astype(jnp.float32)).astype(x.dtype)


def swiglu_ffn(x, w_gate, w_up, w_down):
    return (jax.nn.silu(x @ w_gate) * (x @ w_up)) @ w_down


def causal_depthwise_conv(x, w):
    K = w.shape[0]
    S = x.shape[1]
    xp = jnp.pad(x, ((0, 0), (K - 1, 0), (0, 0)))
    y = xp[:, 0:S] * w[0]
    for k in range(1, K):
        y = y + xp[:, k:k + S] * w[k]
    return y


def _lru_combine(left, right):
    a_l, b_l = left
    a_r, b_r = right
    return a_l * a_r, a_r * b_l + b_r


def rg_lru(x, w_a, b_a, w_i, b_i, lam):
    Bsz, S, W = x.shape
    xh = x.reshape(Bsz, S, N_LRU_HEADS, LRU_HEAD_DIM)
    r = jax.nn.sigmoid(jnp.einsum('bshi,hij->bshj', xh, w_a) + b_a).reshape(Bsz, S, W)
    i = jax.nn.sigmoid(jnp.einsum('bshi,hij->bshj', xh, w_i) + b_i).reshape(Bsz, S, W)
    log_a = -LRU_C * r.astype(jnp.float32) * jax.nn.softplus(-lam.astype(jnp.float32))
    a = jnp.exp(log_a)
    mult = jnp.sqrt(-jnp.expm1(2.0 * log_a))
    u = mult * (i * x).astype(jnp.float32)
    _, h = lax.associative_scan(_lru_combine, (a, u), axis=1)
    return h.astype(x.dtype)


def _fwd_setup_inputs(seed: int = 0) -> dict:
    key = jax.random.key(seed)
    ks = jax.random.split(key, 32)
    f32 = jnp.float32

    def normal(k, shape, fan_in):
        return jax.random.normal(k, shape, f32) * (fan_in ** -0.5)

    def gain(k, shape):
        return 1.0 + 0.02 * jax.random.normal(k, shape, f32)

    def small(k, shape):
        return 0.01 * jax.random.normal(k, shape, f32)

    L = DEPTH
    a0 = jax.random.uniform(ks[13], (L, D_LRU), f32, 0.9, 0.999) ** (1.0 / LRU_C)
    lru_lambda = jnp.log(a0) - jnp.log1p(-a0)
    return {
        "x": jax.random.normal(ks[0], (BATCH, SEQ, D_MODEL), f32),
        "ffn1_norm": gain(ks[1], (L, D_MODEL)),
        "ffn1_w_gate": normal(ks[2], (L, D_MODEL, D_FF), D_MODEL),
        "ffn1_w_up": normal(ks[3], (L, D_MODEL, D_FF), D_MODEL),
        "ffn1_w_down": normal(ks[4], (L, D_FF, D_MODEL), D_FF),
        "mix_norm": gain(ks[5], (L, D_MODEL)),
        "w_in": normal(ks[6], (L, D_MODEL, D_IN_PROJ), D_MODEL),
        "lru_conv_w": normal(ks[7], (L, LRU_CONV_WIDTH, D_LRU), LRU_CONV_WIDTH),
        "lru_conv_b": small(ks[8], (L, D_LRU)),
        "lru_w_a": normal(ks[9], (L, N_LRU_HEADS, LRU_HEAD_DIM, LRU_HEAD_DIM), LRU_HEAD_DIM),
        "lru_b_a": small(ks[10], (L, N_LRU_HEADS, LRU_HEAD_DIM)),
        "lru_w_i": normal(ks[11], (L, N_LRU_HEADS, LRU_HEAD_DIM, LRU_HEAD_DIM), LRU_HEAD_DIM),
        "lru_b_i": small(ks[12], (L, N_LRU_HEADS, LRU_HEAD_DIM)),
        "lru_lambda": lru_lambda,
        "sc_conv_w": normal(ks[14], (L, SC_CONV_WIDTH, D_SC), SC_CONV_WIDTH),
        "lru_out_norm": gain(ks[15], (L, D_LRU)),
        "sc_out_norm": gain(ks[16], (L, D_SC)),
        "w_out": normal(ks[17], (L, D_MIX, D_MODEL), D_MIX),
        "ffn2_norm": gain(ks[18], (L, D_MODEL)),
        "ffn2_w_gate": normal(ks[19], (L, D_MODEL, D_FF), D_MODEL),
        "ffn2_w_up": normal(ks[20], (L, D_MODEL, D_FF), D_MODEL),
        "ffn2_w_down": normal(ks[21], (L, D_FF, D_MODEL), D_FF),
        "final_norm": gain(ks[22], (D_MODEL,)),
    }


def _fwd_reference(x, ffn1_norm, ffn1_w_gate, ffn1_w_up, ffn1_w_down, mix_norm, w_in,
              lru_conv_w, lru_conv_b, lru_w_a, lru_b_a, lru_w_i, lru_b_i, lru_lambda,
              sc_conv_w, lru_out_norm, sc_out_norm, w_out,
              ffn2_norm, ffn2_w_gate, ffn2_w_up, ffn2_w_down, final_norm):
    for l in range(DEPTH):
        x = x + FFN_RESIDUAL_SCALE * swiglu_ffn(rms_norm(x, ffn1_norm[l]),
                                                ffn1_w_gate[l], ffn1_w_up[l], ffn1_w_down[l])
        z = rms_norm(x, mix_norm[l]) @ w_in[l]
        o = 0
        lru_x = z[..., o:o + D_LRU]; o += D_LRU
        lru_gate = z[..., o:o + D_LRU]; o += D_LRU
        sc_b = z[..., o:o + D_SC]; o += D_SC
        sc_c = z[..., o:o + D_SC]; o += D_SC
        sc_x = z[..., o:o + D_SC]
        xc = causal_depthwise_conv(lru_x, lru_conv_w[l]) + lru_conv_b[l]
        h = rg_lru(xc, lru_w_a[l], lru_b_a[l], lru_w_i[l], lru_b_i[l], lru_lambda[l])
        y_lru = h * jax.nn.gelu(lru_gate, approximate=True)
        y_sc = sc_b * causal_depthwise_conv(sc_c * sc_x, sc_conv_w[l])
        y = jnp.concatenate([rms_norm(y_lru, lru_out_norm[l]),
                             rms_norm(y_sc, sc_out_norm[l])], axis=-1)
        x = x + y @ w_out[l]
        x = x + FFN_RESIDUAL_SCALE * swiglu_ffn(rms_norm(x, ffn2_norm[l]),
                                                ffn2_w_gate[l], ffn2_w_up[l], ffn2_w_down[l])
    return rms_norm(x, final_norm)


import jax as _jax
import jax.numpy as _jnp

TWIN_FORMAT = 'train_step'
FWD_PARAMS = ['x', 'ffn1_norm', 'ffn1_w_gate', 'ffn1_w_up', 'ffn1_w_down', 'mix_norm', 'w_in', 'lru_conv_w', 'lru_conv_b', 'lru_w_a', 'lru_b_a', 'lru_w_i', 'lru_b_i', 'lru_lambda', 'sc_conv_w', 'lru_out_norm', 'sc_out_norm', 'w_out', 'ffn2_norm', 'ffn2_w_gate', 'ffn2_w_up', 'ffn2_w_down', 'final_norm']
TWIN_WEIGHTS = ['ffn1_norm', 'ffn1_w_gate', 'ffn1_w_up', 'ffn1_w_down', 'mix_norm', 'w_in', 'lru_conv_w', 'lru_conv_b', 'lru_w_a', 'lru_b_a', 'lru_w_i', 'lru_b_i', 'lru_lambda', 'sc_conv_w', 'lru_out_norm', 'sc_out_norm', 'w_out', 'ffn2_norm', 'ffn2_w_gate', 'ffn2_w_up', 'ffn2_w_down', 'final_norm']
TWIN_DIFF_INPUT = 'x'
TWIN_INPUTS = ['x', 'ffn1_norm', 'ffn1_w_gate', 'ffn1_w_up', 'ffn1_w_down', 'mix_norm', 'w_in', 'lru_conv_w', 'lru_conv_b', 'lru_w_a', 'lru_b_a', 'lru_w_i', 'lru_b_i', 'lru_lambda', 'sc_conv_w', 'lru_out_norm', 'sc_out_norm', 'w_out', 'ffn2_norm', 'ffn2_w_gate', 'ffn2_w_up', 'ffn2_w_down', 'final_norm', 'loss_target', 'm_ffn1_norm', 'm_ffn1_w_gate', 'm_ffn1_w_up', 'm_ffn1_w_down', 'm_mix_norm', 'm_w_in', 'm_lru_conv_w', 'm_lru_conv_b', 'm_lru_w_a', 'm_lru_b_a', 'm_lru_w_i', 'm_lru_b_i', 'm_lru_lambda', 'm_sc_conv_w', 'm_lru_out_norm', 'm_sc_out_norm', 'm_w_out', 'm_ffn2_norm', 'm_ffn2_w_gate', 'm_ffn2_w_up', 'm_ffn2_w_down', 'm_final_norm', 'v_ffn1_norm', 'v_ffn1_w_gate', 'v_ffn1_w_up', 'v_ffn1_w_down', 'v_mix_norm', 'v_w_in', 'v_lru_conv_w', 'v_lru_conv_b', 'v_lru_w_a', 'v_lru_b_a', 'v_lru_w_i', 'v_lru_b_i', 'v_lru_lambda', 'v_sc_conv_w', 'v_lru_out_norm', 'v_sc_out_norm', 'v_w_out', 'v_ffn2_norm', 'v_ffn2_w_gate', 'v_ffn2_w_up', 'v_ffn2_w_down', 'v_final_norm']
TWIN_OUTPUTS = ['loss', 'grad_x', 'grad_ffn1_norm', 'grad_ffn1_w_gate', 'grad_ffn1_w_up', 'grad_ffn1_w_down', 'grad_mix_norm', 'grad_w_in', 'grad_lru_conv_w', 'grad_lru_conv_b', 'grad_lru_w_a', 'grad_lru_b_a', 'grad_lru_w_i', 'grad_lru_b_i', 'grad_lru_lambda', 'grad_sc_conv_w', 'grad_lru_out_norm', 'grad_sc_out_norm', 'grad_w_out', 'grad_ffn2_norm', 'grad_ffn2_w_gate', 'grad_ffn2_w_up', 'grad_ffn2_w_down', 'grad_final_norm', 'delta_ffn1_norm', 'delta_ffn1_w_gate', 'delta_ffn1_w_up', 'delta_ffn1_w_down', 'delta_mix_norm', 'delta_w_in', 'delta_lru_conv_w', 'delta_lru_conv_b', 'delta_lru_w_a', 'delta_lru_b_a', 'delta_lru_w_i', 'delta_lru_b_i', 'delta_lru_lambda', 'delta_sc_conv_w', 'delta_lru_out_norm', 'delta_sc_out_norm', 'delta_w_out', 'delta_ffn2_norm', 'delta_ffn2_w_gate', 'delta_ffn2_w_up', 'delta_ffn2_w_down', 'delta_final_norm', 'new_m_ffn1_norm', 'new_m_ffn1_w_gate', 'new_m_ffn1_w_up', 'new_m_ffn1_w_down', 'new_m_mix_norm', 'new_m_w_in', 'new_m_lru_conv_w', 'new_m_lru_conv_b', 'new_m_lru_w_a', 'new_m_lru_b_a', 'new_m_lru_w_i', 'new_m_lru_b_i', 'new_m_lru_lambda', 'new_m_sc_conv_w', 'new_m_lru_out_norm', 'new_m_sc_out_norm', 'new_m_w_out', 'new_m_ffn2_norm', 'new_m_ffn2_w_gate', 'new_m_ffn2_w_up', 'new_m_ffn2_w_down', 'new_m_final_norm', 'new_v_ffn1_norm', 'new_v_ffn1_w_gate', 'new_v_ffn1_w_up', 'new_v_ffn1_w_down', 'new_v_mix_norm', 'new_v_w_in', 'new_v_lru_conv_w', 'new_v_lru_conv_b', 'new_v_lru_w_a', 'new_v_lru_b_a', 'new_v_lru_w_i', 'new_v_lru_b_i', 'new_v_lru_lambda', 'new_v_sc_conv_w', 'new_v_lru_out_norm', 'new_v_sc_out_norm', 'new_v_w_out', 'new_v_ffn2_norm', 'new_v_ffn2_w_gate', 'new_v_ffn2_w_up', 'new_v_ffn2_w_down', 'new_v_final_norm']
TWIN_LEAF_KINDS = {'loss': 'loss', 'grad_x': 'grad_x', 'grad_ffn1_norm': 'grad_w', 'grad_ffn1_w_gate': 'grad_w', 'grad_ffn1_w_up': 'grad_w', 'grad_ffn1_w_down': 'grad_w', 'grad_mix_norm': 'grad_w', 'grad_w_in': 'grad_w', 'grad_lru_conv_w': 'grad_w', 'grad_lru_conv_b': 'grad_w', 'grad_lru_w_a': 'grad_w', 'grad_lru_b_a': 'grad_w', 'grad_lru_w_i': 'grad_w', 'grad_lru_b_i': 'grad_w', 'grad_lru_lambda': 'grad_w', 'grad_sc_conv_w': 'grad_w', 'grad_lru_out_norm': 'grad_w', 'grad_sc_out_norm': 'grad_w', 'grad_w_out': 'grad_w', 'grad_ffn2_norm': 'grad_w', 'grad_ffn2_w_gate': 'grad_w', 'grad_ffn2_w_up': 'grad_w', 'grad_ffn2_w_down': 'grad_w', 'grad_final_norm': 'grad_w', 'delta_ffn1_norm': 'delta_w', 'delta_ffn1_w_gate': 'delta_w', 'delta_ffn1_w_up': 'delta_w', 'delta_ffn1_w_down': 'delta_w', 'delta_mix_norm': 'delta_w', 'delta_w_in': 'delta_w', 'delta_lru_conv_w': 'delta_w', 'delta_lru_conv_b': 'delta_w', 'delta_lru_w_a': 'delta_w', 'delta_lru_b_a': 'delta_w', 'delta_lru_w_i': 'delta_w', 'delta_lru_b_i': 'delta_w', 'delta_lru_lambda': 'delta_w', 'delta_sc_conv_w': 'delta_w', 'delta_lru_out_norm': 'delta_w', 'delta_sc_out_norm': 'delta_w', 'delta_w_out': 'delta_w', 'delta_ffn2_norm': 'delta_w', 'delta_ffn2_w_gate': 'delta_w', 'delta_ffn2_w_up': 'delta_w', 'delta_ffn2_w_down': 'delta_w', 'delta_final_norm': 'delta_w', 'new_m_ffn1_norm': 'new_m', 'new_m_ffn1_w_gate': 'new_m', 'new_m_ffn1_w_up': 'new_m', 'new_m_ffn1_w_down': 'new_m', 'new_m_mix_norm': 'new_m', 'new_m_w_in': 'new_m', 'new_m_lru_conv_w': 'new_m', 'new_m_lru_conv_b': 'new_m', 'new_m_lru_w_a': 'new_m', 'new_m_lru_b_a': 'new_m', 'new_m_lru_w_i': 'new_m', 'new_m_lru_b_i': 'new_m', 'new_m_lru_lambda': 'new_m', 'new_m_sc_conv_w': 'new_m', 'new_m_lru_out_norm': 'new_m', 'new_m_sc_out_norm': 'new_m', 'new_m_w_out': 'new_m', 'new_m_ffn2_norm': 'new_m', 'new_m_ffn2_w_gate': 'new_m', 'new_m_ffn2_w_up': 'new_m', 'new_m_ffn2_w_down': 'new_m', 'new_m_final_norm': 'new_m', 'new_v_ffn1_norm': 'new_v', 'new_v_ffn1_w_gate': 'new_v', 'new_v_ffn1_w_up': 'new_v', 'new_v_ffn1_w_down': 'new_v', 'new_v_mix_norm': 'new_v', 'new_v_w_in': 'new_v', 'new_v_lru_conv_w': 'new_v', 'new_v_lru_conv_b': 'new_v', 'new_v_lru_w_a': 'new_v', 'new_v_lru_b_a': 'new_v', 'new_v_lru_w_i': 'new_v', 'new_v_lru_b_i': 'new_v', 'new_v_lru_lambda': 'new_v', 'new_v_sc_conv_w': 'new_v', 'new_v_lru_out_norm': 'new_v', 'new_v_sc_out_norm': 'new_v', 'new_v_w_out': 'new_v', 'new_v_ffn2_norm': 'new_v', 'new_v_ffn2_w_gate': 'new_v', 'new_v_ffn2_w_up': 'new_v', 'new_v_ffn2_w_down': 'new_v', 'new_v_final_norm': 'new_v'}


def _forward(args):
    return _fwd_reference(*[args[k] for k in FWD_PARAMS])


def _output_shape():
    def fwd():
        inp = _fwd_setup_inputs(0)
        return _fwd_reference(*[inp[k] for k in FWD_PARAMS])
    out = _jax.eval_shape(fwd)
    return out.shape, out.dtype

N_MICROBATCH = 1
ADAM_LR = 0.001
ADAM_B1 = 0.9
ADAM_B2 = 0.999
ADAM_EPS = 1e-08
ADAM_WD = 0.01
ADAM_STEP = 10
PER_EXAMPLE_BATCH_AXIS = {'x': 0, 'loss_target': 0}
SHARED_INPUTS = []
_WEIGHT_DTYPES = {'ffn1_norm': _jnp.float32, 'ffn1_w_gate': _jnp.float32, 'ffn1_w_up': _jnp.float32, 'ffn1_w_down': _jnp.float32, 'mix_norm': _jnp.float32, 'w_in': _jnp.float32, 'lru_conv_w': _jnp.float32, 'lru_conv_b': _jnp.float32, 'lru_w_a': _jnp.float32, 'lru_b_a': _jnp.float32, 'lru_w_i': _jnp.float32, 'lru_b_i': _jnp.float32, 'lru_lambda': _jnp.float32, 'sc_conv_w': _jnp.float32, 'lru_out_norm': _jnp.float32, 'sc_out_norm': _jnp.float32, 'w_out': _jnp.float32, 'ffn2_norm': _jnp.float32, 'ffn2_w_gate': _jnp.float32, 'ffn2_w_up': _jnp.float32, 'ffn2_w_down': _jnp.float32, 'final_norm': _jnp.float32}
MOMENT_SCALE = {'ffn1_norm': 6.978920e-02, 'ffn1_w_gate': 3.020124e-02, 'ffn1_w_up': 2.923539e-02, 'ffn1_w_down': 4.852070e-02, 'mix_norm': 1.437040e-01, 'w_in': 8.993179e-02, 'lru_conv_w': 9.816641e-02, 'lru_conv_b': 9.889910e-01, 'lru_w_a': 2.958217e-02, 'lru_b_a': 2.419408e-02, 'lru_w_i': 5.337639e-02, 'lru_b_i': 3.307069e-02, 'lru_lambda': 5.117747e-02, 'sc_conv_w': 9.296191e-02, 'lru_out_norm': 1.038777e-01, 'sc_out_norm': 9.300869e-02, 'w_out': 8.917583e-02, 'ffn2_norm': 3.558320e-02, 'ffn2_w_gate': 1.580960e-02, 'ffn2_w_up': 1.529151e-02, 'ffn2_w_down': 2.538026e-02, 'final_norm': 3.198578e+01}


def _to_microbatches(a, axis):
    t = _jnp.moveaxis(a, axis, 0)
    t = t.reshape((N_MICROBATCH, t.shape[0] // N_MICROBATCH) + t.shape[1:])
    return _jnp.moveaxis(t, 1, axis + 1)


def setup_inputs(seed: int = 0) -> dict:
    inp = _fwd_setup_inputs(seed)
    key = _jax.random.fold_in(_jax.random.key(seed), 7919)
    shape, _ = _output_shape()
    out = dict(inp)
    out["loss_target"] = _jax.random.normal(_jax.random.fold_in(key, 0), shape, _jnp.float32)
    for i, name in enumerate(TWIN_WEIGHTS):
        w = inp[name].astype(_jnp.float32)
        if MOMENT_SCALE is None:
            s = _jnp.sqrt(_jnp.mean(_jnp.square(w)) + 1e-30)
        else:
            s = MOMENT_SCALE[name]
        km, kv = _jax.random.split(_jax.random.fold_in(key, i + 1))
        out[name] = w
        out["m_" + name] = s * _jax.random.normal(km, w.shape, _jnp.float32)
        out["v_" + name] = (s * s) * _jax.random.uniform(kv, w.shape, _jnp.float32, 0.5, 1.5)
    if N_MICROBATCH > 1:
        for name, axis in PER_EXAMPLE_BATCH_AXIS.items():
            out[name] = _to_microbatches(out[name], axis)
    return {'x': out['x'], 'ffn1_norm': out['ffn1_norm'], 'ffn1_w_gate': out['ffn1_w_gate'], 'ffn1_w_up': out['ffn1_w_up'], 'ffn1_w_down': out['ffn1_w_down'], 'mix_norm': out['mix_norm'], 'w_in': out['w_in'], 'lru_conv_w': out['lru_conv_w'], 'lru_conv_b': out['lru_conv_b'], 'lru_w_a': out['lru_w_a'], 'lru_b_a': out['lru_b_a'], 'lru_w_i': out['lru_w_i'], 'lru_b_i': out['lru_b_i'], 'lru_lambda': out['lru_lambda'], 'sc_conv_w': out['sc_conv_w'], 'lru_out_norm': out['lru_out_norm'], 'sc_out_norm': out['sc_out_norm'], 'w_out': out['w_out'], 'ffn2_norm': out['ffn2_norm'], 'ffn2_w_gate': out['ffn2_w_gate'], 'ffn2_w_up': out['ffn2_w_up'], 'ffn2_w_down': out['ffn2_w_down'], 'final_norm': out['final_norm'], 'loss_target': out['loss_target'], 'm_ffn1_norm': out['m_ffn1_norm'], 'm_ffn1_w_gate': out['m_ffn1_w_gate'], 'm_ffn1_w_up': out['m_ffn1_w_up'], 'm_ffn1_w_down': out['m_ffn1_w_down'], 'm_mix_norm': out['m_mix_norm'], 'm_w_in': out['m_w_in'], 'm_lru_conv_w': out['m_lru_conv_w'], 'm_lru_conv_b': out['m_lru_conv_b'], 'm_lru_w_a': out['m_lru_w_a'], 'm_lru_b_a': out['m_lru_b_a'], 'm_lru_w_i': out['m_lru_w_i'], 'm_lru_b_i': out['m_lru_b_i'], 'm_lru_lambda': out['m_lru_lambda'], 'm_sc_conv_w': out['m_sc_conv_w'], 'm_lru_out_norm': out['m_lru_out_norm'], 'm_sc_out_norm': out['m_sc_out_norm'], 'm_w_out': out['m_w_out'], 'm_ffn2_norm': out['m_ffn2_norm'], 'm_ffn2_w_gate': out['m_ffn2_w_gate'], 'm_ffn2_w_up': out['m_ffn2_w_up'], 'm_ffn2_w_down': out['m_ffn2_w_down'], 'm_final_norm': out['m_final_norm'], 'v_ffn1_norm': out['v_ffn1_norm'], 'v_ffn1_w_gate': out['v_ffn1_w_gate'], 'v_ffn1_w_up': out['v_ffn1_w_up'], 'v_ffn1_w_down': out['v_ffn1_w_down'], 'v_mix_norm': out['v_mix_norm'], 'v_w_in': out['v_w_in'], 'v_lru_conv_w': out['v_lru_conv_w'], 'v_lru_conv_b': out['v_lru_conv_b'], 'v_lru_w_a': out['v_lru_w_a'], 'v_lru_b_a': out['v_lru_b_a'], 'v_lru_w_i': out['v_lru_w_i'], 'v_lru_b_i': out['v_lru_b_i'], 'v_lru_lambda': out['v_lru_lambda'], 'v_sc_conv_w': out['v_sc_conv_w'], 'v_lru_out_norm': out['v_lru_out_norm'], 'v_sc_out_norm': out['v_sc_out_norm'], 'v_w_out': out['v_w_out'], 'v_ffn2_norm': out['v_ffn2_norm'], 'v_ffn2_w_gate': out['v_ffn2_w_gate'], 'v_ffn2_w_up': out['v_ffn2_w_up'], 'v_ffn2_w_down': out['v_ffn2_w_down'], 'v_final_norm': out['v_final_norm']}


def _loss(weights, diff, rest, loss_target):
    with _jax.named_scope("forward"):
        args = {**rest, TWIN_DIFF_INPUT: diff, **{k: w.astype(_WEIGHT_DTYPES[k]) for k, w in weights.items()}}
        y = _forward(args)
    with _jax.named_scope("loss_head"):
        err = _jnp.square(y.astype(_jnp.float32) - loss_target)
        return 0.5 * _jnp.sum(_jnp.mean(err, axis=-1)) if err.ndim else 0.5 * err


def _adamw(w, g, m, v):
    m = ADAM_B1 * m + (1.0 - ADAM_B1) * g
    v = ADAM_B2 * v + (1.0 - ADAM_B2) * _jnp.square(g)
    m_hat = m / (1.0 - ADAM_B1 ** ADAM_STEP)
    v_hat = v / (1.0 - ADAM_B2 ** ADAM_STEP)
    delta = -ADAM_LR * (m_hat / (_jnp.sqrt(v_hat) + ADAM_EPS) + ADAM_WD * w)
    return delta, m, v


def reference(x, ffn1_norm, ffn1_w_gate, ffn1_w_up, ffn1_w_down, mix_norm, w_in, lru_conv_w, lru_conv_b, lru_w_a, lru_b_a, lru_w_i, lru_b_i, lru_lambda, sc_conv_w, lru_out_norm, sc_out_norm, w_out, ffn2_norm, ffn2_w_gate, ffn2_w_up, ffn2_w_down, final_norm, loss_target, m_ffn1_norm, m_ffn1_w_gate, m_ffn1_w_up, m_ffn1_w_down, m_mix_norm, m_w_in, m_lru_conv_w, m_lru_conv_b, m_lru_w_a, m_lru_b_a, m_lru_w_i, m_lru_b_i, m_lru_lambda, m_sc_conv_w, m_lru_out_norm, m_sc_out_norm, m_w_out, m_ffn2_norm, m_ffn2_w_gate, m_ffn2_w_up, m_ffn2_w_down, m_final_norm, v_ffn1_norm, v_ffn1_w_gate, v_ffn1_w_up, v_ffn1_w_down, v_mix_norm, v_w_in, v_lru_conv_w, v_lru_conv_b, v_lru_w_a, v_lru_b_a, v_lru_w_i, v_lru_b_i, v_lru_lambda, v_sc_conv_w, v_lru_out_norm, v_sc_out_norm, v_w_out, v_ffn2_norm, v_ffn2_w_gate, v_ffn2_w_up, v_ffn2_w_down, v_final_norm):
    given = dict(x=x, ffn1_norm=ffn1_norm, ffn1_w_gate=ffn1_w_gate, ffn1_w_up=ffn1_w_up, ffn1_w_down=ffn1_w_down, mix_norm=mix_norm, w_in=w_in, lru_conv_w=lru_conv_w, lru_conv_b=lru_conv_b, lru_w_a=lru_w_a, lru_b_a=lru_b_a, lru_w_i=lru_w_i, lru_b_i=lru_b_i, lru_lambda=lru_lambda, sc_conv_w=sc_conv_w, lru_out_norm=lru_out_norm, sc_out_norm=sc_out_norm, w_out=w_out, ffn2_norm=ffn2_norm, ffn2_w_gate=ffn2_w_gate, ffn2_w_up=ffn2_w_up, ffn2_w_down=ffn2_w_down, final_norm=final_norm, loss_target=loss_target, m_ffn1_norm=m_ffn1_norm, m_ffn1_w_gate=m_ffn1_w_gate, m_ffn1_w_up=m_ffn1_w_up, m_ffn1_w_down=m_ffn1_w_down, m_mix_norm=m_mix_norm, m_w_in=m_w_in, m_lru_conv_w=m_lru_conv_w, m_lru_conv_b=m_lru_conv_b, m_lru_w_a=m_lru_w_a, m_lru_b_a=m_lru_b_a, m_lru_w_i=m_lru_w_i, m_lru_b_i=m_lru_b_i, m_lru_lambda=m_lru_lambda, m_sc_conv_w=m_sc_conv_w, m_lru_out_norm=m_lru_out_norm, m_sc_out_norm=m_sc_out_norm, m_w_out=m_w_out, m_ffn2_norm=m_ffn2_norm, m_ffn2_w_gate=m_ffn2_w_gate, m_ffn2_w_up=m_ffn2_w_up, m_ffn2_w_down=m_ffn2_w_down, m_final_norm=m_final_norm, v_ffn1_norm=v_ffn1_norm, v_ffn1_w_gate=v_ffn1_w_gate, v_ffn1_w_up=v_ffn1_w_up, v_ffn1_w_down=v_ffn1_w_down, v_mix_norm=v_mix_norm, v_w_in=v_w_in, v_lru_conv_w=v_lru_conv_w, v_lru_conv_b=v_lru_conv_b, v_lru_w_a=v_lru_w_a, v_lru_b_a=v_lru_b_a, v_lru_w_i=v_lru_w_i, v_lru_b_i=v_lru_b_i, v_lru_lambda=v_lru_lambda, v_sc_conv_w=v_sc_conv_w, v_lru_out_norm=v_lru_out_norm, v_sc_out_norm=v_sc_out_norm, v_w_out=v_w_out, v_ffn2_norm=v_ffn2_norm, v_ffn2_w_gate=v_ffn2_w_gate, v_ffn2_w_up=v_ffn2_w_up, v_ffn2_w_down=v_ffn2_w_down, v_final_norm=v_final_norm)
    weights = {n: given[n] for n in TWIN_WEIGHTS}
    shared = {n: given[n] for n in SHARED_INPUTS}
    per_example = {n: given[n] for n in ['x']}
    grad_fn = _jax.value_and_grad(_loss, argnums=(0, 1))

    def one_microbatch(ex, loss_target):
        ex = dict(ex)
        diff = ex.pop(TWIN_DIFF_INPUT)
        return grad_fn(weights, diff, {**shared, **ex}, loss_target)

    if N_MICROBATCH == 1:
        loss, (grad_w, grad_x) = one_microbatch(per_example, given["loss_target"])
    else:
        def body(carry, xs):
            loss_sum, grad_sum = carry
            l_k, (gw_k, gx_k) = one_microbatch(xs[0], xs[1])
            with _jax.named_scope("update"):
                return (loss_sum + l_k, _jax.tree.map(_jnp.add, grad_sum, gw_k)), gx_k

        init = (_jnp.zeros((), _jnp.float32), _jax.tree.map(_jnp.zeros_like, weights))
        (loss, grad_w), grad_x = _jax.lax.scan(body, init, (per_example, given["loss_target"]))
    with _jax.named_scope("update"):
        delta_w, new_m, new_v = {}, {}, {}
        for n in TWIN_WEIGHTS:
            delta_w[n], new_m[n], new_v[n] = _adamw(weights[n], grad_w[n], given["m_" + n], given["v_" + n])
    return (loss, grad_x, *[grad_w[n] for n in TWIN_WEIGHTS], *[delta_w[n] for n in TWIN_WEIGHTS],
            *[new_m[n] for n in TWIN_WEIGHTS], *[new_v[n] for n in TWIN_WEIGHTS])
```

```python
import math

import jax
import jax.numpy as jnp
from jax import lax
from jax.experimental import pallas as pl
from jax.experimental.pallas import tpu as pltpu

F32 = jnp.float32
BF16 = jnp.bfloat16
MESH = pl.DeviceIdType.MESH
ANY = pl.BlockSpec(memory_space=pl.ANY)

NORM_EPS = 1e-6
LRU_C = 8.0
FFN_RESIDUAL_SCALE = 0.5
ADAM_LR = 0.001
ADAM_B1 = 0.9
ADAM_B2 = 0.999
ADAM_EPS = 1e-08
ADAM_WD = 0.01
ADAM_STEP = 10

V7X_VMEM_BYTES = 64 * 2**20
VMEM_LIMIT = V7X_VMEM_BYTES - 8 * 2**20
LANES = 128
SUBLANES = 8
PACK_W = LANES
PACK_ALIGN = SUBLANES * PACK_W

WEIGHTS = ['ffn1_norm', 'ffn1_w_gate', 'ffn1_w_up', 'ffn1_w_down', 'mix_norm', 'w_in', 'lru_conv_w', 'lru_conv_b',
           'lru_w_a', 'lru_b_a', 'lru_w_i', 'lru_b_i', 'lru_lambda', 'sc_conv_w', 'lru_out_norm', 'sc_out_norm',
           'w_out', 'ffn2_norm', 'ffn2_w_gate', 'ffn2_w_up', 'ffn2_w_down', 'final_norm']
BIG = ['ffn1_w_gate', 'ffn1_w_up', 'ffn1_w_down', 'w_in', 'w_out', 'ffn2_w_gate', 'ffn2_w_up', 'ffn2_w_down']
BIG_AXIS = [1, 1, 0, 1, 0, 1, 1, 0]
SMALL_SHARDED = ['lru_conv_w', 'sc_conv_w']


def _tile(n, pref, mult):
    if n <= pref:
        return n
    t = (pref // mult) * mult
    while t >= mult:
        if n % t == 0:
            return t
        t -= mult
    return n


def _params(*sem):
    return pltpu.CompilerParams(dimension_semantics=sem, vmem_limit_bytes=VMEM_LIMIT)


def _me():
    return lax.axis_index("x"), lax.axis_index("y"), lax.axis_index("c")


def _sigmoid(v):
    return 1.0 / (1.0 + jnp.exp(-v))


def _rstd(v):
    return lax.rsqrt(jnp.mean(v * v, axis=-1, keepdims=True) + NORM_EPS)


def _rms_bwd(dy, v, gain):
    r = _rstd(v)
    w = gain * dy
    dv = r * w - v * (r * r * r) * jnp.mean(v * w, axis=-1, keepdims=True)
    dgain = jnp.sum(dy * v * r, axis=0, keepdims=True)
    return dv, dgain


def _dot_nt(a, b):
    return lax.dot_general(a, b, (((1,), (1,)), ((), ())), preferred_element_type=F32)


def _dot_tn(a, b):
    return lax.dot_general(a, b, (((0,), (0,)), ((), ())), preferred_element_type=F32)


def _ffn_fwd(x, gain, wg, wu, wd, name):
    T, D = x.shape
    FF = wg.shape[1]
    tm = _tile(T, 512, 16)
    tf = _tile(FF, 512, LANES)
    nf = FF // tf

    def body(x_ref, g_ref, wg_ref, wu_ref, wd_ref, xo_ref, n_ref, G_ref, U_ref, acc_ref):
        f = pl.program_id(1)

        @pl.when(f == 0)
        def _():
            xv = x_ref[...]
            n_ref[...] = (xv * _rstd(xv) * g_ref[...]).astype(BF16)
            acc_ref[...] = jnp.zeros_like(acc_ref)

        n = n_ref[...]
        G = jnp.dot(n, wg_ref[...], preferred_element_type=F32)
        U = jnp.dot(n, wu_ref[...], preferred_element_type=F32)
        G_ref[...] = G.astype(BF16)
        U_ref[...] = U.astype(BF16)
        H = (G * _sigmoid(G) * U).astype(BF16)
        acc_ref[...] += jnp.dot(H, wd_ref[...], preferred_element_type=F32)

        @pl.when(f == nf - 1)
        def _():
            xo_ref[...] = x_ref[...] + FFN_RESIDUAL_SCALE * acc_ref[...]

    return pl.pallas_call(
        body, name=name, grid=(T // tm, nf),
        in_specs=[pl.BlockSpec((tm, D), lambda i, f: (i, 0)),
                  pl.BlockSpec((1, D), lambda i, f: (0, 0)),
                  pl.BlockSpec((D, tf), lambda i, f: (0, f)),
                  pl.BlockSpec((D, tf), lambda i, f: (0, f)),
                  pl.BlockSpec((tf, D), lambda i, f: (f, 0))],
        out_specs=[pl.BlockSpec((tm, D), lambda i, f: (i, 0)),
                   pl.BlockSpec((tm, D), lambda i, f: (i, 0)),
                   pl.BlockSpec((tm, tf), lambda i, f: (i, f)),
                   pl.BlockSpec((tm, tf), lambda i, f: (i, f))],
        out_shape=[jax.ShapeDtypeStruct((T, D), F32), jax.ShapeDtypeStruct((T, D), BF16),
                   jax.ShapeDtypeStruct((T, FF), BF16), jax.ShapeDtypeStruct((T, FF), BF16)],
        scratch_shapes=[pltpu.VMEM((tm, D), F32)],
        compiler_params=_params("parallel", "arbitrary"),
    )(x, gain, wg, wu, wd)


def _ffn_bwd_act(db, G, U, wg, wu, wd, name):
    T, D = db.shape
    FF = wg.shape[1]
    tm = _tile(T, 512, 16)
    tf = _tile(FF, 512, LANES)

    def body(d_ref, G_ref, U_ref, wg_ref, wu_ref, wd_ref, dG_ref, dU_ref, H_ref, dn_ref):
        f = pl.program_id(1)
        dH = _dot_nt(d_ref[...], wd_ref[...])
        Gv = G_ref[...].astype(F32)
        Uv = U_ref[...].astype(F32)
        s = _sigmoid(Gv)
        sg = Gv * s
        H_ref[...] = (sg * Uv).astype(BF16)
        dU = (dH * sg).astype(BF16)
        dG = (dH * Uv * (s * (1.0 + Gv * (1.0 - s)))).astype(BF16)
        dG_ref[...] = dG
        dU_ref[...] = dU
        contrib = _dot_nt(dG, wg_ref[...]) + _dot_nt(dU, wu_ref[...])

        @pl.when(f == 0)
        def _():
            dn_ref[...] = contrib

        @pl.when(f > 0)
        def _():
            dn_ref[...] += contrib

    return pl.pallas_call(
        body, name=name, grid=(T // tm, FF // tf),
        in_specs=[pl.BlockSpec((tm, D), lambda i, f: (i, 0)),
                  pl.BlockSpec((tm, tf), lambda i, f: (i, f)),
                  pl.BlockSpec((tm, tf), lambda i, f: (i, f)),
                  pl.BlockSpec((D, tf), lambda i, f: (0, f)),
                  pl.BlockSpec((D, tf), lambda i, f: (0, f)),
                  pl.BlockSpec((tf, D), lambda i, f: (f, 0))],
        out_specs=[pl.BlockSpec((tm, tf), lambda i, f: (i, f)),
                   pl.BlockSpec((tm, tf), lambda i, f: (i, f)),
                   pl.BlockSpec((tm, tf), lambda i, f: (i, f)),
                   pl.BlockSpec((tm, D), lambda i, f: (i, 0))],
        out_shape=[jax.ShapeDtypeStruct((T, FF), BF16), jax.ShapeDtypeStruct((T, FF), BF16),
                   jax.ShapeDtypeStruct((T, FF), BF16), jax.ShapeDtypeStruct((T, D), F32)],
        compiler_params=_params("parallel", "arbitrary"),
    )(db, G, U, wg, wu, wd)


def _rms_bwd_res(dn, x, gain, dres, scale, name):
    T, D = x.shape
    tm = _tile(T, 256, 16)

    def body(dn_ref, x_ref, g_ref, dr_ref, dx_ref, dxb_ref, dg_ref):
        i = pl.program_id(0)
        dv, dgain = _rms_bwd(dn_ref[...], x_ref[...], g_ref[...])
        dx = dr_ref[...] + dv
        dx_ref[...] = dx
        dxb_ref[...] = (scale * dx).astype(BF16)

        @pl.when(i == 0)
        def _():
            dg_ref[...] = dgain

        @pl.when(i > 0)
        def _():
            dg_ref[...] += dgain

    row = pl.BlockSpec((tm, D), lambda i: (i, 0))
    vec = pl.BlockSpec((1, D), lambda i: (0, 0))
    return pl.pallas_call(
        body, name=name, grid=(T // tm,),
        in_specs=[row, row, vec, row], out_specs=[row, row, vec],
        out_shape=[jax.ShapeDtypeStruct((T, D), F32), jax.ShapeDtypeStruct((T, D), BF16),
                   jax.ShapeDtypeStruct((1, D), F32)],
        compiler_params=_params("arbitrary"),
    )(dn, x, gain, dres)


def _loss_head(x3, gain, target, name):
    T, D = x3.shape
    tm = _tile(T, 256, 16)

    def body(x_ref, g_ref, t_ref, dx_ref, dxb_ref, ls_ref, dg_ref):
        i = pl.program_id(0)
        xv = x_ref[...]
        err = xv * _rstd(xv) * g_ref[...] - t_ref[...]
        sq = jnp.sum(jnp.sum(err * err, axis=1, keepdims=True), axis=0, keepdims=True)
        dv, dgain = _rms_bwd(err * (1.0 / D), xv, g_ref[...])
        dx_ref[...] = dv
        dxb_ref[...] = (FFN_RESIDUAL_SCALE * dv).astype(BF16)
        sqb = jnp.broadcast_to(sq, (1, LANES))

        @pl.when(i == 0)
        def _():
            dg_ref[...] = dgain
            ls_ref[...] = sqb

        @pl.when(i > 0)
        def _():
            dg_ref[...] += dgain
            ls_ref[...] += sqb

    row = pl.BlockSpec((tm, D), lambda i: (i, 0))
    vec = pl.BlockSpec((1, D), lambda i: (0, 0))
    return pl.pallas_call(
        body, name=name, grid=(T // tm,),
        in_specs=[row, vec, row],
        out_specs=[row, row, pl.BlockSpec((1, LANES), lambda i: (0, 0)), vec],
        out_shape=[jax.ShapeDtypeStruct((T, D), F32), jax.ShapeDtypeStruct((T, D), BF16),
                   jax.ShapeDtypeStruct((1, LANES), F32), jax.ShapeDtypeStruct((1, D), F32)],
        compiler_params=_params("arbitrary"),
    )(x3, gain, target)


def _norm_mm(x, gain, w, name):
    T, D = x.shape
    N = w.shape[1]
    tm = _tile(T, 512, 16)
    tn = _tile(N, 1280, LANES)

    def body(x_ref, g_ref, w_ref, n_ref, z_ref):
        @pl.when(pl.program_id(1) == 0)
        def _():
            xv = x_ref[...]
            n_ref[...] = (xv * _rstd(xv) * g_ref[...]).astype(BF16)

        z_ref[...] = jnp.dot(n_ref[...], w_ref[...], preferred_element_type=F32)

    return pl.pallas_call(
        body, name=name, grid=(T // tm, N // tn),
        in_specs=[pl.BlockSpec((tm, D), lambda i, j: (i, 0)),
                  pl.BlockSpec((1, D), lambda i, j: (0, 0)),
                  pl.BlockSpec((D, tn), lambda i, j: (0, j))],
        out_specs=[pl.BlockSpec((tm, D), lambda i, j: (i, 0)),
                   pl.BlockSpec((tm, tn), lambda i, j: (i, j))],
        out_shape=[jax.ShapeDtypeStruct((T, D), BF16), jax.ShapeDtypeStruct((T, N), F32)],
        compiler_params=_params("parallel", "arbitrary"),
    )(x, gain, w)


def _mm_fullk(a, w, trans_w, residual, name):
    T, K = a.shape
    N = w.shape[0] if trans_w else w.shape[1]
    tm = _tile(T, 512, 16)
    tn = _tile(N, 512, LANES)

    def body(*refs):
        if residual is None:
            a_ref, w_ref, o_ref = refs
        else:
            a_ref, w_ref, r_ref, o_ref = refs
        if trans_w:
            acc = _dot_nt(a_ref[...], w_ref[...])
        else:
            acc = jnp.dot(a_ref[...], w_ref[...], preferred_element_type=F32)
        if residual is not None:
            acc = acc + r_ref[...]
        o_ref[...] = acc

    w_spec = pl.BlockSpec((tn, K), lambda i, j: (j, 0)) if trans_w else pl.BlockSpec((K, tn), lambda i, j: (0, j))
    in_specs = [pl.BlockSpec((tm, K), lambda i, j: (i, 0)), w_spec]
    args = [a, w]
    if residual is not None:
        in_specs.append(pl.BlockSpec((tm, tn), lambda i, j: (i, j)))
        args.append(residual)
    return pl.pallas_call(
        body, name=name, grid=(T // tm, N // tn),
        in_specs=in_specs, out_specs=pl.BlockSpec((tm, tn), lambda i, j: (i, j)),
        out_shape=jax.ShapeDtypeStruct((T, N), F32),
        compiler_params=_params("parallel", "arbitrary"),
    )(*args)


def _mm_tn(a, b, name):
    T, M = a.shape
    N = b.shape[1]
    tmw = _tile(M, 2048, LANES)
    tnw = _tile(N, 2048 * 1408 // tmw, LANES)
    tk = _tile(T, 512, 16)
    nk = T // tk

    def body(a_ref, b_ref, o_ref, acc_ref):
        k = pl.program_id(2)

        @pl.when(k == 0)
        def _():
            acc_ref[...] = jnp.zeros_like(acc_ref)

        acc_ref[...] += _dot_tn(a_ref[...], b_ref[...])

        @pl.when(k == nk - 1)
        def _():
            o_ref[...] = acc_ref[...].astype(BF16)

    return pl.pallas_call(
        body, name=name, grid=(M // tmw, N // tnw, nk),
        in_specs=[pl.BlockSpec((tk, tmw), lambda i, j, k: (k, i)),
                  pl.BlockSpec((tk, tnw), lambda i, j, k: (k, j))],
        out_specs=pl.BlockSpec((tmw, tnw), lambda i, j, k: (i, j)),
        out_shape=jax.ShapeDtypeStruct((M, N), BF16),
        scratch_shapes=[pltpu.VMEM((tmw, tnw), F32)],
        compiler_params=_params("parallel", "parallel", "arbitrary"),
    )(a, b)


GELU_K = math.sqrt(2.0 / math.pi)
GELU_C = 0.044715


def _gelu_and_grad(v):
    u = GELU_K * (v + GELU_C * v * v * v)
    th = jnp.tanh(u)
    g = 0.5 * v * (1.0 + th)
    dg = 0.5 * (1.0 + th) + 0.5 * v * (1.0 - th * th) * GELU_K * (1.0 + 3.0 * GELU_C * v * v)
    return g, dg


def _neg_expm1(v):
    poly = v * (1.0 + v * (0.5 + v * (1.0 / 6 + v * (1.0 / 24 + v * (1.0 / 120 + v * (1.0 / 720))))))
    return jnp.where(v > -0.25, -poly, 1.0 - jnp.exp(v))


def _softplus_neg(lam):
    e = jnp.exp(-jnp.abs(lam))
    log1pe = jnp.where(e < 1e-4, e * (1.0 - 0.5 * e), jnp.log(1.0 + e))
    sp = jnp.maximum(-lam, 0.0) + log1pe
    dsp = -1.0 / (1.0 + jnp.exp(lam))
    return sp, dsp


def _earlier(ext, j):
    return pltpu.roll(ext, j, 0)[SUBLANES:, :]


def _later(ext, j):
    n = ext.shape[0]
    return pltpu.roll(ext, n - j, 0)[:n - SUBLANES, :]


def _taps(v, halo, K):
    ext = jnp.concatenate([halo, v], axis=0)
    return [v] + [_earlier(ext, j) for j in range(1, K)]


def _block_diag(vb, w_ref, nh, hd):
    return jnp.concatenate(
        [jnp.dot(vb[:, h * hd:(h + 1) * hd], w_ref[h], preferred_element_type=F32) for h in range(nh)], axis=1)


def _lru_gates(xc, wa_ref, ba_ref, wi_ref, bi_ref, sp, nh, hd):
    xcb = xc.astype(BF16)
    r = _sigmoid(_block_diag(xcb, wa_ref, nh, hd) + ba_ref[...])
    ig = _sigmoid(_block_diag(xcb, wi_ref, nh, hd) + bi_ref[...])
    log_a = -LRU_C * r * sp
    a = jnp.exp(log_a)
    mult = jnp.sqrt(_neg_expm1(2.0 * log_a))
    return xcb, r, ig, a, mult


def _mix_fwd(z, cw, cb, wa, ba, wi, bi, lam, sw, glo, gso, name):
    T = z.shape[0]
    DL = cb.shape[1]
    DS = gso.shape[1]
    NH, HD = wa.shape[0], wa.shape[1]
    KL, KS = cw.shape[0], sw.shape[0]
    tt = _tile(T, 128, 16)
    o_g, o_b, o_c, o_x = DL, 2 * DL, 2 * DL + DS, 2 * DL + 2 * DS

    def body(z_ref, cw_ref, cb_ref, wa_ref, ba_ref, wi_ref, bi_ref, lam_ref, sw_ref, glo_ref, gso_ref,
             h_ref, y_ref, cx_ref, cp_ref, ch_ref):
        @pl.when(pl.program_id(0) == 0)
        def _():
            cx_ref[...] = jnp.zeros_like(cx_ref)
            cp_ref[...] = jnp.zeros_like(cp_ref)
            ch_ref[...] = jnp.zeros_like(ch_ref)

        lx = z_ref[:, 0:DL]
        xs = _taps(lx, cx_ref[...], KL)
        cx_ref[...] = lx[tt - SUBLANES:, :]
        xc = cb_ref[...] + xs[0] * cw_ref[KL - 1:KL, :]
        for j in range(1, KL):
            xc = xc + xs[j] * cw_ref[KL - 1 - j:KL - j, :]
        sp, _ = _softplus_neg(lam_ref[...])
        _, _, ig, a, mult = _lru_gates(xc, wa_ref, ba_ref, wi_ref, bi_ref, sp, NH, HD)
        b = mult * (ig * xc)
        rows = lax.broadcasted_iota(jnp.int32, (tt, DL), 0)
        s = 1
        while s < tt:
            keep = rows >= s
            b = jnp.where(keep, a * pltpu.roll(b, s, 0) + b, b)
            a = jnp.where(keep, a * pltpu.roll(a, s, 0), a)
            s *= 2
        h = a * ch_ref[SUBLANES - 1:SUBLANES, :] + b
        ch_ref[...] = h[tt - SUBLANES:, :]
        h_ref[...] = h
        ge, _ = _gelu_and_grad(z_ref[:, o_g:o_g + DL])
        ylru = h * ge
        y_ref[:, 0:DL] = (ylru * _rstd(ylru) * glo_ref[...]).astype(BF16)

        p = z_ref[:, o_c:o_c + DS] * z_ref[:, o_x:o_x + DS]
        ps = _taps(p, cp_ref[...], KS)
        cp_ref[...] = p[tt - SUBLANES:, :]
        cv = ps[0] * sw_ref[KS - 1:KS, :]
        for j in range(1, KS):
            cv = cv + ps[j] * sw_ref[KS - 1 - j:KS - j, :]
        ysc = z_ref[:, o_b:o_b + DS] * cv
        y_ref[:, DL:DL + DS] = (ysc * _rstd(ysc) * gso_ref[...]).astype(BF16)

    def full(shape):
        return pl.BlockSpec(shape, lambda t: (0,) * len(shape))

    return pl.pallas_call(
        body, name=name, grid=(T // tt,),
        in_specs=[pl.BlockSpec((tt, z.shape[1]), lambda t: (t, 0)),
                  full(cw.shape), full(cb.shape), full(wa.shape), full(ba.shape), full(wi.shape), full(bi.shape),
                  full(lam.shape), full(sw.shape), full(glo.shape), full(gso.shape)],
        out_specs=[pl.BlockSpec((tt, DL), lambda t: (t, 0)), pl.BlockSpec((tt, DL + DS), lambda t: (t, 0))],
        out_shape=[jax.ShapeDtypeStruct((T, DL), F32), jax.ShapeDtypeStruct((T, DL + DS), BF16)],
        scratch_shapes=[pltpu.VMEM((SUBLANES, DL), F32), pltpu.VMEM((SUBLANES, DS), F32),
                        pltpu.VMEM((SUBLANES, DL), F32)],
        compiler_params=_params("arbitrary"),
    )(z, cw, cb, wa, ba, wi, bi, lam, sw, glo, gso)


V_BA, V_BI, V_LAM, V_CB, V_CW, V_SW, V_GLO, V_GSO, V_ROWS = 0, 1, 2, 3, 4, 8, 11, 12, 16


def _mix_bwd(z, h, dy, cw, cb, wa, ba, wi, bi, lam, sw, glo, gso, name):
    T = z.shape[0]
    DL = cb.shape[1]
    DS = gso.shape[1]
    NH, HD = wa.shape[0], wa.shape[1]
    KL, KS = cw.shape[0], sw.shape[0]
    tt = _tile(T, 64, 16)
    nt = T // tt
    hb = tt // SUBLANES
    o_g, o_b, o_c, o_x = DL, 2 * DL, 2 * DL + DS, 2 * DL + 2 * DS

    def body(z_ref, zh_ref, h_ref, hh_ref, dy_ref, cw_ref, cb_ref, wa_ref, ba_ref, wi_ref, bi_ref, lam_ref,
             sw_ref, glo_ref, gso_ref, dz_ref, dwa_ref, dwi_ref, vec_ref, cdx_ref, cdc_ref, cdh_ref):
        i = pl.program_id(0)
        tr = nt - 1 - i

        @pl.when(i == 0)
        def _():
            dwa_ref[...] = jnp.zeros_like(dwa_ref)
            dwi_ref[...] = jnp.zeros_like(dwi_ref)
            vec_ref[...] = jnp.zeros_like(vec_ref)
            cdx_ref[...] = jnp.zeros_like(cdx_ref)
            cdc_ref[...] = jnp.zeros_like(cdc_ref)
            cdh_ref[...] = jnp.zeros_like(cdh_ref)

        def acc_row(r, v):
            vec_ref[pl.ds(r, 1), :] += jnp.sum(v, axis=0, keepdims=True)

        has_prev = tr > 0
        rows = lax.broadcasted_iota(jnp.int32, (tt, DL), 0)

        lx = z_ref[:, 0:DL]
        xs = _taps(lx, jnp.where(has_prev, zh_ref[:, 0:DL], 0.0), KL)
        xc = cb_ref[...] + xs[0] * cw_ref[KL - 1:KL, :]
        for j in range(1, KL):
            xc = xc + xs[j] * cw_ref[KL - 1 - j:KL - j, :]
        sp, dsp = _softplus_neg(lam_ref[...])
        xcb, r, ig, a, mult = _lru_gates(xc, wa_ref, ba_ref, wi_ref, bi_ref, sp, NH, HD)
        hv = h_ref[...]
        hprev = _earlier(jnp.concatenate([jnp.where(has_prev, hh_ref[...], 0.0), hv], axis=0), 1)
        gate = z_ref[:, o_g:o_g + DL]
        ge, dge = _gelu_and_grad(gate)
        ylru = hv * ge

        d_ylru, dglo = _rms_bwd(dy_ref[:, 0:DL], ylru, glo_ref[...])
        vec_ref[pl.ds(V_GLO, 1), :] += dglo
        dz_ref[:, o_g:o_g + DL] = (d_ylru * hv * dge).astype(BF16)
        bq = d_ylru * ge
        aq = jnp.where(rows == tt - 1, 1.0, pltpu.roll(a, tt - 1, 0))
        s = 1
        while s < tt:
            keep = rows < tt - s
            bq = jnp.where(keep, aq * pltpu.roll(bq, tt - s, 0) + bq, bq)
            aq = jnp.where(keep, aq * pltpu.roll(aq, tt - s, 0), aq)
            s *= 2
        dhh = bq + aq * cdh_ref[0:1, :]
        cdh_ref[0:1, :] = a[0:1, :] * dhh[0:1, :]

        da = dhh * hprev
        dmult = dhh * (ig * xc)
        d_i = dhh * mult * xc
        dxc = dhh * mult * ig
        dlog = da * a - dmult * (a * a) / mult
        acc_row(V_LAM, dlog * (-LRU_C * r) * dsp)
        dpa = dlog * (-LRU_C * sp) * r * (1.0 - r)
        dpi = d_i * ig * (1.0 - ig)
        acc_row(V_BA, dpa)
        acc_row(V_BI, dpi)
        dpab = dpa.astype(BF16)
        dpib = dpi.astype(BF16)
        back = []
        for hh in range(NH):
            sl = slice(hh * HD, (hh + 1) * HD)
            dwa_ref[hh] += _dot_tn(xcb[:, sl], dpab[:, sl])
            dwi_ref[hh] += _dot_tn(xcb[:, sl], dpib[:, sl])
            back.append(_dot_nt(dpab[:, sl], wa_ref[hh]) + _dot_nt(dpib[:, sl], wi_ref[hh]))
        dxc = dxc + jnp.concatenate(back, axis=1)

        acc_row(V_CB, dxc)
        extd = jnp.concatenate([dxc, cdx_ref[...]], axis=0)
        cdx_ref[...] = dxc[0:SUBLANES, :]
        dlx = dxc * cw_ref[KL - 1:KL, :]
        acc_row(V_CW + KL - 1, dxc * xs[0])
        for j in range(1, KL):
            dlx = dlx + _later(extd, j) * cw_ref[KL - 1 - j:KL - j, :]
            acc_row(V_CW + KL - 1 - j, dxc * xs[j])
        dz_ref[:, 0:DL] = dlx.astype(BF16)

        sb = z_ref[:, o_b:o_b + DS]
        sc = z_ref[:, o_c:o_c + DS]
        sx = z_ref[:, o_x:o_x + DS]
        p = sc * sx
        ps = _taps(p, jnp.where(has_prev, zh_ref[:, o_c:o_c + DS] * zh_ref[:, o_x:o_x + DS], 0.0), KS)
        cv = ps[0] * sw_ref[KS - 1:KS, :]
        for j in range(1, KS):
            cv = cv + ps[j] * sw_ref[KS - 1 - j:KS - j, :]
        d_ysc, dgso = _rms_bwd(dy_ref[:, DL:DL + DS], sb * cv, gso_ref[...])
        vec_ref[pl.ds(V_GSO, 1), :] += dgso
        dz_ref[:, o_b:o_b + DS] = (d_ysc * cv).astype(BF16)
        dcv = d_ysc * sb
        extc = jnp.concatenate([dcv, cdc_ref[...]], axis=0)
        cdc_ref[...] = dcv[0:SUBLANES, :]
        dp = dcv * sw_ref[KS - 1:KS, :]
        acc_row(V_SW + KS - 1, dcv * ps[0])
        for j in range(1, KS):
            dp = dp + _later(extc, j) * sw_ref[KS - 1 - j:KS - j, :]
            acc_row(V_SW + KS - 1 - j, dcv * ps[j])
        dz_ref[:, o_c:o_c + DS] = (dp * sx).astype(BF16)
        dz_ref[:, o_x:o_x + DS] = (dp * sc).astype(BF16)

    def full(shape):
        return pl.BlockSpec(shape, lambda t: (0,) * len(shape))

    def rev(t):
        return nt - 1 - t

    def halo(t):
        return jnp.maximum(rev(t) * hb - 1, 0)

    return pl.pallas_call(
        body, name=name, grid=(nt,),
        in_specs=[pl.BlockSpec((tt, z.shape[1]), lambda t: (rev(t), 0)),
                  pl.BlockSpec((SUBLANES, z.shape[1]), lambda t: (halo(t), 0)),
                  pl.BlockSpec((tt, DL), lambda t: (rev(t), 0)),
                  pl.BlockSpec((SUBLANES, DL), lambda t: (halo(t), 0)),
                  pl.BlockSpec((tt, DL + DS), lambda t: (rev(t), 0)),
                  full(cw.shape), full(cb.shape), full(wa.shape), full(ba.shape), full(wi.shape), full(bi.shape),
                  full(lam.shape), full(sw.shape), full(glo.shape), full(gso.shape)],
        out_specs=[pl.BlockSpec((tt, z.shape[1]), lambda t: (rev(t), 0)),
                   full(wa.shape), full(wi.shape), full((V_ROWS, DL))],
        out_shape=[jax.ShapeDtypeStruct(z.shape, BF16), jax.ShapeDtypeStruct(wa.shape, F32),
                   jax.ShapeDtypeStruct(wi.shape, F32), jax.ShapeDtypeStruct((V_ROWS, DL), F32)],
        scratch_shapes=[pltpu.VMEM((SUBLANES, DL), F32), pltpu.VMEM((SUBLANES, DS), F32),
                        pltpu.VMEM((SUBLANES, DL), F32)],
        compiler_params=_params("arbitrary"),
    )(z, z, h, h, dy, cw, cb, wa, ba, wi, bi, lam, sw, glo, gso)


def _pair_add(p, r1, c, name):
    G, R, C = r1.shape
    tr = _tile(R, 256, 16)
    tc = _tile(C, 1408, LANES)

    def body(c_ref, p_ref, r_ref, o_ref):
        o_ref[...] = (p_ref[...].astype(F32) + r_ref[...].astype(F32)).astype(BF16)

    blk = (None, tr, tc)
    return pl.pallas_call(
        body, name=name,
        grid_spec=pltpu.PrefetchScalarGridSpec(
            num_scalar_prefetch=1, grid=(G, R // tr, C // tc),
            in_specs=[pl.BlockSpec(blk, lambda g, i, j, cr: (2 * g + cr[0], i, j)),
                      pl.BlockSpec(blk, lambda g, i, j, cr: (g, i, j))],
            out_specs=pl.BlockSpec(blk, lambda g, i, j, cr: (g, i, j))),
        out_shape=jax.ShapeDtypeStruct((G, R, C), BF16),
        compiler_params=_params("parallel", "parallel", "parallel"),
    )(c, p, r1)


def _quad_add(s, r2, q, axis, name):
    _, R, W = r2.shape
    tr = _tile(R, 256, 16)

    def body(q_ref, s_ref, r0_ref, r1_ref, r2_ref, o_ref):
        o_ref[...] = ((s_ref[...].astype(F32) + r0_ref[...].astype(F32)) + r1_ref[...].astype(F32)) + r2_ref[...].astype(F32)

    blk = (None, tr, W)
    if axis == 1:
        own = pl.BlockSpec(blk, lambda i, qr: (0, i, qr[0]))
    else:
        own = pl.BlockSpec(blk, lambda i, qr: (qr[0], i, 0))
    return pl.pallas_call(
        body, name=name,
        grid_spec=pltpu.PrefetchScalarGridSpec(
            num_scalar_prefetch=1, grid=(R // tr,),
            in_specs=[own] + [pl.BlockSpec(blk, lambda i, qr, j=j: (j, i, 0)) for j in range(3)],
            out_specs=pl.BlockSpec((tr, W), lambda i, qr: (i, 0))),
        out_shape=jax.ShapeDtypeStruct((R, W), F32),
        compiler_params=_params("parallel"),
    )(q, s, r2, r2, r2)


def _adamw(w, g, m, v, name):
    R, C = w.shape
    tr = _tile(R, 256, SUBLANES)
    tc = _tile(C, 2048, LANES)
    c1 = 1.0 - ADAM_B1 ** ADAM_STEP
    c2 = 1.0 - ADAM_B2 ** ADAM_STEP

    def body(w_ref, g_ref, m_ref, v_ref, d_ref, mo_ref, vo_ref):
        gv = g_ref[...]
        mn = ADAM_B1 * m_ref[...] + (1.0 - ADAM_B1) * gv
        vn = ADAM_B2 * v_ref[...] + (1.0 - ADAM_B2) * (gv * gv)
        mo_ref[...] = mn
        vo_ref[...] = vn
        d_ref[...] = -ADAM_LR * ((mn / c1) / (jnp.sqrt(vn / c2) + ADAM_EPS) + ADAM_WD * w_ref[...])

    blk = pl.BlockSpec((tr, tc), lambda i, j: (i, j))
    sh = jax.ShapeDtypeStruct((R, C), F32)
    return pl.pallas_call(
        body, name=name, grid=(R // tr, C // tc),
        in_specs=[blk] * 4, out_specs=[blk] * 3, out_shape=[sh] * 3,
        compiler_params=_params("parallel", "parallel"),
    )(w, g, m, v)


def _other_chips(x, y):
    return [(1 - x, y), (x, 1 - y), (1 - x, 1 - y)]


def _remote(src, dst, send_sems, recv_sems, idx, dev):
    return pltpu.make_async_remote_copy(src_ref=src, dst_ref=dst, send_sem=send_sems.at[idx], recv_sem=recv_sems.at[idx],
                                        device_id=dev, device_id_type=MESH)


def _allgather_big(shards, name):
    M = len(shards)
    out_shape = []
    for s, ax in zip(shards, BIG_AXIS):
        _, Rh, W = s.shape
        out_shape.append(jax.ShapeDtypeStruct((2, Rh, 4 * W) if ax == 1 else (8, Rh, W), s.dtype))

    def body(*refs):
        ins, outs = refs[:M], refs[M:2 * M]
        send_sems, recv_sems, loc_sems = refs[2 * M:]
        x, y, c = _me()
        q = 2 * x + y
        chips = _other_chips(x, y)

        def win(m, qq, cc):
            if BIG_AXIS[m] == 1:
                W = shards[m].shape[2]
                return outs[m].at[cc, :, pl.ds(pl.multiple_of(qq * W, LANES), W)]
            return outs[m].at[2 * qq + cc]

        local = [pltpu.make_async_copy(ins[m].at[cc], win(m, q, cc), loc_sems.at[2 * m + cc])
                 for m in range(M) for cc in (0, 1)]
        for cp in local:
            cp.start()
        sends = []
        for j, (cx, cy) in enumerate(chips):
            for m in range(M):
                sends.append(_remote(ins[m].at[c], win(m, q, c), send_sems, recv_sems, 6 * m + j, (cx, cy, c)))
                sends[-1].start()
        for j, (cx, cy) in enumerate(chips):
            for m in range(M):
                landed = win(m, 2 * cx + cy, c)
                _remote(ins[m].at[c], landed, send_sems, recv_sems, 6 * m + j, (cx, cy, c)).wait_recv()
                sends.append(_remote(landed, landed, send_sems, recv_sems, 6 * m + 3 + j, (x, y, 1 - c)))
                sends[-1].start()
        for j, (cx, cy) in enumerate(chips):
            for m in range(M):
                landed = win(m, 2 * cx + cy, 1 - c)
                _remote(landed, landed, send_sems, recv_sems, 6 * m + 3 + j, (x, y, 1 - c)).wait_recv()
        for cp in sends:
            cp.wait_send()
        for cp in local:
            cp.wait()

    return pl.pallas_call(
        body, name=name, in_specs=[ANY] * M, out_specs=[ANY] * M, out_shape=out_shape,
        scratch_shapes=[pltpu.SemaphoreType.DMA((6 * M,)), pltpu.SemaphoreType.DMA((6 * M,)),
                        pltpu.SemaphoreType.DMA((2 * M,))],
    )(*shards)


def _pair_exchange(parts, name):
    M = len(parts)
    groups = [p.shape[0] // 2 for p in parts]
    base = [sum(groups[:m]) for m in range(M)]
    out_shape = [jax.ShapeDtypeStruct((g,) + p.shape[1:], p.dtype) for g, p in zip(groups, parts)]

    def body(*refs):
        ins, outs = refs[:M], refs[M:2 * M]
        send_sems, recv_sems = refs[2 * M:]
        x, y, c = _me()
        cps = []
        for m in range(M):
            for g in range(groups[m]):
                cps.append(_remote(ins[m].at[2 * g + 1 - c], outs[m].at[g], send_sems, recv_sems, base[m] + g,
                                   (x, y, 1 - c)))
                cps[-1].start()
        for cp in cps:
            cp.wait()

    n = sum(groups)
    return pl.pallas_call(
        body, name=name, in_specs=[ANY] * M, out_specs=[ANY] * M, out_shape=out_shape,
        scratch_shapes=[pltpu.SemaphoreType.DMA((n,)), pltpu.SemaphoreType.DMA((n,))],
    )(*parts)


def _chip_exchange(sums, name):
    M = len(sums)
    out_shape = []
    for s, ax in zip(sums, BIG_AXIS):
        _, Rh, C = s.shape
        out_shape.append(jax.ShapeDtypeStruct((3, Rh, C // 4 if ax == 1 else C), s.dtype))

    def body(*refs):
        ins, outs = refs[:M], refs[M:2 * M]
        send_sems, recv_sems = refs[2 * M:]
        x, y, c = _me()
        cps = []
        for j, (cx, cy) in enumerate(_other_chips(x, y)):
            qj = 2 * cx + cy
            for m in range(M):
                if BIG_AXIS[m] == 1:
                    W = sums[m].shape[2] // 4
                    src = ins[m].at[0, :, pl.ds(pl.multiple_of(qj * W, LANES), W)]
                else:
                    src = ins[m].at[qj]
                cps.append(_remote(src, outs[m].at[j], send_sems, recv_sems, 3 * m + j, (cx, cy, c)))
                cps[-1].start()
        for cp in cps:
            cp.wait()

    return pl.pallas_call(
        body, name=name, in_specs=[ANY] * M, out_specs=[ANY] * M, out_shape=out_shape,
        scratch_shapes=[pltpu.SemaphoreType.DMA((3 * M,)), pltpu.SemaphoreType.DMA((3 * M,))],
    )(*sums)


def _pair_share(halves, name):
    M = len(halves)
    out_shape = [jax.ShapeDtypeStruct((2,) + h.shape, h.dtype) for h in halves]

    def body(*refs):
        ins, outs = refs[:M], refs[M:2 * M]
        send_sems, recv_sems, loc_sems = refs[2 * M:]
        x, y, c = _me()
        local = [pltpu.make_async_copy(ins[m], outs[m].at[c], loc_sems.at[m]) for m in range(M)]
        for cp in local:
            cp.start()
        cps = []
        for m in range(M):
            cps.append(_remote(ins[m], outs[m].at[c], send_sems, recv_sems, m, (x, y, 1 - c)))
            cps[-1].start()
        for m in range(M):
            cps[m].wait_send()
            _remote(ins[m], outs[m].at[1 - c], send_sems, recv_sems, m, (x, y, 1 - c)).wait_recv()
        for cp in local:
            cp.wait()

    return pl.pallas_call(
        body, name=name, in_specs=[ANY] * M, out_specs=[ANY] * M, out_shape=out_shape,
        scratch_shapes=[pltpu.SemaphoreType.DMA((M,)), pltpu.SemaphoreType.DMA((M,)), pltpu.SemaphoreType.DMA((M,))],
    )(*halves)


def _allreduce_small(v, name):
    R, W = v.shape

    def body(v_ref, o_ref, gath, send_sems, recv_sems):
        x, y, c = _me()
        me = 4 * x + 2 * y + c
        gath[0] = v_ref[...]
        cps = []
        for k in range(1, 8):
            peer = (1 - x if k & 4 else x, 1 - y if k & 2 else y, 1 - c if k & 1 else c)
            cps.append(_remote(v_ref, gath.at[k], send_sems, recv_sems, k - 1, peer))
            cps[-1].start()
        for cp in cps:
            cp.wait()
        acc = gath[me]
        for p in range(1, 8):
            acc = acc + gath[jnp.bitwise_xor(me, p)]
        o_ref[...] = acc

    vm = pl.BlockSpec(memory_space=pltpu.VMEM)
    return pl.pallas_call(
        body, name=name, in_specs=[vm], out_specs=vm, out_shape=jax.ShapeDtypeStruct((R, W), F32),
        scratch_shapes=[pltpu.VMEM((8, R, W), F32), pltpu.SemaphoreType.DMA((7,)), pltpu.SemaphoreType.DMA((7,))],
        compiler_params=pltpu.CompilerParams(vmem_limit_bytes=VMEM_LIMIT),
    )(v)


def _pack(pieces):
    flat = []
    for p in pieces:
        p = p.reshape(-1).astype(F32)
        pad = (-p.shape[0]) % PACK_ALIGN
        flat.append(jnp.pad(p, (0, pad)).reshape(-1, PACK_W))
    return jnp.concatenate(flat, axis=0)


def _unpack(packed, shapes):
    out, row = [], 0
    for shp in shapes:
        n = math.prod(shp)
        rows = -(-n // PACK_ALIGN) * SUBLANES
        out.append(packed[row:row + rows].reshape(-1)[:n].reshape(shp))
        row += rows
    return out


def kernel(x, ffn1_norm, ffn1_w_gate, ffn1_w_up, ffn1_w_down, mix_norm, w_in, lru_conv_w, lru_conv_b, lru_w_a, lru_b_a, lru_w_i, lru_b_i, lru_lambda, sc_conv_w, lru_out_norm, sc_out_norm, w_out, ffn2_norm, ffn2_w_gate, ffn2_w_up, ffn2_w_down, final_norm, loss_target, m_ffn1_norm, m_ffn1_w_gate, m_ffn1_w_up, m_ffn1_w_down, m_mix_norm, m_w_in, m_lru_conv_w, m_lru_conv_b, m_lru_w_a, m_lru_b_a, m_lru_w_i, m_lru_b_i, m_lru_lambda, m_sc_conv_w, m_lru_out_norm, m_sc_out_norm, m_w_out, m_ffn2_norm, m_ffn2_w_gate, m_ffn2_w_up, m_ffn2_w_down, m_final_norm, v_ffn1_norm, v_ffn1_w_gate, v_ffn1_w_up, v_ffn1_w_down, v_mix_norm, v_w_in, v_lru_conv_w, v_lru_conv_b, v_lru_w_a, v_lru_b_a, v_lru_w_i, v_lru_b_i, v_lru_lambda, v_sc_conv_w, v_lru_out_norm, v_sc_out_norm, v_w_out, v_ffn2_norm, v_ffn2_w_gate, v_ffn2_w_up, v_ffn2_w_down, v_final_norm):
    vals = locals()
    w = {n: vals[n] for n in WEIGHTS}
    mom = {n: vals["m_" + n] for n in WEIGHTS}
    var = {n: vals["v_" + n] for n in WEIGHTS}

    xi, yi, ci = _me()
    qi = 2 * xi + yi
    c_arr = jnp.reshape(ci, (1,)).astype(jnp.int32)
    q_arr = jnp.reshape(qi, (1,)).astype(jnp.int32)

    T, D = x.shape[1], x.shape[2]
    xt = x.reshape(T, D)
    target = loss_target.reshape(T, D)
    DL = lru_conv_b.shape[-1]
    NH, HD = lru_w_a.shape[1], lru_w_a.shape[2]
    KL, KS = lru_conv_w.shape[1], sc_conv_w.shape[1]
    DLq = lru_conv_w.shape[2]

    shards = []
    for n in BIG:
        s = w[n][0].astype(BF16)
        shards.append(s.reshape(2, s.shape[0] // 2, s.shape[1]))
    gathered = _allgather_big(shards, "allgather_weights")
    full = {}
    for n, ax, g in zip(BIG, BIG_AXIS, gathered):
        full[n] = g.reshape(2 * g.shape[1], g.shape[2]) if ax == 1 else g.reshape(8 * g.shape[1], g.shape[2])

    taps = jnp.zeros((SUBLANES, DL), F32)
    taps = lax.dynamic_update_slice(taps, lru_conv_w[0], (0, qi * DLq))
    taps = lax.dynamic_update_slice(taps, sc_conv_w[0], (KL, qi * DLq))
    taps = _allreduce_small(jnp.where(ci == 0, taps, 0.0), "gather_conv_taps")
    cw, sw = taps[0:KL], taps[KL:KL + KS]

    cb = lru_conv_b
    wa, wi = lru_w_a[0].astype(BF16), lru_w_i[0].astype(BF16)
    ba, bi = lru_b_a.reshape(1, DL), lru_b_i.reshape(1, DL)
    mix_args = (cw, cb, wa, ba, wi, bi, lru_lambda, sw, lru_out_norm, sc_out_norm)
    gf = final_norm.reshape(1, D)

    x1, n1, G1, U1 = _ffn_fwd(xt, ffn1_norm, full['ffn1_w_gate'], full['ffn1_w_up'], full['ffn1_w_down'], "ffn1_fwd")
    n2, z = _norm_mm(x1, mix_norm, full['w_in'], "mix_in_proj")
    h, ymix = _mix_fwd(z, *mix_args, "mix_fwd")
    x2 = _mm_fullk(ymix, full['w_out'], False, x1, "mix_out_proj")
    x3, n3, G2, U2 = _ffn_fwd(x2, ffn2_norm, full['ffn2_w_gate'], full['ffn2_w_up'], full['ffn2_w_down'], "ffn2_fwd")
    dx3, d3b, sqerr, dgf = _loss_head(x3, gf, target, "loss_head")

    grads = {}
    dG2, dU2, H2, dn3 = _ffn_bwd_act(d3b, G2, U2, full['ffn2_w_gate'], full['ffn2_w_up'], full['ffn2_w_down'], "ffn2_bwd")
    grads['ffn2_w_gate'] = _mm_tn(n3, dG2, "ffn2_dwg")
    grads['ffn2_w_up'] = _mm_tn(n3, dU2, "ffn2_dwu")
    grads['ffn2_w_down'] = _mm_tn(H2, d3b, "ffn2_dwd")
    dx2, dx2b, dg_ffn2 = _rms_bwd_res(dn3, x2, ffn2_norm, dx3, 1.0, "ffn2_norm_bwd")

    dy = _mm_fullk(dx2b, full['w_out'], True, None, "mix_out_bwd")
    grads['w_out'] = _mm_tn(ymix, dx2b, "mix_dwout")
    dz, dwa, dwi, vec = _mix_bwd(z, h, dy, *mix_args, "mix_bwd")
    grads['w_in'] = _mm_tn(n2, dz, "mix_dwin")
    dn2 = _mm_fullk(dz, full['w_in'], True, None, "mix_in_bwd")
    dx1, d1b, dg_mix = _rms_bwd_res(dn2, x1, mix_norm, dx2, FFN_RESIDUAL_SCALE, "mix_norm_bwd")

    dG1, dU1, H1, dn1 = _ffn_bwd_act(d1b, G1, U1, full['ffn1_w_gate'], full['ffn1_w_up'], full['ffn1_w_down'], "ffn1_bwd")
    grads['ffn1_w_gate'] = _mm_tn(n1, dG1, "ffn1_dwg")
    grads['ffn1_w_up'] = _mm_tn(n1, dU1, "ffn1_dwu")
    grads['ffn1_w_down'] = _mm_tn(H1, d1b, "ffn1_dwd")
    dx0, _, dg_ffn1 = _rms_bwd_res(dn1, xt, ffn1_norm, dx1, 1.0, "ffn1_norm_bwd")

    parts = []
    for n, ax in zip(BIG, BIG_AXIS):
        g = grads[n]
        R, C = g.shape
        parts.append(g.reshape(2, R // 2, C) if ax == 1 else g.reshape(8, R // 8, C))
    from_sibling = _pair_exchange(parts, "grad_pair_exchange")
    sums = [_pair_add(p, r, c_arr, "grad_pair_add_" + n) for n, p, r in zip(BIG, parts, from_sibling)]
    from_chips = _chip_exchange(sums, "grad_chip_exchange")
    halves = [_quad_add(s, r, q_arr, ax, "grad_chip_add_" + n) for n, ax, s, r in zip(BIG, BIG_AXIS, sums, from_chips)]
    shared = _pair_share(halves, "grad_pair_share")
    out_g, out_d, out_m, out_v = {}, {}, {}, {}
    for n, g in zip(BIG, shared):
        shp = w[n].shape
        g2 = g.reshape(shp[1], shp[2])
        d, mn, vn = _adamw(w[n][0], g2, mom[n][0], var[n][0], "adamw_" + n)
        out_g[n], out_d[n], out_m[n], out_v[n] = (a.reshape(shp) for a in (g2, d, mn, vn))

    small = [n for n in WEIGHTS if n not in BIG]
    local_small = {
        'ffn1_norm': dg_ffn1, 'mix_norm': dg_mix, 'lru_conv_w': vec[V_CW:V_CW + KL], 'lru_conv_b': vec[V_CB],
        'lru_w_a': dwa, 'lru_b_a': vec[V_BA], 'lru_w_i': dwi, 'lru_b_i': vec[V_BI], 'lru_lambda': vec[V_LAM],
        'sc_conv_w': vec[V_SW:V_SW + KS], 'lru_out_norm': vec[V_GLO], 'sc_out_norm': vec[V_GSO],
        'ffn2_norm': dg_ffn2, 'final_norm': dgf,
    }
    full_shapes = [local_small[n].shape for n in small] + [(1,)]
    reduced = _allreduce_small(_pack([local_small[n] for n in small] + [sqerr[0, 0:1]]), "allreduce_small")
    reduced = _unpack(reduced, full_shapes)
    loss = (0.5 / D) * reduced[-1][0]
    gsm = {}
    for n, g in zip(small, reduced[:-1]):
        if n in SMALL_SHARDED:
            g = lax.dynamic_slice(g, (0, qi * DLq), (g.shape[0], DLq))
        gsm[n] = g.reshape(w[n].shape)
    small_shapes = [w[n].shape for n in small]
    d_s, m_s, v_s = _adamw(_pack([w[n] for n in small]), _pack([gsm[n] for n in small]),
                           _pack([mom[n] for n in small]), _pack([var[n] for n in small]), "adamw_small")
    for n, d, mn, vn in zip(small, _unpack(d_s, small_shapes), _unpack(m_s, small_shapes), _unpack(v_s, small_shapes)):
        out_g[n], out_d[n], out_m[n], out_v[n] = gsm[n], d, mn, vn

    return (loss, dx0.reshape(x.shape), *[out_g[n] for n in WEIGHTS], *[out_d[n] for n in WEIGHTS],
            *[out_m[n] for n in WEIGHTS], *[out_v[n] for n in WEIGHTS])
```

```python
import math

import jax
import jax.numpy as jnp
from jax import lax
from jax.experimental import pallas as pl
from jax.experimental.pallas import tpu as pltpu

F32 = jnp.float32
BF16 = jnp.bfloat16
MESH = pl.DeviceIdType.MESH
ANY = pl.BlockSpec(memory_space=pl.ANY)

NORM_EPS = 1e-6
LRU_C = 8.0
FFN_RESIDUAL_SCALE = 0.5
ADAM_LR = 0.001
ADAM_B1 = 0.9
ADAM_B2 = 0.999
ADAM_EPS = 1e-08
ADAM_WD = 0.01
ADAM_STEP = 10

V7X_VMEM_BYTES = 64 * 2**20
VMEM_LIMIT = V7X_VMEM_BYTES - 8 * 2**20
LANES = 128
SUBLANES = 8
PACK_W = LANES
PACK_ALIGN = SUBLANES * PACK_W

WEIGHTS = ['ffn1_norm', 'ffn1_w_gate', 'ffn1_w_up', 'ffn1_w_down', 'mix_norm', 'w_in', 'lru_conv_w', 'lru_conv_b',
           'lru_w_a', 'lru_b_a', 'lru_w_i', 'lru_b_i', 'lru_lambda', 'sc_conv_w', 'lru_out_norm', 'sc_out_norm',
           'w_out', 'ffn2_norm', 'ffn2_w_gate', 'ffn2_w_up', 'ffn2_w_down', 'final_norm']
BIG = ['ffn1_w_gate', 'ffn1_w_up', 'ffn1_w_down', 'w_in', 'w_out', 'ffn2_w_gate', 'ffn2_w_up', 'ffn2_w_down']
BIG_AXIS = [1, 1, 0, 1, 0, 1, 1, 0]
SMALL_SHARDED = ['lru_conv_w', 'sc_conv_w']


def _tile(n, pref, mult):
    if n <= pref:
        return n
    t = (pref // mult) * mult
    while t >= mult:
        if n % t == 0:
            return t
        t -= mult
    return n


def _params(*sem):
    return pltpu.CompilerParams(dimension_semantics=sem, vmem_limit_bytes=VMEM_LIMIT)


def _me():
    return lax.axis_index("x"), lax.axis_index("y"), lax.axis_index("c")


def _sigmoid(v):
    return 1.0 / (1.0 + jnp.exp(-v))


def _rstd(v):
    return lax.rsqrt(jnp.mean(v * v, axis=-1, keepdims=True) + NORM_EPS)


def _rms_bwd(dy, v, gain):
    r = _rstd(v)
    w = gain * dy
    dv = r * w - v * (r * r * r) * jnp.mean(v * w, axis=-1, keepdims=True)
    dgain = jnp.sum(dy * v * r, axis=0, keepdims=True)
    return dv, dgain


def _dot_nt(a, b):
    return lax.dot_general(a, b, (((1,), (1,)), ((), ())), preferred_element_type=F32)


def _dot_tn(a, b):
    return lax.dot_general(a, b, (((0,), (0,)), ((), ())), preferred_element_type=F32)


def _ffn_fwd(x, gain, wg, wu, wd, name):
    T, D = x.shape
    FF = wg.shape[1]
    tm = _tile(T, 512, 16)
    tf = _tile(FF, 512, LANES)
    nf = FF // tf

    def body(x_ref, g_ref, wg_ref, wu_ref, wd_ref, xo_ref, n_ref, G_ref, U_ref, acc_ref):
        f = pl.program_id(1)

        @pl.when(f == 0)
        def _():
            xv = x_ref[...]
            n_ref[...] = (xv * _rstd(xv) * g_ref[...]).astype(BF16)
            acc_ref[...] = jnp.zeros_like(acc_ref)

        n = n_ref[...]
        G = jnp.dot(n, wg_ref[...], preferred_element_type=F32)
        U = jnp.dot(n, wu_ref[...], preferred_element_type=F32)
        G_ref[...] = G.astype(BF16)
        U_ref[...] = U.astype(BF16)
        H = (G * _sigmoid(G) * U).astype(BF16)
        acc_ref[...] += jnp.dot(H, wd_ref[...], preferred_element_type=F32)

        @pl.when(f == nf - 1)
        def _():
            xo_ref[...] = x_ref[...] + FFN_RESIDUAL_SCALE * acc_ref[...]

    return pl.pallas_call(
        body, name=name, grid=(T // tm, nf),
        in_specs=[pl.BlockSpec((tm, D), lambda i, f: (i, 0)),
                  pl.BlockSpec((1, D), lambda i, f: (0, 0)),
                  pl.BlockSpec((D, tf), lambda i, f: (0, f)),
                  pl.BlockSpec((D, tf), lambda i, f: (0, f)),
                  pl.BlockSpec((tf, D), lambda i, f: (f, 0))],
        out_specs=[pl.BlockSpec((tm, D), lambda i, f: (i, 0)),
                   pl.BlockSpec((tm, D), lambda i, f: (i, 0)),
                   pl.BlockSpec((tm, tf), lambda i, f: (i, f)),
                   pl.BlockSpec((tm, tf), lambda i, f: (i, f))],
        out_shape=[jax.ShapeDtypeStruct((T, D), F32), jax.ShapeDtypeStruct((T, D), BF16),
                   jax.ShapeDtypeStruct((T, FF), BF16), jax.ShapeDtypeStruct((T, FF), BF16)],
        scratch_shapes=[pltpu.VMEM((tm, D), F32)],
        compiler_params=_params("parallel", "arbitrary"),
    )(x, gain, wg, wu, wd)


def _ffn_bwd_act(db, G, U, wg, wu, wd, name):
    T, D = db.shape
    FF = wg.shape[1]
    tm = _tile(T, 512, 16)
    tf = _tile(FF, 512, LANES)

    def body(d_ref, G_ref, U_ref, wg_ref, wu_ref, wd_ref, dG_ref, dU_ref, H_ref, dn_ref):
        f = pl.program_id(1)
        dH = _dot_nt(d_ref[...], wd_ref[...])
        Gv = G_ref[...].astype(F32)
        Uv = U_ref[...].astype(F32)
        s = _sigmoid(Gv)
        sg = Gv * s
        H_ref[...] = (sg * Uv).astype(BF16)
        dU = (dH * sg).astype(BF16)
        dG = (dH * Uv * (s * (1.0 + Gv * (1.0 - s)))).astype(BF16)
        dG_ref[...] = dG
        dU_ref[...] = dU
        contrib = _dot_nt(dG, wg_ref[...]) + _dot_nt(dU, wu_ref[...])

        @pl.when(f == 0)
        def _():
            dn_ref[...] = contrib

        @pl.when(f > 0)
        def _():
            dn_ref[...] += contrib

    return pl.pallas_call(
        body, name=name, grid=(T // tm, FF // tf),
        in_specs=[pl.BlockSpec((tm, D), lambda i, f: (i, 0)),
                  pl.BlockSpec((tm, tf), lambda i, f: (i, f)),
                  pl.BlockSpec((tm, tf), lambda i, f: (i, f)),
                  pl.BlockSpec((D, tf), lambda i, f: (0, f)),
                  pl.BlockSpec((D, tf), lambda i, f: (0, f)),
                  pl.BlockSpec((tf, D), lambda i, f: (f, 0))],
        out_specs=[pl.BlockSpec((tm, tf), lambda i, f: (i, f)),
                   pl.BlockSpec((tm, tf), lambda i, f: (i, f)),
                   pl.BlockSpec((tm, tf), lambda i, f: (i, f)),
                   pl.BlockSpec((tm, D), lambda i, f: (i, 0))],
        out_shape=[jax.ShapeDtypeStruct((T, FF), BF16), jax.ShapeDtypeStruct((T, FF), BF16),
                   jax.ShapeDtypeStruct((T, FF), BF16), jax.ShapeDtypeStruct((T, D), F32)],
        compiler_params=_params("parallel", "arbitrary"),
    )(db, G, U, wg, wu, wd)


def _rms_bwd_res(dn, x, gain, dres, scale, name):
    T, D = x.shape
    tm = _tile(T, 256, 16)

    def body(dn_ref, x_ref, g_ref, dr_ref, dx_ref, dxb_ref, dg_ref):
        i = pl.program_id(0)
        dv, dgain = _rms_bwd(dn_ref[...], x_ref[...], g_ref[...])
        dx = dr_ref[...] + dv
        dx_ref[...] = dx
        dxb_ref[...] = (scale * dx).astype(BF16)

        @pl.when(i == 0)
        def _():
            dg_ref[...] = dgain

        @pl.when(i > 0)
        def _():
            dg_ref[...] += dgain

    row = pl.BlockSpec((tm, D), lambda i: (i, 0))
    vec = pl.BlockSpec((1, D), lambda i: (0, 0))
    return pl.pallas_call(
        body, name=name, grid=(T // tm,),
        in_specs=[row, row, vec, row], out_specs=[row, row, vec],
        out_shape=[jax.ShapeDtypeStruct((T, D), F32), jax.ShapeDtypeStruct((T, D), BF16),
                   jax.ShapeDtypeStruct((1, D), F32)],
        compiler_params=_params("arbitrary"),
    )(dn, x, gain, dres)


def _loss_head(x3, gain, target, name):
    T, D = x3.shape
    tm = _tile(T, 256, 16)

    def body(x_ref, g_ref, t_ref, dx_ref, dxb_ref, ls_ref, dg_ref):
        i = pl.program_id(0)
        xv = x_ref[...]
        err = xv * _rstd(xv) * g_ref[...] - t_ref[...]
        sq = jnp.sum(jnp.sum(err * err, axis=1, keepdims=True), axis=0, keepdims=True)
        dv, dgain = _rms_bwd(err * (1.0 / D), xv, g_ref[...])
        dx_ref[...] = dv
        dxb_ref[...] = (FFN_RESIDUAL_SCALE * dv).astype(BF16)
        sqb = jnp.broadcast_to(sq, (1, LANES))

        @pl.when(i == 0)
        def _():
            dg_ref[...] = dgain
            ls_ref[...] = sqb

        @pl.when(i > 0)
        def _():
            dg_ref[...] += dgain
            ls_ref[...] += sqb

    row = pl.BlockSpec((tm, D), lambda i: (i, 0))
    vec = pl.BlockSpec((1, D), lambda i: (0, 0))
    return pl.pallas_call(
        body, name=name, grid=(T // tm,),
        in_specs=[row, vec, row],
        out_specs=[row, row, pl.BlockSpec((1, LANES), lambda i: (0, 0)), vec],
        out_shape=[jax.ShapeDtypeStruct((T, D), F32), jax.ShapeDtypeStruct((T, D), BF16),
                   jax.ShapeDtypeStruct((1, LANES), F32), jax.ShapeDtypeStruct((1, D), F32)],
        compiler_params=_params("arbitrary"),
    )(x3, gain, target)


def _norm_mm(x, gain, w, name):
    T, D = x.shape
    N = w.shape[1]
    tm = _tile(T, 512, 16)
    tn = _tile(N, 1280, LANES)

    def body(x_ref, g_ref, w_ref, n_ref, z_ref):
        @pl.when(pl.program_id(1) == 0)
        def _():
            xv = x_ref[...]
            n_ref[...] = (xv * _rstd(xv) * g_ref[...]).astype(BF16)

        z_ref[...] = jnp.dot(n_ref[...], w_ref[...], preferred_element_type=F32)

    return pl.pallas_call(
        body, name=name, grid=(T // tm, N // tn),
        in_specs=[pl.BlockSpec((tm, D), lambda i, j: (i, 0)),
                  pl.BlockSpec((1, D), lambda i, j: (0, 0)),
                  pl.BlockSpec((D, tn), lambda i, j: (0, j))],
        out_specs=[pl.BlockSpec((tm, D), lambda i, j: (i, 0)),
                   pl.BlockSpec((tm, tn), lambda i, j: (i, j))],
        out_shape=[jax.ShapeDtypeStruct((T, D), BF16), jax.ShapeDtypeStruct((T, N), F32)],
        compiler_params=_params("parallel", "arbitrary"),
    )(x, gain, w)


def _mm_fullk(a, w, trans_w, residual, name):
    T, K = a.shape
    N = w.shape[0] if trans_w else w.shape[1]
    tm = _tile(T, 512, 16)
    tn = _tile(N, 512, LANES)

    def body(*refs):
        if residual is None:
            a_ref, w_ref, o_ref = refs
        else:
            a_ref, w_ref, r_ref, o_ref = refs
        if trans_w:
            acc = _dot_nt(a_ref[...], w_ref[...])
        else:
            acc = jnp.dot(a_ref[...], w_ref[...], preferred_element_type=F32)
        if residual is not None:
            acc = acc + r_ref[...]
        o_ref[...] = acc

    w_spec = pl.BlockSpec((tn, K), lambda i, j: (j, 0)) if trans_w else pl.BlockSpec((K, tn), lambda i, j: (0, j))
    in_specs = [pl.BlockSpec((tm, K), lambda i, j: (i, 0)), w_spec]
    args = [a, w]
    if residual is not None:
        in_specs.append(pl.BlockSpec((tm, tn), lambda i, j: (i, j)))
        args.append(residual)
    return pl.pallas_call(
        body, name=name, grid=(T // tm, N // tn),
        in_specs=in_specs, out_specs=pl.BlockSpec((tm, tn), lambda i, j: (i, j)),
        out_shape=jax.ShapeDtypeStruct((T, N), F32),
        compiler_params=_params("parallel", "arbitrary"),
    )(*args)


def _mm_tn(a, b, name):
    T, M = a.shape
    N = b.shape[1]
    tmw = _tile(M, 2048, LANES)
    tnw = _tile(N, 2048 * 1408 // tmw, LANES)
    tk = _tile(T, 512, 16)
    nk = T // tk

    def body(a_ref, b_ref, o_ref, acc_ref):
        k = pl.program_id(2)

        @pl.when(k == 0)
        def _():
            acc_ref[...] = jnp.zeros_like(acc_ref)

        acc_ref[...] += _dot_tn(a_ref[...], b_ref[...])

        @pl.when(k == nk - 1)
        def _():
            o_ref[...] = acc_ref[...].astype(BF16)

    return pl.pallas_call(
        body, name=name, grid=(M // tmw, N // tnw, nk),
        in_specs=[pl.BlockSpec((tk, tmw), lambda i, j, k: (k, i)),
                  pl.BlockSpec((tk, tnw), lambda i, j, k: (k, j))],
        out_specs=pl.BlockSpec((tmw, tnw), lambda i, j, k: (i, j)),
        out_shape=jax.ShapeDtypeStruct((M, N), BF16),
        scratch_shapes=[pltpu.VMEM((tmw, tnw), F32)],
        compiler_params=_params("parallel", "parallel", "arbitrary"),
    )(a, b)


GELU_K = math.sqrt(2.0 / math.pi)
GELU_C = 0.044715


def _gelu_and_grad(v):
    u = GELU_K * (v + GELU_C * v * v * v)
    th = jnp.tanh(u)
    g = 0.5 * v * (1.0 + th)
    dg = 0.5 * (1.0 + th) + 0.5 * v * (1.0 - th * th) * GELU_K * (1.0 + 3.0 * GELU_C * v * v)
    return g, dg


def _neg_expm1(v):
    poly = v * (1.0 + v * (0.5 + v * (1.0 / 6 + v * (1.0 / 24 + v * (1.0 / 120 + v * (1.0 / 720))))))
    return jnp.where(v > -0.25, -poly, 1.0 - jnp.exp(v))


def _softplus_neg(lam):
    e = jnp.exp(-jnp.abs(lam))
    log1pe = jnp.where(e < 1e-4, e * (1.0 - 0.5 * e), jnp.log(1.0 + e))
    sp = jnp.maximum(-lam, 0.0) + log1pe
    dsp = -1.0 / (1.0 + jnp.exp(lam))
    return sp, dsp


def _earlier(ext, j):
    return pltpu.roll(ext, j, 0)[SUBLANES:, :]


def _later(ext, j):
    n = ext.shape[0]
    return pltpu.roll(ext, n - j, 0)[:n - SUBLANES, :]


def _taps(v, halo, K):
    ext = jnp.concatenate([halo, v], axis=0)
    return [v] + [_earlier(ext, j) for j in range(1, K)]


def _block_diag(vb, w_ref, nh, hd):
    return jnp.concatenate(
        [jnp.dot(vb[:, h * hd:(h + 1) * hd], w_ref[h], preferred_element_type=F32) for h in range(nh)], axis=1)


def _lru_gates(xc, wa_ref, ba_ref, wi_ref, bi_ref, sp, nh, hd):
    xcb = xc.astype(BF16)
    r = _sigmoid(_block_diag(xcb, wa_ref, nh, hd) + ba_ref[...])
    ig = _sigmoid(_block_diag(xcb, wi_ref, nh, hd) + bi_ref[...])
    log_a = -LRU_C * r * sp
    a = jnp.exp(log_a)
    mult = jnp.sqrt(_neg_expm1(2.0 * log_a))
    return xcb, r, ig, a, mult


def _mix_fwd(z, cw, cb, wa, ba, wi, bi, lam, sw, glo, gso, name):
    T = z.shape[0]
    DL = cb.shape[1]
    DS = gso.shape[1]
    NH, HD = wa.shape[0], wa.shape[1]
    KL, KS = cw.shape[0], sw.shape[0]
    tt = _tile(T, 128, 16)
    o_g, o_b, o_c, o_x = DL, 2 * DL, 2 * DL + DS, 2 * DL + 2 * DS

    def body(z_ref, cw_ref, cb_ref, wa_ref, ba_ref, wi_ref, bi_ref, lam_ref, sw_ref, glo_ref, gso_ref,
             h_ref, y_ref, cx_ref, cp_ref, ch_ref):
        @pl.when(pl.program_id(0) == 0)
        def _():
            cx_ref[...] = jnp.zeros_like(cx_ref)
            cp_ref[...] = jnp.zeros_like(cp_ref)
            ch_ref[...] = jnp.zeros_like(ch_ref)

        lx = z_ref[:, 0:DL]
        xs = _taps(lx, cx_ref[...], KL)
        cx_ref[...] = lx[tt - SUBLANES:, :]
        xc = cb_ref[...] + xs[0] * cw_ref[KL - 1:KL, :]
        for j in range(1, KL):
            xc = xc + xs[j] * cw_ref[KL - 1 - j:KL - j, :]
        sp, _ = _softplus_neg(lam_ref[...])
        _, _, ig, a, mult = _lru_gates(xc, wa_ref, ba_ref, wi_ref, bi_ref, sp, NH, HD)
        b = mult * (ig * xc)
        rows = lax.broadcasted_iota(jnp.int32, (tt, DL), 0)
        s = 1
        while s < tt:
            keep = rows >= s
            b = jnp.where(keep, a * pltpu.roll(b, s, 0) + b, b)
            a = jnp.where(keep, a * pltpu.roll(a, s, 0), a)
            s *= 2
        h = a * ch_ref[SUBLANES - 1:SUBLANES, :] + b
        ch_ref[...] = h[tt - SUBLANES:, :]
        h_ref[...] = h
        ge, _ = _gelu_and_grad(z_ref[:, o_g:o_g + DL])
        ylru = h * ge
        y_ref[:, 0:DL] = (ylru * _rstd(ylru) * glo_ref[...]).astype(BF16)

        p = z_ref[:, o_c:o_c + DS] * z_ref[:, o_x:o_x + DS]
        ps = _taps(p, cp_ref[...], KS)
        cp_ref[...] = p[tt - SUBLANES:, :]
        cv = ps[0] * sw_ref[KS - 1:KS, :]
        for j in range(1, KS):
            cv = cv + ps[j] * sw_ref[KS - 1 - j:KS - j, :]
        ysc = z_ref[:, o_b:o_b + DS] * cv
        y_ref[:, DL:DL + DS] = (ysc * _rstd(ysc) * gso_ref[...]).astype(BF16)

    def full(shape):
        return pl.BlockSpec(shape, lambda t: (0,) * len(shape))

    return pl.pallas_call(
        body, name=name, grid=(T // tt,),
        in_specs=[pl.BlockSpec((tt, z.shape[1]), lambda t: (t, 0)),
                  full(cw.shape), full(cb.shape), full(wa.shape), full(ba.shape), full(wi.shape), full(bi.shape),
                  full(lam.shape), full(sw.shape), full(glo.shape), full(gso.shape)],
        out_specs=[pl.BlockSpec((tt, DL), lambda t: (t, 0)), pl.BlockSpec((tt, DL + DS), lambda t: (t, 0))],
        out_shape=[jax.ShapeDtypeStruct((T, DL), F32), jax.ShapeDtypeStruct((T, DL + DS), BF16)],
        scratch_shapes=[pltpu.VMEM((SUBLANES, DL), F32), pltpu.VMEM((SUBLANES, DS), F32),
                        pltpu.VMEM((SUBLANES, DL), F32)],
        compiler_params=_params("arbitrary"),
    )(z, cw, cb, wa, ba, wi, bi, lam, sw, glo, gso)


V_BA, V_BI, V_LAM, V_CB, V_CW, V_SW, V_GLO, V_GSO, V_ROWS = 0, 1, 2, 3, 4, 8, 11, 12, 16


def _mix_bwd(z, h, dy, cw, cb, wa, ba, wi, bi, lam, sw, glo, gso, name):
    T = z.shape[0]
    DL = cb.shape[1]
    DS = gso.shape[1]
    NH, HD = wa.shape[0], wa.shape[1]
    KL, KS = cw.shape[0], sw.shape[0]
    tt = _tile(T, 64, 16)
    nt = T // tt
    hb = tt // SUBLANES
    o_g, o_b, o_c, o_x = DL, 2 * DL, 2 * DL + DS, 2 * DL + 2 * DS

    def body(z_ref, zh_ref, h_ref, hh_ref, dy_ref, cw_ref, cb_ref, wa_ref, ba_ref, wi_ref, bi_ref, lam_ref,
             sw_ref, glo_ref, gso_ref, dz_ref, dwa_ref, dwi_ref, vec_ref, cdx_ref, cdc_ref, cdh_ref):
        i = pl.program_id(0)
        tr = nt - 1 - i

        @pl.when(i == 0)
        def _():
            dwa_ref[...] = jnp.zeros_like(dwa_ref)
            dwi_ref[...] = jnp.zeros_like(dwi_ref)
            vec_ref[...] = jnp.zeros_like(vec_ref)
            cdx_ref[...] = jnp.zeros_like(cdx_ref)
            cdc_ref[...] = jnp.zeros_like(cdc_ref)
            cdh_ref[...] = jnp.zeros_like(cdh_ref)

        def acc_row(r, v):
            vec_ref[pl.ds(r, 1), :] += jnp.sum(v, axis=0, keepdims=True)

        has_prev = tr > 0
        rows = lax.broadcasted_iota(jnp.int32, (tt, DL), 0)

        lx = z_ref[:, 0:DL]
        xs = _taps(lx, jnp.where(has_prev, zh_ref[:, 0:DL], 0.0), KL)
        xc = cb_ref[...] + xs[0] * cw_ref[KL - 1:KL, :]
        for j in range(1, KL):
            xc = xc + xs[j] * cw_ref[KL - 1 - j:KL - j, :]
        sp, dsp = _softplus_neg(lam_ref[...])
        xcb, r, ig, a, mult = _lru_gates(xc, wa_ref, ba_ref, wi_ref, bi_ref, sp, NH, HD)
        hv = h_ref[...]
        hprev = _earlier(jnp.concatenate([jnp.where(has_prev, hh_ref[...], 0.0), hv], axis=0), 1)
        gate = z_ref[:, o_g:o_g + DL]
        ge, dge = _gelu_and_grad(gate)
        ylru = hv * ge

        d_ylru, dglo = _rms_bwd(dy_ref[:, 0:DL], ylru, glo_ref[...])
        vec_ref[pl.ds(V_GLO, 1), :] += dglo
        dz_ref[:, o_g:o_g + DL] = (d_ylru * hv * dge).astype(BF16)
        bq = d_ylru * ge
        aq = jnp.where(rows == tt - 1, 1.0, pltpu.roll(a, tt - 1, 0))
        s = 1
        while s < tt:
            keep = rows < tt - s
            bq = jnp.where(keep, aq * pltpu.roll(bq, tt - s, 0) + bq, bq)
            aq = jnp.where(keep, aq * pltpu.roll(aq, tt - s, 0), aq)
            s *= 2
        dhh = bq + aq * cdh_ref[0:1, :]
        cdh_ref[0:1, :] = a[0:1, :] * dhh[0:1, :]

        da = dhh * hprev
        dmult = dhh * (ig * xc)
        d_i = dhh * mult * xc
        dxc = dhh * mult * ig
        dlog = da * a - dmult * (a * a) / mult
        acc_row(V_LAM, dlog * (-LRU_C * r) * dsp)
        dpa = dlog * (-LRU_C * sp) * r * (1.0 - r)
        dpi = d_i * ig * (1.0 - ig)
        acc_row(V_BA, dpa)
        acc_row(V_BI, dpi)
        dpab = dpa.astype(BF16)
        dpib = dpi.astype(BF16)
        back = []
        for hh in range(NH):
            sl = slice(hh * HD, (hh + 1) * HD)
            dwa_ref[hh] += _dot_tn(xcb[:, sl], dpab[:, sl])
            dwi_ref[hh] += _dot_tn(xcb[:, sl], dpib[:, sl])
            back.append(_dot_nt(dpab[:, sl], wa_ref[hh]) + _dot_nt(dpib[:, sl], wi_ref[hh]))
        dxc = dxc + jnp.concatenate(back, axis=1)

        acc_row(V_CB, dxc)
        extd = jnp.concatenate([dxc, cdx_ref[...]], axis=0)
        cdx_ref[...] = dxc[0:SUBLANES, :]
        dlx = dxc * cw_ref[KL - 1:KL, :]
        acc_row(V_CW + KL - 1, dxc * xs[0])
        for j in range(1, KL):
            dlx = dlx + _later(extd, j) * cw_ref[KL - 1 - j:KL - j, :]
            acc_row(V_CW + KL - 1 - j, dxc * xs[j])
        dz_ref[:, 0:DL] = dlx.astype(BF16)

        sb = z_ref[:, o_b:o_b + DS]
        sc = z_ref[:, o_c:o_c + DS]
        sx = z_ref[:, o_x:o_x + DS]
        p = sc * sx
        ps = _taps(p, jnp.where(has_prev, zh_ref[:, o_c:o_c + DS] * zh_ref[:, o_x:o_x + DS], 0.0), KS)
        cv = ps[0] * sw_ref[KS - 1:KS, :]
        for j in range(1, KS):
            cv = cv + ps[j] * sw_ref[KS - 1 - j:KS - j, :]
        d_ysc, dgso = _rms_bwd(dy_ref[:, DL:DL + DS], sb * cv, gso_ref[...])
        vec_ref[pl.ds(V_GSO, 1), :] += dgso
        dz_ref[:, o_b:o_b + DS] = (d_ysc * cv).astype(BF16)
        dcv = d_ysc * sb
        extc = jnp.concatenate([dcv, cdc_ref[...]], axis=0)
        cdc_ref[...] = dcv[0:SUBLANES, :]
        dp = dcv * sw_ref[KS - 1:KS, :]
        acc_row(V_SW + KS - 1, dcv * ps[0])
        for j in range(1, KS):
            dp = dp + _later(extc, j) * sw_ref[KS - 1 - j:KS - j, :]
            acc_row(V_SW + KS - 1 - j, dcv * ps[j])
        dz_ref[:, o_c:o_c + DS] = (dp * sx).astype(BF16)
        dz_ref[:, o_x:o_x + DS] = (dp * sc).astype(BF16)

    def full(shape):
        return pl.BlockSpec(shape, lambda t: (0,) * len(shape))

    def rev(t):
        return nt - 1 - t

    def halo(t):
        return jnp.maximum(rev(t) * hb - 1, 0)

    return pl.pallas_call(
        body, name=name, grid=(nt,),
        in_specs=[pl.BlockSpec((tt, z.shape[1]), lambda t: (rev(t), 0)),
                  pl.BlockSpec((SUBLANES, z.shape[1]), lambda t: (halo(t), 0)),
                  pl.BlockSpec((tt, DL), lambda t: (rev(t), 0)),
                  pl.BlockSpec((SUBLANES, DL), lambda t: (halo(t), 0)),
                  pl.BlockSpec((tt, DL + DS), lambda t: (rev(t), 0)),
                  full(cw.shape), full(cb.shape), full(wa.shape), full(ba.shape), full(wi.shape), full(bi.shape),
                  full(lam.shape), full(sw.shape), full(glo.shape), full(gso.shape)],
        out_specs=[pl.BlockSpec((tt, z.shape[1]), lambda t: (rev(t), 0)),
                   full(wa.shape), full(wi.shape), full((V_ROWS, DL))],
        out_shape=[jax.ShapeDtypeStruct(z.shape, BF16), jax.ShapeDtypeStruct(wa.shape, F32),
                   jax.ShapeDtypeStruct(wi.shape, F32), jax.ShapeDtypeStruct((V_ROWS, DL), F32)],
        scratch_shapes=[pltpu.VMEM((SUBLANES, DL), F32), pltpu.VMEM((SUBLANES, DS), F32),
                        pltpu.VMEM((SUBLANES, DL), F32)],
        compiler_params=_params("arbitrary"),
    )(z, z, h, h, dy, cw, cb, wa, ba, wi, bi, lam, sw, glo, gso)


def _pair_add(p, r1, c, name):
    G, R, C = r1.shape
    tr = _tile(R, 256, 16)
    tc = _tile(C, 1408, LANES)

    def body(c_ref, p_ref, r_ref, o_ref):
        o_ref[...] = (p_ref[...].astype(F32) + r_ref[...].astype(F32)).astype(BF16)

    blk = (None, tr, tc)
    return pl.pallas_call(
        body, name=name,
        grid_spec=pltpu.PrefetchScalarGridSpec(
            num_scalar_prefetch=1, grid=(G, R // tr, C // tc),
            in_specs=[pl.BlockSpec(blk, lambda g, i, j, cr: (2 * g + cr[0], i, j)),
                      pl.BlockSpec(blk, lambda g, i, j, cr: (g, i, j))],
            out_specs=pl.BlockSpec(blk, lambda g, i, j, cr: (g, i, j))),
        out_shape=jax.ShapeDtypeStruct((G, R, C), BF16),
        compiler_params=_params("parallel", "parallel", "parallel"),
    )(c, p, r1)


def _quad_add(s, r2, qc, axis, name):
    _, R, W = r2.shape
    tr = _tile(R, 256, 16)

    def body(qc_ref, s_ref, r0_ref, r1_ref, r2_ref, o_ref):
        o_ref[...] = ((s_ref[...].astype(F32) + r0_ref[...].astype(F32)) + r1_ref[...].astype(F32)) + r2_ref[...].astype(F32)

    blk = (None, tr, W)
    if axis == 1:
        own = pl.BlockSpec(blk, lambda i, qr: (0, i, qr[0]))
    else:
        own = pl.BlockSpec(blk, lambda i, qr: (qr[0], i, 0))
    return pl.pallas_call(
        body, name=name,
        grid_spec=pltpu.PrefetchScalarGridSpec(
            num_scalar_prefetch=1, grid=(R // tr,),
            in_specs=[own] + [pl.BlockSpec(blk, lambda i, qr, j=j: (j, i, 0)) for j in range(3)],
            out_specs=pl.BlockSpec(blk, lambda i, qr: (qr[1], i, 0))),
        out_shape=jax.ShapeDtypeStruct((2, R, W), F32),
        compiler_params=_params("parallel"),
    )(qc, s, r2, r2, r2)


def _cast_into_full(shard, qc, axis, name):
    R, W = shard.shape
    Rh = R // 2
    tr = _tile(Rh, 256, 16)
    nb = Rh // tr

    def body(qc_ref, s_ref, o_ref):
        o_ref[...] = s_ref[...].astype(BF16)

    if axis == 1:
        out_shape = (2, Rh, 4 * W)
        out_spec = pl.BlockSpec((None, tr, W), lambda hf, i, qr: (hf, i, qr[0]))
    else:
        out_shape = (8, Rh, W)
        out_spec = pl.BlockSpec((None, tr, W), lambda hf, i, qr: (2 * qr[0] + hf, i, 0))
    return pl.pallas_call(
        body, name=name,
        grid_spec=pltpu.PrefetchScalarGridSpec(
            num_scalar_prefetch=1, grid=(2, nb),
            in_specs=[pl.BlockSpec((tr, W), lambda hf, i, qr: (hf * nb + i, 0))],
            out_specs=out_spec),
        out_shape=jax.ShapeDtypeStruct(out_shape, BF16),
        compiler_params=_params("parallel", "parallel"),
    )(qc, shard)


def _adamw(w, g, m, v, name):
    R, C = w.shape
    tr = _tile(R, 256, SUBLANES)
    tc = _tile(C, 2048, LANES)
    c1 = 1.0 - ADAM_B1 ** ADAM_STEP
    c2 = 1.0 - ADAM_B2 ** ADAM_STEP

    def body(w_ref, g_ref, m_ref, v_ref, d_ref, mo_ref, vo_ref):
        gv = g_ref[...]
        mn = ADAM_B1 * m_ref[...] + (1.0 - ADAM_B1) * gv
        vn = ADAM_B2 * v_ref[...] + (1.0 - ADAM_B2) * (gv * gv)
        mo_ref[...] = mn
        vo_ref[...] = vn
        d_ref[...] = -ADAM_LR * ((mn / c1) / (jnp.sqrt(vn / c2) + ADAM_EPS) + ADAM_WD * w_ref[...])

    blk = pl.BlockSpec((tr, tc), lambda i, j: (i, j))
    sh = jax.ShapeDtypeStruct((R, C), F32)
    return pl.pallas_call(
        body, name=name, grid=(R // tr, C // tc),
        in_specs=[blk] * 4, out_specs=[blk] * 3, out_shape=[sh] * 3,
        compiler_params=_params("parallel", "parallel"),
    )(w, g, m, v)


def _other_chips(x, y):
    return [(1 - x, y), (x, 1 - y), (1 - x, 1 - y)]


def _remote(src, dst, send_sems, recv_sems, idx, dev):
    return pltpu.make_async_remote_copy(src_ref=src, dst_ref=dst, send_sem=send_sems.at[idx], recv_sem=recv_sems.at[idx],
                                        device_id=dev, device_id_type=MESH)


def _allgather_big(fulls, axes, name):
    M = len(fulls)

    def body(*refs):
        outs = refs[M:2 * M]
        send_sems, recv_sems = refs[2 * M:]
        x, y, c = _me()
        q = 2 * x + y
        chips = _other_chips(x, y)

        def win(m, qq, cc):
            if axes[m] == 1:
                W = fulls[m].shape[2] // 4
                return outs[m].at[cc, :, pl.ds(pl.multiple_of(qq * W, LANES), W)]
            return outs[m].at[2 * qq + cc]

        sends = []
        for j, (cx, cy) in enumerate(chips):
            for m in range(M):
                sends.append(_remote(win(m, q, c), win(m, q, c), send_sems, recv_sems, 6 * m + j, (cx, cy, c)))
                sends[-1].start()
        for j, (cx, cy) in enumerate(chips):
            for m in range(M):
                landed = win(m, 2 * cx + cy, c)
                _remote(landed, landed, send_sems, recv_sems, 6 * m + j, (cx, cy, c)).wait_recv()
                sends.append(_remote(landed, landed, send_sems, recv_sems, 6 * m + 3 + j, (x, y, 1 - c)))
                sends[-1].start()
        for j, (cx, cy) in enumerate(chips):
            for m in range(M):
                landed = win(m, 2 * cx + cy, 1 - c)
                _remote(landed, landed, send_sems, recv_sems, 6 * m + 3 + j, (x, y, 1 - c)).wait_recv()
        for cp in sends:
            cp.wait_send()

    return pl.pallas_call(
        body, name=name, in_specs=[ANY] * M, out_specs=[ANY] * M,
        out_shape=[jax.ShapeDtypeStruct(f.shape, f.dtype) for f in fulls],
        input_output_aliases={m: m for m in range(M)},
        scratch_shapes=[pltpu.SemaphoreType.DMA((6 * M,)), pltpu.SemaphoreType.DMA((6 * M,))],
    )(*fulls)


def _pair_exchange(parts, name):
    M = len(parts)
    groups = [p.shape[0] // 2 for p in parts]
    base = [sum(groups[:m]) for m in range(M)]
    out_shape = [jax.ShapeDtypeStruct((g,) + p.shape[1:], p.dtype) for g, p in zip(groups, parts)]

    def body(*refs):
        ins, outs = refs[:M], refs[M:2 * M]
        send_sems, recv_sems = refs[2 * M:]
        x, y, c = _me()
        cps = []
        for m in range(M):
            for g in range(groups[m]):
                cps.append(_remote(ins[m].at[2 * g + 1 - c], outs[m].at[g], send_sems, recv_sems, base[m] + g,
                                   (x, y, 1 - c)))
                cps[-1].start()
        for cp in cps:
            cp.wait()

    n = sum(groups)
    return pl.pallas_call(
        body, name=name, in_specs=[ANY] * M, out_specs=[ANY] * M, out_shape=out_shape,
        scratch_shapes=[pltpu.SemaphoreType.DMA((n,)), pltpu.SemaphoreType.DMA((n,))],
    )(*parts)


def _chip_exchange(sums, name):
    M = len(sums)
    out_shape = []
    for s, ax in zip(sums, BIG_AXIS):
        _, Rh, C = s.shape
        out_shape.append(jax.ShapeDtypeStruct((3, Rh, C // 4 if ax == 1 else C), s.dtype))

    def body(*refs):
        ins, outs = refs[:M], refs[M:2 * M]
        send_sems, recv_sems = refs[2 * M:]
        x, y, c = _me()
        cps = []
        for j, (cx, cy) in enumerate(_other_chips(x, y)):
            qj = 2 * cx + cy
            for m in range(M):
                if BIG_AXIS[m] == 1:
                    W = sums[m].shape[2] // 4
                    src = ins[m].at[0, :, pl.ds(pl.multiple_of(qj * W, LANES), W)]
                else:
                    src = ins[m].at[qj]
                cps.append(_remote(src, outs[m].at[j], send_sems, recv_sems, 3 * m + j, (cx, cy, c)))
                cps[-1].start()
        for cp in cps:
            cp.wait()

    return pl.pallas_call(
        body, name=name, in_specs=[ANY] * M, out_specs=[ANY] * M, out_shape=out_shape,
        scratch_shapes=[pltpu.SemaphoreType.DMA((3 * M,)), pltpu.SemaphoreType.DMA((3 * M,))],
    )(*sums)


def _pair_share(bufs, name):
    M = len(bufs)

    def body(*refs):
        outs = refs[M:2 * M]
        send_sems, recv_sems = refs[2 * M:]
        x, y, c = _me()
        cps = []
        for m in range(M):
            cps.append(_remote(outs[m].at[c], outs[m].at[c], send_sems, recv_sems, m, (x, y, 1 - c)))
            cps[-1].start()
        for m in range(M):
            cps[m].wait_send()
            _remote(outs[m].at[1 - c], outs[m].at[1 - c], send_sems, recv_sems, m, (x, y, 1 - c)).wait_recv()

    return pl.pallas_call(
        body, name=name, in_specs=[ANY] * M, out_specs=[ANY] * M,
        out_shape=[jax.ShapeDtypeStruct(b.shape, b.dtype) for b in bufs],
        input_output_aliases={m: m for m in range(M)},
        scratch_shapes=[pltpu.SemaphoreType.DMA((M,)), pltpu.SemaphoreType.DMA((M,))],
    )(*bufs)


def _allreduce_small(v, name):
    R, W = v.shape

    def body(v_ref, o_ref, gath, send_sems, recv_sems):
        x, y, c = _me()
        me = 4 * x + 2 * y + c
        gath[0] = v_ref[...]
        cps = []
        for k in range(1, 8):
            peer = (1 - x if k & 4 else x, 1 - y if k & 2 else y, 1 - c if k & 1 else c)
            cps.append(_remote(v_ref, gath.at[k], send_sems, recv_sems, k - 1, peer))
            cps[-1].start()
        for cp in cps:
            cp.wait()
        acc = gath[me]
        for p in range(1, 8):
            acc = acc + gath[jnp.bitwise_xor(me, p)]
        o_ref[...] = acc

    vm = pl.BlockSpec(memory_space=pltpu.VMEM)
    return pl.pallas_call(
        body, name=name, in_specs=[vm], out_specs=vm, out_shape=jax.ShapeDtypeStruct((R, W), F32),
        scratch_shapes=[pltpu.VMEM((8, R, W), F32), pltpu.SemaphoreType.DMA((7,)), pltpu.SemaphoreType.DMA((7,))],
        compiler_params=pltpu.CompilerParams(vmem_limit_bytes=VMEM_LIMIT),
    )(v)


def _pack(pieces):
    flat = []
    for p in pieces:
        p = p.reshape(-1).astype(F32)
        pad = (-p.shape[0]) % PACK_ALIGN
        flat.append(jnp.pad(p, (0, pad)).reshape(-1, PACK_W))
    return jnp.concatenate(flat, axis=0)


def _unpack(packed, shapes):
    out, row = [], 0
    for shp in shapes:
        n = math.prod(shp)
        rows = -(-n // PACK_ALIGN) * SUBLANES
        out.append(packed[row:row + rows].reshape(-1)[:n].reshape(shp))
        row += rows
    return out


def kernel(x, ffn1_norm, ffn1_w_gate, ffn1_w_up, ffn1_w_down, mix_norm, w_in, lru_conv_w, lru_conv_b, lru_w_a, lru_b_a, lru_w_i, lru_b_i, lru_lambda, sc_conv_w, lru_out_norm, sc_out_norm, w_out, ffn2_norm, ffn2_w_gate, ffn2_w_up, ffn2_w_down, final_norm, loss_target, m_ffn1_norm, m_ffn1_w_gate, m_ffn1_w_up, m_ffn1_w_down, m_mix_norm, m_w_in, m_lru_conv_w, m_lru_conv_b, m_lru_w_a, m_lru_b_a, m_lru_w_i, m_lru_b_i, m_lru_lambda, m_sc_conv_w, m_lru_out_norm, m_sc_out_norm, m_w_out, m_ffn2_norm, m_ffn2_w_gate, m_ffn2_w_up, m_ffn2_w_down, m_final_norm, v_ffn1_norm, v_ffn1_w_gate, v_ffn1_w_up, v_ffn1_w_down, v_mix_norm, v_w_in, v_lru_conv_w, v_lru_conv_b, v_lru_w_a, v_lru_b_a, v_lru_w_i, v_lru_b_i, v_lru_lambda, v_sc_conv_w, v_lru_out_norm, v_sc_out_norm, v_w_out, v_ffn2_norm, v_ffn2_w_gate, v_ffn2_w_up, v_ffn2_w_down, v_final_norm):
    vals = locals()
    w = {n: vals[n] for n in WEIGHTS}
    mom = {n: vals["m_" + n] for n in WEIGHTS}
    var = {n: vals["v_" + n] for n in WEIGHTS}

    xi, yi, ci = _me()
    qi = 2 * xi + yi
    c_arr = jnp.reshape(ci, (1,)).astype(jnp.int32)
    qc_arr = jnp.stack([qi, ci]).astype(jnp.int32)

    T, D = x.shape[1], x.shape[2]
    xt = x.reshape(T, D)
    target = loss_target.reshape(T, D)
    DL = lru_conv_b.shape[-1]
    NH, HD = lru_w_a.shape[1], lru_w_a.shape[2]
    KL, KS = lru_conv_w.shape[1], sc_conv_w.shape[1]
    DLq = lru_conv_w.shape[2]

    placed = [_cast_into_full(w[n][0], qc_arr, ax, "cast_" + n) for n, ax in zip(BIG, BIG_AXIS)]
    gathered = _allgather_big(placed, BIG_AXIS, "allgather_weights")
    full = {}
    for n, ax, g in zip(BIG, BIG_AXIS, gathered):
        full[n] = g.reshape(2 * g.shape[1], g.shape[2]) if ax == 1 else g.reshape(8 * g.shape[1], g.shape[2])

    taps = jnp.zeros((SUBLANES, DL), F32)
    taps = lax.dynamic_update_slice(taps, lru_conv_w[0], (0, qi * DLq))
    taps = lax.dynamic_update_slice(taps, sc_conv_w[0], (KL, qi * DLq))
    taps = _allreduce_small(jnp.where(ci == 0, taps, 0.0), "gather_conv_taps")
    cw, sw = taps[0:KL], taps[KL:KL + KS]

    cb = lru_conv_b
    wa, wi = lru_w_a[0].astype(BF16), lru_w_i[0].astype(BF16)
    ba, bi = lru_b_a.reshape(1, DL), lru_b_i.reshape(1, DL)
    mix_args = (cw, cb, wa, ba, wi, bi, lru_lambda, sw, lru_out_norm, sc_out_norm)
    gf = final_norm.reshape(1, D)

    x1, n1, G1, U1 = _ffn_fwd(xt, ffn1_norm, full['ffn1_w_gate'], full['ffn1_w_up'], full['ffn1_w_down'], "ffn1_fwd")
    n2, z = _norm_mm(x1, mix_norm, full['w_in'], "mix_in_proj")
    h, ymix = _mix_fwd(z, *mix_args, "mix_fwd")
    x2 = _mm_fullk(ymix, full['w_out'], False, x1, "mix_out_proj")
    x3, n3, G2, U2 = _ffn_fwd(x2, ffn2_norm, full['ffn2_w_gate'], full['ffn2_w_up'], full['ffn2_w_down'], "ffn2_fwd")
    dx3, d3b, sqerr, dgf = _loss_head(x3, gf, target, "loss_head")

    grads = {}
    dG2, dU2, H2, dn3 = _ffn_bwd_act(d3b, G2, U2, full['ffn2_w_gate'], full['ffn2_w_up'], full['ffn2_w_down'], "ffn2_bwd")
    grads['ffn2_w_gate'] = _mm_tn(n3, dG2, "ffn2_dwg")
    grads['ffn2_w_up'] = _mm_tn(n3, dU2, "ffn2_dwu")
    grads['ffn2_w_down'] = _mm_tn(H2, d3b, "ffn2_dwd")
    dx2, dx2b, dg_ffn2 = _rms_bwd_res(dn3, x2, ffn2_norm, dx3, 1.0, "ffn2_norm_bwd")

    dy = _mm_fullk(dx2b, full['w_out'], True, None, "mix_out_bwd")
    grads['w_out'] = _mm_tn(ymix, dx2b, "mix_dwout")
    dz, dwa, dwi, vec = _mix_bwd(z, h, dy, *mix_args, "mix_bwd")
    grads['w_in'] = _mm_tn(n2, dz, "mix_dwin")
    dn2 = _mm_fullk(dz, full['w_in'], True, None, "mix_in_bwd")
    dx1, d1b, dg_mix = _rms_bwd_res(dn2, x1, mix_norm, dx2, FFN_RESIDUAL_SCALE, "mix_norm_bwd")

    dG1, dU1, H1, dn1 = _ffn_bwd_act(d1b, G1, U1, full['ffn1_w_gate'], full['ffn1_w_up'], full['ffn1_w_down'], "ffn1_bwd")
    grads['ffn1_w_gate'] = _mm_tn(n1, dG1, "ffn1_dwg")
    grads['ffn1_w_up'] = _mm_tn(n1, dU1, "ffn1_dwu")
    grads['ffn1_w_down'] = _mm_tn(H1, d1b, "ffn1_dwd")
    dx0, _, dg_ffn1 = _rms_bwd_res(dn1, xt, ffn1_norm, dx1, 1.0, "ffn1_norm_bwd")

    parts = []
    for n, ax in zip(BIG, BIG_AXIS):
        g = grads[n]
        R, C = g.shape
        parts.append(g.reshape(2, R // 2, C) if ax == 1 else g.reshape(8, R // 8, C))
    from_sibling = _pair_exchange(parts, "grad_pair_exchange")
    sums = [_pair_add(p, r, c_arr, "grad_pair_add_" + n) for n, p, r in zip(BIG, parts, from_sibling)]
    from_chips = _chip_exchange(sums, "grad_chip_exchange")
    halves = [_quad_add(s, r, qc_arr, ax, "grad_chip_add_" + n) for n, ax, s, r in zip(BIG, BIG_AXIS, sums, from_chips)]
    shared = _pair_share(halves, "grad_pair_share")
    out_g, out_d, out_m, out_v = {}, {}, {}, {}
    for n, g in zip(BIG, shared):
        shp = w[n].shape
        g2 = g.reshape(shp[1], shp[2])
        d, mn, vn = _adamw(w[n][0], g2, mom[n][0], var[n][0], "adamw_" + n)
        out_g[n], out_d[n], out_m[n], out_v[n] = (a.reshape(shp) for a in (g2, d, mn, vn))

    small = [n for n in WEIGHTS if n not in BIG]
    local_small = {
        'ffn1_norm': dg_ffn1, 'mix_norm': dg_mix, 'lru_conv_w': vec[V_CW:V_CW + KL], 'lru_conv_b': vec[V_CB],
        'lru_w_a': dwa, 'lru_b_a': vec[V_BA], 'lru_w_i': dwi, 'lru_b_i': vec[V_BI], 'lru_lambda': vec[V_LAM],
        'sc_conv_w': vec[V_SW:V_SW + KS], 'lru_out_norm': vec[V_GLO], 'sc_out_norm': vec[V_GSO],
        'ffn2_norm': dg_ffn2, 'final_norm': dgf,
    }
    full_shapes = [local_small[n].shape for n in small] + [(1,)]
    reduced = _allreduce_small(_pack([local_small[n] for n in small] + [sqerr[0, 0:1]]), "allreduce_small")
    reduced = _unpack(reduced, full_shapes)
    loss = (0.5 / D) * reduced[-1][0]
    gsm = {}
    for n, g in zip(small, reduced[:-1]):
        if n in SMALL_SHARDED:
            g = lax.dynamic_slice(g, (0, qi * DLq), (g.shape[0], DLq))
        gsm[n] = g.reshape(w[n].shape)
    small_shapes = [w[n].shape for n in small]
    d_s, m_s, v_s = _adamw(_pack([w[n] for n in small]), _pack([gsm[n] for n in small]),
                           _pack([mom[n] for n in small]), _pack([var[n] for n in small]), "adamw_small")
    for n, d, mn, vn in zip(small, _unpack(d_s, small_shapes), _unpack(m_s, small_shapes), _unpack(v_s, small_shapes)):
        out_g[n], out_d[n], out_m[n], out_v[n] = gsm[n], d, mn, vn

    return (loss, dx0.reshape(x.shape), *[out_g[n] for n in WEIGHTS], *[out_d[n] for n in WEIGHTS],
            *[out_m[n] for n in WEIGHTS], *[out_v[n] for n in WEIGHTS])
```

```python
import math

import jax
import jax.numpy as jnp
from jax import lax
from jax.experimental import pallas as pl
from jax.experimental.pallas import tpu as pltpu

F32 = jnp.float32
BF16 = jnp.bfloat16
MESH = pl.DeviceIdType.MESH
ANY = pl.BlockSpec(memory_space=pl.ANY)

NORM_EPS = 1e-6
LRU_C = 8.0
FFN_RESIDUAL_SCALE = 0.5
ADAM_LR = 0.001
ADAM_B1 = 0.9
ADAM_B2 = 0.999
ADAM_EPS = 1e-08
ADAM_WD = 0.01
ADAM_STEP = 10

V7X_VMEM_BYTES = 64 * 2**20
VMEM_LIMIT = V7X_VMEM_BYTES - 8 * 2**20
LANES = 128
SUBLANES = 8
PACK_W = LANES
PACK_ALIGN = SUBLANES * PACK_W

WEIGHTS = ['ffn1_norm', 'ffn1_w_gate', 'ffn1_w_up', 'ffn1_w_down', 'mix_norm', 'w_in', 'lru_conv_w', 'lru_conv_b',
           'lru_w_a', 'lru_b_a', 'lru_w_i', 'lru_b_i', 'lru_lambda', 'sc_conv_w', 'lru_out_norm', 'sc_out_norm',
           'w_out', 'ffn2_norm', 'ffn2_w_gate', 'ffn2_w_up', 'ffn2_w_down', 'final_norm']
BIG = ['ffn1_w_gate', 'ffn1_w_up', 'ffn1_w_down', 'w_in', 'w_out', 'ffn2_w_gate', 'ffn2_w_up', 'ffn2_w_down']
BIG_AXIS = [1, 1, 0, 1, 0, 1, 1, 0]
SMALL_SHARDED = ['lru_conv_w', 'sc_conv_w']


def _tile(n, pref, mult):
    if n <= pref:
        return n
    t = (pref // mult) * mult
    while t >= mult:
        if n % t == 0:
            return t
        t -= mult
    return n


def _params(*sem):
    return pltpu.CompilerParams(dimension_semantics=sem, vmem_limit_bytes=VMEM_LIMIT)


def _me():
    return lax.axis_index("x"), lax.axis_index("y"), lax.axis_index("c")


def _sigmoid(v):
    return 1.0 / (1.0 + jnp.exp(-v))


def _rstd(v):
    return lax.rsqrt(jnp.mean(v * v, axis=-1, keepdims=True) + NORM_EPS)


def _rms_bwd(dy, v, gain):
    r = _rstd(v)
    w = gain * dy
    dv = r * w - v * (r * r * r) * jnp.mean(v * w, axis=-1, keepdims=True)
    dgain = jnp.sum(dy * v * r, axis=0, keepdims=True)
    return dv, dgain


def _dot_nt(a, b):
    return lax.dot_general(a, b, (((1,), (1,)), ((), ())), preferred_element_type=F32)


def _dot_tn(a, b):
    return lax.dot_general(a, b, (((0,), (0,)), ((), ())), preferred_element_type=F32)


class _Carry:
    def __init__(self, inputs, out_shape, aliases, sems, start, finish):
        self.inputs, self.out_shape, self.aliases, self.sems = list(inputs), list(out_shape), dict(aliases), list(sems)
        self.start, self.finish = start, finish


def _call(body, name, grid, in_specs, out_specs, out_shape, scratch_shapes, semantics, args, carry=None):
    if carry is None:
        return pl.pallas_call(body, name=name, grid=grid, in_specs=in_specs, out_specs=out_specs, out_shape=out_shape,
                              scratch_shapes=scratch_shapes, compiler_params=_params(*semantics))(*args)
    ni, no, ns = len(in_specs), len(out_specs), len(scratch_shapes)
    ci, co = len(carry.inputs), len(carry.out_shape)

    def carrying(*refs):
        ins, refs = refs[:ni], refs[ni:]
        cins, refs = refs[:ci], refs[ci:]
        outs, refs = refs[:no], refs[no:]
        couts, refs = refs[:co], refs[co:]
        scratch, csems = refs[:ns], refs[ns:]
        first = pl.program_id(0) == 0
        last = pl.program_id(0) == grid[0] - 1
        for ax in range(1, len(grid)):
            first = jnp.logical_and(first, pl.program_id(ax) == 0)
            last = jnp.logical_and(last, pl.program_id(ax) == grid[ax] - 1)

        @pl.when(first)
        def _():
            carry.start(cins, couts, csems)

        body(*ins, *outs, *scratch)

        @pl.when(last)
        def _():
            carry.finish(cins, couts, csems)

    return pl.pallas_call(
        carrying, name=name, grid=grid, in_specs=list(in_specs) + [ANY] * ci, out_specs=list(out_specs) + [ANY] * co,
        out_shape=list(out_shape) + carry.out_shape,
        input_output_aliases={ni + i: no + j for i, j in carry.aliases.items()},
        scratch_shapes=list(scratch_shapes) + carry.sems,
        compiler_params=_params(*(["arbitrary"] * len(grid))),
    )(*args, *carry.inputs)


def _run_carry(carry, name):
    ci, co = len(carry.inputs), len(carry.out_shape)

    def body(*refs):
        cins, couts, csems = refs[:ci], refs[ci:ci + co], refs[ci + co:]
        carry.start(cins, couts, csems)
        carry.finish(cins, couts, csems)

    return pl.pallas_call(body, name=name, in_specs=[ANY] * ci, out_specs=[ANY] * co, out_shape=carry.out_shape,
                          input_output_aliases=carry.aliases, scratch_shapes=carry.sems)(*carry.inputs)


def _ffn_fwd(x, gain, wg, wu, wd, name, carry=None):
    T, D = x.shape
    FF = wg.shape[1]
    tm = _tile(T, 512, 16)
    tf = _tile(FF, 512, LANES)
    nf = FF // tf

    def body(x_ref, g_ref, wg_ref, wu_ref, wd_ref, xo_ref, n_ref, G_ref, U_ref, acc_ref):
        f = pl.program_id(1)

        @pl.when(f == 0)
        def _():
            xv = x_ref[...]
            n_ref[...] = (xv * _rstd(xv) * g_ref[...]).astype(BF16)
            acc_ref[...] = jnp.zeros_like(acc_ref)

        n = n_ref[...]
        G = jnp.dot(n, wg_ref[...], preferred_element_type=F32)
        U = jnp.dot(n, wu_ref[...], preferred_element_type=F32)
        G_ref[...] = G.astype(BF16)
        U_ref[...] = U.astype(BF16)
        H = (G * _sigmoid(G) * U).astype(BF16)
        acc_ref[...] += jnp.dot(H, wd_ref[...], preferred_element_type=F32)

        @pl.when(f == nf - 1)
        def _():
            xo_ref[...] = x_ref[...] + FFN_RESIDUAL_SCALE * acc_ref[...]

    return _call(
        body, name, (T // tm, nf),
        [pl.BlockSpec((tm, D), lambda i, f: (i, 0)),
         pl.BlockSpec((1, D), lambda i, f: (0, 0)),
         pl.BlockSpec((D, tf), lambda i, f: (0, f)),
         pl.BlockSpec((D, tf), lambda i, f: (0, f)),
         pl.BlockSpec((tf, D), lambda i, f: (f, 0))],
        [pl.BlockSpec((tm, D), lambda i, f: (i, 0)),
         pl.BlockSpec((tm, D), lambda i, f: (i, 0)),
         pl.BlockSpec((tm, tf), lambda i, f: (i, f)),
         pl.BlockSpec((tm, tf), lambda i, f: (i, f))],
        [jax.ShapeDtypeStruct((T, D), F32), jax.ShapeDtypeStruct((T, D), BF16),
         jax.ShapeDtypeStruct((T, FF), BF16), jax.ShapeDtypeStruct((T, FF), BF16)],
        [pltpu.VMEM((tm, D), F32)], ("parallel", "arbitrary"), (x, gain, wg, wu, wd), carry)


def _ffn_bwd_act(db, G, U, wg, wu, wd, name, carry=None):
    T, D = db.shape
    FF = wg.shape[1]
    tm = _tile(T, 512, 16)
    tf = _tile(FF, 512, LANES)

    def body(d_ref, G_ref, U_ref, wg_ref, wu_ref, wd_ref, dG_ref, dU_ref, H_ref, dn_ref):
        f = pl.program_id(1)
        dH = _dot_nt(d_ref[...], wd_ref[...])
        Gv = G_ref[...].astype(F32)
        Uv = U_ref[...].astype(F32)
        s = _sigmoid(Gv)
        sg = Gv * s
        H_ref[...] = (sg * Uv).astype(BF16)
        dU = (dH * sg).astype(BF16)
        dG = (dH * Uv * (s * (1.0 + Gv * (1.0 - s)))).astype(BF16)
        dG_ref[...] = dG
        dU_ref[...] = dU
        contrib = _dot_nt(dG, wg_ref[...]) + _dot_nt(dU, wu_ref[...])

        @pl.when(f == 0)
        def _():
            dn_ref[...] = contrib

        @pl.when(f > 0)
        def _():
            dn_ref[...] += contrib

    return _call(
        body, name, (T // tm, FF // tf),
        [pl.BlockSpec((tm, D), lambda i, f: (i, 0)),
         pl.BlockSpec((tm, tf), lambda i, f: (i, f)),
         pl.BlockSpec((tm, tf), lambda i, f: (i, f)),
         pl.BlockSpec((D, tf), lambda i, f: (0, f)),
         pl.BlockSpec((D, tf), lambda i, f: (0, f)),
         pl.BlockSpec((tf, D), lambda i, f: (f, 0))],
        [pl.BlockSpec((tm, tf), lambda i, f: (i, f)),
         pl.BlockSpec((tm, tf), lambda i, f: (i, f)),
         pl.BlockSpec((tm, tf), lambda i, f: (i, f)),
         pl.BlockSpec((tm, D), lambda i, f: (i, 0))],
        [jax.ShapeDtypeStruct((T, FF), BF16), jax.ShapeDtypeStruct((T, FF), BF16),
         jax.ShapeDtypeStruct((T, FF), BF16), jax.ShapeDtypeStruct((T, D), F32)],
        [], ("parallel", "arbitrary"), (db, G, U, wg, wu, wd), carry)


def _rms_bwd_res(dn, x, gain, dres, scale, name):
    T, D = x.shape
    tm = _tile(T, 256, 16)

    def body(dn_ref, x_ref, g_ref, dr_ref, dx_ref, dxb_ref, dg_ref):
        i = pl.program_id(0)
        dv, dgain = _rms_bwd(dn_ref[...], x_ref[...], g_ref[...])
        dx = dr_ref[...] + dv
        dx_ref[...] = dx
        dxb_ref[...] = (scale * dx).astype(BF16)

        @pl.when(i == 0)
        def _():
            dg_ref[...] = dgain

        @pl.when(i > 0)
        def _():
            dg_ref[...] += dgain

    row = pl.BlockSpec((tm, D), lambda i: (i, 0))
    vec = pl.BlockSpec((1, D), lambda i: (0, 0))
    return pl.pallas_call(
        body, name=name, grid=(T // tm,),
        in_specs=[row, row, vec, row], out_specs=[row, row, vec],
        out_shape=[jax.ShapeDtypeStruct((T, D), F32), jax.ShapeDtypeStruct((T, D), BF16),
                   jax.ShapeDtypeStruct((1, D), F32)],
        compiler_params=_params("arbitrary"),
    )(dn, x, gain, dres)


def _loss_head(x3, gain, target, name):
    T, D = x3.shape
    tm = _tile(T, 256, 16)

    def body(x_ref, g_ref, t_ref, dx_ref, dxb_ref, ls_ref, dg_ref):
        i = pl.program_id(0)
        xv = x_ref[...]
        err = xv * _rstd(xv) * g_ref[...] - t_ref[...]
        sq = jnp.sum(jnp.sum(err * err, axis=1, keepdims=True), axis=0, keepdims=True)
        dv, dgain = _rms_bwd(err * (1.0 / D), xv, g_ref[...])
        dx_ref[...] = dv
        dxb_ref[...] = (FFN_RESIDUAL_SCALE * dv).astype(BF16)
        sqb = jnp.broadcast_to(sq, (1, LANES))

        @pl.when(i == 0)
        def _():
            dg_ref[...] = dgain
            ls_ref[...] = sqb

        @pl.when(i > 0)
        def _():
            dg_ref[...] += dgain
            ls_ref[...] += sqb

    row = pl.BlockSpec((tm, D), lambda i: (i, 0))
    vec = pl.BlockSpec((1, D), lambda i: (0, 0))
    return pl.pallas_call(
        body, name=name, grid=(T // tm,),
        in_specs=[row, vec, row],
        out_specs=[row, row, pl.BlockSpec((1, LANES), lambda i: (0, 0)), vec],
        out_shape=[jax.ShapeDtypeStruct((T, D), F32), jax.ShapeDtypeStruct((T, D), BF16),
                   jax.ShapeDtypeStruct((1, LANES), F32), jax.ShapeDtypeStruct((1, D), F32)],
        compiler_params=_params("arbitrary"),
    )(x3, gain, target)


def _norm_mm(x, gain, w, name):
    T, D = x.shape
    N = w.shape[1]
    tm = _tile(T, 512, 16)
    tn = _tile(N, 1280, LANES)

    def body(x_ref, g_ref, w_ref, n_ref, z_ref):
        @pl.when(pl.program_id(1) == 0)
        def _():
            xv = x_ref[...]
            n_ref[...] = (xv * _rstd(xv) * g_ref[...]).astype(BF16)

        z_ref[...] = jnp.dot(n_ref[...], w_ref[...], preferred_element_type=F32)

    return pl.pallas_call(
        body, name=name, grid=(T // tm, N // tn),
        in_specs=[pl.BlockSpec((tm, D), lambda i, j: (i, 0)),
                  pl.BlockSpec((1, D), lambda i, j: (0, 0)),
                  pl.BlockSpec((D, tn), lambda i, j: (0, j))],
        out_specs=[pl.BlockSpec((tm, D), lambda i, j: (i, 0)),
                   pl.BlockSpec((tm, tn), lambda i, j: (i, j))],
        out_shape=[jax.ShapeDtypeStruct((T, D), BF16), jax.ShapeDtypeStruct((T, N), F32)],
        compiler_params=_params("parallel", "arbitrary"),
    )(x, gain, w)


def _mm_fullk(a, w, trans_w, residual, name):
    T, K = a.shape
    N = w.shape[0] if trans_w else w.shape[1]
    tm = _tile(T, 512, 16)
    tn = _tile(N, 512, LANES)

    def body(*refs):
        if residual is None:
            a_ref, w_ref, o_ref = refs
        else:
            a_ref, w_ref, r_ref, o_ref = refs
        if trans_w:
            acc = _dot_nt(a_ref[...], w_ref[...])
        else:
            acc = jnp.dot(a_ref[...], w_ref[...], preferred_element_type=F32)
        if residual is not None:
            acc = acc + r_ref[...]
        o_ref[...] = acc

    w_spec = pl.BlockSpec((tn, K), lambda i, j: (j, 0)) if trans_w else pl.BlockSpec((K, tn), lambda i, j: (0, j))
    in_specs = [pl.BlockSpec((tm, K), lambda i, j: (i, 0)), w_spec]
    args = [a, w]
    if residual is not None:
        in_specs.append(pl.BlockSpec((tm, tn), lambda i, j: (i, j)))
        args.append(residual)
    return pl.pallas_call(
        body, name=name, grid=(T // tm, N // tn),
        in_specs=in_specs, out_specs=pl.BlockSpec((tm, tn), lambda i, j: (i, j)),
        out_shape=jax.ShapeDtypeStruct((T, N), F32),
        compiler_params=_params("parallel", "arbitrary"),
    )(*args)


def _mm_tn(a, b, name, carry=None):
    T, M = a.shape
    N = b.shape[1]
    tmw = _tile(M, 2048, LANES)
    tnw = _tile(N, 2048 * 1408 // tmw, LANES)
    tk = _tile(T, 512, 16)
    nk = T // tk

    def body(a_ref, b_ref, o_ref, acc_ref):
        k = pl.program_id(2)

        @pl.when(k == 0)
        def _():
            acc_ref[...] = jnp.zeros_like(acc_ref)

        acc_ref[...] += _dot_tn(a_ref[...], b_ref[...])

        @pl.when(k == nk - 1)
        def _():
            o_ref[...] = acc_ref[...].astype(BF16)

    return _call(
        body, name, (M // tmw, N // tnw, nk),
        [pl.BlockSpec((tk, tmw), lambda i, j, k: (k, i)),
         pl.BlockSpec((tk, tnw), lambda i, j, k: (k, j))],
        [pl.BlockSpec((tmw, tnw), lambda i, j, k: (i, j))],
        [jax.ShapeDtypeStruct((M, N), BF16)],
        [pltpu.VMEM((tmw, tnw), F32)], ("parallel", "parallel", "arbitrary"), (a, b), carry)


GELU_K = math.sqrt(2.0 / math.pi)
GELU_C = 0.044715


def _gelu_and_grad(v):
    u = GELU_K * (v + GELU_C * v * v * v)
    th = jnp.tanh(u)
    g = 0.5 * v * (1.0 + th)
    dg = 0.5 * (1.0 + th) + 0.5 * v * (1.0 - th * th) * GELU_K * (1.0 + 3.0 * GELU_C * v * v)
    return g, dg


def _neg_expm1(v):
    poly = v * (1.0 + v * (0.5 + v * (1.0 / 6 + v * (1.0 / 24 + v * (1.0 / 120 + v * (1.0 / 720))))))
    return jnp.where(v > -0.25, -poly, 1.0 - jnp.exp(v))


def _softplus_neg(lam):
    e = jnp.exp(-jnp.abs(lam))
    log1pe = jnp.where(e < 1e-4, e * (1.0 - 0.5 * e), jnp.log(1.0 + e))
    sp = jnp.maximum(-lam, 0.0) + log1pe
    dsp = -1.0 / (1.0 + jnp.exp(lam))
    return sp, dsp


def _earlier(ext, j):
    return pltpu.roll(ext, j, 0)[SUBLANES:, :]


def _later(ext, j):
    n = ext.shape[0]
    return pltpu.roll(ext, n - j, 0)[:n - SUBLANES, :]


def _taps(v, halo, K):
    ext = jnp.concatenate([halo, v], axis=0)
    return [v] + [_earlier(ext, j) for j in range(1, K)]


def _block_diag(vb, w_ref, nh, hd):
    return jnp.concatenate(
        [jnp.dot(vb[:, h * hd:(h + 1) * hd], w_ref[h], preferred_element_type=F32) for h in range(nh)], axis=1)


def _lru_gates(xc, wa_ref, ba_ref, wi_ref, bi_ref, sp, nh, hd):
    xcb = xc.astype(BF16)
    r = _sigmoid(_block_diag(xcb, wa_ref, nh, hd) + ba_ref[...])
    ig = _sigmoid(_block_diag(xcb, wi_ref, nh, hd) + bi_ref[...])
    log_a = -LRU_C * r * sp
    a = jnp.exp(log_a)
    mult = jnp.sqrt(_neg_expm1(2.0 * log_a))
    return xcb, r, ig, a, mult


def _mix_fwd(z, cw, cb, wa, ba, wi, bi, lam, sw, glo, gso, name):
    T = z.shape[0]
    DL = cb.shape[1]
    DS = gso.shape[1]
    NH, HD = wa.shape[0], wa.shape[1]
    KL, KS = cw.shape[0], sw.shape[0]
    tt = _tile(T, 128, 16)
    o_g, o_b, o_c, o_x = DL, 2 * DL, 2 * DL + DS, 2 * DL + 2 * DS

    def body(z_ref, cw_ref, cb_ref, wa_ref, ba_ref, wi_ref, bi_ref, lam_ref, sw_ref, glo_ref, gso_ref,
             h_ref, y_ref, cx_ref, cp_ref, ch_ref):
        @pl.when(pl.program_id(0) == 0)
        def _():
            cx_ref[...] = jnp.zeros_like(cx_ref)
            cp_ref[...] = jnp.zeros_like(cp_ref)
            ch_ref[...] = jnp.zeros_like(ch_ref)

        lx = z_ref[:, 0:DL]
        xs = _taps(lx, cx_ref[...], KL)
        cx_ref[...] = lx[tt - SUBLANES:, :]
        xc = cb_ref[...] + xs[0] * cw_ref[KL - 1:KL, :]
        for j in range(1, KL):
            xc = xc + xs[j] * cw_ref[KL - 1 - j:KL - j, :]
        sp, _ = _softplus_neg(lam_ref[...])
        _, _, ig, a, mult = _lru_gates(xc, wa_ref, ba_ref, wi_ref, bi_ref, sp, NH, HD)
        b = mult * (ig * xc)
        rows = lax.broadcasted_iota(jnp.int32, (tt, DL), 0)
        s = 1
        while s < tt:
            keep = rows >= s
            b = jnp.where(keep, a * pltpu.roll(b, s, 0) + b, b)
            a = jnp.where(keep, a * pltpu.roll(a, s, 0), a)
            s *= 2
        h = a * ch_ref[SUBLANES - 1:SUBLANES, :] + b
        ch_ref[...] = h[tt - SUBLANES:, :]
        h_ref[...] = h
        ge, _ = _gelu_and_grad(z_ref[:, o_g:o_g + DL])
        ylru = h * ge
        y_ref[:, 0:DL] = (ylru * _rstd(ylru) * glo_ref[...]).astype(BF16)

        p = z_ref[:, o_c:o_c + DS] * z_ref[:, o_x:o_x + DS]
        ps = _taps(p, cp_ref[...], KS)
        cp_ref[...] = p[tt - SUBLANES:, :]
        cv = ps[0] * sw_ref[KS - 1:KS, :]
        for j in range(1, KS):
            cv = cv + ps[j] * sw_ref[KS - 1 - j:KS - j, :]
        ysc = z_ref[:, o_b:o_b + DS] * cv
        y_ref[:, DL:DL + DS] = (ysc * _rstd(ysc) * gso_ref[...]).astype(BF16)

    def full(shape):
        return pl.BlockSpec(shape, lambda t: (0,) * len(shape))

    return pl.pallas_call(
        body, name=name, grid=(T // tt,),
        in_specs=[pl.BlockSpec((tt, z.shape[1]), lambda t: (t, 0)),
                  full(cw.shape), full(cb.shape), full(wa.shape), full(ba.shape), full(wi.shape), full(bi.shape),
                  full(lam.shape), full(sw.shape), full(glo.shape), full(gso.shape)],
        out_specs=[pl.BlockSpec((tt, DL), lambda t: (t, 0)), pl.BlockSpec((tt, DL + DS), lambda t: (t, 0))],
        out_shape=[jax.ShapeDtypeStruct((T, DL), F32), jax.ShapeDtypeStruct((T, DL + DS), BF16)],
        scratch_shapes=[pltpu.VMEM((SUBLANES, DL), F32), pltpu.VMEM((SUBLANES, DS), F32),
                        pltpu.VMEM((SUBLANES, DL), F32)],
        compiler_params=_params("arbitrary"),
    )(z, cw, cb, wa, ba, wi, bi, lam, sw, glo, gso)


V_BA, V_BI, V_LAM, V_CB, V_CW, V_SW, V_GLO, V_GSO, V_ROWS = 0, 1, 2, 3, 4, 8, 11, 12, 16


def _mix_bwd(z, h, dy, cw, cb, wa, ba, wi, bi, lam, sw, glo, gso, name):
    T = z.shape[0]
    DL = cb.shape[1]
    DS = gso.shape[1]
    NH, HD = wa.shape[0], wa.shape[1]
    KL, KS = cw.shape[0], sw.shape[0]
    tt = _tile(T, 64, 16)
    nt = T // tt
    hb = tt // SUBLANES
    o_g, o_b, o_c, o_x = DL, 2 * DL, 2 * DL + DS, 2 * DL + 2 * DS

    def body(z_ref, zh_ref, h_ref, hh_ref, dy_ref, cw_ref, cb_ref, wa_ref, ba_ref, wi_ref, bi_ref, lam_ref,
             sw_ref, glo_ref, gso_ref, dz_ref, dwa_ref, dwi_ref, vec_ref, cdx_ref, cdc_ref, cdh_ref):
        i = pl.program_id(0)
        tr = nt - 1 - i

        @pl.when(i == 0)
        def _():
            dwa_ref[...] = jnp.zeros_like(dwa_ref)
            dwi_ref[...] = jnp.zeros_like(dwi_ref)
            vec_ref[...] = jnp.zeros_like(vec_ref)
            cdx_ref[...] = jnp.zeros_like(cdx_ref)
            cdc_ref[...] = jnp.zeros_like(cdc_ref)
            cdh_ref[...] = jnp.zeros_like(cdh_ref)

        def acc_row(r, v):
            vec_ref[pl.ds(r, 1), :] += jnp.sum(v, axis=0, keepdims=True)

        has_prev = tr > 0
        rows = lax.broadcasted_iota(jnp.int32, (tt, DL), 0)

        lx = z_ref[:, 0:DL]
        xs = _taps(lx, jnp.where(has_prev, zh_ref[:, 0:DL], 0.0), KL)
        xc = cb_ref[...] + xs[0] * cw_ref[KL - 1:KL, :]
        for j in range(1, KL):
            xc = xc + xs[j] * cw_ref[KL - 1 - j:KL - j, :]
        sp, dsp = _softplus_neg(lam_ref[...])
        xcb, r, ig, a, mult = _lru_gates(xc, wa_ref, ba_ref, wi_ref, bi_ref, sp, NH, HD)
        hv = h_ref[...]
        hprev = _earlier(jnp.concatenate([jnp.where(has_prev, hh_ref[...], 0.0), hv], axis=0), 1)
        gate = z_ref[:, o_g:o_g + DL]
        ge, dge = _gelu_and_grad(gate)
        ylru = hv * ge

        d_ylru, dglo = _rms_bwd(dy_ref[:, 0:DL], ylru, glo_ref[...])
        vec_ref[pl.ds(V_GLO, 1), :] += dglo
        dz_ref[:, o_g:o_g + DL] = (d_ylru * hv * dge).astype(BF16)
        bq = d_ylru * ge
        aq = jnp.where(rows == tt - 1, 1.0, pltpu.roll(a, tt - 1, 0))
        s = 1
        while s < tt:
            keep = rows < tt - s
            bq = jnp.where(keep, aq * pltpu.roll(bq, tt - s, 0) + bq, bq)
            aq = jnp.where(keep, aq * pltpu.roll(aq, tt - s, 0), aq)
            s *= 2
        dhh = bq + aq * cdh_ref[0:1, :]
        cdh_ref[0:1, :] = a[0:1, :] * dhh[0:1, :]

        da = dhh * hprev
        dmult = dhh * (ig * xc)
        d_i = dhh * mult * xc
        dxc = dhh * mult * ig
        dlog = da * a - dmult * (a * a) / mult
        acc_row(V_LAM, dlog * (-LRU_C * r) * dsp)
        dpa = dlog * (-LRU_C * sp) * r * (1.0 - r)
        dpi = d_i * ig * (1.0 - ig)
        acc_row(V_BA, dpa)
        acc_row(V_BI, dpi)
        dpab = dpa.astype(BF16)
        dpib = dpi.astype(BF16)
        back = []
        for hh in range(NH):
            sl = slice(hh * HD, (hh + 1) * HD)
            dwa_ref[hh] += _dot_tn(xcb[:, sl], dpab[:, sl])
            dwi_ref[hh] += _dot_tn(xcb[:, sl], dpib[:, sl])
            back.append(_dot_nt(dpab[:, sl], wa_ref[hh]) + _dot_nt(dpib[:, sl], wi_ref[hh]))
        dxc = dxc + jnp.concatenate(back, axis=1)

        acc_row(V_CB, dxc)
        extd = jnp.concatenate([dxc, cdx_ref[...]], axis=0)
        cdx_ref[...] = dxc[0:SUBLANES, :]
        dlx = dxc * cw_ref[KL - 1:KL, :]
        acc_row(V_CW + KL - 1, dxc * xs[0])
        for j in range(1, KL):
            dlx = dlx + _later(extd, j) * cw_ref[KL - 1 - j:KL - j, :]
            acc_row(V_CW + KL - 1 - j, dxc * xs[j])
        dz_ref[:, 0:DL] = dlx.astype(BF16)

        sb = z_ref[:, o_b:o_b + DS]
        sc = z_ref[:, o_c:o_c + DS]
        sx = z_ref[:, o_x:o_x + DS]
        p = sc * sx
        ps = _taps(p, jnp.where(has_prev, zh_ref[:, o_c:o_c + DS] * zh_ref[:, o_x:o_x + DS], 0.0), KS)
        cv = ps[0] * sw_ref[KS - 1:KS, :]
        for j in range(1, KS):
            cv = cv + ps[j] * sw_ref[KS - 1 - j:KS - j, :]
        d_ysc, dgso = _rms_bwd(dy_ref[:, DL:DL + DS], sb * cv, gso_ref[...])
        vec_ref[pl.ds(V_GSO, 1), :] += dgso
        dz_ref[:, o_b:o_b + DS] = (d_ysc * cv).astype(BF16)
        dcv = d_ysc * sb
        extc = jnp.concatenate([dcv, cdc_ref[...]], axis=0)
        cdc_ref[...] = dcv[0:SUBLANES, :]
        dp = dcv * sw_ref[KS - 1:KS, :]
        acc_row(V_SW + KS - 1, dcv * ps[0])
        for j in range(1, KS):
            dp = dp + _later(extc, j) * sw_ref[KS - 1 - j:KS - j, :]
            acc_row(V_SW + KS - 1 - j, dcv * ps[j])
        dz_ref[:, o_c:o_c + DS] = (dp * sx).astype(BF16)
        dz_ref[:, o_x:o_x + DS] = (dp * sc).astype(BF16)

    def full(shape):
        return pl.BlockSpec(shape, lambda t: (0,) * len(shape))

    def rev(t):
        return nt - 1 - t

    def halo(t):
        return jnp.maximum(rev(t) * hb - 1, 0)

    return pl.pallas_call(
        body, name=name, grid=(nt,),
        in_specs=[pl.BlockSpec((tt, z.shape[1]), lambda t: (rev(t), 0)),
                  pl.BlockSpec((SUBLANES, z.shape[1]), lambda t: (halo(t), 0)),
                  pl.BlockSpec((tt, DL), lambda t: (rev(t), 0)),
                  pl.BlockSpec((SUBLANES, DL), lambda t: (halo(t), 0)),
                  pl.BlockSpec((tt, DL + DS), lambda t: (rev(t), 0)),
                  full(cw.shape), full(cb.shape), full(wa.shape), full(ba.shape), full(wi.shape), full(bi.shape),
                  full(lam.shape), full(sw.shape), full(glo.shape), full(gso.shape)],
        out_specs=[pl.BlockSpec((tt, z.shape[1]), lambda t: (rev(t), 0)),
                   full(wa.shape), full(wi.shape), full((V_ROWS, DL))],
        out_shape=[jax.ShapeDtypeStruct(z.shape, BF16), jax.ShapeDtypeStruct(wa.shape, F32),
                   jax.ShapeDtypeStruct(wi.shape, F32), jax.ShapeDtypeStruct((V_ROWS, DL), F32)],
        scratch_shapes=[pltpu.VMEM((SUBLANES, DL), F32), pltpu.VMEM((SUBLANES, DS), F32),
                        pltpu.VMEM((SUBLANES, DL), F32)],
        compiler_params=_params("arbitrary"),
    )(z, z, h, h, dy, cw, cb, wa, ba, wi, bi, lam, sw, glo, gso)


def _pair_add(p, r1, c, name):
    G, R, C = r1.shape
    tr = _tile(R, 256, 16)
    tc = _tile(C, 1408, LANES)

    def body(c_ref, p_ref, r_ref, o_ref):
        o_ref[...] = (p_ref[...].astype(F32) + r_ref[...].astype(F32)).astype(BF16)

    blk = (None, tr, tc)
    return pl.pallas_call(
        body, name=name,
        grid_spec=pltpu.PrefetchScalarGridSpec(
            num_scalar_prefetch=1, grid=(G, R // tr, C // tc),
            in_specs=[pl.BlockSpec(blk, lambda g, i, j, cr: (2 * g + cr[0], i, j)),
                      pl.BlockSpec(blk, lambda g, i, j, cr: (g, i, j))],
            out_specs=pl.BlockSpec(blk, lambda g, i, j, cr: (g, i, j))),
        out_shape=jax.ShapeDtypeStruct((G, R, C), BF16),
        compiler_params=_params("parallel", "parallel", "parallel"),
    )(c, p, r1)


def _quad_add(s, r2, qc, axis, name):
    _, R, W = r2.shape
    tr = _tile(R, 256, 16)

    def body(qc_ref, s_ref, r0_ref, r1_ref, r2_ref, o_ref):
        o_ref[...] = ((s_ref[...].astype(F32) + r0_ref[...].astype(F32)) + r1_ref[...].astype(F32)) + r2_ref[...].astype(F32)

    blk = (None, tr, W)
    if axis == 1:
        own = pl.BlockSpec(blk, lambda i, qr: (0, i, qr[0]))
    else:
        own = pl.BlockSpec(blk, lambda i, qr: (qr[0], i, 0))
    return pl.pallas_call(
        body, name=name,
        grid_spec=pltpu.PrefetchScalarGridSpec(
            num_scalar_prefetch=1, grid=(R // tr,),
            in_specs=[own] + [pl.BlockSpec(blk, lambda i, qr, j=j: (j, i, 0)) for j in range(3)],
            out_specs=pl.BlockSpec(blk, lambda i, qr: (qr[1], i, 0))),
        out_shape=jax.ShapeDtypeStruct((2, R, W), F32),
        compiler_params=_params("parallel"),
    )(qc, s, r2, r2, r2)


def _cast_into_full(shard, qc, axis, name):
    R, W = shard.shape
    Rh = R // 2
    tr = _tile(Rh, 256, 16)
    nb = Rh // tr

    def body(qc_ref, s_ref, o_ref):
        o_ref[...] = s_ref[...].astype(BF16)

    if axis == 1:
        out_shape = (2, Rh, 4 * W)
        out_spec = pl.BlockSpec((None, tr, W), lambda hf, i, qr: (hf, i, qr[0]))
    else:
        out_shape = (8, Rh, W)
        out_spec = pl.BlockSpec((None, tr, W), lambda hf, i, qr: (2 * qr[0] + hf, i, 0))
    return pl.pallas_call(
        body, name=name,
        grid_spec=pltpu.PrefetchScalarGridSpec(
            num_scalar_prefetch=1, grid=(2, nb),
            in_specs=[pl.BlockSpec((tr, W), lambda hf, i, qr: (hf * nb + i, 0))],
            out_specs=out_spec),
        out_shape=jax.ShapeDtypeStruct(out_shape, BF16),
        compiler_params=_params("parallel", "parallel"),
    )(qc, shard)


def _adamw(w, g, m, v, name):
    R, C = w.shape
    tr = _tile(R, 256, SUBLANES)
    tc = _tile(C, 2048, LANES)
    c1 = 1.0 - ADAM_B1 ** ADAM_STEP
    c2 = 1.0 - ADAM_B2 ** ADAM_STEP

    def body(w_ref, g_ref, m_ref, v_ref, d_ref, mo_ref, vo_ref):
        gv = g_ref[...]
        mn = ADAM_B1 * m_ref[...] + (1.0 - ADAM_B1) * gv
        vn = ADAM_B2 * v_ref[...] + (1.0 - ADAM_B2) * (gv * gv)
        mo_ref[...] = mn
        vo_ref[...] = vn
        d_ref[...] = -ADAM_LR * ((mn / c1) / (jnp.sqrt(vn / c2) + ADAM_EPS) + ADAM_WD * w_ref[...])

    blk = pl.BlockSpec((tr, tc), lambda i, j: (i, j))
    sh = jax.ShapeDtypeStruct((R, C), F32)
    return pl.pallas_call(
        body, name=name, grid=(R // tr, C // tc),
        in_specs=[blk] * 4, out_specs=[blk] * 3, out_shape=[sh] * 3,
        compiler_params=_params("parallel", "parallel"),
    )(w, g, m, v)


def _other_chips(x, y):
    return [(1 - x, y), (x, 1 - y), (1 - x, 1 - y)]


def _remote(src, dst, send_sems, recv_sems, idx, dev):
    return pltpu.make_async_remote_copy(src_ref=src, dst_ref=dst, send_sem=send_sems.at[idx], recv_sem=recv_sems.at[idx],
                                        device_id=dev, device_id_type=MESH)


def _gather_carry(fulls, axes):
    M = len(fulls)

    def win(outs, m, qq, cc):
        if axes[m] == 1:
            W = fulls[m].shape[2] // 4
            return outs[m].at[cc, :, pl.ds(pl.multiple_of(qq * W, LANES), W)]
        return outs[m].at[2 * qq + cc]

    def ici(outs, sems, m, j, src_q):
        x, y, c = _me()
        cx, cy = _other_chips(x, y)[j]
        blk = win(outs, m, src_q, c)
        return _remote(blk, blk, sems[0], sems[1], 6 * m + j, (cx, cy, c))

    def d2d(outs, sems, m, j, half):
        x, y, c = _me()
        cx, cy = _other_chips(x, y)[j]
        blk = win(outs, m, 2 * cx + cy, half)
        return _remote(blk, blk, sems[0], sems[1], 6 * m + 3 + j, (x, y, 1 - c))

    def start(ins, outs, sems):
        x, y, c = _me()
        for m in range(M):
            for j in range(3):
                ici(outs, sems, m, j, 2 * x + y).start()

    def finish(ins, outs, sems):
        x, y, c = _me()
        chips = _other_chips(x, y)
        for m in range(M):
            for j, (cx, cy) in enumerate(chips):
                ici(outs, sems, m, j, 2 * cx + cy).wait_recv()
                d2d(outs, sems, m, j, c).start()
        for m in range(M):
            for j in range(3):
                d2d(outs, sems, m, j, 1 - c).wait_recv()
        for m in range(M):
            for j in range(3):
                ici(outs, sems, m, j, 2 * x + y).wait_send()
                d2d(outs, sems, m, j, c).wait_send()

    return _Carry(fulls, [jax.ShapeDtypeStruct(f.shape, f.dtype) for f in fulls], {m: m for m in range(M)},
                  [pltpu.SemaphoreType.DMA((6 * M,)), pltpu.SemaphoreType.DMA((6 * M,))], start, finish)


def _pair_exchange_carry(parts):
    M = len(parts)
    groups = [p.shape[0] // 2 for p in parts]
    base = [sum(groups[:m]) for m in range(M)]
    out_shape = [jax.ShapeDtypeStruct((g,) + p.shape[1:], p.dtype) for g, p in zip(groups, parts)]

    def copies(ins, outs, sems):
        x, y, c = _me()
        return [_remote(ins[m].at[2 * g + 1 - c], outs[m].at[g], sems[0], sems[1], base[m] + g, (x, y, 1 - c))
                for m in range(M) for g in range(groups[m])]

    def start(ins, outs, sems):
        for cp in copies(ins, outs, sems):
            cp.start()

    def finish(ins, outs, sems):
        for cp in copies(ins, outs, sems):
            cp.wait()

    n = sum(groups)
    return _Carry(parts, out_shape, {}, [pltpu.SemaphoreType.DMA((n,)), pltpu.SemaphoreType.DMA((n,))], start, finish)


def _chip_exchange_carry(sums, axes):
    M = len(sums)
    out_shape = []
    for s, ax in zip(sums, axes):
        _, Rh, C = s.shape
        out_shape.append(jax.ShapeDtypeStruct((3, Rh, C // 4 if ax == 1 else C), s.dtype))

    def copies(ins, outs, sems):
        x, y, c = _me()
        cps = []
        for m in range(M):
            for j, (cx, cy) in enumerate(_other_chips(x, y)):
                qj = 2 * cx + cy
                if axes[m] == 1:
                    W = sums[m].shape[2] // 4
                    src = ins[m].at[0, :, pl.ds(pl.multiple_of(qj * W, LANES), W)]
                else:
                    src = ins[m].at[qj]
                cps.append(_remote(src, outs[m].at[j], sems[0], sems[1], 3 * m + j, (cx, cy, c)))
        return cps

    def start(ins, outs, sems):
        for cp in copies(ins, outs, sems):
            cp.start()

    def finish(ins, outs, sems):
        for cp in copies(ins, outs, sems):
            cp.wait()

    return _Carry(sums, out_shape, {}, [pltpu.SemaphoreType.DMA((3 * M,)), pltpu.SemaphoreType.DMA((3 * M,))],
                  start, finish)


def _pair_share_carry(bufs):
    M = len(bufs)

    def start(ins, outs, sems):
        x, y, c = _me()
        for m in range(M):
            _remote(outs[m].at[c], outs[m].at[c], sems[0], sems[1], m, (x, y, 1 - c)).start()

    def finish(ins, outs, sems):
        x, y, c = _me()
        for m in range(M):
            _remote(outs[m].at[c], outs[m].at[c], sems[0], sems[1], m, (x, y, 1 - c)).wait_send()
            _remote(outs[m].at[1 - c], outs[m].at[1 - c], sems[0], sems[1], m, (x, y, 1 - c)).wait_recv()

    return _Carry(bufs, [jax.ShapeDtypeStruct(b.shape, b.dtype) for b in bufs], {m: m for m in range(M)},
                  [pltpu.SemaphoreType.DMA((M,)), pltpu.SemaphoreType.DMA((M,))], start, finish)


def _allreduce_small(v, name):
    R, W = v.shape

    def body(v_ref, o_ref, gath, send_sems, recv_sems):
        x, y, c = _me()
        me = 4 * x + 2 * y + c
        gath[0] = v_ref[...]
        cps = []
        for k in range(1, 8):
            peer = (1 - x if k & 4 else x, 1 - y if k & 2 else y, 1 - c if k & 1 else c)
            cps.append(_remote(v_ref, gath.at[k], send_sems, recv_sems, k - 1, peer))
            cps[-1].start()
        for cp in cps:
            cp.wait()
        acc = gath[me]
        for p in range(1, 8):
            acc = acc + gath[jnp.bitwise_xor(me, p)]
        o_ref[...] = acc

    vm = pl.BlockSpec(memory_space=pltpu.VMEM)
    return pl.pallas_call(
        body, name=name, in_specs=[vm], out_specs=vm, out_shape=jax.ShapeDtypeStruct((R, W), F32),
        scratch_shapes=[pltpu.VMEM((8, R, W), F32), pltpu.SemaphoreType.DMA((7,)), pltpu.SemaphoreType.DMA((7,))],
        compiler_params=pltpu.CompilerParams(vmem_limit_bytes=VMEM_LIMIT),
    )(v)


def _pack(pieces):
    flat = []
    for p in pieces:
        p = p.reshape(-1).astype(F32)
        pad = (-p.shape[0]) % PACK_ALIGN
        flat.append(jnp.pad(p, (0, pad)).reshape(-1, PACK_W))
    return jnp.concatenate(flat, axis=0)


def _unpack(packed, shapes):
    out, row = [], 0
    for shp in shapes:
        n = math.prod(shp)
        rows = -(-n // PACK_ALIGN) * SUBLANES
        out.append(packed[row:row + rows].reshape(-1)[:n].reshape(shp))
        row += rows
    return out


def kernel(x, ffn1_norm, ffn1_w_gate, ffn1_w_up, ffn1_w_down, mix_norm, w_in, lru_conv_w, lru_conv_b, lru_w_a, lru_b_a, lru_w_i, lru_b_i, lru_lambda, sc_conv_w, lru_out_norm, sc_out_norm, w_out, ffn2_norm, ffn2_w_gate, ffn2_w_up, ffn2_w_down, final_norm, loss_target, m_ffn1_norm, m_ffn1_w_gate, m_ffn1_w_up, m_ffn1_w_down, m_mix_norm, m_w_in, m_lru_conv_w, m_lru_conv_b, m_lru_w_a, m_lru_b_a, m_lru_w_i, m_lru_b_i, m_lru_lambda, m_sc_conv_w, m_lru_out_norm, m_sc_out_norm, m_w_out, m_ffn2_norm, m_ffn2_w_gate, m_ffn2_w_up, m_ffn2_w_down, m_final_norm, v_ffn1_norm, v_ffn1_w_gate, v_ffn1_w_up, v_ffn1_w_down, v_mix_norm, v_w_in, v_lru_conv_w, v_lru_conv_b, v_lru_w_a, v_lru_b_a, v_lru_w_i, v_lru_b_i, v_lru_lambda, v_sc_conv_w, v_lru_out_norm, v_sc_out_norm, v_w_out, v_ffn2_norm, v_ffn2_w_gate, v_ffn2_w_up, v_ffn2_w_down, v_final_norm):
    vals = locals()
    w = {n: vals[n] for n in WEIGHTS}
    mom = {n: vals["m_" + n] for n in WEIGHTS}
    var = {n: vals["v_" + n] for n in WEIGHTS}

    xi, yi, ci = _me()
    qi = 2 * xi + yi
    c_arr = jnp.reshape(ci, (1,)).astype(jnp.int32)
    qc_arr = jnp.stack([qi, ci]).astype(jnp.int32)

    T, D = x.shape[1], x.shape[2]
    xt = x.reshape(T, D)
    target = loss_target.reshape(T, D)
    DL = lru_conv_b.shape[-1]
    NH, HD = lru_w_a.shape[1], lru_w_a.shape[2]
    KL, KS = lru_conv_w.shape[1], sc_conv_w.shape[1]
    DLq = lru_conv_w.shape[2]

    axis_of = dict(zip(BIG, BIG_AXIS))
    first_names = ['ffn1_w_gate', 'ffn1_w_up', 'ffn1_w_down']
    later_names = ['w_in', 'w_out', 'ffn2_w_gate', 'ffn2_w_up', 'ffn2_w_down']

    def unview(n, g):
        return g.reshape(2 * g.shape[1], g.shape[2]) if axis_of[n] == 1 else g.reshape(8 * g.shape[1], g.shape[2])

    placed = {n: _cast_into_full(w[n][0], qc_arr, axis_of[n], "cast_" + n) for n in BIG}
    gathered = _run_carry(_gather_carry([placed[n] for n in first_names], [axis_of[n] for n in first_names]),
                          "gather_ffn1_weights")
    full = {n: unview(n, g) for n, g in zip(first_names, gathered)}
    gather_later = _gather_carry([placed[n] for n in later_names], [axis_of[n] for n in later_names])

    taps = jnp.zeros((SUBLANES, DL), F32)
    taps = lax.dynamic_update_slice(taps, lru_conv_w[0], (0, qi * DLq))
    taps = lax.dynamic_update_slice(taps, sc_conv_w[0], (KL, qi * DLq))
    taps = _allreduce_small(jnp.where(ci == 0, taps, 0.0), "gather_conv_taps")
    cw, sw = taps[0:KL], taps[KL:KL + KS]

    cb = lru_conv_b
    wa, wi = lru_w_a[0].astype(BF16), lru_w_i[0].astype(BF16)
    ba, bi = lru_b_a.reshape(1, DL), lru_b_i.reshape(1, DL)
    mix_args = (cw, cb, wa, ba, wi, bi, lru_lambda, sw, lru_out_norm, sc_out_norm)
    gf = final_norm.reshape(1, D)

    res = _ffn_fwd(xt, ffn1_norm, full['ffn1_w_gate'], full['ffn1_w_up'], full['ffn1_w_down'], "ffn1_fwd", gather_later)
    x1, n1, G1, U1 = res[:4]
    full.update({n: unview(n, g) for n, g in zip(later_names, res[4:])})
    n2, z = _norm_mm(x1, mix_norm, full['w_in'], "mix_in_proj")
    h, ymix = _mix_fwd(z, *mix_args, "mix_fwd")
    x2 = _mm_fullk(ymix, full['w_out'], False, x1, "mix_out_proj")
    x3, n3, G2, U2 = _ffn_fwd(x2, ffn2_norm, full['ffn2_w_gate'], full['ffn2_w_up'], full['ffn2_w_down'], "ffn2_fwd")
    dx3, d3b, sqerr, dgf = _loss_head(x3, gf, target, "loss_head")

    grads, halves = {}, {}

    def pair_sum(names, tag):
        parts = []
        for n in names:
            R, C = grads[n].shape
            parts.append(grads[n].reshape(2, R // 2, C) if axis_of[n] == 1 else grads[n].reshape(8, R // 8, C))
        recv = _run_carry(_pair_exchange_carry(parts), "grad_pair_exchange_" + tag)
        return [_pair_add(p, r, c_arr, "grad_pair_add_" + n) for n, p, r in zip(names, parts, recv)]

    def chip_sum(names, sums, recv):
        for n, s, r in zip(names, sums, recv):
            halves[n] = _quad_add(s, r, qc_arr, axis_of[n], "grad_chip_add_" + n)

    dG2, dU2, H2, dn3 = _ffn_bwd_act(d3b, G2, U2, full['ffn2_w_gate'], full['ffn2_w_up'], full['ffn2_w_down'], "ffn2_bwd")
    grads['ffn2_w_gate'] = _mm_tn(n3, dG2, "ffn2_dwg")[0]
    grads['ffn2_w_up'] = _mm_tn(n3, dU2, "ffn2_dwu")[0]
    grads['ffn2_w_down'] = _mm_tn(H2, d3b, "ffn2_dwd")[0]
    names_a = ['ffn2_w_gate', 'ffn2_w_up', 'ffn2_w_down', 'w_out', 'w_in']
    sums_a = pair_sum(names_a[:3], "ffn2")
    dx2, dx2b, dg_ffn2 = _rms_bwd_res(dn3, x2, ffn2_norm, dx3, 1.0, "ffn2_norm_bwd")

    dy = _mm_fullk(dx2b, full['w_out'], True, None, "mix_out_bwd")
    grads['w_out'] = _mm_tn(ymix, dx2b, "mix_dwout")[0]
    dz, dwa, dwi, vec = _mix_bwd(z, h, dy, *mix_args, "mix_bwd")
    grads['w_in'] = _mm_tn(n2, dz, "mix_dwin")[0]
    sums_a += pair_sum(names_a[3:], "mix")
    dn2 = _mm_fullk(dz, full['w_in'], True, None, "mix_in_bwd")
    dx1, d1b, dg_mix = _rms_bwd_res(dn2, x1, mix_norm, dx2, FFN_RESIDUAL_SCALE, "mix_norm_bwd")

    res = _ffn_bwd_act(d1b, G1, U1, full['ffn1_w_gate'], full['ffn1_w_up'], full['ffn1_w_down'], "ffn1_bwd",
                       _chip_exchange_carry(sums_a, [axis_of[n] for n in names_a]))
    dG1, dU1, H1, dn1 = res[:4]
    chip_sum(names_a, sums_a, res[4:])
    dx0, _, dg_ffn1 = _rms_bwd_res(dn1, xt, ffn1_norm, dx1, 1.0, "ffn1_norm_bwd")
    grads['ffn1_w_gate'] = _mm_tn(n1, dG1, "ffn1_dwg")[0]
    grads['ffn1_w_up'] = _mm_tn(n1, dU1, "ffn1_dwu")[0]
    names_b = ['ffn1_w_gate', 'ffn1_w_up']
    sums_b = pair_sum(names_b, "ffn1_in")
    res = _mm_tn(H1, d1b, "ffn1_dwd", _chip_exchange_carry(sums_b, [axis_of[n] for n in names_b]))
    grads['ffn1_w_down'] = res[0]
    chip_sum(names_b, sums_b, res[1:])
    names_c = ['ffn1_w_down']
    sums_c = pair_sum(names_c, "ffn1_out")
    chip_sum(names_c, sums_c, _run_carry(_chip_exchange_carry(sums_c, [axis_of[n] for n in names_c]),
                                         "grad_chip_exchange_ffn1_out"))
    shared = _run_carry(_pair_share_carry([halves[n] for n in BIG]), "grad_pair_share")
    out_g, out_d, out_m, out_v = {}, {}, {}, {}
    for n, g in zip(BIG, shared):
        shp = w[n].shape
        g2 = g.reshape(shp[1], shp[2])
        d, mn, vn = _adamw(w[n][0], g2, mom[n][0], var[n][0], "adamw_" + n)
        out_g[n], out_d[n], out_m[n], out_v[n] = (a.reshape(shp) for a in (g2, d, mn, vn))

    small = [n for n in WEIGHTS if n not in BIG]
    local_small = {
        'ffn1_norm': dg_ffn1, 'mix_norm': dg_mix, 'lru_conv_w': vec[V_CW:V_CW + KL], 'lru_conv_b': vec[V_CB],
        'lru_w_a': dwa, 'lru_b_a': vec[V_BA], 'lru_w_i': dwi, 'lru_b_i': vec[V_BI], 'lru_lambda': vec[V_LAM],
        'sc_conv_w': vec[V_SW:V_SW + KS], 'lru_out_norm': vec[V_GLO], 'sc_out_norm': vec[V_GSO],
        'ffn2_norm': dg_ffn2, 'final_norm': dgf,
    }
    full_shapes = [local_small[n].shape for n in small] + [(1,)]
    reduced = _allreduce_small(_pack([local_small[n] for n in small] + [sqerr[0, 0:1]]), "allreduce_small")
    reduced = _unpack(reduced, full_shapes)
    loss = (0.5 / D) * reduced[-1][0]
    gsm = {}
    for n, g in zip(small, reduced[:-1]):
        if n in SMALL_SHARDED:
            g = lax.dynamic_slice(g, (0, qi * DLq), (g.shape[0], DLq))
        gsm[n] = g.reshape(w[n].shape)
    small_shapes = [w[n].shape for n in small]
    d_s, m_s, v_s = _adamw(_pack([w[n] for n in small]), _pack([gsm[n] for n in small]),
                           _pack([mom[n] for n in small]), _pack([var[n] for n in small]), "adamw_small")
    for n, d, mn, vn in zip(small, _unpack(d_s, small_shapes), _unpack(m_s, small_shapes), _unpack(v_s, small_shapes)):
        out_g[n], out_d[n], out_m[n], out_v[n] = gsm[n], d, mn, vn

    return (loss, dx0.reshape(x.shape), *[out_g[n] for n in WEIGHTS], *[out_d[n] for n in WEIGHTS],
            *[out_m[n] for n in WEIGHTS], *[out_v[n] for n in WEIGHTS])
```

```python
import math

import jax
import jax.numpy as jnp
from jax import lax
from jax.experimental import pallas as pl
from jax.experimental.pallas import tpu as pltpu

F32 = jnp.float32
BF16 = jnp.bfloat16
MESH = pl.DeviceIdType.MESH
ANY = pl.BlockSpec(memory_space=pl.ANY)

NORM_EPS = 1e-6
LRU_C = 8.0
FFN_RESIDUAL_SCALE = 0.5
ADAM_LR = 0.001
ADAM_B1 = 0.9
ADAM_B2 = 0.999
ADAM_EPS = 1e-08
ADAM_WD = 0.01
ADAM_STEP = 10

V7X_VMEM_BYTES = 64 * 2**20
VMEM_LIMIT = V7X_VMEM_BYTES - 8 * 2**20
LANES = 128
SUBLANES = 8
PACK_W = LANES
PACK_ALIGN = SUBLANES * PACK_W

WEIGHTS = ['ffn1_norm', 'ffn1_w_gate', 'ffn1_w_up', 'ffn1_w_down', 'mix_norm', 'w_in', 'lru_conv_w', 'lru_conv_b',
           'lru_w_a', 'lru_b_a', 'lru_w_i', 'lru_b_i', 'lru_lambda', 'sc_conv_w', 'lru_out_norm', 'sc_out_norm',
           'w_out', 'ffn2_norm', 'ffn2_w_gate', 'ffn2_w_up', 'ffn2_w_down', 'final_norm']
BIG = ['ffn1_w_gate', 'ffn1_w_up', 'ffn1_w_down', 'w_in', 'w_out', 'ffn2_w_gate', 'ffn2_w_up', 'ffn2_w_down']
BIG_AXIS = [1, 1, 0, 1, 0, 1, 1, 0]
SMALL_SHARDED = ['lru_conv_w', 'sc_conv_w']


def _tile(n, pref, mult):
    if n <= pref:
        return n
    t = (pref // mult) * mult
    while t >= mult:
        if n % t == 0:
            return t
        t -= mult
    return n


def _params(*sem):
    return pltpu.CompilerParams(dimension_semantics=sem, vmem_limit_bytes=VMEM_LIMIT)


def _me():
    return lax.axis_index("x"), lax.axis_index("y"), lax.axis_index("c")


def _sigmoid(v):
    return 1.0 / (1.0 + jnp.exp(-v))


def _rstd(v):
    return lax.rsqrt(jnp.mean(v * v, axis=-1, keepdims=True) + NORM_EPS)


def _rms_bwd(dy, v, gain):
    r = _rstd(v)
    w = gain * dy
    dv = r * w - v * (r * r * r) * jnp.mean(v * w, axis=-1, keepdims=True)
    dgain = jnp.sum(dy * v * r, axis=0, keepdims=True)
    return dv, dgain


def _dot_nt(a, b):
    return lax.dot_general(a, b, (((1,), (1,)), ((), ())), preferred_element_type=F32)


def _dot_tn(a, b):
    return lax.dot_general(a, b, (((0,), (0,)), ((), ())), preferred_element_type=F32)


class _Carry:
    def __init__(self, inputs, out_shape, aliases, sems, start, finish):
        self.inputs, self.out_shape, self.aliases, self.sems = list(inputs), list(out_shape), dict(aliases), list(sems)
        self.start, self.finish = start, finish


def _call(body, name, grid, in_specs, out_specs, out_shape, scratch_shapes, semantics, args, carry=None):
    if carry is None:
        return pl.pallas_call(body, name=name, grid=grid, in_specs=in_specs, out_specs=out_specs, out_shape=out_shape,
                              scratch_shapes=scratch_shapes, compiler_params=_params(*semantics))(*args)
    ni, no, ns = len(in_specs), len(out_specs), len(scratch_shapes)
    ci, co = len(carry.inputs), len(carry.out_shape)

    def carrying(*refs):
        ins, refs = refs[:ni], refs[ni:]
        cins, refs = refs[:ci], refs[ci:]
        outs, refs = refs[:no], refs[no:]
        couts, refs = refs[:co], refs[co:]
        scratch, csems = refs[:ns], refs[ns:]
        first = pl.program_id(0) == 0
        last = pl.program_id(0) == grid[0] - 1
        for ax in range(1, len(grid)):
            first = jnp.logical_and(first, pl.program_id(ax) == 0)
            last = jnp.logical_and(last, pl.program_id(ax) == grid[ax] - 1)

        @pl.when(first)
        def _():
            carry.start(cins, couts, csems)

        body(*ins, *outs, *scratch)

        @pl.when(last)
        def _():
            carry.finish(cins, couts, csems)

    return pl.pallas_call(
        carrying, name=name, grid=grid, in_specs=list(in_specs) + [ANY] * ci, out_specs=list(out_specs) + [ANY] * co,
        out_shape=list(out_shape) + carry.out_shape,
        input_output_aliases={ni + i: no + j for i, j in carry.aliases.items()},
        scratch_shapes=list(scratch_shapes) + carry.sems,
        compiler_params=_params(*(["arbitrary"] * len(grid))),
    )(*args, *carry.inputs)


def _run_carry(carry, name):
    ci, co = len(carry.inputs), len(carry.out_shape)

    def body(*refs):
        cins, couts, csems = refs[:ci], refs[ci:ci + co], refs[ci + co:]
        carry.start(cins, couts, csems)
        carry.finish(cins, couts, csems)

    return pl.pallas_call(body, name=name, in_specs=[ANY] * ci, out_specs=[ANY] * co, out_shape=carry.out_shape,
                          input_output_aliases=carry.aliases, scratch_shapes=carry.sems)(*carry.inputs)


def _ffn_fwd(x, gain, wg, wu, wd, name, carry=None):
    T, D = x.shape
    FF = wg.shape[1]
    tm = _tile(T, 512, 16)
    tf = _tile(FF, 512, LANES)
    nf = FF // tf

    def body(x_ref, g_ref, wg_ref, wu_ref, wd_ref, xo_ref, n_ref, G_ref, U_ref, acc_ref):
        f = pl.program_id(1)

        @pl.when(f == 0)
        def _():
            xv = x_ref[...]
            n_ref[...] = (xv * _rstd(xv) * g_ref[...]).astype(BF16)
            acc_ref[...] = jnp.zeros_like(acc_ref)

        n = n_ref[...]
        G = jnp.dot(n, wg_ref[...], preferred_element_type=F32)
        U = jnp.dot(n, wu_ref[...], preferred_element_type=F32)
        G_ref[...] = G.astype(BF16)
        U_ref[...] = U.astype(BF16)
        H = (G * _sigmoid(G) * U).astype(BF16)
        acc_ref[...] += jnp.dot(H, wd_ref[...], preferred_element_type=F32)

        @pl.when(f == nf - 1)
        def _():
            xo_ref[...] = x_ref[...] + FFN_RESIDUAL_SCALE * acc_ref[...]

    return _call(
        body, name, (T // tm, nf),
        [pl.BlockSpec((tm, D), lambda i, f: (i, 0)),
         pl.BlockSpec((1, D), lambda i, f: (0, 0)),
         pl.BlockSpec((D, tf), lambda i, f: (0, f)),
         pl.BlockSpec((D, tf), lambda i, f: (0, f)),
         pl.BlockSpec((tf, D), lambda i, f: (f, 0))],
        [pl.BlockSpec((tm, D), lambda i, f: (i, 0)),
         pl.BlockSpec((tm, D), lambda i, f: (i, 0)),
         pl.BlockSpec((tm, tf), lambda i, f: (i, f)),
         pl.BlockSpec((tm, tf), lambda i, f: (i, f))],
        [jax.ShapeDtypeStruct((T, D), F32), jax.ShapeDtypeStruct((T, D), BF16),
         jax.ShapeDtypeStruct((T, FF), BF16), jax.ShapeDtypeStruct((T, FF), BF16)],
        [pltpu.VMEM((tm, D), F32)], ("parallel", "arbitrary"), (x, gain, wg, wu, wd), carry)


def _ffn_bwd_act(db, G, U, wg, wu, wd, name, carry=None):
    T, D = db.shape
    FF = wg.shape[1]
    tm = _tile(T, 512, 16)
    tf = _tile(FF, 512, LANES)

    def body(d_ref, G_ref, U_ref, wg_ref, wu_ref, wd_ref, dG_ref, dU_ref, H_ref, dn_ref):
        f = pl.program_id(1)
        dH = _dot_nt(d_ref[...], wd_ref[...])
        Gv = G_ref[...].astype(F32)
        Uv = U_ref[...].astype(F32)
        s = _sigmoid(Gv)
        sg = Gv * s
        H_ref[...] = (sg * Uv).astype(BF16)
        dU = (dH * sg).astype(BF16)
        dG = (dH * Uv * (s * (1.0 + Gv * (1.0 - s)))).astype(BF16)
        dG_ref[...] = dG
        dU_ref[...] = dU
        contrib = _dot_nt(dG, wg_ref[...]) + _dot_nt(dU, wu_ref[...])

        @pl.when(f == 0)
        def _():
            dn_ref[...] = contrib

        @pl.when(f > 0)
        def _():
            dn_ref[...] += contrib

    return _call(
        body, name, (T // tm, FF // tf),
        [pl.BlockSpec((tm, D), lambda i, f: (i, 0)),
         pl.BlockSpec((tm, tf), lambda i, f: (i, f)),
         pl.BlockSpec((tm, tf), lambda i, f: (i, f)),
         pl.BlockSpec((D, tf), lambda i, f: (0, f)),
         pl.BlockSpec((D, tf), lambda i, f: (0, f)),
         pl.BlockSpec((tf, D), lambda i, f: (f, 0))],
        [pl.BlockSpec((tm, tf), lambda i, f: (i, f)),
         pl.BlockSpec((tm, tf), lambda i, f: (i, f)),
         pl.BlockSpec((tm, tf), lambda i, f: (i, f)),
         pl.BlockSpec((tm, D), lambda i, f: (i, 0))],
        [jax.ShapeDtypeStruct((T, FF), BF16), jax.ShapeDtypeStruct((T, FF), BF16),
         jax.ShapeDtypeStruct((T, FF), BF16), jax.ShapeDtypeStruct((T, D), F32)],
        [], ("parallel", "arbitrary"), (db, G, U, wg, wu, wd), carry)


def _rms_bwd_res(dn, x, gain, dres, scale, name, carry=None):
    T, D = x.shape
    tm = _tile(T, 256, 16)

    def body(dn_ref, x_ref, g_ref, dr_ref, dx_ref, dxb_ref, dg_ref):
        i = pl.program_id(0)
        dv, dgain = _rms_bwd(dn_ref[...], x_ref[...], g_ref[...])
        dx = dr_ref[...] + dv
        dx_ref[...] = dx
        dxb_ref[...] = (scale * dx).astype(BF16)

        @pl.when(i == 0)
        def _():
            dg_ref[...] = dgain

        @pl.when(i > 0)
        def _():
            dg_ref[...] += dgain

    row = pl.BlockSpec((tm, D), lambda i: (i, 0))
    vec = pl.BlockSpec((1, D), lambda i: (0, 0))
    return _call(
        body, name, (T // tm,), [row, row, vec, row], [row, row, vec],
        [jax.ShapeDtypeStruct((T, D), F32), jax.ShapeDtypeStruct((T, D), BF16), jax.ShapeDtypeStruct((1, D), F32)],
        [], ("arbitrary",), (dn, x, gain, dres), carry)


def _loss_head(x3, gain, target, name):
    T, D = x3.shape
    tm = _tile(T, 256, 16)

    def body(x_ref, g_ref, t_ref, dx_ref, dxb_ref, ls_ref, dg_ref):
        i = pl.program_id(0)
        xv = x_ref[...]
        err = xv * _rstd(xv) * g_ref[...] - t_ref[...]
        sq = jnp.sum(jnp.sum(err * err, axis=1, keepdims=True), axis=0, keepdims=True)
        dv, dgain = _rms_bwd(err * (1.0 / D), xv, g_ref[...])
        dx_ref[...] = dv
        dxb_ref[...] = (FFN_RESIDUAL_SCALE * dv).astype(BF16)
        sqb = jnp.broadcast_to(sq, (1, LANES))

        @pl.when(i == 0)
        def _():
            dg_ref[...] = dgain
            ls_ref[...] = sqb

        @pl.when(i > 0)
        def _():
            dg_ref[...] += dgain
            ls_ref[...] += sqb

    row = pl.BlockSpec((tm, D), lambda i: (i, 0))
    vec = pl.BlockSpec((1, D), lambda i: (0, 0))
    return pl.pallas_call(
        body, name=name, grid=(T // tm,),
        in_specs=[row, vec, row],
        out_specs=[row, row, pl.BlockSpec((1, LANES), lambda i: (0, 0)), vec],
        out_shape=[jax.ShapeDtypeStruct((T, D), F32), jax.ShapeDtypeStruct((T, D), BF16),
                   jax.ShapeDtypeStruct((1, LANES), F32), jax.ShapeDtypeStruct((1, D), F32)],
        compiler_params=_params("arbitrary"),
    )(x3, gain, target)


def _norm_mm(x, gain, w, name):
    T, D = x.shape
    N = w.shape[1]
    tm = _tile(T, 512, 16)
    tn = _tile(N, 2560, LANES)

    def body(x_ref, g_ref, w_ref, n_ref, z_ref):
        @pl.when(pl.program_id(1) == 0)
        def _():
            xv = x_ref[...]
            n_ref[...] = (xv * _rstd(xv) * g_ref[...]).astype(BF16)

        z_ref[...] = jnp.dot(n_ref[...], w_ref[...], preferred_element_type=F32).astype(BF16)

    return pl.pallas_call(
        body, name=name, grid=(T // tm, N // tn),
        in_specs=[pl.BlockSpec((tm, D), lambda i, j: (i, 0)),
                  pl.BlockSpec((1, D), lambda i, j: (0, 0)),
                  pl.BlockSpec((D, tn), lambda i, j: (0, j))],
        out_specs=[pl.BlockSpec((tm, D), lambda i, j: (i, 0)),
                   pl.BlockSpec((tm, tn), lambda i, j: (i, j))],
        out_shape=[jax.ShapeDtypeStruct((T, D), BF16), jax.ShapeDtypeStruct((T, N), BF16)],
        compiler_params=_params("parallel", "arbitrary"),
    )(x, gain, w)


def _mm_fullk(a, w, trans_w, residual, out_dtype, name, carry=None):
    T, K = a.shape
    N = w.shape[0] if trans_w else w.shape[1]
    tm = _tile(T, 512, 16)
    tn = _tile(N, 2048 * 2560 // K, LANES)

    def body(*refs):
        if residual is None:
            a_ref, w_ref, o_ref = refs
        else:
            a_ref, w_ref, r_ref, o_ref = refs
        if trans_w:
            acc = _dot_nt(a_ref[...], w_ref[...])
        else:
            acc = jnp.dot(a_ref[...], w_ref[...], preferred_element_type=F32)
        if residual is not None:
            acc = acc + r_ref[...]
        o_ref[...] = acc.astype(out_dtype)

    w_spec = pl.BlockSpec((tn, K), lambda i, j: (j, 0)) if trans_w else pl.BlockSpec((K, tn), lambda i, j: (0, j))
    in_specs = [pl.BlockSpec((tm, K), lambda i, j: (i, 0)), w_spec]
    args = [a, w]
    if residual is not None:
        in_specs.append(pl.BlockSpec((tm, tn), lambda i, j: (i, j)))
        args.append(residual)
    return _call(body, name, (T // tm, N // tn), in_specs, [pl.BlockSpec((tm, tn), lambda i, j: (i, j))],
                 [jax.ShapeDtypeStruct((T, N), out_dtype)], [], ("parallel", "arbitrary"), args, carry)


def _mm_tn(a, b, name, carry=None):
    T, M = a.shape
    N = b.shape[1]
    tmw = _tile(M, 2048, LANES)
    tnw = _tile(N, 2048 * 1408 // tmw, LANES)
    tk = _tile(T, 512, 16)
    nk = T // tk

    def body(a_ref, b_ref, o_ref, acc_ref):
        k = pl.program_id(2)

        @pl.when(k == 0)
        def _():
            acc_ref[...] = jnp.zeros_like(acc_ref)

        acc_ref[...] += _dot_tn(a_ref[...], b_ref[...])

        @pl.when(k == nk - 1)
        def _():
            o_ref[...] = acc_ref[...].astype(BF16)

    return _call(
        body, name, (M // tmw, N // tnw, nk),
        [pl.BlockSpec((tk, tmw), lambda i, j, k: (k, i)),
         pl.BlockSpec((tk, tnw), lambda i, j, k: (k, j))],
        [pl.BlockSpec((tmw, tnw), lambda i, j, k: (i, j))],
        [jax.ShapeDtypeStruct((M, N), BF16)],
        [pltpu.VMEM((tmw, tnw), F32)], ("parallel", "parallel", "arbitrary"), (a, b), carry)


GELU_K = math.sqrt(2.0 / math.pi)
GELU_C = 0.044715


def _gelu_and_grad(v):
    u = GELU_K * (v + GELU_C * v * v * v)
    th = jnp.tanh(u)
    g = 0.5 * v * (1.0 + th)
    dg = 0.5 * (1.0 + th) + 0.5 * v * (1.0 - th * th) * GELU_K * (1.0 + 3.0 * GELU_C * v * v)
    return g, dg


def _neg_expm1(v):
    poly = v * (1.0 + v * (0.5 + v * (1.0 / 6 + v * (1.0 / 24 + v * (1.0 / 120 + v * (1.0 / 720))))))
    return jnp.where(v > -0.25, -poly, 1.0 - jnp.exp(v))


def _softplus_neg(lam):
    e = jnp.exp(-jnp.abs(lam))
    log1pe = jnp.where(e < 1e-4, e * (1.0 - 0.5 * e), jnp.log(1.0 + e))
    sp = jnp.maximum(-lam, 0.0) + log1pe
    dsp = -1.0 / (1.0 + jnp.exp(lam))
    return sp, dsp


def _earlier(ext, j):
    return pltpu.roll(ext, j, 0)[SUBLANES:, :]


def _later(ext, j):
    n = ext.shape[0]
    return pltpu.roll(ext, n - j, 0)[:n - SUBLANES, :]


def _taps(v, halo, K):
    ext = jnp.concatenate([halo, v], axis=0)
    return [v] + [_earlier(ext, j) for j in range(1, K)]


def _block_diag(vb, w_ref, nh, hd):
    return jnp.concatenate(
        [jnp.dot(vb[:, h * hd:(h + 1) * hd], w_ref[h], preferred_element_type=F32) for h in range(nh)], axis=1)


def _lru_gates(xc, wa_ref, ba_ref, wi_ref, bi_ref, sp, nh, hd):
    xcb = xc.astype(BF16)
    r = _sigmoid(_block_diag(xcb, wa_ref, nh, hd) + ba_ref[...])
    ig = _sigmoid(_block_diag(xcb, wi_ref, nh, hd) + bi_ref[...])
    log_a = -LRU_C * r * sp
    a = jnp.exp(log_a)
    mult = jnp.sqrt(_neg_expm1(2.0 * log_a))
    return xcb, r, ig, a, mult


def _mix_fwd(z, cw, cb, wa, ba, wi, bi, lam, sw, glo, gso, name):
    T = z.shape[0]
    DL = cb.shape[1]
    DS = gso.shape[1]
    NH, HD = wa.shape[0], wa.shape[1]
    KL, KS = cw.shape[0], sw.shape[0]
    tt = _tile(T, 128, 16)
    o_g, o_b, o_c, o_x = DL, 2 * DL, 2 * DL + DS, 2 * DL + 2 * DS

    def body(z_ref, cw_ref, cb_ref, wa_ref, ba_ref, wi_ref, bi_ref, lam_ref, sw_ref, glo_ref, gso_ref,
             h_ref, y_ref, cx_ref, cp_ref, ch_ref):
        @pl.when(pl.program_id(0) == 0)
        def _():
            cx_ref[...] = jnp.zeros_like(cx_ref)
            cp_ref[...] = jnp.zeros_like(cp_ref)
            ch_ref[...] = jnp.zeros_like(ch_ref)

        def zcol(o, n):
            return z_ref[:, o:o + n].astype(F32)

        lx = zcol(0, DL)
        xs = _taps(lx, cx_ref[...], KL)
        cx_ref[...] = lx[tt - SUBLANES:, :]
        xc = cb_ref[...] + xs[0] * cw_ref[KL - 1:KL, :]
        for j in range(1, KL):
            xc = xc + xs[j] * cw_ref[KL - 1 - j:KL - j, :]
        sp, _ = _softplus_neg(lam_ref[...])
        _, _, ig, a, mult = _lru_gates(xc, wa_ref, ba_ref, wi_ref, bi_ref, sp, NH, HD)
        b = mult * (ig * xc)
        rows = lax.broadcasted_iota(jnp.int32, (tt, DL), 0)
        s = 1
        while s < tt:
            keep = rows >= s
            b = jnp.where(keep, a * pltpu.roll(b, s, 0) + b, b)
            a = jnp.where(keep, a * pltpu.roll(a, s, 0), a)
            s *= 2
        h = a * ch_ref[SUBLANES - 1:SUBLANES, :] + b
        ch_ref[...] = h[tt - SUBLANES:, :]
        h_ref[...] = h
        ge, _ = _gelu_and_grad(zcol(o_g, DL))
        ylru = h * ge
        y_ref[:, 0:DL] = (ylru * _rstd(ylru) * glo_ref[...]).astype(BF16)

        p = zcol(o_c, DS) * zcol(o_x, DS)
        ps = _taps(p, cp_ref[...], KS)
        cp_ref[...] = p[tt - SUBLANES:, :]
        cv = ps[0] * sw_ref[KS - 1:KS, :]
        for j in range(1, KS):
            cv = cv + ps[j] * sw_ref[KS - 1 - j:KS - j, :]
        ysc = zcol(o_b, DS) * cv
        y_ref[:, DL:DL + DS] = (ysc * _rstd(ysc) * gso_ref[...]).astype(BF16)

    def full(shape):
        return pl.BlockSpec(shape, lambda t: (0,) * len(shape))

    return pl.pallas_call(
        body, name=name, grid=(T // tt,),
        in_specs=[pl.BlockSpec((tt, z.shape[1]), lambda t: (t, 0)),
                  full(cw.shape), full(cb.shape), full(wa.shape), full(ba.shape), full(wi.shape), full(bi.shape),
                  full(lam.shape), full(sw.shape), full(glo.shape), full(gso.shape)],
        out_specs=[pl.BlockSpec((tt, DL), lambda t: (t, 0)), pl.BlockSpec((tt, DL + DS), lambda t: (t, 0))],
        out_shape=[jax.ShapeDtypeStruct((T, DL), F32), jax.ShapeDtypeStruct((T, DL + DS), BF16)],
        scratch_shapes=[pltpu.VMEM((SUBLANES, DL), F32), pltpu.VMEM((SUBLANES, DS), F32),
                        pltpu.VMEM((SUBLANES, DL), F32)],
        compiler_params=_params("arbitrary"),
    )(z, cw, cb, wa, ba, wi, bi, lam, sw, glo, gso)


V_BA, V_BI, V_LAM, V_CB, V_CW, V_SW, V_GLO, V_GSO, V_ROWS = 0, 1, 2, 3, 4, 8, 11, 12, 16


def _mix_bwd(z, h, dy, cw, cb, wa, ba, wi, bi, lam, sw, glo, gso, name):
    T = z.shape[0]
    DL = cb.shape[1]
    DS = gso.shape[1]
    NH, HD = wa.shape[0], wa.shape[1]
    KL, KS = cw.shape[0], sw.shape[0]
    tt = _tile(T, 64, 16)
    nt = T // tt
    ZH = 2 * SUBLANES
    o_g, o_b, o_c, o_x = DL, 2 * DL, 2 * DL + DS, 2 * DL + 2 * DS

    def body(z_ref, zh_ref, h_ref, hh_ref, dy_ref, cw_ref, cb_ref, wa_ref, ba_ref, wi_ref, bi_ref, lam_ref,
             sw_ref, glo_ref, gso_ref, dz_ref, dwa_ref, dwi_ref, vec_ref, cdx_ref, cdc_ref, cdh_ref):
        i = pl.program_id(0)
        tr = nt - 1 - i

        @pl.when(i == 0)
        def _():
            dwa_ref[...] = jnp.zeros_like(dwa_ref)
            dwi_ref[...] = jnp.zeros_like(dwi_ref)
            vec_ref[...] = jnp.zeros_like(vec_ref)
            cdx_ref[...] = jnp.zeros_like(cdx_ref)
            cdc_ref[...] = jnp.zeros_like(cdc_ref)
            cdh_ref[...] = jnp.zeros_like(cdh_ref)

        def acc_row(r, v):
            vec_ref[pl.ds(r, 1), :] += jnp.sum(v, axis=0, keepdims=True)

        has_prev = tr > 0
        rows = lax.broadcasted_iota(jnp.int32, (tt, DL), 0)

        def zcol(o, n):
            return z_ref[:, o:o + n].astype(F32)

        def zhalo(o, n):
            return jnp.where(has_prev, zh_ref[:, o:o + n].astype(F32)[SUBLANES:, :], 0.0)

        lx = zcol(0, DL)
        xs = _taps(lx, zhalo(0, DL), KL)
        xc = cb_ref[...] + xs[0] * cw_ref[KL - 1:KL, :]
        for j in range(1, KL):
            xc = xc + xs[j] * cw_ref[KL - 1 - j:KL - j, :]
        sp, dsp = _softplus_neg(lam_ref[...])
        xcb, r, ig, a, mult = _lru_gates(xc, wa_ref, ba_ref, wi_ref, bi_ref, sp, NH, HD)
        hv = h_ref[...]
        hprev = _earlier(jnp.concatenate([jnp.where(has_prev, hh_ref[...], 0.0), hv], axis=0), 1)
        gate = zcol(o_g, DL)
        ge, dge = _gelu_and_grad(gate)
        ylru = hv * ge

        d_ylru, dglo = _rms_bwd(dy_ref[:, 0:DL].astype(F32), ylru, glo_ref[...])
        vec_ref[pl.ds(V_GLO, 1), :] += dglo
        dz_ref[:, o_g:o_g + DL] = (d_ylru * hv * dge).astype(BF16)
        bq = d_ylru * ge
        aq = jnp.where(rows == tt - 1, 1.0, pltpu.roll(a, tt - 1, 0))
        s = 1
        while s < tt:
            keep = rows < tt - s
            bq = jnp.where(keep, aq * pltpu.roll(bq, tt - s, 0) + bq, bq)
            aq = jnp.where(keep, aq * pltpu.roll(aq, tt - s, 0), aq)
            s *= 2
        dhh = bq + aq * cdh_ref[0:1, :]
        cdh_ref[0:1, :] = a[0:1, :] * dhh[0:1, :]

        da = dhh * hprev
        dmult = dhh * (ig * xc)
        d_i = dhh * mult * xc
        dxc = dhh * mult * ig
        dlog = da * a - dmult * (a * a) / mult
        acc_row(V_LAM, dlog * (-LRU_C * r) * dsp)
        dpa = dlog * (-LRU_C * sp) * r * (1.0 - r)
        dpi = d_i * ig * (1.0 - ig)
        acc_row(V_BA, dpa)
        acc_row(V_BI, dpi)
        dpab = dpa.astype(BF16)
        dpib = dpi.astype(BF16)
        back = []
        for hh in range(NH):
            sl = slice(hh * HD, (hh + 1) * HD)
            dwa_ref[hh] += _dot_tn(xcb[:, sl], dpab[:, sl])
            dwi_ref[hh] += _dot_tn(xcb[:, sl], dpib[:, sl])
            back.append(_dot_nt(dpab[:, sl], wa_ref[hh]) + _dot_nt(dpib[:, sl], wi_ref[hh]))
        dxc = dxc + jnp.concatenate(back, axis=1)

        acc_row(V_CB, dxc)
        extd = jnp.concatenate([dxc, cdx_ref[...]], axis=0)
        cdx_ref[...] = dxc[0:SUBLANES, :]
        dlx = dxc * cw_ref[KL - 1:KL, :]
        acc_row(V_CW + KL - 1, dxc * xs[0])
        for j in range(1, KL):
            dlx = dlx + _later(extd, j) * cw_ref[KL - 1 - j:KL - j, :]
            acc_row(V_CW + KL - 1 - j, dxc * xs[j])
        dz_ref[:, 0:DL] = dlx.astype(BF16)

        sb = zcol(o_b, DS)
        sc = zcol(o_c, DS)
        sx = zcol(o_x, DS)
        p = sc * sx
        ps = _taps(p, zhalo(o_c, DS) * zhalo(o_x, DS), KS)
        cv = ps[0] * sw_ref[KS - 1:KS, :]
        for j in range(1, KS):
            cv = cv + ps[j] * sw_ref[KS - 1 - j:KS - j, :]
        d_ysc, dgso = _rms_bwd(dy_ref[:, DL:DL + DS].astype(F32), sb * cv, gso_ref[...])
        vec_ref[pl.ds(V_GSO, 1), :] += dgso
        dz_ref[:, o_b:o_b + DS] = (d_ysc * cv).astype(BF16)
        dcv = d_ysc * sb
        extc = jnp.concatenate([dcv, cdc_ref[...]], axis=0)
        cdc_ref[...] = dcv[0:SUBLANES, :]
        dp = dcv * sw_ref[KS - 1:KS, :]
        acc_row(V_SW + KS - 1, dcv * ps[0])
        for j in range(1, KS):
            dp = dp + _later(extc, j) * sw_ref[KS - 1 - j:KS - j, :]
            acc_row(V_SW + KS - 1 - j, dcv * ps[j])
        dz_ref[:, o_c:o_c + DS] = (dp * sx).astype(BF16)
        dz_ref[:, o_x:o_x + DS] = (dp * sc).astype(BF16)

    def full(shape):
        return pl.BlockSpec(shape, lambda t: (0,) * len(shape))

    def rev(t):
        return nt - 1 - t

    def halo(t, rows):
        return jnp.maximum(rev(t) * (tt // rows) - 1, 0)

    return pl.pallas_call(
        body, name=name, grid=(nt,),
        in_specs=[pl.BlockSpec((tt, z.shape[1]), lambda t: (rev(t), 0)),
                  pl.BlockSpec((ZH, z.shape[1]), lambda t: (halo(t, ZH), 0)),
                  pl.BlockSpec((tt, DL), lambda t: (rev(t), 0)),
                  pl.BlockSpec((SUBLANES, DL), lambda t: (halo(t, SUBLANES), 0)),
                  pl.BlockSpec((tt, DL + DS), lambda t: (rev(t), 0)),
                  full(cw.shape), full(cb.shape), full(wa.shape), full(ba.shape), full(wi.shape), full(bi.shape),
                  full(lam.shape), full(sw.shape), full(glo.shape), full(gso.shape)],
        out_specs=[pl.BlockSpec((tt, z.shape[1]), lambda t: (rev(t), 0)),
                   full(wa.shape), full(wi.shape), full((V_ROWS, DL))],
        out_shape=[jax.ShapeDtypeStruct(z.shape, BF16), jax.ShapeDtypeStruct(wa.shape, F32),
                   jax.ShapeDtypeStruct(wi.shape, F32), jax.ShapeDtypeStruct((V_ROWS, DL), F32)],
        scratch_shapes=[pltpu.VMEM((SUBLANES, DL), F32), pltpu.VMEM((SUBLANES, DS), F32),
                        pltpu.VMEM((SUBLANES, DL), F32)],
        compiler_params=_params("arbitrary"),
    )(z, z, h, h, dy, cw, cb, wa, ba, wi, bi, lam, sw, glo, gso)


def _pair_add(p, r1, c, name):
    G, R, C = r1.shape
    tr = _tile(R, 256, 16)
    tc = _tile(C, 1408, LANES)

    def body(c_ref, p_ref, r_ref, o_ref):
        o_ref[...] = (p_ref[...].astype(F32) + r_ref[...].astype(F32)).astype(BF16)

    blk = (None, tr, tc)
    return pl.pallas_call(
        body, name=name,
        grid_spec=pltpu.PrefetchScalarGridSpec(
            num_scalar_prefetch=1, grid=(G, R // tr, C // tc),
            in_specs=[pl.BlockSpec(blk, lambda g, i, j, cr: (2 * g + cr[0], i, j)),
                      pl.BlockSpec(blk, lambda g, i, j, cr: (g, i, j))],
            out_specs=pl.BlockSpec(blk, lambda g, i, j, cr: (g, i, j))),
        out_shape=jax.ShapeDtypeStruct((G, R, C), BF16),
        compiler_params=_params("parallel", "parallel", "parallel"),
    )(c, p, r1)


def _quad_add(s, r2, qc, axis, name):
    _, R, W = r2.shape
    tr = _tile(R, 256, 16)

    def body(qc_ref, s_ref, r0_ref, r1_ref, r2_ref, o_ref):
        o_ref[...] = ((s_ref[...].astype(F32) + r0_ref[...].astype(F32)) + r1_ref[...].astype(F32)) + r2_ref[...].astype(F32)

    blk = (None, tr, W)
    if axis == 1:
        own = pl.BlockSpec(blk, lambda i, qr: (0, i, qr[0]))
    else:
        own = pl.BlockSpec(blk, lambda i, qr: (qr[0], i, 0))
    return pl.pallas_call(
        body, name=name,
        grid_spec=pltpu.PrefetchScalarGridSpec(
            num_scalar_prefetch=1, grid=(R // tr,),
            in_specs=[own] + [pl.BlockSpec(blk, lambda i, qr, j=j: (j, i, 0)) for j in range(3)],
            out_specs=pl.BlockSpec(blk, lambda i, qr: (qr[1], i, 0))),
        out_shape=jax.ShapeDtypeStruct((2, R, W), F32),
        compiler_params=_params("parallel"),
    )(qc, s, r2, r2, r2)


def _cast_into_full(shard, qc, axis, name):
    R, W = shard.shape
    Rh = R // 2
    tr = _tile(Rh, 256, 16)
    nb = Rh // tr

    def body(qc_ref, s_ref, o_ref):
        o_ref[...] = s_ref[...].astype(BF16)

    if axis == 1:
        out_shape = (2, Rh, 4 * W)
        out_spec = pl.BlockSpec((None, tr, W), lambda hf, i, qr: (hf, i, qr[0]))
    else:
        out_shape = (8, Rh, W)
        out_spec = pl.BlockSpec((None, tr, W), lambda hf, i, qr: (2 * qr[0] + hf, i, 0))
    return pl.pallas_call(
        body, name=name,
        grid_spec=pltpu.PrefetchScalarGridSpec(
            num_scalar_prefetch=1, grid=(2, nb),
            in_specs=[pl.BlockSpec((tr, W), lambda hf, i, qr: (hf * nb + i, 0))],
            out_specs=out_spec),
        out_shape=jax.ShapeDtypeStruct(out_shape, BF16),
        compiler_params=_params("parallel", "parallel"),
    )(qc, shard)


def _adamw(w, g, m, v, name):
    R, C = w.shape
    tr = _tile(R, 256, SUBLANES)
    tc = _tile(C, 2048, LANES)
    c1 = 1.0 - ADAM_B1 ** ADAM_STEP
    c2 = 1.0 - ADAM_B2 ** ADAM_STEP

    def body(w_ref, g_ref, m_ref, v_ref, d_ref, mo_ref, vo_ref, go_ref):
        gv = g_ref[...]
        go_ref[...] = gv
        mn = ADAM_B1 * m_ref[...] + (1.0 - ADAM_B1) * gv
        vn = ADAM_B2 * v_ref[...] + (1.0 - ADAM_B2) * (gv * gv)
        mo_ref[...] = mn
        vo_ref[...] = vn
        d_ref[...] = -ADAM_LR * ((mn / c1) / (jnp.sqrt(vn / c2) + ADAM_EPS) + ADAM_WD * w_ref[...])

    blk = pl.BlockSpec((tr, tc), lambda i, j: (i, j))
    sh = jax.ShapeDtypeStruct((R, C), F32)
    return pl.pallas_call(
        body, name=name, grid=(R // tr, C // tc),
        in_specs=[blk] * 4, out_specs=[blk] * 4, out_shape=[sh] * 4,
        compiler_params=_params("parallel", "parallel"),
    )(w, g, m, v)


def _other_chips(x, y):
    return [(1 - x, y), (x, 1 - y), (1 - x, 1 - y)]


def _remote(src, dst, send_sems, recv_sems, idx, dev):
    return pltpu.make_async_remote_copy(src_ref=src, dst_ref=dst, send_sem=send_sems.at[idx], recv_sem=recv_sems.at[idx],
                                        device_id=dev, device_id_type=MESH)


def _gather_carry(fulls, axes):
    M = len(fulls)

    def win(outs, m, qq, cc):
        if axes[m] == 1:
            W = fulls[m].shape[2] // 4
            return outs[m].at[cc, :, pl.ds(pl.multiple_of(qq * W, LANES), W)]
        return outs[m].at[2 * qq + cc]

    def ici(outs, sems, m, j, src_q):
        x, y, c = _me()
        cx, cy = _other_chips(x, y)[j]
        blk = win(outs, m, src_q, c)
        return _remote(blk, blk, sems[0], sems[1], 6 * m + j, (cx, cy, c))

    def d2d(outs, sems, m, j, half):
        x, y, c = _me()
        cx, cy = _other_chips(x, y)[j]
        blk = win(outs, m, 2 * cx + cy, half)
        return _remote(blk, blk, sems[0], sems[1], 6 * m + 3 + j, (x, y, 1 - c))

    def start(ins, outs, sems):
        x, y, c = _me()
        for m in range(M):
            for j in range(3):
                ici(outs, sems, m, j, 2 * x + y).start()

    def finish(ins, outs, sems):
        x, y, c = _me()
        chips = _other_chips(x, y)
        for m in range(M):
            for j, (cx, cy) in enumerate(chips):
                ici(outs, sems, m, j, 2 * cx + cy).wait_recv()
                d2d(outs, sems, m, j, c).start()
        for m in range(M):
            for j in range(3):
                d2d(outs, sems, m, j, 1 - c).wait_recv()
        for m in range(M):
            for j in range(3):
                ici(outs, sems, m, j, 2 * x + y).wait_send()
                d2d(outs, sems, m, j, c).wait_send()

    return _Carry(fulls, [jax.ShapeDtypeStruct(f.shape, f.dtype) for f in fulls], {m: m for m in range(M)},
                  [pltpu.SemaphoreType.DMA((6 * M,)), pltpu.SemaphoreType.DMA((6 * M,))], start, finish)


def _pair_exchange_carry(parts):
    M = len(parts)
    groups = [p.shape[0] // 2 for p in parts]
    base = [sum(groups[:m]) for m in range(M)]
    out_shape = [jax.ShapeDtypeStruct((g,) + p.shape[1:], p.dtype) for g, p in zip(groups, parts)]

    def copies(ins, outs, sems):
        x, y, c = _me()
        return [_remote(ins[m].at[2 * g + 1 - c], outs[m].at[g], sems[0], sems[1], base[m] + g, (x, y, 1 - c))
                for m in range(M) for g in range(groups[m])]

    def start(ins, outs, sems):
        for cp in copies(ins, outs, sems):
            cp.start()

    def finish(ins, outs, sems):
        for cp in copies(ins, outs, sems):
            cp.wait()

    n = sum(groups)
    return _Carry(parts, out_shape, {}, [pltpu.SemaphoreType.DMA((n,)), pltpu.SemaphoreType.DMA((n,))], start, finish)


def _chip_exchange_carry(sums, axes):
    M = len(sums)
    out_shape = []
    for s, ax in zip(sums, axes):
        _, Rh, C = s.shape
        out_shape.append(jax.ShapeDtypeStruct((3, Rh, C // 4 if ax == 1 else C), s.dtype))

    def copies(ins, outs, sems):
        x, y, c = _me()
        cps = []
        for m in range(M):
            for j, (cx, cy) in enumerate(_other_chips(x, y)):
                qj = 2 * cx + cy
                if axes[m] == 1:
                    W = sums[m].shape[2] // 4
                    src = ins[m].at[0, :, pl.ds(pl.multiple_of(qj * W, LANES), W)]
                else:
                    src = ins[m].at[qj]
                cps.append(_remote(src, outs[m].at[j], sems[0], sems[1], 3 * m + j, (cx, cy, c)))
        return cps

    def start(ins, outs, sems):
        for cp in copies(ins, outs, sems):
            cp.start()

    def finish(ins, outs, sems):
        for cp in copies(ins, outs, sems):
            cp.wait()

    return _Carry(sums, out_shape, {}, [pltpu.SemaphoreType.DMA((3 * M,)), pltpu.SemaphoreType.DMA((3 * M,))],
                  start, finish)


def _pair_share_carry(bufs):
    M = len(bufs)

    def start(ins, outs, sems):
        x, y, c = _me()
        for m in range(M):
            _remote(outs[m].at[c], outs[m].at[c], sems[0], sems[1], m, (x, y, 1 - c)).start()

    def finish(ins, outs, sems):
        x, y, c = _me()
        for m in range(M):
            _remote(outs[m].at[c], outs[m].at[c], sems[0], sems[1], m, (x, y, 1 - c)).wait_send()
            _remote(outs[m].at[1 - c], outs[m].at[1 - c], sems[0], sems[1], m, (x, y, 1 - c)).wait_recv()

    return _Carry(bufs, [jax.ShapeDtypeStruct(b.shape, b.dtype) for b in bufs], {m: m for m in range(M)},
                  [pltpu.SemaphoreType.DMA((M,)), pltpu.SemaphoreType.DMA((M,))], start, finish)


def _allreduce_small(v, name):
    R, W = v.shape
    Rh = R // 2

    def body(v_ref, o_ref, sib, quad, send_sems, recv_sems):
        x, y, c = _me()
        q = 2 * x + y
        sibling = (x, y, 1 - c)
        pair = _remote(v_ref, sib, send_sems, recv_sems, 0, sibling)
        pair.start()
        pair.wait()
        mine = pl.ds(pl.multiple_of(c * Rh, SUBLANES), Rh)
        quad[0] = v_ref[mine, :] + sib[mine, :]
        cps = []
        for k in (1, 2, 3):
            peer = (1 - x if k & 2 else x, 1 - y if k & 1 else y, c)
            cps.append(_remote(quad.at[0], quad.at[k], send_sems, recv_sems, k, peer))
            cps[-1].start()
        for cp in cps:
            cp.wait()
        acc = quad[q]
        for p in (1, 2, 3):
            acc = acc + quad[jnp.bitwise_xor(q, p)]
        o_ref[mine, :] = acc
        theirs = pl.ds(pl.multiple_of((1 - c) * Rh, SUBLANES), Rh)
        done = _remote(o_ref.at[mine, :], o_ref.at[mine, :], send_sems, recv_sems, 4, sibling)
        done.start()
        done.wait_send()
        _remote(o_ref.at[theirs, :], o_ref.at[theirs, :], send_sems, recv_sems, 4, sibling).wait_recv()

    vm = pl.BlockSpec(memory_space=pltpu.VMEM)
    return pl.pallas_call(
        body, name=name, in_specs=[vm], out_specs=vm, out_shape=jax.ShapeDtypeStruct((R, W), F32),
        scratch_shapes=[pltpu.VMEM((R, W), F32), pltpu.VMEM((4, Rh, W), F32),
                        pltpu.SemaphoreType.DMA((5,)), pltpu.SemaphoreType.DMA((5,))],
        compiler_params=pltpu.CompilerParams(vmem_limit_bytes=VMEM_LIMIT),
    )(v)


def _pack(pieces):
    flat = []
    for p in pieces:
        p = p.reshape(-1).astype(F32)
        pad = (-p.shape[0]) % PACK_ALIGN
        flat.append(jnp.pad(p, (0, pad)).reshape(-1, PACK_W))
    if sum(f.shape[0] for f in flat) % (2 * SUBLANES):
        flat.append(jnp.zeros((SUBLANES, PACK_W), F32))
    return jnp.concatenate(flat, axis=0)


def _unpack(packed, shapes):
    out, row = [], 0
    for shp in shapes:
        n = math.prod(shp)
        rows = -(-n // PACK_ALIGN) * SUBLANES
        out.append(packed[row:row + rows].reshape(-1)[:n].reshape(shp))
        row += rows
    return out


def kernel(x, ffn1_norm, ffn1_w_gate, ffn1_w_up, ffn1_w_down, mix_norm, w_in, lru_conv_w, lru_conv_b, lru_w_a, lru_b_a, lru_w_i, lru_b_i, lru_lambda, sc_conv_w, lru_out_norm, sc_out_norm, w_out, ffn2_norm, ffn2_w_gate, ffn2_w_up, ffn2_w_down, final_norm, loss_target, m_ffn1_norm, m_ffn1_w_gate, m_ffn1_w_up, m_ffn1_w_down, m_mix_norm, m_w_in, m_lru_conv_w, m_lru_conv_b, m_lru_w_a, m_lru_b_a, m_lru_w_i, m_lru_b_i, m_lru_lambda, m_sc_conv_w, m_lru_out_norm, m_sc_out_norm, m_w_out, m_ffn2_norm, m_ffn2_w_gate, m_ffn2_w_up, m_ffn2_w_down, m_final_norm, v_ffn1_norm, v_ffn1_w_gate, v_ffn1_w_up, v_ffn1_w_down, v_mix_norm, v_w_in, v_lru_conv_w, v_lru_conv_b, v_lru_w_a, v_lru_b_a, v_lru_w_i, v_lru_b_i, v_lru_lambda, v_sc_conv_w, v_lru_out_norm, v_sc_out_norm, v_w_out, v_ffn2_norm, v_ffn2_w_gate, v_ffn2_w_up, v_ffn2_w_down, v_final_norm):
    vals = locals()
    w = {n: vals[n] for n in WEIGHTS}
    mom = {n: vals["m_" + n] for n in WEIGHTS}
    var = {n: vals["v_" + n] for n in WEIGHTS}

    xi, yi, ci = _me()
    qi = 2 * xi + yi
    c_arr = jnp.reshape(ci, (1,)).astype(jnp.int32)
    qc_arr = jnp.stack([qi, ci]).astype(jnp.int32)

    T, D = x.shape[1], x.shape[2]
    xt = x.reshape(T, D)
    target = loss_target.reshape(T, D)
    DL = lru_conv_b.shape[-1]
    NH, HD = lru_w_a.shape[1], lru_w_a.shape[2]
    KL, KS = lru_conv_w.shape[1], sc_conv_w.shape[1]
    DLq = lru_conv_w.shape[2]

    axis_of = dict(zip(BIG, BIG_AXIS))
    first_names = ['ffn1_w_gate', 'ffn1_w_up', 'ffn1_w_down']
    later_names = ['w_in', 'w_out', 'ffn2_w_gate', 'ffn2_w_up', 'ffn2_w_down']

    def unview(n, g):
        return g.reshape(2 * g.shape[1], g.shape[2]) if axis_of[n] == 1 else g.reshape(8 * g.shape[1], g.shape[2])

    placed = {n: _cast_into_full(w[n][0], qc_arr, axis_of[n], "cast_" + n) for n in BIG}
    gathered = _run_carry(_gather_carry([placed[n] for n in first_names], [axis_of[n] for n in first_names]),
                          "gather_ffn1_weights")
    full = {n: unview(n, g) for n, g in zip(first_names, gathered)}
    gather_later = _gather_carry([placed[n] for n in later_names], [axis_of[n] for n in later_names])

    taps = jnp.zeros((2 * SUBLANES, DL), F32)
    taps = lax.dynamic_update_slice(taps, lru_conv_w[0], (0, qi * DLq))
    taps = lax.dynamic_update_slice(taps, sc_conv_w[0], (KL, qi * DLq))
    taps = _allreduce_small(jnp.where(ci == 0, taps, 0.0), "gather_conv_taps")
    cw, sw = taps[0:KL], taps[KL:KL + KS]

    cb = lru_conv_b
    wa, wi = lru_w_a[0].astype(BF16), lru_w_i[0].astype(BF16)
    ba, bi = lru_b_a.reshape(1, DL), lru_b_i.reshape(1, DL)
    mix_args = (cw, cb, wa, ba, wi, bi, lru_lambda, sw, lru_out_norm, sc_out_norm)
    gf = final_norm.reshape(1, D)

    res = _ffn_fwd(xt, ffn1_norm, full['ffn1_w_gate'], full['ffn1_w_up'], full['ffn1_w_down'], "ffn1_fwd", gather_later)
    x1, n1, G1, U1 = res[:4]
    full.update({n: unview(n, g) for n, g in zip(later_names, res[4:])})
    n2, z = _norm_mm(x1, mix_norm, full['w_in'], "mix_in_proj")
    h, ymix = _mix_fwd(z, *mix_args, "mix_fwd")
    x2 = _mm_fullk(ymix, full['w_out'], False, x1, F32, "mix_out_proj")[0]
    x3, n3, G2, U2 = _ffn_fwd(x2, ffn2_norm, full['ffn2_w_gate'], full['ffn2_w_up'], full['ffn2_w_down'], "ffn2_fwd")
    dx3, d3b, sqerr, dgf = _loss_head(x3, gf, target, "loss_head")

    grads, halves, shared = {}, {}, {}

    def pair_carry(names):
        parts = []
        for n in names:
            R, C = grads[n].shape
            parts.append(grads[n].reshape(2, R // 2, C) if axis_of[n] == 1 else grads[n].reshape(8, R // 8, C))
        return parts, _pair_exchange_carry(parts)

    def pair_add(names, parts, recv):
        return [_pair_add(p, r, c_arr, "grad_pair_add_" + n) for n, p, r in zip(names, parts, recv)]

    def chip_carry(names, sums):
        return _chip_exchange_carry(sums, [axis_of[n] for n in names])

    def chip_add(names, sums, recv):
        for n, s, r in zip(names, sums, recv):
            halves[n] = _quad_add(s, r, qc_arr, axis_of[n], "grad_chip_add_" + n)

    dG2, dU2, H2, dn3 = _ffn_bwd_act(d3b, G2, U2, full['ffn2_w_gate'], full['ffn2_w_up'], full['ffn2_w_down'], "ffn2_bwd")
    grads['ffn2_w_gate'] = _mm_tn(n3, dG2, "ffn2_dwg")[0]
    grads['ffn2_w_up'] = _mm_tn(n3, dU2, "ffn2_dwu")[0]
    grads['ffn2_w_down'] = _mm_tn(H2, d3b, "ffn2_dwd")[0]
    names_a = ['ffn2_w_gate', 'ffn2_w_up', 'ffn2_w_down', 'w_out', 'w_in']
    parts, carry = pair_carry(names_a[:3])
    res = _rms_bwd_res(dn3, x2, ffn2_norm, dx3, 1.0, "ffn2_norm_bwd", carry)
    dx2, dx2b, dg_ffn2 = res[:3]
    sums_a = pair_add(names_a[:3], parts, res[3:])

    dy = _mm_fullk(dx2b, full['w_out'], True, None, BF16, "mix_out_bwd")[0]
    grads['w_out'] = _mm_tn(ymix, dx2b, "mix_dwout")[0]
    dz, dwa, dwi, vec = _mix_bwd(z, h, dy, *mix_args, "mix_bwd")
    grads['w_in'] = _mm_tn(n2, dz, "mix_dwin")[0]
    parts, carry = pair_carry(names_a[3:])
    res = _mm_fullk(dz, full['w_in'], True, None, F32, "mix_in_bwd", carry)
    dn2 = res[0]
    sums_a += pair_add(names_a[3:], parts, res[1:])
    dx1, d1b, dg_mix = _rms_bwd_res(dn2, x1, mix_norm, dx2, FFN_RESIDUAL_SCALE, "mix_norm_bwd")

    res = _ffn_bwd_act(d1b, G1, U1, full['ffn1_w_gate'], full['ffn1_w_up'], full['ffn1_w_down'], "ffn1_bwd",
                       chip_carry(names_a, sums_a))
    dG1, dU1, H1, dn1 = res[:4]
    chip_add(names_a, sums_a, res[4:])
    res = _mm_tn(n1, dG1, "ffn1_dwg", _pair_share_carry([halves[n] for n in names_a]))
    grads['ffn1_w_gate'] = res[0]
    shared.update(zip(names_a, res[1:]))
    grads['ffn1_w_up'] = _mm_tn(n1, dU1, "ffn1_dwu")[0]
    names_b = ['ffn1_w_gate', 'ffn1_w_up']
    parts, carry = pair_carry(names_b)
    res = _rms_bwd_res(dn1, xt, ffn1_norm, dx1, 1.0, "ffn1_norm_bwd", carry)
    dx0, dg_ffn1 = res[0], res[2]
    sums_b = pair_add(names_b, parts, res[3:])
    res = _mm_tn(H1, d1b, "ffn1_dwd", chip_carry(names_b, sums_b))
    grads['ffn1_w_down'] = res[0]
    chip_add(names_b, sums_b, res[1:])
    names_c = ['ffn1_w_down']
    parts, carry = pair_carry(names_c)
    sums_c = pair_add(names_c, parts, _run_carry(carry, "grad_pair_exchange_ffn1_out"))
    chip_add(names_c, sums_c, _run_carry(chip_carry(names_c, sums_c), "grad_chip_exchange_ffn1_out"))
    names_bc = names_b + names_c
    shared.update(zip(names_bc, _run_carry(_pair_share_carry([halves[n] for n in names_bc]), "grad_pair_share")))
    out_g, out_d, out_m, out_v = {}, {}, {}, {}
    for n in BIG:
        shp = w[n].shape
        outs = _adamw(w[n][0], shared[n].reshape(shp[1], shp[2]), mom[n][0], var[n][0], "adamw_" + n)
        out_d[n], out_m[n], out_v[n], out_g[n] = (a.reshape(shp) for a in outs)

    small = [n for n in WEIGHTS if n not in BIG]
    local_small = {
        'ffn1_norm': dg_ffn1, 'mix_norm': dg_mix, 'lru_conv_w': vec[V_CW:V_CW + KL], 'lru_conv_b': vec[V_CB],
        'lru_w_a': dwa, 'lru_b_a': vec[V_BA], 'lru_w_i': dwi, 'lru_b_i': vec[V_BI], 'lru_lambda': vec[V_LAM],
        'sc_conv_w': vec[V_SW:V_SW + KS], 'lru_out_norm': vec[V_GLO], 'sc_out_norm': vec[V_GSO],
        'ffn2_norm': dg_ffn2, 'final_norm': dgf,
    }
    full_shapes = [local_small[n].shape for n in small] + [(1,)]
    reduced = _allreduce_small(_pack([local_small[n] for n in small] + [sqerr[0, 0:1]]), "allreduce_small")
    reduced = _unpack(reduced, full_shapes)
    loss = (0.5 / D) * reduced[-1][0]
    gsm = {}
    for n, g in zip(small, reduced[:-1]):
        if n in SMALL_SHARDED:
            g = lax.dynamic_slice(g, (0, qi * DLq), (g.shape[0], DLq))
        gsm[n] = g.reshape(w[n].shape)
    small_shapes = [w[n].shape for n in small]
    d_s, m_s, v_s, _ = _adamw(_pack([w[n] for n in small]), _pack([gsm[n] for n in small]),
                              _pack([mom[n] for n in small]), _pack([var[n] for n in small]), "adamw_small")
    for n, d, mn, vn in zip(small, _unpack(d_s, small_shapes), _unpack(m_s, small_shapes), _unpack(v_s, small_shapes)):
        out_g[n], out_d[n], out_m[n], out_v[n] = gsm[n], d, mn, vn

    return (loss, dx0.reshape(x.shape), *[out_g[n] for n in WEIGHTS], *[out_d[n] for n in WEIGHTS],
            *[out_m[n] for n in WEIGHTS], *[out_v[n] for n in WEIGHTS])
```

```python
import math

import jax
import jax.numpy as jnp
from jax import lax
from jax.experimental import pallas as pl
from jax.experimental.pallas import tpu as pltpu

F32 = jnp.float32
BF16 = jnp.bfloat16
MESH = pl.DeviceIdType.MESH
ANY = pl.BlockSpec(memory_space=pl.ANY)

NORM_EPS = 1e-6
LRU_C = 8.0
FFN_RESIDUAL_SCALE = 0.5
ADAM_LR = 0.001
ADAM_B1 = 0.9
ADAM_B2 = 0.999
ADAM_EPS = 1e-08
ADAM_WD = 0.01
ADAM_STEP = 10

V7X_VMEM_BYTES = 64 * 2**20
VMEM_LIMIT = V7X_VMEM_BYTES - 8 * 2**20
LANES = 128
SUBLANES = 8
PACK_W = LANES
PACK_ALIGN = SUBLANES * PACK_W

WEIGHTS = ['ffn1_norm', 'ffn1_w_gate', 'ffn1_w_up', 'ffn1_w_down', 'mix_norm', 'w_in', 'lru_conv_w', 'lru_conv_b',
           'lru_w_a', 'lru_b_a', 'lru_w_i', 'lru_b_i', 'lru_lambda', 'sc_conv_w', 'lru_out_norm', 'sc_out_norm',
           'w_out', 'ffn2_norm', 'ffn2_w_gate', 'ffn2_w_up', 'ffn2_w_down', 'final_norm']
BIG = ['ffn1_w_gate', 'ffn1_w_up', 'ffn1_w_down', 'w_in', 'w_out', 'ffn2_w_gate', 'ffn2_w_up', 'ffn2_w_down']
BIG_AXIS = [1, 1, 0, 1, 0, 1, 1, 0]
SMALL_SHARDED = ['lru_conv_w', 'sc_conv_w']


def _tile(n, pref, mult):
    if n <= pref:
        return n
    t = (pref // mult) * mult
    while t >= mult:
        if n % t == 0:
            return t
        t -= mult
    return n


def _params(*sem):
    return pltpu.CompilerParams(dimension_semantics=sem, vmem_limit_bytes=VMEM_LIMIT)


def _me():
    return lax.axis_index("x"), lax.axis_index("y"), lax.axis_index("c")


def _sigmoid(v):
    return 1.0 / (1.0 + jnp.exp(-v))


def _rstd(v):
    return lax.rsqrt(jnp.mean(v * v, axis=-1, keepdims=True) + NORM_EPS)


def _rms_bwd(dy, v, gain):
    r = _rstd(v)
    w = gain * dy
    dv = r * w - v * (r * r * r) * jnp.mean(v * w, axis=-1, keepdims=True)
    dgain = jnp.sum(dy * v * r, axis=0, keepdims=True)
    return dv, dgain


def _dot_nt(a, b):
    return lax.dot_general(a, b, (((1,), (1,)), ((), ())), preferred_element_type=F32)


def _dot_tn(a, b):
    return lax.dot_general(a, b, (((0,), (0,)), ((), ())), preferred_element_type=F32)


class _Carry:
    def __init__(self, inputs, out_shape, aliases, sems, start, finish):
        self.inputs, self.out_shape, self.aliases, self.sems = list(inputs), list(out_shape), dict(aliases), list(sems)
        self.start, self.finish = start, finish


def _call(body, name, grid, in_specs, out_specs, out_shape, scratch_shapes, semantics, args, carry=None):
    if carry is None:
        return pl.pallas_call(body, name=name, grid=grid, in_specs=in_specs, out_specs=out_specs, out_shape=out_shape,
                              scratch_shapes=scratch_shapes, compiler_params=_params(*semantics))(*args)
    ni, no, ns = len(in_specs), len(out_specs), len(scratch_shapes)
    ci, co = len(carry.inputs), len(carry.out_shape)

    def carrying(*refs):
        ins, refs = refs[:ni], refs[ni:]
        cins, refs = refs[:ci], refs[ci:]
        outs, refs = refs[:no], refs[no:]
        couts, refs = refs[:co], refs[co:]
        scratch, csems = refs[:ns], refs[ns:]
        first = pl.program_id(0) == 0
        last = pl.program_id(0) == grid[0] - 1
        for ax in range(1, len(grid)):
            first = jnp.logical_and(first, pl.program_id(ax) == 0)
            last = jnp.logical_and(last, pl.program_id(ax) == grid[ax] - 1)

        @pl.when(first)
        def _():
            carry.start(cins, couts, csems)

        body(*ins, *outs, *scratch)

        @pl.when(last)
        def _():
            carry.finish(cins, couts, csems)

    return pl.pallas_call(
        carrying, name=name, grid=grid, in_specs=list(in_specs) + [ANY] * ci, out_specs=list(out_specs) + [ANY] * co,
        out_shape=list(out_shape) + carry.out_shape,
        input_output_aliases={ni + i: no + j for i, j in carry.aliases.items()},
        scratch_shapes=list(scratch_shapes) + carry.sems,
        compiler_params=_params(*(["arbitrary"] * len(grid))),
    )(*args, *carry.inputs)


def _run_carry(carry, name):
    ci, co = len(carry.inputs), len(carry.out_shape)

    def body(*refs):
        cins, couts, csems = refs[:ci], refs[ci:ci + co], refs[ci + co:]
        carry.start(cins, couts, csems)
        carry.finish(cins, couts, csems)

    return pl.pallas_call(body, name=name, in_specs=[ANY] * ci, out_specs=[ANY] * co, out_shape=carry.out_shape,
                          input_output_aliases=carry.aliases, scratch_shapes=carry.sems)(*carry.inputs)


def _ffn_fwd(x, gain, wg, wu, wd, name, carry=None):
    T, D = x.shape
    FF = wg.shape[1]
    tm = _tile(T, 512, 16)
    tf = _tile(FF, 512, LANES)
    nf = FF // tf

    def body(x_ref, g_ref, wg_ref, wu_ref, wd_ref, xo_ref, n_ref, G_ref, U_ref, acc_ref):
        f = pl.program_id(1)

        @pl.when(f == 0)
        def _():
            xv = x_ref[...]
            n_ref[...] = (xv * _rstd(xv) * g_ref[...]).astype(BF16)
            acc_ref[...] = jnp.zeros_like(acc_ref)

        n = n_ref[...]
        G = jnp.dot(n, wg_ref[...], preferred_element_type=F32)
        U = jnp.dot(n, wu_ref[...], preferred_element_type=F32)
        G_ref[...] = G.astype(BF16)
        U_ref[...] = U.astype(BF16)
        H = (G * _sigmoid(G) * U).astype(BF16)
        acc_ref[...] += jnp.dot(H, wd_ref[...], preferred_element_type=F32)

        @pl.when(f == nf - 1)
        def _():
            xo_ref[...] = x_ref[...] + FFN_RESIDUAL_SCALE * acc_ref[...]

    return _call(
        body, name, (T // tm, nf),
        [pl.BlockSpec((tm, D), lambda i, f: (i, 0)),
         pl.BlockSpec((1, D), lambda i, f: (0, 0)),
         pl.BlockSpec((D, tf), lambda i, f: (0, f)),
         pl.BlockSpec((D, tf), lambda i, f: (0, f)),
         pl.BlockSpec((tf, D), lambda i, f: (f, 0))],
        [pl.BlockSpec((tm, D), lambda i, f: (i, 0)),
         pl.BlockSpec((tm, D), lambda i, f: (i, 0)),
         pl.BlockSpec((tm, tf), lambda i, f: (i, f)),
         pl.BlockSpec((tm, tf), lambda i, f: (i, f))],
        [jax.ShapeDtypeStruct((T, D), F32), jax.ShapeDtypeStruct((T, D), BF16),
         jax.ShapeDtypeStruct((T, FF), BF16), jax.ShapeDtypeStruct((T, FF), BF16)],
        [pltpu.VMEM((tm, D), F32)], ("parallel", "arbitrary"), (x, gain, wg, wu, wd), carry)


def _ffn_bwd_act(db, G, U, wg, wu, wd, name, carry=None):
    T, D = db.shape
    FF = wg.shape[1]
    tm = _tile(T, 512, 16)
    tf = _tile(FF, 512, LANES)

    def body(d_ref, G_ref, U_ref, wg_ref, wu_ref, wd_ref, dG_ref, dU_ref, H_ref, dn_ref):
        f = pl.program_id(1)
        dH = _dot_nt(d_ref[...], wd_ref[...])
        Gv = G_ref[...].astype(F32)
        Uv = U_ref[...].astype(F32)
        s = _sigmoid(Gv)
        sg = Gv * s
        H_ref[...] = (sg * Uv).astype(BF16)
        dU = (dH * sg).astype(BF16)
        dG = (dH * Uv * (s * (1.0 + Gv * (1.0 - s)))).astype(BF16)
        dG_ref[...] = dG
        dU_ref[...] = dU
        contrib = _dot_nt(dG, wg_ref[...]) + _dot_nt(dU, wu_ref[...])

        @pl.when(f == 0)
        def _():
            dn_ref[...] = contrib

        @pl.when(f > 0)
        def _():
            dn_ref[...] += contrib

    return _call(
        body, name, (T // tm, FF // tf),
        [pl.BlockSpec((tm, D), lambda i, f: (i, 0)),
         pl.BlockSpec((tm, tf), lambda i, f: (i, f)),
         pl.BlockSpec((tm, tf), lambda i, f: (i, f)),
         pl.BlockSpec((D, tf), lambda i, f: (0, f)),
         pl.BlockSpec((D, tf), lambda i, f: (0, f)),
         pl.BlockSpec((tf, D), lambda i, f: (f, 0))],
        [pl.BlockSpec((tm, tf), lambda i, f: (i, f)),
         pl.BlockSpec((tm, tf), lambda i, f: (i, f)),
         pl.BlockSpec((tm, tf), lambda i, f: (i, f)),
         pl.BlockSpec((tm, D), lambda i, f: (i, 0))],
        [jax.ShapeDtypeStruct((T, FF), BF16), jax.ShapeDtypeStruct((T, FF), BF16),
         jax.ShapeDtypeStruct((T, FF), BF16), jax.ShapeDtypeStruct((T, D), F32)],
        [], ("parallel", "arbitrary"), (db, G, U, wg, wu, wd), carry)


TAIL_ROWS = 128


class _NormBwdTail:
    def __init__(self, T, D, tm, scale):
        self.T, self.D, self.tm, self.scale = T, D, tm, scale
        self.ni = T // tm
        self.scratch = [pltpu.VMEM((tm, D), F32), pltpu.VMEM((tm, D), F32), pltpu.VMEM((tm, D), F32),
                        pltpu.VMEM((tm, D), BF16), pltpu.SemaphoreType.DMA((4,))]
        self.out_shape = [jax.ShapeDtypeStruct((T, D), F32), jax.ShapeDtypeStruct((T, D), BF16)]

    def _rows(self, k):
        return pl.ds(pl.multiple_of(k * self.tm, self.tm), self.tm)

    def _loads(self, k, x_hbm, r_hbm, bufs):
        xbuf, rbuf, _, _, sems = bufs
        return [pltpu.make_async_copy(x_hbm.at[self._rows(k)], xbuf, sems.at[0]),
                pltpu.make_async_copy(r_hbm.at[self._rows(k)], rbuf, sems.at[1])]

    def _stores(self, k, dx_hbm, dxb_hbm, bufs):
        _, _, obuf, obbuf, sems = bufs
        return [pltpu.make_async_copy(obuf, dx_hbm.at[self._rows(k)], sems.at[2]),
                pltpu.make_async_copy(obbuf, dxb_hbm.at[self._rows(k)], sems.at[3])]

    def prefetch(self, i, x_hbm, r_hbm, bufs):
        for cp in self._loads(i, x_hbm, r_hbm, bufs):
            cp.start()

    def run(self, i, acc_ref, g_ref, x_hbm, r_hbm, dx_hbm, dxb_hbm, dg_ref, bufs):
        xbuf, rbuf, obuf, obbuf, _ = bufs
        for cp in self._loads(i, x_hbm, r_hbm, bufs):
            cp.wait()

        @pl.when(i > 0)
        def _():
            for cp in self._stores(i - 1, dx_hbm, dxb_hbm, bufs):
                cp.wait()

        dgain = None
        for r0 in range(0, self.tm, TAIL_ROWS):
            rs = slice(r0, min(r0 + TAIL_ROWS, self.tm))
            dv, dgr = _rms_bwd(acc_ref[rs, :], xbuf[rs, :], g_ref[...])
            dx = rbuf[rs, :] + dv
            obuf[rs, :] = dx
            obbuf[rs, :] = (self.scale * dx).astype(BF16)
            dgain = dgr if dgain is None else dgain + dgr
        for cp in self._stores(i, dx_hbm, dxb_hbm, bufs):
            cp.start()

        @pl.when(i == 0)
        def _():
            dg_ref[...] = dgain

        @pl.when(i > 0)
        def _():
            dg_ref[...] += dgain

        @pl.when(i == self.ni - 1)
        def _():
            for cp in self._stores(i, dx_hbm, dxb_hbm, bufs):
                cp.wait()


def _ffn_bwd_fused(db, G, U, wg, wu, wd, x_in, gain, dres, scale, name):
    T, D = db.shape
    FF = wg.shape[1]
    tm = _tile(T, 512, 16)
    tf = _tile(FF, 512, LANES)
    nf = FF // tf
    tail = _NormBwdTail(T, D, tm, scale)

    def body(d_ref, G_ref, U_ref, wg_ref, wu_ref, wd_ref, g_ref, x_hbm, r_hbm,
             dG_ref, dU_ref, H_ref, dg_ref, dx_hbm, dxb_hbm, acc_ref, *bufs):
        i, f = pl.program_id(0), pl.program_id(1)

        @pl.when(f == 0)
        def _():
            tail.prefetch(i, x_hbm, r_hbm, bufs)

        dH = _dot_nt(d_ref[...], wd_ref[...])
        Gv = G_ref[...].astype(F32)
        Uv = U_ref[...].astype(F32)
        s = _sigmoid(Gv)
        sg = Gv * s
        H_ref[...] = (sg * Uv).astype(BF16)
        dU = (dH * sg).astype(BF16)
        dG = (dH * Uv * (s * (1.0 + Gv * (1.0 - s)))).astype(BF16)
        dG_ref[...] = dG
        dU_ref[...] = dU
        contrib = _dot_nt(dG, wg_ref[...]) + _dot_nt(dU, wu_ref[...])

        @pl.when(f == 0)
        def _():
            acc_ref[...] = contrib

        @pl.when(f > 0)
        def _():
            acc_ref[...] += contrib

        @pl.when(f == nf - 1)
        def _():
            tail.run(i, acc_ref, g_ref, x_hbm, r_hbm, dx_hbm, dxb_hbm, dg_ref, bufs)

    act = pl.BlockSpec((tm, tf), lambda i, f: (i, f))
    return pl.pallas_call(
        body, name=name, grid=(T // tm, nf),
        in_specs=[pl.BlockSpec((tm, D), lambda i, f: (i, 0)), act, act,
                  pl.BlockSpec((D, tf), lambda i, f: (0, f)),
                  pl.BlockSpec((D, tf), lambda i, f: (0, f)),
                  pl.BlockSpec((tf, D), lambda i, f: (f, 0)),
                  pl.BlockSpec((1, D), lambda i, f: (0, 0)), ANY, ANY],
        out_specs=[act, act, act, pl.BlockSpec((1, D), lambda i, f: (0, 0)), ANY, ANY],
        out_shape=[jax.ShapeDtypeStruct((T, FF), BF16)] * 3 + [jax.ShapeDtypeStruct((1, D), F32)] + tail.out_shape,
        scratch_shapes=[pltpu.VMEM((tm, D), F32)] + tail.scratch,
        compiler_params=_params("arbitrary", "arbitrary"),
    )(db, G, U, wg, wu, wd, gain, x_in, dres)


def _mm_nt_norm_bwd(a, w, x_in, gain, dres, scale, name, carry=None):
    T, K = a.shape
    D = w.shape[0]
    tm = _tile(T, 512, 16)
    tk = _tile(K, 1280, LANES)
    nk = K // tk
    tail = _NormBwdTail(T, D, tm, scale)

    def body(a_ref, w_ref, g_ref, x_hbm, r_hbm, dg_ref, dx_hbm, dxb_hbm, acc_ref, *bufs):
        i, k = pl.program_id(0), pl.program_id(1)

        @pl.when(k == 0)
        def _():
            tail.prefetch(i, x_hbm, r_hbm, bufs)

        contrib = _dot_nt(a_ref[...], w_ref[...])

        @pl.when(k == 0)
        def _():
            acc_ref[...] = contrib

        @pl.when(k > 0)
        def _():
            acc_ref[...] += contrib

        @pl.when(k == nk - 1)
        def _():
            tail.run(i, acc_ref, g_ref, x_hbm, r_hbm, dx_hbm, dxb_hbm, dg_ref, bufs)

    return _call(
        body, name, (T // tm, nk),
        [pl.BlockSpec((tm, tk), lambda i, k: (i, k)), pl.BlockSpec((D, tk), lambda i, k: (0, k)),
         pl.BlockSpec((1, D), lambda i, k: (0, 0)), ANY, ANY],
        [pl.BlockSpec((1, D), lambda i, k: (0, 0)), ANY, ANY],
        [jax.ShapeDtypeStruct((1, D), F32)] + tail.out_shape,
        [pltpu.VMEM((tm, D), F32)] + tail.scratch, ("arbitrary", "arbitrary"), (a, w, gain, x_in, dres), carry)


def _rms_bwd_res(dn, x, gain, dres, scale, name, carry=None):
    T, D = x.shape
    tm = _tile(T, 256, 16)

    def body(dn_ref, x_ref, g_ref, dr_ref, dx_ref, dxb_ref, dg_ref):
        i = pl.program_id(0)
        dv, dgain = _rms_bwd(dn_ref[...], x_ref[...], g_ref[...])
        dx = dr_ref[...] + dv
        dx_ref[...] = dx
        dxb_ref[...] = (scale * dx).astype(BF16)

        @pl.when(i == 0)
        def _():
            dg_ref[...] = dgain

        @pl.when(i > 0)
        def _():
            dg_ref[...] += dgain

    row = pl.BlockSpec((tm, D), lambda i: (i, 0))
    vec = pl.BlockSpec((1, D), lambda i: (0, 0))
    return _call(
        body, name, (T // tm,), [row, row, vec, row], [row, row, vec],
        [jax.ShapeDtypeStruct((T, D), F32), jax.ShapeDtypeStruct((T, D), BF16), jax.ShapeDtypeStruct((1, D), F32)],
        [], ("arbitrary",), (dn, x, gain, dres), carry)


def _loss_head(x3, gain, target, name):
    T, D = x3.shape
    tm = _tile(T, 256, 16)

    def body(x_ref, g_ref, t_ref, dx_ref, dxb_ref, ls_ref, dg_ref):
        i = pl.program_id(0)
        xv = x_ref[...]
        err = xv * _rstd(xv) * g_ref[...] - t_ref[...]
        sq = jnp.sum(jnp.sum(err * err, axis=1, keepdims=True), axis=0, keepdims=True)
        dv, dgain = _rms_bwd(err * (1.0 / D), xv, g_ref[...])
        dx_ref[...] = dv
        dxb_ref[...] = (FFN_RESIDUAL_SCALE * dv).astype(BF16)
        sqb = jnp.broadcast_to(sq, (1, LANES))

        @pl.when(i == 0)
        def _():
            dg_ref[...] = dgain
            ls_ref[...] = sqb

        @pl.when(i > 0)
        def _():
            dg_ref[...] += dgain
            ls_ref[...] += sqb

    row = pl.BlockSpec((tm, D), lambda i: (i, 0))
    vec = pl.BlockSpec((1, D), lambda i: (0, 0))
    return pl.pallas_call(
        body, name=name, grid=(T // tm,),
        in_specs=[row, vec, row],
        out_specs=[row, row, pl.BlockSpec((1, LANES), lambda i: (0, 0)), vec],
        out_shape=[jax.ShapeDtypeStruct((T, D), F32), jax.ShapeDtypeStruct((T, D), BF16),
                   jax.ShapeDtypeStruct((1, LANES), F32), jax.ShapeDtypeStruct((1, D), F32)],
        compiler_params=_params("arbitrary"),
    )(x3, gain, target)


def _norm_mm(x, gain, w, name):
    T, D = x.shape
    N = w.shape[1]
    tm = _tile(T, 512, 16)
    tn = _tile(N, 2560, LANES)

    def body(x_ref, g_ref, w_ref, n_ref, z_ref):
        @pl.when(pl.program_id(1) == 0)
        def _():
            xv = x_ref[...]
            n_ref[...] = (xv * _rstd(xv) * g_ref[...]).astype(BF16)

        z_ref[...] = jnp.dot(n_ref[...], w_ref[...], preferred_element_type=F32).astype(BF16)

    return pl.pallas_call(
        body, name=name, grid=(T // tm, N // tn),
        in_specs=[pl.BlockSpec((tm, D), lambda i, j: (i, 0)),
                  pl.BlockSpec((1, D), lambda i, j: (0, 0)),
                  pl.BlockSpec((D, tn), lambda i, j: (0, j))],
        out_specs=[pl.BlockSpec((tm, D), lambda i, j: (i, 0)),
                   pl.BlockSpec((tm, tn), lambda i, j: (i, j))],
        out_shape=[jax.ShapeDtypeStruct((T, D), BF16), jax.ShapeDtypeStruct((T, N), BF16)],
        compiler_params=_params("parallel", "arbitrary"),
    )(x, gain, w)


def _mm_fullk(a, w, trans_w, residual, out_dtype, name, carry=None):
    T, K = a.shape
    N = w.shape[0] if trans_w else w.shape[1]
    tm = _tile(T, 512, 16)
    tn = _tile(N, 2048 * 2560 // K, LANES)

    def body(*refs):
        if residual is None:
            a_ref, w_ref, o_ref = refs
        else:
            a_ref, w_ref, r_ref, o_ref = refs
        if trans_w:
            acc = _dot_nt(a_ref[...], w_ref[...])
        else:
            acc = jnp.dot(a_ref[...], w_ref[...], preferred_element_type=F32)
        if residual is not None:
            acc = acc + r_ref[...]
        o_ref[...] = acc.astype(out_dtype)

    w_spec = pl.BlockSpec((tn, K), lambda i, j: (j, 0)) if trans_w else pl.BlockSpec((K, tn), lambda i, j: (0, j))
    in_specs = [pl.BlockSpec((tm, K), lambda i, j: (i, 0)), w_spec]
    args = [a, w]
    if residual is not None:
        in_specs.append(pl.BlockSpec((tm, tn), lambda i, j: (i, j)))
        args.append(residual)
    return _call(body, name, (T // tm, N // tn), in_specs, [pl.BlockSpec((tm, tn), lambda i, j: (i, j))],
                 [jax.ShapeDtypeStruct((T, N), out_dtype)], [], ("parallel", "arbitrary"), args, carry)


def _mm_tn(a, b, name, carry=None):
    T, M = a.shape
    N = b.shape[1]
    tmw = _tile(M, 2048, LANES)
    tnw = _tile(N, 2048 * 1408 // tmw, LANES)
    tk = _tile(T, 512, 16)
    nk = T // tk

    def body(a_ref, b_ref, o_ref, acc_ref):
        k = pl.program_id(2)

        @pl.when(k == 0)
        def _():
            acc_ref[...] = jnp.zeros_like(acc_ref)

        acc_ref[...] += _dot_tn(a_ref[...], b_ref[...])

        @pl.when(k == nk - 1)
        def _():
            o_ref[...] = acc_ref[...].astype(BF16)

    return _call(
        body, name, (M // tmw, N // tnw, nk),
        [pl.BlockSpec((tk, tmw), lambda i, j, k: (k, i)),
         pl.BlockSpec((tk, tnw), lambda i, j, k: (k, j))],
        [pl.BlockSpec((tmw, tnw), lambda i, j, k: (i, j))],
        [jax.ShapeDtypeStruct((M, N), BF16)],
        [pltpu.VMEM((tmw, tnw), F32)], ("parallel", "parallel", "arbitrary"), (a, b), carry)


GELU_K = math.sqrt(2.0 / math.pi)
GELU_C = 0.044715


def _gelu_and_grad(v):
    u = GELU_K * (v + GELU_C * v * v * v)
    th = jnp.tanh(u)
    g = 0.5 * v * (1.0 + th)
    dg = 0.5 * (1.0 + th) + 0.5 * v * (1.0 - th * th) * GELU_K * (1.0 + 3.0 * GELU_C * v * v)
    return g, dg


def _neg_expm1(v):
    poly = v * (1.0 + v * (0.5 + v * (1.0 / 6 + v * (1.0 / 24 + v * (1.0 / 120 + v * (1.0 / 720))))))
    return jnp.where(v > -0.25, -poly, 1.0 - jnp.exp(v))


def _softplus_neg(lam):
    e = jnp.exp(-jnp.abs(lam))
    log1pe = jnp.where(e < 1e-4, e * (1.0 - 0.5 * e), jnp.log(1.0 + e))
    sp = jnp.maximum(-lam, 0.0) + log1pe
    dsp = -1.0 / (1.0 + jnp.exp(lam))
    return sp, dsp


def _earlier(ext, j):
    return pltpu.roll(ext, j, 0)[SUBLANES:, :]


def _later(ext, j):
    n = ext.shape[0]
    return pltpu.roll(ext, n - j, 0)[:n - SUBLANES, :]


def _taps(v, halo, K):
    ext = jnp.concatenate([halo, v], axis=0)
    return [v] + [_earlier(ext, j) for j in range(1, K)]


def _block_diag(vb, w_ref, nh, hd):
    return jnp.concatenate(
        [jnp.dot(vb[:, h * hd:(h + 1) * hd], w_ref[h], preferred_element_type=F32) for h in range(nh)], axis=1)


def _lru_gates(xc, wa_ref, ba_ref, wi_ref, bi_ref, sp, nh, hd):
    xcb = xc.astype(BF16)
    r = _sigmoid(_block_diag(xcb, wa_ref, nh, hd) + ba_ref[...])
    ig = _sigmoid(_block_diag(xcb, wi_ref, nh, hd) + bi_ref[...])
    log_a = -LRU_C * r * sp
    a = jnp.exp(log_a)
    mult = jnp.sqrt(_neg_expm1(2.0 * log_a))
    return xcb, r, ig, a, mult


def _mix_fwd(z, cw, cb, wa, ba, wi, bi, lam, sw, glo, gso, name):
    T = z.shape[0]
    DL = cb.shape[1]
    DS = gso.shape[1]
    NH, HD = wa.shape[0], wa.shape[1]
    KL, KS = cw.shape[0], sw.shape[0]
    tt = _tile(T, 128, 16)
    o_g, o_b, o_c, o_x = DL, 2 * DL, 2 * DL + DS, 2 * DL + 2 * DS

    def body(z_ref, cw_ref, cb_ref, wa_ref, ba_ref, wi_ref, bi_ref, lam_ref, sw_ref, glo_ref, gso_ref,
             h_ref, y_ref, cx_ref, cp_ref, ch_ref):
        @pl.when(pl.program_id(0) == 0)
        def _():
            cx_ref[...] = jnp.zeros_like(cx_ref)
            cp_ref[...] = jnp.zeros_like(cp_ref)
            ch_ref[...] = jnp.zeros_like(ch_ref)

        def zcol(o, n):
            return z_ref[:, o:o + n].astype(F32)

        lx = zcol(0, DL)
        xs = _taps(lx, cx_ref[...], KL)
        cx_ref[...] = lx[tt - SUBLANES:, :]
        xc = cb_ref[...] + xs[0] * cw_ref[KL - 1:KL, :]
        for j in range(1, KL):
            xc = xc + xs[j] * cw_ref[KL - 1 - j:KL - j, :]
        sp, _ = _softplus_neg(lam_ref[...])
        _, _, ig, a, mult = _lru_gates(xc, wa_ref, ba_ref, wi_ref, bi_ref, sp, NH, HD)
        b = mult * (ig * xc)
        rows = lax.broadcasted_iota(jnp.int32, (tt, DL), 0)
        s = 1
        while s < tt:
            keep = rows >= s
            b = jnp.where(keep, a * pltpu.roll(b, s, 0) + b, b)
            a = jnp.where(keep, a * pltpu.roll(a, s, 0), a)
            s *= 2
        h = a * ch_ref[SUBLANES - 1:SUBLANES, :] + b
        ch_ref[...] = h[tt - SUBLANES:, :]
        h_ref[...] = h
        ge, _ = _gelu_and_grad(zcol(o_g, DL))
        ylru = h * ge
        y_ref[:, 0:DL] = (ylru * _rstd(ylru) * glo_ref[...]).astype(BF16)

        p = zcol(o_c, DS) * zcol(o_x, DS)
        ps = _taps(p, cp_ref[...], KS)
        cp_ref[...] = p[tt - SUBLANES:, :]
        cv = ps[0] * sw_ref[KS - 1:KS, :]
        for j in range(1, KS):
            cv = cv + ps[j] * sw_ref[KS - 1 - j:KS - j, :]
        ysc = zcol(o_b, DS) * cv
        y_ref[:, DL:DL + DS] = (ysc * _rstd(ysc) * gso_ref[...]).astype(BF16)

    def full(shape):
        return pl.BlockSpec(shape, lambda t: (0,) * len(shape))

    return pl.pallas_call(
        body, name=name, grid=(T // tt,),
        in_specs=[pl.BlockSpec((tt, z.shape[1]), lambda t: (t, 0)),
                  full(cw.shape), full(cb.shape), full(wa.shape), full(ba.shape), full(wi.shape), full(bi.shape),
                  full(lam.shape), full(sw.shape), full(glo.shape), full(gso.shape)],
        out_specs=[pl.BlockSpec((tt, DL), lambda t: (t, 0)), pl.BlockSpec((tt, DL + DS), lambda t: (t, 0))],
        out_shape=[jax.ShapeDtypeStruct((T, DL), F32), jax.ShapeDtypeStruct((T, DL + DS), BF16)],
        scratch_shapes=[pltpu.VMEM((SUBLANES, DL), F32), pltpu.VMEM((SUBLANES, DS), F32),
                        pltpu.VMEM((SUBLANES, DL), F32)],
        compiler_params=_params("arbitrary"),
    )(z, cw, cb, wa, ba, wi, bi, lam, sw, glo, gso)


V_BA, V_BI, V_LAM, V_CB, V_CW, V_SW, V_GLO, V_GSO, V_ROWS = 0, 1, 2, 3, 4, 8, 11, 12, 16


def _mix_bwd(z, h, dy, cw, cb, wa, ba, wi, bi, lam, sw, glo, gso, name):
    T = z.shape[0]
    DL = cb.shape[1]
    DS = gso.shape[1]
    NH, HD = wa.shape[0], wa.shape[1]
    KL, KS = cw.shape[0], sw.shape[0]
    tt = _tile(T, 64, 16)
    nt = T // tt
    ZH = 2 * SUBLANES
    o_g, o_b, o_c, o_x = DL, 2 * DL, 2 * DL + DS, 2 * DL + 2 * DS

    def body(z_ref, zh_ref, h_ref, hh_ref, dy_ref, cw_ref, cb_ref, wa_ref, ba_ref, wi_ref, bi_ref, lam_ref,
             sw_ref, glo_ref, gso_ref, dz_ref, dwa_ref, dwi_ref, vec_ref, cdx_ref, cdc_ref, cdh_ref):
        i = pl.program_id(0)
        tr = nt - 1 - i

        @pl.when(i == 0)
        def _():
            dwa_ref[...] = jnp.zeros_like(dwa_ref)
            dwi_ref[...] = jnp.zeros_like(dwi_ref)
            vec_ref[...] = jnp.zeros_like(vec_ref)
            cdx_ref[...] = jnp.zeros_like(cdx_ref)
            cdc_ref[...] = jnp.zeros_like(cdc_ref)
            cdh_ref[...] = jnp.zeros_like(cdh_ref)

        def acc_row(r, v):
            vec_ref[pl.ds(r, 1), :] += jnp.sum(v, axis=0, keepdims=True)

        has_prev = tr > 0
        rows = lax.broadcasted_iota(jnp.int32, (tt, DL), 0)

        def zcol(o, n):
            return z_ref[:, o:o + n].astype(F32)

        def zhalo(o, n):
            return jnp.where(has_prev, zh_ref[:, o:o + n].astype(F32)[SUBLANES:, :], 0.0)

        lx = zcol(0, DL)
        xs = _taps(lx, zhalo(0, DL), KL)
        xc = cb_ref[...] + xs[0] * cw_ref[KL - 1:KL, :]
        for j in range(1, KL):
            xc = xc + xs[j] * cw_ref[KL - 1 - j:KL - j, :]
        sp, dsp = _softplus_neg(lam_ref[...])
        xcb, r, ig, a, mult = _lru_gates(xc, wa_ref, ba_ref, wi_ref, bi_ref, sp, NH, HD)
        hv = h_ref[...]
        hprev = _earlier(jnp.concatenate([jnp.where(has_prev, hh_ref[...], 0.0), hv], axis=0), 1)
        gate = zcol(o_g, DL)
        ge, dge = _gelu_and_grad(gate)
        ylru = hv * ge

        d_ylru, dglo = _rms_bwd(dy_ref[:, 0:DL].astype(F32), ylru, glo_ref[...])
        vec_ref[pl.ds(V_GLO, 1), :] += dglo
        dz_ref[:, o_g:o_g + DL] = (d_ylru * hv * dge).astype(BF16)
        bq = d_ylru * ge
        aq = jnp.where(rows == tt - 1, 1.0, pltpu.roll(a, tt - 1, 0))
        s = 1
        while s < tt:
            keep = rows < tt - s
            bq = jnp.where(keep, aq * pltpu.roll(bq, tt - s, 0) + bq, bq)
            aq = jnp.where(keep, aq * pltpu.roll(aq, tt - s, 0), aq)
            s *= 2
        dhh = bq + aq * cdh_ref[0:1, :]
        cdh_ref[0:1, :] = a[0:1, :] * dhh[0:1, :]

        da = dhh * hprev
        dmult = dhh * (ig * xc)
        d_i = dhh * mult * xc
        dxc = dhh * mult * ig
        dlog = da * a - dmult * (a * a) / mult
        acc_row(V_LAM, dlog * (-LRU_C * r) * dsp)
        dpa = dlog * (-LRU_C * sp) * r * (1.0 - r)
        dpi = d_i * ig * (1.0 - ig)
        acc_row(V_BA, dpa)
        acc_row(V_BI, dpi)
        dpab = dpa.astype(BF16)
        dpib = dpi.astype(BF16)
        back = []
        for hh in range(NH):
            sl = slice(hh * HD, (hh + 1) * HD)
            dwa_ref[hh] += _dot_tn(xcb[:, sl], dpab[:, sl])
            dwi_ref[hh] += _dot_tn(xcb[:, sl], dpib[:, sl])
            back.append(_dot_nt(dpab[:, sl], wa_ref[hh]) + _dot_nt(dpib[:, sl], wi_ref[hh]))
        dxc = dxc + jnp.concatenate(back, axis=1)

        acc_row(V_CB, dxc)
        extd = jnp.concatenate([dxc, cdx_ref[...]], axis=0)
        cdx_ref[...] = dxc[0:SUBLANES, :]
        dlx = dxc * cw_ref[KL - 1:KL, :]
        acc_row(V_CW + KL - 1, dxc * xs[0])
        for j in range(1, KL):
            dlx = dlx + _later(extd, j) * cw_ref[KL - 1 - j:KL - j, :]
            acc_row(V_CW + KL - 1 - j, dxc * xs[j])
        dz_ref[:, 0:DL] = dlx.astype(BF16)

        sb = zcol(o_b, DS)
        sc = zcol(o_c, DS)
        sx = zcol(o_x, DS)
        p = sc * sx
        ps = _taps(p, zhalo(o_c, DS) * zhalo(o_x, DS), KS)
        cv = ps[0] * sw_ref[KS - 1:KS, :]
        for j in range(1, KS):
            cv = cv + ps[j] * sw_ref[KS - 1 - j:KS - j, :]
        d_ysc, dgso = _rms_bwd(dy_ref[:, DL:DL + DS].astype(F32), sb * cv, gso_ref[...])
        vec_ref[pl.ds(V_GSO, 1), :] += dgso
        dz_ref[:, o_b:o_b + DS] = (d_ysc * cv).astype(BF16)
        dcv = d_ysc * sb
        extc = jnp.concatenate([dcv, cdc_ref[...]], axis=0)
        cdc_ref[...] = dcv[0:SUBLANES, :]
        dp = dcv * sw_ref[KS - 1:KS, :]
        acc_row(V_SW + KS - 1, dcv * ps[0])
        for j in range(1, KS):
            dp = dp + _later(extc, j) * sw_ref[KS - 1 - j:KS - j, :]
            acc_row(V_SW + KS - 1 - j, dcv * ps[j])
        dz_ref[:, o_c:o_c + DS] = (dp * sx).astype(BF16)
        dz_ref[:, o_x:o_x + DS] = (dp * sc).astype(BF16)

    def full(shape):
        return pl.BlockSpec(shape, lambda t: (0,) * len(shape))

    def rev(t):
        return nt - 1 - t

    def halo(t, rows):
        return jnp.maximum(rev(t) * (tt // rows) - 1, 0)

    return pl.pallas_call(
        body, name=name, grid=(nt,),
        in_specs=[pl.BlockSpec((tt, z.shape[1]), lambda t: (rev(t), 0)),
                  pl.BlockSpec((ZH, z.shape[1]), lambda t: (halo(t, ZH), 0)),
                  pl.BlockSpec((tt, DL), lambda t: (rev(t), 0)),
                  pl.BlockSpec((SUBLANES, DL), lambda t: (halo(t, SUBLANES), 0)),
                  pl.BlockSpec((tt, DL + DS), lambda t: (rev(t), 0)),
                  full(cw.shape), full(cb.shape), full(wa.shape), full(ba.shape), full(wi.shape), full(bi.shape),
                  full(lam.shape), full(sw.shape), full(glo.shape), full(gso.shape)],
        out_specs=[pl.BlockSpec((tt, z.shape[1]), lambda t: (rev(t), 0)),
                   full(wa.shape), full(wi.shape), full((V_ROWS, DL))],
        out_shape=[jax.ShapeDtypeStruct(z.shape, BF16), jax.ShapeDtypeStruct(wa.shape, F32),
                   jax.ShapeDtypeStruct(wi.shape, F32), jax.ShapeDtypeStruct((V_ROWS, DL), F32)],
        scratch_shapes=[pltpu.VMEM((SUBLANES, DL), F32), pltpu.VMEM((SUBLANES, DS), F32),
                        pltpu.VMEM((SUBLANES, DL), F32)],
        compiler_params=_params("arbitrary"),
    )(z, z, h, h, dy, cw, cb, wa, ba, wi, bi, lam, sw, glo, gso)


def _pair_add(p, r1, c, name):
    G, R, C = r1.shape
    tr = _tile(R, 256, 16)
    tc = _tile(C, 1408, LANES)

    def body(c_ref, p_ref, r_ref, o_ref):
        o_ref[...] = (p_ref[...].astype(F32) + r_ref[...].astype(F32)).astype(BF16)

    blk = (None, tr, tc)
    return pl.pallas_call(
        body, name=name,
        grid_spec=pltpu.PrefetchScalarGridSpec(
            num_scalar_prefetch=1, grid=(G, R // tr, C // tc),
            in_specs=[pl.BlockSpec(blk, lambda g, i, j, cr: (2 * g + cr[0], i, j)),
                      pl.BlockSpec(blk, lambda g, i, j, cr: (g, i, j))],
            out_specs=pl.BlockSpec(blk, lambda g, i, j, cr: (g, i, j))),
        out_shape=jax.ShapeDtypeStruct((G, R, C), BF16),
        compiler_params=_params("parallel", "parallel", "parallel"),
    )(c, p, r1)


def _quad_add(s, r2, qc, axis, name):
    _, R, W = r2.shape
    tr = _tile(R, 256, 16)

    def body(qc_ref, s_ref, r0_ref, r1_ref, r2_ref, o_ref):
        o_ref[...] = ((s_ref[...].astype(F32) + r0_ref[...].astype(F32)) + r1_ref[...].astype(F32)) + r2_ref[...].astype(F32)

    blk = (None, tr, W)
    if axis == 1:
        own = pl.BlockSpec(blk, lambda i, qr: (0, i, qr[0]))
    else:
        own = pl.BlockSpec(blk, lambda i, qr: (qr[0], i, 0))
    return pl.pallas_call(
        body, name=name,
        grid_spec=pltpu.PrefetchScalarGridSpec(
            num_scalar_prefetch=1, grid=(R // tr,),
            in_specs=[own] + [pl.BlockSpec(blk, lambda i, qr, j=j: (j, i, 0)) for j in range(3)],
            out_specs=pl.BlockSpec(blk, lambda i, qr: (qr[1], i, 0))),
        out_shape=jax.ShapeDtypeStruct((2, R, W), F32),
        compiler_params=_params("parallel"),
    )(qc, s, r2, r2, r2)


def _cast_into_full(shard, qc, axis, name):
    R, W = shard.shape
    Rh = R // 2
    tr = _tile(Rh, 256, 16)
    nb = Rh // tr

    def body(qc_ref, s_ref, o_ref):
        o_ref[...] = s_ref[...].astype(BF16)

    if axis == 1:
        out_shape = (2, Rh, 4 * W)
        out_spec = pl.BlockSpec((None, tr, W), lambda hf, i, qr: (hf, i, qr[0]))
    else:
        out_shape = (8, Rh, W)
        out_spec = pl.BlockSpec((None, tr, W), lambda hf, i, qr: (2 * qr[0] + hf, i, 0))
    return pl.pallas_call(
        body, name=name,
        grid_spec=pltpu.PrefetchScalarGridSpec(
            num_scalar_prefetch=1, grid=(2, nb),
            in_specs=[pl.BlockSpec((tr, W), lambda hf, i, qr: (hf * nb + i, 0))],
            out_specs=out_spec),
        out_shape=jax.ShapeDtypeStruct(out_shape, BF16),
        compiler_params=_params("parallel", "parallel"),
    )(qc, shard)


def _adamw(w, g, m, v, name):
    R, C = w.shape
    tr = _tile(R, 256, SUBLANES)
    tc = _tile(C, 2048, LANES)
    c1 = 1.0 - ADAM_B1 ** ADAM_STEP
    c2 = 1.0 - ADAM_B2 ** ADAM_STEP

    def body(w_ref, g_ref, m_ref, v_ref, d_ref, mo_ref, vo_ref, go_ref):
        gv = g_ref[...]
        go_ref[...] = gv
        mn = ADAM_B1 * m_ref[...] + (1.0 - ADAM_B1) * gv
        vn = ADAM_B2 * v_ref[...] + (1.0 - ADAM_B2) * (gv * gv)
        mo_ref[...] = mn
        vo_ref[...] = vn
        d_ref[...] = -ADAM_LR * ((mn / c1) / (jnp.sqrt(vn / c2) + ADAM_EPS) + ADAM_WD * w_ref[...])

    blk = pl.BlockSpec((tr, tc), lambda i, j: (i, j))
    sh = jax.ShapeDtypeStruct((R, C), F32)
    return pl.pallas_call(
        body, name=name, grid=(R // tr, C // tc),
        in_specs=[blk] * 4, out_specs=[blk] * 4, out_shape=[sh] * 4,
        compiler_params=_params("parallel", "parallel"),
    )(w, g, m, v)


def _other_chips(x, y):
    return [(1 - x, y), (x, 1 - y), (1 - x, 1 - y)]


def _remote(src, dst, send_sems, recv_sems, idx, dev):
    return pltpu.make_async_remote_copy(src_ref=src, dst_ref=dst, send_sem=send_sems.at[idx], recv_sem=recv_sems.at[idx],
                                        device_id=dev, device_id_type=MESH)


def _gather_carry(fulls, axes):
    M = len(fulls)

    def win(outs, m, qq, cc):
        if axes[m] == 1:
            W = fulls[m].shape[2] // 4
            return outs[m].at[cc, :, pl.ds(pl.multiple_of(qq * W, LANES), W)]
        return outs[m].at[2 * qq + cc]

    def ici(outs, sems, m, j, src_q):
        x, y, c = _me()
        cx, cy = _other_chips(x, y)[j]
        blk = win(outs, m, src_q, c)
        return _remote(blk, blk, sems[0], sems[1], 6 * m + j, (cx, cy, c))

    def d2d(outs, sems, m, j, half):
        x, y, c = _me()
        cx, cy = _other_chips(x, y)[j]
        blk = win(outs, m, 2 * cx + cy, half)
        return _remote(blk, blk, sems[0], sems[1], 6 * m + 3 + j, (x, y, 1 - c))

    def start(ins, outs, sems):
        x, y, c = _me()
        for m in range(M):
            for j in range(3):
                ici(outs, sems, m, j, 2 * x + y).start()

    def finish(ins, outs, sems):
        x, y, c = _me()
        chips = _other_chips(x, y)
        for m in range(M):
            for j, (cx, cy) in enumerate(chips):
                ici(outs, sems, m, j, 2 * cx + cy).wait_recv()
                d2d(outs, sems, m, j, c).start()
        for m in range(M):
            for j in range(3):
                d2d(outs, sems, m, j, 1 - c).wait_recv()
        for m in range(M):
            for j in range(3):
                ici(outs, sems, m, j, 2 * x + y).wait_send()
                d2d(outs, sems, m, j, c).wait_send()

    return _Carry(fulls, [jax.ShapeDtypeStruct(f.shape, f.dtype) for f in fulls], {m: m for m in range(M)},
                  [pltpu.SemaphoreType.DMA((6 * M,)), pltpu.SemaphoreType.DMA((6 * M,))], start, finish)


def _pair_exchange_carry(parts):
    M = len(parts)
    groups = [p.shape[0] // 2 for p in parts]
    base = [sum(groups[:m]) for m in range(M)]
    out_shape = [jax.ShapeDtypeStruct((g,) + p.shape[1:], p.dtype) for g, p in zip(groups, parts)]

    def copies(ins, outs, sems):
        x, y, c = _me()
        return [_remote(ins[m].at[2 * g + 1 - c], outs[m].at[g], sems[0], sems[1], base[m] + g, (x, y, 1 - c))
                for m in range(M) for g in range(groups[m])]

    def start(ins, outs, sems):
        for cp in copies(ins, outs, sems):
            cp.start()

    def finish(ins, outs, sems):
        for cp in copies(ins, outs, sems):
            cp.wait()

    n = sum(groups)
    return _Carry(parts, out_shape, {}, [pltpu.SemaphoreType.DMA((n,)), pltpu.SemaphoreType.DMA((n,))], start, finish)


def _chip_exchange_carry(sums, axes):
    M = len(sums)
    out_shape = []
    for s, ax in zip(sums, axes):
        _, Rh, C = s.shape
        out_shape.append(jax.ShapeDtypeStruct((3, Rh, C // 4 if ax == 1 else C), s.dtype))

    def copies(ins, outs, sems):
        x, y, c = _me()
        cps = []
        for m in range(M):
            for j, (cx, cy) in enumerate(_other_chips(x, y)):
                qj = 2 * cx + cy
                if axes[m] == 1:
                    W = sums[m].shape[2] // 4
                    src = ins[m].at[0, :, pl.ds(pl.multiple_of(qj * W, LANES), W)]
                else:
                    src = ins[m].at[qj]
                cps.append(_remote(src, outs[m].at[j], sems[0], sems[1], 3 * m + j, (cx, cy, c)))
        return cps

    def start(ins, outs, sems):
        for cp in copies(ins, outs, sems):
            cp.start()

    def finish(ins, outs, sems):
        for cp in copies(ins, outs, sems):
            cp.wait()

    return _Carry(sums, out_shape, {}, [pltpu.SemaphoreType.DMA((3 * M,)), pltpu.SemaphoreType.DMA((3 * M,))],
                  start, finish)


def _pair_share_carry(bufs):
    M = len(bufs)

    def start(ins, outs, sems):
        x, y, c = _me()
        for m in range(M):
            _remote(outs[m].at[c], outs[m].at[c], sems[0], sems[1], m, (x, y, 1 - c)).start()

    def finish(ins, outs, sems):
        x, y, c = _me()
        for m in range(M):
            _remote(outs[m].at[c], outs[m].at[c], sems[0], sems[1], m, (x, y, 1 - c)).wait_send()
            _remote(outs[m].at[1 - c], outs[m].at[1 - c], sems[0], sems[1], m, (x, y, 1 - c)).wait_recv()

    return _Carry(bufs, [jax.ShapeDtypeStruct(b.shape, b.dtype) for b in bufs], {m: m for m in range(M)},
                  [pltpu.SemaphoreType.DMA((M,)), pltpu.SemaphoreType.DMA((M,))], start, finish)


def _allreduce_small(v, name):
    R, W = v.shape
    Rh = R // 2

    def body(v_ref, o_ref, sib, quad, send_sems, recv_sems):
        x, y, c = _me()
        q = 2 * x + y
        sibling = (x, y, 1 - c)
        pair = _remote(v_ref, sib, send_sems, recv_sems, 0, sibling)
        pair.start()
        pair.wait()
        mine = pl.ds(pl.multiple_of(c * Rh, SUBLANES), Rh)
        quad[0] = v_ref[mine, :] + sib[mine, :]
        cps = []
        for k in (1, 2, 3):
            peer = (1 - x if k & 2 else x, 1 - y if k & 1 else y, c)
            cps.append(_remote(quad.at[0], quad.at[k], send_sems, recv_sems, k, peer))
            cps[-1].start()
        for cp in cps:
            cp.wait()
        acc = quad[q]
        for p in (1, 2, 3):
            acc = acc + quad[jnp.bitwise_xor(q, p)]
        o_ref[mine, :] = acc
        theirs = pl.ds(pl.multiple_of((1 - c) * Rh, SUBLANES), Rh)
        done = _remote(o_ref.at[mine, :], o_ref.at[mine, :], send_sems, recv_sems, 4, sibling)
        done.start()
        done.wait_send()
        _remote(o_ref.at[theirs, :], o_ref.at[theirs, :], send_sems, recv_sems, 4, sibling).wait_recv()

    vm = pl.BlockSpec(memory_space=pltpu.VMEM)
    return pl.pallas_call(
        body, name=name, in_specs=[vm], out_specs=vm, out_shape=jax.ShapeDtypeStruct((R, W), F32),
        scratch_shapes=[pltpu.VMEM((R, W), F32), pltpu.VMEM((4, Rh, W), F32),
                        pltpu.SemaphoreType.DMA((5,)), pltpu.SemaphoreType.DMA((5,))],
        compiler_params=pltpu.CompilerParams(vmem_limit_bytes=VMEM_LIMIT),
    )(v)


def _pack(pieces):
    flat = []
    for p in pieces:
        p = p.reshape(-1).astype(F32)
        pad = (-p.shape[0]) % PACK_ALIGN
        flat.append(jnp.pad(p, (0, pad)).reshape(-1, PACK_W))
    if sum(f.shape[0] for f in flat) % (2 * SUBLANES):
        flat.append(jnp.zeros((SUBLANES, PACK_W), F32))
    return jnp.concatenate(flat, axis=0)


def _unpack(packed, shapes):
    out, row = [], 0
    for shp in shapes:
        n = math.prod(shp)
        rows = -(-n // PACK_ALIGN) * SUBLANES
        out.append(packed[row:row + rows].reshape(-1)[:n].reshape(shp))
        row += rows
    return out


def kernel(x, ffn1_norm, ffn1_w_gate, ffn1_w_up, ffn1_w_down, mix_norm, w_in, lru_conv_w, lru_conv_b, lru_w_a, lru_b_a, lru_w_i, lru_b_i, lru_lambda, sc_conv_w, lru_out_norm, sc_out_norm, w_out, ffn2_norm, ffn2_w_gate, ffn2_w_up, ffn2_w_down, final_norm, loss_target, m_ffn1_norm, m_ffn1_w_gate, m_ffn1_w_up, m_ffn1_w_down, m_mix_norm, m_w_in, m_lru_conv_w, m_lru_conv_b, m_lru_w_a, m_lru_b_a, m_lru_w_i, m_lru_b_i, m_lru_lambda, m_sc_conv_w, m_lru_out_norm, m_sc_out_norm, m_w_out, m_ffn2_norm, m_ffn2_w_gate, m_ffn2_w_up, m_ffn2_w_down, m_final_norm, v_ffn1_norm, v_ffn1_w_gate, v_ffn1_w_up, v_ffn1_w_down, v_mix_norm, v_w_in, v_lru_conv_w, v_lru_conv_b, v_lru_w_a, v_lru_b_a, v_lru_w_i, v_lru_b_i, v_lru_lambda, v_sc_conv_w, v_lru_out_norm, v_sc_out_norm, v_w_out, v_ffn2_norm, v_ffn2_w_gate, v_ffn2_w_up, v_ffn2_w_down, v_final_norm):
    vals = locals()
    w = {n: vals[n] for n in WEIGHTS}
    mom = {n: vals["m_" + n] for n in WEIGHTS}
    var = {n: vals["v_" + n] for n in WEIGHTS}

    xi, yi, ci = _me()
    qi = 2 * xi + yi
    c_arr = jnp.reshape(ci, (1,)).astype(jnp.int32)
    qc_arr = jnp.stack([qi, ci]).astype(jnp.int32)

    T, D = x.shape[1], x.shape[2]
    xt = x.reshape(T, D)
    target = loss_target.reshape(T, D)
    DL = lru_conv_b.shape[-1]
    NH, HD = lru_w_a.shape[1], lru_w_a.shape[2]
    KL, KS = lru_conv_w.shape[1], sc_conv_w.shape[1]
    DLq = lru_conv_w.shape[2]

    axis_of = dict(zip(BIG, BIG_AXIS))
    first_names = ['ffn1_w_gate', 'ffn1_w_up', 'ffn1_w_down']
    later_names = ['w_in', 'w_out', 'ffn2_w_gate', 'ffn2_w_up', 'ffn2_w_down']

    def unview(n, g):
        return g.reshape(2 * g.shape[1], g.shape[2]) if axis_of[n] == 1 else g.reshape(8 * g.shape[1], g.shape[2])

    placed = {n: _cast_into_full(w[n][0], qc_arr, axis_of[n], "cast_" + n) for n in BIG}
    gathered = _run_carry(_gather_carry([placed[n] for n in first_names], [axis_of[n] for n in first_names]),
                          "gather_ffn1_weights")
    full = {n: unview(n, g) for n, g in zip(first_names, gathered)}
    gather_later = _gather_carry([placed[n] for n in later_names], [axis_of[n] for n in later_names])

    taps = jnp.zeros((2 * SUBLANES, DL), F32)
    taps = lax.dynamic_update_slice(taps, lru_conv_w[0], (0, qi * DLq))
    taps = lax.dynamic_update_slice(taps, sc_conv_w[0], (KL, qi * DLq))
    taps = _allreduce_small(jnp.where(ci == 0, taps, 0.0), "gather_conv_taps")
    cw, sw = taps[0:KL], taps[KL:KL + KS]

    cb = lru_conv_b
    wa, wi = lru_w_a[0].astype(BF16), lru_w_i[0].astype(BF16)
    ba, bi = lru_b_a.reshape(1, DL), lru_b_i.reshape(1, DL)
    mix_args = (cw, cb, wa, ba, wi, bi, lru_lambda, sw, lru_out_norm, sc_out_norm)
    gf = final_norm.reshape(1, D)

    res = _ffn_fwd(xt, ffn1_norm, full['ffn1_w_gate'], full['ffn1_w_up'], full['ffn1_w_down'], "ffn1_fwd", gather_later)
    x1, n1, G1, U1 = res[:4]
    full.update({n: unview(n, g) for n, g in zip(later_names, res[4:])})
    n2, z = _norm_mm(x1, mix_norm, full['w_in'], "mix_in_proj")
    h, ymix = _mix_fwd(z, *mix_args, "mix_fwd")
    x2 = _mm_fullk(ymix, full['w_out'], False, x1, F32, "mix_out_proj")[0]
    x3, n3, G2, U2 = _ffn_fwd(x2, ffn2_norm, full['ffn2_w_gate'], full['ffn2_w_up'], full['ffn2_w_down'], "ffn2_fwd")
    dx3, d3b, sqerr, dgf = _loss_head(x3, gf, target, "loss_head")

    grads, halves, shared = {}, {}, {}

    def pair_carry(names):
        parts = []
        for n in names:
            R, C = grads[n].shape
            parts.append(grads[n].reshape(2, R // 2, C) if axis_of[n] == 1 else grads[n].reshape(8, R // 8, C))
        return parts, _pair_exchange_carry(parts)

    def pair_add(names, parts, recv):
        return [_pair_add(p, r, c_arr, "grad_pair_add_" + n) for n, p, r in zip(names, parts, recv)]

    def chip_carry(names, sums):
        return _chip_exchange_carry(sums, [axis_of[n] for n in names])

    def chip_add(names, sums, recv):
        for n, s, r in zip(names, sums, recv):
            halves[n] = _quad_add(s, r, qc_arr, axis_of[n], "grad_chip_add_" + n)

    dG2, dU2, H2, dg_ffn2, dx2, dx2b = _ffn_bwd_fused(
        d3b, G2, U2, full['ffn2_w_gate'], full['ffn2_w_up'], full['ffn2_w_down'], x2, ffn2_norm, dx3, 1.0, "ffn2_bwd")
    grads['ffn2_w_gate'] = _mm_tn(n3, dG2, "ffn2_dwg")[0]
    grads['ffn2_w_up'] = _mm_tn(n3, dU2, "ffn2_dwu")[0]
    grads['ffn2_w_down'] = _mm_tn(H2, d3b, "ffn2_dwd")[0]
    names_a = ['ffn2_w_gate', 'ffn2_w_up', 'ffn2_w_down', 'w_out', 'w_in']
    parts, carry = pair_carry(names_a[:3])
    res = _mm_fullk(dx2b, full['w_out'], True, None, BF16, "mix_out_bwd", carry)
    dy = res[0]
    sums_a = pair_add(names_a[:3], parts, res[1:])
    grads['w_out'] = _mm_tn(ymix, dx2b, "mix_dwout")[0]
    dz, dwa, dwi, vec = _mix_bwd(z, h, dy, *mix_args, "mix_bwd")
    grads['w_in'] = _mm_tn(n2, dz, "mix_dwin")[0]
    parts, carry = pair_carry(names_a[3:])
    res = _mm_nt_norm_bwd(dz, full['w_in'], x1, mix_norm, dx2, FFN_RESIDUAL_SCALE, "mix_in_bwd", carry)
    dg_mix, dx1, d1b = res[:3]
    sums_a += pair_add(names_a[3:], parts, res[3:])

    res = _ffn_bwd_act(d1b, G1, U1, full['ffn1_w_gate'], full['ffn1_w_up'], full['ffn1_w_down'], "ffn1_bwd",
                       chip_carry(names_a, sums_a))
    dG1, dU1, H1, dn1 = res[:4]
    chip_add(names_a, sums_a, res[4:])
    dx0, _, dg_ffn1 = _rms_bwd_res(dn1, xt, ffn1_norm, dx1, 1.0, "ffn1_norm_bwd")
    res = _mm_tn(n1, dG1, "ffn1_dwg", _pair_share_carry([halves[n] for n in names_a]))
    grads['ffn1_w_gate'] = res[0]
    shared.update(zip(names_a, res[1:]))
    names_b = ['ffn1_w_gate', 'ffn1_w_up']
    parts_g, carry = pair_carry(names_b[:1])
    res = _mm_tn(n1, dU1, "ffn1_dwu", carry)
    grads['ffn1_w_up'] = res[0]
    sums_b = pair_add(names_b[:1], parts_g, res[1:])
    parts_u, carry = pair_carry(names_b[1:])
    sums_b += pair_add(names_b[1:], parts_u, _run_carry(carry, "grad_pair_exchange_ffn1_up"))
    res = _mm_tn(H1, d1b, "ffn1_dwd", chip_carry(names_b, sums_b))
    grads['ffn1_w_down'] = res[0]
    chip_add(names_b, sums_b, res[1:])
    names_c = ['ffn1_w_down']
    parts, carry = pair_carry(names_c)
    sums_c = pair_add(names_c, parts, _run_carry(carry, "grad_pair_exchange_ffn1_out"))
    chip_add(names_c, sums_c, _run_carry(chip_carry(names_c, sums_c), "grad_chip_exchange_ffn1_out"))
    names_bc = names_b + names_c
    shared.update(zip(names_bc, _run_carry(_pair_share_carry([halves[n] for n in names_bc]), "grad_pair_share")))
    out_g, out_d, out_m, out_v = {}, {}, {}, {}
    for n in BIG:
        shp = w[n].shape
        outs = _adamw(w[n][0], shared[n].reshape(shp[1], shp[2]), mom[n][0], var[n][0], "adamw_" + n)
        out_d[n], out_m[n], out_v[n], out_g[n] = (a.reshape(shp) for a in outs)

    small = [n for n in WEIGHTS if n not in BIG]
    local_small = {
        'ffn1_norm': dg_ffn1, 'mix_norm': dg_mix, 'lru_conv_w': vec[V_CW:V_CW + KL], 'lru_conv_b': vec[V_CB],
        'lru_w_a': dwa, 'lru_b_a': vec[V_BA], 'lru_w_i': dwi, 'lru_b_i': vec[V_BI], 'lru_lambda': vec[V_LAM],
        'sc_conv_w': vec[V_SW:V_SW + KS], 'lru_out_norm': vec[V_GLO], 'sc_out_norm': vec[V_GSO],
        'ffn2_norm': dg_ffn2, 'final_norm': dgf,
    }
    full_shapes = [local_small[n].shape for n in small] + [(1,)]
    reduced = _allreduce_small(_pack([local_small[n] for n in small] + [sqerr[0, 0:1]]), "allreduce_small")
    reduced = _unpack(reduced, full_shapes)
    loss = (0.5 / D) * reduced[-1][0]
    gsm = {}
    for n, g in zip(small, reduced[:-1]):
        if n in SMALL_SHARDED:
            g = lax.dynamic_slice(g, (0, qi * DLq), (g.shape[0], DLq))
        gsm[n] = g.reshape(w[n].shape)
    small_shapes = [w[n].shape for n in small]
    d_s, m_s, v_s, _ = _adamw(_pack([w[n] for n in small]), _pack([gsm[n] for n in small]),
                              _pack([mom[n] for n in small]), _pack([var[n] for n in small]), "adamw_small")
    for n, d, mn, vn in zip(small, _unpack(d_s, small_shapes), _unpack(m_s, small_shapes), _unpack(v_s, small_shapes)):
        out_g[n], out_d[n], out_m[n], out_v[n] = gsm[n], d, mn, vn

    return (loss, dx0.reshape(x.shape), *[out_g[n] for n in WEIGHTS], *[out_d[n] for n in WEIGHTS],
            *[out_m[n] for n in WEIGHTS], *[out_v[n] for n in WEIGHTS])
```

```python
import math

import jax
import jax.numpy as jnp
from jax import lax
from jax.experimental import pallas as pl
from jax.experimental.pallas import tpu as pltpu

F32 = jnp.float32
BF16 = jnp.bfloat16
MESH = pl.DeviceIdType.MESH
ANY = pl.BlockSpec(memory_space=pl.ANY)

NORM_EPS = 1e-6
LRU_C = 8.0
FFN_RESIDUAL_SCALE = 0.5
ADAM_LR = 0.001
ADAM_B1 = 0.9
ADAM_B2 = 0.999
ADAM_EPS = 1e-08
ADAM_WD = 0.01
ADAM_STEP = 10

V7X_VMEM_BYTES = 64 * 2**20
VMEM_LIMIT = V7X_VMEM_BYTES - 8 * 2**20
LANES = 128
SUBLANES = 8
PACK_W = LANES
PACK_ALIGN = SUBLANES * PACK_W

WEIGHTS = ['ffn1_norm', 'ffn1_w_gate', 'ffn1_w_up', 'ffn1_w_down', 'mix_norm', 'w_in', 'lru_conv_w', 'lru_conv_b',
           'lru_w_a', 'lru_b_a', 'lru_w_i', 'lru_b_i', 'lru_lambda', 'sc_conv_w', 'lru_out_norm', 'sc_out_norm',
           'w_out', 'ffn2_norm', 'ffn2_w_gate', 'ffn2_w_up', 'ffn2_w_down', 'final_norm']
BIG = ['ffn1_w_gate', 'ffn1_w_up', 'ffn1_w_down', 'w_in', 'w_out', 'ffn2_w_gate', 'ffn2_w_up', 'ffn2_w_down']
BIG_AXIS = [1, 1, 0, 1, 0, 1, 1, 0]
SMALL_SHARDED = ['lru_conv_w', 'sc_conv_w']


def _tile(n, pref, mult):
    if n <= pref:
        return n
    t = (pref // mult) * mult
    while t >= mult:
        if n % t == 0:
            return t
        t -= mult
    return n


def _params(*sem):
    return pltpu.CompilerParams(dimension_semantics=sem, vmem_limit_bytes=VMEM_LIMIT)


def _me():
    return lax.axis_index("x"), lax.axis_index("y"), lax.axis_index("c")


def _sigmoid(v):
    return 1.0 / (1.0 + jnp.exp(-v))


def _rstd(v):
    return lax.rsqrt(jnp.mean(v * v, axis=-1, keepdims=True) + NORM_EPS)


def _rms_bwd(dy, v, gain):
    r = _rstd(v)
    w = gain * dy
    dv = r * w - v * (r * r * r) * jnp.mean(v * w, axis=-1, keepdims=True)
    dgain = jnp.sum(dy * v * r, axis=0, keepdims=True)
    return dv, dgain


def _dot_nt(a, b):
    return lax.dot_general(a, b, (((1,), (1,)), ((), ())), preferred_element_type=F32)


def _dot_tn(a, b):
    return lax.dot_general(a, b, (((0,), (0,)), ((), ())), preferred_element_type=F32)


class _Carry:
    def __init__(self, inputs, out_shape, aliases, sems, start, finish, middle=None, middle_at=0.85):
        self.inputs, self.out_shape, self.aliases, self.sems = list(inputs), list(out_shape), dict(aliases), list(sems)
        self.start, self.finish = start, finish
        self.middle, self.middle_at = middle, middle_at


def _call(body, name, grid, in_specs, out_specs, out_shape, scratch_shapes, semantics, args, carry=None):
    if carry is None:
        return pl.pallas_call(body, name=name, grid=grid, in_specs=in_specs, out_specs=out_specs, out_shape=out_shape,
                              scratch_shapes=scratch_shapes, compiler_params=_params(*semantics))(*args)
    ni, no, ns = len(in_specs), len(out_specs), len(scratch_shapes)
    ci, co = len(carry.inputs), len(carry.out_shape)

    def carrying(*refs):
        ins, refs = refs[:ni], refs[ni:]
        cins, refs = refs[:ci], refs[ci:]
        outs, refs = refs[:no], refs[no:]
        couts, refs = refs[:co], refs[co:]
        scratch, csems = refs[:ns], refs[ns:]
        step = pl.program_id(0)
        for ax in range(1, len(grid)):
            step = step * grid[ax] + pl.program_id(ax)
        steps = math.prod(grid)
        first = step == 0
        last = step == steps - 1

        @pl.when(first)
        def _():
            carry.start(cins, couts, csems)

        if carry.middle is not None:
            @pl.when(step == min(int(carry.middle_at * steps), steps - 1))
            def _():
                carry.middle(cins, couts, csems)

        body(*ins, *outs, *scratch)

        @pl.when(last)
        def _():
            carry.finish(cins, couts, csems)

    return pl.pallas_call(
        carrying, name=name, grid=grid, in_specs=list(in_specs) + [ANY] * ci, out_specs=list(out_specs) + [ANY] * co,
        out_shape=list(out_shape) + carry.out_shape,
        input_output_aliases={ni + i: no + j for i, j in carry.aliases.items()},
        scratch_shapes=list(scratch_shapes) + carry.sems,
        compiler_params=_params(*(["arbitrary"] * len(grid))),
    )(*args, *carry.inputs)


def _run_carry(carry, name):
    ci, co = len(carry.inputs), len(carry.out_shape)

    def body(*refs):
        cins, couts, csems = refs[:ci], refs[ci:ci + co], refs[ci + co:]
        carry.start(cins, couts, csems)
        if carry.middle is not None:
            carry.middle(cins, couts, csems)
        carry.finish(cins, couts, csems)

    return pl.pallas_call(body, name=name, in_specs=[ANY] * ci, out_specs=[ANY] * co, out_shape=carry.out_shape,
                          input_output_aliases=carry.aliases, scratch_shapes=carry.sems)(*carry.inputs)


def _ffn_fwd(x, gain, wg, wu, wd, name, carry=None):
    T, D = x.shape
    FF = wg.shape[1]
    tm = _tile(T, 512, 16)
    tf = _tile(FF, 512, LANES)
    nf = FF // tf

    def body(x_ref, g_ref, wg_ref, wu_ref, wd_ref, xo_ref, n_ref, G_ref, U_ref, acc_ref):
        f = pl.program_id(1)

        @pl.when(f == 0)
        def _():
            xv = x_ref[...]
            n_ref[...] = (xv * _rstd(xv) * g_ref[...]).astype(BF16)
            acc_ref[...] = jnp.zeros_like(acc_ref)

        n = n_ref[...]
        G = jnp.dot(n, wg_ref[...], preferred_element_type=F32)
        U = jnp.dot(n, wu_ref[...], preferred_element_type=F32)
        G_ref[...] = G.astype(BF16)
        U_ref[...] = U.astype(BF16)
        H = (G * _sigmoid(G) * U).astype(BF16)
        acc_ref[...] += jnp.dot(H, wd_ref[...], preferred_element_type=F32)

        @pl.when(f == nf - 1)
        def _():
            xo_ref[...] = x_ref[...] + FFN_RESIDUAL_SCALE * acc_ref[...]

    return _call(
        body, name, (T // tm, nf),
        [pl.BlockSpec((tm, D), lambda i, f: (i, 0)),
         pl.BlockSpec((1, D), lambda i, f: (0, 0)),
         pl.BlockSpec((D, tf), lambda i, f: (0, f)),
         pl.BlockSpec((D, tf), lambda i, f: (0, f)),
         pl.BlockSpec((tf, D), lambda i, f: (f, 0))],
        [pl.BlockSpec((tm, D), lambda i, f: (i, 0)),
         pl.BlockSpec((tm, D), lambda i, f: (i, 0)),
         pl.BlockSpec((tm, tf), lambda i, f: (i, f)),
         pl.BlockSpec((tm, tf), lambda i, f: (i, f))],
        [jax.ShapeDtypeStruct((T, D), F32), jax.ShapeDtypeStruct((T, D), BF16),
         jax.ShapeDtypeStruct((T, FF), BF16), jax.ShapeDtypeStruct((T, FF), BF16)],
        [pltpu.VMEM((tm, D), F32)], ("parallel", "arbitrary"), (x, gain, wg, wu, wd), carry)


def _ffn_bwd_act(db, G, U, wg, wu, wd, name, carry=None):
    T, D = db.shape
    FF = wg.shape[1]
    tm = _tile(T, 512, 16)
    tf = _tile(FF, 512, LANES)

    def body(d_ref, G_ref, U_ref, wg_ref, wu_ref, wd_ref, dG_ref, dU_ref, H_ref, dn_ref):
        f = pl.program_id(1)
        dH = _dot_nt(d_ref[...], wd_ref[...])
        Gv = G_ref[...].astype(F32)
        Uv = U_ref[...].astype(F32)
        s = _sigmoid(Gv)
        sg = Gv * s
        H_ref[...] = (sg * Uv).astype(BF16)
        dU = (dH * sg).astype(BF16)
        dG = (dH * Uv * (s * (1.0 + Gv * (1.0 - s)))).astype(BF16)
        dG_ref[...] = dG
        dU_ref[...] = dU
        contrib = _dot_nt(dG, wg_ref[...]) + _dot_nt(dU, wu_ref[...])

        @pl.when(f == 0)
        def _():
            dn_ref[...] = contrib

        @pl.when(f > 0)
        def _():
            dn_ref[...] += contrib

    return _call(
        body, name, (T // tm, FF // tf),
        [pl.BlockSpec((tm, D), lambda i, f: (i, 0)),
         pl.BlockSpec((tm, tf), lambda i, f: (i, f)),
         pl.BlockSpec((tm, tf), lambda i, f: (i, f)),
         pl.BlockSpec((D, tf), lambda i, f: (0, f)),
         pl.BlockSpec((D, tf), lambda i, f: (0, f)),
         pl.BlockSpec((tf, D), lambda i, f: (f, 0))],
        [pl.BlockSpec((tm, tf), lambda i, f: (i, f)),
         pl.BlockSpec((tm, tf), lambda i, f: (i, f)),
         pl.BlockSpec((tm, tf), lambda i, f: (i, f)),
         pl.BlockSpec((tm, D), lambda i, f: (i, 0))],
        [jax.ShapeDtypeStruct((T, FF), BF16), jax.ShapeDtypeStruct((T, FF), BF16),
         jax.ShapeDtypeStruct((T, FF), BF16), jax.ShapeDtypeStruct((T, D), F32)],
        [], ("parallel", "arbitrary"), (db, G, U, wg, wu, wd), carry)


TAIL_ROWS = 128


class _NormBwdTail:
    def __init__(self, T, D, tm, scale):
        self.T, self.D, self.tm, self.scale = T, D, tm, scale
        self.ni = T // tm
        self.scratch = [pltpu.VMEM((tm, D), F32), pltpu.VMEM((tm, D), F32), pltpu.VMEM((tm, D), F32),
                        pltpu.VMEM((tm, D), BF16), pltpu.SemaphoreType.DMA((4,))]
        self.out_shape = [jax.ShapeDtypeStruct((T, D), F32), jax.ShapeDtypeStruct((T, D), BF16)]

    def _rows(self, k):
        return pl.ds(pl.multiple_of(k * self.tm, self.tm), self.tm)

    def _loads(self, k, x_hbm, r_hbm, bufs):
        xbuf, rbuf, _, _, sems = bufs
        return [pltpu.make_async_copy(x_hbm.at[self._rows(k)], xbuf, sems.at[0]),
                pltpu.make_async_copy(r_hbm.at[self._rows(k)], rbuf, sems.at[1])]

    def _stores(self, k, dx_hbm, dxb_hbm, bufs):
        _, _, obuf, obbuf, sems = bufs
        return [pltpu.make_async_copy(obuf, dx_hbm.at[self._rows(k)], sems.at[2]),
                pltpu.make_async_copy(obbuf, dxb_hbm.at[self._rows(k)], sems.at[3])]

    def prefetch(self, i, x_hbm, r_hbm, bufs):
        for cp in self._loads(i, x_hbm, r_hbm, bufs):
            cp.start()

    def run(self, i, acc_ref, g_ref, x_hbm, r_hbm, dx_hbm, dxb_hbm, dg_ref, bufs):
        xbuf, rbuf, obuf, obbuf, _ = bufs
        for cp in self._loads(i, x_hbm, r_hbm, bufs):
            cp.wait()

        @pl.when(i > 0)
        def _():
            for cp in self._stores(i - 1, dx_hbm, dxb_hbm, bufs):
                cp.wait()

        dgain = None
        for r0 in range(0, self.tm, TAIL_ROWS):
            rs = slice(r0, min(r0 + TAIL_ROWS, self.tm))
            dv, dgr = _rms_bwd(acc_ref[rs, :], xbuf[rs, :], g_ref[...])
            dx = rbuf[rs, :] + dv
            obuf[rs, :] = dx
            obbuf[rs, :] = (self.scale * dx).astype(BF16)
            dgain = dgr if dgain is None else dgain + dgr
        for cp in self._stores(i, dx_hbm, dxb_hbm, bufs):
            cp.start()

        @pl.when(i == 0)
        def _():
            dg_ref[...] = dgain

        @pl.when(i > 0)
        def _():
            dg_ref[...] += dgain

        @pl.when(i == self.ni - 1)
        def _():
            for cp in self._stores(i, dx_hbm, dxb_hbm, bufs):
                cp.wait()


def _ffn_bwd_fused(db, G, U, wg, wu, wd, x_in, gain, dres, scale, name):
    T, D = db.shape
    FF = wg.shape[1]
    tm = _tile(T, 512, 16)
    tf = _tile(FF, 512, LANES)
    nf = FF // tf
    tail = _NormBwdTail(T, D, tm, scale)

    def body(d_ref, G_ref, U_ref, wg_ref, wu_ref, wd_ref, g_ref, x_hbm, r_hbm,
             dG_ref, dU_ref, H_ref, dg_ref, dx_hbm, dxb_hbm, acc_ref, *bufs):
        i, f = pl.program_id(0), pl.program_id(1)

        @pl.when(f == 0)
        def _():
            tail.prefetch(i, x_hbm, r_hbm, bufs)

        dH = _dot_nt(d_ref[...], wd_ref[...])
        Gv = G_ref[...].astype(F32)
        Uv = U_ref[...].astype(F32)
        s = _sigmoid(Gv)
        sg = Gv * s
        H_ref[...] = (sg * Uv).astype(BF16)
        dU = (dH * sg).astype(BF16)
        dG = (dH * Uv * (s * (1.0 + Gv * (1.0 - s)))).astype(BF16)
        dG_ref[...] = dG
        dU_ref[...] = dU
        contrib = _dot_nt(dG, wg_ref[...]) + _dot_nt(dU, wu_ref[...])

        @pl.when(f == 0)
        def _():
            acc_ref[...] = contrib

        @pl.when(f > 0)
        def _():
            acc_ref[...] += contrib

        @pl.when(f == nf - 1)
        def _():
            tail.run(i, acc_ref, g_ref, x_hbm, r_hbm, dx_hbm, dxb_hbm, dg_ref, bufs)

    act = pl.BlockSpec((tm, tf), lambda i, f: (i, f))
    return pl.pallas_call(
        body, name=name, grid=(T // tm, nf),
        in_specs=[pl.BlockSpec((tm, D), lambda i, f: (i, 0)), act, act,
                  pl.BlockSpec((D, tf), lambda i, f: (0, f)),
                  pl.BlockSpec((D, tf), lambda i, f: (0, f)),
                  pl.BlockSpec((tf, D), lambda i, f: (f, 0)),
                  pl.BlockSpec((1, D), lambda i, f: (0, 0)), ANY, ANY],
        out_specs=[act, act, act, pl.BlockSpec((1, D), lambda i, f: (0, 0)), ANY, ANY],
        out_shape=[jax.ShapeDtypeStruct((T, FF), BF16)] * 3 + [jax.ShapeDtypeStruct((1, D), F32)] + tail.out_shape,
        scratch_shapes=[pltpu.VMEM((tm, D), F32)] + tail.scratch,
        compiler_params=_params("arbitrary", "arbitrary"),
    )(db, G, U, wg, wu, wd, gain, x_in, dres)


def _mm_nt_norm_bwd(a, w, x_in, gain, dres, scale, name, carry=None):
    T, K = a.shape
    D = w.shape[0]
    tm = _tile(T, 512, 16)
    tk = _tile(K, 1280, LANES)
    nk = K // tk
    tail = _NormBwdTail(T, D, tm, scale)

    def body(a_ref, w_ref, g_ref, x_hbm, r_hbm, dg_ref, dx_hbm, dxb_hbm, acc_ref, *bufs):
        i, k = pl.program_id(0), pl.program_id(1)

        @pl.when(k == 0)
        def _():
            tail.prefetch(i, x_hbm, r_hbm, bufs)

        contrib = _dot_nt(a_ref[...], w_ref[...])

        @pl.when(k == 0)
        def _():
            acc_ref[...] = contrib

        @pl.when(k > 0)
        def _():
            acc_ref[...] += contrib

        @pl.when(k == nk - 1)
        def _():
            tail.run(i, acc_ref, g_ref, x_hbm, r_hbm, dx_hbm, dxb_hbm, dg_ref, bufs)

    return _call(
        body, name, (T // tm, nk),
        [pl.BlockSpec((tm, tk), lambda i, k: (i, k)), pl.BlockSpec((D, tk), lambda i, k: (0, k)),
         pl.BlockSpec((1, D), lambda i, k: (0, 0)), ANY, ANY],
        [pl.BlockSpec((1, D), lambda i, k: (0, 0)), ANY, ANY],
        [jax.ShapeDtypeStruct((1, D), F32)] + tail.out_shape,
        [pltpu.VMEM((tm, D), F32)] + tail.scratch, ("arbitrary", "arbitrary"), (a, w, gain, x_in, dres), carry)


def _rms_bwd_res(dn, x, gain, dres, scale, name, carry=None):
    T, D = x.shape
    tm = _tile(T, 256, 16)

    def body(dn_ref, x_ref, g_ref, dr_ref, dx_ref, dxb_ref, dg_ref):
        i = pl.program_id(0)
        dv, dgain = _rms_bwd(dn_ref[...], x_ref[...], g_ref[...])
        dx = dr_ref[...] + dv
        dx_ref[...] = dx
        dxb_ref[...] = (scale * dx).astype(BF16)

        @pl.when(i == 0)
        def _():
            dg_ref[...] = dgain

        @pl.when(i > 0)
        def _():
            dg_ref[...] += dgain

    row = pl.BlockSpec((tm, D), lambda i: (i, 0))
    vec = pl.BlockSpec((1, D), lambda i: (0, 0))
    return _call(
        body, name, (T // tm,), [row, row, vec, row], [row, row, vec],
        [jax.ShapeDtypeStruct((T, D), F32), jax.ShapeDtypeStruct((T, D), BF16), jax.ShapeDtypeStruct((1, D), F32)],
        [], ("arbitrary",), (dn, x, gain, dres), carry)


def _loss_head(x3, gain, target, name):
    T, D = x3.shape
    tm = _tile(T, 256, 16)

    def body(x_ref, g_ref, t_ref, dx_ref, dxb_ref, ls_ref, dg_ref):
        i = pl.program_id(0)
        xv = x_ref[...]
        err = xv * _rstd(xv) * g_ref[...] - t_ref[...]
        sq = jnp.sum(jnp.sum(err * err, axis=1, keepdims=True), axis=0, keepdims=True)
        dv, dgain = _rms_bwd(err * (1.0 / D), xv, g_ref[...])
        dx_ref[...] = dv
        dxb_ref[...] = (FFN_RESIDUAL_SCALE * dv).astype(BF16)
        sqb = jnp.broadcast_to(sq, (1, LANES))

        @pl.when(i == 0)
        def _():
            dg_ref[...] = dgain
            ls_ref[...] = sqb

        @pl.when(i > 0)
        def _():
            dg_ref[...] += dgain
            ls_ref[...] += sqb

    row = pl.BlockSpec((tm, D), lambda i: (i, 0))
    vec = pl.BlockSpec((1, D), lambda i: (0, 0))
    return pl.pallas_call(
        body, name=name, grid=(T // tm,),
        in_specs=[row, vec, row],
        out_specs=[row, row, pl.BlockSpec((1, LANES), lambda i: (0, 0)), vec],
        out_shape=[jax.ShapeDtypeStruct((T, D), F32), jax.ShapeDtypeStruct((T, D), BF16),
                   jax.ShapeDtypeStruct((1, LANES), F32), jax.ShapeDtypeStruct((1, D), F32)],
        compiler_params=_params("arbitrary"),
    )(x3, gain, target)


def _norm_mm(x, gain, w, name):
    T, D = x.shape
    N = w.shape[1]
    tm = _tile(T, 512, 16)
    tn = _tile(N, 2560, LANES)

    def body(x_ref, g_ref, w_ref, n_ref, z_ref):
        @pl.when(pl.program_id(1) == 0)
        def _():
            xv = x_ref[...]
            n_ref[...] = (xv * _rstd(xv) * g_ref[...]).astype(BF16)

        z_ref[...] = jnp.dot(n_ref[...], w_ref[...], preferred_element_type=F32).astype(BF16)

    return pl.pallas_call(
        body, name=name, grid=(T // tm, N // tn),
        in_specs=[pl.BlockSpec((tm, D), lambda i, j: (i, 0)),
                  pl.BlockSpec((1, D), lambda i, j: (0, 0)),
                  pl.BlockSpec((D, tn), lambda i, j: (0, j))],
        out_specs=[pl.BlockSpec((tm, D), lambda i, j: (i, 0)),
                   pl.BlockSpec((tm, tn), lambda i, j: (i, j))],
        out_shape=[jax.ShapeDtypeStruct((T, D), BF16), jax.ShapeDtypeStruct((T, N), BF16)],
        compiler_params=_params("parallel", "arbitrary"),
    )(x, gain, w)


def _mm_fullk(a, w, trans_w, residual, out_dtype, name, carry=None):
    T, K = a.shape
    N = w.shape[0] if trans_w else w.shape[1]
    tm = _tile(T, 512, 16)
    tn = _tile(N, 2048 * 2560 // K, LANES)

    def body(*refs):
        if residual is None:
            a_ref, w_ref, o_ref = refs
        else:
            a_ref, w_ref, r_ref, o_ref = refs
        if trans_w:
            acc = _dot_nt(a_ref[...], w_ref[...])
        else:
            acc = jnp.dot(a_ref[...], w_ref[...], preferred_element_type=F32)
        if residual is not None:
            acc = acc + r_ref[...]
        o_ref[...] = acc.astype(out_dtype)

    w_spec = pl.BlockSpec((tn, K), lambda i, j: (j, 0)) if trans_w else pl.BlockSpec((K, tn), lambda i, j: (0, j))
    in_specs = [pl.BlockSpec((tm, K), lambda i, j: (i, 0)), w_spec]
    args = [a, w]
    if residual is not None:
        in_specs.append(pl.BlockSpec((tm, tn), lambda i, j: (i, j)))
        args.append(residual)
    return _call(body, name, (T // tm, N // tn), in_specs, [pl.BlockSpec((tm, tn), lambda i, j: (i, j))],
                 [jax.ShapeDtypeStruct((T, N), out_dtype)], [], ("parallel", "arbitrary"), args, carry)


def _mm_tn(a, b, name, carry=None):
    T, M = a.shape
    N = b.shape[1]
    tmw = _tile(M, 2048, LANES)
    tnw = _tile(N, 2048 * 1408 // tmw, LANES)
    tk = _tile(T, 512, 16)
    nk = T // tk

    def body(a_ref, b_ref, o_ref, acc_ref):
        k = pl.program_id(2)

        @pl.when(k == 0)
        def _():
            acc_ref[...] = jnp.zeros_like(acc_ref)

        acc_ref[...] += _dot_tn(a_ref[...], b_ref[...])

        @pl.when(k == nk - 1)
        def _():
            o_ref[...] = acc_ref[...].astype(BF16)

    return _call(
        body, name, (M // tmw, N // tnw, nk),
        [pl.BlockSpec((tk, tmw), lambda i, j, k: (k, i)),
         pl.BlockSpec((tk, tnw), lambda i, j, k: (k, j))],
        [pl.BlockSpec((tmw, tnw), lambda i, j, k: (i, j))],
        [jax.ShapeDtypeStruct((M, N), BF16)],
        [pltpu.VMEM((tmw, tnw), F32)], ("parallel", "parallel", "arbitrary"), (a, b), carry)


GELU_K = math.sqrt(2.0 / math.pi)
GELU_C = 0.044715


def _gelu_and_grad(v):
    u = GELU_K * (v + GELU_C * v * v * v)
    th = jnp.tanh(u)
    g = 0.5 * v * (1.0 + th)
    dg = 0.5 * (1.0 + th) + 0.5 * v * (1.0 - th * th) * GELU_K * (1.0 + 3.0 * GELU_C * v * v)
    return g, dg


def _neg_expm1(v):
    poly = v * (1.0 + v * (0.5 + v * (1.0 / 6 + v * (1.0 / 24 + v * (1.0 / 120 + v * (1.0 / 720))))))
    return jnp.where(v > -0.25, -poly, 1.0 - jnp.exp(v))


def _softplus_neg(lam):
    e = jnp.exp(-jnp.abs(lam))
    log1pe = jnp.where(e < 1e-4, e * (1.0 - 0.5 * e), jnp.log(1.0 + e))
    sp = jnp.maximum(-lam, 0.0) + log1pe
    dsp = -1.0 / (1.0 + jnp.exp(lam))
    return sp, dsp


def _earlier(ext, j):
    return pltpu.roll(ext, j, 0)[SUBLANES:, :]


def _later(ext, j):
    n = ext.shape[0]
    return pltpu.roll(ext, n - j, 0)[:n - SUBLANES, :]


def _taps(v, halo, K):
    ext = jnp.concatenate([halo, v], axis=0)
    return [v] + [_earlier(ext, j) for j in range(1, K)]


def _block_diag(vb, w_ref, nh, hd):
    return jnp.concatenate(
        [jnp.dot(vb[:, h * hd:(h + 1) * hd], w_ref[h], preferred_element_type=F32) for h in range(nh)], axis=1)


def _lru_gates(xc, wa_ref, ba_ref, wi_ref, bi_ref, sp, nh, hd):
    xcb = xc.astype(BF16)
    r = _sigmoid(_block_diag(xcb, wa_ref, nh, hd) + ba_ref[...])
    ig = _sigmoid(_block_diag(xcb, wi_ref, nh, hd) + bi_ref[...])
    log_a = -LRU_C * r * sp
    a = jnp.exp(log_a)
    mult = jnp.sqrt(_neg_expm1(2.0 * log_a))
    return xcb, r, ig, a, mult


def _mix_fwd(z, cw, cb, wa, ba, wi, bi, lam, sw, glo, gso, name):
    T = z.shape[0]
    DL = cb.shape[1]
    DS = gso.shape[1]
    NH, HD = wa.shape[0], wa.shape[1]
    KL, KS = cw.shape[0], sw.shape[0]
    tt = _tile(T, 128, 16)
    o_g, o_b, o_c, o_x = DL, 2 * DL, 2 * DL + DS, 2 * DL + 2 * DS

    def body(z_ref, cw_ref, cb_ref, wa_ref, ba_ref, wi_ref, bi_ref, lam_ref, sw_ref, glo_ref, gso_ref,
             h_ref, y_ref, cx_ref, cp_ref, ch_ref):
        @pl.when(pl.program_id(0) == 0)
        def _():
            cx_ref[...] = jnp.zeros_like(cx_ref)
            cp_ref[...] = jnp.zeros_like(cp_ref)
            ch_ref[...] = jnp.zeros_like(ch_ref)

        def zcol(o, n):
            return z_ref[:, o:o + n].astype(F32)

        lx = zcol(0, DL)
        xs = _taps(lx, cx_ref[...], KL)
        cx_ref[...] = lx[tt - SUBLANES:, :]
        xc = cb_ref[...] + xs[0] * cw_ref[KL - 1:KL, :]
        for j in range(1, KL):
            xc = xc + xs[j] * cw_ref[KL - 1 - j:KL - j, :]
        sp, _ = _softplus_neg(lam_ref[...])
        _, _, ig, a, mult = _lru_gates(xc, wa_ref, ba_ref, wi_ref, bi_ref, sp, NH, HD)
        b = mult * (ig * xc)
        rows = lax.broadcasted_iota(jnp.int32, (tt, DL), 0)
        s = 1
        while s < tt:
            keep = rows >= s
            b = jnp.where(keep, a * pltpu.roll(b, s, 0) + b, b)
            a = jnp.where(keep, a * pltpu.roll(a, s, 0), a)
            s *= 2
        h = a * ch_ref[SUBLANES - 1:SUBLANES, :] + b
        ch_ref[...] = h[tt - SUBLANES:, :]
        h_ref[...] = h
        ge, _ = _gelu_and_grad(zcol(o_g, DL))
        ylru = h * ge
        y_ref[:, 0:DL] = (ylru * _rstd(ylru) * glo_ref[...]).astype(BF16)

        p = zcol(o_c, DS) * zcol(o_x, DS)
        ps = _taps(p, cp_ref[...], KS)
        cp_ref[...] = p[tt - SUBLANES:, :]
        cv = ps[0] * sw_ref[KS - 1:KS, :]
        for j in range(1, KS):
            cv = cv + ps[j] * sw_ref[KS - 1 - j:KS - j, :]
        ysc = zcol(o_b, DS) * cv
        y_ref[:, DL:DL + DS] = (ysc * _rstd(ysc) * gso_ref[...]).astype(BF16)

    def full(shape):
        return pl.BlockSpec(shape, lambda t: (0,) * len(shape))

    return pl.pallas_call(
        body, name=name, grid=(T // tt,),
        in_specs=[pl.BlockSpec((tt, z.shape[1]), lambda t: (t, 0)),
                  full(cw.shape), full(cb.shape), full(wa.shape), full(ba.shape), full(wi.shape), full(bi.shape),
                  full(lam.shape), full(sw.shape), full(glo.shape), full(gso.shape)],
        out_specs=[pl.BlockSpec((tt, DL), lambda t: (t, 0)), pl.BlockSpec((tt, DL + DS), lambda t: (t, 0))],
        out_shape=[jax.ShapeDtypeStruct((T, DL), F32), jax.ShapeDtypeStruct((T, DL + DS), BF16)],
        scratch_shapes=[pltpu.VMEM((SUBLANES, DL), F32), pltpu.VMEM((SUBLANES, DS), F32),
                        pltpu.VMEM((SUBLANES, DL), F32)],
        compiler_params=_params("arbitrary"),
    )(z, cw, cb, wa, ba, wi, bi, lam, sw, glo, gso)


V_BA, V_BI, V_LAM, V_CB, V_CW, V_SW, V_GLO, V_GSO, V_ROWS = 0, 1, 2, 3, 4, 8, 11, 12, 16


def _mix_bwd(z, h, dy, cw, cb, wa, ba, wi, bi, lam, sw, glo, gso, name):
    T = z.shape[0]
    DL = cb.shape[1]
    DS = gso.shape[1]
    NH, HD = wa.shape[0], wa.shape[1]
    KL, KS = cw.shape[0], sw.shape[0]
    tt = _tile(T, 64, 16)
    nt = T // tt
    ZH = 2 * SUBLANES
    o_g, o_b, o_c, o_x = DL, 2 * DL, 2 * DL + DS, 2 * DL + 2 * DS

    def body(z_ref, zh_ref, h_ref, hh_ref, dy_ref, cw_ref, cb_ref, wa_ref, ba_ref, wi_ref, bi_ref, lam_ref,
             sw_ref, glo_ref, gso_ref, dz_ref, dwa_ref, dwi_ref, vec_ref, cdx_ref, cdc_ref, cdh_ref):
        i = pl.program_id(0)
        tr = nt - 1 - i

        @pl.when(i == 0)
        def _():
            dwa_ref[...] = jnp.zeros_like(dwa_ref)
            dwi_ref[...] = jnp.zeros_like(dwi_ref)
            vec_ref[...] = jnp.zeros_like(vec_ref)
            cdx_ref[...] = jnp.zeros_like(cdx_ref)
            cdc_ref[...] = jnp.zeros_like(cdc_ref)
            cdh_ref[...] = jnp.zeros_like(cdh_ref)

        def acc_row(r, v):
            vec_ref[pl.ds(r, 1), :] += jnp.sum(v, axis=0, keepdims=True)

        has_prev = tr > 0
        rows = lax.broadcasted_iota(jnp.int32, (tt, DL), 0)

        def zcol(o, n):
            return z_ref[:, o:o + n].astype(F32)

        def zhalo(o, n):
            return jnp.where(has_prev, zh_ref[:, o:o + n].astype(F32)[SUBLANES:, :], 0.0)

        lx = zcol(0, DL)
        xs = _taps(lx, zhalo(0, DL), KL)
        xc = cb_ref[...] + xs[0] * cw_ref[KL - 1:KL, :]
        for j in range(1, KL):
            xc = xc + xs[j] * cw_ref[KL - 1 - j:KL - j, :]
        sp, dsp = _softplus_neg(lam_ref[...])
        xcb, r, ig, a, mult = _lru_gates(xc, wa_ref, ba_ref, wi_ref, bi_ref, sp, NH, HD)
        hv = h_ref[...]
        hprev = _earlier(jnp.concatenate([jnp.where(has_prev, hh_ref[...], 0.0), hv], axis=0), 1)
        gate = zcol(o_g, DL)
        ge, dge = _gelu_and_grad(gate)
        ylru = hv * ge

        d_ylru, dglo = _rms_bwd(dy_ref[:, 0:DL].astype(F32), ylru, glo_ref[...])
        vec_ref[pl.ds(V_GLO, 1), :] += dglo
        dz_ref[:, o_g:o_g + DL] = (d_ylru * hv * dge).astype(BF16)
        bq = d_ylru * ge
        aq = jnp.where(rows == tt - 1, 1.0, pltpu.roll(a, tt - 1, 0))
        s = 1
        while s < tt:
            keep = rows < tt - s
            bq = jnp.where(keep, aq * pltpu.roll(bq, tt - s, 0) + bq, bq)
            aq = jnp.where(keep, aq * pltpu.roll(aq, tt - s, 0), aq)
            s *= 2
        dhh = bq + aq * cdh_ref[0:1, :]
        cdh_ref[0:1, :] = a[0:1, :] * dhh[0:1, :]

        da = dhh * hprev
        dmult = dhh * (ig * xc)
        d_i = dhh * mult * xc
        dxc = dhh * mult * ig
        dlog = da * a - dmult * (a * a) / mult
        acc_row(V_LAM, dlog * (-LRU_C * r) * dsp)
        dpa = dlog * (-LRU_C * sp) * r * (1.0 - r)
        dpi = d_i * ig * (1.0 - ig)
        acc_row(V_BA, dpa)
        acc_row(V_BI, dpi)
        dpab = dpa.astype(BF16)
        dpib = dpi.astype(BF16)
        back = []
        for hh in range(NH):
            sl = slice(hh * HD, (hh + 1) * HD)
            dwa_ref[hh] += _dot_tn(xcb[:, sl], dpab[:, sl])
            dwi_ref[hh] += _dot_tn(xcb[:, sl], dpib[:, sl])
            back.append(_dot_nt(dpab[:, sl], wa_ref[hh]) + _dot_nt(dpib[:, sl], wi_ref[hh]))
        dxc = dxc + jnp.concatenate(back, axis=1)

        acc_row(V_CB, dxc)
        extd = jnp.concatenate([dxc, cdx_ref[...]], axis=0)
        cdx_ref[...] = dxc[0:SUBLANES, :]
        dlx = dxc * cw_ref[KL - 1:KL, :]
        acc_row(V_CW + KL - 1, dxc * xs[0])
        for j in range(1, KL):
            dlx = dlx + _later(extd, j) * cw_ref[KL - 1 - j:KL - j, :]
            acc_row(V_CW + KL - 1 - j, dxc * xs[j])
        dz_ref[:, 0:DL] = dlx.astype(BF16)

        sb = zcol(o_b, DS)
        sc = zcol(o_c, DS)
        sx = zcol(o_x, DS)
        p = sc * sx
        ps = _taps(p, zhalo(o_c, DS) * zhalo(o_x, DS), KS)
        cv = ps[0] * sw_ref[KS - 1:KS, :]
        for j in range(1, KS):
            cv = cv + ps[j] * sw_ref[KS - 1 - j:KS - j, :]
        d_ysc, dgso = _rms_bwd(dy_ref[:, DL:DL + DS].astype(F32), sb * cv, gso_ref[...])
        vec_ref[pl.ds(V_GSO, 1), :] += dgso
        dz_ref[:, o_b:o_b + DS] = (d_ysc * cv).astype(BF16)
        dcv = d_ysc * sb
        extc = jnp.concatenate([dcv, cdc_ref[...]], axis=0)
        cdc_ref[...] = dcv[0:SUBLANES, :]
        dp = dcv * sw_ref[KS - 1:KS, :]
        acc_row(V_SW + KS - 1, dcv * ps[0])
        for j in range(1, KS):
            dp = dp + _later(extc, j) * sw_ref[KS - 1 - j:KS - j, :]
            acc_row(V_SW + KS - 1 - j, dcv * ps[j])
        dz_ref[:, o_c:o_c + DS] = (dp * sx).astype(BF16)
        dz_ref[:, o_x:o_x + DS] = (dp * sc).astype(BF16)

    def full(shape):
        return pl.BlockSpec(shape, lambda t: (0,) * len(shape))

    def rev(t):
        return nt - 1 - t

    def halo(t, rows):
        return jnp.maximum(rev(t) * (tt // rows) - 1, 0)

    return pl.pallas_call(
        body, name=name, grid=(nt,),
        in_specs=[pl.BlockSpec((tt, z.shape[1]), lambda t: (rev(t), 0)),
                  pl.BlockSpec((ZH, z.shape[1]), lambda t: (halo(t, ZH), 0)),
                  pl.BlockSpec((tt, DL), lambda t: (rev(t), 0)),
                  pl.BlockSpec((SUBLANES, DL), lambda t: (halo(t, SUBLANES), 0)),
                  pl.BlockSpec((tt, DL + DS), lambda t: (rev(t), 0)),
                  full(cw.shape), full(cb.shape), full(wa.shape), full(ba.shape), full(wi.shape), full(bi.shape),
                  full(lam.shape), full(sw.shape), full(glo.shape), full(gso.shape)],
        out_specs=[pl.BlockSpec((tt, z.shape[1]), lambda t: (rev(t), 0)),
                   full(wa.shape), full(wi.shape), full((V_ROWS, DL))],
        out_shape=[jax.ShapeDtypeStruct(z.shape, BF16), jax.ShapeDtypeStruct(wa.shape, F32),
                   jax.ShapeDtypeStruct(wi.shape, F32), jax.ShapeDtypeStruct((V_ROWS, DL), F32)],
        scratch_shapes=[pltpu.VMEM((SUBLANES, DL), F32), pltpu.VMEM((SUBLANES, DS), F32),
                        pltpu.VMEM((SUBLANES, DL), F32)],
        compiler_params=_params("arbitrary"),
    )(z, z, h, h, dy, cw, cb, wa, ba, wi, bi, lam, sw, glo, gso)


def _pair_add(p, r1, c, name):
    G, R, C = r1.shape
    tr = _tile(R, 256, 16)
    tc = _tile(C, 1408, LANES)

    def body(c_ref, p_ref, r_ref, o_ref):
        o_ref[...] = (p_ref[...].astype(F32) + r_ref[...].astype(F32)).astype(BF16)

    blk = (None, tr, tc)
    return pl.pallas_call(
        body, name=name,
        grid_spec=pltpu.PrefetchScalarGridSpec(
            num_scalar_prefetch=1, grid=(G, R // tr, C // tc),
            in_specs=[pl.BlockSpec(blk, lambda g, i, j, cr: (2 * g + cr[0], i, j)),
                      pl.BlockSpec(blk, lambda g, i, j, cr: (g, i, j))],
            out_specs=pl.BlockSpec(blk, lambda g, i, j, cr: (g, i, j))),
        out_shape=jax.ShapeDtypeStruct((G, R, C), BF16),
        compiler_params=_params("parallel", "parallel", "parallel"),
    )(c, p, r1)


def _quad_add(s, r2, qc, axis, name):
    _, R, W = r2.shape
    tr = _tile(R, 256, 16)

    def body(qc_ref, s_ref, r0_ref, r1_ref, r2_ref, o_ref):
        o_ref[...] = ((s_ref[...].astype(F32) + r0_ref[...].astype(F32)) + r1_ref[...].astype(F32)) + r2_ref[...].astype(F32)

    blk = (None, tr, W)
    if axis == 1:
        own = pl.BlockSpec(blk, lambda i, qr: (0, i, qr[0]))
    else:
        own = pl.BlockSpec(blk, lambda i, qr: (qr[0], i, 0))
    return pl.pallas_call(
        body, name=name,
        grid_spec=pltpu.PrefetchScalarGridSpec(
            num_scalar_prefetch=1, grid=(R // tr,),
            in_specs=[own] + [pl.BlockSpec(blk, lambda i, qr, j=j: (j, i, 0)) for j in range(3)],
            out_specs=pl.BlockSpec(blk, lambda i, qr: (qr[1], i, 0))),
        out_shape=jax.ShapeDtypeStruct((2, R, W), F32),
        compiler_params=_params("parallel"),
    )(qc, s, r2, r2, r2)


def _cast_into_full(shard, qc, axis, name):
    R, W = shard.shape
    Rh = R // 2
    tr = _tile(Rh, 256, 16)
    nb = Rh // tr

    def body(qc_ref, s_ref, o_ref):
        o_ref[...] = s_ref[...].astype(BF16)

    if axis == 1:
        out_shape = (2, Rh, 4 * W)
        out_spec = pl.BlockSpec((None, tr, W), lambda hf, i, qr: (hf, i, qr[0]))
    else:
        out_shape = (8, Rh, W)
        out_spec = pl.BlockSpec((None, tr, W), lambda hf, i, qr: (2 * qr[0] + hf, i, 0))
    return pl.pallas_call(
        body, name=name,
        grid_spec=pltpu.PrefetchScalarGridSpec(
            num_scalar_prefetch=1, grid=(2, nb),
            in_specs=[pl.BlockSpec((tr, W), lambda hf, i, qr: (hf * nb + i, 0))],
            out_specs=out_spec),
        out_shape=jax.ShapeDtypeStruct(out_shape, BF16),
        compiler_params=_params("parallel", "parallel"),
    )(qc, shard)


def _adamw(w, g, m, v, name):
    R, C = w.shape
    tr = _tile(R, 256, SUBLANES)
    tc = _tile(C, 2048, LANES)
    c1 = 1.0 - ADAM_B1 ** ADAM_STEP
    c2 = 1.0 - ADAM_B2 ** ADAM_STEP

    def body(w_ref, g_ref, m_ref, v_ref, d_ref, mo_ref, vo_ref, go_ref):
        gv = g_ref[...]
        go_ref[...] = gv
        mn = ADAM_B1 * m_ref[...] + (1.0 - ADAM_B1) * gv
        vn = ADAM_B2 * v_ref[...] + (1.0 - ADAM_B2) * (gv * gv)
        mo_ref[...] = mn
        vo_ref[...] = vn
        d_ref[...] = -ADAM_LR * ((mn / c1) / (jnp.sqrt(vn / c2) + ADAM_EPS) + ADAM_WD * w_ref[...])

    blk = pl.BlockSpec((tr, tc), lambda i, j: (i, j))
    sh = jax.ShapeDtypeStruct((R, C), F32)
    return pl.pallas_call(
        body, name=name, grid=(R // tr, C // tc),
        in_specs=[blk] * 4, out_specs=[blk] * 4, out_shape=[sh] * 4,
        compiler_params=_params("parallel", "parallel"),
    )(w, g, m, v)


def _other_chips(x, y):
    return [(1 - x, y), (x, 1 - y), (1 - x, 1 - y)]


def _remote(src, dst, send_sems, recv_sems, idx, dev):
    return pltpu.make_async_remote_copy(src_ref=src, dst_ref=dst, send_sem=send_sems.at[idx], recv_sem=recv_sems.at[idx],
                                        device_id=dev, device_id_type=MESH)


def _gather_carry(fulls, axes):
    M = len(fulls)

    def win(outs, m, qq, cc):
        if axes[m] == 1:
            W = fulls[m].shape[2] // 4
            return outs[m].at[cc, :, pl.ds(pl.multiple_of(qq * W, LANES), W)]
        return outs[m].at[2 * qq + cc]

    def ici(outs, sems, m, j, src_q):
        x, y, c = _me()
        cx, cy = _other_chips(x, y)[j]
        blk = win(outs, m, src_q, c)
        return _remote(blk, blk, sems[0], sems[1], 6 * m + j, (cx, cy, c))

    def d2d(outs, sems, m, j, half):
        x, y, c = _me()
        cx, cy = _other_chips(x, y)[j]
        blk = win(outs, m, 2 * cx + cy, half)
        return _remote(blk, blk, sems[0], sems[1], 6 * m + 3 + j, (x, y, 1 - c))

    def start(ins, outs, sems):
        x, y, c = _me()
        for m in range(M):
            for j in range(3):
                ici(outs, sems, m, j, 2 * x + y).start()

    def middle(ins, outs, sems):
        x, y, c = _me()
        for m in range(M):
            for j, (cx, cy) in enumerate(_other_chips(x, y)):
                ici(outs, sems, m, j, 2 * cx + cy).wait_recv()
                d2d(outs, sems, m, j, c).start()

    def finish(ins, outs, sems):
        x, y, c = _me()
        for m in range(M):
            for j in range(3):
                d2d(outs, sems, m, j, 1 - c).wait_recv()
        for m in range(M):
            for j in range(3):
                ici(outs, sems, m, j, 2 * x + y).wait_send()
                d2d(outs, sems, m, j, c).wait_send()

    return _Carry(fulls, [jax.ShapeDtypeStruct(f.shape, f.dtype) for f in fulls], {m: m for m in range(M)},
                  [pltpu.SemaphoreType.DMA((6 * M,)), pltpu.SemaphoreType.DMA((6 * M,))], start, finish, middle)


def _gather_two_way_carry(fulls, axes):
    M = len(fulls)

    def part(outs, m, qq, cc, p):
        Rp = fulls[m].shape[1] // 2
        rows = pl.ds(p * Rp, Rp)
        if axes[m] == 1:
            W = fulls[m].shape[2] // 4
            return outs[m].at[cc, rows, pl.ds(pl.multiple_of(qq * W, LANES), W)]
        return outs[m].at[2 * qq + cc, rows, :]

    def copies(outs, sems):
        x, y, c = _me()
        q, qx, qy, qd = 2 * x + y, 2 * (1 - x) + y, 2 * x + (1 - y), 2 * (1 - x) + (1 - y)
        xn, yn, sib = (1 - x, y, c), (x, 1 - y, c), (x, y, 1 - c)
        table = {}
        for m in range(M):
            def cp(blk, k, dev):
                return _remote(blk, blk, sems[0], sems[1], 12 * m + k, dev)
            own0, own1 = part(outs, m, q, c, 0), part(outs, m, q, c, 1)
            table[m] = dict(
                to=[cp(own0, 0, xn), cp(own1, 1, yn), cp(own1, 2, xn), cp(own0, 3, yn)],
                landed=[cp(part(outs, m, qx, c, 0), 0, xn), cp(part(outs, m, qy, c, 1), 1, yn),
                        cp(part(outs, m, qx, c, 1), 2, xn), cp(part(outs, m, qy, c, 0), 3, yn),
                        cp(part(outs, m, qd, c, 0), 4, yn), cp(part(outs, m, qd, c, 1), 5, xn)],
                passed=[cp(part(outs, m, qx, c, 0), 4, yn), cp(part(outs, m, qy, c, 1), 5, xn)],
                handed=[cp(part(outs, m, qq, c, p), 6 + k, sib)
                        for k, (qq, p) in enumerate([(qx, 0), (qy, 1), (qx, 1), (qy, 0), (qd, 0), (qd, 1)])],
                taken=[cp(part(outs, m, qq, 1 - c, p), 6 + k, sib)
                       for k, (qq, p) in enumerate([(qx, 0), (qy, 1), (qx, 1), (qy, 0), (qd, 0), (qd, 1)])])
        return table

    def start(ins, outs, sems):
        t = copies(outs, sems)
        for m in range(M):
            for cp in t[m]['to']:
                cp.start()

    def finish(ins, outs, sems):
        t = copies(outs, sems)
        for m in range(M):
            for k in range(4):
                t[m]['landed'][k].wait_recv()
                if k < 2:
                    t[m]['passed'][k].start()
                t[m]['handed'][k].start()
        for m in range(M):
            for k in (4, 5):
                t[m]['landed'][k].wait_recv()
                t[m]['handed'][k].start()
        for m in range(M):
            for cp in t[m]['taken']:
                cp.wait_recv()
            for cp in t[m]['to'] + t[m]['passed'] + t[m]['handed']:
                cp.wait_send()

    return _Carry(fulls, [jax.ShapeDtypeStruct(f.shape, f.dtype) for f in fulls], {m: m for m in range(M)},
                  [pltpu.SemaphoreType.DMA((12 * M,)), pltpu.SemaphoreType.DMA((12 * M,))], start, finish)


def _pair_exchange_carry(parts):
    M = len(parts)
    groups = [p.shape[0] // 2 for p in parts]
    base = [sum(groups[:m]) for m in range(M)]
    out_shape = [jax.ShapeDtypeStruct((g,) + p.shape[1:], p.dtype) for g, p in zip(groups, parts)]

    def copies(ins, outs, sems):
        x, y, c = _me()
        return [_remote(ins[m].at[2 * g + 1 - c], outs[m].at[g], sems[0], sems[1], base[m] + g, (x, y, 1 - c))
                for m in range(M) for g in range(groups[m])]

    def start(ins, outs, sems):
        for cp in copies(ins, outs, sems):
            cp.start()

    def finish(ins, outs, sems):
        for cp in copies(ins, outs, sems):
            cp.wait()

    n = sum(groups)
    return _Carry(parts, out_shape, {}, [pltpu.SemaphoreType.DMA((n,)), pltpu.SemaphoreType.DMA((n,))], start, finish)


def _chip_exchange_carry(sums, axes):
    M = len(sums)
    out_shape = []
    for s, ax in zip(sums, axes):
        _, Rh, C = s.shape
        out_shape.append(jax.ShapeDtypeStruct((3, Rh, C // 4 if ax == 1 else C), s.dtype))

    def copies(ins, outs, sems):
        x, y, c = _me()
        cps = []
        for m in range(M):
            for j, (cx, cy) in enumerate(_other_chips(x, y)):
                qj = 2 * cx + cy
                if axes[m] == 1:
                    W = sums[m].shape[2] // 4
                    src = ins[m].at[0, :, pl.ds(pl.multiple_of(qj * W, LANES), W)]
                else:
                    src = ins[m].at[qj]
                cps.append(_remote(src, outs[m].at[j], sems[0], sems[1], 3 * m + j, (cx, cy, c)))
        return cps

    def start(ins, outs, sems):
        for cp in copies(ins, outs, sems):
            cp.start()

    def finish(ins, outs, sems):
        for cp in copies(ins, outs, sems):
            cp.wait()

    return _Carry(sums, out_shape, {}, [pltpu.SemaphoreType.DMA((3 * M,)), pltpu.SemaphoreType.DMA((3 * M,))],
                  start, finish)


def _pair_share_carry(bufs):
    M = len(bufs)

    def start(ins, outs, sems):
        x, y, c = _me()
        for m in range(M):
            _remote(outs[m].at[c], outs[m].at[c], sems[0], sems[1], m, (x, y, 1 - c)).start()

    def finish(ins, outs, sems):
        x, y, c = _me()
        for m in range(M):
            _remote(outs[m].at[c], outs[m].at[c], sems[0], sems[1], m, (x, y, 1 - c)).wait_send()
            _remote(outs[m].at[1 - c], outs[m].at[1 - c], sems[0], sems[1], m, (x, y, 1 - c)).wait_recv()

    return _Carry(bufs, [jax.ShapeDtypeStruct(b.shape, b.dtype) for b in bufs], {m: m for m in range(M)},
                  [pltpu.SemaphoreType.DMA((M,)), pltpu.SemaphoreType.DMA((M,))], start, finish)


def _allreduce_small(v, name):
    R, W = v.shape
    Rh = R // 2

    def body(v_ref, o_ref, sib, quad, send_sems, recv_sems):
        x, y, c = _me()
        q = 2 * x + y
        sibling = (x, y, 1 - c)
        pair = _remote(v_ref, sib, send_sems, recv_sems, 0, sibling)
        pair.start()
        pair.wait()
        mine = pl.ds(pl.multiple_of(c * Rh, SUBLANES), Rh)
        quad[0] = v_ref[mine, :] + sib[mine, :]
        cps = []
        for k in (1, 2, 3):
            peer = (1 - x if k & 2 else x, 1 - y if k & 1 else y, c)
            cps.append(_remote(quad.at[0], quad.at[k], send_sems, recv_sems, k, peer))
            cps[-1].start()
        for cp in cps:
            cp.wait()
        acc = quad[q]
        for p in (1, 2, 3):
            acc = acc + quad[jnp.bitwise_xor(q, p)]
        o_ref[mine, :] = acc
        theirs = pl.ds(pl.multiple_of((1 - c) * Rh, SUBLANES), Rh)
        done = _remote(o_ref.at[mine, :], o_ref.at[mine, :], send_sems, recv_sems, 4, sibling)
        done.start()
        done.wait_send()
        _remote(o_ref.at[theirs, :], o_ref.at[theirs, :], send_sems, recv_sems, 4, sibling).wait_recv()

    vm = pl.BlockSpec(memory_space=pltpu.VMEM)
    return pl.pallas_call(
        body, name=name, in_specs=[vm], out_specs=vm, out_shape=jax.ShapeDtypeStruct((R, W), F32),
        scratch_shapes=[pltpu.VMEM((R, W), F32), pltpu.VMEM((4, Rh, W), F32),
                        pltpu.SemaphoreType.DMA((5,)), pltpu.SemaphoreType.DMA((5,))],
        compiler_params=pltpu.CompilerParams(vmem_limit_bytes=VMEM_LIMIT),
    )(v)


def _pack(pieces):
    flat = []
    for p in pieces:
        p = p.reshape(-1).astype(F32)
        pad = (-p.shape[0]) % PACK_ALIGN
        flat.append(jnp.pad(p, (0, pad)).reshape(-1, PACK_W))
    if sum(f.shape[0] for f in flat) % (2 * SUBLANES):
        flat.append(jnp.zeros((SUBLANES, PACK_W), F32))
    return jnp.concatenate(flat, axis=0)


def _unpack(packed, shapes):
    out, row = [], 0
    for shp in shapes:
        n = math.prod(shp)
        rows = -(-n // PACK_ALIGN) * SUBLANES
        out.append(packed[row:row + rows].reshape(-1)[:n].reshape(shp))
        row += rows
    return out


def kernel(x, ffn1_norm, ffn1_w_gate, ffn1_w_up, ffn1_w_down, mix_norm, w_in, lru_conv_w, lru_conv_b, lru_w_a, lru_b_a, lru_w_i, lru_b_i, lru_lambda, sc_conv_w, lru_out_norm, sc_out_norm, w_out, ffn2_norm, ffn2_w_gate, ffn2_w_up, ffn2_w_down, final_norm, loss_target, m_ffn1_norm, m_ffn1_w_gate, m_ffn1_w_up, m_ffn1_w_down, m_mix_norm, m_w_in, m_lru_conv_w, m_lru_conv_b, m_lru_w_a, m_lru_b_a, m_lru_w_i, m_lru_b_i, m_lru_lambda, m_sc_conv_w, m_lru_out_norm, m_sc_out_norm, m_w_out, m_ffn2_norm, m_ffn2_w_gate, m_ffn2_w_up, m_ffn2_w_down, m_final_norm, v_ffn1_norm, v_ffn1_w_gate, v_ffn1_w_up, v_ffn1_w_down, v_mix_norm, v_w_in, v_lru_conv_w, v_lru_conv_b, v_lru_w_a, v_lru_b_a, v_lru_w_i, v_lru_b_i, v_lru_lambda, v_sc_conv_w, v_lru_out_norm, v_sc_out_norm, v_w_out, v_ffn2_norm, v_ffn2_w_gate, v_ffn2_w_up, v_ffn2_w_down, v_final_norm):
    vals = locals()
    w = {n: vals[n] for n in WEIGHTS}
    mom = {n: vals["m_" + n] for n in WEIGHTS}
    var = {n: vals["v_" + n] for n in WEIGHTS}

    xi, yi, ci = _me()
    qi = 2 * xi + yi
    c_arr = jnp.reshape(ci, (1,)).astype(jnp.int32)
    qc_arr = jnp.stack([qi, ci]).astype(jnp.int32)

    T, D = x.shape[1], x.shape[2]
    xt = x.reshape(T, D)
    target = loss_target.reshape(T, D)
    DL = lru_conv_b.shape[-1]
    NH, HD = lru_w_a.shape[1], lru_w_a.shape[2]
    KL, KS = lru_conv_w.shape[1], sc_conv_w.shape[1]
    DLq = lru_conv_w.shape[2]

    axis_of = dict(zip(BIG, BIG_AXIS))
    first_names = ['ffn1_w_gate', 'ffn1_w_up', 'ffn1_w_down']
    later_names = ['w_in', 'w_out', 'ffn2_w_gate', 'ffn2_w_up', 'ffn2_w_down']

    def unview(n, g):
        return g.reshape(2 * g.shape[1], g.shape[2]) if axis_of[n] == 1 else g.reshape(8 * g.shape[1], g.shape[2])

    placed = {n: _cast_into_full(w[n][0], qc_arr, axis_of[n], "cast_" + n) for n in BIG}
    gathered = _run_carry(_gather_two_way_carry([placed[n] for n in first_names], [axis_of[n] for n in first_names]),
                          "gather_ffn1_weights")
    full = {n: unview(n, g) for n, g in zip(first_names, gathered)}
    gather_later = _gather_carry([placed[n] for n in later_names], [axis_of[n] for n in later_names])

    taps = jnp.zeros((2 * SUBLANES, DL), F32)
    taps = lax.dynamic_update_slice(taps, lru_conv_w[0], (0, qi * DLq))
    taps = lax.dynamic_update_slice(taps, sc_conv_w[0], (KL, qi * DLq))
    taps = _allreduce_small(jnp.where(ci == 0, taps, 0.0), "gather_conv_taps")
    cw, sw = taps[0:KL], taps[KL:KL + KS]

    cb = lru_conv_b
    wa, wi = lru_w_a[0].astype(BF16), lru_w_i[0].astype(BF16)
    ba, bi = lru_b_a.reshape(1, DL), lru_b_i.reshape(1, DL)
    mix_args = (cw, cb, wa, ba, wi, bi, lru_lambda, sw, lru_out_norm, sc_out_norm)
    gf = final_norm.reshape(1, D)

    res = _ffn_fwd(xt, ffn1_norm, full['ffn1_w_gate'], full['ffn1_w_up'], full['ffn1_w_down'], "ffn1_fwd", gather_later)
    x1, n1, G1, U1 = res[:4]
    full.update({n: unview(n, g) for n, g in zip(later_names, res[4:])})
    n2, z = _norm_mm(x1, mix_norm, full['w_in'], "mix_in_proj")
    h, ymix = _mix_fwd(z, *mix_args, "mix_fwd")
    x2 = _mm_fullk(ymix, full['w_out'], False, x1, F32, "mix_out_proj")[0]
    x3, n3, G2, U2 = _ffn_fwd(x2, ffn2_norm, full['ffn2_w_gate'], full['ffn2_w_up'], full['ffn2_w_down'], "ffn2_fwd")
    dx3, d3b, sqerr, dgf = _loss_head(x3, gf, target, "loss_head")

    grads, halves, shared = {}, {}, {}

    def pair_carry(names):
        parts = []
        for n in names:
            R, C = grads[n].shape
            parts.append(grads[n].reshape(2, R // 2, C) if axis_of[n] == 1 else grads[n].reshape(8, R // 8, C))
        return parts, _pair_exchange_carry(parts)

    def pair_add(names, parts, recv):
        return [_pair_add(p, r, c_arr, "grad_pair_add_" + n) for n, p, r in zip(names, parts, recv)]

    def chip_carry(names, sums):
        return _chip_exchange_carry(sums, [axis_of[n] for n in names])

    def chip_add(names, sums, recv):
        for n, s, r in zip(names, sums, recv):
            halves[n] = _quad_add(s, r, qc_arr, axis_of[n], "grad_chip_add_" + n)

    dG2, dU2, H2, dg_ffn2, dx2, dx2b = _ffn_bwd_fused(
        d3b, G2, U2, full['ffn2_w_gate'], full['ffn2_w_up'], full['ffn2_w_down'], x2, ffn2_norm, dx3, 1.0, "ffn2_bwd")
    grads['ffn2_w_gate'] = _mm_tn(n3, dG2, "ffn2_dwg")[0]
    grads['ffn2_w_up'] = _mm_tn(n3, dU2, "ffn2_dwu")[0]
    grads['ffn2_w_down'] = _mm_tn(H2, d3b, "ffn2_dwd")[0]
    names_a = ['ffn2_w_gate', 'ffn2_w_up', 'ffn2_w_down', 'w_out', 'w_in']
    parts, carry = pair_carry(names_a[:3])
    res = _mm_fullk(dx2b, full['w_out'], True, None, BF16, "mix_out_bwd", carry)
    dy = res[0]
    sums_a = pair_add(names_a[:3], parts, res[1:])
    grads['w_out'] = _mm_tn(ymix, dx2b, "mix_dwout")[0]
    dz, dwa, dwi, vec = _mix_bwd(z, h, dy, *mix_args, "mix_bwd")
    grads['w_in'] = _mm_tn(n2, dz, "mix_dwin")[0]
    parts, carry = pair_carry(names_a[3:])
    res = _mm_nt_norm_bwd(dz, full['w_in'], x1, mix_norm, dx2, FFN_RESIDUAL_SCALE, "mix_in_bwd", carry)
    dg_mix, dx1, d1b = res[:3]
    sums_a += pair_add(names_a[3:], parts, res[3:])

    res = _ffn_bwd_act(d1b, G1, U1, full['ffn1_w_gate'], full['ffn1_w_up'], full['ffn1_w_down'], "ffn1_bwd",
                       chip_carry(names_a, sums_a))
    dG1, dU1, H1, dn1 = res[:4]
    chip_add(names_a, sums_a, res[4:])
    dx0, _, dg_ffn1 = _rms_bwd_res(dn1, xt, ffn1_norm, dx1, 1.0, "ffn1_norm_bwd")
    res = _mm_tn(n1, dG1, "ffn1_dwg", _pair_share_carry([halves[n] for n in names_a]))
    grads['ffn1_w_gate'] = res[0]
    shared.update(zip(names_a, res[1:]))
    names_b = ['ffn1_w_gate', 'ffn1_w_up']
    parts_g, carry = pair_carry(names_b[:1])
    res = _mm_tn(n1, dU1, "ffn1_dwu", carry)
    grads['ffn1_w_up'] = res[0]
    sums_b = pair_add(names_b[:1], parts_g, res[1:])
    parts_u, carry = pair_carry(names_b[1:])
    sums_b += pair_add(names_b[1:], parts_u, _run_carry(carry, "grad_pair_exchange_ffn1_up"))
    res = _mm_tn(H1, d1b, "ffn1_dwd", chip_carry(names_b, sums_b))
    grads['ffn1_w_down'] = res[0]
    chip_add(names_b, sums_b, res[1:])
    names_c = ['ffn1_w_down']
    parts, carry = pair_carry(names_c)
    sums_c = pair_add(names_c, parts, _run_carry(carry, "grad_pair_exchange_ffn1_out"))
    chip_add(names_c, sums_c, _run_carry(chip_carry(names_c, sums_c), "grad_chip_exchange_ffn1_out"))
    names_bc = names_b + names_c
    shared.update(zip(names_bc, _run_carry(_pair_share_carry([halves[n] for n in names_bc]), "grad_pair_share")))
    out_g, out_d, out_m, out_v = {}, {}, {}, {}
    for n in BIG:
        shp = w[n].shape
        outs = _adamw(w[n][0], shared[n].reshape(shp[1], shp[2]), mom[n][0], var[n][0], "adamw_" + n)
        out_d[n], out_m[n], out_v[n], out_g[n] = (a.reshape(shp) for a in outs)

    small = [n for n in WEIGHTS if n not in BIG]
    local_small = {
        'ffn1_norm': dg_ffn1, 'mix_norm': dg_mix, 'lru_conv_w': vec[V_CW:V_CW + KL], 'lru_conv_b': vec[V_CB],
        'lru_w_a': dwa, 'lru_b_a': vec[V_BA], 'lru_w_i': dwi, 'lru_b_i': vec[V_BI], 'lru_lambda': vec[V_LAM],
        'sc_conv_w': vec[V_SW:V_SW + KS], 'lru_out_norm': vec[V_GLO], 'sc_out_norm': vec[V_GSO],
        'ffn2_norm': dg_ffn2, 'final_norm': dgf,
    }
    full_shapes = [local_small[n].shape for n in small] + [(1,)]
    reduced = _allreduce_small(_pack([local_small[n] for n in small] + [sqerr[0, 0:1]]), "allreduce_small")
    reduced = _unpack(reduced, full_shapes)
    loss = (0.5 / D) * reduced[-1][0]
    gsm = {}
    for n, g in zip(small, reduced[:-1]):
        if n in SMALL_SHARDED:
            g = lax.dynamic_slice(g, (0, qi * DLq), (g.shape[0], DLq))
        gsm[n] = g.reshape(w[n].shape)
    small_shapes = [w[n].shape for n in small]
    d_s, m_s, v_s, _ = _adamw(_pack([w[n] for n in small]), _pack([gsm[n] for n in small]),
                              _pack([mom[n] for n in small]), _pack([var[n] for n in small]), "adamw_small")
    for n, d, mn, vn in zip(small, _unpack(d_s, small_shapes), _unpack(m_s, small_shapes), _unpack(v_s, small_shapes)):
        out_g[n], out_d[n], out_m[n], out_v[n] = gsm[n], d, mn, vn

    return (loss, dx0.reshape(x.shape), *[out_g[n] for n in WEIGHTS], *[out_d[n] for n in WEIGHTS],
            *[out_m[n] for n in WEIGHTS], *[out_v[n] for n in WEIGHTS])
```

```python
import math

import jax
import jax.numpy as jnp
from jax import lax
from jax.experimental import pallas as pl
from jax.experimental.pallas import tpu as pltpu

F32 = jnp.float32
BF16 = jnp.bfloat16
MESH = pl.DeviceIdType.MESH
ANY = pl.BlockSpec(memory_space=pl.ANY)

NORM_EPS = 1e-6
LRU_C = 8.0
FFN_RESIDUAL_SCALE = 0.5
ADAM_LR = 0.001
ADAM_B1 = 0.9
ADAM_B2 = 0.999
ADAM_EPS = 1e-08
ADAM_WD = 0.01
ADAM_STEP = 10

V7X_VMEM_BYTES = 64 * 2**20
VMEM_LIMIT = V7X_VMEM_BYTES - 8 * 2**20
LANES = 128
SUBLANES = 8
PACK_W = LANES
PACK_ALIGN = SUBLANES * PACK_W

WEIGHTS = ['ffn1_norm', 'ffn1_w_gate', 'ffn1_w_up', 'ffn1_w_down', 'mix_norm', 'w_in', 'lru_conv_w', 'lru_conv_b',
           'lru_w_a', 'lru_b_a', 'lru_w_i', 'lru_b_i', 'lru_lambda', 'sc_conv_w', 'lru_out_norm', 'sc_out_norm',
           'w_out', 'ffn2_norm', 'ffn2_w_gate', 'ffn2_w_up', 'ffn2_w_down', 'final_norm']
BIG = ['ffn1_w_gate', 'ffn1_w_up', 'ffn1_w_down', 'w_in', 'w_out', 'ffn2_w_gate', 'ffn2_w_up', 'ffn2_w_down']
BIG_AXIS = [1, 1, 0, 1, 0, 1, 1, 0]
SMALL_SHARDED = ['lru_conv_w', 'sc_conv_w']


def _tile(n, pref, mult):
    if n <= pref:
        return n
    t = (pref // mult) * mult
    while t >= mult:
        if n % t == 0:
            return t
        t -= mult
    return n


def _params(*sem):
    return pltpu.CompilerParams(dimension_semantics=sem, vmem_limit_bytes=VMEM_LIMIT)


def _me():
    return lax.axis_index("x"), lax.axis_index("y"), lax.axis_index("c")


def _sigmoid(v):
    return 1.0 / (1.0 + jnp.exp(-v))


def _rstd(v):
    return lax.rsqrt(jnp.mean(v * v, axis=-1, keepdims=True) + NORM_EPS)


def _rms_bwd(dy, v, gain):
    r = _rstd(v)
    w = gain * dy
    dv = r * w - v * (r * r * r) * jnp.mean(v * w, axis=-1, keepdims=True)
    dgain = jnp.sum(dy * v * r, axis=0, keepdims=True)
    return dv, dgain


def _dot_nt(a, b):
    return lax.dot_general(a, b, (((1,), (1,)), ((), ())), preferred_element_type=F32)


def _dot_tn(a, b):
    return lax.dot_general(a, b, (((0,), (0,)), ((), ())), preferred_element_type=F32)


class _Carry:
    def __init__(self, inputs, out_shape, aliases, sems, start, finish, middle=None, middle_at=0.85):
        self.inputs, self.out_shape, self.aliases, self.sems = list(inputs), list(out_shape), dict(aliases), list(sems)
        self.start, self.finish = start, finish
        self.middle, self.middle_at = middle, middle_at


def _call(body, name, grid, in_specs, out_specs, out_shape, scratch_shapes, semantics, args, carry=None, prefetch=()):
    np_ = len(prefetch)
    if carry is None:
        spec = pltpu.PrefetchScalarGridSpec(num_scalar_prefetch=np_, grid=grid, in_specs=in_specs, out_specs=out_specs,
                                            scratch_shapes=scratch_shapes)
        return pl.pallas_call(body, name=name, grid_spec=spec, out_shape=out_shape,
                              compiler_params=_params(*semantics))(*prefetch, *args)
    ni, no, ns = len(in_specs), len(out_specs), len(scratch_shapes)
    ci, co = len(carry.inputs), len(carry.out_shape)

    def carrying(*refs):
        pre, refs = refs[:np_], refs[np_:]
        ins, refs = refs[:ni], refs[ni:]
        cins, refs = refs[:ci], refs[ci:]
        outs, refs = refs[:no], refs[no:]
        couts, refs = refs[:co], refs[co:]
        scratch, csems = refs[:ns], refs[ns:]
        step = pl.program_id(0)
        for ax in range(1, len(grid)):
            step = step * grid[ax] + pl.program_id(ax)
        steps = math.prod(grid)
        first = step == 0
        last = step == steps - 1

        @pl.when(first)
        def _():
            carry.start(cins, couts, csems)

        if carry.middle is not None:
            @pl.when(step == min(int(carry.middle_at * steps), steps - 1))
            def _():
                carry.middle(cins, couts, csems)

        body(*pre, *ins, *outs, *scratch)

        @pl.when(last)
        def _():
            carry.finish(cins, couts, csems)

    spec = pltpu.PrefetchScalarGridSpec(
        num_scalar_prefetch=np_, grid=grid, in_specs=list(in_specs) + [ANY] * ci,
        out_specs=list(out_specs) + [ANY] * co, scratch_shapes=list(scratch_shapes) + carry.sems)
    return pl.pallas_call(
        carrying, name=name, grid_spec=spec, out_shape=list(out_shape) + carry.out_shape,
        input_output_aliases={np_ + ni + i: no + j for i, j in carry.aliases.items()},
        compiler_params=_params(*(["arbitrary"] * len(grid))),
    )(*prefetch, *args, *carry.inputs)


def _run_carry(carry, name):
    ci, co = len(carry.inputs), len(carry.out_shape)

    def body(*refs):
        cins, couts, csems = refs[:ci], refs[ci:ci + co], refs[ci + co:]
        carry.start(cins, couts, csems)
        if carry.middle is not None:
            carry.middle(cins, couts, csems)
        carry.finish(cins, couts, csems)

    return pl.pallas_call(body, name=name, in_specs=[ANY] * ci, out_specs=[ANY] * co, out_shape=carry.out_shape,
                          input_output_aliases=carry.aliases, scratch_shapes=carry.sems)(*carry.inputs)


def _ffn_fwd(x, gain, wg, wu, wd, name, carry=None):
    T, D = x.shape
    FF = wg.shape[1]
    tm = _tile(T, 512, 16)
    tf = _tile(FF, 512, LANES)
    nf = FF // tf

    def body(x_ref, g_ref, wg_ref, wu_ref, wd_ref, xo_ref, n_ref, G_ref, U_ref, acc_ref):
        f = pl.program_id(1)

        @pl.when(f == 0)
        def _():
            xv = x_ref[...]
            n_ref[...] = (xv * _rstd(xv) * g_ref[...]).astype(BF16)
            acc_ref[...] = jnp.zeros_like(acc_ref)

        n = n_ref[...]
        G = jnp.dot(n, wg_ref[...], preferred_element_type=F32)
        U = jnp.dot(n, wu_ref[...], preferred_element_type=F32)
        G_ref[...] = G.astype(BF16)
        U_ref[...] = U.astype(BF16)
        H = (G * _sigmoid(G) * U).astype(BF16)
        acc_ref[...] += jnp.dot(H, wd_ref[...], preferred_element_type=F32)

        @pl.when(f == nf - 1)
        def _():
            xo_ref[...] = x_ref[...] + FFN_RESIDUAL_SCALE * acc_ref[...]

    return _call(
        body, name, (T // tm, nf),
        [pl.BlockSpec((tm, D), lambda i, f: (i, 0)),
         pl.BlockSpec((1, D), lambda i, f: (0, 0)),
         pl.BlockSpec((D, tf), lambda i, f: (0, f)),
         pl.BlockSpec((D, tf), lambda i, f: (0, f)),
         pl.BlockSpec((tf, D), lambda i, f: (f, 0))],
        [pl.BlockSpec((tm, D), lambda i, f: (i, 0)),
         pl.BlockSpec((tm, D), lambda i, f: (i, 0)),
         pl.BlockSpec((tm, tf), lambda i, f: (i, f)),
         pl.BlockSpec((tm, tf), lambda i, f: (i, f))],
        [jax.ShapeDtypeStruct((T, D), F32), jax.ShapeDtypeStruct((T, D), BF16),
         jax.ShapeDtypeStruct((T, FF), BF16), jax.ShapeDtypeStruct((T, FF), BF16)],
        [pltpu.VMEM((tm, D), F32)], ("parallel", "arbitrary"), (x, gain, wg, wu, wd), carry)


def _ffn_bwd_act(db, G, U, wg, wu, wd, name, carry=None):
    T, D = db.shape
    FF = wg.shape[1]
    tm = _tile(T, 512, 16)
    tf = _tile(FF, 512, LANES)

    def body(d_ref, G_ref, U_ref, wg_ref, wu_ref, wd_ref, dG_ref, dU_ref, H_ref, dn_ref):
        f = pl.program_id(1)
        dH = _dot_nt(d_ref[...], wd_ref[...])
        Gv = G_ref[...].astype(F32)
        Uv = U_ref[...].astype(F32)
        s = _sigmoid(Gv)
        sg = Gv * s
        H_ref[...] = (sg * Uv).astype(BF16)
        dU = (dH * sg).astype(BF16)
        dG = (dH * Uv * (s * (1.0 + Gv * (1.0 - s)))).astype(BF16)
        dG_ref[...] = dG
        dU_ref[...] = dU
        contrib = _dot_nt(dG, wg_ref[...]) + _dot_nt(dU, wu_ref[...])

        @pl.when(f == 0)
        def _():
            dn_ref[...] = contrib

        @pl.when(f > 0)
        def _():
            dn_ref[...] += contrib

    return _call(
        body, name, (T // tm, FF // tf),
        [pl.BlockSpec((tm, D), lambda i, f: (i, 0)),
         pl.BlockSpec((tm, tf), lambda i, f: (i, f)),
         pl.BlockSpec((tm, tf), lambda i, f: (i, f)),
         pl.BlockSpec((D, tf), lambda i, f: (0, f)),
         pl.BlockSpec((D, tf), lambda i, f: (0, f)),
         pl.BlockSpec((tf, D), lambda i, f: (f, 0))],
        [pl.BlockSpec((tm, tf), lambda i, f: (i, f)),
         pl.BlockSpec((tm, tf), lambda i, f: (i, f)),
         pl.BlockSpec((tm, tf), lambda i, f: (i, f)),
         pl.BlockSpec((tm, D), lambda i, f: (i, 0))],
        [jax.ShapeDtypeStruct((T, FF), BF16), jax.ShapeDtypeStruct((T, FF), BF16),
         jax.ShapeDtypeStruct((T, FF), BF16), jax.ShapeDtypeStruct((T, D), F32)],
        [], ("parallel", "arbitrary"), (db, G, U, wg, wu, wd), carry)


TAIL_ROWS = 128


class _NormBwdTail:
    def __init__(self, T, D, tm, scale):
        self.T, self.D, self.tm, self.scale = T, D, tm, scale
        self.ni = T // tm
        self.scratch = [pltpu.VMEM((tm, D), F32), pltpu.VMEM((tm, D), F32), pltpu.VMEM((tm, D), F32),
                        pltpu.VMEM((tm, D), BF16), pltpu.SemaphoreType.DMA((4,))]
        self.out_shape = [jax.ShapeDtypeStruct((T, D), F32), jax.ShapeDtypeStruct((T, D), BF16)]

    def _rows(self, k):
        return pl.ds(pl.multiple_of(k * self.tm, self.tm), self.tm)

    def _loads(self, k, x_hbm, r_hbm, bufs):
        xbuf, rbuf, _, _, sems = bufs
        return [pltpu.make_async_copy(x_hbm.at[self._rows(k)], xbuf, sems.at[0]),
                pltpu.make_async_copy(r_hbm.at[self._rows(k)], rbuf, sems.at[1])]

    def _stores(self, k, dx_hbm, dxb_hbm, bufs):
        _, _, obuf, obbuf, sems = bufs
        return [pltpu.make_async_copy(obuf, dx_hbm.at[self._rows(k)], sems.at[2]),
                pltpu.make_async_copy(obbuf, dxb_hbm.at[self._rows(k)], sems.at[3])]

    def prefetch(self, i, x_hbm, r_hbm, bufs):
        for cp in self._loads(i, x_hbm, r_hbm, bufs):
            cp.start()

    def run(self, i, acc_ref, g_ref, x_hbm, r_hbm, dx_hbm, dxb_hbm, dg_ref, bufs):
        xbuf, rbuf, obuf, obbuf, _ = bufs
        for cp in self._loads(i, x_hbm, r_hbm, bufs):
            cp.wait()

        @pl.when(i > 0)
        def _():
            for cp in self._stores(i - 1, dx_hbm, dxb_hbm, bufs):
                cp.wait()

        dgain = None
        for r0 in range(0, self.tm, TAIL_ROWS):
            rs = slice(r0, min(r0 + TAIL_ROWS, self.tm))
            dv, dgr = _rms_bwd(acc_ref[rs, :], xbuf[rs, :], g_ref[...])
            dx = rbuf[rs, :] + dv
            obuf[rs, :] = dx
            obbuf[rs, :] = (self.scale * dx).astype(BF16)
            dgain = dgr if dgain is None else dgain + dgr
        for cp in self._stores(i, dx_hbm, dxb_hbm, bufs):
            cp.start()

        @pl.when(i == 0)
        def _():
            dg_ref[...] = dgain

        @pl.when(i > 0)
        def _():
            dg_ref[...] += dgain

        @pl.when(i == self.ni - 1)
        def _():
            for cp in self._stores(i, dx_hbm, dxb_hbm, bufs):
                cp.wait()


def _ffn_bwd_fused(db, G, U, wg, wu, wd, x_in, gain, dres, scale, name):
    T, D = db.shape
    FF = wg.shape[1]
    tm = _tile(T, 512, 16)
    tf = _tile(FF, 512, LANES)
    nf = FF // tf
    tail = _NormBwdTail(T, D, tm, scale)

    def body(d_ref, G_ref, U_ref, wg_ref, wu_ref, wd_ref, g_ref, x_hbm, r_hbm,
             dG_ref, dU_ref, H_ref, dg_ref, dx_hbm, dxb_hbm, acc_ref, *bufs):
        i, f = pl.program_id(0), pl.program_id(1)

        @pl.when(f == 0)
        def _():
            tail.prefetch(i, x_hbm, r_hbm, bufs)

        dH = _dot_nt(d_ref[...], wd_ref[...])
        Gv = G_ref[...].astype(F32)
        Uv = U_ref[...].astype(F32)
        s = _sigmoid(Gv)
        sg = Gv * s
        H_ref[...] = (sg * Uv).astype(BF16)
        dU = (dH * sg).astype(BF16)
        dG = (dH * Uv * (s * (1.0 + Gv * (1.0 - s)))).astype(BF16)
        dG_ref[...] = dG
        dU_ref[...] = dU
        contrib = _dot_nt(dG, wg_ref[...]) + _dot_nt(dU, wu_ref[...])

        @pl.when(f == 0)
        def _():
            acc_ref[...] = contrib

        @pl.when(f > 0)
        def _():
            acc_ref[...] += contrib

        @pl.when(f == nf - 1)
        def _():
            tail.run(i, acc_ref, g_ref, x_hbm, r_hbm, dx_hbm, dxb_hbm, dg_ref, bufs)

    act = pl.BlockSpec((tm, tf), lambda i, f: (i, f))
    return pl.pallas_call(
        body, name=name, grid=(T // tm, nf),
        in_specs=[pl.BlockSpec((tm, D), lambda i, f: (i, 0)), act, act,
                  pl.BlockSpec((D, tf), lambda i, f: (0, f)),
                  pl.BlockSpec((D, tf), lambda i, f: (0, f)),
                  pl.BlockSpec((tf, D), lambda i, f: (f, 0)),
                  pl.BlockSpec((1, D), lambda i, f: (0, 0)), ANY, ANY],
        out_specs=[act, act, act, pl.BlockSpec((1, D), lambda i, f: (0, 0)), ANY, ANY],
        out_shape=[jax.ShapeDtypeStruct((T, FF), BF16)] * 3 + [jax.ShapeDtypeStruct((1, D), F32)] + tail.out_shape,
        scratch_shapes=[pltpu.VMEM((tm, D), F32)] + tail.scratch,
        compiler_params=_params("arbitrary", "arbitrary"),
    )(db, G, U, wg, wu, wd, gain, x_in, dres)


def _mm_nt_norm_bwd(a, w, x_in, gain, dres, scale, name, carry=None):
    T, K = a.shape
    D = w.shape[0]
    tm = _tile(T, 512, 16)
    tk = _tile(K, 1280, LANES)
    nk = K // tk
    tail = _NormBwdTail(T, D, tm, scale)

    def body(a_ref, w_ref, g_ref, x_hbm, r_hbm, dg_ref, dx_hbm, dxb_hbm, acc_ref, *bufs):
        i, k = pl.program_id(0), pl.program_id(1)

        @pl.when(k == 0)
        def _():
            tail.prefetch(i, x_hbm, r_hbm, bufs)

        contrib = _dot_nt(a_ref[...], w_ref[...])

        @pl.when(k == 0)
        def _():
            acc_ref[...] = contrib

        @pl.when(k > 0)
        def _():
            acc_ref[...] += contrib

        @pl.when(k == nk - 1)
        def _():
            tail.run(i, acc_ref, g_ref, x_hbm, r_hbm, dx_hbm, dxb_hbm, dg_ref, bufs)

    return _call(
        body, name, (T // tm, nk),
        [pl.BlockSpec((tm, tk), lambda i, k: (i, k)), pl.BlockSpec((D, tk), lambda i, k: (0, k)),
         pl.BlockSpec((1, D), lambda i, k: (0, 0)), ANY, ANY],
        [pl.BlockSpec((1, D), lambda i, k: (0, 0)), ANY, ANY],
        [jax.ShapeDtypeStruct((1, D), F32)] + tail.out_shape,
        [pltpu.VMEM((tm, D), F32)] + tail.scratch, ("arbitrary", "arbitrary"), (a, w, gain, x_in, dres), carry)


def _rms_bwd_res(dn, x, gain, dres, scale, name, carry=None):
    T, D = x.shape
    tm = _tile(T, 256, 16)

    def body(dn_ref, x_ref, g_ref, dr_ref, dx_ref, dxb_ref, dg_ref):
        i = pl.program_id(0)
        dv, dgain = _rms_bwd(dn_ref[...], x_ref[...], g_ref[...])
        dx = dr_ref[...] + dv
        dx_ref[...] = dx
        dxb_ref[...] = (scale * dx).astype(BF16)

        @pl.when(i == 0)
        def _():
            dg_ref[...] = dgain

        @pl.when(i > 0)
        def _():
            dg_ref[...] += dgain

    row = pl.BlockSpec((tm, D), lambda i: (i, 0))
    vec = pl.BlockSpec((1, D), lambda i: (0, 0))
    return _call(
        body, name, (T // tm,), [row, row, vec, row], [row, row, vec],
        [jax.ShapeDtypeStruct((T, D), F32), jax.ShapeDtypeStruct((T, D), BF16), jax.ShapeDtypeStruct((1, D), F32)],
        [], ("arbitrary",), (dn, x, gain, dres), carry)


def _loss_head(x3, gain, target, name):
    T, D = x3.shape
    tm = _tile(T, 256, 16)

    def body(x_ref, g_ref, t_ref, dx_ref, dxb_ref, ls_ref, dg_ref):
        i = pl.program_id(0)
        xv = x_ref[...]
        err = xv * _rstd(xv) * g_ref[...] - t_ref[...]
        sq = jnp.sum(jnp.sum(err * err, axis=1, keepdims=True), axis=0, keepdims=True)
        dv, dgain = _rms_bwd(err * (1.0 / D), xv, g_ref[...])
        dx_ref[...] = dv
        dxb_ref[...] = (FFN_RESIDUAL_SCALE * dv).astype(BF16)
        sqb = jnp.broadcast_to(sq, (1, LANES))

        @pl.when(i == 0)
        def _():
            dg_ref[...] = dgain
            ls_ref[...] = sqb

        @pl.when(i > 0)
        def _():
            dg_ref[...] += dgain
            ls_ref[...] += sqb

    row = pl.BlockSpec((tm, D), lambda i: (i, 0))
    vec = pl.BlockSpec((1, D), lambda i: (0, 0))
    return pl.pallas_call(
        body, name=name, grid=(T // tm,),
        in_specs=[row, vec, row],
        out_specs=[row, row, pl.BlockSpec((1, LANES), lambda i: (0, 0)), vec],
        out_shape=[jax.ShapeDtypeStruct((T, D), F32), jax.ShapeDtypeStruct((T, D), BF16),
                   jax.ShapeDtypeStruct((1, LANES), F32), jax.ShapeDtypeStruct((1, D), F32)],
        compiler_params=_params("arbitrary"),
    )(x3, gain, target)


def _norm_mm(x, gain, w, name):
    T, D = x.shape
    N = w.shape[1]
    tm = _tile(T, 512, 16)
    tn = _tile(N, 2560, LANES)

    def body(x_ref, g_ref, w_ref, n_ref, z_ref):
        @pl.when(pl.program_id(1) == 0)
        def _():
            xv = x_ref[...]
            n_ref[...] = (xv * _rstd(xv) * g_ref[...]).astype(BF16)

        z_ref[...] = jnp.dot(n_ref[...], w_ref[...], preferred_element_type=F32).astype(BF16)

    return pl.pallas_call(
        body, name=name, grid=(T // tm, N // tn),
        in_specs=[pl.BlockSpec((tm, D), lambda i, j: (i, 0)),
                  pl.BlockSpec((1, D), lambda i, j: (0, 0)),
                  pl.BlockSpec((D, tn), lambda i, j: (0, j))],
        out_specs=[pl.BlockSpec((tm, D), lambda i, j: (i, 0)),
                   pl.BlockSpec((tm, tn), lambda i, j: (i, j))],
        out_shape=[jax.ShapeDtypeStruct((T, D), BF16), jax.ShapeDtypeStruct((T, N), BF16)],
        compiler_params=_params("parallel", "arbitrary"),
    )(x, gain, w)


def _mm_fullk(a, w, trans_w, residual, out_dtype, name, carry=None):
    T, K = a.shape
    N = w.shape[0] if trans_w else w.shape[1]
    tm = _tile(T, 512, 16)
    tn = _tile(N, 2048 * 2560 // K, LANES)

    def body(*refs):
        if residual is None:
            a_ref, w_ref, o_ref = refs
        else:
            a_ref, w_ref, r_ref, o_ref = refs
        if trans_w:
            acc = _dot_nt(a_ref[...], w_ref[...])
        else:
            acc = jnp.dot(a_ref[...], w_ref[...], preferred_element_type=F32)
        if residual is not None:
            acc = acc + r_ref[...]
        o_ref[...] = acc.astype(out_dtype)

    w_spec = pl.BlockSpec((tn, K), lambda i, j: (j, 0)) if trans_w else pl.BlockSpec((K, tn), lambda i, j: (0, j))
    in_specs = [pl.BlockSpec((tm, K), lambda i, j: (i, 0)), w_spec]
    args = [a, w]
    if residual is not None:
        in_specs.append(pl.BlockSpec((tm, tn), lambda i, j: (i, j)))
        args.append(residual)
    return _call(body, name, (T // tm, N // tn), in_specs, [pl.BlockSpec((tm, tn), lambda i, j: (i, j))],
                 [jax.ShapeDtypeStruct((T, N), out_dtype)], [], ("parallel", "arbitrary"), args, carry)


def _mm_tn(a, b, name, carry=None):
    T, M = a.shape
    N = b.shape[1]
    tmw = _tile(M, 2048, LANES)
    tnw = _tile(N, 2048 * 1408 // tmw, LANES)
    tk = _tile(T, 1024, 16)
    nk = T // tk

    def body(a_ref, b_ref, o_ref, acc_ref):
        k = pl.program_id(2)

        @pl.when(k == 0)
        def _():
            acc_ref[...] = jnp.zeros_like(acc_ref)

        acc_ref[...] += _dot_tn(a_ref[...], b_ref[...])

        @pl.when(k == nk - 1)
        def _():
            o_ref[...] = acc_ref[...].astype(BF16)

    return _call(
        body, name, (M // tmw, N // tnw, nk),
        [pl.BlockSpec((tk, tmw), lambda i, j, k: (k, i)),
         pl.BlockSpec((tk, tnw), lambda i, j, k: (k, j))],
        [pl.BlockSpec((tmw, tnw), lambda i, j, k: (i, j))],
        [jax.ShapeDtypeStruct((M, N), BF16)],
        [pltpu.VMEM((tmw, tnw), F32)], ("parallel", "parallel", "arbitrary"), (a, b), carry)


GELU_K = math.sqrt(2.0 / math.pi)
GELU_C = 0.044715


def _gelu_and_grad(v):
    u = GELU_K * (v + GELU_C * v * v * v)
    th = jnp.tanh(u)
    g = 0.5 * v * (1.0 + th)
    dg = 0.5 * (1.0 + th) + 0.5 * v * (1.0 - th * th) * GELU_K * (1.0 + 3.0 * GELU_C * v * v)
    return g, dg


def _neg_expm1(v):
    poly = v * (1.0 + v * (0.5 + v * (1.0 / 6 + v * (1.0 / 24 + v * (1.0 / 120 + v * (1.0 / 720))))))
    return jnp.where(v > -0.25, -poly, 1.0 - jnp.exp(v))


def _softplus_neg(lam):
    e = jnp.exp(-jnp.abs(lam))
    log1pe = jnp.where(e < 1e-4, e * (1.0 - 0.5 * e), jnp.log(1.0 + e))
    sp = jnp.maximum(-lam, 0.0) + log1pe
    dsp = -1.0 / (1.0 + jnp.exp(lam))
    return sp, dsp


def _earlier(ext, j):
    return pltpu.roll(ext, j, 0)[SUBLANES:, :]


def _later(ext, j):
    n = ext.shape[0]
    return pltpu.roll(ext, n - j, 0)[:n - SUBLANES, :]


def _taps(v, halo, K):
    ext = jnp.concatenate([halo, v], axis=0)
    return [v] + [_earlier(ext, j) for j in range(1, K)]


def _block_diag(vb, w_ref, nh, hd):
    return jnp.concatenate(
        [jnp.dot(vb[:, h * hd:(h + 1) * hd], w_ref[h], preferred_element_type=F32) for h in range(nh)], axis=1)


def _lru_gates(xc, wa_ref, ba_ref, wi_ref, bi_ref, sp, nh, hd):
    xcb = xc.astype(BF16)
    r = _sigmoid(_block_diag(xcb, wa_ref, nh, hd) + ba_ref[...])
    ig = _sigmoid(_block_diag(xcb, wi_ref, nh, hd) + bi_ref[...])
    log_a = -LRU_C * r * sp
    a = jnp.exp(log_a)
    mult = jnp.sqrt(_neg_expm1(2.0 * log_a))
    return xcb, r, ig, a, mult


def _mix_fwd(z, cw, cb, wa, ba, wi, bi, lam, sw, glo, gso, name):
    T = z.shape[0]
    DL = cb.shape[1]
    DS = gso.shape[1]
    NH, HD = wa.shape[0], wa.shape[1]
    KL, KS = cw.shape[0], sw.shape[0]
    tt = _tile(T, 128, 16)
    o_g, o_b, o_c, o_x = DL, 2 * DL, 2 * DL + DS, 2 * DL + 2 * DS

    def body(z_ref, cw_ref, cb_ref, wa_ref, ba_ref, wi_ref, bi_ref, lam_ref, sw_ref, glo_ref, gso_ref,
             h_ref, y_ref, cx_ref, cp_ref, ch_ref):
        @pl.when(pl.program_id(0) == 0)
        def _():
            cx_ref[...] = jnp.zeros_like(cx_ref)
            cp_ref[...] = jnp.zeros_like(cp_ref)
            ch_ref[...] = jnp.zeros_like(ch_ref)

        def zcol(o, n):
            return z_ref[:, o:o + n].astype(F32)

        lx = zcol(0, DL)
        xs = _taps(lx, cx_ref[...], KL)
        cx_ref[...] = lx[tt - SUBLANES:, :]
        xc = cb_ref[...] + xs[0] * cw_ref[KL - 1:KL, :]
        for j in range(1, KL):
            xc = xc + xs[j] * cw_ref[KL - 1 - j:KL - j, :]
        sp, _ = _softplus_neg(lam_ref[...])
        _, _, ig, a, mult = _lru_gates(xc, wa_ref, ba_ref, wi_ref, bi_ref, sp, NH, HD)
        b = mult * (ig * xc)
        rows = lax.broadcasted_iota(jnp.int32, (tt, DL), 0)
        s = 1
        while s < tt:
            keep = rows >= s
            b = jnp.where(keep, a * pltpu.roll(b, s, 0) + b, b)
            a = jnp.where(keep, a * pltpu.roll(a, s, 0), a)
            s *= 2
        h = a * ch_ref[SUBLANES - 1:SUBLANES, :] + b
        ch_ref[...] = h[tt - SUBLANES:, :]
        h_ref[...] = h
        ge, _ = _gelu_and_grad(zcol(o_g, DL))
        ylru = h * ge
        y_ref[:, 0:DL] = (ylru * _rstd(ylru) * glo_ref[...]).astype(BF16)

        p = zcol(o_c, DS) * zcol(o_x, DS)
        ps = _taps(p, cp_ref[...], KS)
        cp_ref[...] = p[tt - SUBLANES:, :]
        cv = ps[0] * sw_ref[KS - 1:KS, :]
        for j in range(1, KS):
            cv = cv + ps[j] * sw_ref[KS - 1 - j:KS - j, :]
        ysc = zcol(o_b, DS) * cv
        y_ref[:, DL:DL + DS] = (ysc * _rstd(ysc) * gso_ref[...]).astype(BF16)

    def full(shape):
        return pl.BlockSpec(shape, lambda t: (0,) * len(shape))

    return pl.pallas_call(
        body, name=name, grid=(T // tt,),
        in_specs=[pl.BlockSpec((tt, z.shape[1]), lambda t: (t, 0)),
                  full(cw.shape), full(cb.shape), full(wa.shape), full(ba.shape), full(wi.shape), full(bi.shape),
                  full(lam.shape), full(sw.shape), full(glo.shape), full(gso.shape)],
        out_specs=[pl.BlockSpec((tt, DL), lambda t: (t, 0)), pl.BlockSpec((tt, DL + DS), lambda t: (t, 0))],
        out_shape=[jax.ShapeDtypeStruct((T, DL), F32), jax.ShapeDtypeStruct((T, DL + DS), BF16)],
        scratch_shapes=[pltpu.VMEM((SUBLANES, DL), F32), pltpu.VMEM((SUBLANES, DS), F32),
                        pltpu.VMEM((SUBLANES, DL), F32)],
        compiler_params=_params("arbitrary"),
    )(z, cw, cb, wa, ba, wi, bi, lam, sw, glo, gso)


V_BA, V_BI, V_LAM, V_CB, V_CW, V_SW, V_GLO, V_GSO, V_ROWS = 0, 1, 2, 3, 4, 8, 11, 12, 16


def _mix_bwd(z, h, dy, cw, cb, wa, ba, wi, bi, lam, sw, glo, gso, name):
    T = z.shape[0]
    DL = cb.shape[1]
    DS = gso.shape[1]
    NH, HD = wa.shape[0], wa.shape[1]
    KL, KS = cw.shape[0], sw.shape[0]
    tt = _tile(T, 64, 16)
    nt = T // tt
    ZH = 2 * SUBLANES
    o_g, o_b, o_c, o_x = DL, 2 * DL, 2 * DL + DS, 2 * DL + 2 * DS

    def body(z_ref, zh_ref, h_ref, hh_ref, dy_ref, cw_ref, cb_ref, wa_ref, ba_ref, wi_ref, bi_ref, lam_ref,
             sw_ref, glo_ref, gso_ref, dz_ref, dwa_ref, dwi_ref, vec_ref, cdx_ref, cdc_ref, cdh_ref):
        i = pl.program_id(0)
        tr = nt - 1 - i

        @pl.when(i == 0)
        def _():
            dwa_ref[...] = jnp.zeros_like(dwa_ref)
            dwi_ref[...] = jnp.zeros_like(dwi_ref)
            vec_ref[...] = jnp.zeros_like(vec_ref)
            cdx_ref[...] = jnp.zeros_like(cdx_ref)
            cdc_ref[...] = jnp.zeros_like(cdc_ref)
            cdh_ref[...] = jnp.zeros_like(cdh_ref)

        def acc_row(r, v):
            vec_ref[pl.ds(r, 1), :] += jnp.sum(v, axis=0, keepdims=True)

        has_prev = tr > 0
        rows = lax.broadcasted_iota(jnp.int32, (tt, DL), 0)

        def zcol(o, n):
            return z_ref[:, o:o + n].astype(F32)

        def zhalo(o, n):
            return jnp.where(has_prev, zh_ref[:, o:o + n].astype(F32)[SUBLANES:, :], 0.0)

        lx = zcol(0, DL)
        xs = _taps(lx, zhalo(0, DL), KL)
        xc = cb_ref[...] + xs[0] * cw_ref[KL - 1:KL, :]
        for j in range(1, KL):
            xc = xc + xs[j] * cw_ref[KL - 1 - j:KL - j, :]
        sp, dsp = _softplus_neg(lam_ref[...])
        xcb, r, ig, a, mult = _lru_gates(xc, wa_ref, ba_ref, wi_ref, bi_ref, sp, NH, HD)
        hv = h_ref[...]
        hprev = _earlier(jnp.concatenate([jnp.where(has_prev, hh_ref[...], 0.0), hv], axis=0), 1)
        gate = zcol(o_g, DL)
        ge, dge = _gelu_and_grad(gate)
        ylru = hv * ge

        d_ylru, dglo = _rms_bwd(dy_ref[:, 0:DL].astype(F32), ylru, glo_ref[...])
        vec_ref[pl.ds(V_GLO, 1), :] += dglo
        dz_ref[:, o_g:o_g + DL] = (d_ylru * hv * dge).astype(BF16)
        bq = d_ylru * ge
        aq = jnp.where(rows == tt - 1, 1.0, pltpu.roll(a, tt - 1, 0))
        s = 1
        while s < tt:
            keep = rows < tt - s
            bq = jnp.where(keep, aq * pltpu.roll(bq, tt - s, 0) + bq, bq)
            aq = jnp.where(keep, aq * pltpu.roll(aq, tt - s, 0), aq)
            s *= 2
        dhh = bq + aq * cdh_ref[0:1, :]
        cdh_ref[0:1, :] = a[0:1, :] * dhh[0:1, :]

        da = dhh * hprev
        dmult = dhh * (ig * xc)
        d_i = dhh * mult * xc
        dxc = dhh * mult * ig
        dlog = da * a - dmult * (a * a) / mult
        acc_row(V_LAM, dlog * (-LRU_C * r) * dsp)
        dpa = dlog * (-LRU_C * sp) * r * (1.0 - r)
        dpi = d_i * ig * (1.0 - ig)
        acc_row(V_BA, dpa)
        acc_row(V_BI, dpi)
        dpab = dpa.astype(BF16)
        dpib = dpi.astype(BF16)
        back = []
        for hh in range(NH):
            sl = slice(hh * HD, (hh + 1) * HD)
            dwa_ref[hh] += _dot_tn(xcb[:, sl], dpab[:, sl])
            dwi_ref[hh] += _dot_tn(xcb[:, sl], dpib[:, sl])
            back.append(_dot_nt(dpab[:, sl], wa_ref[hh]) + _dot_nt(dpib[:, sl], wi_ref[hh]))
        dxc = dxc + jnp.concatenate(back, axis=1)

        acc_row(V_CB, dxc)
        extd = jnp.concatenate([dxc, cdx_ref[...]], axis=0)
        cdx_ref[...] = dxc[0:SUBLANES, :]
        dlx = dxc * cw_ref[KL - 1:KL, :]
        acc_row(V_CW + KL - 1, dxc * xs[0])
        for j in range(1, KL):
            dlx = dlx + _later(extd, j) * cw_ref[KL - 1 - j:KL - j, :]
            acc_row(V_CW + KL - 1 - j, dxc * xs[j])
        dz_ref[:, 0:DL] = dlx.astype(BF16)

        sb = zcol(o_b, DS)
        sc = zcol(o_c, DS)
        sx = zcol(o_x, DS)
        p = sc * sx
        ps = _taps(p, zhalo(o_c, DS) * zhalo(o_x, DS), KS)
        cv = ps[0] * sw_ref[KS - 1:KS, :]
        for j in range(1, KS):
            cv = cv + ps[j] * sw_ref[KS - 1 - j:KS - j, :]
        d_ysc, dgso = _rms_bwd(dy_ref[:, DL:DL + DS].astype(F32), sb * cv, gso_ref[...])
        vec_ref[pl.ds(V_GSO, 1), :] += dgso
        dz_ref[:, o_b:o_b + DS] = (d_ysc * cv).astype(BF16)
        dcv = d_ysc * sb
        extc = jnp.concatenate([dcv, cdc_ref[...]], axis=0)
        cdc_ref[...] = dcv[0:SUBLANES, :]
        dp = dcv * sw_ref[KS - 1:KS, :]
        acc_row(V_SW + KS - 1, dcv * ps[0])
        for j in range(1, KS):
            dp = dp + _later(extc, j) * sw_ref[KS - 1 - j:KS - j, :]
            acc_row(V_SW + KS - 1 - j, dcv * ps[j])
        dz_ref[:, o_c:o_c + DS] = (dp * sx).astype(BF16)
        dz_ref[:, o_x:o_x + DS] = (dp * sc).astype(BF16)

    def full(shape):
        return pl.BlockSpec(shape, lambda t: (0,) * len(shape))

    def rev(t):
        return nt - 1 - t

    def halo(t, rows):
        return jnp.maximum(rev(t) * (tt // rows) - 1, 0)

    return pl.pallas_call(
        body, name=name, grid=(nt,),
        in_specs=[pl.BlockSpec((tt, z.shape[1]), lambda t: (rev(t), 0)),
                  pl.BlockSpec((ZH, z.shape[1]), lambda t: (halo(t, ZH), 0)),
                  pl.BlockSpec((tt, DL), lambda t: (rev(t), 0)),
                  pl.BlockSpec((SUBLANES, DL), lambda t: (halo(t, SUBLANES), 0)),
                  pl.BlockSpec((tt, DL + DS), lambda t: (rev(t), 0)),
                  full(cw.shape), full(cb.shape), full(wa.shape), full(ba.shape), full(wi.shape), full(bi.shape),
                  full(lam.shape), full(sw.shape), full(glo.shape), full(gso.shape)],
        out_specs=[pl.BlockSpec((tt, z.shape[1]), lambda t: (rev(t), 0)),
                   full(wa.shape), full(wi.shape), full((V_ROWS, DL))],
        out_shape=[jax.ShapeDtypeStruct(z.shape, BF16), jax.ShapeDtypeStruct(wa.shape, F32),
                   jax.ShapeDtypeStruct(wi.shape, F32), jax.ShapeDtypeStruct((V_ROWS, DL), F32)],
        scratch_shapes=[pltpu.VMEM((SUBLANES, DL), F32), pltpu.VMEM((SUBLANES, DS), F32),
                        pltpu.VMEM((SUBLANES, DL), F32)],
        compiler_params=_params("arbitrary"),
    )(z, z, h, h, dy, cw, cb, wa, ba, wi, bi, lam, sw, glo, gso)


def _pair_add(p, r1, c, name):
    G, R, C = r1.shape
    tr = _tile(R, 256, 16)
    tc = _tile(C, 1408, LANES)

    def body(c_ref, p_ref, r_ref, o_ref):
        o_ref[...] = (p_ref[...].astype(F32) + r_ref[...].astype(F32)).astype(BF16)

    blk = (None, tr, tc)
    return pl.pallas_call(
        body, name=name,
        grid_spec=pltpu.PrefetchScalarGridSpec(
            num_scalar_prefetch=1, grid=(G, R // tr, C // tc),
            in_specs=[pl.BlockSpec(blk, lambda g, i, j, cr: (2 * g + cr[0], i, j)),
                      pl.BlockSpec(blk, lambda g, i, j, cr: (g, i, j))],
            out_specs=pl.BlockSpec(blk, lambda g, i, j, cr: (g, i, j))),
        out_shape=jax.ShapeDtypeStruct((G, R, C), BF16),
        compiler_params=_params("parallel", "parallel", "parallel"),
    )(c, p, r1)


def _quad_add(s, r2, qc, axis, name):
    _, R, W = r2.shape
    tr = _tile(R, 256, 16)

    def body(qc_ref, s_ref, r0_ref, r1_ref, r2_ref, o_ref):
        o_ref[...] = ((s_ref[...].astype(F32) + r0_ref[...].astype(F32)) + r1_ref[...].astype(F32)) + r2_ref[...].astype(F32)

    blk = (None, tr, W)
    if axis == 1:
        own = pl.BlockSpec(blk, lambda i, qr: (0, i, qr[0]))
    else:
        own = pl.BlockSpec(blk, lambda i, qr: (qr[0], i, 0))
    return pl.pallas_call(
        body, name=name,
        grid_spec=pltpu.PrefetchScalarGridSpec(
            num_scalar_prefetch=1, grid=(R // tr,),
            in_specs=[own] + [pl.BlockSpec(blk, lambda i, qr, j=j: (j, i, 0)) for j in range(3)],
            out_specs=pl.BlockSpec(blk, lambda i, qr: (qr[1], i, 0))),
        out_shape=jax.ShapeDtypeStruct((2, R, W), F32),
        compiler_params=_params("parallel"),
    )(qc, s, r2, r2, r2)


CAST_BLOCKS = 4


def _cast_into_full(shards, qc, axes, name, carry=None):
    M = len(shards)
    nb = CAST_BLOCKS

    def body(qc_ref, *refs):
        for s_ref, o_ref in zip(refs[:M], refs[M:]):
            o_ref[...] = s_ref[...].astype(BF16)

    in_specs, out_specs, out_shape = [], [], []
    for s, ax in zip(shards, axes):
        R, W = s.shape
        Rh = R // 2
        tr = Rh // nb
        in_specs.append(pl.BlockSpec((tr, W), lambda hf, i, qr: (hf * nb + i, 0)))
        if ax == 1:
            out_shape.append(jax.ShapeDtypeStruct((2, Rh, 4 * W), BF16))
            out_specs.append(pl.BlockSpec((None, tr, W), lambda hf, i, qr: (hf, i, qr[0])))
        else:
            out_shape.append(jax.ShapeDtypeStruct((8, Rh, W), BF16))
            out_specs.append(pl.BlockSpec((None, tr, W), lambda hf, i, qr: (2 * qr[0] + hf, i, 0)))
    return _call(body, name, (2, nb), in_specs, out_specs, out_shape, [], ("parallel", "parallel"), shards, carry, (qc,))


def _adamw(w, g, m, v, name):
    R, C = w.shape
    tr = _tile(R, 256, SUBLANES)
    tc = _tile(C, 2048, LANES)
    c1 = 1.0 - ADAM_B1 ** ADAM_STEP
    c2 = 1.0 - ADAM_B2 ** ADAM_STEP

    def body(w_ref, g_ref, m_ref, v_ref, d_ref, mo_ref, vo_ref, go_ref):
        gv = g_ref[...]
        go_ref[...] = gv
        mn = ADAM_B1 * m_ref[...] + (1.0 - ADAM_B1) * gv
        vn = ADAM_B2 * v_ref[...] + (1.0 - ADAM_B2) * (gv * gv)
        mo_ref[...] = mn
        vo_ref[...] = vn
        d_ref[...] = -ADAM_LR * ((mn / c1) / (jnp.sqrt(vn / c2) + ADAM_EPS) + ADAM_WD * w_ref[...])

    blk = pl.BlockSpec((tr, tc), lambda i, j: (i, j))
    sh = jax.ShapeDtypeStruct((R, C), F32)
    return pl.pallas_call(
        body, name=name, grid=(R // tr, C // tc),
        in_specs=[blk] * 4, out_specs=[blk] * 4, out_shape=[sh] * 4,
        compiler_params=_params("parallel", "parallel"),
    )(w, g, m, v)


def _other_chips(x, y):
    return [(1 - x, y), (x, 1 - y), (1 - x, 1 - y)]


def _remote(src, dst, send_sems, recv_sems, idx, dev):
    return pltpu.make_async_remote_copy(src_ref=src, dst_ref=dst, send_sem=send_sems.at[idx], recv_sem=recv_sems.at[idx],
                                        device_id=dev, device_id_type=MESH)


def _gather_carry(fulls, axes):
    M = len(fulls)

    def win(outs, m, qq, cc):
        if axes[m] == 1:
            W = fulls[m].shape[2] // 4
            return outs[m].at[cc, :, pl.ds(pl.multiple_of(qq * W, LANES), W)]
        return outs[m].at[2 * qq + cc]

    def ici(outs, sems, m, j, src_q):
        x, y, c = _me()
        cx, cy = _other_chips(x, y)[j]
        blk = win(outs, m, src_q, c)
        return _remote(blk, blk, sems[0], sems[1], 6 * m + j, (cx, cy, c))

    def d2d(outs, sems, m, j, half):
        x, y, c = _me()
        cx, cy = _other_chips(x, y)[j]
        blk = win(outs, m, 2 * cx + cy, half)
        return _remote(blk, blk, sems[0], sems[1], 6 * m + 3 + j, (x, y, 1 - c))

    def start(ins, outs, sems):
        x, y, c = _me()
        for m in range(M):
            for j in range(3):
                ici(outs, sems, m, j, 2 * x + y).start()

    def middle(ins, outs, sems):
        x, y, c = _me()
        for m in range(M):
            for j, (cx, cy) in enumerate(_other_chips(x, y)):
                ici(outs, sems, m, j, 2 * cx + cy).wait_recv()
                d2d(outs, sems, m, j, c).start()

    def finish(ins, outs, sems):
        x, y, c = _me()
        for m in range(M):
            for j in range(3):
                d2d(outs, sems, m, j, 1 - c).wait_recv()
        for m in range(M):
            for j in range(3):
                ici(outs, sems, m, j, 2 * x + y).wait_send()
                d2d(outs, sems, m, j, c).wait_send()

    return _Carry(fulls, [jax.ShapeDtypeStruct(f.shape, f.dtype) for f in fulls], {m: m for m in range(M)},
                  [pltpu.SemaphoreType.DMA((6 * M,)), pltpu.SemaphoreType.DMA((6 * M,))], start, finish, middle)


def _gather_two_way_carry(fulls, axes):
    M = len(fulls)

    def part(outs, m, qq, cc, p):
        Rp = fulls[m].shape[1] // 2
        rows = pl.ds(p * Rp, Rp)
        if axes[m] == 1:
            W = fulls[m].shape[2] // 4
            return outs[m].at[cc, rows, pl.ds(pl.multiple_of(qq * W, LANES), W)]
        return outs[m].at[2 * qq + cc, rows, :]

    def copies(outs, sems):
        x, y, c = _me()
        q, qx, qy, qd = 2 * x + y, 2 * (1 - x) + y, 2 * x + (1 - y), 2 * (1 - x) + (1 - y)
        xn, yn, sib = (1 - x, y, c), (x, 1 - y, c), (x, y, 1 - c)
        table = {}
        for m in range(M):
            def cp(blk, k, dev):
                return _remote(blk, blk, sems[0], sems[1], 12 * m + k, dev)
            own0, own1 = part(outs, m, q, c, 0), part(outs, m, q, c, 1)
            table[m] = dict(
                to=[cp(own0, 0, xn), cp(own1, 1, yn), cp(own1, 2, xn), cp(own0, 3, yn)],
                landed=[cp(part(outs, m, qx, c, 0), 0, xn), cp(part(outs, m, qy, c, 1), 1, yn),
                        cp(part(outs, m, qx, c, 1), 2, xn), cp(part(outs, m, qy, c, 0), 3, yn),
                        cp(part(outs, m, qd, c, 0), 4, yn), cp(part(outs, m, qd, c, 1), 5, xn)],
                passed=[cp(part(outs, m, qx, c, 0), 4, yn), cp(part(outs, m, qy, c, 1), 5, xn)],
                handed=[cp(part(outs, m, qq, c, p), 6 + k, sib)
                        for k, (qq, p) in enumerate([(qx, 0), (qy, 1), (qx, 1), (qy, 0), (qd, 0), (qd, 1)])],
                taken=[cp(part(outs, m, qq, 1 - c, p), 6 + k, sib)
                       for k, (qq, p) in enumerate([(qx, 0), (qy, 1), (qx, 1), (qy, 0), (qd, 0), (qd, 1)])])
        return table

    def start(ins, outs, sems):
        t = copies(outs, sems)
        for m in range(M):
            for cp in t[m]['to']:
                cp.start()

    def finish(ins, outs, sems):
        t = copies(outs, sems)
        for m in range(M):
            for k in range(4):
                t[m]['landed'][k].wait_recv()
                if k < 2:
                    t[m]['passed'][k].start()
                t[m]['handed'][k].start()
        for m in range(M):
            for k in (4, 5):
                t[m]['landed'][k].wait_recv()
                t[m]['handed'][k].start()
        for m in range(M):
            for cp in t[m]['taken']:
                cp.wait_recv()
            for cp in t[m]['to'] + t[m]['passed'] + t[m]['handed']:
                cp.wait_send()

    return _Carry(fulls, [jax.ShapeDtypeStruct(f.shape, f.dtype) for f in fulls], {m: m for m in range(M)},
                  [pltpu.SemaphoreType.DMA((12 * M,)), pltpu.SemaphoreType.DMA((12 * M,))], start, finish)


def _pair_exchange_carry(parts):
    M = len(parts)
    groups = [p.shape[0] // 2 for p in parts]
    base = [sum(groups[:m]) for m in range(M)]
    out_shape = [jax.ShapeDtypeStruct((g,) + p.shape[1:], p.dtype) for g, p in zip(groups, parts)]

    def copies(ins, outs, sems):
        x, y, c = _me()
        return [_remote(ins[m].at[2 * g + 1 - c], outs[m].at[g], sems[0], sems[1], base[m] + g, (x, y, 1 - c))
                for m in range(M) for g in range(groups[m])]

    def start(ins, outs, sems):
        for cp in copies(ins, outs, sems):
            cp.start()

    def finish(ins, outs, sems):
        for cp in copies(ins, outs, sems):
            cp.wait()

    n = sum(groups)
    return _Carry(parts, out_shape, {}, [pltpu.SemaphoreType.DMA((n,)), pltpu.SemaphoreType.DMA((n,))], start, finish)


def _chip_exchange_carry(sums, axes):
    M = len(sums)
    out_shape = []
    for s, ax in zip(sums, axes):
        _, Rh, C = s.shape
        out_shape.append(jax.ShapeDtypeStruct((3, Rh, C // 4 if ax == 1 else C), s.dtype))

    def copies(ins, outs, sems):
        x, y, c = _me()
        cps = []
        for m in range(M):
            for j, (cx, cy) in enumerate(_other_chips(x, y)):
                qj = 2 * cx + cy
                if axes[m] == 1:
                    W = sums[m].shape[2] // 4
                    src = ins[m].at[0, :, pl.ds(pl.multiple_of(qj * W, LANES), W)]
                else:
                    src = ins[m].at[qj]
                cps.append(_remote(src, outs[m].at[j], sems[0], sems[1], 3 * m + j, (cx, cy, c)))
        return cps

    def start(ins, outs, sems):
        for cp in copies(ins, outs, sems):
            cp.start()

    def finish(ins, outs, sems):
        for cp in copies(ins, outs, sems):
            cp.wait()

    return _Carry(sums, out_shape, {}, [pltpu.SemaphoreType.DMA((3 * M,)), pltpu.SemaphoreType.DMA((3 * M,))],
                  start, finish)


def _pair_share_carry(bufs):
    M = len(bufs)

    def start(ins, outs, sems):
        x, y, c = _me()
        for m in range(M):
            _remote(outs[m].at[c], outs[m].at[c], sems[0], sems[1], m, (x, y, 1 - c)).start()

    def finish(ins, outs, sems):
        x, y, c = _me()
        for m in range(M):
            _remote(outs[m].at[c], outs[m].at[c], sems[0], sems[1], m, (x, y, 1 - c)).wait_send()
            _remote(outs[m].at[1 - c], outs[m].at[1 - c], sems[0], sems[1], m, (x, y, 1 - c)).wait_recv()

    return _Carry(bufs, [jax.ShapeDtypeStruct(b.shape, b.dtype) for b in bufs], {m: m for m in range(M)},
                  [pltpu.SemaphoreType.DMA((M,)), pltpu.SemaphoreType.DMA((M,))], start, finish)


def _allreduce_small(v, name):
    R, W = v.shape
    Rh = R // 2

    def body(v_ref, o_ref, sib, quad, send_sems, recv_sems):
        x, y, c = _me()
        q = 2 * x + y
        sibling = (x, y, 1 - c)
        pair = _remote(v_ref, sib, send_sems, recv_sems, 0, sibling)
        pair.start()
        pair.wait()
        mine = pl.ds(pl.multiple_of(c * Rh, SUBLANES), Rh)
        quad[0] = v_ref[mine, :] + sib[mine, :]
        cps = []
        for k in (1, 2, 3):
            peer = (1 - x if k & 2 else x, 1 - y if k & 1 else y, c)
            cps.append(_remote(quad.at[0], quad.at[k], send_sems, recv_sems, k, peer))
            cps[-1].start()
        for cp in cps:
            cp.wait()
        acc = quad[q]
        for p in (1, 2, 3):
            acc = acc + quad[jnp.bitwise_xor(q, p)]
        o_ref[mine, :] = acc
        theirs = pl.ds(pl.multiple_of((1 - c) * Rh, SUBLANES), Rh)
        done = _remote(o_ref.at[mine, :], o_ref.at[mine, :], send_sems, recv_sems, 4, sibling)
        done.start()
        done.wait_send()
        _remote(o_ref.at[theirs, :], o_ref.at[theirs, :], send_sems, recv_sems, 4, sibling).wait_recv()

    vm = pl.BlockSpec(memory_space=pltpu.VMEM)
    return pl.pallas_call(
        body, name=name, in_specs=[vm], out_specs=vm, out_shape=jax.ShapeDtypeStruct((R, W), F32),
        scratch_shapes=[pltpu.VMEM((R, W), F32), pltpu.VMEM((4, Rh, W), F32),
                        pltpu.SemaphoreType.DMA((5,)), pltpu.SemaphoreType.DMA((5,))],
        compiler_params=pltpu.CompilerParams(vmem_limit_bytes=VMEM_LIMIT),
    )(v)


def _pack(pieces):
    flat = []
    for p in pieces:
        p = p.reshape(-1).astype(F32)
        pad = (-p.shape[0]) % PACK_ALIGN
        flat.append(jnp.pad(p, (0, pad)).reshape(-1, PACK_W))
    if sum(f.shape[0] for f in flat) % (2 * SUBLANES):
        flat.append(jnp.zeros((SUBLANES, PACK_W), F32))
    return jnp.concatenate(flat, axis=0)


def _unpack(packed, shapes):
    out, row = [], 0
    for shp in shapes:
        n = math.prod(shp)
        rows = -(-n // PACK_ALIGN) * SUBLANES
        out.append(packed[row:row + rows].reshape(-1)[:n].reshape(shp))
        row += rows
    return out


def kernel(x, ffn1_norm, ffn1_w_gate, ffn1_w_up, ffn1_w_down, mix_norm, w_in, lru_conv_w, lru_conv_b, lru_w_a, lru_b_a, lru_w_i, lru_b_i, lru_lambda, sc_conv_w, lru_out_norm, sc_out_norm, w_out, ffn2_norm, ffn2_w_gate, ffn2_w_up, ffn2_w_down, final_norm, loss_target, m_ffn1_norm, m_ffn1_w_gate, m_ffn1_w_up, m_ffn1_w_down, m_mix_norm, m_w_in, m_lru_conv_w, m_lru_conv_b, m_lru_w_a, m_lru_b_a, m_lru_w_i, m_lru_b_i, m_lru_lambda, m_sc_conv_w, m_lru_out_norm, m_sc_out_norm, m_w_out, m_ffn2_norm, m_ffn2_w_gate, m_ffn2_w_up, m_ffn2_w_down, m_final_norm, v_ffn1_norm, v_ffn1_w_gate, v_ffn1_w_up, v_ffn1_w_down, v_mix_norm, v_w_in, v_lru_conv_w, v_lru_conv_b, v_lru_w_a, v_lru_b_a, v_lru_w_i, v_lru_b_i, v_lru_lambda, v_sc_conv_w, v_lru_out_norm, v_sc_out_norm, v_w_out, v_ffn2_norm, v_ffn2_w_gate, v_ffn2_w_up, v_ffn2_w_down, v_final_norm):
    vals = locals()
    w = {n: vals[n] for n in WEIGHTS}
    mom = {n: vals["m_" + n] for n in WEIGHTS}
    var = {n: vals["v_" + n] for n in WEIGHTS}

    xi, yi, ci = _me()
    qi = 2 * xi + yi
    c_arr = jnp.reshape(ci, (1,)).astype(jnp.int32)
    qc_arr = jnp.stack([qi, ci]).astype(jnp.int32)

    T, D = x.shape[1], x.shape[2]
    xt = x.reshape(T, D)
    target = loss_target.reshape(T, D)
    DL = lru_conv_b.shape[-1]
    NH, HD = lru_w_a.shape[1], lru_w_a.shape[2]
    KL, KS = lru_conv_w.shape[1], sc_conv_w.shape[1]
    DLq = lru_conv_w.shape[2]

    axis_of = dict(zip(BIG, BIG_AXIS))
    first_names = ['ffn1_w_gate', 'ffn1_w_up', 'ffn1_w_down']
    later_names = ['w_in', 'w_out', 'ffn2_w_gate', 'ffn2_w_up', 'ffn2_w_down']

    def unview(n, g):
        return g.reshape(2 * g.shape[1], g.shape[2]) if axis_of[n] == 1 else g.reshape(8 * g.shape[1], g.shape[2])

    first_axes, later_axes = [axis_of[n] for n in first_names], [axis_of[n] for n in later_names]
    placed = _cast_into_full([w[n][0] for n in first_names], qc_arr, first_axes, "cast_ffn1_weights")
    res = _cast_into_full([w[n][0] for n in later_names], qc_arr, later_axes, "cast_later_weights",
                          _gather_two_way_carry(placed, first_axes))
    full = {n: unview(n, g) for n, g in zip(first_names, res[len(later_names):])}
    gather_later = _gather_carry(res[:len(later_names)], later_axes)

    taps = jnp.zeros((2 * SUBLANES, DL), F32)
    taps = lax.dynamic_update_slice(taps, lru_conv_w[0], (0, qi * DLq))
    taps = lax.dynamic_update_slice(taps, sc_conv_w[0], (KL, qi * DLq))
    taps = _allreduce_small(jnp.where(ci == 0, taps, 0.0), "gather_conv_taps")
    cw, sw = taps[0:KL], taps[KL:KL + KS]

    cb = lru_conv_b
    wa, wi = lru_w_a[0].astype(BF16), lru_w_i[0].astype(BF16)
    ba, bi = lru_b_a.reshape(1, DL), lru_b_i.reshape(1, DL)
    mix_args = (cw, cb, wa, ba, wi, bi, lru_lambda, sw, lru_out_norm, sc_out_norm)
    gf = final_norm.reshape(1, D)

    res = _ffn_fwd(xt, ffn1_norm, full['ffn1_w_gate'], full['ffn1_w_up'], full['ffn1_w_down'], "ffn1_fwd", gather_later)
    x1, n1, G1, U1 = res[:4]
    full.update({n: unview(n, g) for n, g in zip(later_names, res[4:])})
    n2, z = _norm_mm(x1, mix_norm, full['w_in'], "mix_in_proj")
    h, ymix = _mix_fwd(z, *mix_args, "mix_fwd")
    x2 = _mm_fullk(ymix, full['w_out'], False, x1, F32, "mix_out_proj")[0]
    x3, n3, G2, U2 = _ffn_fwd(x2, ffn2_norm, full['ffn2_w_gate'], full['ffn2_w_up'], full['ffn2_w_down'], "ffn2_fwd")
    dx3, d3b, sqerr, dgf = _loss_head(x3, gf, target, "loss_head")

    grads, halves, shared = {}, {}, {}

    def pair_carry(names):
        parts = []
        for n in names:
            R, C = grads[n].shape
            parts.append(grads[n].reshape(2, R // 2, C) if axis_of[n] == 1 else grads[n].reshape(8, R // 8, C))
        return parts, _pair_exchange_carry(parts)

    def pair_add(names, parts, recv):
        return [_pair_add(p, r, c_arr, "grad_pair_add_" + n) for n, p, r in zip(names, parts, recv)]

    def chip_carry(names, sums):
        return _chip_exchange_carry(sums, [axis_of[n] for n in names])

    def chip_add(names, sums, recv):
        for n, s, r in zip(names, sums, recv):
            halves[n] = _quad_add(s, r, qc_arr, axis_of[n], "grad_chip_add_" + n)

    dG2, dU2, H2, dg_ffn2, dx2, dx2b = _ffn_bwd_fused(
        d3b, G2, U2, full['ffn2_w_gate'], full['ffn2_w_up'], full['ffn2_w_down'], x2, ffn2_norm, dx3, 1.0, "ffn2_bwd")
    grads['ffn2_w_gate'] = _mm_tn(n3, dG2, "ffn2_dwg")[0]
    grads['ffn2_w_up'] = _mm_tn(n3, dU2, "ffn2_dwu")[0]
    grads['ffn2_w_down'] = _mm_tn(H2, d3b, "ffn2_dwd")[0]
    names_a = ['ffn2_w_gate', 'ffn2_w_up', 'ffn2_w_down', 'w_out', 'w_in']
    parts, carry = pair_carry(names_a[:3])
    res = _mm_fullk(dx2b, full['w_out'], True, None, BF16, "mix_out_bwd", carry)
    dy = res[0]
    sums_a = pair_add(names_a[:3], parts, res[1:])
    grads['w_out'] = _mm_tn(ymix, dx2b, "mix_dwout")[0]
    dz, dwa, dwi, vec = _mix_bwd(z, h, dy, *mix_args, "mix_bwd")
    grads['w_in'] = _mm_tn(n2, dz, "mix_dwin")[0]
    parts, carry = pair_carry(names_a[3:])
    res = _mm_nt_norm_bwd(dz, full['w_in'], x1, mix_norm, dx2, FFN_RESIDUAL_SCALE, "mix_in_bwd", carry)
    dg_mix, dx1, d1b = res[:3]
    sums_a += pair_add(names_a[3:], parts, res[3:])

    res = _ffn_bwd_act(d1b, G1, U1, full['ffn1_w_gate'], full['ffn1_w_up'], full['ffn1_w_down'], "ffn1_bwd",
                       chip_carry(names_a, sums_a))
    dG1, dU1, H1, dn1 = res[:4]
    chip_add(names_a, sums_a, res[4:])
    dx0, _, dg_ffn1 = _rms_bwd_res(dn1, xt, ffn1_norm, dx1, 1.0, "ffn1_norm_bwd")
    res = _mm_tn(n1, dG1, "ffn1_dwg", _pair_share_carry([halves[n] for n in names_a]))
    grads['ffn1_w_gate'] = res[0]
    shared.update(zip(names_a, res[1:]))
    names_b = ['ffn1_w_gate', 'ffn1_w_up']
    parts_g, carry = pair_carry(names_b[:1])
    res = _mm_tn(n1, dU1, "ffn1_dwu", carry)
    grads['ffn1_w_up'] = res[0]
    sums_b = pair_add(names_b[:1], parts_g, res[1:])
    parts_u, carry = pair_carry(names_b[1:])
    sums_b += pair_add(names_b[1:], parts_u, _run_carry(carry, "grad_pair_exchange_ffn1_up"))
    res = _mm_tn(H1, d1b, "ffn1_dwd", chip_carry(names_b, sums_b))
    grads['ffn1_w_down'] = res[0]
    chip_add(names_b, sums_b, res[1:])
    names_c = ['ffn1_w_down']
    parts, carry = pair_carry(names_c)
    sums_c = pair_add(names_c, parts, _run_carry(carry, "grad_pair_exchange_ffn1_out"))
    chip_add(names_c, sums_c, _run_carry(chip_carry(names_c, sums_c), "grad_chip_exchange_ffn1_out"))
    names_bc = names_b + names_c
    shared.update(zip(names_bc, _run_carry(_pair_share_carry([halves[n] for n in names_bc]), "grad_pair_share")))
    out_g, out_d, out_m, out_v = {}, {}, {}, {}
    for n in BIG:
        shp = w[n].shape
        outs = _adamw(w[n][0], shared[n].reshape(shp[1], shp[2]), mom[n][0], var[n][0], "adamw_" + n)
        out_d[n], out_m[n], out_v[n], out_g[n] = (a.reshape(shp) for a in outs)

    small = [n for n in WEIGHTS if n not in BIG]
    local_small = {
        'ffn1_norm': dg_ffn1, 'mix_norm': dg_mix, 'lru_conv_w': vec[V_CW:V_CW + KL], 'lru_conv_b': vec[V_CB],
        'lru_w_a': dwa, 'lru_b_a': vec[V_BA], 'lru_w_i': dwi, 'lru_b_i': vec[V_BI], 'lru_lambda': vec[V_LAM],
        'sc_conv_w': vec[V_SW:V_SW + KS], 'lru_out_norm': vec[V_GLO], 'sc_out_norm': vec[V_GSO],
        'ffn2_norm': dg_ffn2, 'final_norm': dgf,
    }
    full_shapes = [local_small[n].shape for n in small] + [(1,)]
    reduced = _allreduce_small(_pack([local_small[n] for n in small] + [sqerr[0, 0:1]]), "allreduce_small")
    reduced = _unpack(reduced, full_shapes)
    loss = (0.5 / D) * reduced[-1][0]
    gsm = {}
    for n, g in zip(small, reduced[:-1]):
        if n in SMALL_SHARDED:
            g = lax.dynamic_slice(g, (0, qi * DLq), (g.shape[0], DLq))
        gsm[n] = g.reshape(w[n].shape)
    small_shapes = [w[n].shape for n in small]
    d_s, m_s, v_s, _ = _adamw(_pack([w[n] for n in small]), _pack([gsm[n] for n in small]),
                              _pack([mom[n] for n in small]), _pack([var[n] for n in small]), "adamw_small")
    for n, d, mn, vn in zip(small, _unpack(d_s, small_shapes), _unpack(m_s, small_shapes), _unpack(v_s, small_shapes)):
        out_g[n], out_d[n], out_m[n], out_v[n] = gsm[n], d, mn, vn

    return (loss, dx0.reshape(x.shape), *[out_g[n] for n in WEIGHTS], *[out_d[n] for n in WEIGHTS],
            *[out_m[n] for n in WEIGHTS], *[out_v[n] for n in WEIGHTS])
```

```python
import math

import jax
import jax.numpy as jnp
from jax import lax
from jax.experimental import pallas as pl
from jax.experimental.pallas import tpu as pltpu

F32 = jnp.float32
BF16 = jnp.bfloat16
MESH = pl.DeviceIdType.MESH
ANY = pl.BlockSpec(memory_space=pl.ANY)

NORM_EPS = 1e-6
LRU_C = 8.0
FFN_RESIDUAL_SCALE = 0.5
ADAM_LR = 0.001
ADAM_B1 = 0.9
ADAM_B2 = 0.999
ADAM_EPS = 1e-08
ADAM_WD = 0.01
ADAM_STEP = 10

V7X_VMEM_BYTES = 64 * 2**20
VMEM_LIMIT = V7X_VMEM_BYTES - 8 * 2**20
LANES = 128
SUBLANES = 8
PACK_W = LANES
PACK_ALIGN = SUBLANES * PACK_W

WEIGHTS = ['ffn1_norm', 'ffn1_w_gate', 'ffn1_w_up', 'ffn1_w_down', 'mix_norm', 'w_in', 'lru_conv_w', 'lru_conv_b',
           'lru_w_a', 'lru_b_a', 'lru_w_i', 'lru_b_i', 'lru_lambda', 'sc_conv_w', 'lru_out_norm', 'sc_out_norm',
           'w_out', 'ffn2_norm', 'ffn2_w_gate', 'ffn2_w_up', 'ffn2_w_down', 'final_norm']
BIG = ['ffn1_w_gate', 'ffn1_w_up', 'ffn1_w_down', 'w_in', 'w_out', 'ffn2_w_gate', 'ffn2_w_up', 'ffn2_w_down']
BIG_AXIS = [1, 1, 0, 1, 0, 1, 1, 0]
SMALL_SHARDED = ['lru_conv_w', 'sc_conv_w']


def _tile(n, pref, mult):
    if n <= pref:
        return n
    t = (pref // mult) * mult
    while t >= mult:
        if n % t == 0:
            return t
        t -= mult
    return n


def _params(*sem):
    return pltpu.CompilerParams(dimension_semantics=sem, vmem_limit_bytes=VMEM_LIMIT)


def _me():
    return lax.axis_index("x"), lax.axis_index("y"), lax.axis_index("c")


def _sigmoid(v):
    return 1.0 / (1.0 + jnp.exp(-v))


def _rstd(v):
    return lax.rsqrt(jnp.mean(v * v, axis=-1, keepdims=True) + NORM_EPS)


def _rms_bwd(dy, v, gain):
    r = _rstd(v)
    w = gain * dy
    dv = r * w - v * (r * r * r) * jnp.mean(v * w, axis=-1, keepdims=True)
    dgain = jnp.sum(dy * v * r, axis=0, keepdims=True)
    return dv, dgain


def _dot_nt(a, b):
    return lax.dot_general(a, b, (((1,), (1,)), ((), ())), preferred_element_type=F32)


def _dot_tn(a, b):
    return lax.dot_general(a, b, (((0,), (0,)), ((), ())), preferred_element_type=F32)


class _Carry:
    def __init__(self, inputs, out_shape, aliases, sems, start, finish, middle=None, middle_at=0.85):
        self.inputs, self.out_shape, self.aliases, self.sems = list(inputs), list(out_shape), dict(aliases), list(sems)
        self.start, self.finish = start, finish
        self.middle, self.middle_at = middle, middle_at


def _call(body, name, grid, in_specs, out_specs, out_shape, scratch_shapes, semantics, args, carry=None, prefetch=()):
    np_ = len(prefetch)
    if carry is None:
        spec = pltpu.PrefetchScalarGridSpec(num_scalar_prefetch=np_, grid=grid, in_specs=in_specs, out_specs=out_specs,
                                            scratch_shapes=scratch_shapes)
        return pl.pallas_call(body, name=name, grid_spec=spec, out_shape=out_shape,
                              compiler_params=_params(*semantics))(*prefetch, *args)
    ni, no, ns = len(in_specs), len(out_specs), len(scratch_shapes)
    ci, co = len(carry.inputs), len(carry.out_shape)

    def carrying(*refs):
        pre, refs = refs[:np_], refs[np_:]
        ins, refs = refs[:ni], refs[ni:]
        cins, refs = refs[:ci], refs[ci:]
        outs, refs = refs[:no], refs[no:]
        couts, refs = refs[:co], refs[co:]
        scratch, csems = refs[:ns], refs[ns:]
        step = pl.program_id(0)
        for ax in range(1, len(grid)):
            step = step * grid[ax] + pl.program_id(ax)
        steps = math.prod(grid)
        first = step == 0
        last = step == steps - 1

        @pl.when(first)
        def _():
            carry.start(cins, couts, csems)

        if carry.middle is not None:
            @pl.when(step == min(int(carry.middle_at * steps), steps - 1))
            def _():
                carry.middle(cins, couts, csems)

        body(*pre, *ins, *outs, *scratch)

        @pl.when(last)
        def _():
            carry.finish(cins, couts, csems)

    spec = pltpu.PrefetchScalarGridSpec(
        num_scalar_prefetch=np_, grid=grid, in_specs=list(in_specs) + [ANY] * ci,
        out_specs=list(out_specs) + [ANY] * co, scratch_shapes=list(scratch_shapes) + carry.sems)
    return pl.pallas_call(
        carrying, name=name, grid_spec=spec, out_shape=list(out_shape) + carry.out_shape,
        input_output_aliases={np_ + ni + i: no + j for i, j in carry.aliases.items()},
        compiler_params=_params(*(["arbitrary"] * len(grid))),
    )(*prefetch, *args, *carry.inputs)


def _run_carry(carry, name):
    ci, co = len(carry.inputs), len(carry.out_shape)

    def body(*refs):
        cins, couts, csems = refs[:ci], refs[ci:ci + co], refs[ci + co:]
        carry.start(cins, couts, csems)
        if carry.middle is not None:
            carry.middle(cins, couts, csems)
        carry.finish(cins, couts, csems)

    return pl.pallas_call(body, name=name, in_specs=[ANY] * ci, out_specs=[ANY] * co, out_shape=carry.out_shape,
                          input_output_aliases=carry.aliases, scratch_shapes=carry.sems)(*carry.inputs)


def _ffn_fwd(x, gain, wg, wu, wd, name, carry=None):
    T, D = x.shape
    FF = wg.shape[1]
    tm = _tile(T, 512, 16)
    tf = _tile(FF, 512, LANES)
    nf = FF // tf

    def body(x_ref, g_ref, wg_ref, wu_ref, wd_ref, xo_ref, n_ref, G_ref, U_ref, acc_ref):
        f = pl.program_id(1)

        @pl.when(f == 0)
        def _():
            xv = x_ref[...]
            n_ref[...] = (xv * _rstd(xv) * g_ref[...]).astype(BF16)
            acc_ref[...] = jnp.zeros_like(acc_ref)

        n = n_ref[...]
        G = jnp.dot(n, wg_ref[...], preferred_element_type=F32)
        U = jnp.dot(n, wu_ref[...], preferred_element_type=F32)
        G_ref[...] = G.astype(BF16)
        U_ref[...] = U.astype(BF16)
        H = (G * _sigmoid(G) * U).astype(BF16)
        acc_ref[...] += jnp.dot(H, wd_ref[...], preferred_element_type=F32)

        @pl.when(f == nf - 1)
        def _():
            xo_ref[...] = x_ref[...] + FFN_RESIDUAL_SCALE * acc_ref[...]

    return _call(
        body, name, (T // tm, nf),
        [pl.BlockSpec((tm, D), lambda i, f: (i, 0)),
         pl.BlockSpec((1, D), lambda i, f: (0, 0)),
         pl.BlockSpec((D, tf), lambda i, f: (0, f)),
         pl.BlockSpec((D, tf), lambda i, f: (0, f)),
         pl.BlockSpec((tf, D), lambda i, f: (f, 0))],
        [pl.BlockSpec((tm, D), lambda i, f: (i, 0)),
         pl.BlockSpec((tm, D), lambda i, f: (i, 0)),
         pl.BlockSpec((tm, tf), lambda i, f: (i, f)),
         pl.BlockSpec((tm, tf), lambda i, f: (i, f))],
        [jax.ShapeDtypeStruct((T, D), F32), jax.ShapeDtypeStruct((T, D), BF16),
         jax.ShapeDtypeStruct((T, FF), BF16), jax.ShapeDtypeStruct((T, FF), BF16)],
        [pltpu.VMEM((tm, D), F32)], ("parallel", "arbitrary"), (x, gain, wg, wu, wd), carry)


def _ffn_bwd_act(db, G, U, wg, wu, wd, name, carry=None):
    T, D = db.shape
    FF = wg.shape[1]
    tm = _tile(T, 512, 16)
    tf = _tile(FF, 512, LANES)

    def body(d_ref, G_ref, U_ref, wg_ref, wu_ref, wd_ref, dG_ref, dU_ref, H_ref, dn_ref):
        f = pl.program_id(1)
        dH = _dot_nt(d_ref[...], wd_ref[...])
        Gv = G_ref[...].astype(F32)
        Uv = U_ref[...].astype(F32)
        s = _sigmoid(Gv)
        sg = Gv * s
        H_ref[...] = (sg * Uv).astype(BF16)
        dU = (dH * sg).astype(BF16)
        dG = (dH * Uv * (s * (1.0 + Gv * (1.0 - s)))).astype(BF16)
        dG_ref[...] = dG
        dU_ref[...] = dU
        contrib = _dot_nt(dG, wg_ref[...]) + _dot_nt(dU, wu_ref[...])

        @pl.when(f == 0)
        def _():
            dn_ref[...] = contrib

        @pl.when(f > 0)
        def _():
            dn_ref[...] += contrib

    return _call(
        body, name, (T // tm, FF // tf),
        [pl.BlockSpec((tm, D), lambda i, f: (i, 0)),
         pl.BlockSpec((tm, tf), lambda i, f: (i, f)),
         pl.BlockSpec((tm, tf), lambda i, f: (i, f)),
         pl.BlockSpec((D, tf), lambda i, f: (0, f)),
         pl.BlockSpec((D, tf), lambda i, f: (0, f)),
         pl.BlockSpec((tf, D), lambda i, f: (f, 0))],
        [pl.BlockSpec((tm, tf), lambda i, f: (i, f)),
         pl.BlockSpec((tm, tf), lambda i, f: (i, f)),
         pl.BlockSpec((tm, tf), lambda i, f: (i, f)),
         pl.BlockSpec((tm, D), lambda i, f: (i, 0))],
        [jax.ShapeDtypeStruct((T, FF), BF16), jax.ShapeDtypeStruct((T, FF), BF16),
         jax.ShapeDtypeStruct((T, FF), BF16), jax.ShapeDtypeStruct((T, D), F32)],
        [], ("parallel", "arbitrary"), (db, G, U, wg, wu, wd), carry)


def _ffn_bwd_hidden(db, G, U, wd, name, carry=None):
    T, D = db.shape
    FF = wd.shape[0]
    tm = _tile(T, 1024, 16)
    tf = _tile(FF, 512, LANES)

    def body(d_ref, G_ref, U_ref, wd_ref, dG_ref, dU_ref, H_ref):
        dH = _dot_nt(d_ref[...], wd_ref[...])
        Gv = G_ref[...].astype(F32)
        Uv = U_ref[...].astype(F32)
        s = _sigmoid(Gv)
        sg = Gv * s
        H_ref[...] = (sg * Uv).astype(BF16)
        dU_ref[...] = (dH * sg).astype(BF16)
        dG_ref[...] = (dH * Uv * (s * (1.0 + Gv * (1.0 - s)))).astype(BF16)

    act = pl.BlockSpec((tm, tf), lambda i, f: (i, f))
    return _call(
        body, name, (T // tm, FF // tf),
        [pl.BlockSpec((tm, D), lambda i, f: (i, 0)), act, act, pl.BlockSpec((tf, D), lambda i, f: (f, 0))],
        [act, act, act], [jax.ShapeDtypeStruct((T, FF), BF16)] * 3,
        [], ("parallel", "arbitrary"), (db, G, U, wd), carry)


def _ffn_bwd_input(dG, dU, wg, wu, name, carry=None):
    T, FF = dG.shape
    D = wg.shape[0]
    tm = _tile(T, 512, 16)
    tn = _tile(D, 512, LANES)

    def body(dG_ref, dU_ref, wg_ref, wu_ref, dn_ref):
        dn_ref[...] = _dot_nt(dG_ref[...], wg_ref[...]) + _dot_nt(dU_ref[...], wu_ref[...])

    act = pl.BlockSpec((tm, FF), lambda i, j: (i, 0))
    wt = pl.BlockSpec((tn, FF), lambda i, j: (j, 0))
    return _call(body, name, (T // tm, D // tn), [act, act, wt, wt], [pl.BlockSpec((tm, tn), lambda i, j: (i, j))],
                 [jax.ShapeDtypeStruct((T, D), F32)], [], ("parallel", "arbitrary"), (dG, dU, wg, wu), carry)


TAIL_ROWS = 128


class _NormBwdTail:
    def __init__(self, T, D, tm, scale):
        self.T, self.D, self.tm, self.scale = T, D, tm, scale
        self.ni = T // tm
        self.scratch = [pltpu.VMEM((tm, D), F32), pltpu.VMEM((tm, D), F32), pltpu.VMEM((tm, D), F32),
                        pltpu.VMEM((tm, D), BF16), pltpu.SemaphoreType.DMA((4,))]
        self.out_shape = [jax.ShapeDtypeStruct((T, D), F32), jax.ShapeDtypeStruct((T, D), BF16)]

    def _rows(self, k):
        return pl.ds(pl.multiple_of(k * self.tm, self.tm), self.tm)

    def _loads(self, k, x_hbm, r_hbm, bufs):
        xbuf, rbuf, _, _, sems = bufs
        return [pltpu.make_async_copy(x_hbm.at[self._rows(k)], xbuf, sems.at[0]),
                pltpu.make_async_copy(r_hbm.at[self._rows(k)], rbuf, sems.at[1])]

    def _stores(self, k, dx_hbm, dxb_hbm, bufs):
        _, _, obuf, obbuf, sems = bufs
        return [pltpu.make_async_copy(obuf, dx_hbm.at[self._rows(k)], sems.at[2]),
                pltpu.make_async_copy(obbuf, dxb_hbm.at[self._rows(k)], sems.at[3])]

    def prefetch(self, i, x_hbm, r_hbm, bufs):
        for cp in self._loads(i, x_hbm, r_hbm, bufs):
            cp.start()

    def run(self, i, acc_ref, g_ref, x_hbm, r_hbm, dx_hbm, dxb_hbm, dg_ref, bufs):
        xbuf, rbuf, obuf, obbuf, _ = bufs
        for cp in self._loads(i, x_hbm, r_hbm, bufs):
            cp.wait()

        @pl.when(i > 0)
        def _():
            for cp in self._stores(i - 1, dx_hbm, dxb_hbm, bufs):
                cp.wait()

        dgain = None
        for r0 in range(0, self.tm, TAIL_ROWS):
            rs = slice(r0, min(r0 + TAIL_ROWS, self.tm))
            dv, dgr = _rms_bwd(acc_ref[rs, :], xbuf[rs, :], g_ref[...])
            dx = rbuf[rs, :] + dv
            obuf[rs, :] = dx
            obbuf[rs, :] = (self.scale * dx).astype(BF16)
            dgain = dgr if dgain is None else dgain + dgr
        for cp in self._stores(i, dx_hbm, dxb_hbm, bufs):
            cp.start()

        @pl.when(i == 0)
        def _():
            dg_ref[...] = dgain

        @pl.when(i > 0)
        def _():
            dg_ref[...] += dgain

        @pl.when(i == self.ni - 1)
        def _():
            for cp in self._stores(i, dx_hbm, dxb_hbm, bufs):
                cp.wait()


def _ffn_bwd_fused(db, G, U, wg, wu, wd, x_in, gain, dres, scale, name):
    T, D = db.shape
    FF = wg.shape[1]
    tm = _tile(T, 512, 16)
    tf = _tile(FF, 512, LANES)
    nf = FF // tf
    tail = _NormBwdTail(T, D, tm, scale)

    def body(d_ref, G_ref, U_ref, wg_ref, wu_ref, wd_ref, g_ref, x_hbm, r_hbm,
             dG_ref, dU_ref, H_ref, dg_ref, dx_hbm, dxb_hbm, acc_ref, *bufs):
        i, f = pl.program_id(0), pl.program_id(1)

        @pl.when(f == 0)
        def _():
            tail.prefetch(i, x_hbm, r_hbm, bufs)

        dH = _dot_nt(d_ref[...], wd_ref[...])
        Gv = G_ref[...].astype(F32)
        Uv = U_ref[...].astype(F32)
        s = _sigmoid(Gv)
        sg = Gv * s
        H_ref[...] = (sg * Uv).astype(BF16)
        dU = (dH * sg).astype(BF16)
        dG = (dH * Uv * (s * (1.0 + Gv * (1.0 - s)))).astype(BF16)
        dG_ref[...] = dG
        dU_ref[...] = dU
        contrib = _dot_nt(dG, wg_ref[...]) + _dot_nt(dU, wu_ref[...])

        @pl.when(f == 0)
        def _():
            acc_ref[...] = contrib

        @pl.when(f > 0)
        def _():
            acc_ref[...] += contrib

        @pl.when(f == nf - 1)
        def _():
            tail.run(i, acc_ref, g_ref, x_hbm, r_hbm, dx_hbm, dxb_hbm, dg_ref, bufs)

    act = pl.BlockSpec((tm, tf), lambda i, f: (i, f))
    return pl.pallas_call(
        body, name=name, grid=(T // tm, nf),
        in_specs=[pl.BlockSpec((tm, D), lambda i, f: (i, 0)), act, act,
                  pl.BlockSpec((D, tf), lambda i, f: (0, f)),
                  pl.BlockSpec((D, tf), lambda i, f: (0, f)),
                  pl.BlockSpec((tf, D), lambda i, f: (f, 0)),
                  pl.BlockSpec((1, D), lambda i, f: (0, 0)), ANY, ANY],
        out_specs=[act, act, act, pl.BlockSpec((1, D), lambda i, f: (0, 0)), ANY, ANY],
        out_shape=[jax.ShapeDtypeStruct((T, FF), BF16)] * 3 + [jax.ShapeDtypeStruct((1, D), F32)] + tail.out_shape,
        scratch_shapes=[pltpu.VMEM((tm, D), F32)] + tail.scratch,
        compiler_params=_params("arbitrary", "arbitrary"),
    )(db, G, U, wg, wu, wd, gain, x_in, dres)


def _mm_nt_norm_bwd(a, w, x_in, gain, dres, scale, name, carry=None):
    T, K = a.shape
    D = w.shape[0]
    tm = _tile(T, 512, 16)
    tk = _tile(K, 1280, LANES)
    nk = K // tk
    tail = _NormBwdTail(T, D, tm, scale)

    def body(a_ref, w_ref, g_ref, x_hbm, r_hbm, dg_ref, dx_hbm, dxb_hbm, acc_ref, *bufs):
        i, k = pl.program_id(0), pl.program_id(1)

        @pl.when(k == 0)
        def _():
            tail.prefetch(i, x_hbm, r_hbm, bufs)

        contrib = _dot_nt(a_ref[...], w_ref[...])

        @pl.when(k == 0)
        def _():
            acc_ref[...] = contrib

        @pl.when(k > 0)
        def _():
            acc_ref[...] += contrib

        @pl.when(k == nk - 1)
        def _():
            tail.run(i, acc_ref, g_ref, x_hbm, r_hbm, dx_hbm, dxb_hbm, dg_ref, bufs)

    return _call(
        body, name, (T // tm, nk),
        [pl.BlockSpec((tm, tk), lambda i, k: (i, k)), pl.BlockSpec((D, tk), lambda i, k: (0, k)),
         pl.BlockSpec((1, D), lambda i, k: (0, 0)), ANY, ANY],
        [pl.BlockSpec((1, D), lambda i, k: (0, 0)), ANY, ANY],
        [jax.ShapeDtypeStruct((1, D), F32)] + tail.out_shape,
        [pltpu.VMEM((tm, D), F32)] + tail.scratch, ("arbitrary", "arbitrary"), (a, w, gain, x_in, dres), carry)


def _rms_bwd_res(dn, x, gain, dres, scale, name, carry=None):
    T, D = x.shape
    tm = _tile(T, 256, 16)

    def body(dn_ref, x_ref, g_ref, dr_ref, dx_ref, dxb_ref, dg_ref):
        i = pl.program_id(0)
        dv, dgain = _rms_bwd(dn_ref[...], x_ref[...], g_ref[...])
        dx = dr_ref[...] + dv
        dx_ref[...] = dx
        dxb_ref[...] = (scale * dx).astype(BF16)

        @pl.when(i == 0)
        def _():
            dg_ref[...] = dgain

        @pl.when(i > 0)
        def _():
            dg_ref[...] += dgain

    row = pl.BlockSpec((tm, D), lambda i: (i, 0))
    vec = pl.BlockSpec((1, D), lambda i: (0, 0))
    return _call(
        body, name, (T // tm,), [row, row, vec, row], [row, row, vec],
        [jax.ShapeDtypeStruct((T, D), F32), jax.ShapeDtypeStruct((T, D), BF16), jax.ShapeDtypeStruct((1, D), F32)],
        [], ("arbitrary",), (dn, x, gain, dres), carry)


def _loss_head(x3, gain, target, name):
    T, D = x3.shape
    tm = _tile(T, 256, 16)

    def body(x_ref, g_ref, t_ref, dx_ref, dxb_ref, ls_ref, dg_ref):
        i = pl.program_id(0)
        xv = x_ref[...]
        err = xv * _rstd(xv) * g_ref[...] - t_ref[...]
        sq = jnp.sum(jnp.sum(err * err, axis=1, keepdims=True), axis=0, keepdims=True)
        dv, dgain = _rms_bwd(err * (1.0 / D), xv, g_ref[...])
        dx_ref[...] = dv
        dxb_ref[...] = (FFN_RESIDUAL_SCALE * dv).astype(BF16)
        sqb = jnp.broadcast_to(sq, (1, LANES))

        @pl.when(i == 0)
        def _():
            dg_ref[...] = dgain
            ls_ref[...] = sqb

        @pl.when(i > 0)
        def _():
            dg_ref[...] += dgain
            ls_ref[...] += sqb

    row = pl.BlockSpec((tm, D), lambda i: (i, 0))
    vec = pl.BlockSpec((1, D), lambda i: (0, 0))
    return pl.pallas_call(
        body, name=name, grid=(T // tm,),
        in_specs=[row, vec, row],
        out_specs=[row, row, pl.BlockSpec((1, LANES), lambda i: (0, 0)), vec],
        out_shape=[jax.ShapeDtypeStruct((T, D), F32), jax.ShapeDtypeStruct((T, D), BF16),
                   jax.ShapeDtypeStruct((1, LANES), F32), jax.ShapeDtypeStruct((1, D), F32)],
        compiler_params=_params("arbitrary"),
    )(x3, gain, target)


def _norm_mm(x, gain, w, name):
    T, D = x.shape
    N = w.shape[1]
    tm = _tile(T, 512, 16)
    tn = _tile(N, 2560, LANES)

    def body(x_ref, g_ref, w_ref, n_ref, z_ref):
        @pl.when(pl.program_id(1) == 0)
        def _():
            xv = x_ref[...]
            n_ref[...] = (xv * _rstd(xv) * g_ref[...]).astype(BF16)

        z_ref[...] = jnp.dot(n_ref[...], w_ref[...], preferred_element_type=F32).astype(BF16)

    return pl.pallas_call(
        body, name=name, grid=(T // tm, N // tn),
        in_specs=[pl.BlockSpec((tm, D), lambda i, j: (i, 0)),
                  pl.BlockSpec((1, D), lambda i, j: (0, 0)),
                  pl.BlockSpec((D, tn), lambda i, j: (0, j))],
        out_specs=[pl.BlockSpec((tm, D), lambda i, j: (i, 0)),
                   pl.BlockSpec((tm, tn), lambda i, j: (i, j))],
        out_shape=[jax.ShapeDtypeStruct((T, D), BF16), jax.ShapeDtypeStruct((T, N), BF16)],
        compiler_params=_params("parallel", "arbitrary"),
    )(x, gain, w)


def _mm_fullk(a, w, trans_w, residual, out_dtype, name, carry=None):
    T, K = a.shape
    N = w.shape[0] if trans_w else w.shape[1]
    tm = _tile(T, 512, 16)
    tn = _tile(N, 2048 * 2560 // K, LANES)

    def body(*refs):
        if residual is None:
            a_ref, w_ref, o_ref = refs
        else:
            a_ref, w_ref, r_ref, o_ref = refs
        if trans_w:
            acc = _dot_nt(a_ref[...], w_ref[...])
        else:
            acc = jnp.dot(a_ref[...], w_ref[...], preferred_element_type=F32)
        if residual is not None:
            acc = acc + r_ref[...]
        o_ref[...] = acc.astype(out_dtype)

    w_spec = pl.BlockSpec((tn, K), lambda i, j: (j, 0)) if trans_w else pl.BlockSpec((K, tn), lambda i, j: (0, j))
    in_specs = [pl.BlockSpec((tm, K), lambda i, j: (i, 0)), w_spec]
    args = [a, w]
    if residual is not None:
        in_specs.append(pl.BlockSpec((tm, tn), lambda i, j: (i, j)))
        args.append(residual)
    return _call(body, name, (T // tm, N // tn), in_specs, [pl.BlockSpec((tm, tn), lambda i, j: (i, j))],
                 [jax.ShapeDtypeStruct((T, N), out_dtype)], [], ("parallel", "arbitrary"), args, carry)


def _mm_tn(a, b, name, carry=None):
    T, M = a.shape
    N = b.shape[1]
    tmw = _tile(M, 2048, LANES)
    tnw = _tile(N, 2048 * 1408 // tmw, LANES)
    tk = _tile(T, 1024, 16)
    nk = T // tk

    def body(a_ref, b_ref, o_ref, acc_ref):
        k = pl.program_id(2)

        @pl.when(k == 0)
        def _():
            acc_ref[...] = jnp.zeros_like(acc_ref)

        acc_ref[...] += _dot_tn(a_ref[...], b_ref[...])

        @pl.when(k == nk - 1)
        def _():
            o_ref[...] = acc_ref[...].astype(BF16)

    return _call(
        body, name, (M // tmw, N // tnw, nk),
        [pl.BlockSpec((tk, tmw), lambda i, j, k: (k, i)),
         pl.BlockSpec((tk, tnw), lambda i, j, k: (k, j))],
        [pl.BlockSpec((tmw, tnw), lambda i, j, k: (i, j))],
        [jax.ShapeDtypeStruct((M, N), BF16)],
        [pltpu.VMEM((tmw, tnw), F32)], ("parallel", "parallel", "arbitrary"), (a, b), carry)


GELU_K = math.sqrt(2.0 / math.pi)
GELU_C = 0.044715


def _gelu_and_grad(v):
    u = GELU_K * (v + GELU_C * v * v * v)
    th = jnp.tanh(u)
    g = 0.5 * v * (1.0 + th)
    dg = 0.5 * (1.0 + th) + 0.5 * v * (1.0 - th * th) * GELU_K * (1.0 + 3.0 * GELU_C * v * v)
    return g, dg


def _neg_expm1(v):
    poly = v * (1.0 + v * (0.5 + v * (1.0 / 6 + v * (1.0 / 24 + v * (1.0 / 120 + v * (1.0 / 720))))))
    return jnp.where(v > -0.25, -poly, 1.0 - jnp.exp(v))


def _softplus_neg(lam):
    e = jnp.exp(-jnp.abs(lam))
    log1pe = jnp.where(e < 1e-4, e * (1.0 - 0.5 * e), jnp.log(1.0 + e))
    sp = jnp.maximum(-lam, 0.0) + log1pe
    dsp = -1.0 / (1.0 + jnp.exp(lam))
    return sp, dsp


def _earlier(ext, j):
    return pltpu.roll(ext, j, 0)[SUBLANES:, :]


def _later(ext, j):
    n = ext.shape[0]
    return pltpu.roll(ext, n - j, 0)[:n - SUBLANES, :]


def _taps(v, halo, K):
    ext = jnp.concatenate([halo, v], axis=0)
    return [v] + [_earlier(ext, j) for j in range(1, K)]


def _block_diag(vb, w_ref, nh, hd):
    return jnp.concatenate(
        [jnp.dot(vb[:, h * hd:(h + 1) * hd], w_ref[h], preferred_element_type=F32) for h in range(nh)], axis=1)


def _lru_gates(xc, wa_ref, ba_ref, wi_ref, bi_ref, sp, nh, hd):
    xcb = xc.astype(BF16)
    r = _sigmoid(_block_diag(xcb, wa_ref, nh, hd) + ba_ref[...])
    ig = _sigmoid(_block_diag(xcb, wi_ref, nh, hd) + bi_ref[...])
    log_a = -LRU_C * r * sp
    a = jnp.exp(log_a)
    mult = jnp.sqrt(_neg_expm1(2.0 * log_a))
    return xcb, r, ig, a, mult


def _mix_fwd(z, cw, cb, wa, ba, wi, bi, lam, sw, glo, gso, name):
    T = z.shape[0]
    DL = cb.shape[1]
    DS = gso.shape[1]
    NH, HD = wa.shape[0], wa.shape[1]
    KL, KS = cw.shape[0], sw.shape[0]
    tt = _tile(T, 128, 16)
    o_g, o_b, o_c, o_x = DL, 2 * DL, 2 * DL + DS, 2 * DL + 2 * DS

    def body(z_ref, cw_ref, cb_ref, wa_ref, ba_ref, wi_ref, bi_ref, lam_ref, sw_ref, glo_ref, gso_ref,
             h_ref, y_ref, cx_ref, cp_ref, ch_ref):
        @pl.when(pl.program_id(0) == 0)
        def _():
            cx_ref[...] = jnp.zeros_like(cx_ref)
            cp_ref[...] = jnp.zeros_like(cp_ref)
            ch_ref[...] = jnp.zeros_like(ch_ref)

        def zcol(o, n):
            return z_ref[:, o:o + n].astype(F32)

        lx = zcol(0, DL)
        xs = _taps(lx, cx_ref[...], KL)
        cx_ref[...] = lx[tt - SUBLANES:, :]
        xc = cb_ref[...] + xs[0] * cw_ref[KL - 1:KL, :]
        for j in range(1, KL):
            xc = xc + xs[j] * cw_ref[KL - 1 - j:KL - j, :]
        sp, _ = _softplus_neg(lam_ref[...])
        _, _, ig, a, mult = _lru_gates(xc, wa_ref, ba_ref, wi_ref, bi_ref, sp, NH, HD)
        b = mult * (ig * xc)
        rows = lax.broadcasted_iota(jnp.int32, (tt, DL), 0)
        s = 1
        while s < tt:
            keep = rows >= s
            b = jnp.where(keep, a * pltpu.roll(b, s, 0) + b, b)
            a = jnp.where(keep, a * pltpu.roll(a, s, 0), a)
            s *= 2
        h = a * ch_ref[SUBLANES - 1:SUBLANES, :] + b
        ch_ref[...] = h[tt - SUBLANES:, :]
        h_ref[...] = h
        ge, _ = _gelu_and_grad(zcol(o_g, DL))
        ylru = h * ge
        y_ref[:, 0:DL] = (ylru * _rstd(ylru) * glo_ref[...]).astype(BF16)

        p = zcol(o_c, DS) * zcol(o_x, DS)
        ps = _taps(p, cp_ref[...], KS)
        cp_ref[...] = p[tt - SUBLANES:, :]
        cv = ps[0] * sw_ref[KS - 1:KS, :]
        for j in range(1, KS):
            cv = cv + ps[j] * sw_ref[KS - 1 - j:KS - j, :]
        ysc = zcol(o_b, DS) * cv
        y_ref[:, DL:DL + DS] = (ysc * _rstd(ysc) * gso_ref[...]).astype(BF16)

    def full(shape):
        return pl.BlockSpec(shape, lambda t: (0,) * len(shape))

    return pl.pallas_call(
        body, name=name, grid=(T // tt,),
        in_specs=[pl.BlockSpec((tt, z.shape[1]), lambda t: (t, 0)),
                  full(cw.shape), full(cb.shape), full(wa.shape), full(ba.shape), full(wi.shape), full(bi.shape),
                  full(lam.shape), full(sw.shape), full(glo.shape), full(gso.shape)],
        out_specs=[pl.BlockSpec((tt, DL), lambda t: (t, 0)), pl.BlockSpec((tt, DL + DS), lambda t: (t, 0))],
        out_shape=[jax.ShapeDtypeStruct((T, DL), F32), jax.ShapeDtypeStruct((T, DL + DS), BF16)],
        scratch_shapes=[pltpu.VMEM((SUBLANES, DL), F32), pltpu.VMEM((SUBLANES, DS), F32),
                        pltpu.VMEM((SUBLANES, DL), F32)],
        compiler_params=_params("arbitrary"),
    )(z, cw, cb, wa, ba, wi, bi, lam, sw, glo, gso)


V_BA, V_BI, V_LAM, V_CB, V_CW, V_SW, V_GLO, V_GSO, V_ROWS = 0, 1, 2, 3, 4, 8, 11, 12, 16


def _mix_bwd(z, h, dy, cw, cb, wa, ba, wi, bi, lam, sw, glo, gso, name):
    T = z.shape[0]
    DL = cb.shape[1]
    DS = gso.shape[1]
    NH, HD = wa.shape[0], wa.shape[1]
    KL, KS = cw.shape[0], sw.shape[0]
    tt = _tile(T, 64, 16)
    nt = T // tt
    ZH = 2 * SUBLANES
    o_g, o_b, o_c, o_x = DL, 2 * DL, 2 * DL + DS, 2 * DL + 2 * DS

    def body(z_ref, zh_ref, h_ref, hh_ref, dy_ref, cw_ref, cb_ref, wa_ref, ba_ref, wi_ref, bi_ref, lam_ref,
             sw_ref, glo_ref, gso_ref, dz_ref, dwa_ref, dwi_ref, vec_ref, cdx_ref, cdc_ref, cdh_ref):
        i = pl.program_id(0)
        tr = nt - 1 - i

        @pl.when(i == 0)
        def _():
            dwa_ref[...] = jnp.zeros_like(dwa_ref)
            dwi_ref[...] = jnp.zeros_like(dwi_ref)
            vec_ref[...] = jnp.zeros_like(vec_ref)
            cdx_ref[...] = jnp.zeros_like(cdx_ref)
            cdc_ref[...] = jnp.zeros_like(cdc_ref)
            cdh_ref[...] = jnp.zeros_like(cdh_ref)

        def acc_row(r, v):
            vec_ref[pl.ds(r, 1), :] += jnp.sum(v, axis=0, keepdims=True)

        has_prev = tr > 0
        rows = lax.broadcasted_iota(jnp.int32, (tt, DL), 0)

        def zcol(o, n):
            return z_ref[:, o:o + n].astype(F32)

        def zhalo(o, n):
            return jnp.where(has_prev, zh_ref[:, o:o + n].astype(F32)[SUBLANES:, :], 0.0)

        lx = zcol(0, DL)
        xs = _taps(lx, zhalo(0, DL), KL)
        xc = cb_ref[...] + xs[0] * cw_ref[KL - 1:KL, :]
        for j in range(1, KL):
            xc = xc + xs[j] * cw_ref[KL - 1 - j:KL - j, :]
        sp, dsp = _softplus_neg(lam_ref[...])
        xcb, r, ig, a, mult = _lru_gates(xc, wa_ref, ba_ref, wi_ref, bi_ref, sp, NH, HD)
        hv = h_ref[...]
        hprev = _earlier(jnp.concatenate([jnp.where(has_prev, hh_ref[...], 0.0), hv], axis=0), 1)
        gate = zcol(o_g, DL)
        ge, dge = _gelu_and_grad(gate)
        ylru = hv * ge

        d_ylru, dglo = _rms_bwd(dy_ref[:, 0:DL].astype(F32), ylru, glo_ref[...])
        vec_ref[pl.ds(V_GLO, 1), :] += dglo
        dz_ref[:, o_g:o_g + DL] = (d_ylru * hv * dge).astype(BF16)
        bq = d_ylru * ge
        aq = jnp.where(rows == tt - 1, 1.0, pltpu.roll(a, tt - 1, 0))
        s = 1
        while s < tt:
            keep = rows < tt - s
            bq = jnp.where(keep, aq * pltpu.roll(bq, tt - s, 0) + bq, bq)
            aq = jnp.where(keep, aq * pltpu.roll(aq, tt - s, 0), aq)
            s *= 2
        dhh = bq + aq * cdh_ref[0:1, :]
        cdh_ref[0:1, :] = a[0:1, :] * dhh[0:1, :]

        da = dhh * hprev
        dmult = dhh * (ig * xc)
        d_i = dhh * mult * xc
        dxc = dhh * mult * ig
        dlog = da * a - dmult * (a * a) / mult
        acc_row(V_LAM, dlog * (-LRU_C * r) * dsp)
        dpa = dlog * (-LRU_C * sp) * r * (1.0 - r)
        dpi = d_i * ig * (1.0 - ig)
        acc_row(V_BA, dpa)
        acc_row(V_BI, dpi)
        dpab = dpa.astype(BF16)
        dpib = dpi.astype(BF16)
        back = []
        for hh in range(NH):
            sl = slice(hh * HD, (hh + 1) * HD)
            dwa_ref[hh] += _dot_tn(xcb[:, sl], dpab[:, sl])
            dwi_ref[hh] += _dot_tn(xcb[:, sl], dpib[:, sl])
            back.append(_dot_nt(dpab[:, sl], wa_ref[hh]) + _dot_nt(dpib[:, sl], wi_ref[hh]))
        dxc = dxc + jnp.concatenate(back, axis=1)

        acc_row(V_CB, dxc)
        extd = jnp.concatenate([dxc, cdx_ref[...]], axis=0)
        cdx_ref[...] = dxc[0:SUBLANES, :]
        dlx = dxc * cw_ref[KL - 1:KL, :]
        acc_row(V_CW + KL - 1, dxc * xs[0])
        for j in range(1, KL):
            dlx = dlx + _later(extd, j) * cw_ref[KL - 1 - j:KL - j, :]
            acc_row(V_CW + KL - 1 - j, dxc * xs[j])
        dz_ref[:, 0:DL] = dlx.astype(BF16)

        sb = zcol(o_b, DS)
        sc = zcol(o_c, DS)
        sx = zcol(o_x, DS)
        p = sc * sx
        ps = _taps(p, zhalo(o_c, DS) * zhalo(o_x, DS), KS)
        cv = ps[0] * sw_ref[KS - 1:KS, :]
        for j in range(1, KS):
            cv = cv + ps[j] * sw_ref[KS - 1 - j:KS - j, :]
        d_ysc, dgso = _rms_bwd(dy_ref[:, DL:DL + DS].astype(F32), sb * cv, gso_ref[...])
        vec_ref[pl.ds(V_GSO, 1), :] += dgso
        dz_ref[:, o_b:o_b + DS] = (d_ysc * cv).astype(BF16)
        dcv = d_ysc * sb
        extc = jnp.concatenate([dcv, cdc_ref[...]], axis=0)
        cdc_ref[...] = dcv[0:SUBLANES, :]
        dp = dcv * sw_ref[KS - 1:KS, :]
        acc_row(V_SW + KS - 1, dcv * ps[0])
        for j in range(1, KS):
            dp = dp + _later(extc, j) * sw_ref[KS - 1 - j:KS - j, :]
            acc_row(V_SW + KS - 1 - j, dcv * ps[j])
        dz_ref[:, o_c:o_c + DS] = (dp * sx).astype(BF16)
        dz_ref[:, o_x:o_x + DS] = (dp * sc).astype(BF16)

    def full(shape):
        return pl.BlockSpec(shape, lambda t: (0,) * len(shape))

    def rev(t):
        return nt - 1 - t

    def halo(t, rows):
        return jnp.maximum(rev(t) * (tt // rows) - 1, 0)

    return pl.pallas_call(
        body, name=name, grid=(nt,),
        in_specs=[pl.BlockSpec((tt, z.shape[1]), lambda t: (rev(t), 0)),
                  pl.BlockSpec((ZH, z.shape[1]), lambda t: (halo(t, ZH), 0)),
                  pl.BlockSpec((tt, DL), lambda t: (rev(t), 0)),
                  pl.BlockSpec((SUBLANES, DL), lambda t: (halo(t, SUBLANES), 0)),
                  pl.BlockSpec((tt, DL + DS), lambda t: (rev(t), 0)),
                  full(cw.shape), full(cb.shape), full(wa.shape), full(ba.shape), full(wi.shape), full(bi.shape),
                  full(lam.shape), full(sw.shape), full(glo.shape), full(gso.shape)],
        out_specs=[pl.BlockSpec((tt, z.shape[1]), lambda t: (rev(t), 0)),
                   full(wa.shape), full(wi.shape), full((V_ROWS, DL))],
        out_shape=[jax.ShapeDtypeStruct(z.shape, BF16), jax.ShapeDtypeStruct(wa.shape, F32),
                   jax.ShapeDtypeStruct(wi.shape, F32), jax.ShapeDtypeStruct((V_ROWS, DL), F32)],
        scratch_shapes=[pltpu.VMEM((SUBLANES, DL), F32), pltpu.VMEM((SUBLANES, DS), F32),
                        pltpu.VMEM((SUBLANES, DL), F32)],
        compiler_params=_params("arbitrary"),
    )(z, z, h, h, dy, cw, cb, wa, ba, wi, bi, lam, sw, glo, gso)


def _pair_add(p, r1, c, name):
    G, R, C = r1.shape
    tr = _tile(R, 256, 16)
    tc = _tile(C, 1408, LANES)

    def body(c_ref, p_ref, r_ref, o_ref):
        o_ref[...] = (p_ref[...].astype(F32) + r_ref[...].astype(F32)).astype(BF16)

    blk = (None, tr, tc)
    return pl.pallas_call(
        body, name=name,
        grid_spec=pltpu.PrefetchScalarGridSpec(
            num_scalar_prefetch=1, grid=(G, R // tr, C // tc),
            in_specs=[pl.BlockSpec(blk, lambda g, i, j, cr: (2 * g + cr[0], i, j)),
                      pl.BlockSpec(blk, lambda g, i, j, cr: (g, i, j))],
            out_specs=pl.BlockSpec(blk, lambda g, i, j, cr: (g, i, j))),
        out_shape=jax.ShapeDtypeStruct((G, R, C), BF16),
        compiler_params=_params("parallel", "parallel", "parallel"),
    )(c, p, r1)


def _quad_add(s, r2, qc, axis, name):
    _, R, W = r2.shape
    tr = _tile(R, 256, 16)

    def body(qc_ref, s_ref, r0_ref, r1_ref, r2_ref, o_ref):
        o_ref[...] = ((s_ref[...].astype(F32) + r0_ref[...].astype(F32)) + r1_ref[...].astype(F32)) + r2_ref[...].astype(F32)

    blk = (None, tr, W)
    if axis == 1:
        own = pl.BlockSpec(blk, lambda i, qr: (0, i, qr[0]))
    else:
        own = pl.BlockSpec(blk, lambda i, qr: (qr[0], i, 0))
    return pl.pallas_call(
        body, name=name,
        grid_spec=pltpu.PrefetchScalarGridSpec(
            num_scalar_prefetch=1, grid=(R // tr,),
            in_specs=[own] + [pl.BlockSpec(blk, lambda i, qr, j=j: (j, i, 0)) for j in range(3)],
            out_specs=pl.BlockSpec(blk, lambda i, qr: (qr[1], i, 0))),
        out_shape=jax.ShapeDtypeStruct((2, R, W), F32),
        compiler_params=_params("parallel"),
    )(qc, s, r2, r2, r2)


CAST_BLOCKS = 4


def _cast_into_full(shards, qc, axes, name, carry=None):
    M = len(shards)
    nb = CAST_BLOCKS

    def body(qc_ref, *refs):
        for s_ref, o_ref in zip(refs[:M], refs[M:]):
            o_ref[...] = s_ref[...].astype(BF16)

    in_specs, out_specs, out_shape = [], [], []
    for s, ax in zip(shards, axes):
        R, W = s.shape
        Rh = R // 2
        tr = Rh // nb
        in_specs.append(pl.BlockSpec((tr, W), lambda hf, i, qr: (hf * nb + i, 0)))
        if ax == 1:
            out_shape.append(jax.ShapeDtypeStruct((2, Rh, 4 * W), BF16))
            out_specs.append(pl.BlockSpec((None, tr, W), lambda hf, i, qr: (hf, i, qr[0])))
        else:
            out_shape.append(jax.ShapeDtypeStruct((8, Rh, W), BF16))
            out_specs.append(pl.BlockSpec((None, tr, W), lambda hf, i, qr: (2 * qr[0] + hf, i, 0)))
    return _call(body, name, (2, nb), in_specs, out_specs, out_shape, [], ("parallel", "parallel"), shards, carry, (qc,))


def _adamw(w, g, m, v, name):
    R, C = w.shape
    tr = _tile(R, 256, SUBLANES)
    tc = _tile(C, 2048, LANES)
    c1 = 1.0 - ADAM_B1 ** ADAM_STEP
    c2 = 1.0 - ADAM_B2 ** ADAM_STEP

    def body(w_ref, g_ref, m_ref, v_ref, d_ref, mo_ref, vo_ref, go_ref):
        gv = g_ref[...]
        go_ref[...] = gv
        mn = ADAM_B1 * m_ref[...] + (1.0 - ADAM_B1) * gv
        vn = ADAM_B2 * v_ref[...] + (1.0 - ADAM_B2) * (gv * gv)
        mo_ref[...] = mn
        vo_ref[...] = vn
        d_ref[...] = -ADAM_LR * ((mn / c1) / (jnp.sqrt(vn / c2) + ADAM_EPS) + ADAM_WD * w_ref[...])

    blk = pl.BlockSpec((tr, tc), lambda i, j: (i, j))
    sh = jax.ShapeDtypeStruct((R, C), F32)
    return pl.pallas_call(
        body, name=name, grid=(R // tr, C // tc),
        in_specs=[blk] * 4, out_specs=[blk] * 4, out_shape=[sh] * 4,
        compiler_params=_params("parallel", "parallel"),
    )(w, g, m, v)


def _other_chips(x, y):
    return [(1 - x, y), (x, 1 - y), (1 - x, 1 - y)]


def _remote(src, dst, send_sems, recv_sems, idx, dev):
    return pltpu.make_async_remote_copy(src_ref=src, dst_ref=dst, send_sem=send_sems.at[idx], recv_sem=recv_sems.at[idx],
                                        device_id=dev, device_id_type=MESH)


def _gather_carry(fulls, axes):
    M = len(fulls)

    def win(outs, m, qq, cc):
        if axes[m] == 1:
            W = fulls[m].shape[2] // 4
            return outs[m].at[cc, :, pl.ds(pl.multiple_of(qq * W, LANES), W)]
        return outs[m].at[2 * qq + cc]

    def ici(outs, sems, m, j, src_q):
        x, y, c = _me()
        cx, cy = _other_chips(x, y)[j]
        blk = win(outs, m, src_q, c)
        return _remote(blk, blk, sems[0], sems[1], 6 * m + j, (cx, cy, c))

    def d2d(outs, sems, m, j, half):
        x, y, c = _me()
        cx, cy = _other_chips(x, y)[j]
        blk = win(outs, m, 2 * cx + cy, half)
        return _remote(blk, blk, sems[0], sems[1], 6 * m + 3 + j, (x, y, 1 - c))

    def start(ins, outs, sems):
        x, y, c = _me()
        for m in range(M):
            for j in range(3):
                ici(outs, sems, m, j, 2 * x + y).start()

    def middle(ins, outs, sems):
        x, y, c = _me()
        for m in range(M):
            for j, (cx, cy) in enumerate(_other_chips(x, y)):
                ici(outs, sems, m, j, 2 * cx + cy).wait_recv()
                d2d(outs, sems, m, j, c).start()

    def finish(ins, outs, sems):
        x, y, c = _me()
        for m in range(M):
            for j in range(3):
                d2d(outs, sems, m, j, 1 - c).wait_recv()
        for m in range(M):
            for j in range(3):
                ici(outs, sems, m, j, 2 * x + y).wait_send()
                d2d(outs, sems, m, j, c).wait_send()

    return _Carry(fulls, [jax.ShapeDtypeStruct(f.shape, f.dtype) for f in fulls], {m: m for m in range(M)},
                  [pltpu.SemaphoreType.DMA((6 * M,)), pltpu.SemaphoreType.DMA((6 * M,))], start, finish, middle)


def _gather_two_way_carry(fulls, axes):
    M = len(fulls)

    def part(outs, m, qq, cc, p):
        Rp = fulls[m].shape[1] // 2
        rows = pl.ds(p * Rp, Rp)
        if axes[m] == 1:
            W = fulls[m].shape[2] // 4
            return outs[m].at[cc, rows, pl.ds(pl.multiple_of(qq * W, LANES), W)]
        return outs[m].at[2 * qq + cc, rows, :]

    def copies(outs, sems):
        x, y, c = _me()
        q, qx, qy, qd = 2 * x + y, 2 * (1 - x) + y, 2 * x + (1 - y), 2 * (1 - x) + (1 - y)
        xn, yn, sib = (1 - x, y, c), (x, 1 - y, c), (x, y, 1 - c)
        table = {}
        for m in range(M):
            def cp(blk, k, dev):
                return _remote(blk, blk, sems[0], sems[1], 12 * m + k, dev)
            own0, own1 = part(outs, m, q, c, 0), part(outs, m, q, c, 1)
            table[m] = dict(
                to=[cp(own0, 0, xn), cp(own1, 1, yn), cp(own1, 2, xn), cp(own0, 3, yn)],
                landed=[cp(part(outs, m, qx, c, 0), 0, xn), cp(part(outs, m, qy, c, 1), 1, yn),
                        cp(part(outs, m, qx, c, 1), 2, xn), cp(part(outs, m, qy, c, 0), 3, yn),
                        cp(part(outs, m, qd, c, 0), 4, yn), cp(part(outs, m, qd, c, 1), 5, xn)],
                passed=[cp(part(outs, m, qx, c, 0), 4, yn), cp(part(outs, m, qy, c, 1), 5, xn)],
                handed=[cp(part(outs, m, qq, c, p), 6 + k, sib)
                        for k, (qq, p) in enumerate([(qx, 0), (qy, 1), (qx, 1), (qy, 0), (qd, 0), (qd, 1)])],
                taken=[cp(part(outs, m, qq, 1 - c, p), 6 + k, sib)
                       for k, (qq, p) in enumerate([(qx, 0), (qy, 1), (qx, 1), (qy, 0), (qd, 0), (qd, 1)])])
        return table

    def start(ins, outs, sems):
        t = copies(outs, sems)
        for m in range(M):
            for cp in t[m]['to']:
                cp.start()

    def finish(ins, outs, sems):
        t = copies(outs, sems)
        for m in range(M):
            for k in range(4):
                t[m]['landed'][k].wait_recv()
                if k < 2:
                    t[m]['passed'][k].start()
                t[m]['handed'][k].start()
        for m in range(M):
            for k in (4, 5):
                t[m]['landed'][k].wait_recv()
                t[m]['handed'][k].start()
        for m in range(M):
            for cp in t[m]['taken']:
                cp.wait_recv()
            for cp in t[m]['to'] + t[m]['passed'] + t[m]['handed']:
                cp.wait_send()

    return _Carry(fulls, [jax.ShapeDtypeStruct(f.shape, f.dtype) for f in fulls], {m: m for m in range(M)},
                  [pltpu.SemaphoreType.DMA((12 * M,)), pltpu.SemaphoreType.DMA((12 * M,))], start, finish)


def _pair_exchange_carry(parts):
    M = len(parts)
    groups = [p.shape[0] // 2 for p in parts]
    base = [sum(groups[:m]) for m in range(M)]
    out_shape = [jax.ShapeDtypeStruct((g,) + p.shape[1:], p.dtype) for g, p in zip(groups, parts)]

    def copies(ins, outs, sems):
        x, y, c = _me()
        return [_remote(ins[m].at[2 * g + 1 - c], outs[m].at[g], sems[0], sems[1], base[m] + g, (x, y, 1 - c))
                for m in range(M) for g in range(groups[m])]

    def start(ins, outs, sems):
        for cp in copies(ins, outs, sems):
            cp.start()

    def finish(ins, outs, sems):
        for cp in copies(ins, outs, sems):
            cp.wait()

    n = sum(groups)
    return _Carry(parts, out_shape, {}, [pltpu.SemaphoreType.DMA((n,)), pltpu.SemaphoreType.DMA((n,))], start, finish)


def _chip_exchange_carry(sums, axes):
    M = len(sums)
    out_shape = []
    for s, ax in zip(sums, axes):
        _, Rh, C = s.shape
        out_shape.append(jax.ShapeDtypeStruct((3, Rh, C // 4 if ax == 1 else C), s.dtype))

    def copies(ins, outs, sems):
        x, y, c = _me()
        cps = []
        for m in range(M):
            for j, (cx, cy) in enumerate(_other_chips(x, y)):
                qj = 2 * cx + cy
                if axes[m] == 1:
                    W = sums[m].shape[2] // 4
                    src = ins[m].at[0, :, pl.ds(pl.multiple_of(qj * W, LANES), W)]
                else:
                    src = ins[m].at[qj]
                cps.append(_remote(src, outs[m].at[j], sems[0], sems[1], 3 * m + j, (cx, cy, c)))
        return cps

    def start(ins, outs, sems):
        for cp in copies(ins, outs, sems):
            cp.start()

    def finish(ins, outs, sems):
        for cp in copies(ins, outs, sems):
            cp.wait()

    return _Carry(sums, out_shape, {}, [pltpu.SemaphoreType.DMA((3 * M,)), pltpu.SemaphoreType.DMA((3 * M,))],
                  start, finish)


def _pair_share_carry(bufs):
    M = len(bufs)

    def start(ins, outs, sems):
        x, y, c = _me()
        for m in range(M):
            _remote(outs[m].at[c], outs[m].at[c], sems[0], sems[1], m, (x, y, 1 - c)).start()

    def finish(ins, outs, sems):
        x, y, c = _me()
        for m in range(M):
            _remote(outs[m].at[c], outs[m].at[c], sems[0], sems[1], m, (x, y, 1 - c)).wait_send()
            _remote(outs[m].at[1 - c], outs[m].at[1 - c], sems[0], sems[1], m, (x, y, 1 - c)).wait_recv()

    return _Carry(bufs, [jax.ShapeDtypeStruct(b.shape, b.dtype) for b in bufs], {m: m for m in range(M)},
                  [pltpu.SemaphoreType.DMA((M,)), pltpu.SemaphoreType.DMA((M,))], start, finish)


def _allreduce_small(v, name):
    R, W = v.shape
    Rh = R // 2

    def body(v_ref, o_ref, sib, quad, send_sems, recv_sems):
        x, y, c = _me()
        q = 2 * x + y
        sibling = (x, y, 1 - c)
        pair = _remote(v_ref, sib, send_sems, recv_sems, 0, sibling)
        pair.start()
        pair.wait()
        mine = pl.ds(pl.multiple_of(c * Rh, SUBLANES), Rh)
        quad[0] = v_ref[mine, :] + sib[mine, :]
        cps = []
        for k in (1, 2, 3):
            peer = (1 - x if k & 2 else x, 1 - y if k & 1 else y, c)
            cps.append(_remote(quad.at[0], quad.at[k], send_sems, recv_sems, k, peer))
            cps[-1].start()
        for cp in cps:
            cp.wait()
        acc = quad[q]
        for p in (1, 2, 3):
            acc = acc + quad[jnp.bitwise_xor(q, p)]
        o_ref[mine, :] = acc
        theirs = pl.ds(pl.multiple_of((1 - c) * Rh, SUBLANES), Rh)
        done = _remote(o_ref.at[mine, :], o_ref.at[mine, :], send_sems, recv_sems, 4, sibling)
        done.start()
        done.wait_send()
        _remote(o_ref.at[theirs, :], o_ref.at[theirs, :], send_sems, recv_sems, 4, sibling).wait_recv()

    vm = pl.BlockSpec(memory_space=pltpu.VMEM)
    return pl.pallas_call(
        body, name=name, in_specs=[vm], out_specs=vm, out_shape=jax.ShapeDtypeStruct((R, W), F32),
        scratch_shapes=[pltpu.VMEM((R, W), F32), pltpu.VMEM((4, Rh, W), F32),
                        pltpu.SemaphoreType.DMA((5,)), pltpu.SemaphoreType.DMA((5,))],
        compiler_params=pltpu.CompilerParams(vmem_limit_bytes=VMEM_LIMIT),
    )(v)


def _pack(pieces):
    flat = []
    for p in pieces:
        p = p.reshape(-1).astype(F32)
        pad = (-p.shape[0]) % PACK_ALIGN
        flat.append(jnp.pad(p, (0, pad)).reshape(-1, PACK_W))
    if sum(f.shape[0] for f in flat) % (2 * SUBLANES):
        flat.append(jnp.zeros((SUBLANES, PACK_W), F32))
    return jnp.concatenate(flat, axis=0)


def _unpack(packed, shapes):
    out, row = [], 0
    for shp in shapes:
        n = math.prod(shp)
        rows = -(-n // PACK_ALIGN) * SUBLANES
        out.append(packed[row:row + rows].reshape(-1)[:n].reshape(shp))
        row += rows
    return out


def kernel(x, ffn1_norm, ffn1_w_gate, ffn1_w_up, ffn1_w_down, mix_norm, w_in, lru_conv_w, lru_conv_b, lru_w_a, lru_b_a, lru_w_i, lru_b_i, lru_lambda, sc_conv_w, lru_out_norm, sc_out_norm, w_out, ffn2_norm, ffn2_w_gate, ffn2_w_up, ffn2_w_down, final_norm, loss_target, m_ffn1_norm, m_ffn1_w_gate, m_ffn1_w_up, m_ffn1_w_down, m_mix_norm, m_w_in, m_lru_conv_w, m_lru_conv_b, m_lru_w_a, m_lru_b_a, m_lru_w_i, m_lru_b_i, m_lru_lambda, m_sc_conv_w, m_lru_out_norm, m_sc_out_norm, m_w_out, m_ffn2_norm, m_ffn2_w_gate, m_ffn2_w_up, m_ffn2_w_down, m_final_norm, v_ffn1_norm, v_ffn1_w_gate, v_ffn1_w_up, v_ffn1_w_down, v_mix_norm, v_w_in, v_lru_conv_w, v_lru_conv_b, v_lru_w_a, v_lru_b_a, v_lru_w_i, v_lru_b_i, v_lru_lambda, v_sc_conv_w, v_lru_out_norm, v_sc_out_norm, v_w_out, v_ffn2_norm, v_ffn2_w_gate, v_ffn2_w_up, v_ffn2_w_down, v_final_norm):
    vals = locals()
    w = {n: vals[n] for n in WEIGHTS}
    mom = {n: vals["m_" + n] for n in WEIGHTS}
    var = {n: vals["v_" + n] for n in WEIGHTS}

    xi, yi, ci = _me()
    qi = 2 * xi + yi
    c_arr = jnp.reshape(ci, (1,)).astype(jnp.int32)
    qc_arr = jnp.stack([qi, ci]).astype(jnp.int32)

    T, D = x.shape[1], x.shape[2]
    xt = x.reshape(T, D)
    target = loss_target.reshape(T, D)
    DL = lru_conv_b.shape[-1]
    NH, HD = lru_w_a.shape[1], lru_w_a.shape[2]
    KL, KS = lru_conv_w.shape[1], sc_conv_w.shape[1]
    DLq = lru_conv_w.shape[2]

    axis_of = dict(zip(BIG, BIG_AXIS))
    first_names = ['ffn1_w_gate', 'ffn1_w_up', 'ffn1_w_down']
    later_names = ['w_in', 'w_out', 'ffn2_w_gate', 'ffn2_w_up', 'ffn2_w_down']

    def unview(n, g):
        return g.reshape(2 * g.shape[1], g.shape[2]) if axis_of[n] == 1 else g.reshape(8 * g.shape[1], g.shape[2])

    first_axes, later_axes = [axis_of[n] for n in first_names], [axis_of[n] for n in later_names]
    placed = _cast_into_full([w[n][0] for n in first_names], qc_arr, first_axes, "cast_ffn1_weights")
    res = _cast_into_full([w[n][0] for n in later_names], qc_arr, later_axes, "cast_later_weights",
                          _gather_two_way_carry(placed, first_axes))
    full = {n: unview(n, g) for n, g in zip(first_names, res[len(later_names):])}
    gather_later = _gather_carry(res[:len(later_names)], later_axes)

    taps = jnp.zeros((2 * SUBLANES, DL), F32)
    taps = lax.dynamic_update_slice(taps, lru_conv_w[0], (0, qi * DLq))
    taps = lax.dynamic_update_slice(taps, sc_conv_w[0], (KL, qi * DLq))
    taps = _allreduce_small(jnp.where(ci == 0, taps, 0.0), "gather_conv_taps")
    cw, sw = taps[0:KL], taps[KL:KL + KS]

    cb = lru_conv_b
    wa, wi = lru_w_a[0].astype(BF16), lru_w_i[0].astype(BF16)
    ba, bi = lru_b_a.reshape(1, DL), lru_b_i.reshape(1, DL)
    mix_args = (cw, cb, wa, ba, wi, bi, lru_lambda, sw, lru_out_norm, sc_out_norm)
    gf = final_norm.reshape(1, D)

    res = _ffn_fwd(xt, ffn1_norm, full['ffn1_w_gate'], full['ffn1_w_up'], full['ffn1_w_down'], "ffn1_fwd", gather_later)
    x1, n1, G1, U1 = res[:4]
    full.update({n: unview(n, g) for n, g in zip(later_names, res[4:])})
    n2, z = _norm_mm(x1, mix_norm, full['w_in'], "mix_in_proj")
    h, ymix = _mix_fwd(z, *mix_args, "mix_fwd")
    x2 = _mm_fullk(ymix, full['w_out'], False, x1, F32, "mix_out_proj")[0]
    x3, n3, G2, U2 = _ffn_fwd(x2, ffn2_norm, full['ffn2_w_gate'], full['ffn2_w_up'], full['ffn2_w_down'], "ffn2_fwd")
    dx3, d3b, sqerr, dgf = _loss_head(x3, gf, target, "loss_head")

    grads, halves, shared = {}, {}, {}

    def pair_carry(names):
        parts = []
        for n in names:
            R, C = grads[n].shape
            parts.append(grads[n].reshape(2, R // 2, C) if axis_of[n] == 1 else grads[n].reshape(8, R // 8, C))
        return parts, _pair_exchange_carry(parts)

    def pair_add(names, parts, recv):
        return [_pair_add(p, r, c_arr, "grad_pair_add_" + n) for n, p, r in zip(names, parts, recv)]

    def chip_carry(names, sums):
        return _chip_exchange_carry(sums, [axis_of[n] for n in names])

    def chip_add(names, sums, recv):
        for n, s, r in zip(names, sums, recv):
            halves[n] = _quad_add(s, r, qc_arr, axis_of[n], "grad_chip_add_" + n)

    dG2, dU2, H2 = _ffn_bwd_hidden(d3b, G2, U2, full['ffn2_w_down'], "ffn2_bwd_hidden")
    dn3 = _ffn_bwd_input(dG2, dU2, full['ffn2_w_gate'], full['ffn2_w_up'], "ffn2_bwd_input")[0]
    dx2, dx2b, dg_ffn2 = _rms_bwd_res(dn3, x2, ffn2_norm, dx3, 1.0, "ffn2_norm_bwd")
    grads['ffn2_w_gate'] = _mm_tn(n3, dG2, "ffn2_dwg")[0]
    grads['ffn2_w_up'] = _mm_tn(n3, dU2, "ffn2_dwu")[0]
    grads['ffn2_w_down'] = _mm_tn(H2, d3b, "ffn2_dwd")[0]
    names_a = ['ffn2_w_gate', 'ffn2_w_up', 'ffn2_w_down', 'w_out', 'w_in']
    parts, carry = pair_carry(names_a[:3])
    res = _mm_fullk(dx2b, full['w_out'], True, None, BF16, "mix_out_bwd", carry)
    dy = res[0]
    sums_a = pair_add(names_a[:3], parts, res[1:])
    grads['w_out'] = _mm_tn(ymix, dx2b, "mix_dwout")[0]
    dz, dwa, dwi, vec = _mix_bwd(z, h, dy, *mix_args, "mix_bwd")
    grads['w_in'] = _mm_tn(n2, dz, "mix_dwin")[0]
    parts, carry = pair_carry(names_a[3:])
    res = _mm_nt_norm_bwd(dz, full['w_in'], x1, mix_norm, dx2, FFN_RESIDUAL_SCALE, "mix_in_bwd", carry)
    dg_mix, dx1, d1b = res[:3]
    sums_a += pair_add(names_a[3:], parts, res[3:])

    res = _ffn_bwd_hidden(d1b, G1, U1, full['ffn1_w_down'], "ffn1_bwd_hidden", chip_carry(names_a[3:], sums_a[3:]))
    dG1, dU1, H1 = res[:3]
    chip_add(names_a[3:], sums_a[3:], res[3:])
    res = _ffn_bwd_input(dG1, dU1, full['ffn1_w_gate'], full['ffn1_w_up'], "ffn1_bwd_input",
                         chip_carry(names_a[:3], sums_a[:3]))
    dn1 = res[0]
    chip_add(names_a[:3], sums_a[:3], res[1:])
    dx0, _, dg_ffn1 = _rms_bwd_res(dn1, xt, ffn1_norm, dx1, 1.0, "ffn1_norm_bwd")
    res = _mm_tn(n1, dG1, "ffn1_dwg", _pair_share_carry([halves[n] for n in names_a]))
    grads['ffn1_w_gate'] = res[0]
    shared.update(zip(names_a, res[1:]))
    names_b = ['ffn1_w_gate', 'ffn1_w_up']
    parts_g, carry = pair_carry(names_b[:1])
    res = _mm_tn(n1, dU1, "ffn1_dwu", carry)
    grads['ffn1_w_up'] = res[0]
    sums_b = pair_add(names_b[:1], parts_g, res[1:])
    parts_u, carry = pair_carry(names_b[1:])
    sums_b += pair_add(names_b[1:], parts_u, _run_carry(carry, "grad_pair_exchange_ffn1_up"))
    res = _mm_tn(H1, d1b, "ffn1_dwd", chip_carry(names_b, sums_b))
    grads['ffn1_w_down'] = res[0]
    chip_add(names_b, sums_b, res[1:])
    names_c = ['ffn1_w_down']
    parts, carry = pair_carry(names_c)
    sums_c = pair_add(names_c, parts, _run_carry(carry, "grad_pair_exchange_ffn1_out"))
    chip_add(names_c, sums_c, _run_carry(chip_carry(names_c, sums_c), "grad_chip_exchange_ffn1_out"))
    names_bc = names_b + names_c
    shared.update(zip(names_bc, _run_carry(_pair_share_carry([halves[n] for n in names_bc]), "grad_pair_share")))
    out_g, out_d, out_m, out_v = {}, {}, {}, {}
    for n in BIG:
        shp = w[n].shape
        outs = _adamw(w[n][0], shared[n].reshape(shp[1], shp[2]), mom[n][0], var[n][0], "adamw_" + n)
        out_d[n], out_m[n], out_v[n], out_g[n] = (a.reshape(shp) for a in outs)

    small = [n for n in WEIGHTS if n not in BIG]
    local_small = {
        'ffn1_norm': dg_ffn1, 'mix_norm': dg_mix, 'lru_conv_w': vec[V_CW:V_CW + KL], 'lru_conv_b': vec[V_CB],
        'lru_w_a': dwa, 'lru_b_a': vec[V_BA], 'lru_w_i': dwi, 'lru_b_i': vec[V_BI], 'lru_lambda': vec[V_LAM],
        'sc_conv_w': vec[V_SW:V_SW + KS], 'lru_out_norm': vec[V_GLO], 'sc_out_norm': vec[V_GSO],
        'ffn2_norm': dg_ffn2, 'final_norm': dgf,
    }
    full_shapes = [local_small[n].shape for n in small] + [(1,)]
    reduced = _allreduce_small(_pack([local_small[n] for n in small] + [sqerr[0, 0:1]]), "allreduce_small")
    reduced = _unpack(reduced, full_shapes)
    loss = (0.5 / D) * reduced[-1][0]
    gsm = {}
    for n, g in zip(small, reduced[:-1]):
        if n in SMALL_SHARDED:
            g = lax.dynamic_slice(g, (0, qi * DLq), (g.shape[0], DLq))
        gsm[n] = g.reshape(w[n].shape)
    small_shapes = [w[n].shape for n in small]
    d_s, m_s, v_s, _ = _adamw(_pack([w[n] for n in small]), _pack([gsm[n] for n in small]),
                              _pack([mom[n] for n in small]), _pack([var[n] for n in small]), "adamw_small")
    for n, d, mn, vn in zip(small, _unpack(d_s, small_shapes), _unpack(m_s, small_shapes), _unpack(v_s, small_shapes)):
        out_g[n], out_d[n], out_m[n], out_v[n] = gsm[n], d, mn, vn

    return (loss, dx0.reshape(x.shape), *[out_g[n] for n in WEIGHTS], *[out_d[n] for n in WEIGHTS],
            *[out_m[n] for n in WEIGHTS], *[out_v[n] for n in WEIGHTS])
```

```python
import math

import jax
import jax.numpy as jnp
from jax import lax
from jax.experimental import pallas as pl
from jax.experimental.pallas import tpu as pltpu

F32 = jnp.float32
BF16 = jnp.bfloat16
MESH = pl.DeviceIdType.MESH
ANY = pl.BlockSpec(memory_space=pl.ANY)

NORM_EPS = 1e-6
LRU_C = 8.0
FFN_RESIDUAL_SCALE = 0.5
ADAM_LR = 0.001
ADAM_B1 = 0.9
ADAM_B2 = 0.999
ADAM_EPS = 1e-08
ADAM_WD = 0.01
ADAM_STEP = 10

V7X_VMEM_BYTES = 64 * 2**20
VMEM_LIMIT = V7X_VMEM_BYTES - 8 * 2**20
LANES = 128
SUBLANES = 8
PACK_W = LANES
PACK_ALIGN = SUBLANES * PACK_W

WEIGHTS = ['ffn1_norm', 'ffn1_w_gate', 'ffn1_w_up', 'ffn1_w_down', 'mix_norm', 'w_in', 'lru_conv_w', 'lru_conv_b',
           'lru_w_a', 'lru_b_a', 'lru_w_i', 'lru_b_i', 'lru_lambda', 'sc_conv_w', 'lru_out_norm', 'sc_out_norm',
           'w_out', 'ffn2_norm', 'ffn2_w_gate', 'ffn2_w_up', 'ffn2_w_down', 'final_norm']
BIG = ['ffn1_w_gate', 'ffn1_w_up', 'ffn1_w_down', 'w_in', 'w_out', 'ffn2_w_gate', 'ffn2_w_up', 'ffn2_w_down']
BIG_AXIS = [1, 1, 0, 1, 0, 1, 1, 0]
SMALL_SHARDED = ['lru_conv_w', 'sc_conv_w']


def _tile(n, pref, mult):
    if n <= pref:
        return n
    t = (pref // mult) * mult
    while t >= mult:
        if n % t == 0:
            return t
        t -= mult
    return n


def _params(*sem):
    return pltpu.CompilerParams(dimension_semantics=sem, vmem_limit_bytes=VMEM_LIMIT)


def _me():
    return lax.axis_index("x"), lax.axis_index("y"), lax.axis_index("c")


def _sigmoid(v):
    return 1.0 / (1.0 + jnp.exp(-v))


def _rstd(v):
    return lax.rsqrt(jnp.mean(v * v, axis=-1, keepdims=True) + NORM_EPS)


def _rms_bwd(dy, v, gain):
    r = _rstd(v)
    w = gain * dy
    dv = r * w - v * (r * r * r) * jnp.mean(v * w, axis=-1, keepdims=True)
    dgain = jnp.sum(dy * v * r, axis=0, keepdims=True)
    return dv, dgain


def _dot_nt(a, b):
    return lax.dot_general(a, b, (((1,), (1,)), ((), ())), preferred_element_type=F32)


def _dot_tn(a, b):
    return lax.dot_general(a, b, (((0,), (0,)), ((), ())), preferred_element_type=F32)


class _Carry:
    def __init__(self, inputs, out_shape, aliases, sems, start, finish, middle=None, middle_at=0.85):
        self.inputs, self.out_shape, self.aliases, self.sems = list(inputs), list(out_shape), dict(aliases), list(sems)
        self.start, self.finish = start, finish
        self.middle, self.middle_at = middle, middle_at


def _call(body, name, grid, in_specs, out_specs, out_shape, scratch_shapes, semantics, args, carry=None, prefetch=()):
    np_ = len(prefetch)
    if carry is None:
        spec = pltpu.PrefetchScalarGridSpec(num_scalar_prefetch=np_, grid=grid, in_specs=in_specs, out_specs=out_specs,
                                            scratch_shapes=scratch_shapes)
        return pl.pallas_call(body, name=name, grid_spec=spec, out_shape=out_shape,
                              compiler_params=_params(*semantics))(*prefetch, *args)
    ni, no, ns = len(in_specs), len(out_specs), len(scratch_shapes)
    ci, co = len(carry.inputs), len(carry.out_shape)

    def carrying(*refs):
        pre, refs = refs[:np_], refs[np_:]
        ins, refs = refs[:ni], refs[ni:]
        cins, refs = refs[:ci], refs[ci:]
        outs, refs = refs[:no], refs[no:]
        couts, refs = refs[:co], refs[co:]
        scratch, csems = refs[:ns], refs[ns:]
        step = pl.program_id(0)
        for ax in range(1, len(grid)):
            step = step * grid[ax] + pl.program_id(ax)
        steps = math.prod(grid)
        first = step == 0
        last = step == steps - 1

        @pl.when(first)
        def _():
            carry.start(cins, couts, csems)

        if carry.middle is not None:
            @pl.when(step == min(int(carry.middle_at * steps), steps - 1))
            def _():
                carry.middle(cins, couts, csems)

        body(*pre, *ins, *outs, *scratch)

        @pl.when(last)
        def _():
            carry.finish(cins, couts, csems)

    spec = pltpu.PrefetchScalarGridSpec(
        num_scalar_prefetch=np_, grid=grid, in_specs=list(in_specs) + [ANY] * ci,
        out_specs=list(out_specs) + [ANY] * co, scratch_shapes=list(scratch_shapes) + carry.sems)
    return pl.pallas_call(
        carrying, name=name, grid_spec=spec, out_shape=list(out_shape) + carry.out_shape,
        input_output_aliases={np_ + ni + i: no + j for i, j in carry.aliases.items()},
        compiler_params=_params(*(["arbitrary"] * len(grid))),
    )(*prefetch, *args, *carry.inputs)


def _run_carry(carry, name):
    ci, co = len(carry.inputs), len(carry.out_shape)

    def body(*refs):
        cins, couts, csems = refs[:ci], refs[ci:ci + co], refs[ci + co:]
        carry.start(cins, couts, csems)
        if carry.middle is not None:
            carry.middle(cins, couts, csems)
        carry.finish(cins, couts, csems)

    return pl.pallas_call(body, name=name, in_specs=[ANY] * ci, out_specs=[ANY] * co, out_shape=carry.out_shape,
                          input_output_aliases=carry.aliases, scratch_shapes=carry.sems)(*carry.inputs)


def _ffn_fwd(x, gain, wg, wu, wd, name, carry=None):
    T, D = x.shape
    FF = wg.shape[1]
    tm = _tile(T, 512, 16)
    tf = _tile(FF, 512, LANES)
    nf = FF // tf

    def body(x_ref, g_ref, wg_ref, wu_ref, wd_ref, xo_ref, n_ref, G_ref, U_ref, acc_ref):
        f = pl.program_id(1)

        @pl.when(f == 0)
        def _():
            xv = x_ref[...]
            n_ref[...] = (xv * _rstd(xv) * g_ref[...]).astype(BF16)
            acc_ref[...] = jnp.zeros_like(acc_ref)

        n = n_ref[...]
        G = jnp.dot(n, wg_ref[...], preferred_element_type=F32)
        U = jnp.dot(n, wu_ref[...], preferred_element_type=F32)
        G_ref[...] = G.astype(BF16)
        U_ref[...] = U.astype(BF16)
        H = (G * _sigmoid(G) * U).astype(BF16)
        acc_ref[...] += jnp.dot(H, wd_ref[...], preferred_element_type=F32)

        @pl.when(f == nf - 1)
        def _():
            xo_ref[...] = x_ref[...] + FFN_RESIDUAL_SCALE * acc_ref[...]

    return _call(
        body, name, (T // tm, nf),
        [pl.BlockSpec((tm, D), lambda i, f: (i, 0)),
         pl.BlockSpec((1, D), lambda i, f: (0, 0)),
         pl.BlockSpec((D, tf), lambda i, f: (0, f)),
         pl.BlockSpec((D, tf), lambda i, f: (0, f)),
         pl.BlockSpec((tf, D), lambda i, f: (f, 0))],
        [pl.BlockSpec((tm, D), lambda i, f: (i, 0)),
         pl.BlockSpec((tm, D), lambda i, f: (i, 0)),
         pl.BlockSpec((tm, tf), lambda i, f: (i, f)),
         pl.BlockSpec((tm, tf), lambda i, f: (i, f))],
        [jax.ShapeDtypeStruct((T, D), F32), jax.ShapeDtypeStruct((T, D), BF16),
         jax.ShapeDtypeStruct((T, FF), BF16), jax.ShapeDtypeStruct((T, FF), BF16)],
        [pltpu.VMEM((tm, D), F32)], ("parallel", "arbitrary"), (x, gain, wg, wu, wd), carry)


def _ffn_bwd_act(db, G, U, wg, wu, wd, name, carry=None):
    T, D = db.shape
    FF = wg.shape[1]
    tm = _tile(T, 512, 16)
    tf = _tile(FF, 512, LANES)

    def body(d_ref, G_ref, U_ref, wg_ref, wu_ref, wd_ref, dG_ref, dU_ref, H_ref, dn_ref):
        f = pl.program_id(1)
        dH = _dot_nt(d_ref[...], wd_ref[...])
        Gv = G_ref[...].astype(F32)
        Uv = U_ref[...].astype(F32)
        s = _sigmoid(Gv)
        sg = Gv * s
        H_ref[...] = (sg * Uv).astype(BF16)
        dU = (dH * sg).astype(BF16)
        dG = (dH * Uv * (s * (1.0 + Gv * (1.0 - s)))).astype(BF16)
        dG_ref[...] = dG
        dU_ref[...] = dU
        contrib = _dot_nt(dG, wg_ref[...]) + _dot_nt(dU, wu_ref[...])

        @pl.when(f == 0)
        def _():
            dn_ref[...] = contrib

        @pl.when(f > 0)
        def _():
            dn_ref[...] += contrib

    return _call(
        body, name, (T // tm, FF // tf),
        [pl.BlockSpec((tm, D), lambda i, f: (i, 0)),
         pl.BlockSpec((tm, tf), lambda i, f: (i, f)),
         pl.BlockSpec((tm, tf), lambda i, f: (i, f)),
         pl.BlockSpec((D, tf), lambda i, f: (0, f)),
         pl.BlockSpec((D, tf), lambda i, f: (0, f)),
         pl.BlockSpec((tf, D), lambda i, f: (f, 0))],
        [pl.BlockSpec((tm, tf), lambda i, f: (i, f)),
         pl.BlockSpec((tm, tf), lambda i, f: (i, f)),
         pl.BlockSpec((tm, tf), lambda i, f: (i, f)),
         pl.BlockSpec((tm, D), lambda i, f: (i, 0))],
        [jax.ShapeDtypeStruct((T, FF), BF16), jax.ShapeDtypeStruct((T, FF), BF16),
         jax.ShapeDtypeStruct((T, FF), BF16), jax.ShapeDtypeStruct((T, D), F32)],
        [], ("parallel", "arbitrary"), (db, G, U, wg, wu, wd), carry)


def _ffn_bwd_hidden(db, G, U, wd, name, carry=None):
    T, D = db.shape
    FF = wd.shape[0]
    tm = _tile(T, 1024, 16)
    tf = _tile(FF, 512, LANES)

    def body(d_ref, G_ref, U_ref, wd_ref, dG_ref, dU_ref, H_ref):
        dH = _dot_nt(d_ref[...], wd_ref[...])
        Gv = G_ref[...].astype(F32)
        Uv = U_ref[...].astype(F32)
        s = _sigmoid(Gv)
        sg = Gv * s
        H_ref[...] = (sg * Uv).astype(BF16)
        dU_ref[...] = (dH * sg).astype(BF16)
        dG_ref[...] = (dH * Uv * (s * (1.0 + Gv * (1.0 - s)))).astype(BF16)

    act = pl.BlockSpec((tm, tf), lambda i, f: (i, f))
    return _call(
        body, name, (T // tm, FF // tf),
        [pl.BlockSpec((tm, D), lambda i, f: (i, 0)), act, act, pl.BlockSpec((tf, D), lambda i, f: (f, 0))],
        [act, act, act], [jax.ShapeDtypeStruct((T, FF), BF16)] * 3,
        [], ("parallel", "arbitrary"), (db, G, U, wd), carry)


def _ffn_bwd_input(dG, dU, wg, wu, name, carry=None):
    T, FF = dG.shape
    D = wg.shape[0]
    tm = _tile(T, 512, 16)
    tn = _tile(D, 512, LANES)

    def body(dG_ref, dU_ref, wg_ref, wu_ref, dn_ref):
        dn_ref[...] = _dot_nt(dG_ref[...], wg_ref[...]) + _dot_nt(dU_ref[...], wu_ref[...])

    act = pl.BlockSpec((tm, FF), lambda i, j: (i, 0))
    wt = pl.BlockSpec((tn, FF), lambda i, j: (j, 0))
    return _call(body, name, (T // tm, D // tn), [act, act, wt, wt], [pl.BlockSpec((tm, tn), lambda i, j: (i, j))],
                 [jax.ShapeDtypeStruct((T, D), F32)], [], ("parallel", "arbitrary"), (dG, dU, wg, wu), carry)


TAIL_ROWS = 128


class _NormBwdTail:
    def __init__(self, T, D, tm, scale):
        self.T, self.D, self.tm, self.scale = T, D, tm, scale
        self.ni = T // tm
        self.scratch = [pltpu.VMEM((tm, D), F32), pltpu.VMEM((tm, D), F32), pltpu.VMEM((tm, D), F32),
                        pltpu.VMEM((tm, D), BF16), pltpu.SemaphoreType.DMA((4,))]
        self.out_shape = [jax.ShapeDtypeStruct((T, D), F32), jax.ShapeDtypeStruct((T, D), BF16)]

    def _rows(self, k):
        return pl.ds(pl.multiple_of(k * self.tm, self.tm), self.tm)

    def _loads(self, k, x_hbm, r_hbm, bufs):
        xbuf, rbuf, _, _, sems = bufs
        return [pltpu.make_async_copy(x_hbm.at[self._rows(k)], xbuf, sems.at[0]),
                pltpu.make_async_copy(r_hbm.at[self._rows(k)], rbuf, sems.at[1])]

    def _stores(self, k, dx_hbm, dxb_hbm, bufs):
        _, _, obuf, obbuf, sems = bufs
        return [pltpu.make_async_copy(obuf, dx_hbm.at[self._rows(k)], sems.at[2]),
                pltpu.make_async_copy(obbuf, dxb_hbm.at[self._rows(k)], sems.at[3])]

    def prefetch(self, i, x_hbm, r_hbm, bufs):
        for cp in self._loads(i, x_hbm, r_hbm, bufs):
            cp.start()

    def run(self, i, acc_ref, g_ref, x_hbm, r_hbm, dx_hbm, dxb_hbm, dg_ref, bufs):
        xbuf, rbuf, obuf, obbuf, _ = bufs
        for cp in self._loads(i, x_hbm, r_hbm, bufs):
            cp.wait()

        @pl.when(i > 0)
        def _():
            for cp in self._stores(i - 1, dx_hbm, dxb_hbm, bufs):
                cp.wait()

        dgain = None
        for r0 in range(0, self.tm, TAIL_ROWS):
            rs = slice(r0, min(r0 + TAIL_ROWS, self.tm))
            dv, dgr = _rms_bwd(acc_ref[rs, :], xbuf[rs, :], g_ref[...])
            dx = rbuf[rs, :] + dv
            obuf[rs, :] = dx
            obbuf[rs, :] = (self.scale * dx).astype(BF16)
            dgain = dgr if dgain is None else dgain + dgr
        for cp in self._stores(i, dx_hbm, dxb_hbm, bufs):
            cp.start()

        @pl.when(i == 0)
        def _():
            dg_ref[...] = dgain

        @pl.when(i > 0)
        def _():
            dg_ref[...] += dgain

        @pl.when(i == self.ni - 1)
        def _():
            for cp in self._stores(i, dx_hbm, dxb_hbm, bufs):
                cp.wait()


def _ffn_bwd_fused(db, G, U, wg, wu, wd, x_in, gain, dres, scale, name):
    T, D = db.shape
    FF = wg.shape[1]
    tm = _tile(T, 512, 16)
    tf = _tile(FF, 512, LANES)
    nf = FF // tf
    tail = _NormBwdTail(T, D, tm, scale)

    def body(d_ref, G_ref, U_ref, wg_ref, wu_ref, wd_ref, g_ref, x_hbm, r_hbm,
             dG_ref, dU_ref, H_ref, dg_ref, dx_hbm, dxb_hbm, acc_ref, *bufs):
        i, f = pl.program_id(0), pl.program_id(1)

        @pl.when(f == 0)
        def _():
            tail.prefetch(i, x_hbm, r_hbm, bufs)

        dH = _dot_nt(d_ref[...], wd_ref[...])
        Gv = G_ref[...].astype(F32)
        Uv = U_ref[...].astype(F32)
        s = _sigmoid(Gv)
        sg = Gv * s
        H_ref[...] = (sg * Uv).astype(BF16)
        dU = (dH * sg).astype(BF16)
        dG = (dH * Uv * (s * (1.0 + Gv * (1.0 - s)))).astype(BF16)
        dG_ref[...] = dG
        dU_ref[...] = dU
        contrib = _dot_nt(dG, wg_ref[...]) + _dot_nt(dU, wu_ref[...])

        @pl.when(f == 0)
        def _():
            acc_ref[...] = contrib

        @pl.when(f > 0)
        def _():
            acc_ref[...] += contrib

        @pl.when(f == nf - 1)
        def _():
            tail.run(i, acc_ref, g_ref, x_hbm, r_hbm, dx_hbm, dxb_hbm, dg_ref, bufs)

    act = pl.BlockSpec((tm, tf), lambda i, f: (i, f))
    return pl.pallas_call(
        body, name=name, grid=(T // tm, nf),
        in_specs=[pl.BlockSpec((tm, D), lambda i, f: (i, 0)), act, act,
                  pl.BlockSpec((D, tf), lambda i, f: (0, f)),
                  pl.BlockSpec((D, tf), lambda i, f: (0, f)),
                  pl.BlockSpec((tf, D), lambda i, f: (f, 0)),
                  pl.BlockSpec((1, D), lambda i, f: (0, 0)), ANY, ANY],
        out_specs=[act, act, act, pl.BlockSpec((1, D), lambda i, f: (0, 0)), ANY, ANY],
        out_shape=[jax.ShapeDtypeStruct((T, FF), BF16)] * 3 + [jax.ShapeDtypeStruct((1, D), F32)] + tail.out_shape,
        scratch_shapes=[pltpu.VMEM((tm, D), F32)] + tail.scratch,
        compiler_params=_params("arbitrary", "arbitrary"),
    )(db, G, U, wg, wu, wd, gain, x_in, dres)


def _mm_nt_norm_bwd(a, w, x_in, gain, dres, scale, name, carry=None):
    T, K = a.shape
    D = w.shape[0]
    tm = _tile(T, 512, 16)
    tk = _tile(K, 1280, LANES)
    nk = K // tk
    tail = _NormBwdTail(T, D, tm, scale)

    def body(a_ref, w_ref, g_ref, x_hbm, r_hbm, dg_ref, dx_hbm, dxb_hbm, acc_ref, *bufs):
        i, k = pl.program_id(0), pl.program_id(1)

        @pl.when(k == 0)
        def _():
            tail.prefetch(i, x_hbm, r_hbm, bufs)

        contrib = _dot_nt(a_ref[...], w_ref[...])

        @pl.when(k == 0)
        def _():
            acc_ref[...] = contrib

        @pl.when(k > 0)
        def _():
            acc_ref[...] += contrib

        @pl.when(k == nk - 1)
        def _():
            tail.run(i, acc_ref, g_ref, x_hbm, r_hbm, dx_hbm, dxb_hbm, dg_ref, bufs)

    return _call(
        body, name, (T // tm, nk),
        [pl.BlockSpec((tm, tk), lambda i, k: (i, k)), pl.BlockSpec((D, tk), lambda i, k: (0, k)),
         pl.BlockSpec((1, D), lambda i, k: (0, 0)), ANY, ANY],
        [pl.BlockSpec((1, D), lambda i, k: (0, 0)), ANY, ANY],
        [jax.ShapeDtypeStruct((1, D), F32)] + tail.out_shape,
        [pltpu.VMEM((tm, D), F32)] + tail.scratch, ("arbitrary", "arbitrary"), (a, w, gain, x_in, dres), carry)


def _rms_bwd_res(dn, x, gain, dres, scale, name, carry=None):
    T, D = x.shape
    tm = _tile(T, 256, 16)

    def body(dn_ref, x_ref, g_ref, dr_ref, dx_ref, dxb_ref, dg_ref):
        i = pl.program_id(0)
        dv, dgain = _rms_bwd(dn_ref[...], x_ref[...], g_ref[...])
        dx = dr_ref[...] + dv
        dx_ref[...] = dx
        dxb_ref[...] = (scale * dx).astype(BF16)

        @pl.when(i == 0)
        def _():
            dg_ref[...] = dgain

        @pl.when(i > 0)
        def _():
            dg_ref[...] += dgain

    row = pl.BlockSpec((tm, D), lambda i: (i, 0))
    vec = pl.BlockSpec((1, D), lambda i: (0, 0))
    return _call(
        body, name, (T // tm,), [row, row, vec, row], [row, row, vec],
        [jax.ShapeDtypeStruct((T, D), F32), jax.ShapeDtypeStruct((T, D), BF16), jax.ShapeDtypeStruct((1, D), F32)],
        [], ("arbitrary",), (dn, x, gain, dres), carry)


def _loss_head(x3, gain, target, name):
    T, D = x3.shape
    tm = _tile(T, 256, 16)

    def body(x_ref, g_ref, t_ref, dx_ref, dxb_ref, ls_ref, dg_ref):
        i = pl.program_id(0)
        xv = x_ref[...]
        err = xv * _rstd(xv) * g_ref[...] - t_ref[...]
        sq = jnp.sum(jnp.sum(err * err, axis=1, keepdims=True), axis=0, keepdims=True)
        dv, dgain = _rms_bwd(err * (1.0 / D), xv, g_ref[...])
        dx_ref[...] = dv
        dxb_ref[...] = (FFN_RESIDUAL_SCALE * dv).astype(BF16)
        sqb = jnp.broadcast_to(sq, (1, LANES))

        @pl.when(i == 0)
        def _():
            dg_ref[...] = dgain
            ls_ref[...] = sqb

        @pl.when(i > 0)
        def _():
            dg_ref[...] += dgain
            ls_ref[...] += sqb

    row = pl.BlockSpec((tm, D), lambda i: (i, 0))
    vec = pl.BlockSpec((1, D), lambda i: (0, 0))
    return pl.pallas_call(
        body, name=name, grid=(T // tm,),
        in_specs=[row, vec, row],
        out_specs=[row, row, pl.BlockSpec((1, LANES), lambda i: (0, 0)), vec],
        out_shape=[jax.ShapeDtypeStruct((T, D), F32), jax.ShapeDtypeStruct((T, D), BF16),
                   jax.ShapeDtypeStruct((1, LANES), F32), jax.ShapeDtypeStruct((1, D), F32)],
        compiler_params=_params("arbitrary"),
    )(x3, gain, target)


def _norm_mm(x, gain, w, name):
    T, D = x.shape
    N = w.shape[1]
    tm = _tile(T, 512, 16)
    tn = _tile(N, 2560, LANES)

    def body(x_ref, g_ref, w_ref, n_ref, z_ref):
        @pl.when(pl.program_id(1) == 0)
        def _():
            xv = x_ref[...]
            n_ref[...] = (xv * _rstd(xv) * g_ref[...]).astype(BF16)

        z_ref[...] = jnp.dot(n_ref[...], w_ref[...], preferred_element_type=F32).astype(BF16)

    return pl.pallas_call(
        body, name=name, grid=(T // tm, N // tn),
        in_specs=[pl.BlockSpec((tm, D), lambda i, j: (i, 0)),
                  pl.BlockSpec((1, D), lambda i, j: (0, 0)),
                  pl.BlockSpec((D, tn), lambda i, j: (0, j))],
        out_specs=[pl.BlockSpec((tm, D), lambda i, j: (i, 0)),
                   pl.BlockSpec((tm, tn), lambda i, j: (i, j))],
        out_shape=[jax.ShapeDtypeStruct((T, D), BF16), jax.ShapeDtypeStruct((T, N), BF16)],
        compiler_params=_params("parallel", "arbitrary"),
    )(x, gain, w)


def _mm_fullk(a, w, trans_w, residual, out_dtype, name, carry=None):
    T, K = a.shape
    N = w.shape[0] if trans_w else w.shape[1]
    tm = _tile(T, 512, 16)
    tn = _tile(N, 2048 * 2560 // K, LANES)

    def body(*refs):
        if residual is None:
            a_ref, w_ref, o_ref = refs
        else:
            a_ref, w_ref, r_ref, o_ref = refs
        if trans_w:
            acc = _dot_nt(a_ref[...], w_ref[...])
        else:
            acc = jnp.dot(a_ref[...], w_ref[...], preferred_element_type=F32)
        if residual is not None:
            acc = acc + r_ref[...]
        o_ref[...] = acc.astype(out_dtype)

    w_spec = pl.BlockSpec((tn, K), lambda i, j: (j, 0)) if trans_w else pl.BlockSpec((K, tn), lambda i, j: (0, j))
    in_specs = [pl.BlockSpec((tm, K), lambda i, j: (i, 0)), w_spec]
    args = [a, w]
    if residual is not None:
        in_specs.append(pl.BlockSpec((tm, tn), lambda i, j: (i, j)))
        args.append(residual)
    return _call(body, name, (T // tm, N // tn), in_specs, [pl.BlockSpec((tm, tn), lambda i, j: (i, j))],
                 [jax.ShapeDtypeStruct((T, N), out_dtype)], [], ("parallel", "arbitrary"), args, carry)


def _mm_tn(a, b, name, carry=None):
    T, M = a.shape
    N = b.shape[1]
    tmw = _tile(M, 2048, LANES)
    tnw = _tile(N, 2048 * 1408 // tmw, LANES)
    tk = _tile(T, 1024, 16)
    nk = T // tk

    def body(a_ref, b_ref, o_ref, acc_ref):
        k = pl.program_id(2)

        @pl.when(k == 0)
        def _():
            acc_ref[...] = jnp.zeros_like(acc_ref)

        acc_ref[...] += _dot_tn(a_ref[...], b_ref[...])

        @pl.when(k == nk - 1)
        def _():
            o_ref[...] = acc_ref[...].astype(BF16)

    return _call(
        body, name, (M // tmw, N // tnw, nk),
        [pl.BlockSpec((tk, tmw), lambda i, j, k: (k, i)),
         pl.BlockSpec((tk, tnw), lambda i, j, k: (k, j))],
        [pl.BlockSpec((tmw, tnw), lambda i, j, k: (i, j))],
        [jax.ShapeDtypeStruct((M, N), BF16)],
        [pltpu.VMEM((tmw, tnw), F32)], ("parallel", "parallel", "arbitrary"), (a, b), carry)


def _mm_tn_pair(a, b, c_arr, axis, name, carry=None):
    T, M = a.shape
    N = b.shape[1]
    tk = _tile(T, 1024, 16)
    nk = T // tk
    nq = 4
    if axis == 1:
        rows, cols = M // 2, N // nq
        a_spec = pl.BlockSpec((tk, rows), lambda p, j, k, cr: (k, jnp.where(p == 0, 1 - cr[0], cr[0])))
        b_spec = pl.BlockSpec((tk, cols), lambda p, j, k, cr: (k, j))
        s_shape, land_shape = (1, rows, N), (rows, N)
        s_spec = pl.BlockSpec((None, rows, cols), lambda p, j, k, cr: (0, 0, j * p))
    else:
        rows, cols = M // nq, N // 2
        a_spec = pl.BlockSpec((tk, rows), lambda p, j, k, cr: (k, j))
        b_spec = pl.BlockSpec((tk, cols), lambda p, j, k, cr: (k, jnp.where(p == 0, 1 - cr[0], cr[0])))
        s_shape, land_shape = (nq, rows, cols), (nq, rows, cols)
        s_spec = pl.BlockSpec((None, rows, cols), lambda p, j, k, cr: (j * p, 0, 0))

    def body(c_ref, a_ref, b_ref, s_ref, land, acc_ref, stage, got, send_sems, recv_sems, loc_sem):
        p, j, k = pl.program_id(0), pl.program_id(1), pl.program_id(2)
        x, y, c = _me()

        def tile(jj):
            return land.at[:, pl.ds(jj * cols, cols)] if axis == 1 else land.at[jj]

        def send(jj):
            return _remote(stage, tile(jj), send_sems, recv_sems, jj, (x, y, 1 - c))

        @pl.when(k == 0)
        def _():
            acc_ref[...] = jnp.zeros_like(acc_ref)

        acc_ref[...] += _dot_tn(a_ref[...], b_ref[...])

        for jj in range(nq):
            @pl.when(jnp.logical_and(k == nk - 1, jnp.logical_and(p == 0, j == jj)))
            def _():
                if jj > 0:
                    send(jj - 1).wait_send()
                stage[...] = acc_ref[...].astype(BF16)
                send(jj).start()

            @pl.when(jnp.logical_and(k == nk - 1, jnp.logical_and(p == 1, j == jj)))
            def _():
                if jj == 0:
                    send(nq - 1).wait_send()
                send(jj).wait_recv()
                fetch = pltpu.make_async_copy(tile(jj), got, loc_sem.at[0])
                fetch.start()
                fetch.wait()
                s_ref[...] = (acc_ref[...] + got[...].astype(F32)).astype(BF16)

    return _call(
        body, name, (2, nq, nk), [a_spec, b_spec], [s_spec, ANY],
        [jax.ShapeDtypeStruct(s_shape, BF16), jax.ShapeDtypeStruct(land_shape, BF16)],
        [pltpu.VMEM((rows, cols), F32), pltpu.VMEM((rows, cols), BF16), pltpu.VMEM((rows, cols), BF16),
         pltpu.SemaphoreType.DMA((nq,)), pltpu.SemaphoreType.DMA((nq,)), pltpu.SemaphoreType.DMA((1,))],
        ("arbitrary", "arbitrary", "arbitrary"), (a, b), carry, (c_arr,))


GELU_K = math.sqrt(2.0 / math.pi)
GELU_C = 0.044715


def _gelu_and_grad(v):
    u = GELU_K * (v + GELU_C * v * v * v)
    th = jnp.tanh(u)
    g = 0.5 * v * (1.0 + th)
    dg = 0.5 * (1.0 + th) + 0.5 * v * (1.0 - th * th) * GELU_K * (1.0 + 3.0 * GELU_C * v * v)
    return g, dg


def _neg_expm1(v):
    poly = v * (1.0 + v * (0.5 + v * (1.0 / 6 + v * (1.0 / 24 + v * (1.0 / 120 + v * (1.0 / 720))))))
    return jnp.where(v > -0.25, -poly, 1.0 - jnp.exp(v))


def _softplus_neg(lam):
    e = jnp.exp(-jnp.abs(lam))
    log1pe = jnp.where(e < 1e-4, e * (1.0 - 0.5 * e), jnp.log(1.0 + e))
    sp = jnp.maximum(-lam, 0.0) + log1pe
    dsp = -1.0 / (1.0 + jnp.exp(lam))
    return sp, dsp


def _earlier(ext, j):
    return pltpu.roll(ext, j, 0)[SUBLANES:, :]


def _later(ext, j):
    n = ext.shape[0]
    return pltpu.roll(ext, n - j, 0)[:n - SUBLANES, :]


def _taps(v, halo, K):
    ext = jnp.concatenate([halo, v], axis=0)
    return [v] + [_earlier(ext, j) for j in range(1, K)]


def _block_diag(vb, w_ref, nh, hd):
    return jnp.concatenate(
        [jnp.dot(vb[:, h * hd:(h + 1) * hd], w_ref[h], preferred_element_type=F32) for h in range(nh)], axis=1)


def _lru_gates(xc, wa_ref, ba_ref, wi_ref, bi_ref, sp, nh, hd):
    xcb = xc.astype(BF16)
    r = _sigmoid(_block_diag(xcb, wa_ref, nh, hd) + ba_ref[...])
    ig = _sigmoid(_block_diag(xcb, wi_ref, nh, hd) + bi_ref[...])
    log_a = -LRU_C * r * sp
    a = jnp.exp(log_a)
    mult = jnp.sqrt(_neg_expm1(2.0 * log_a))
    return xcb, r, ig, a, mult


def _mix_fwd(z, cw, cb, wa, ba, wi, bi, lam, sw, glo, gso, name):
    T = z.shape[0]
    DL = cb.shape[1]
    DS = gso.shape[1]
    NH, HD = wa.shape[0], wa.shape[1]
    KL, KS = cw.shape[0], sw.shape[0]
    tt = _tile(T, 128, 16)
    o_g, o_b, o_c, o_x = DL, 2 * DL, 2 * DL + DS, 2 * DL + 2 * DS

    def body(z_ref, cw_ref, cb_ref, wa_ref, ba_ref, wi_ref, bi_ref, lam_ref, sw_ref, glo_ref, gso_ref,
             h_ref, y_ref, cx_ref, cp_ref, ch_ref):
        @pl.when(pl.program_id(0) == 0)
        def _():
            cx_ref[...] = jnp.zeros_like(cx_ref)
            cp_ref[...] = jnp.zeros_like(cp_ref)
            ch_ref[...] = jnp.zeros_like(ch_ref)

        def zcol(o, n):
            return z_ref[:, o:o + n].astype(F32)

        lx = zcol(0, DL)
        xs = _taps(lx, cx_ref[...], KL)
        cx_ref[...] = lx[tt - SUBLANES:, :]
        xc = cb_ref[...] + xs[0] * cw_ref[KL - 1:KL, :]
        for j in range(1, KL):
            xc = xc + xs[j] * cw_ref[KL - 1 - j:KL - j, :]
        sp, _ = _softplus_neg(lam_ref[...])
        _, _, ig, a, mult = _lru_gates(xc, wa_ref, ba_ref, wi_ref, bi_ref, sp, NH, HD)
        b = mult * (ig * xc)
        rows = lax.broadcasted_iota(jnp.int32, (tt, DL), 0)
        s = 1
        while s < tt:
            keep = rows >= s
            b = jnp.where(keep, a * pltpu.roll(b, s, 0) + b, b)
            a = jnp.where(keep, a * pltpu.roll(a, s, 0), a)
            s *= 2
        h = a * ch_ref[SUBLANES - 1:SUBLANES, :] + b
        ch_ref[...] = h[tt - SUBLANES:, :]
        h_ref[...] = h
        ge, _ = _gelu_and_grad(zcol(o_g, DL))
        ylru = h * ge
        y_ref[:, 0:DL] = (ylru * _rstd(ylru) * glo_ref[...]).astype(BF16)

        p = zcol(o_c, DS) * zcol(o_x, DS)
        ps = _taps(p, cp_ref[...], KS)
        cp_ref[...] = p[tt - SUBLANES:, :]
        cv = ps[0] * sw_ref[KS - 1:KS, :]
        for j in range(1, KS):
            cv = cv + ps[j] * sw_ref[KS - 1 - j:KS - j, :]
        ysc = zcol(o_b, DS) * cv
        y_ref[:, DL:DL + DS] = (ysc * _rstd(ysc) * gso_ref[...]).astype(BF16)

    def full(shape):
        return pl.BlockSpec(shape, lambda t: (0,) * len(shape))

    return pl.pallas_call(
        body, name=name, grid=(T // tt,),
        in_specs=[pl.BlockSpec((tt, z.shape[1]), lambda t: (t, 0)),
                  full(cw.shape), full(cb.shape), full(wa.shape), full(ba.shape), full(wi.shape), full(bi.shape),
                  full(lam.shape), full(sw.shape), full(glo.shape), full(gso.shape)],
        out_specs=[pl.BlockSpec((tt, DL), lambda t: (t, 0)), pl.BlockSpec((tt, DL + DS), lambda t: (t, 0))],
        out_shape=[jax.ShapeDtypeStruct((T, DL), F32), jax.ShapeDtypeStruct((T, DL + DS), BF16)],
        scratch_shapes=[pltpu.VMEM((SUBLANES, DL), F32), pltpu.VMEM((SUBLANES, DS), F32),
                        pltpu.VMEM((SUBLANES, DL), F32)],
        compiler_params=_params("arbitrary"),
    )(z, cw, cb, wa, ba, wi, bi, lam, sw, glo, gso)


V_BA, V_BI, V_LAM, V_CB, V_CW, V_SW, V_GLO, V_GSO, V_ROWS = 0, 1, 2, 3, 4, 8, 11, 12, 16


def _mix_bwd(z, h, dy, cw, cb, wa, ba, wi, bi, lam, sw, glo, gso, name):
    T = z.shape[0]
    DL = cb.shape[1]
    DS = gso.shape[1]
    NH, HD = wa.shape[0], wa.shape[1]
    KL, KS = cw.shape[0], sw.shape[0]
    tt = _tile(T, 64, 16)
    nt = T // tt
    ZH = 2 * SUBLANES
    o_g, o_b, o_c, o_x = DL, 2 * DL, 2 * DL + DS, 2 * DL + 2 * DS

    def body(z_ref, zh_ref, h_ref, hh_ref, dy_ref, cw_ref, cb_ref, wa_ref, ba_ref, wi_ref, bi_ref, lam_ref,
             sw_ref, glo_ref, gso_ref, dz_ref, dwa_ref, dwi_ref, vec_ref, cdx_ref, cdc_ref, cdh_ref):
        i = pl.program_id(0)
        tr = nt - 1 - i

        @pl.when(i == 0)
        def _():
            dwa_ref[...] = jnp.zeros_like(dwa_ref)
            dwi_ref[...] = jnp.zeros_like(dwi_ref)
            vec_ref[...] = jnp.zeros_like(vec_ref)
            cdx_ref[...] = jnp.zeros_like(cdx_ref)
            cdc_ref[...] = jnp.zeros_like(cdc_ref)
            cdh_ref[...] = jnp.zeros_like(cdh_ref)

        def acc_row(r, v):
            vec_ref[pl.ds(r, 1), :] += jnp.sum(v, axis=0, keepdims=True)

        has_prev = tr > 0
        rows = lax.broadcasted_iota(jnp.int32, (tt, DL), 0)

        def zcol(o, n):
            return z_ref[:, o:o + n].astype(F32)

        def zhalo(o, n):
            return jnp.where(has_prev, zh_ref[:, o:o + n].astype(F32)[SUBLANES:, :], 0.0)

        lx = zcol(0, DL)
        xs = _taps(lx, zhalo(0, DL), KL)
        xc = cb_ref[...] + xs[0] * cw_ref[KL - 1:KL, :]
        for j in range(1, KL):
            xc = xc + xs[j] * cw_ref[KL - 1 - j:KL - j, :]
        sp, dsp = _softplus_neg(lam_ref[...])
        xcb, r, ig, a, mult = _lru_gates(xc, wa_ref, ba_ref, wi_ref, bi_ref, sp, NH, HD)
        hv = h_ref[...]
        hprev = _earlier(jnp.concatenate([jnp.where(has_prev, hh_ref[...], 0.0), hv], axis=0), 1)
        gate = zcol(o_g, DL)
        ge, dge = _gelu_and_grad(gate)
        ylru = hv * ge

        d_ylru, dglo = _rms_bwd(dy_ref[:, 0:DL].astype(F32), ylru, glo_ref[...])
        vec_ref[pl.ds(V_GLO, 1), :] += dglo
        dz_ref[:, o_g:o_g + DL] = (d_ylru * hv * dge).astype(BF16)
        bq = d_ylru * ge
        aq = jnp.where(rows == tt - 1, 1.0, pltpu.roll(a, tt - 1, 0))
        s = 1
        while s < tt:
            keep = rows < tt - s
            bq = jnp.where(keep, aq * pltpu.roll(bq, tt - s, 0) + bq, bq)
            aq = jnp.where(keep, aq * pltpu.roll(aq, tt - s, 0), aq)
            s *= 2
        dhh = bq + aq * cdh_ref[0:1, :]
        cdh_ref[0:1, :] = a[0:1, :] * dhh[0:1, :]

        da = dhh * hprev
        dmult = dhh * (ig * xc)
        d_i = dhh * mult * xc
        dxc = dhh * mult * ig
        dlog = da * a - dmult * (a * a) / mult
        acc_row(V_LAM, dlog * (-LRU_C * r) * dsp)
        dpa = dlog * (-LRU_C * sp) * r * (1.0 - r)
        dpi = d_i * ig * (1.0 - ig)
        acc_row(V_BA, dpa)
        acc_row(V_BI, dpi)
        dpab = dpa.astype(BF16)
        dpib = dpi.astype(BF16)
        back = []
        for hh in range(NH):
            sl = slice(hh * HD, (hh + 1) * HD)
            dwa_ref[hh] += _dot_tn(xcb[:, sl], dpab[:, sl])
            dwi_ref[hh] += _dot_tn(xcb[:, sl], dpib[:, sl])
            back.append(_dot_nt(dpab[:, sl], wa_ref[hh]) + _dot_nt(dpib[:, sl], wi_ref[hh]))
        dxc = dxc + jnp.concatenate(back, axis=1)

        acc_row(V_CB, dxc)
        extd = jnp.concatenate([dxc, cdx_ref[...]], axis=0)
        cdx_ref[...] = dxc[0:SUBLANES, :]
        dlx = dxc * cw_ref[KL - 1:KL, :]
        acc_row(V_CW + KL - 1, dxc * xs[0])
        for j in range(1, KL):
            dlx = dlx + _later(extd, j) * cw_ref[KL - 1 - j:KL - j, :]
            acc_row(V_CW + KL - 1 - j, dxc * xs[j])
        dz_ref[:, 0:DL] = dlx.astype(BF16)

        sb = zcol(o_b, DS)
        sc = zcol(o_c, DS)
        sx = zcol(o_x, DS)
        p = sc * sx
        ps = _taps(p, zhalo(o_c, DS) * zhalo(o_x, DS), KS)
        cv = ps[0] * sw_ref[KS - 1:KS, :]
        for j in range(1, KS):
            cv = cv + ps[j] * sw_ref[KS - 1 - j:KS - j, :]
        d_ysc, dgso = _rms_bwd(dy_ref[:, DL:DL + DS].astype(F32), sb * cv, gso_ref[...])
        vec_ref[pl.ds(V_GSO, 1), :] += dgso
        dz_ref[:, o_b:o_b + DS] = (d_ysc * cv).astype(BF16)
        dcv = d_ysc * sb
        extc = jnp.concatenate([dcv, cdc_ref[...]], axis=0)
        cdc_ref[...] = dcv[0:SUBLANES, :]
        dp = dcv * sw_ref[KS - 1:KS, :]
        acc_row(V_SW + KS - 1, dcv * ps[0])
        for j in range(1, KS):
            dp = dp + _later(extc, j) * sw_ref[KS - 1 - j:KS - j, :]
            acc_row(V_SW + KS - 1 - j, dcv * ps[j])
        dz_ref[:, o_c:o_c + DS] = (dp * sx).astype(BF16)
        dz_ref[:, o_x:o_x + DS] = (dp * sc).astype(BF16)

    def full(shape):
        return pl.BlockSpec(shape, lambda t: (0,) * len(shape))

    def rev(t):
        return nt - 1 - t

    def halo(t, rows):
        return jnp.maximum(rev(t) * (tt // rows) - 1, 0)

    return pl.pallas_call(
        body, name=name, grid=(nt,),
        in_specs=[pl.BlockSpec((tt, z.shape[1]), lambda t: (rev(t), 0)),
                  pl.BlockSpec((ZH, z.shape[1]), lambda t: (halo(t, ZH), 0)),
                  pl.BlockSpec((tt, DL), lambda t: (rev(t), 0)),
                  pl.BlockSpec((SUBLANES, DL), lambda t: (halo(t, SUBLANES), 0)),
                  pl.BlockSpec((tt, DL + DS), lambda t: (rev(t), 0)),
                  full(cw.shape), full(cb.shape), full(wa.shape), full(ba.shape), full(wi.shape), full(bi.shape),
                  full(lam.shape), full(sw.shape), full(glo.shape), full(gso.shape)],
        out_specs=[pl.BlockSpec((tt, z.shape[1]), lambda t: (rev(t), 0)),
                   full(wa.shape), full(wi.shape), full((V_ROWS, DL))],
        out_shape=[jax.ShapeDtypeStruct(z.shape, BF16), jax.ShapeDtypeStruct(wa.shape, F32),
                   jax.ShapeDtypeStruct(wi.shape, F32), jax.ShapeDtypeStruct((V_ROWS, DL), F32)],
        scratch_shapes=[pltpu.VMEM((SUBLANES, DL), F32), pltpu.VMEM((SUBLANES, DS), F32),
                        pltpu.VMEM((SUBLANES, DL), F32)],
        compiler_params=_params("arbitrary"),
    )(z, z, h, h, dy, cw, cb, wa, ba, wi, bi, lam, sw, glo, gso)


def _pair_add(p, r1, c, name):
    G, R, C = r1.shape
    tr = _tile(R, 256, 16)
    tc = _tile(C, 1408, LANES)

    def body(c_ref, p_ref, r_ref, o_ref):
        o_ref[...] = (p_ref[...].astype(F32) + r_ref[...].astype(F32)).astype(BF16)

    blk = (None, tr, tc)
    return pl.pallas_call(
        body, name=name,
        grid_spec=pltpu.PrefetchScalarGridSpec(
            num_scalar_prefetch=1, grid=(G, R // tr, C // tc),
            in_specs=[pl.BlockSpec(blk, lambda g, i, j, cr: (2 * g + cr[0], i, j)),
                      pl.BlockSpec(blk, lambda g, i, j, cr: (g, i, j))],
            out_specs=pl.BlockSpec(blk, lambda g, i, j, cr: (g, i, j))),
        out_shape=jax.ShapeDtypeStruct((G, R, C), BF16),
        compiler_params=_params("parallel", "parallel", "parallel"),
    )(c, p, r1)


def _quad_add(s, r2, qc, axis, name):
    _, R, W = r2.shape
    tr = _tile(R, 256, 16)

    def body(qc_ref, s_ref, r0_ref, r1_ref, r2_ref, o_ref):
        o_ref[...] = ((s_ref[...].astype(F32) + r0_ref[...].astype(F32)) + r1_ref[...].astype(F32)) + r2_ref[...].astype(F32)

    blk = (None, tr, W)
    if axis == 1:
        own = pl.BlockSpec(blk, lambda i, qr: (0, i, qr[0]))
    else:
        own = pl.BlockSpec(blk, lambda i, qr: (qr[0], i, 0))
    return pl.pallas_call(
        body, name=name,
        grid_spec=pltpu.PrefetchScalarGridSpec(
            num_scalar_prefetch=1, grid=(R // tr,),
            in_specs=[own] + [pl.BlockSpec(blk, lambda i, qr, j=j: (j, i, 0)) for j in range(3)],
            out_specs=pl.BlockSpec(blk, lambda i, qr: (qr[1], i, 0))),
        out_shape=jax.ShapeDtypeStruct((2, R, W), F32),
        compiler_params=_params("parallel"),
    )(qc, s, r2, r2, r2)


CAST_BLOCKS = 4


def _cast_into_full(shards, qc, axes, name, carry=None):
    M = len(shards)
    nb = CAST_BLOCKS

    def body(qc_ref, *refs):
        for s_ref, o_ref in zip(refs[:M], refs[M:]):
            o_ref[...] = s_ref[...].astype(BF16)

    in_specs, out_specs, out_shape = [], [], []
    for s, ax in zip(shards, axes):
        R, W = s.shape
        Rh = R // 2
        tr = Rh // nb
        in_specs.append(pl.BlockSpec((tr, W), lambda hf, i, qr: (hf * nb + i, 0)))
        if ax == 1:
            out_shape.append(jax.ShapeDtypeStruct((2, Rh, 4 * W), BF16))
            out_specs.append(pl.BlockSpec((None, tr, W), lambda hf, i, qr: (hf, i, qr[0])))
        else:
            out_shape.append(jax.ShapeDtypeStruct((8, Rh, W), BF16))
            out_specs.append(pl.BlockSpec((None, tr, W), lambda hf, i, qr: (2 * qr[0] + hf, i, 0)))
    return _call(body, name, (2, nb), in_specs, out_specs, out_shape, [], ("parallel", "parallel"), shards, carry, (qc,))


def _adamw(w, g, m, v, name):
    R, C = w.shape
    tr = _tile(R, 256, SUBLANES)
    tc = C // 2 if g.ndim == 3 else _tile(C, 2048, LANES)
    c1 = 1.0 - ADAM_B1 ** ADAM_STEP
    c2 = 1.0 - ADAM_B2 ** ADAM_STEP

    def body(w_ref, g_ref, m_ref, v_ref, d_ref, mo_ref, vo_ref, go_ref):
        gv = g_ref[...]
        go_ref[...] = gv
        mn = ADAM_B1 * m_ref[...] + (1.0 - ADAM_B1) * gv
        vn = ADAM_B2 * v_ref[...] + (1.0 - ADAM_B2) * (gv * gv)
        mo_ref[...] = mn
        vo_ref[...] = vn
        d_ref[...] = -ADAM_LR * ((mn / c1) / (jnp.sqrt(vn / c2) + ADAM_EPS) + ADAM_WD * w_ref[...])

    blk = pl.BlockSpec((tr, tc), lambda i, j: (i, j))
    g_blk = pl.BlockSpec((None, tr, tc), lambda i, j: (j, i, 0)) if g.ndim == 3 else blk
    sh = jax.ShapeDtypeStruct((R, C), F32)
    return pl.pallas_call(
        body, name=name, grid=(R // tr, C // tc),
        in_specs=[blk, g_blk, blk, blk], out_specs=[blk] * 4, out_shape=[sh] * 4,
        compiler_params=_params("parallel", "parallel"),
    )(w, g, m, v)


def _other_chips(x, y):
    return [(1 - x, y), (x, 1 - y), (1 - x, 1 - y)]


def _remote(src, dst, send_sems, recv_sems, idx, dev):
    return pltpu.make_async_remote_copy(src_ref=src, dst_ref=dst, send_sem=send_sems.at[idx], recv_sem=recv_sems.at[idx],
                                        device_id=dev, device_id_type=MESH)


def _gather_carry(fulls, axes):
    M = len(fulls)

    def win(outs, m, qq, cc):
        if axes[m] == 1:
            W = fulls[m].shape[2] // 4
            return outs[m].at[cc, :, pl.ds(pl.multiple_of(qq * W, LANES), W)]
        return outs[m].at[2 * qq + cc]

    def ici(outs, sems, m, j, src_q):
        x, y, c = _me()
        cx, cy = _other_chips(x, y)[j]
        blk = win(outs, m, src_q, c)
        return _remote(blk, blk, sems[0], sems[1], 6 * m + j, (cx, cy, c))

    def d2d(outs, sems, m, j, half):
        x, y, c = _me()
        cx, cy = _other_chips(x, y)[j]
        blk = win(outs, m, 2 * cx + cy, half)
        return _remote(blk, blk, sems[0], sems[1], 6 * m + 3 + j, (x, y, 1 - c))

    def start(ins, outs, sems):
        x, y, c = _me()
        for m in range(M):
            for j in range(3):
                ici(outs, sems, m, j, 2 * x + y).start()

    def middle(ins, outs, sems):
        x, y, c = _me()
        for m in range(M):
            for j, (cx, cy) in enumerate(_other_chips(x, y)):
                ici(outs, sems, m, j, 2 * cx + cy).wait_recv()
                d2d(outs, sems, m, j, c).start()

    def finish(ins, outs, sems):
        x, y, c = _me()
        for m in range(M):
            for j in range(3):
                d2d(outs, sems, m, j, 1 - c).wait_recv()
        for m in range(M):
            for j in range(3):
                ici(outs, sems, m, j, 2 * x + y).wait_send()
                d2d(outs, sems, m, j, c).wait_send()

    return _Carry(fulls, [jax.ShapeDtypeStruct(f.shape, f.dtype) for f in fulls], {m: m for m in range(M)},
                  [pltpu.SemaphoreType.DMA((6 * M,)), pltpu.SemaphoreType.DMA((6 * M,))], start, finish, middle)


def _gather_two_way_carry(fulls, axes):
    M = len(fulls)

    def part(outs, m, qq, cc, p):
        Rp = fulls[m].shape[1] // 2
        rows = pl.ds(p * Rp, Rp)
        if axes[m] == 1:
            W = fulls[m].shape[2] // 4
            return outs[m].at[cc, rows, pl.ds(pl.multiple_of(qq * W, LANES), W)]
        return outs[m].at[2 * qq + cc, rows, :]

    def copies(outs, sems):
        x, y, c = _me()
        q, qx, qy, qd = 2 * x + y, 2 * (1 - x) + y, 2 * x + (1 - y), 2 * (1 - x) + (1 - y)
        xn, yn, sib = (1 - x, y, c), (x, 1 - y, c), (x, y, 1 - c)
        table = {}
        for m in range(M):
            def cp(blk, k, dev):
                return _remote(blk, blk, sems[0], sems[1], 12 * m + k, dev)
            own0, own1 = part(outs, m, q, c, 0), part(outs, m, q, c, 1)
            table[m] = dict(
                to=[cp(own0, 0, xn), cp(own1, 1, yn), cp(own1, 2, xn), cp(own0, 3, yn)],
                landed=[cp(part(outs, m, qx, c, 0), 0, xn), cp(part(outs, m, qy, c, 1), 1, yn),
                        cp(part(outs, m, qx, c, 1), 2, xn), cp(part(outs, m, qy, c, 0), 3, yn),
                        cp(part(outs, m, qd, c, 0), 4, yn), cp(part(outs, m, qd, c, 1), 5, xn)],
                passed=[cp(part(outs, m, qx, c, 0), 4, yn), cp(part(outs, m, qy, c, 1), 5, xn)],
                handed=[cp(part(outs, m, qq, c, p), 6 + k, sib)
                        for k, (qq, p) in enumerate([(qx, 0), (qy, 1), (qx, 1), (qy, 0), (qd, 0), (qd, 1)])],
                taken=[cp(part(outs, m, qq, 1 - c, p), 6 + k, sib)
                       for k, (qq, p) in enumerate([(qx, 0), (qy, 1), (qx, 1), (qy, 0), (qd, 0), (qd, 1)])])
        return table

    def start(ins, outs, sems):
        t = copies(outs, sems)
        for m in range(M):
            for cp in t[m]['to']:
                cp.start()

    def finish(ins, outs, sems):
        t = copies(outs, sems)
        for m in range(M):
            for k in range(4):
                t[m]['landed'][k].wait_recv()
                if k < 2:
                    t[m]['passed'][k].start()
                t[m]['handed'][k].start()
        for m in range(M):
            for k in (4, 5):
                t[m]['landed'][k].wait_recv()
                t[m]['handed'][k].start()
        for m in range(M):
            for cp in t[m]['taken']:
                cp.wait_recv()
            for cp in t[m]['to'] + t[m]['passed'] + t[m]['handed']:
                cp.wait_send()

    return _Carry(fulls, [jax.ShapeDtypeStruct(f.shape, f.dtype) for f in fulls], {m: m for m in range(M)},
                  [pltpu.SemaphoreType.DMA((12 * M,)), pltpu.SemaphoreType.DMA((12 * M,))], start, finish)


def _pair_exchange_carry(parts):
    M = len(parts)
    groups = [p.shape[0] // 2 for p in parts]
    base = [sum(groups[:m]) for m in range(M)]
    out_shape = [jax.ShapeDtypeStruct((g,) + p.shape[1:], p.dtype) for g, p in zip(groups, parts)]

    def copies(ins, outs, sems):
        x, y, c = _me()
        return [_remote(ins[m].at[2 * g + 1 - c], outs[m].at[g], sems[0], sems[1], base[m] + g, (x, y, 1 - c))
                for m in range(M) for g in range(groups[m])]

    def start(ins, outs, sems):
        for cp in copies(ins, outs, sems):
            cp.start()

    def finish(ins, outs, sems):
        for cp in copies(ins, outs, sems):
            cp.wait()

    n = sum(groups)
    return _Carry(parts, out_shape, {}, [pltpu.SemaphoreType.DMA((n,)), pltpu.SemaphoreType.DMA((n,))], start, finish)


def _chip_exchange_carry(sums, axes):
    M = len(sums)
    out_shape = []
    for s, ax in zip(sums, axes):
        _, Rh, C = s.shape
        out_shape.append(jax.ShapeDtypeStruct((3, Rh, C // 4 if ax == 1 else C), s.dtype))

    def copies(ins, outs, sems):
        x, y, c = _me()
        cps = []
        for m in range(M):
            for j, (cx, cy) in enumerate(_other_chips(x, y)):
                qj = 2 * cx + cy
                if axes[m] == 1:
                    W = sums[m].shape[2] // 4
                    src = ins[m].at[0, :, pl.ds(pl.multiple_of(qj * W, LANES), W)]
                else:
                    src = ins[m].at[qj]
                cps.append(_remote(src, outs[m].at[j], sems[0], sems[1], 3 * m + j, (cx, cy, c)))
        return cps

    def start(ins, outs, sems):
        for cp in copies(ins, outs, sems):
            cp.start()

    def finish(ins, outs, sems):
        for cp in copies(ins, outs, sems):
            cp.wait()

    return _Carry(sums, out_shape, {}, [pltpu.SemaphoreType.DMA((3 * M,)), pltpu.SemaphoreType.DMA((3 * M,))],
                  start, finish)


def _pair_share_carry(bufs):
    M = len(bufs)

    def start(ins, outs, sems):
        x, y, c = _me()
        for m in range(M):
            _remote(outs[m].at[c], outs[m].at[c], sems[0], sems[1], m, (x, y, 1 - c)).start()

    def finish(ins, outs, sems):
        x, y, c = _me()
        for m in range(M):
            _remote(outs[m].at[c], outs[m].at[c], sems[0], sems[1], m, (x, y, 1 - c)).wait_send()
            _remote(outs[m].at[1 - c], outs[m].at[1 - c], sems[0], sems[1], m, (x, y, 1 - c)).wait_recv()

    return _Carry(bufs, [jax.ShapeDtypeStruct(b.shape, b.dtype) for b in bufs], {m: m for m in range(M)},
                  [pltpu.SemaphoreType.DMA((M,)), pltpu.SemaphoreType.DMA((M,))], start, finish)


def _allreduce_small(v, name):
    R, W = v.shape
    Rh = R // 2

    def body(v_ref, o_ref, sib, quad, send_sems, recv_sems):
        x, y, c = _me()
        q = 2 * x + y
        sibling = (x, y, 1 - c)
        pair = _remote(v_ref, sib, send_sems, recv_sems, 0, sibling)
        pair.start()
        pair.wait()
        mine = pl.ds(pl.multiple_of(c * Rh, SUBLANES), Rh)
        quad[0] = v_ref[mine, :] + sib[mine, :]
        cps = []
        for k in (1, 2, 3):
            peer = (1 - x if k & 2 else x, 1 - y if k & 1 else y, c)
            cps.append(_remote(quad.at[0], quad.at[k], send_sems, recv_sems, k, peer))
            cps[-1].start()
        for cp in cps:
            cp.wait()
        acc = quad[q]
        for p in (1, 2, 3):
            acc = acc + quad[jnp.bitwise_xor(q, p)]
        o_ref[mine, :] = acc
        theirs = pl.ds(pl.multiple_of((1 - c) * Rh, SUBLANES), Rh)
        done = _remote(o_ref.at[mine, :], o_ref.at[mine, :], send_sems, recv_sems, 4, sibling)
        done.start()
        done.wait_send()
        _remote(o_ref.at[theirs, :], o_ref.at[theirs, :], send_sems, recv_sems, 4, sibling).wait_recv()

    vm = pl.BlockSpec(memory_space=pltpu.VMEM)
    return pl.pallas_call(
        body, name=name, in_specs=[vm], out_specs=vm, out_shape=jax.ShapeDtypeStruct((R, W), F32),
        scratch_shapes=[pltpu.VMEM((R, W), F32), pltpu.VMEM((4, Rh, W), F32),
                        pltpu.SemaphoreType.DMA((5,)), pltpu.SemaphoreType.DMA((5,))],
        compiler_params=pltpu.CompilerParams(vmem_limit_bytes=VMEM_LIMIT),
    )(v)


def _pack(pieces):
    flat = []
    for p in pieces:
        p = p.reshape(-1).astype(F32)
        pad = (-p.shape[0]) % PACK_ALIGN
        flat.append(jnp.pad(p, (0, pad)).reshape(-1, PACK_W))
    if sum(f.shape[0] for f in flat) % (2 * SUBLANES):
        flat.append(jnp.zeros((SUBLANES, PACK_W), F32))
    return jnp.concatenate(flat, axis=0)


def _unpack(packed, shapes):
    out, row = [], 0
    for shp in shapes:
        n = math.prod(shp)
        rows = -(-n // PACK_ALIGN) * SUBLANES
        out.append(packed[row:row + rows].reshape(-1)[:n].reshape(shp))
        row += rows
    return out


def kernel(x, ffn1_norm, ffn1_w_gate, ffn1_w_up, ffn1_w_down, mix_norm, w_in, lru_conv_w, lru_conv_b, lru_w_a, lru_b_a, lru_w_i, lru_b_i, lru_lambda, sc_conv_w, lru_out_norm, sc_out_norm, w_out, ffn2_norm, ffn2_w_gate, ffn2_w_up, ffn2_w_down, final_norm, loss_target, m_ffn1_norm, m_ffn1_w_gate, m_ffn1_w_up, m_ffn1_w_down, m_mix_norm, m_w_in, m_lru_conv_w, m_lru_conv_b, m_lru_w_a, m_lru_b_a, m_lru_w_i, m_lru_b_i, m_lru_lambda, m_sc_conv_w, m_lru_out_norm, m_sc_out_norm, m_w_out, m_ffn2_norm, m_ffn2_w_gate, m_ffn2_w_up, m_ffn2_w_down, m_final_norm, v_ffn1_norm, v_ffn1_w_gate, v_ffn1_w_up, v_ffn1_w_down, v_mix_norm, v_w_in, v_lru_conv_w, v_lru_conv_b, v_lru_w_a, v_lru_b_a, v_lru_w_i, v_lru_b_i, v_lru_lambda, v_sc_conv_w, v_lru_out_norm, v_sc_out_norm, v_w_out, v_ffn2_norm, v_ffn2_w_gate, v_ffn2_w_up, v_ffn2_w_down, v_final_norm):
    vals = locals()
    w = {n: vals[n] for n in WEIGHTS}
    mom = {n: vals["m_" + n] for n in WEIGHTS}
    var = {n: vals["v_" + n] for n in WEIGHTS}

    xi, yi, ci = _me()
    qi = 2 * xi + yi
    c_arr = jnp.reshape(ci, (1,)).astype(jnp.int32)
    qc_arr = jnp.stack([qi, ci]).astype(jnp.int32)

    T, D = x.shape[1], x.shape[2]
    xt = x.reshape(T, D)
    target = loss_target.reshape(T, D)
    DL = lru_conv_b.shape[-1]
    NH, HD = lru_w_a.shape[1], lru_w_a.shape[2]
    KL, KS = lru_conv_w.shape[1], sc_conv_w.shape[1]
    DLq = lru_conv_w.shape[2]

    axis_of = dict(zip(BIG, BIG_AXIS))
    first_names = ['ffn1_w_gate', 'ffn1_w_up', 'ffn1_w_down']
    later_names = ['w_in', 'w_out', 'ffn2_w_gate', 'ffn2_w_up', 'ffn2_w_down']

    def unview(n, g):
        return g.reshape(2 * g.shape[1], g.shape[2]) if axis_of[n] == 1 else g.reshape(8 * g.shape[1], g.shape[2])

    first_axes, later_axes = [axis_of[n] for n in first_names], [axis_of[n] for n in later_names]
    placed = _cast_into_full([w[n][0] for n in first_names], qc_arr, first_axes, "cast_ffn1_weights")
    res = _cast_into_full([w[n][0] for n in later_names], qc_arr, later_axes, "cast_later_weights",
                          _gather_two_way_carry(placed, first_axes))
    full = {n: unview(n, g) for n, g in zip(first_names, res[len(later_names):])}
    gather_later = _gather_carry(res[:len(later_names)], later_axes)

    taps = jnp.zeros((2 * SUBLANES, DL), F32)
    taps = lax.dynamic_update_slice(taps, lru_conv_w[0], (0, qi * DLq))
    taps = lax.dynamic_update_slice(taps, sc_conv_w[0], (KL, qi * DLq))
    taps = _allreduce_small(jnp.where(ci == 0, taps, 0.0), "gather_conv_taps")
    cw, sw = taps[0:KL], taps[KL:KL + KS]

    cb = lru_conv_b
    wa, wi = lru_w_a[0].astype(BF16), lru_w_i[0].astype(BF16)
    ba, bi = lru_b_a.reshape(1, DL), lru_b_i.reshape(1, DL)
    mix_args = (cw, cb, wa, ba, wi, bi, lru_lambda, sw, lru_out_norm, sc_out_norm)
    gf = final_norm.reshape(1, D)

    res = _ffn_fwd(xt, ffn1_norm, full['ffn1_w_gate'], full['ffn1_w_up'], full['ffn1_w_down'], "ffn1_fwd", gather_later)
    x1, n1, G1, U1 = res[:4]
    full.update({n: unview(n, g) for n, g in zip(later_names, res[4:])})
    n2, z = _norm_mm(x1, mix_norm, full['w_in'], "mix_in_proj")
    h, ymix = _mix_fwd(z, *mix_args, "mix_fwd")
    x2 = _mm_fullk(ymix, full['w_out'], False, x1, F32, "mix_out_proj")[0]
    x3, n3, G2, U2 = _ffn_fwd(x2, ffn2_norm, full['ffn2_w_gate'], full['ffn2_w_up'], full['ffn2_w_down'], "ffn2_fwd")
    dx3, d3b, sqerr, dgf = _loss_head(x3, gf, target, "loss_head")

    sums, halves, shared = {}, {}, {}

    def dw(n, a, b, name, carry=None):
        res = _mm_tn_pair(a, b, c_arr, axis_of[n], name, carry)
        sums[n] = res[0]
        return res[2:]

    def chip_carry(names):
        return _chip_exchange_carry([sums[n] for n in names], [axis_of[n] for n in names])

    def chip_add(names, recv):
        for n, r in zip(names, recv):
            halves[n] = _quad_add(sums[n], r, qc_arr, axis_of[n], "grad_chip_add_" + n)

    dG2, dU2, H2 = _ffn_bwd_hidden(d3b, G2, U2, full['ffn2_w_down'], "ffn2_bwd_hidden")
    dn3 = _ffn_bwd_input(dG2, dU2, full['ffn2_w_gate'], full['ffn2_w_up'], "ffn2_bwd_input")[0]
    dx2, dx2b, dg_ffn2 = _rms_bwd_res(dn3, x2, ffn2_norm, dx3, 1.0, "ffn2_norm_bwd")
    dw('ffn2_w_gate', n3, dG2, "ffn2_dwg")
    dw('ffn2_w_up', n3, dU2, "ffn2_dwu")
    dw('ffn2_w_down', H2, d3b, "ffn2_dwd")
    dy = _mm_fullk(dx2b, full['w_out'], True, None, BF16, "mix_out_bwd")[0]
    dw('w_out', ymix, dx2b, "mix_dwout")
    dz, dwa, dwi, vec = _mix_bwd(z, h, dy, *mix_args, "mix_bwd")
    dw('w_in', n2, dz, "mix_dwin")
    dg_mix, dx1, d1b = _mm_nt_norm_bwd(dz, full['w_in'], x1, mix_norm, dx2, FFN_RESIDUAL_SCALE, "mix_in_bwd")

    names_mix, names_ffn2 = ['w_out', 'w_in'], ['ffn2_w_gate', 'ffn2_w_up', 'ffn2_w_down']
    res = _ffn_bwd_hidden(d1b, G1, U1, full['ffn1_w_down'], "ffn1_bwd_hidden", chip_carry(names_mix))
    dG1, dU1, H1 = res[:3]
    chip_add(names_mix, res[3:])
    res = _ffn_bwd_input(dG1, dU1, full['ffn1_w_gate'], full['ffn1_w_up'], "ffn1_bwd_input", chip_carry(names_ffn2))
    dn1 = res[0]
    chip_add(names_ffn2, res[1:])
    dx0, _, dg_ffn1 = _rms_bwd_res(dn1, xt, ffn1_norm, dx1, 1.0, "ffn1_norm_bwd")
    early = names_ffn2 + names_mix
    shared.update(zip(early, dw('ffn1_w_gate', n1, dG1, "ffn1_dwg", _pair_share_carry([halves[n] for n in early]))))
    chip_add(['ffn1_w_gate'], dw('ffn1_w_up', n1, dU1, "ffn1_dwu", chip_carry(['ffn1_w_gate'])))
    chip_add(['ffn1_w_up'], dw('ffn1_w_down', H1, d1b, "ffn1_dwd", chip_carry(['ffn1_w_up'])))
    chip_add(['ffn1_w_down'], _run_carry(chip_carry(['ffn1_w_down']), "grad_chip_exchange_ffn1_down"))
    late = ['ffn1_w_gate', 'ffn1_w_up', 'ffn1_w_down']
    shared.update(zip(late, _run_carry(_pair_share_carry([halves[n] for n in late]), "grad_pair_share")))
    out_g, out_d, out_m, out_v = {}, {}, {}, {}
    for n in BIG:
        shp = w[n].shape
        g = shared[n].reshape(shp[1], shp[2]) if axis_of[n] == 1 else shared[n]
        outs = _adamw(w[n][0], g, mom[n][0], var[n][0], "adamw_" + n)
        out_d[n], out_m[n], out_v[n], out_g[n] = (a.reshape(shp) for a in outs)

    small = [n for n in WEIGHTS if n not in BIG]
    local_small = {
        'ffn1_norm': dg_ffn1, 'mix_norm': dg_mix, 'lru_conv_w': vec[V_CW:V_CW + KL], 'lru_conv_b': vec[V_CB],
        'lru_w_a': dwa, 'lru_b_a': vec[V_BA], 'lru_w_i': dwi, 'lru_b_i': vec[V_BI], 'lru_lambda': vec[V_LAM],
        'sc_conv_w': vec[V_SW:V_SW + KS], 'lru_out_norm': vec[V_GLO], 'sc_out_norm': vec[V_GSO],
        'ffn2_norm': dg_ffn2, 'final_norm': dgf,
    }
    full_shapes = [local_small[n].shape for n in small] + [(1,)]
    reduced = _allreduce_small(_pack([local_small[n] for n in small] + [sqerr[0, 0:1]]), "allreduce_small")
    reduced = _unpack(reduced, full_shapes)
    loss = (0.5 / D) * reduced[-1][0]
    gsm = {}
    for n, g in zip(small, reduced[:-1]):
        if n in SMALL_SHARDED:
            g = lax.dynamic_slice(g, (0, qi * DLq), (g.shape[0], DLq))
        gsm[n] = g.reshape(w[n].shape)
    small_shapes = [w[n].shape for n in small]
    d_s, m_s, v_s, _ = _adamw(_pack([w[n] for n in small]), _pack([gsm[n] for n in small]),
                              _pack([mom[n] for n in small]), _pack([var[n] for n in small]), "adamw_small")
    for n, d, mn, vn in zip(small, _unpack(d_s, small_shapes), _unpack(m_s, small_shapes), _unpack(v_s, small_shapes)):
        out_g[n], out_d[n], out_m[n], out_v[n] = gsm[n], d, mn, vn

    return (loss, dx0.reshape(x.shape), *[out_g[n] for n in WEIGHTS], *[out_d[n] for n in WEIGHTS],
            *[out_m[n] for n in WEIGHTS], *[out_v[n] for n in WEIGHTS])
```

```python
import math

import jax
import jax.numpy as jnp
from jax import lax
from jax.experimental import pallas as pl
from jax.experimental.pallas import tpu as pltpu

F32 = jnp.float32
BF16 = jnp.bfloat16
MESH = pl.DeviceIdType.MESH
ANY = pl.BlockSpec(memory_space=pl.ANY)

NORM_EPS = 1e-6
LRU_C = 8.0
FFN_RESIDUAL_SCALE = 0.5
ADAM_LR = 0.001
ADAM_B1 = 0.9
ADAM_B2 = 0.999
ADAM_EPS = 1e-08
ADAM_WD = 0.01
ADAM_STEP = 10

V7X_VMEM_BYTES = 64 * 2**20
VMEM_LIMIT = V7X_VMEM_BYTES - 8 * 2**20
LANES = 128
SUBLANES = 8
PACK_W = LANES
PACK_ALIGN = SUBLANES * PACK_W

WEIGHTS = ['ffn1_norm', 'ffn1_w_gate', 'ffn1_w_up', 'ffn1_w_down', 'mix_norm', 'w_in', 'lru_conv_w', 'lru_conv_b',
           'lru_w_a', 'lru_b_a', 'lru_w_i', 'lru_b_i', 'lru_lambda', 'sc_conv_w', 'lru_out_norm', 'sc_out_norm',
           'w_out', 'ffn2_norm', 'ffn2_w_gate', 'ffn2_w_up', 'ffn2_w_down', 'final_norm']
BIG = ['ffn1_w_gate', 'ffn1_w_up', 'ffn1_w_down', 'w_in', 'w_out', 'ffn2_w_gate', 'ffn2_w_up', 'ffn2_w_down']
BIG_AXIS = [1, 1, 0, 1, 0, 1, 1, 0]
SMALL_SHARDED = ['lru_conv_w', 'sc_conv_w']


def _tile(n, pref, mult):
    if n <= pref:
        return n
    t = (pref // mult) * mult
    while t >= mult:
        if n % t == 0:
            return t
        t -= mult
    return n


def _params(*sem):
    return pltpu.CompilerParams(dimension_semantics=sem, vmem_limit_bytes=VMEM_LIMIT)


def _me():
    return lax.axis_index("x"), lax.axis_index("y"), lax.axis_index("c")


def _sigmoid(v):
    return 1.0 / (1.0 + jnp.exp(-v))


def _rstd(v):
    return lax.rsqrt(jnp.mean(v * v, axis=-1, keepdims=True) + NORM_EPS)


def _rms_bwd(dy, v, gain):
    r = _rstd(v)
    w = gain * dy
    dv = r * w - v * (r * r * r) * jnp.mean(v * w, axis=-1, keepdims=True)
    dgain = jnp.sum(dy * v * r, axis=0, keepdims=True)
    return dv, dgain


def _dot_nt(a, b):
    return lax.dot_general(a, b, (((1,), (1,)), ((), ())), preferred_element_type=F32)


def _dot_tn(a, b):
    return lax.dot_general(a, b, (((0,), (0,)), ((), ())), preferred_element_type=F32)


class _Carry:
    def __init__(self, inputs, out_shape, aliases, sems, start, finish, middle=None, middle_at=0.85):
        self.inputs, self.out_shape, self.aliases, self.sems = list(inputs), list(out_shape), dict(aliases), list(sems)
        self.start, self.finish = start, finish
        self.middle, self.middle_at = middle, middle_at


def _call(body, name, grid, in_specs, out_specs, out_shape, scratch_shapes, semantics, args, carry=None, prefetch=()):
    np_ = len(prefetch)
    if carry is None:
        spec = pltpu.PrefetchScalarGridSpec(num_scalar_prefetch=np_, grid=grid, in_specs=in_specs, out_specs=out_specs,
                                            scratch_shapes=scratch_shapes)
        return pl.pallas_call(body, name=name, grid_spec=spec, out_shape=out_shape,
                              compiler_params=_params(*semantics))(*prefetch, *args)
    ni, no, ns = len(in_specs), len(out_specs), len(scratch_shapes)
    ci, co = len(carry.inputs), len(carry.out_shape)

    def carrying(*refs):
        pre, refs = refs[:np_], refs[np_:]
        ins, refs = refs[:ni], refs[ni:]
        cins, refs = refs[:ci], refs[ci:]
        outs, refs = refs[:no], refs[no:]
        couts, refs = refs[:co], refs[co:]
        scratch, csems = refs[:ns], refs[ns:]
        step = pl.program_id(0)
        for ax in range(1, len(grid)):
            step = step * grid[ax] + pl.program_id(ax)
        steps = math.prod(grid)
        first = step == 0
        last = step == steps - 1

        @pl.when(first)
        def _():
            carry.start(cins, couts, csems)

        if carry.middle is not None:
            @pl.when(step == min(int(carry.middle_at * steps), steps - 1))
            def _():
                carry.middle(cins, couts, csems)

        body(*pre, *ins, *outs, *scratch)

        @pl.when(last)
        def _():
            carry.finish(cins, couts, csems)

    spec = pltpu.PrefetchScalarGridSpec(
        num_scalar_prefetch=np_, grid=grid, in_specs=list(in_specs) + [ANY] * ci,
        out_specs=list(out_specs) + [ANY] * co, scratch_shapes=list(scratch_shapes) + carry.sems)
    return pl.pallas_call(
        carrying, name=name, grid_spec=spec, out_shape=list(out_shape) + carry.out_shape,
        input_output_aliases={np_ + ni + i: no + j for i, j in carry.aliases.items()},
        compiler_params=_params(*(["arbitrary"] * len(grid))),
    )(*prefetch, *args, *carry.inputs)


def _run_carry(carry, name):
    ci, co = len(carry.inputs), len(carry.out_shape)

    def body(*refs):
        cins, couts, csems = refs[:ci], refs[ci:ci + co], refs[ci + co:]
        carry.start(cins, couts, csems)
        if carry.middle is not None:
            carry.middle(cins, couts, csems)
        carry.finish(cins, couts, csems)

    return pl.pallas_call(body, name=name, in_specs=[ANY] * ci, out_specs=[ANY] * co, out_shape=carry.out_shape,
                          input_output_aliases=carry.aliases, scratch_shapes=carry.sems)(*carry.inputs)


def _ffn_fwd(x, gain, wg, wu, wd, name, carry=None):
    T, D = x.shape
    FF = wg.shape[1]
    tm = _tile(T, 512, 16)
    tf = _tile(FF, 512, LANES)
    nf = FF // tf

    def body(x_ref, g_ref, wg_ref, wu_ref, wd_ref, xo_ref, n_ref, G_ref, U_ref, acc_ref):
        f = pl.program_id(1)

        @pl.when(f == 0)
        def _():
            xv = x_ref[...]
            n_ref[...] = (xv * _rstd(xv) * g_ref[...]).astype(BF16)
            acc_ref[...] = jnp.zeros_like(acc_ref)

        n = n_ref[...]
        G = jnp.dot(n, wg_ref[...], preferred_element_type=F32)
        U = jnp.dot(n, wu_ref[...], preferred_element_type=F32)
        G_ref[...] = G.astype(BF16)
        U_ref[...] = U.astype(BF16)
        H = (G * _sigmoid(G) * U).astype(BF16)
        acc_ref[...] += jnp.dot(H, wd_ref[...], preferred_element_type=F32)

        @pl.when(f == nf - 1)
        def _():
            xo_ref[...] = x_ref[...] + FFN_RESIDUAL_SCALE * acc_ref[...]

    return _call(
        body, name, (T // tm, nf),
        [pl.BlockSpec((tm, D), lambda i, f: (i, 0)),
         pl.BlockSpec((1, D), lambda i, f: (0, 0)),
         pl.BlockSpec((D, tf), lambda i, f: (0, f)),
         pl.BlockSpec((D, tf), lambda i, f: (0, f)),
         pl.BlockSpec((tf, D), lambda i, f: (f, 0))],
        [pl.BlockSpec((tm, D), lambda i, f: (i, 0)),
         pl.BlockSpec((tm, D), lambda i, f: (i, 0)),
         pl.BlockSpec((tm, tf), lambda i, f: (i, f)),
         pl.BlockSpec((tm, tf), lambda i, f: (i, f))],
        [jax.ShapeDtypeStruct((T, D), F32), jax.ShapeDtypeStruct((T, D), BF16),
         jax.ShapeDtypeStruct((T, FF), BF16), jax.ShapeDtypeStruct((T, FF), BF16)],
        [pltpu.VMEM((tm, D), F32)], ("parallel", "arbitrary"), (x, gain, wg, wu, wd), carry)


def _ffn_bwd_act(db, G, U, wg, wu, wd, name, carry=None):
    T, D = db.shape
    FF = wg.shape[1]
    tm = _tile(T, 512, 16)
    tf = _tile(FF, 512, LANES)

    def body(d_ref, G_ref, U_ref, wg_ref, wu_ref, wd_ref, dG_ref, dU_ref, H_ref, dn_ref):
        f = pl.program_id(1)
        dH = _dot_nt(d_ref[...], wd_ref[...])
        Gv = G_ref[...].astype(F32)
        Uv = U_ref[...].astype(F32)
        s = _sigmoid(Gv)
        sg = Gv * s
        H_ref[...] = (sg * Uv).astype(BF16)
        dU = (dH * sg).astype(BF16)
        dG = (dH * Uv * (s * (1.0 + Gv * (1.0 - s)))).astype(BF16)
        dG_ref[...] = dG
        dU_ref[...] = dU
        contrib = _dot_nt(dG, wg_ref[...]) + _dot_nt(dU, wu_ref[...])

        @pl.when(f == 0)
        def _():
            dn_ref[...] = contrib

        @pl.when(f > 0)
        def _():
            dn_ref[...] += contrib

    return _call(
        body, name, (T // tm, FF // tf),
        [pl.BlockSpec((tm, D), lambda i, f: (i, 0)),
         pl.BlockSpec((tm, tf), lambda i, f: (i, f)),
         pl.BlockSpec((tm, tf), lambda i, f: (i, f)),
         pl.BlockSpec((D, tf), lambda i, f: (0, f)),
         pl.BlockSpec((D, tf), lambda i, f: (0, f)),
         pl.BlockSpec((tf, D), lambda i, f: (f, 0))],
        [pl.BlockSpec((tm, tf), lambda i, f: (i, f)),
         pl.BlockSpec((tm, tf), lambda i, f: (i, f)),
         pl.BlockSpec((tm, tf), lambda i, f: (i, f)),
         pl.BlockSpec((tm, D), lambda i, f: (i, 0))],
        [jax.ShapeDtypeStruct((T, FF), BF16), jax.ShapeDtypeStruct((T, FF), BF16),
         jax.ShapeDtypeStruct((T, FF), BF16), jax.ShapeDtypeStruct((T, D), F32)],
        [], ("parallel", "arbitrary"), (db, G, U, wg, wu, wd), carry)


def _ffn_bwd_hidden(db, G, U, wd, name, carry=None):
    T, D = db.shape
    FF = wd.shape[0]
    tm = _tile(T, 1024, 16)
    tf = _tile(FF, 512, LANES)

    def body(d_ref, G_ref, U_ref, wd_ref, dG_ref, dU_ref, H_ref):
        dH = _dot_nt(d_ref[...], wd_ref[...])
        Gv = G_ref[...].astype(F32)
        Uv = U_ref[...].astype(F32)
        s = _sigmoid(Gv)
        sg = Gv * s
        H_ref[...] = (sg * Uv).astype(BF16)
        dU_ref[...] = (dH * sg).astype(BF16)
        dG_ref[...] = (dH * Uv * (s * (1.0 + Gv * (1.0 - s)))).astype(BF16)

    act = pl.BlockSpec((tm, tf), lambda i, f: (i, f))
    return _call(
        body, name, (T // tm, FF // tf),
        [pl.BlockSpec((tm, D), lambda i, f: (i, 0)), act, act, pl.BlockSpec((tf, D), lambda i, f: (f, 0))],
        [act, act, act], [jax.ShapeDtypeStruct((T, FF), BF16)] * 3,
        [], ("parallel", "arbitrary"), (db, G, U, wd), carry)


def _ffn_bwd_input(dG, dU, wg, wu, name, carry=None):
    T, FF = dG.shape
    D = wg.shape[0]
    tm = _tile(T, 512, 16)
    tn = _tile(D, 512, LANES)

    def body(dG_ref, dU_ref, wg_ref, wu_ref, dn_ref):
        dn_ref[...] = _dot_nt(dG_ref[...], wg_ref[...]) + _dot_nt(dU_ref[...], wu_ref[...])

    act = pl.BlockSpec((tm, FF), lambda i, j: (i, 0))
    wt = pl.BlockSpec((tn, FF), lambda i, j: (j, 0))
    return _call(body, name, (T // tm, D // tn), [act, act, wt, wt], [pl.BlockSpec((tm, tn), lambda i, j: (i, j))],
                 [jax.ShapeDtypeStruct((T, D), F32)], [], ("parallel", "arbitrary"), (dG, dU, wg, wu), carry)


TAIL_ROWS = 128


class _NormBwdTail:
    def __init__(self, T, D, tm, scale):
        self.T, self.D, self.tm, self.scale = T, D, tm, scale
        self.ni = T // tm
        self.scratch = [pltpu.VMEM((tm, D), F32), pltpu.VMEM((tm, D), F32), pltpu.VMEM((tm, D), F32),
                        pltpu.VMEM((tm, D), BF16), pltpu.SemaphoreType.DMA((4,))]
        self.out_shape = [jax.ShapeDtypeStruct((T, D), F32), jax.ShapeDtypeStruct((T, D), BF16)]

    def _rows(self, k):
        return pl.ds(pl.multiple_of(k * self.tm, self.tm), self.tm)

    def _loads(self, k, x_hbm, r_hbm, bufs):
        xbuf, rbuf, _, _, sems = bufs
        return [pltpu.make_async_copy(x_hbm.at[self._rows(k)], xbuf, sems.at[0]),
                pltpu.make_async_copy(r_hbm.at[self._rows(k)], rbuf, sems.at[1])]

    def _stores(self, k, dx_hbm, dxb_hbm, bufs):
        _, _, obuf, obbuf, sems = bufs
        return [pltpu.make_async_copy(obuf, dx_hbm.at[self._rows(k)], sems.at[2]),
                pltpu.make_async_copy(obbuf, dxb_hbm.at[self._rows(k)], sems.at[3])]

    def prefetch(self, i, x_hbm, r_hbm, bufs):
        for cp in self._loads(i, x_hbm, r_hbm, bufs):
            cp.start()

    def run(self, i, acc_ref, g_ref, x_hbm, r_hbm, dx_hbm, dxb_hbm, dg_ref, bufs):
        xbuf, rbuf, obuf, obbuf, _ = bufs
        for cp in self._loads(i, x_hbm, r_hbm, bufs):
            cp.wait()

        @pl.when(i > 0)
        def _():
            for cp in self._stores(i - 1, dx_hbm, dxb_hbm, bufs):
                cp.wait()

        dgain = None
        for r0 in range(0, self.tm, TAIL_ROWS):
            rs = slice(r0, min(r0 + TAIL_ROWS, self.tm))
            dv, dgr = _rms_bwd(acc_ref[rs, :], xbuf[rs, :], g_ref[...])
            dx = rbuf[rs, :] + dv
            obuf[rs, :] = dx
            obbuf[rs, :] = (self.scale * dx).astype(BF16)
            dgain = dgr if dgain is None else dgain + dgr
        for cp in self._stores(i, dx_hbm, dxb_hbm, bufs):
            cp.start()

        @pl.when(i == 0)
        def _():
            dg_ref[...] = dgain

        @pl.when(i > 0)
        def _():
            dg_ref[...] += dgain

        @pl.when(i == self.ni - 1)
        def _():
            for cp in self._stores(i, dx_hbm, dxb_hbm, bufs):
                cp.wait()


def _ffn_bwd_fused(db, G, U, wg, wu, wd, x_in, gain, dres, scale, name):
    T, D = db.shape
    FF = wg.shape[1]
    tm = _tile(T, 512, 16)
    tf = _tile(FF, 512, LANES)
    nf = FF // tf
    tail = _NormBwdTail(T, D, tm, scale)

    def body(d_ref, G_ref, U_ref, wg_ref, wu_ref, wd_ref, g_ref, x_hbm, r_hbm,
             dG_ref, dU_ref, H_ref, dg_ref, dx_hbm, dxb_hbm, acc_ref, *bufs):
        i, f = pl.program_id(0), pl.program_id(1)

        @pl.when(f == 0)
        def _():
            tail.prefetch(i, x_hbm, r_hbm, bufs)

        dH = _dot_nt(d_ref[...], wd_ref[...])
        Gv = G_ref[...].astype(F32)
        Uv = U_ref[...].astype(F32)
        s = _sigmoid(Gv)
        sg = Gv * s
        H_ref[...] = (sg * Uv).astype(BF16)
        dU = (dH * sg).astype(BF16)
        dG = (dH * Uv * (s * (1.0 + Gv * (1.0 - s)))).astype(BF16)
        dG_ref[...] = dG
        dU_ref[...] = dU
        contrib = _dot_nt(dG, wg_ref[...]) + _dot_nt(dU, wu_ref[...])

        @pl.when(f == 0)
        def _():
            acc_ref[...] = contrib

        @pl.when(f > 0)
        def _():
            acc_ref[...] += contrib

        @pl.when(f == nf - 1)
        def _():
            tail.run(i, acc_ref, g_ref, x_hbm, r_hbm, dx_hbm, dxb_hbm, dg_ref, bufs)

    act = pl.BlockSpec((tm, tf), lambda i, f: (i, f))
    return pl.pallas_call(
        body, name=name, grid=(T // tm, nf),
        in_specs=[pl.BlockSpec((tm, D), lambda i, f: (i, 0)), act, act,
                  pl.BlockSpec((D, tf), lambda i, f: (0, f)),
                  pl.BlockSpec((D, tf), lambda i, f: (0, f)),
                  pl.BlockSpec((tf, D), lambda i, f: (f, 0)),
                  pl.BlockSpec((1, D), lambda i, f: (0, 0)), ANY, ANY],
        out_specs=[act, act, act, pl.BlockSpec((1, D), lambda i, f: (0, 0)), ANY, ANY],
        out_shape=[jax.ShapeDtypeStruct((T, FF), BF16)] * 3 + [jax.ShapeDtypeStruct((1, D), F32)] + tail.out_shape,
        scratch_shapes=[pltpu.VMEM((tm, D), F32)] + tail.scratch,
        compiler_params=_params("arbitrary", "arbitrary"),
    )(db, G, U, wg, wu, wd, gain, x_in, dres)


def _mm_nt_norm_bwd(a, w, x_in, gain, dres, scale, name, carry=None):
    T, K = a.shape
    D = w.shape[0]
    tm = _tile(T, 512, 16)
    tk = _tile(K, 1280, LANES)
    nk = K // tk
    tail = _NormBwdTail(T, D, tm, scale)

    def body(a_ref, w_ref, g_ref, x_hbm, r_hbm, dg_ref, dx_hbm, dxb_hbm, acc_ref, *bufs):
        i, k = pl.program_id(0), pl.program_id(1)

        @pl.when(k == 0)
        def _():
            tail.prefetch(i, x_hbm, r_hbm, bufs)

        contrib = _dot_nt(a_ref[...], w_ref[...])

        @pl.when(k == 0)
        def _():
            acc_ref[...] = contrib

        @pl.when(k > 0)
        def _():
            acc_ref[...] += contrib

        @pl.when(k == nk - 1)
        def _():
            tail.run(i, acc_ref, g_ref, x_hbm, r_hbm, dx_hbm, dxb_hbm, dg_ref, bufs)

    return _call(
        body, name, (T // tm, nk),
        [pl.BlockSpec((tm, tk), lambda i, k: (i, k)), pl.BlockSpec((D, tk), lambda i, k: (0, k)),
         pl.BlockSpec((1, D), lambda i, k: (0, 0)), ANY, ANY],
        [pl.BlockSpec((1, D), lambda i, k: (0, 0)), ANY, ANY],
        [jax.ShapeDtypeStruct((1, D), F32)] + tail.out_shape,
        [pltpu.VMEM((tm, D), F32)] + tail.scratch, ("arbitrary", "arbitrary"), (a, w, gain, x_in, dres), carry)


def _rms_bwd_res(dn, x, gain, dres, scale, name, carry=None):
    T, D = x.shape
    tm = _tile(T, 256, 16)

    def body(dn_ref, x_ref, g_ref, dr_ref, dx_ref, dxb_ref, dg_ref):
        i = pl.program_id(0)
        dv, dgain = _rms_bwd(dn_ref[...], x_ref[...], g_ref[...])
        dx = dr_ref[...] + dv
        dx_ref[...] = dx
        dxb_ref[...] = (scale * dx).astype(BF16)

        @pl.when(i == 0)
        def _():
            dg_ref[...] = dgain

        @pl.when(i > 0)
        def _():
            dg_ref[...] += dgain

    row = pl.BlockSpec((tm, D), lambda i: (i, 0))
    vec = pl.BlockSpec((1, D), lambda i: (0, 0))
    return _call(
        body, name, (T // tm,), [row, row, vec, row], [row, row, vec],
        [jax.ShapeDtypeStruct((T, D), F32), jax.ShapeDtypeStruct((T, D), BF16), jax.ShapeDtypeStruct((1, D), F32)],
        [], ("arbitrary",), (dn, x, gain, dres), carry)


def _loss_head(x3, gain, target, name):
    T, D = x3.shape
    tm = _tile(T, 256, 16)

    def body(x_ref, g_ref, t_ref, dx_ref, dxb_ref, ls_ref, dg_ref):
        i = pl.program_id(0)
        xv = x_ref[...]
        err = xv * _rstd(xv) * g_ref[...] - t_ref[...]
        sq = jnp.sum(jnp.sum(err * err, axis=1, keepdims=True), axis=0, keepdims=True)
        dv, dgain = _rms_bwd(err * (1.0 / D), xv, g_ref[...])
        dx_ref[...] = dv
        dxb_ref[...] = (FFN_RESIDUAL_SCALE * dv).astype(BF16)
        sqb = jnp.broadcast_to(sq, (1, LANES))

        @pl.when(i == 0)
        def _():
            dg_ref[...] = dgain
            ls_ref[...] = sqb

        @pl.when(i > 0)
        def _():
            dg_ref[...] += dgain
            ls_ref[...] += sqb

    row = pl.BlockSpec((tm, D), lambda i: (i, 0))
    vec = pl.BlockSpec((1, D), lambda i: (0, 0))
    return pl.pallas_call(
        body, name=name, grid=(T // tm,),
        in_specs=[row, vec, row],
        out_specs=[row, row, pl.BlockSpec((1, LANES), lambda i: (0, 0)), vec],
        out_shape=[jax.ShapeDtypeStruct((T, D), F32), jax.ShapeDtypeStruct((T, D), BF16),
                   jax.ShapeDtypeStruct((1, LANES), F32), jax.ShapeDtypeStruct((1, D), F32)],
        compiler_params=_params("arbitrary"),
    )(x3, gain, target)


def _norm_mm(x, gain, w, name):
    T, D = x.shape
    N = w.shape[1]
    tm = _tile(T, 512, 16)
    tn = _tile(N, 2560, LANES)

    def body(x_ref, g_ref, w_ref, n_ref, z_ref):
        @pl.when(pl.program_id(1) == 0)
        def _():
            xv = x_ref[...]
            n_ref[...] = (xv * _rstd(xv) * g_ref[...]).astype(BF16)

        z_ref[...] = jnp.dot(n_ref[...], w_ref[...], preferred_element_type=F32).astype(BF16)

    return pl.pallas_call(
        body, name=name, grid=(T // tm, N // tn),
        in_specs=[pl.BlockSpec((tm, D), lambda i, j: (i, 0)),
                  pl.BlockSpec((1, D), lambda i, j: (0, 0)),
                  pl.BlockSpec((D, tn), lambda i, j: (0, j))],
        out_specs=[pl.BlockSpec((tm, D), lambda i, j: (i, 0)),
                   pl.BlockSpec((tm, tn), lambda i, j: (i, j))],
        out_shape=[jax.ShapeDtypeStruct((T, D), BF16), jax.ShapeDtypeStruct((T, N), BF16)],
        compiler_params=_params("parallel", "arbitrary"),
    )(x, gain, w)


def _mm_fullk(a, w, trans_w, residual, out_dtype, name, carry=None):
    T, K = a.shape
    N = w.shape[0] if trans_w else w.shape[1]
    tm = _tile(T, 512, 16)
    tn = _tile(N, 2048 * 2560 // K, LANES)

    def body(*refs):
        if residual is None:
            a_ref, w_ref, o_ref = refs
        else:
            a_ref, w_ref, r_ref, o_ref = refs
        if trans_w:
            acc = _dot_nt(a_ref[...], w_ref[...])
        else:
            acc = jnp.dot(a_ref[...], w_ref[...], preferred_element_type=F32)
        if residual is not None:
            acc = acc + r_ref[...]
        o_ref[...] = acc.astype(out_dtype)

    w_spec = pl.BlockSpec((tn, K), lambda i, j: (j, 0)) if trans_w else pl.BlockSpec((K, tn), lambda i, j: (0, j))
    in_specs = [pl.BlockSpec((tm, K), lambda i, j: (i, 0)), w_spec]
    args = [a, w]
    if residual is not None:
        in_specs.append(pl.BlockSpec((tm, tn), lambda i, j: (i, j)))
        args.append(residual)
    return _call(body, name, (T // tm, N // tn), in_specs, [pl.BlockSpec((tm, tn), lambda i, j: (i, j))],
                 [jax.ShapeDtypeStruct((T, N), out_dtype)], [], ("parallel", "arbitrary"), args, carry)


def _mm_tn(a, b, name, carry=None):
    T, M = a.shape
    N = b.shape[1]
    tmw = _tile(M, 2048, LANES)
    tnw = _tile(N, 2048 * 1408 // tmw, LANES)
    tk = _tile(T, 1024, 16)
    nk = T // tk

    def body(a_ref, b_ref, o_ref, acc_ref):
        k = pl.program_id(2)

        @pl.when(k == 0)
        def _():
            acc_ref[...] = jnp.zeros_like(acc_ref)

        acc_ref[...] += _dot_tn(a_ref[...], b_ref[...])

        @pl.when(k == nk - 1)
        def _():
            o_ref[...] = acc_ref[...].astype(BF16)

    return _call(
        body, name, (M // tmw, N // tnw, nk),
        [pl.BlockSpec((tk, tmw), lambda i, j, k: (k, i)),
         pl.BlockSpec((tk, tnw), lambda i, j, k: (k, j))],
        [pl.BlockSpec((tmw, tnw), lambda i, j, k: (i, j))],
        [jax.ShapeDtypeStruct((M, N), BF16)],
        [pltpu.VMEM((tmw, tnw), F32)], ("parallel", "parallel", "arbitrary"), (a, b), carry)


def _mm_tn_pair(a, b, c_arr, axis, name, carry=None):
    T, M = a.shape
    N = b.shape[1]
    tk = _tile(T, 2048, 16)
    nk = T // tk
    nq = 4
    if axis == 1:
        rows, cols = M // 2, N // nq
        a_spec = pl.BlockSpec((tk, rows), lambda p, j, k, cr: (k, jnp.where(p == 0, 1 - cr[0], cr[0])))
        b_spec = pl.BlockSpec((tk, cols), lambda p, j, k, cr: (k, j))
        s_shape, land_shape = (1, rows, N), (rows, N)
        s_spec = pl.BlockSpec((None, rows, cols), lambda p, j, k, cr: (0, 0, j * p))
    else:
        rows, cols = M // nq, N // 2
        a_spec = pl.BlockSpec((tk, rows), lambda p, j, k, cr: (k, j))
        b_spec = pl.BlockSpec((tk, cols), lambda p, j, k, cr: (k, jnp.where(p == 0, 1 - cr[0], cr[0])))
        s_shape, land_shape = (nq, rows, cols), (nq, rows, cols)
        s_spec = pl.BlockSpec((None, rows, cols), lambda p, j, k, cr: (j * p, 0, 0))

    def body(c_ref, a_ref, b_ref, s_ref, land, acc_ref, stage, got, send_sems, recv_sems, loc_sem):
        p, j, k = pl.program_id(0), pl.program_id(1), pl.program_id(2)
        x, y, c = _me()

        def tile(jj):
            return land.at[:, pl.ds(jj * cols, cols)] if axis == 1 else land.at[jj]

        def send(jj):
            return _remote(stage, tile(jj), send_sems, recv_sems, jj, (x, y, 1 - c))

        @pl.when(k == 0)
        def _():
            acc_ref[...] = jnp.zeros_like(acc_ref)

        acc_ref[...] += _dot_tn(a_ref[...], b_ref[...])

        def fetch(jj):
            return pltpu.make_async_copy(tile(jj), got, loc_sem.at[0])

        for jj in range(nq):
            @pl.when(jnp.logical_and(k == nk - 1, jnp.logical_and(p == 0, j == jj)))
            def _():
                if jj > 0:
                    send(jj - 1).wait_send()
                stage[...] = acc_ref[...].astype(BF16)
                send(jj).start()

            @pl.when(jnp.logical_and(k == max(nk - 2, 0), jnp.logical_and(p == 1, j == jj)))
            def _():
                if jj == 0:
                    send(nq - 1).wait_send()
                send(jj).wait_recv()
                fetch(jj).start()

            @pl.when(jnp.logical_and(k == nk - 1, jnp.logical_and(p == 1, j == jj)))
            def _():
                fetch(jj).wait()
                s_ref[...] = (acc_ref[...] + got[...].astype(F32)).astype(BF16)

    return _call(
        body, name, (2, nq, nk), [a_spec, b_spec], [s_spec, ANY],
        [jax.ShapeDtypeStruct(s_shape, BF16), jax.ShapeDtypeStruct(land_shape, BF16)],
        [pltpu.VMEM((rows, cols), F32), pltpu.VMEM((rows, cols), BF16), pltpu.VMEM((rows, cols), BF16),
         pltpu.SemaphoreType.DMA((nq,)), pltpu.SemaphoreType.DMA((nq,)), pltpu.SemaphoreType.DMA((1,))],
        ("arbitrary", "arbitrary", "arbitrary"), (a, b), carry, (c_arr,))


GELU_K = math.sqrt(2.0 / math.pi)
GELU_C = 0.044715


def _gelu_and_grad(v):
    u = GELU_K * (v + GELU_C * v * v * v)
    th = jnp.tanh(u)
    g = 0.5 * v * (1.0 + th)
    dg = 0.5 * (1.0 + th) + 0.5 * v * (1.0 - th * th) * GELU_K * (1.0 + 3.0 * GELU_C * v * v)
    return g, dg


def _neg_expm1(v):
    poly = v * (1.0 + v * (0.5 + v * (1.0 / 6 + v * (1.0 / 24 + v * (1.0 / 120 + v * (1.0 / 720))))))
    return jnp.where(v > -0.25, -poly, 1.0 - jnp.exp(v))


def _softplus_neg(lam):
    e = jnp.exp(-jnp.abs(lam))
    log1pe = jnp.where(e < 1e-4, e * (1.0 - 0.5 * e), jnp.log(1.0 + e))
    sp = jnp.maximum(-lam, 0.0) + log1pe
    dsp = -1.0 / (1.0 + jnp.exp(lam))
    return sp, dsp


def _earlier(ext, j):
    return pltpu.roll(ext, j, 0)[SUBLANES:, :]


def _later(ext, j):
    n = ext.shape[0]
    return pltpu.roll(ext, n - j, 0)[:n - SUBLANES, :]


def _taps(v, halo, K):
    ext = jnp.concatenate([halo, v], axis=0)
    return [v] + [_earlier(ext, j) for j in range(1, K)]


def _block_diag(vb, w_ref, nh, hd):
    return jnp.concatenate(
        [jnp.dot(vb[:, h * hd:(h + 1) * hd], w_ref[h], preferred_element_type=F32) for h in range(nh)], axis=1)


def _lru_gates(xc, wa_ref, ba_ref, wi_ref, bi_ref, sp, nh, hd):
    xcb = xc.astype(BF16)
    r = _sigmoid(_block_diag(xcb, wa_ref, nh, hd) + ba_ref[...])
    ig = _sigmoid(_block_diag(xcb, wi_ref, nh, hd) + bi_ref[...])
    log_a = -LRU_C * r * sp
    a = jnp.exp(log_a)
    mult = jnp.sqrt(_neg_expm1(2.0 * log_a))
    return xcb, r, ig, a, mult


def _mix_fwd(z, cw, cb, wa, ba, wi, bi, lam, sw, glo, gso, name):
    T = z.shape[0]
    DL = cb.shape[1]
    DS = gso.shape[1]
    NH, HD = wa.shape[0], wa.shape[1]
    KL, KS = cw.shape[0], sw.shape[0]
    tt = _tile(T, 128, 16)
    o_g, o_b, o_c, o_x = DL, 2 * DL, 2 * DL + DS, 2 * DL + 2 * DS

    def body(z_ref, cw_ref, cb_ref, wa_ref, ba_ref, wi_ref, bi_ref, lam_ref, sw_ref, glo_ref, gso_ref,
             h_ref, y_ref, cx_ref, cp_ref, ch_ref):
        @pl.when(pl.program_id(0) == 0)
        def _():
            cx_ref[...] = jnp.zeros_like(cx_ref)
            cp_ref[...] = jnp.zeros_like(cp_ref)
            ch_ref[...] = jnp.zeros_like(ch_ref)

        def zcol(o, n):
            return z_ref[:, o:o + n].astype(F32)

        lx = zcol(0, DL)
        xs = _taps(lx, cx_ref[...], KL)
        cx_ref[...] = lx[tt - SUBLANES:, :]
        xc = cb_ref[...] + xs[0] * cw_ref[KL - 1:KL, :]
        for j in range(1, KL):
            xc = xc + xs[j] * cw_ref[KL - 1 - j:KL - j, :]
        sp, _ = _softplus_neg(lam_ref[...])
        _, _, ig, a, mult = _lru_gates(xc, wa_ref, ba_ref, wi_ref, bi_ref, sp, NH, HD)
        b = mult * (ig * xc)
        rows = lax.broadcasted_iota(jnp.int32, (tt, DL), 0)
        s = 1
        while s < tt:
            keep = rows >= s
            b = jnp.where(keep, a * pltpu.roll(b, s, 0) + b, b)
            a = jnp.where(keep, a * pltpu.roll(a, s, 0), a)
            s *= 2
        h = a * ch_ref[SUBLANES - 1:SUBLANES, :] + b
        ch_ref[...] = h[tt - SUBLANES:, :]
        h_ref[...] = h
        ge, _ = _gelu_and_grad(zcol(o_g, DL))
        ylru = h * ge
        y_ref[:, 0:DL] = (ylru * _rstd(ylru) * glo_ref[...]).astype(BF16)

        p = zcol(o_c, DS) * zcol(o_x, DS)
        ps = _taps(p, cp_ref[...], KS)
        cp_ref[...] = p[tt - SUBLANES:, :]
        cv = ps[0] * sw_ref[KS - 1:KS, :]
        for j in range(1, KS):
            cv = cv + ps[j] * sw_ref[KS - 1 - j:KS - j, :]
        ysc = zcol(o_b, DS) * cv
        y_ref[:, DL:DL + DS] = (ysc * _rstd(ysc) * gso_ref[...]).astype(BF16)

    def full(shape):
        return pl.BlockSpec(shape, lambda t: (0,) * len(shape))

    return pl.pallas_call(
        body, name=name, grid=(T // tt,),
        in_specs=[pl.BlockSpec((tt, z.shape[1]), lambda t: (t, 0)),
                  full(cw.shape), full(cb.shape), full(wa.shape), full(ba.shape), full(wi.shape), full(bi.shape),
                  full(lam.shape), full(sw.shape), full(glo.shape), full(gso.shape)],
        out_specs=[pl.BlockSpec((tt, DL), lambda t: (t, 0)), pl.BlockSpec((tt, DL + DS), lambda t: (t, 0))],
        out_shape=[jax.ShapeDtypeStruct((T, DL), F32), jax.ShapeDtypeStruct((T, DL + DS), BF16)],
        scratch_shapes=[pltpu.VMEM((SUBLANES, DL), F32), pltpu.VMEM((SUBLANES, DS), F32),
                        pltpu.VMEM((SUBLANES, DL), F32)],
        compiler_params=_params("arbitrary"),
    )(z, cw, cb, wa, ba, wi, bi, lam, sw, glo, gso)


V_BA, V_BI, V_LAM, V_CB, V_CW, V_SW, V_GLO, V_GSO, V_ROWS = 0, 1, 2, 3, 4, 8, 11, 12, 16


def _mix_bwd(z, h, dy, cw, cb, wa, ba, wi, bi, lam, sw, glo, gso, name):
    T = z.shape[0]
    DL = cb.shape[1]
    DS = gso.shape[1]
    NH, HD = wa.shape[0], wa.shape[1]
    KL, KS = cw.shape[0], sw.shape[0]
    tt = _tile(T, 64, 16)
    nt = T // tt
    ZH = 2 * SUBLANES
    o_g, o_b, o_c, o_x = DL, 2 * DL, 2 * DL + DS, 2 * DL + 2 * DS

    def body(z_ref, zh_ref, h_ref, hh_ref, dy_ref, cw_ref, cb_ref, wa_ref, ba_ref, wi_ref, bi_ref, lam_ref,
             sw_ref, glo_ref, gso_ref, dz_ref, dwa_ref, dwi_ref, vec_ref, cdx_ref, cdc_ref, cdh_ref):
        i = pl.program_id(0)
        tr = nt - 1 - i

        @pl.when(i == 0)
        def _():
            dwa_ref[...] = jnp.zeros_like(dwa_ref)
            dwi_ref[...] = jnp.zeros_like(dwi_ref)
            vec_ref[...] = jnp.zeros_like(vec_ref)
            cdx_ref[...] = jnp.zeros_like(cdx_ref)
            cdc_ref[...] = jnp.zeros_like(cdc_ref)
            cdh_ref[...] = jnp.zeros_like(cdh_ref)

        def acc_row(r, v):
            vec_ref[pl.ds(r, 1), :] += jnp.sum(v, axis=0, keepdims=True)

        has_prev = tr > 0
        rows = lax.broadcasted_iota(jnp.int32, (tt, DL), 0)

        def zcol(o, n):
            return z_ref[:, o:o + n].astype(F32)

        def zhalo(o, n):
            return jnp.where(has_prev, zh_ref[:, o:o + n].astype(F32)[SUBLANES:, :], 0.0)

        lx = zcol(0, DL)
        xs = _taps(lx, zhalo(0, DL), KL)
        xc = cb_ref[...] + xs[0] * cw_ref[KL - 1:KL, :]
        for j in range(1, KL):
            xc = xc + xs[j] * cw_ref[KL - 1 - j:KL - j, :]
        sp, dsp = _softplus_neg(lam_ref[...])
        xcb, r, ig, a, mult = _lru_gates(xc, wa_ref, ba_ref, wi_ref, bi_ref, sp, NH, HD)
        hv = h_ref[...]
        hprev = _earlier(jnp.concatenate([jnp.where(has_prev, hh_ref[...], 0.0), hv], axis=0), 1)
        gate = zcol(o_g, DL)
        ge, dge = _gelu_and_grad(gate)
        ylru = hv * ge

        d_ylru, dglo = _rms_bwd(dy_ref[:, 0:DL].astype(F32), ylru, glo_ref[...])
        vec_ref[pl.ds(V_GLO, 1), :] += dglo
        dz_ref[:, o_g:o_g + DL] = (d_ylru * hv * dge).astype(BF16)
        bq = d_ylru * ge
        aq = jnp.where(rows == tt - 1, 1.0, pltpu.roll(a, tt - 1, 0))
        s = 1
        while s < tt:
            keep = rows < tt - s
            bq = jnp.where(keep, aq * pltpu.roll(bq, tt - s, 0) + bq, bq)
            aq = jnp.where(keep, aq * pltpu.roll(aq, tt - s, 0), aq)
            s *= 2
        dhh = bq + aq * cdh_ref[0:1, :]
        cdh_ref[0:1, :] = a[0:1, :] * dhh[0:1, :]

        da = dhh * hprev
        dmult = dhh * (ig * xc)
        d_i = dhh * mult * xc
        dxc = dhh * mult * ig
        dlog = da * a - dmult * (a * a) / mult
        acc_row(V_LAM, dlog * (-LRU_C * r) * dsp)
        dpa = dlog * (-LRU_C * sp) * r * (1.0 - r)
        dpi = d_i * ig * (1.0 - ig)
        acc_row(V_BA, dpa)
        acc_row(V_BI, dpi)
        dpab = dpa.astype(BF16)
        dpib = dpi.astype(BF16)
        back = []
        for hh in range(NH):
            sl = slice(hh * HD, (hh + 1) * HD)
            dwa_ref[hh] += _dot_tn(xcb[:, sl], dpab[:, sl])
            dwi_ref[hh] += _dot_tn(xcb[:, sl], dpib[:, sl])
            back.append(_dot_nt(dpab[:, sl], wa_ref[hh]) + _dot_nt(dpib[:, sl], wi_ref[hh]))
        dxc = dxc + jnp.concatenate(back, axis=1)

        acc_row(V_CB, dxc)
        extd = jnp.concatenate([dxc, cdx_ref[...]], axis=0)
        cdx_ref[...] = dxc[0:SUBLANES, :]
        dlx = dxc * cw_ref[KL - 1:KL, :]
        acc_row(V_CW + KL - 1, dxc * xs[0])
        for j in range(1, KL):
            dlx = dlx + _later(extd, j) * cw_ref[KL - 1 - j:KL - j, :]
            acc_row(V_CW + KL - 1 - j, dxc * xs[j])
        dz_ref[:, 0:DL] = dlx.astype(BF16)

        sb = zcol(o_b, DS)
        sc = zcol(o_c, DS)
        sx = zcol(o_x, DS)
        p = sc * sx
        ps = _taps(p, zhalo(o_c, DS) * zhalo(o_x, DS), KS)
        cv = ps[0] * sw_ref[KS - 1:KS, :]
        for j in range(1, KS):
            cv = cv + ps[j] * sw_ref[KS - 1 - j:KS - j, :]
        d_ysc, dgso = _rms_bwd(dy_ref[:, DL:DL + DS].astype(F32), sb * cv, gso_ref[...])
        vec_ref[pl.ds(V_GSO, 1), :] += dgso
        dz_ref[:, o_b:o_b + DS] = (d_ysc * cv).astype(BF16)
        dcv = d_ysc * sb
        extc = jnp.concatenate([dcv, cdc_ref[...]], axis=0)
        cdc_ref[...] = dcv[0:SUBLANES, :]
        dp = dcv * sw_ref[KS - 1:KS, :]
        acc_row(V_SW + KS - 1, dcv * ps[0])
        for j in range(1, KS):
            dp = dp + _later(extc, j) * sw_ref[KS - 1 - j:KS - j, :]
            acc_row(V_SW + KS - 1 - j, dcv * ps[j])
        dz_ref[:, o_c:o_c + DS] = (dp * sx).astype(BF16)
        dz_ref[:, o_x:o_x + DS] = (dp * sc).astype(BF16)

    def full(shape):
        return pl.BlockSpec(shape, lambda t: (0,) * len(shape))

    def rev(t):
        return nt - 1 - t

    def halo(t, rows):
        return jnp.maximum(rev(t) * (tt // rows) - 1, 0)

    return pl.pallas_call(
        body, name=name, grid=(nt,),
        in_specs=[pl.BlockSpec((tt, z.shape[1]), lambda t: (rev(t), 0)),
                  pl.BlockSpec((ZH, z.shape[1]), lambda t: (halo(t, ZH), 0)),
                  pl.BlockSpec((tt, DL), lambda t: (rev(t), 0)),
                  pl.BlockSpec((SUBLANES, DL), lambda t: (halo(t, SUBLANES), 0)),
                  pl.BlockSpec((tt, DL + DS), lambda t: (rev(t), 0)),
                  full(cw.shape), full(cb.shape), full(wa.shape), full(ba.shape), full(wi.shape), full(bi.shape),
                  full(lam.shape), full(sw.shape), full(glo.shape), full(gso.shape)],
        out_specs=[pl.BlockSpec((tt, z.shape[1]), lambda t: (rev(t), 0)),
                   full(wa.shape), full(wi.shape), full((V_ROWS, DL))],
        out_shape=[jax.ShapeDtypeStruct(z.shape, BF16), jax.ShapeDtypeStruct(wa.shape, F32),
                   jax.ShapeDtypeStruct(wi.shape, F32), jax.ShapeDtypeStruct((V_ROWS, DL), F32)],
        scratch_shapes=[pltpu.VMEM((SUBLANES, DL), F32), pltpu.VMEM((SUBLANES, DS), F32),
                        pltpu.VMEM((SUBLANES, DL), F32)],
        compiler_params=_params("arbitrary"),
    )(z, z, h, h, dy, cw, cb, wa, ba, wi, bi, lam, sw, glo, gso)


def _pair_add(p, r1, c, name):
    G, R, C = r1.shape
    tr = _tile(R, 256, 16)
    tc = _tile(C, 1408, LANES)

    def body(c_ref, p_ref, r_ref, o_ref):
        o_ref[...] = (p_ref[...].astype(F32) + r_ref[...].astype(F32)).astype(BF16)

    blk = (None, tr, tc)
    return pl.pallas_call(
        body, name=name,
        grid_spec=pltpu.PrefetchScalarGridSpec(
            num_scalar_prefetch=1, grid=(G, R // tr, C // tc),
            in_specs=[pl.BlockSpec(blk, lambda g, i, j, cr: (2 * g + cr[0], i, j)),
                      pl.BlockSpec(blk, lambda g, i, j, cr: (g, i, j))],
            out_specs=pl.BlockSpec(blk, lambda g, i, j, cr: (g, i, j))),
        out_shape=jax.ShapeDtypeStruct((G, R, C), BF16),
        compiler_params=_params("parallel", "parallel", "parallel"),
    )(c, p, r1)


def _quad_add(s, r2, qc, axis, name):
    _, R, W = r2.shape
    tr = _tile(R, 256, 16)

    def body(qc_ref, s_ref, r0_ref, r1_ref, r2_ref, o_ref):
        o_ref[...] = ((s_ref[...].astype(F32) + r0_ref[...].astype(F32)) + r1_ref[...].astype(F32)) + r2_ref[...].astype(F32)

    blk = (None, tr, W)
    if axis == 1:
        own = pl.BlockSpec(blk, lambda i, qr: (0, i, qr[0]))
    else:
        own = pl.BlockSpec(blk, lambda i, qr: (qr[0], i, 0))
    return pl.pallas_call(
        body, name=name,
        grid_spec=pltpu.PrefetchScalarGridSpec(
            num_scalar_prefetch=1, grid=(R // tr,),
            in_specs=[own] + [pl.BlockSpec(blk, lambda i, qr, j=j: (j, i, 0)) for j in range(3)],
            out_specs=pl.BlockSpec(blk, lambda i, qr: (qr[1], i, 0))),
        out_shape=jax.ShapeDtypeStruct((2, R, W), F32),
        compiler_params=_params("parallel"),
    )(qc, s, r2, r2, r2)


CAST_BLOCKS = 4


def _cast_into_full(shards, qc, axes, name, carry=None):
    M = len(shards)
    nb = CAST_BLOCKS

    def body(qc_ref, *refs):
        for s_ref, o_ref in zip(refs[:M], refs[M:]):
            o_ref[...] = s_ref[...].astype(BF16)

    in_specs, out_specs, out_shape = [], [], []
    for s, ax in zip(shards, axes):
        R, W = s.shape
        Rh = R // 2
        tr = Rh // nb
        in_specs.append(pl.BlockSpec((tr, W), lambda hf, i, qr: (hf * nb + i, 0)))
        if ax == 1:
            out_shape.append(jax.ShapeDtypeStruct((2, Rh, 4 * W), BF16))
            out_specs.append(pl.BlockSpec((None, tr, W), lambda hf, i, qr: (hf, i, qr[0])))
        else:
            out_shape.append(jax.ShapeDtypeStruct((8, Rh, W), BF16))
            out_specs.append(pl.BlockSpec((None, tr, W), lambda hf, i, qr: (2 * qr[0] + hf, i, 0)))
    return _call(body, name, (2, nb), in_specs, out_specs, out_shape, [], ("parallel", "parallel"), shards, carry, (qc,))


def _adamw(w, g, m, v, name):
    R, C = w.shape
    tr = _tile(R, 256, SUBLANES)
    tc = C // 2 if g.ndim == 3 else _tile(C, 2048, LANES)
    c1 = 1.0 - ADAM_B1 ** ADAM_STEP
    c2 = 1.0 - ADAM_B2 ** ADAM_STEP

    def body(w_ref, g_ref, m_ref, v_ref, d_ref, mo_ref, vo_ref, go_ref):
        gv = g_ref[...]
        go_ref[...] = gv
        mn = ADAM_B1 * m_ref[...] + (1.0 - ADAM_B1) * gv
        vn = ADAM_B2 * v_ref[...] + (1.0 - ADAM_B2) * (gv * gv)
        mo_ref[...] = mn
        vo_ref[...] = vn
        d_ref[...] = -ADAM_LR * ((mn / c1) / (jnp.sqrt(vn / c2) + ADAM_EPS) + ADAM_WD * w_ref[...])

    blk = pl.BlockSpec((tr, tc), lambda i, j: (i, j))
    g_blk = pl.BlockSpec((None, tr, tc), lambda i, j: (j, i, 0)) if g.ndim == 3 else blk
    sh = jax.ShapeDtypeStruct((R, C), F32)
    return pl.pallas_call(
        body, name=name, grid=(R // tr, C // tc),
        in_specs=[blk, g_blk, blk, blk], out_specs=[blk] * 4, out_shape=[sh] * 4,
        compiler_params=_params("parallel", "parallel"),
    )(w, g, m, v)


def _other_chips(x, y):
    return [(1 - x, y), (x, 1 - y), (1 - x, 1 - y)]


def _remote(src, dst, send_sems, recv_sems, idx, dev):
    return pltpu.make_async_remote_copy(src_ref=src, dst_ref=dst, send_sem=send_sems.at[idx], recv_sem=recv_sems.at[idx],
                                        device_id=dev, device_id_type=MESH)


def _gather_carry(fulls, axes):
    M = len(fulls)

    def win(outs, m, qq, cc):
        if axes[m] == 1:
            W = fulls[m].shape[2] // 4
            return outs[m].at[cc, :, pl.ds(pl.multiple_of(qq * W, LANES), W)]
        return outs[m].at[2 * qq + cc]

    def ici(outs, sems, m, j, src_q):
        x, y, c = _me()
        cx, cy = _other_chips(x, y)[j]
        blk = win(outs, m, src_q, c)
        return _remote(blk, blk, sems[0], sems[1], 6 * m + j, (cx, cy, c))

    def d2d(outs, sems, m, j, half):
        x, y, c = _me()
        cx, cy = _other_chips(x, y)[j]
        blk = win(outs, m, 2 * cx + cy, half)
        return _remote(blk, blk, sems[0], sems[1], 6 * m + 3 + j, (x, y, 1 - c))

    def start(ins, outs, sems):
        x, y, c = _me()
        for m in range(M):
            for j in range(3):
                ici(outs, sems, m, j, 2 * x + y).start()

    def middle(ins, outs, sems):
        x, y, c = _me()
        for m in range(M):
            for j, (cx, cy) in enumerate(_other_chips(x, y)):
                ici(outs, sems, m, j, 2 * cx + cy).wait_recv()
                d2d(outs, sems, m, j, c).start()

    def finish(ins, outs, sems):
        x, y, c = _me()
        for m in range(M):
            for j in range(3):
                d2d(outs, sems, m, j, 1 - c).wait_recv()
        for m in range(M):
            for j in range(3):
                ici(outs, sems, m, j, 2 * x + y).wait_send()
                d2d(outs, sems, m, j, c).wait_send()

    return _Carry(fulls, [jax.ShapeDtypeStruct(f.shape, f.dtype) for f in fulls], {m: m for m in range(M)},
                  [pltpu.SemaphoreType.DMA((6 * M,)), pltpu.SemaphoreType.DMA((6 * M,))], start, finish, middle)


def _gather_two_way_carry(fulls, axes):
    M = len(fulls)

    def part(outs, m, qq, cc, p):
        Rp = fulls[m].shape[1] // 2
        rows = pl.ds(p * Rp, Rp)
        if axes[m] == 1:
            W = fulls[m].shape[2] // 4
            return outs[m].at[cc, rows, pl.ds(pl.multiple_of(qq * W, LANES), W)]
        return outs[m].at[2 * qq + cc, rows, :]

    def copies(outs, sems):
        x, y, c = _me()
        q, qx, qy, qd = 2 * x + y, 2 * (1 - x) + y, 2 * x + (1 - y), 2 * (1 - x) + (1 - y)
        xn, yn, sib = (1 - x, y, c), (x, 1 - y, c), (x, y, 1 - c)
        table = {}
        for m in range(M):
            def cp(blk, k, dev):
                return _remote(blk, blk, sems[0], sems[1], 12 * m + k, dev)
            own0, own1 = part(outs, m, q, c, 0), part(outs, m, q, c, 1)
            table[m] = dict(
                to=[cp(own0, 0, xn), cp(own1, 1, yn), cp(own1, 2, xn), cp(own0, 3, yn)],
                landed=[cp(part(outs, m, qx, c, 0), 0, xn), cp(part(outs, m, qy, c, 1), 1, yn),
                        cp(part(outs, m, qx, c, 1), 2, xn), cp(part(outs, m, qy, c, 0), 3, yn),
                        cp(part(outs, m, qd, c, 0), 4, yn), cp(part(outs, m, qd, c, 1), 5, xn)],
                passed=[cp(part(outs, m, qx, c, 0), 4, yn), cp(part(outs, m, qy, c, 1), 5, xn)],
                handed=[cp(part(outs, m, qq, c, p), 6 + k, sib)
                        for k, (qq, p) in enumerate([(qx, 0), (qy, 1), (qx, 1), (qy, 0), (qd, 0), (qd, 1)])],
                taken=[cp(part(outs, m, qq, 1 - c, p), 6 + k, sib)
                       for k, (qq, p) in enumerate([(qx, 0), (qy, 1), (qx, 1), (qy, 0), (qd, 0), (qd, 1)])])
        return table

    def start(ins, outs, sems):
        t = copies(outs, sems)
        for m in range(M):
            for cp in t[m]['to']:
                cp.start()

    def finish(ins, outs, sems):
        t = copies(outs, sems)
        for m in range(M):
            for k in range(4):
                t[m]['landed'][k].wait_recv()
                if k < 2:
                    t[m]['passed'][k].start()
                t[m]['handed'][k].start()
        for m in range(M):
            for k in (4, 5):
                t[m]['landed'][k].wait_recv()
                t[m]['handed'][k].start()
        for m in range(M):
            for cp in t[m]['taken']:
                cp.wait_recv()
            for cp in t[m]['to'] + t[m]['passed'] + t[m]['handed']:
                cp.wait_send()

    return _Carry(fulls, [jax.ShapeDtypeStruct(f.shape, f.dtype) for f in fulls], {m: m for m in range(M)},
                  [pltpu.SemaphoreType.DMA((12 * M,)), pltpu.SemaphoreType.DMA((12 * M,))], start, finish)


def _pair_exchange_carry(parts):
    M = len(parts)
    groups = [p.shape[0] // 2 for p in parts]
    base = [sum(groups[:m]) for m in range(M)]
    out_shape = [jax.ShapeDtypeStruct((g,) + p.shape[1:], p.dtype) for g, p in zip(groups, parts)]

    def copies(ins, outs, sems):
        x, y, c = _me()
        return [_remote(ins[m].at[2 * g + 1 - c], outs[m].at[g], sems[0], sems[1], base[m] + g, (x, y, 1 - c))
                for m in range(M) for g in range(groups[m])]

    def start(ins, outs, sems):
        for cp in copies(ins, outs, sems):
            cp.start()

    def finish(ins, outs, sems):
        for cp in copies(ins, outs, sems):
            cp.wait()

    n = sum(groups)
    return _Carry(parts, out_shape, {}, [pltpu.SemaphoreType.DMA((n,)), pltpu.SemaphoreType.DMA((n,))], start, finish)


def _chip_exchange_carry(sums, axes):
    M = len(sums)
    out_shape = []
    for s, ax in zip(sums, axes):
        _, Rh, C = s.shape
        out_shape.append(jax.ShapeDtypeStruct((3, Rh, C // 4 if ax == 1 else C), s.dtype))

    def copies(ins, outs, sems):
        x, y, c = _me()
        cps = []
        for m in range(M):
            for j, (cx, cy) in enumerate(_other_chips(x, y)):
                qj = 2 * cx + cy
                if axes[m] == 1:
                    W = sums[m].shape[2] // 4
                    src = ins[m].at[0, :, pl.ds(pl.multiple_of(qj * W, LANES), W)]
                else:
                    src = ins[m].at[qj]
                cps.append(_remote(src, outs[m].at[j], sems[0], sems[1], 3 * m + j, (cx, cy, c)))
        return cps

    def start(ins, outs, sems):
        for cp in copies(ins, outs, sems):
            cp.start()

    def finish(ins, outs, sems):
        for cp in copies(ins, outs, sems):
            cp.wait()

    return _Carry(sums, out_shape, {}, [pltpu.SemaphoreType.DMA((3 * M,)), pltpu.SemaphoreType.DMA((3 * M,))],
                  start, finish)


def _pair_share_carry(bufs):
    M = len(bufs)

    def start(ins, outs, sems):
        x, y, c = _me()
        for m in range(M):
            _remote(outs[m].at[c], outs[m].at[c], sems[0], sems[1], m, (x, y, 1 - c)).start()

    def finish(ins, outs, sems):
        x, y, c = _me()
        for m in range(M):
            _remote(outs[m].at[c], outs[m].at[c], sems[0], sems[1], m, (x, y, 1 - c)).wait_send()
            _remote(outs[m].at[1 - c], outs[m].at[1 - c], sems[0], sems[1], m, (x, y, 1 - c)).wait_recv()

    return _Carry(bufs, [jax.ShapeDtypeStruct(b.shape, b.dtype) for b in bufs], {m: m for m in range(M)},
                  [pltpu.SemaphoreType.DMA((M,)), pltpu.SemaphoreType.DMA((M,))], start, finish)


def _allreduce_small(v, name):
    R, W = v.shape
    Rh = R // 2

    def body(v_ref, o_ref, sib, quad, send_sems, recv_sems):
        x, y, c = _me()
        q = 2 * x + y
        sibling = (x, y, 1 - c)
        pair = _remote(v_ref, sib, send_sems, recv_sems, 0, sibling)
        pair.start()
        pair.wait()
        mine = pl.ds(pl.multiple_of(c * Rh, SUBLANES), Rh)
        quad[0] = v_ref[mine, :] + sib[mine, :]
        cps = []
        for k in (1, 2, 3):
            peer = (1 - x if k & 2 else x, 1 - y if k & 1 else y, c)
            cps.append(_remote(quad.at[0], quad.at[k], send_sems, recv_sems, k, peer))
            cps[-1].start()
        for cp in cps:
            cp.wait()
        acc = quad[q]
        for p in (1, 2, 3):
            acc = acc + quad[jnp.bitwise_xor(q, p)]
        o_ref[mine, :] = acc
        theirs = pl.ds(pl.multiple_of((1 - c) * Rh, SUBLANES), Rh)
        done = _remote(o_ref.at[mine, :], o_ref.at[mine, :], send_sems, recv_sems, 4, sibling)
        done.start()
        done.wait_send()
        _remote(o_ref.at[theirs, :], o_ref.at[theirs, :], send_sems, recv_sems, 4, sibling).wait_recv()

    vm = pl.BlockSpec(memory_space=pltpu.VMEM)
    return pl.pallas_call(
        body, name=name, in_specs=[vm], out_specs=vm, out_shape=jax.ShapeDtypeStruct((R, W), F32),
        scratch_shapes=[pltpu.VMEM((R, W), F32), pltpu.VMEM((4, Rh, W), F32),
                        pltpu.SemaphoreType.DMA((5,)), pltpu.SemaphoreType.DMA((5,))],
        compiler_params=pltpu.CompilerParams(vmem_limit_bytes=VMEM_LIMIT),
    )(v)


def _pack(pieces):
    flat = []
    for p in pieces:
        p = p.reshape(-1).astype(F32)
        pad = (-p.shape[0]) % PACK_ALIGN
        flat.append(jnp.pad(p, (0, pad)).reshape(-1, PACK_W))
    if sum(f.shape[0] for f in flat) % (2 * SUBLANES):
        flat.append(jnp.zeros((SUBLANES, PACK_W), F32))
    return jnp.concatenate(flat, axis=0)


def _unpack(packed, shapes):
    out, row = [], 0
    for shp in shapes:
        n = math.prod(shp)
        rows = -(-n // PACK_ALIGN) * SUBLANES
        out.append(packed[row:row + rows].reshape(-1)[:n].reshape(shp))
        row += rows
    return out


def kernel(x, ffn1_norm, ffn1_w_gate, ffn1_w_up, ffn1_w_down, mix_norm, w_in, lru_conv_w, lru_conv_b, lru_w_a, lru_b_a, lru_w_i, lru_b_i, lru_lambda, sc_conv_w, lru_out_norm, sc_out_norm, w_out, ffn2_norm, ffn2_w_gate, ffn2_w_up, ffn2_w_down, final_norm, loss_target, m_ffn1_norm, m_ffn1_w_gate, m_ffn1_w_up, m_ffn1_w_down, m_mix_norm, m_w_in, m_lru_conv_w, m_lru_conv_b, m_lru_w_a, m_lru_b_a, m_lru_w_i, m_lru_b_i, m_lru_lambda, m_sc_conv_w, m_lru_out_norm, m_sc_out_norm, m_w_out, m_ffn2_norm, m_ffn2_w_gate, m_ffn2_w_up, m_ffn2_w_down, m_final_norm, v_ffn1_norm, v_ffn1_w_gate, v_ffn1_w_up, v_ffn1_w_down, v_mix_norm, v_w_in, v_lru_conv_w, v_lru_conv_b, v_lru_w_a, v_lru_b_a, v_lru_w_i, v_lru_b_i, v_lru_lambda, v_sc_conv_w, v_lru_out_norm, v_sc_out_norm, v_w_out, v_ffn2_norm, v_ffn2_w_gate, v_ffn2_w_up, v_ffn2_w_down, v_final_norm):
    vals = locals()
    w = {n: vals[n] for n in WEIGHTS}
    mom = {n: vals["m_" + n] for n in WEIGHTS}
    var = {n: vals["v_" + n] for n in WEIGHTS}

    xi, yi, ci = _me()
    qi = 2 * xi + yi
    c_arr = jnp.reshape(ci, (1,)).astype(jnp.int32)
    qc_arr = jnp.stack([qi, ci]).astype(jnp.int32)

    T, D = x.shape[1], x.shape[2]
    xt = x.reshape(T, D)
    target = loss_target.reshape(T, D)
    DL = lru_conv_b.shape[-1]
    NH, HD = lru_w_a.shape[1], lru_w_a.shape[2]
    KL, KS = lru_conv_w.shape[1], sc_conv_w.shape[1]
    DLq = lru_conv_w.shape[2]

    axis_of = dict(zip(BIG, BIG_AXIS))
    first_names = ['ffn1_w_gate', 'ffn1_w_up', 'ffn1_w_down']
    later_names = ['w_in', 'w_out', 'ffn2_w_gate', 'ffn2_w_up', 'ffn2_w_down']

    def unview(n, g):
        return g.reshape(2 * g.shape[1], g.shape[2]) if axis_of[n] == 1 else g.reshape(8 * g.shape[1], g.shape[2])

    first_axes, later_axes = [axis_of[n] for n in first_names], [axis_of[n] for n in later_names]
    placed = _cast_into_full([w[n][0] for n in first_names], qc_arr, first_axes, "cast_ffn1_weights")
    res = _cast_into_full([w[n][0] for n in later_names], qc_arr, later_axes, "cast_later_weights",
                          _gather_two_way_carry(placed, first_axes))
    full = {n: unview(n, g) for n, g in zip(first_names, res[len(later_names):])}
    gather_later = _gather_carry(res[:len(later_names)], later_axes)

    taps = jnp.zeros((2 * SUBLANES, DL), F32)
    taps = lax.dynamic_update_slice(taps, lru_conv_w[0], (0, qi * DLq))
    taps = lax.dynamic_update_slice(taps, sc_conv_w[0], (KL, qi * DLq))
    taps = _allreduce_small(jnp.where(ci == 0, taps, 0.0), "gather_conv_taps")
    cw, sw = taps[0:KL], taps[KL:KL + KS]

    cb = lru_conv_b
    wa, wi = lru_w_a[0].astype(BF16), lru_w_i[0].astype(BF16)
    ba, bi = lru_b_a.reshape(1, DL), lru_b_i.reshape(1, DL)
    mix_args = (cw, cb, wa, ba, wi, bi, lru_lambda, sw, lru_out_norm, sc_out_norm)
    gf = final_norm.reshape(1, D)

    res = _ffn_fwd(xt, ffn1_norm, full['ffn1_w_gate'], full['ffn1_w_up'], full['ffn1_w_down'], "ffn1_fwd", gather_later)
    x1, n1, G1, U1 = res[:4]
    full.update({n: unview(n, g) for n, g in zip(later_names, res[4:])})
    n2, z = _norm_mm(x1, mix_norm, full['w_in'], "mix_in_proj")
    h, ymix = _mix_fwd(z, *mix_args, "mix_fwd")
    x2 = _mm_fullk(ymix, full['w_out'], False, x1, F32, "mix_out_proj")[0]
    x3, n3, G2, U2 = _ffn_fwd(x2, ffn2_norm, full['ffn2_w_gate'], full['ffn2_w_up'], full['ffn2_w_down'], "ffn2_fwd")
    dx3, d3b, sqerr, dgf = _loss_head(x3, gf, target, "loss_head")

    sums, halves, shared = {}, {}, {}

    def dw(n, a, b, name, carry=None):
        res = _mm_tn_pair(a, b, c_arr, axis_of[n], name, carry)
        sums[n] = res[0]
        return res[2:]

    def chip_carry(names):
        return _chip_exchange_carry([sums[n] for n in names], [axis_of[n] for n in names])

    def chip_add(names, recv):
        for n, r in zip(names, recv):
            halves[n] = _quad_add(sums[n], r, qc_arr, axis_of[n], "grad_chip_add_" + n)

    dG2, dU2, H2 = _ffn_bwd_hidden(d3b, G2, U2, full['ffn2_w_down'], "ffn2_bwd_hidden")
    dn3 = _ffn_bwd_input(dG2, dU2, full['ffn2_w_gate'], full['ffn2_w_up'], "ffn2_bwd_input")[0]
    dx2, dx2b, dg_ffn2 = _rms_bwd_res(dn3, x2, ffn2_norm, dx3, 1.0, "ffn2_norm_bwd")
    dw('ffn2_w_gate', n3, dG2, "ffn2_dwg")
    dw('ffn2_w_up', n3, dU2, "ffn2_dwu")
    dw('ffn2_w_down', H2, d3b, "ffn2_dwd")
    dy = _mm_fullk(dx2b, full['w_out'], True, None, BF16, "mix_out_bwd")[0]
    dw('w_out', ymix, dx2b, "mix_dwout")
    dz, dwa, dwi, vec = _mix_bwd(z, h, dy, *mix_args, "mix_bwd")
    dw('w_in', n2, dz, "mix_dwin")
    dg_mix, dx1, d1b = _mm_nt_norm_bwd(dz, full['w_in'], x1, mix_norm, dx2, FFN_RESIDUAL_SCALE, "mix_in_bwd")

    names_mix, names_ffn2 = ['w_out', 'w_in'], ['ffn2_w_gate', 'ffn2_w_up', 'ffn2_w_down']
    res = _ffn_bwd_hidden(d1b, G1, U1, full['ffn1_w_down'], "ffn1_bwd_hidden", chip_carry(names_mix))
    dG1, dU1, H1 = res[:3]
    chip_add(names_mix, res[3:])
    res = _ffn_bwd_input(dG1, dU1, full['ffn1_w_gate'], full['ffn1_w_up'], "ffn1_bwd_input", chip_carry(names_ffn2))
    dn1 = res[0]
    chip_add(names_ffn2, res[1:])
    dx0, _, dg_ffn1 = _rms_bwd_res(dn1, xt, ffn1_norm, dx1, 1.0, "ffn1_norm_bwd")
    early = names_ffn2 + names_mix
    shared.update(zip(early, dw('ffn1_w_gate', n1, dG1, "ffn1_dwg", _pair_share_carry([halves[n] for n in early]))))
    chip_add(['ffn1_w_gate'], dw('ffn1_w_up', n1, dU1, "ffn1_dwu", chip_carry(['ffn1_w_gate'])))
    chip_add(['ffn1_w_up'], dw('ffn1_w_down', H1, d1b, "ffn1_dwd", chip_carry(['ffn1_w_up'])))
    chip_add(['ffn1_w_down'], _run_carry(chip_carry(['ffn1_w_down']), "grad_chip_exchange_ffn1_down"))
    late = ['ffn1_w_gate', 'ffn1_w_up', 'ffn1_w_down']
    shared.update(zip(late, _run_carry(_pair_share_carry([halves[n] for n in late]), "grad_pair_share")))
    out_g, out_d, out_m, out_v = {}, {}, {}, {}
    for n in BIG:
        shp = w[n].shape
        g = shared[n].reshape(shp[1], shp[2]) if axis_of[n] == 1 else shared[n]
        outs = _adamw(w[n][0], g, mom[n][0], var[n][0], "adamw_" + n)
        out_d[n], out_m[n], out_v[n], out_g[n] = (a.reshape(shp) for a in outs)

    small = [n for n in WEIGHTS if n not in BIG]
    local_small = {
        'ffn1_norm': dg_ffn1, 'mix_norm': dg_mix, 'lru_conv_w': vec[V_CW:V_CW + KL], 'lru_conv_b': vec[V_CB],
        'lru_w_a': dwa, 'lru_b_a': vec[V_BA], 'lru_w_i': dwi, 'lru_b_i': vec[V_BI], 'lru_lambda': vec[V_LAM],
        'sc_conv_w': vec[V_SW:V_SW + KS], 'lru_out_norm': vec[V_GLO], 'sc_out_norm': vec[V_GSO],
        'ffn2_norm': dg_ffn2, 'final_norm': dgf,
    }
    full_shapes = [local_small[n].shape for n in small] + [(1,)]
    reduced = _allreduce_small(_pack([local_small[n] for n in small] + [sqerr[0, 0:1]]), "allreduce_small")
    reduced = _unpack(reduced, full_shapes)
    loss = (0.5 / D) * reduced[-1][0]
    gsm = {}
    for n, g in zip(small, reduced[:-1]):
        if n in SMALL_SHARDED:
            g = lax.dynamic_slice(g, (0, qi * DLq), (g.shape[0], DLq))
        gsm[n] = g.reshape(w[n].shape)
    small_shapes = [w[n].shape for n in small]
    d_s, m_s, v_s, _ = _adamw(_pack([w[n] for n in small]), _pack([gsm[n] for n in small]),
                              _pack([mom[n] for n in small]), _pack([var[n] for n in small]), "adamw_small")
    for n, d, mn, vn in zip(small, _unpack(d_s, small_shapes), _unpack(m_s, small_shapes), _unpack(v_s, small_shapes)):
        out_g[n], out_d[n], out_m[n], out_v[n] = gsm[n], d, mn, vn

    return (loss, dx0.reshape(x.shape), *[out_g[n] for n in WEIGHTS], *[out_d[n] for n in WEIGHTS],
            *[out_m[n] for n in WEIGHTS], *[out_v[n] for n in WEIGHTS])
```

```python
import math

import jax
import jax.numpy as jnp
from jax import lax
from jax.experimental import pallas as pl
from jax.experimental.pallas import tpu as pltpu

F32 = jnp.float32
BF16 = jnp.bfloat16
MESH = pl.DeviceIdType.MESH
ANY = pl.BlockSpec(memory_space=pl.ANY)

NORM_EPS = 1e-6
LRU_C = 8.0
FFN_RESIDUAL_SCALE = 0.5
ADAM_LR = 0.001
ADAM_B1 = 0.9
ADAM_B2 = 0.999
ADAM_EPS = 1e-08
ADAM_WD = 0.01
ADAM_STEP = 10

V7X_VMEM_BYTES = 64 * 2**20
VMEM_LIMIT = V7X_VMEM_BYTES - 8 * 2**20
LANES = 128
SUBLANES = 8
PACK_W = LANES
PACK_ALIGN = SUBLANES * PACK_W

WEIGHTS = ['ffn1_norm', 'ffn1_w_gate', 'ffn1_w_up', 'ffn1_w_down', 'mix_norm', 'w_in', 'lru_conv_w', 'lru_conv_b',
           'lru_w_a', 'lru_b_a', 'lru_w_i', 'lru_b_i', 'lru_lambda', 'sc_conv_w', 'lru_out_norm', 'sc_out_norm',
           'w_out', 'ffn2_norm', 'ffn2_w_gate', 'ffn2_w_up', 'ffn2_w_down', 'final_norm']
BIG = ['ffn1_w_gate', 'ffn1_w_up', 'ffn1_w_down', 'w_in', 'w_out', 'ffn2_w_gate', 'ffn2_w_up', 'ffn2_w_down']
BIG_AXIS = [1, 1, 0, 1, 0, 1, 1, 0]
SMALL_SHARDED = ['lru_conv_w', 'sc_conv_w']


def _tile(n, pref, mult):
    if n <= pref:
        return n
    t = (pref // mult) * mult
    while t >= mult:
        if n % t == 0:
            return t
        t -= mult
    return n


def _params(*sem):
    return pltpu.CompilerParams(dimension_semantics=sem, vmem_limit_bytes=VMEM_LIMIT)


def _me():
    return lax.axis_index("x"), lax.axis_index("y"), lax.axis_index("c")


def _sigmoid(v):
    return 1.0 / (1.0 + jnp.exp(-v))


def _rstd(v):
    return lax.rsqrt(jnp.mean(v * v, axis=-1, keepdims=True) + NORM_EPS)


def _rms_bwd(dy, v, gain):
    r = _rstd(v)
    w = gain * dy
    dv = r * w - v * (r * r * r) * jnp.mean(v * w, axis=-1, keepdims=True)
    dgain = jnp.sum(dy * v * r, axis=0, keepdims=True)
    return dv, dgain


def _dot_nt(a, b):
    return lax.dot_general(a, b, (((1,), (1,)), ((), ())), preferred_element_type=F32)


def _dot_tn(a, b):
    return lax.dot_general(a, b, (((0,), (0,)), ((), ())), preferred_element_type=F32)


class _Carry:
    def __init__(self, inputs, out_shape, aliases, sems, start, finish, middle=None, middle_at=0.85):
        self.inputs, self.out_shape, self.aliases, self.sems = list(inputs), list(out_shape), dict(aliases), list(sems)
        self.start, self.finish = start, finish
        self.middle, self.middle_at = middle, middle_at


def _merge_carries(a, b):
    ia, oa, sa = len(a.inputs), len(a.out_shape), len(a.sems)
    aliases = dict(a.aliases)
    aliases.update({ia + i: oa + j for i, j in b.aliases.items()})

    def both(which):
        def run(ins, outs, sems):
            getattr(a, which)(ins[:ia], outs[:oa], sems[:sa])
            getattr(b, which)(ins[ia:], outs[oa:], sems[sa:])
        return run

    return _Carry(a.inputs + b.inputs, a.out_shape + b.out_shape, aliases, a.sems + b.sems, both("start"), both("finish"))


def _call(body, name, grid, in_specs, out_specs, out_shape, scratch_shapes, semantics, args, carry=None, prefetch=()):
    np_ = len(prefetch)
    if carry is None:
        spec = pltpu.PrefetchScalarGridSpec(num_scalar_prefetch=np_, grid=grid, in_specs=in_specs, out_specs=out_specs,
                                            scratch_shapes=scratch_shapes)
        return pl.pallas_call(body, name=name, grid_spec=spec, out_shape=out_shape,
                              compiler_params=_params(*semantics))(*prefetch, *args)
    ni, no, ns = len(in_specs), len(out_specs), len(scratch_shapes)
    ci, co = len(carry.inputs), len(carry.out_shape)

    def carrying(*refs):
        pre, refs = refs[:np_], refs[np_:]
        ins, refs = refs[:ni], refs[ni:]
        cins, refs = refs[:ci], refs[ci:]
        outs, refs = refs[:no], refs[no:]
        couts, refs = refs[:co], refs[co:]
        scratch, csems = refs[:ns], refs[ns:]
        step = pl.program_id(0)
        for ax in range(1, len(grid)):
            step = step * grid[ax] + pl.program_id(ax)
        steps = math.prod(grid)
        first = step == 0
        last = step == steps - 1

        @pl.when(first)
        def _():
            carry.start(cins, couts, csems)

        if carry.middle is not None:
            @pl.when(step == min(int(carry.middle_at * steps), steps - 1))
            def _():
                carry.middle(cins, couts, csems)

        body(*pre, *ins, *outs, *scratch)

        @pl.when(last)
        def _():
            carry.finish(cins, couts, csems)

    spec = pltpu.PrefetchScalarGridSpec(
        num_scalar_prefetch=np_, grid=grid, in_specs=list(in_specs) + [ANY] * ci,
        out_specs=list(out_specs) + [ANY] * co, scratch_shapes=list(scratch_shapes) + carry.sems)
    return pl.pallas_call(
        carrying, name=name, grid_spec=spec, out_shape=list(out_shape) + carry.out_shape,
        input_output_aliases={np_ + ni + i: no + j for i, j in carry.aliases.items()},
        compiler_params=_params(*(["arbitrary"] * len(grid))),
    )(*prefetch, *args, *carry.inputs)


def _run_carry(carry, name):
    ci, co = len(carry.inputs), len(carry.out_shape)

    def body(*refs):
        cins, couts, csems = refs[:ci], refs[ci:ci + co], refs[ci + co:]
        carry.start(cins, couts, csems)
        if carry.middle is not None:
            carry.middle(cins, couts, csems)
        carry.finish(cins, couts, csems)

    return pl.pallas_call(body, name=name, in_specs=[ANY] * ci, out_specs=[ANY] * co, out_shape=carry.out_shape,
                          input_output_aliases=carry.aliases, scratch_shapes=carry.sems)(*carry.inputs)


def _ffn_fwd(x, gain, wg, wu, wd, name, carry=None):
    T, D = x.shape
    FF = wg.shape[1]
    tm = _tile(T, 512, 16)
    tf = _tile(FF, 512, LANES)
    nf = FF // tf

    def body(x_ref, g_ref, wg_ref, wu_ref, wd_ref, xo_ref, n_ref, G_ref, U_ref, acc_ref):
        f = pl.program_id(1)

        @pl.when(f == 0)
        def _():
            xv = x_ref[...]
            n_ref[...] = (xv * _rstd(xv) * g_ref[...]).astype(BF16)
            acc_ref[...] = jnp.zeros_like(acc_ref)

        n = n_ref[...]
        G = jnp.dot(n, wg_ref[...], preferred_element_type=F32)
        U = jnp.dot(n, wu_ref[...], preferred_element_type=F32)
        G_ref[...] = G.astype(BF16)
        U_ref[...] = U.astype(BF16)
        H = (G * _sigmoid(G) * U).astype(BF16)
        acc_ref[...] += jnp.dot(H, wd_ref[...], preferred_element_type=F32)

        @pl.when(f == nf - 1)
        def _():
            xo_ref[...] = x_ref[...] + FFN_RESIDUAL_SCALE * acc_ref[...]

    return _call(
        body, name, (T // tm, nf),
        [pl.BlockSpec((tm, D), lambda i, f: (i, 0)),
         pl.BlockSpec((1, D), lambda i, f: (0, 0)),
         pl.BlockSpec((D, tf), lambda i, f: (0, f)),
         pl.BlockSpec((D, tf), lambda i, f: (0, f)),
         pl.BlockSpec((tf, D), lambda i, f: (f, 0))],
        [pl.BlockSpec((tm, D), lambda i, f: (i, 0)),
         pl.BlockSpec((tm, D), lambda i, f: (i, 0)),
         pl.BlockSpec((tm, tf), lambda i, f: (i, f)),
         pl.BlockSpec((tm, tf), lambda i, f: (i, f))],
        [jax.ShapeDtypeStruct((T, D), F32), jax.ShapeDtypeStruct((T, D), BF16),
         jax.ShapeDtypeStruct((T, FF), BF16), jax.ShapeDtypeStruct((T, FF), BF16)],
        [pltpu.VMEM((tm, D), F32)], ("parallel", "arbitrary"), (x, gain, wg, wu, wd), carry)


def _ffn_bwd_act(db, G, U, wg, wu, wd, name, carry=None):
    T, D = db.shape
    FF = wg.shape[1]
    tm = _tile(T, 512, 16)
    tf = _tile(FF, 512, LANES)

    def body(d_ref, G_ref, U_ref, wg_ref, wu_ref, wd_ref, dG_ref, dU_ref, H_ref, dn_ref):
        f = pl.program_id(1)
        dH = _dot_nt(d_ref[...], wd_ref[...])
        Gv = G_ref[...].astype(F32)
        Uv = U_ref[...].astype(F32)
        s = _sigmoid(Gv)
        sg = Gv * s
        H_ref[...] = (sg * Uv).astype(BF16)
        dU = (dH * sg).astype(BF16)
        dG = (dH * Uv * (s * (1.0 + Gv * (1.0 - s)))).astype(BF16)
        dG_ref[...] = dG
        dU_ref[...] = dU
        contrib = _dot_nt(dG, wg_ref[...]) + _dot_nt(dU, wu_ref[...])

        @pl.when(f == 0)
        def _():
            dn_ref[...] = contrib

        @pl.when(f > 0)
        def _():
            dn_ref[...] += contrib

    return _call(
        body, name, (T // tm, FF // tf),
        [pl.BlockSpec((tm, D), lambda i, f: (i, 0)),
         pl.BlockSpec((tm, tf), lambda i, f: (i, f)),
         pl.BlockSpec((tm, tf), lambda i, f: (i, f)),
         pl.BlockSpec((D, tf), lambda i, f: (0, f)),
         pl.BlockSpec((D, tf), lambda i, f: (0, f)),
         pl.BlockSpec((tf, D), lambda i, f: (f, 0))],
        [pl.BlockSpec((tm, tf), lambda i, f: (i, f)),
         pl.BlockSpec((tm, tf), lambda i, f: (i, f)),
         pl.BlockSpec((tm, tf), lambda i, f: (i, f)),
         pl.BlockSpec((tm, D), lambda i, f: (i, 0))],
        [jax.ShapeDtypeStruct((T, FF), BF16), jax.ShapeDtypeStruct((T, FF), BF16),
         jax.ShapeDtypeStruct((T, FF), BF16), jax.ShapeDtypeStruct((T, D), F32)],
        [], ("parallel", "arbitrary"), (db, G, U, wg, wu, wd), carry)


def _ffn_bwd_hidden(db, G, U, wd, name, carry=None):
    T, D = db.shape
    FF = wd.shape[0]
    tm = _tile(T, 1024, 16)
    tf = _tile(FF, 512, LANES)

    def body(d_ref, G_ref, U_ref, wd_ref, dG_ref, dU_ref, H_ref):
        dH = _dot_nt(d_ref[...], wd_ref[...])
        Gv = G_ref[...].astype(F32)
        Uv = U_ref[...].astype(F32)
        s = _sigmoid(Gv)
        sg = Gv * s
        H_ref[...] = (sg * Uv).astype(BF16)
        dU_ref[...] = (dH * sg).astype(BF16)
        dG_ref[...] = (dH * Uv * (s * (1.0 + Gv * (1.0 - s)))).astype(BF16)

    act = pl.BlockSpec((tm, tf), lambda i, f: (i, f))
    return _call(
        body, name, (T // tm, FF // tf),
        [pl.BlockSpec((tm, D), lambda i, f: (i, 0)), act, act, pl.BlockSpec((tf, D), lambda i, f: (f, 0))],
        [act, act, act], [jax.ShapeDtypeStruct((T, FF), BF16)] * 3,
        [], ("parallel", "arbitrary"), (db, G, U, wd), carry)


def _ffn_bwd_input(dG, dU, wg, wu, name, carry=None):
    T, FF = dG.shape
    D = wg.shape[0]
    tm = _tile(T, 512, 16)
    tn = _tile(D, 512, LANES)

    def body(dG_ref, dU_ref, wg_ref, wu_ref, dn_ref):
        dn_ref[...] = _dot_nt(dG_ref[...], wg_ref[...]) + _dot_nt(dU_ref[...], wu_ref[...])

    act = pl.BlockSpec((tm, FF), lambda i, j: (i, 0))
    wt = pl.BlockSpec((tn, FF), lambda i, j: (j, 0))
    return _call(body, name, (T // tm, D // tn), [act, act, wt, wt], [pl.BlockSpec((tm, tn), lambda i, j: (i, j))],
                 [jax.ShapeDtypeStruct((T, D), F32)], [], ("parallel", "arbitrary"), (dG, dU, wg, wu), carry)


TAIL_ROWS = 128


class _NormBwdTail:
    def __init__(self, T, D, tm, scale):
        self.T, self.D, self.tm, self.scale = T, D, tm, scale
        self.ni = T // tm
        self.scratch = [pltpu.VMEM((tm, D), F32), pltpu.VMEM((tm, D), F32), pltpu.VMEM((tm, D), F32),
                        pltpu.VMEM((tm, D), BF16), pltpu.SemaphoreType.DMA((4,))]
        self.out_shape = [jax.ShapeDtypeStruct((T, D), F32), jax.ShapeDtypeStruct((T, D), BF16)]

    def _rows(self, k):
        return pl.ds(pl.multiple_of(k * self.tm, self.tm), self.tm)

    def _loads(self, k, x_hbm, r_hbm, bufs):
        xbuf, rbuf, _, _, sems = bufs
        return [pltpu.make_async_copy(x_hbm.at[self._rows(k)], xbuf, sems.at[0]),
                pltpu.make_async_copy(r_hbm.at[self._rows(k)], rbuf, sems.at[1])]

    def _stores(self, k, dx_hbm, dxb_hbm, bufs):
        _, _, obuf, obbuf, sems = bufs
        return [pltpu.make_async_copy(obuf, dx_hbm.at[self._rows(k)], sems.at[2]),
                pltpu.make_async_copy(obbuf, dxb_hbm.at[self._rows(k)], sems.at[3])]

    def prefetch(self, i, x_hbm, r_hbm, bufs):
        for cp in self._loads(i, x_hbm, r_hbm, bufs):
            cp.start()

    def run(self, i, acc_ref, g_ref, x_hbm, r_hbm, dx_hbm, dxb_hbm, dg_ref, bufs):
        xbuf, rbuf, obuf, obbuf, _ = bufs
        for cp in self._loads(i, x_hbm, r_hbm, bufs):
            cp.wait()

        @pl.when(i > 0)
        def _():
            for cp in self._stores(i - 1, dx_hbm, dxb_hbm, bufs):
                cp.wait()

        dgain = None
        for r0 in range(0, self.tm, TAIL_ROWS):
            rs = slice(r0, min(r0 + TAIL_ROWS, self.tm))
            dv, dgr = _rms_bwd(acc_ref[rs, :], xbuf[rs, :], g_ref[...])
            dx = rbuf[rs, :] + dv
            obuf[rs, :] = dx
            obbuf[rs, :] = (self.scale * dx).astype(BF16)
            dgain = dgr if dgain is None else dgain + dgr
        for cp in self._stores(i, dx_hbm, dxb_hbm, bufs):
            cp.start()

        @pl.when(i == 0)
        def _():
            dg_ref[...] = dgain

        @pl.when(i > 0)
        def _():
            dg_ref[...] += dgain

        @pl.when(i == self.ni - 1)
        def _():
            for cp in self._stores(i, dx_hbm, dxb_hbm, bufs):
                cp.wait()


def _ffn_bwd_fused(db, G, U, wg, wu, wd, x_in, gain, dres, scale, name):
    T, D = db.shape
    FF = wg.shape[1]
    tm = _tile(T, 512, 16)
    tf = _tile(FF, 512, LANES)
    nf = FF // tf
    tail = _NormBwdTail(T, D, tm, scale)

    def body(d_ref, G_ref, U_ref, wg_ref, wu_ref, wd_ref, g_ref, x_hbm, r_hbm,
             dG_ref, dU_ref, H_ref, dg_ref, dx_hbm, dxb_hbm, acc_ref, *bufs):
        i, f = pl.program_id(0), pl.program_id(1)

        @pl.when(f == 0)
        def _():
            tail.prefetch(i, x_hbm, r_hbm, bufs)

        dH = _dot_nt(d_ref[...], wd_ref[...])
        Gv = G_ref[...].astype(F32)
        Uv = U_ref[...].astype(F32)
        s = _sigmoid(Gv)
        sg = Gv * s
        H_ref[...] = (sg * Uv).astype(BF16)
        dU = (dH * sg).astype(BF16)
        dG = (dH * Uv * (s * (1.0 + Gv * (1.0 - s)))).astype(BF16)
        dG_ref[...] = dG
        dU_ref[...] = dU
        contrib = _dot_nt(dG, wg_ref[...]) + _dot_nt(dU, wu_ref[...])

        @pl.when(f == 0)
        def _():
            acc_ref[...] = contrib

        @pl.when(f > 0)
        def _():
            acc_ref[...] += contrib

        @pl.when(f == nf - 1)
        def _():
            tail.run(i, acc_ref, g_ref, x_hbm, r_hbm, dx_hbm, dxb_hbm, dg_ref, bufs)

    act = pl.BlockSpec((tm, tf), lambda i, f: (i, f))
    return pl.pallas_call(
        body, name=name, grid=(T // tm, nf),
        in_specs=[pl.BlockSpec((tm, D), lambda i, f: (i, 0)), act, act,
                  pl.BlockSpec((D, tf), lambda i, f: (0, f)),
                  pl.BlockSpec((D, tf), lambda i, f: (0, f)),
                  pl.BlockSpec((tf, D), lambda i, f: (f, 0)),
                  pl.BlockSpec((1, D), lambda i, f: (0, 0)), ANY, ANY],
        out_specs=[act, act, act, pl.BlockSpec((1, D), lambda i, f: (0, 0)), ANY, ANY],
        out_shape=[jax.ShapeDtypeStruct((T, FF), BF16)] * 3 + [jax.ShapeDtypeStruct((1, D), F32)] + tail.out_shape,
        scratch_shapes=[pltpu.VMEM((tm, D), F32)] + tail.scratch,
        compiler_params=_params("arbitrary", "arbitrary"),
    )(db, G, U, wg, wu, wd, gain, x_in, dres)


def _mm_nt_norm_bwd(a, w, x_in, gain, dres, scale, name, carry=None):
    T, K = a.shape
    D = w.shape[0]
    tm = _tile(T, 512, 16)
    tk = _tile(K, 1280, LANES)
    nk = K // tk
    tail = _NormBwdTail(T, D, tm, scale)

    def body(a_ref, w_ref, g_ref, x_hbm, r_hbm, dg_ref, dx_hbm, dxb_hbm, acc_ref, *bufs):
        i, k = pl.program_id(0), pl.program_id(1)

        @pl.when(k == 0)
        def _():
            tail.prefetch(i, x_hbm, r_hbm, bufs)

        contrib = _dot_nt(a_ref[...], w_ref[...])

        @pl.when(k == 0)
        def _():
            acc_ref[...] = contrib

        @pl.when(k > 0)
        def _():
            acc_ref[...] += contrib

        @pl.when(k == nk - 1)
        def _():
            tail.run(i, acc_ref, g_ref, x_hbm, r_hbm, dx_hbm, dxb_hbm, dg_ref, bufs)

    return _call(
        body, name, (T // tm, nk),
        [pl.BlockSpec((tm, tk), lambda i, k: (i, k)), pl.BlockSpec((D, tk), lambda i, k: (0, k)),
         pl.BlockSpec((1, D), lambda i, k: (0, 0)), ANY, ANY],
        [pl.BlockSpec((1, D), lambda i, k: (0, 0)), ANY, ANY],
        [jax.ShapeDtypeStruct((1, D), F32)] + tail.out_shape,
        [pltpu.VMEM((tm, D), F32)] + tail.scratch, ("arbitrary", "arbitrary"), (a, w, gain, x_in, dres), carry)


def _rms_bwd_res(dn, x, gain, dres, scale, name, carry=None):
    T, D = x.shape
    tm = _tile(T, 256, 16)

    def body(dn_ref, x_ref, g_ref, dr_ref, dx_ref, dxb_ref, dg_ref):
        i = pl.program_id(0)
        dv, dgain = _rms_bwd(dn_ref[...], x_ref[...], g_ref[...])
        dx = dr_ref[...] + dv
        dx_ref[...] = dx
        dxb_ref[...] = (scale * dx).astype(BF16)

        @pl.when(i == 0)
        def _():
            dg_ref[...] = dgain

        @pl.when(i > 0)
        def _():
            dg_ref[...] += dgain

    row = pl.BlockSpec((tm, D), lambda i: (i, 0))
    vec = pl.BlockSpec((1, D), lambda i: (0, 0))
    return _call(
        body, name, (T // tm,), [row, row, vec, row], [row, row, vec],
        [jax.ShapeDtypeStruct((T, D), F32), jax.ShapeDtypeStruct((T, D), BF16), jax.ShapeDtypeStruct((1, D), F32)],
        [], ("arbitrary",), (dn, x, gain, dres), carry)


def _loss_head(x3, gain, target, name):
    T, D = x3.shape
    tm = _tile(T, 256, 16)

    def body(x_ref, g_ref, t_ref, dx_ref, dxb_ref, ls_ref, dg_ref):
        i = pl.program_id(0)
        xv = x_ref[...]
        err = xv * _rstd(xv) * g_ref[...] - t_ref[...]
        sq = jnp.sum(jnp.sum(err * err, axis=1, keepdims=True), axis=0, keepdims=True)
        dv, dgain = _rms_bwd(err * (1.0 / D), xv, g_ref[...])
        dx_ref[...] = dv
        dxb_ref[...] = (FFN_RESIDUAL_SCALE * dv).astype(BF16)
        sqb = jnp.broadcast_to(sq, (1, LANES))

        @pl.when(i == 0)
        def _():
            dg_ref[...] = dgain
            ls_ref[...] = sqb

        @pl.when(i > 0)
        def _():
            dg_ref[...] += dgain
            ls_ref[...] += sqb

    row = pl.BlockSpec((tm, D), lambda i: (i, 0))
    vec = pl.BlockSpec((1, D), lambda i: (0, 0))
    return pl.pallas_call(
        body, name=name, grid=(T // tm,),
        in_specs=[row, vec, row],
        out_specs=[row, row, pl.BlockSpec((1, LANES), lambda i: (0, 0)), vec],
        out_shape=[jax.ShapeDtypeStruct((T, D), F32), jax.ShapeDtypeStruct((T, D), BF16),
                   jax.ShapeDtypeStruct((1, LANES), F32), jax.ShapeDtypeStruct((1, D), F32)],
        compiler_params=_params("arbitrary"),
    )(x3, gain, target)


def _norm_mm(x, gain, w, name):
    T, D = x.shape
    N = w.shape[1]
    tm = _tile(T, 512, 16)
    tn = _tile(N, 2560, LANES)

    def body(x_ref, g_ref, w_ref, n_ref, z_ref):
        @pl.when(pl.program_id(1) == 0)
        def _():
            xv = x_ref[...]
            n_ref[...] = (xv * _rstd(xv) * g_ref[...]).astype(BF16)

        z_ref[...] = jnp.dot(n_ref[...], w_ref[...], preferred_element_type=F32).astype(BF16)

    return pl.pallas_call(
        body, name=name, grid=(T // tm, N // tn),
        in_specs=[pl.BlockSpec((tm, D), lambda i, j: (i, 0)),
                  pl.BlockSpec((1, D), lambda i, j: (0, 0)),
                  pl.BlockSpec((D, tn), lambda i, j: (0, j))],
        out_specs=[pl.BlockSpec((tm, D), lambda i, j: (i, 0)),
                   pl.BlockSpec((tm, tn), lambda i, j: (i, j))],
        out_shape=[jax.ShapeDtypeStruct((T, D), BF16), jax.ShapeDtypeStruct((T, N), BF16)],
        compiler_params=_params("parallel", "arbitrary"),
    )(x, gain, w)


def _mm_fullk(a, w, trans_w, residual, out_dtype, name, carry=None):
    T, K = a.shape
    N = w.shape[0] if trans_w else w.shape[1]
    tm = _tile(T, 512, 16)
    tn = _tile(N, 2048 * 2560 // K, LANES)

    def body(*refs):
        if residual is None:
            a_ref, w_ref, o_ref = refs
        else:
            a_ref, w_ref, r_ref, o_ref = refs
        if trans_w:
            acc = _dot_nt(a_ref[...], w_ref[...])
        else:
            acc = jnp.dot(a_ref[...], w_ref[...], preferred_element_type=F32)
        if residual is not None:
            acc = acc + r_ref[...]
        o_ref[...] = acc.astype(out_dtype)

    w_spec = pl.BlockSpec((tn, K), lambda i, j: (j, 0)) if trans_w else pl.BlockSpec((K, tn), lambda i, j: (0, j))
    in_specs = [pl.BlockSpec((tm, K), lambda i, j: (i, 0)), w_spec]
    args = [a, w]
    if residual is not None:
        in_specs.append(pl.BlockSpec((tm, tn), lambda i, j: (i, j)))
        args.append(residual)
    return _call(body, name, (T // tm, N // tn), in_specs, [pl.BlockSpec((tm, tn), lambda i, j: (i, j))],
                 [jax.ShapeDtypeStruct((T, N), out_dtype)], [], ("parallel", "arbitrary"), args, carry)


def _mm_tn(a, b, name, carry=None):
    T, M = a.shape
    N = b.shape[1]
    tmw = _tile(M, 2048, LANES)
    tnw = _tile(N, 2048 * 1408 // tmw, LANES)
    tk = _tile(T, 1024, 16)
    nk = T // tk

    def body(a_ref, b_ref, o_ref, acc_ref):
        k = pl.program_id(2)

        @pl.when(k == 0)
        def _():
            acc_ref[...] = jnp.zeros_like(acc_ref)

        acc_ref[...] += _dot_tn(a_ref[...], b_ref[...])

        @pl.when(k == nk - 1)
        def _():
            o_ref[...] = acc_ref[...].astype(BF16)

    return _call(
        body, name, (M // tmw, N // tnw, nk),
        [pl.BlockSpec((tk, tmw), lambda i, j, k: (k, i)),
         pl.BlockSpec((tk, tnw), lambda i, j, k: (k, j))],
        [pl.BlockSpec((tmw, tnw), lambda i, j, k: (i, j))],
        [jax.ShapeDtypeStruct((M, N), BF16)],
        [pltpu.VMEM((tmw, tnw), F32)], ("parallel", "parallel", "arbitrary"), (a, b), carry)


def _mm_tn_pair(a, b, c_arr, axis, name, carry=None):
    T, M = a.shape
    N = b.shape[1]
    tk = _tile(T, 2048, 16)
    nk = T // tk
    nq = 4
    if axis == 1:
        rows, cols = M // 2, N // nq
        a_spec = pl.BlockSpec((tk, rows), lambda p, j, k, cr: (k, jnp.where(p == 0, 1 - cr[0], cr[0])))
        b_spec = pl.BlockSpec((tk, cols), lambda p, j, k, cr: (k, j))
        s_shape, land_shape = (1, rows, N), (rows, N)
        s_spec = pl.BlockSpec((None, rows, cols), lambda p, j, k, cr: (0, 0, j * p))
    else:
        rows, cols = M // nq, N // 2
        a_spec = pl.BlockSpec((tk, rows), lambda p, j, k, cr: (k, j))
        b_spec = pl.BlockSpec((tk, cols), lambda p, j, k, cr: (k, jnp.where(p == 0, 1 - cr[0], cr[0])))
        s_shape, land_shape = (nq, rows, cols), (nq, rows, cols)
        s_spec = pl.BlockSpec((None, rows, cols), lambda p, j, k, cr: (j * p, 0, 0))

    def body(c_ref, a_ref, b_ref, s_ref, land, acc_ref, stage, got, send_sems, recv_sems, loc_sem):
        p, j, k = pl.program_id(0), pl.program_id(1), pl.program_id(2)
        x, y, c = _me()

        def tile(jj):
            return land.at[:, pl.ds(jj * cols, cols)] if axis == 1 else land.at[jj]

        def send(jj):
            return _remote(stage, tile(jj), send_sems, recv_sems, jj, (x, y, 1 - c))

        @pl.when(k == 0)
        def _():
            acc_ref[...] = jnp.zeros_like(acc_ref)

        acc_ref[...] += _dot_tn(a_ref[...], b_ref[...])

        def fetch(jj):
            return pltpu.make_async_copy(tile(jj), got, loc_sem.at[0])

        for jj in range(nq):
            @pl.when(jnp.logical_and(k == nk - 1, jnp.logical_and(p == 0, j == jj)))
            def _():
                if jj > 0:
                    send(jj - 1).wait_send()
                stage[...] = acc_ref[...].astype(BF16)
                send(jj).start()

            @pl.when(jnp.logical_and(k == max(nk - 2, 0), jnp.logical_and(p == 1, j == jj)))
            def _():
                if jj == 0:
                    send(nq - 1).wait_send()
                send(jj).wait_recv()
                fetch(jj).start()

            @pl.when(jnp.logical_and(k == nk - 1, jnp.logical_and(p == 1, j == jj)))
            def _():
                fetch(jj).wait()
                s_ref[...] = (acc_ref[...] + got[...].astype(F32)).astype(BF16)

    return _call(
        body, name, (2, nq, nk), [a_spec, b_spec], [s_spec, ANY],
        [jax.ShapeDtypeStruct(s_shape, BF16), jax.ShapeDtypeStruct(land_shape, BF16)],
        [pltpu.VMEM((rows, cols), F32), pltpu.VMEM((rows, cols), BF16), pltpu.VMEM((rows, cols), BF16),
         pltpu.SemaphoreType.DMA((nq,)), pltpu.SemaphoreType.DMA((nq,)), pltpu.SemaphoreType.DMA((1,))],
        ("arbitrary", "arbitrary", "arbitrary"), (a, b), carry, (c_arr,))


GELU_K = math.sqrt(2.0 / math.pi)
GELU_C = 0.044715


def _gelu_and_grad(v):
    u = GELU_K * (v + GELU_C * v * v * v)
    th = jnp.tanh(u)
    g = 0.5 * v * (1.0 + th)
    dg = 0.5 * (1.0 + th) + 0.5 * v * (1.0 - th * th) * GELU_K * (1.0 + 3.0 * GELU_C * v * v)
    return g, dg


def _neg_expm1(v):
    poly = v * (1.0 + v * (0.5 + v * (1.0 / 6 + v * (1.0 / 24 + v * (1.0 / 120 + v * (1.0 / 720))))))
    return jnp.where(v > -0.25, -poly, 1.0 - jnp.exp(v))


def _softplus_neg(lam):
    e = jnp.exp(-jnp.abs(lam))
    log1pe = jnp.where(e < 1e-4, e * (1.0 - 0.5 * e), jnp.log(1.0 + e))
    sp = jnp.maximum(-lam, 0.0) + log1pe
    dsp = -1.0 / (1.0 + jnp.exp(lam))
    return sp, dsp


def _earlier(ext, j):
    return pltpu.roll(ext, j, 0)[SUBLANES:, :]


def _later(ext, j):
    n = ext.shape[0]
    return pltpu.roll(ext, n - j, 0)[:n - SUBLANES, :]


def _taps(v, halo, K):
    ext = jnp.concatenate([halo, v], axis=0)
    return [v] + [_earlier(ext, j) for j in range(1, K)]


def _block_diag(vb, w_ref, nh, hd):
    return jnp.concatenate(
        [jnp.dot(vb[:, h * hd:(h + 1) * hd], w_ref[h], preferred_element_type=F32) for h in range(nh)], axis=1)


def _lru_gates(xc, wa_ref, ba_ref, wi_ref, bi_ref, sp, nh, hd):
    xcb = xc.astype(BF16)
    r = _sigmoid(_block_diag(xcb, wa_ref, nh, hd) + ba_ref[...])
    ig = _sigmoid(_block_diag(xcb, wi_ref, nh, hd) + bi_ref[...])
    log_a = -LRU_C * r * sp
    a = jnp.exp(log_a)
    mult = jnp.sqrt(_neg_expm1(2.0 * log_a))
    return xcb, r, ig, a, mult


def _mix_fwd(z, cw, cb, wa, ba, wi, bi, lam, sw, glo, gso, name):
    T = z.shape[0]
    DL = cb.shape[1]
    DS = gso.shape[1]
    NH, HD = wa.shape[0], wa.shape[1]
    KL, KS = cw.shape[0], sw.shape[0]
    tt = _tile(T, 128, 16)
    o_g, o_b, o_c, o_x = DL, 2 * DL, 2 * DL + DS, 2 * DL + 2 * DS

    def body(z_ref, cw_ref, cb_ref, wa_ref, ba_ref, wi_ref, bi_ref, lam_ref, sw_ref, glo_ref, gso_ref,
             h_ref, y_ref, cx_ref, cp_ref, ch_ref):
        @pl.when(pl.program_id(0) == 0)
        def _():
            cx_ref[...] = jnp.zeros_like(cx_ref)
            cp_ref[...] = jnp.zeros_like(cp_ref)
            ch_ref[...] = jnp.zeros_like(ch_ref)

        def zcol(o, n):
            return z_ref[:, o:o + n].astype(F32)

        lx = zcol(0, DL)
        xs = _taps(lx, cx_ref[...], KL)
        cx_ref[...] = lx[tt - SUBLANES:, :]
        xc = cb_ref[...] + xs[0] * cw_ref[KL - 1:KL, :]
        for j in range(1, KL):
            xc = xc + xs[j] * cw_ref[KL - 1 - j:KL - j, :]
        sp, _ = _softplus_neg(lam_ref[...])
        _, _, ig, a, mult = _lru_gates(xc, wa_ref, ba_ref, wi_ref, bi_ref, sp, NH, HD)
        b = mult * (ig * xc)
        rows = lax.broadcasted_iota(jnp.int32, (tt, DL), 0)
        s = 1
        while s < tt:
            keep = rows >= s
            b = jnp.where(keep, a * pltpu.roll(b, s, 0) + b, b)
            a = jnp.where(keep, a * pltpu.roll(a, s, 0), a)
            s *= 2
        h = a * ch_ref[SUBLANES - 1:SUBLANES, :] + b
        ch_ref[...] = h[tt - SUBLANES:, :]
        h_ref[...] = h
        ge, _ = _gelu_and_grad(zcol(o_g, DL))
        ylru = h * ge
        y_ref[:, 0:DL] = (ylru * _rstd(ylru) * glo_ref[...]).astype(BF16)

        p = zcol(o_c, DS) * zcol(o_x, DS)
        ps = _taps(p, cp_ref[...], KS)
        cp_ref[...] = p[tt - SUBLANES:, :]
        cv = ps[0] * sw_ref[KS - 1:KS, :]
        for j in range(1, KS):
            cv = cv + ps[j] * sw_ref[KS - 1 - j:KS - j, :]
        ysc = zcol(o_b, DS) * cv
        y_ref[:, DL:DL + DS] = (ysc * _rstd(ysc) * gso_ref[...]).astype(BF16)

    def full(shape):
        return pl.BlockSpec(shape, lambda t: (0,) * len(shape))

    return pl.pallas_call(
        body, name=name, grid=(T // tt,),
        in_specs=[pl.BlockSpec((tt, z.shape[1]), lambda t: (t, 0)),
                  full(cw.shape), full(cb.shape), full(wa.shape), full(ba.shape), full(wi.shape), full(bi.shape),
                  full(lam.shape), full(sw.shape), full(glo.shape), full(gso.shape)],
        out_specs=[pl.BlockSpec((tt, DL), lambda t: (t, 0)), pl.BlockSpec((tt, DL + DS), lambda t: (t, 0))],
        out_shape=[jax.ShapeDtypeStruct((T, DL), F32), jax.ShapeDtypeStruct((T, DL + DS), BF16)],
        scratch_shapes=[pltpu.VMEM((SUBLANES, DL), F32), pltpu.VMEM((SUBLANES, DS), F32),
                        pltpu.VMEM((SUBLANES, DL), F32)],
        compiler_params=_params("arbitrary"),
    )(z, cw, cb, wa, ba, wi, bi, lam, sw, glo, gso)


V_BA, V_BI, V_LAM, V_CB, V_CW, V_SW, V_GLO, V_GSO, V_ROWS = 0, 1, 2, 3, 4, 8, 11, 12, 16


def _mix_bwd(z, h, dy, cw, cb, wa, ba, wi, bi, lam, sw, glo, gso, name):
    T = z.shape[0]
    DL = cb.shape[1]
    DS = gso.shape[1]
    NH, HD = wa.shape[0], wa.shape[1]
    KL, KS = cw.shape[0], sw.shape[0]
    tt = _tile(T, 64, 16)
    nt = T // tt
    ZH = 2 * SUBLANES
    o_g, o_b, o_c, o_x = DL, 2 * DL, 2 * DL + DS, 2 * DL + 2 * DS

    def body(z_ref, zh_ref, h_ref, hh_ref, dy_ref, cw_ref, cb_ref, wa_ref, ba_ref, wi_ref, bi_ref, lam_ref,
             sw_ref, glo_ref, gso_ref, dz_ref, dwa_ref, dwi_ref, vec_ref, cdx_ref, cdc_ref, cdh_ref):
        i = pl.program_id(0)
        tr = nt - 1 - i

        @pl.when(i == 0)
        def _():
            dwa_ref[...] = jnp.zeros_like(dwa_ref)
            dwi_ref[...] = jnp.zeros_like(dwi_ref)
            vec_ref[...] = jnp.zeros_like(vec_ref)
            cdx_ref[...] = jnp.zeros_like(cdx_ref)
            cdc_ref[...] = jnp.zeros_like(cdc_ref)
            cdh_ref[...] = jnp.zeros_like(cdh_ref)

        def acc_row(r, v):
            vec_ref[pl.ds(r, 1), :] += jnp.sum(v, axis=0, keepdims=True)

        has_prev = tr > 0
        rows = lax.broadcasted_iota(jnp.int32, (tt, DL), 0)

        def zcol(o, n):
            return z_ref[:, o:o + n].astype(F32)

        def zhalo(o, n):
            return jnp.where(has_prev, zh_ref[:, o:o + n].astype(F32)[SUBLANES:, :], 0.0)

        lx = zcol(0, DL)
        xs = _taps(lx, zhalo(0, DL), KL)
        xc = cb_ref[...] + xs[0] * cw_ref[KL - 1:KL, :]
        for j in range(1, KL):
            xc = xc + xs[j] * cw_ref[KL - 1 - j:KL - j, :]
        sp, dsp = _softplus_neg(lam_ref[...])
        xcb, r, ig, a, mult = _lru_gates(xc, wa_ref, ba_ref, wi_ref, bi_ref, sp, NH, HD)
        hv = h_ref[...]
        hprev = _earlier(jnp.concatenate([jnp.where(has_prev, hh_ref[...], 0.0), hv], axis=0), 1)
        gate = zcol(o_g, DL)
        ge, dge = _gelu_and_grad(gate)
        ylru = hv * ge

        d_ylru, dglo = _rms_bwd(dy_ref[:, 0:DL].astype(F32), ylru, glo_ref[...])
        vec_ref[pl.ds(V_GLO, 1), :] += dglo
        dz_ref[:, o_g:o_g + DL] = (d_ylru * hv * dge).astype(BF16)
        bq = d_ylru * ge
        aq = jnp.where(rows == tt - 1, 1.0, pltpu.roll(a, tt - 1, 0))
        s = 1
        while s < tt:
            keep = rows < tt - s
            bq = jnp.where(keep, aq * pltpu.roll(bq, tt - s, 0) + bq, bq)
            aq = jnp.where(keep, aq * pltpu.roll(aq, tt - s, 0), aq)
            s *= 2
        dhh = bq + aq * cdh_ref[0:1, :]
        cdh_ref[0:1, :] = a[0:1, :] * dhh[0:1, :]

        da = dhh * hprev
        dmult = dhh * (ig * xc)
        d_i = dhh * mult * xc
        dxc = dhh * mult * ig
        dlog = da * a - dmult * (a * a) / mult
        acc_row(V_LAM, dlog * (-LRU_C * r) * dsp)
        dpa = dlog * (-LRU_C * sp) * r * (1.0 - r)
        dpi = d_i * ig * (1.0 - ig)
        acc_row(V_BA, dpa)
        acc_row(V_BI, dpi)
        dpab = dpa.astype(BF16)
        dpib = dpi.astype(BF16)
        back = []
        for hh in range(NH):
            sl = slice(hh * HD, (hh + 1) * HD)
            dwa_ref[hh] += _dot_tn(xcb[:, sl], dpab[:, sl])
            dwi_ref[hh] += _dot_tn(xcb[:, sl], dpib[:, sl])
            back.append(_dot_nt(dpab[:, sl], wa_ref[hh]) + _dot_nt(dpib[:, sl], wi_ref[hh]))
        dxc = dxc + jnp.concatenate(back, axis=1)

        acc_row(V_CB, dxc)
        extd = jnp.concatenate([dxc, cdx_ref[...]], axis=0)
        cdx_ref[...] = dxc[0:SUBLANES, :]
        dlx = dxc * cw_ref[KL - 1:KL, :]
        acc_row(V_CW + KL - 1, dxc * xs[0])
        for j in range(1, KL):
            dlx = dlx + _later(extd, j) * cw_ref[KL - 1 - j:KL - j, :]
            acc_row(V_CW + KL - 1 - j, dxc * xs[j])
        dz_ref[:, 0:DL] = dlx.astype(BF16)

        sb = zcol(o_b, DS)
        sc = zcol(o_c, DS)
        sx = zcol(o_x, DS)
        p = sc * sx
        ps = _taps(p, zhalo(o_c, DS) * zhalo(o_x, DS), KS)
        cv = ps[0] * sw_ref[KS - 1:KS, :]
        for j in range(1, KS):
            cv = cv + ps[j] * sw_ref[KS - 1 - j:KS - j, :]
        d_ysc, dgso = _rms_bwd(dy_ref[:, DL:DL + DS].astype(F32), sb * cv, gso_ref[...])
        vec_ref[pl.ds(V_GSO, 1), :] += dgso
        dz_ref[:, o_b:o_b + DS] = (d_ysc * cv).astype(BF16)
        dcv = d_ysc * sb
        extc = jnp.concatenate([dcv, cdc_ref[...]], axis=0)
        cdc_ref[...] = dcv[0:SUBLANES, :]
        dp = dcv * sw_ref[KS - 1:KS, :]
        acc_row(V_SW + KS - 1, dcv * ps[0])
        for j in range(1, KS):
            dp = dp + _later(extc, j) * sw_ref[KS - 1 - j:KS - j, :]
            acc_row(V_SW + KS - 1 - j, dcv * ps[j])
        dz_ref[:, o_c:o_c + DS] = (dp * sx).astype(BF16)
        dz_ref[:, o_x:o_x + DS] = (dp * sc).astype(BF16)

    def full(shape):
        return pl.BlockSpec(shape, lambda t: (0,) * len(shape))

    def rev(t):
        return nt - 1 - t

    def halo(t, rows):
        return jnp.maximum(rev(t) * (tt // rows) - 1, 0)

    return pl.pallas_call(
        body, name=name, grid=(nt,),
        in_specs=[pl.BlockSpec((tt, z.shape[1]), lambda t: (rev(t), 0)),
                  pl.BlockSpec((ZH, z.shape[1]), lambda t: (halo(t, ZH), 0)),
                  pl.BlockSpec((tt, DL), lambda t: (rev(t), 0)),
                  pl.BlockSpec((SUBLANES, DL), lambda t: (halo(t, SUBLANES), 0)),
                  pl.BlockSpec((tt, DL + DS), lambda t: (rev(t), 0)),
                  full(cw.shape), full(cb.shape), full(wa.shape), full(ba.shape), full(wi.shape), full(bi.shape),
                  full(lam.shape), full(sw.shape), full(glo.shape), full(gso.shape)],
        out_specs=[pl.BlockSpec((tt, z.shape[1]), lambda t: (rev(t), 0)),
                   full(wa.shape), full(wi.shape), full((V_ROWS, DL))],
        out_shape=[jax.ShapeDtypeStruct(z.shape, BF16), jax.ShapeDtypeStruct(wa.shape, F32),
                   jax.ShapeDtypeStruct(wi.shape, F32), jax.ShapeDtypeStruct((V_ROWS, DL), F32)],
        scratch_shapes=[pltpu.VMEM((SUBLANES, DL), F32), pltpu.VMEM((SUBLANES, DS), F32),
                        pltpu.VMEM((SUBLANES, DL), F32)],
        compiler_params=_params("arbitrary"),
    )(z, z, h, h, dy, cw, cb, wa, ba, wi, bi, lam, sw, glo, gso)


def _pair_add(p, r1, c, name):
    G, R, C = r1.shape
    tr = _tile(R, 256, 16)
    tc = _tile(C, 1408, LANES)

    def body(c_ref, p_ref, r_ref, o_ref):
        o_ref[...] = (p_ref[...].astype(F32) + r_ref[...].astype(F32)).astype(BF16)

    blk = (None, tr, tc)
    return pl.pallas_call(
        body, name=name,
        grid_spec=pltpu.PrefetchScalarGridSpec(
            num_scalar_prefetch=1, grid=(G, R // tr, C // tc),
            in_specs=[pl.BlockSpec(blk, lambda g, i, j, cr: (2 * g + cr[0], i, j)),
                      pl.BlockSpec(blk, lambda g, i, j, cr: (g, i, j))],
            out_specs=pl.BlockSpec(blk, lambda g, i, j, cr: (g, i, j))),
        out_shape=jax.ShapeDtypeStruct((G, R, C), BF16),
        compiler_params=_params("parallel", "parallel", "parallel"),
    )(c, p, r1)


def _quad_add(s, r2, qc, axis, name):
    _, R, W = r2.shape
    tr = _tile(R, 256, 16)

    def body(qc_ref, s_ref, r0_ref, r1_ref, r2_ref, o_ref):
        o_ref[...] = ((s_ref[...].astype(F32) + r0_ref[...].astype(F32)) + r1_ref[...].astype(F32)) + r2_ref[...].astype(F32)

    blk = (None, tr, W)
    if axis == 1:
        own = pl.BlockSpec(blk, lambda i, qr: (0, i, qr[0]))
    else:
        own = pl.BlockSpec(blk, lambda i, qr: (qr[0], i, 0))
    return pl.pallas_call(
        body, name=name,
        grid_spec=pltpu.PrefetchScalarGridSpec(
            num_scalar_prefetch=1, grid=(R // tr,),
            in_specs=[own] + [pl.BlockSpec(blk, lambda i, qr, j=j: (j, i, 0)) for j in range(3)],
            out_specs=pl.BlockSpec(blk, lambda i, qr: (qr[1], i, 0))),
        out_shape=jax.ShapeDtypeStruct((2, R, W), F32),
        compiler_params=_params("parallel"),
    )(qc, s, r2, r2, r2)


CAST_BLOCKS = 4


def _cast_into_full(shards, qc, axes, name, carry=None):
    M = len(shards)
    nb = CAST_BLOCKS

    def body(qc_ref, *refs):
        for s_ref, o_ref in zip(refs[:M], refs[M:]):
            o_ref[...] = s_ref[...].astype(BF16)

    in_specs, out_specs, out_shape = [], [], []
    for s, ax in zip(shards, axes):
        R, W = s.shape
        Rh = R // 2
        tr = Rh // nb
        in_specs.append(pl.BlockSpec((tr, W), lambda hf, i, qr: (hf * nb + i, 0)))
        if ax == 1:
            out_shape.append(jax.ShapeDtypeStruct((2, Rh, 4 * W), BF16))
            out_specs.append(pl.BlockSpec((None, tr, W), lambda hf, i, qr: (hf, i, qr[0])))
        else:
            out_shape.append(jax.ShapeDtypeStruct((8, Rh, W), BF16))
            out_specs.append(pl.BlockSpec((None, tr, W), lambda hf, i, qr: (2 * qr[0] + hf, i, 0)))
    return _call(body, name, (2, nb), in_specs, out_specs, out_shape, [], ("parallel", "parallel"), shards, carry, (qc,))


def _adamw(w, g, m, v, name):
    R, C = w.shape
    tr = _tile(R, 256, SUBLANES)
    tc = C // 2 if g.ndim == 3 else _tile(C, 2048, LANES)
    c1 = 1.0 - ADAM_B1 ** ADAM_STEP
    c2 = 1.0 - ADAM_B2 ** ADAM_STEP

    def body(w_ref, g_ref, m_ref, v_ref, d_ref, mo_ref, vo_ref, go_ref):
        gv = g_ref[...]
        go_ref[...] = gv
        mn = ADAM_B1 * m_ref[...] + (1.0 - ADAM_B1) * gv
        vn = ADAM_B2 * v_ref[...] + (1.0 - ADAM_B2) * (gv * gv)
        mo_ref[...] = mn
        vo_ref[...] = vn
        d_ref[...] = -ADAM_LR * ((mn / c1) / (jnp.sqrt(vn / c2) + ADAM_EPS) + ADAM_WD * w_ref[...])

    blk = pl.BlockSpec((tr, tc), lambda i, j: (i, j))
    g_blk = pl.BlockSpec((None, tr, tc), lambda i, j: (j, i, 0)) if g.ndim == 3 else blk
    sh = jax.ShapeDtypeStruct((R, C), F32)
    return pl.pallas_call(
        body, name=name, grid=(R // tr, C // tc),
        in_specs=[blk, g_blk, blk, blk], out_specs=[blk] * 4, out_shape=[sh] * 4,
        compiler_params=_params("parallel", "parallel"),
    )(w, g, m, v)


def _other_chips(x, y):
    return [(1 - x, y), (x, 1 - y), (1 - x, 1 - y)]


def _remote(src, dst, send_sems, recv_sems, idx, dev):
    return pltpu.make_async_remote_copy(src_ref=src, dst_ref=dst, send_sem=send_sems.at[idx], recv_sem=recv_sems.at[idx],
                                        device_id=dev, device_id_type=MESH)


def _gather_carry(fulls, axes):
    M = len(fulls)

    def win(outs, m, qq, cc):
        if axes[m] == 1:
            W = fulls[m].shape[2] // 4
            return outs[m].at[cc, :, pl.ds(pl.multiple_of(qq * W, LANES), W)]
        return outs[m].at[2 * qq + cc]

    def ici(outs, sems, m, j, src_q):
        x, y, c = _me()
        cx, cy = _other_chips(x, y)[j]
        blk = win(outs, m, src_q, c)
        return _remote(blk, blk, sems[0], sems[1], 6 * m + j, (cx, cy, c))

    def d2d(outs, sems, m, j, half):
        x, y, c = _me()
        cx, cy = _other_chips(x, y)[j]
        blk = win(outs, m, 2 * cx + cy, half)
        return _remote(blk, blk, sems[0], sems[1], 6 * m + 3 + j, (x, y, 1 - c))

    def start(ins, outs, sems):
        x, y, c = _me()
        for m in range(M):
            for j in range(3):
                ici(outs, sems, m, j, 2 * x + y).start()

    def middle(ins, outs, sems):
        x, y, c = _me()
        for m in range(M):
            for j, (cx, cy) in enumerate(_other_chips(x, y)):
                ici(outs, sems, m, j, 2 * cx + cy).wait_recv()
                d2d(outs, sems, m, j, c).start()

    def finish(ins, outs, sems):
        x, y, c = _me()
        for m in range(M):
            for j in range(3):
                d2d(outs, sems, m, j, 1 - c).wait_recv()
        for m in range(M):
            for j in range(3):
                ici(outs, sems, m, j, 2 * x + y).wait_send()
                d2d(outs, sems, m, j, c).wait_send()

    return _Carry(fulls, [jax.ShapeDtypeStruct(f.shape, f.dtype) for f in fulls], {m: m for m in range(M)},
                  [pltpu.SemaphoreType.DMA((6 * M,)), pltpu.SemaphoreType.DMA((6 * M,))], start, finish, middle)


def _gather_two_way_carry(fulls, axes):
    M = len(fulls)

    def part(outs, m, qq, cc, p):
        Rp = fulls[m].shape[1] // 2
        rows = pl.ds(p * Rp, Rp)
        if axes[m] == 1:
            W = fulls[m].shape[2] // 4
            return outs[m].at[cc, rows, pl.ds(pl.multiple_of(qq * W, LANES), W)]
        return outs[m].at[2 * qq + cc, rows, :]

    def copies(outs, sems):
        x, y, c = _me()
        q, qx, qy, qd = 2 * x + y, 2 * (1 - x) + y, 2 * x + (1 - y), 2 * (1 - x) + (1 - y)
        xn, yn, sib = (1 - x, y, c), (x, 1 - y, c), (x, y, 1 - c)
        table = {}
        for m in range(M):
            def cp(blk, k, dev):
                return _remote(blk, blk, sems[0], sems[1], 12 * m + k, dev)
            own0, own1 = part(outs, m, q, c, 0), part(outs, m, q, c, 1)
            table[m] = dict(
                to=[cp(own0, 0, xn), cp(own1, 1, yn), cp(own1, 2, xn), cp(own0, 3, yn)],
                landed=[cp(part(outs, m, qx, c, 0), 0, xn), cp(part(outs, m, qy, c, 1), 1, yn),
                        cp(part(outs, m, qx, c, 1), 2, xn), cp(part(outs, m, qy, c, 0), 3, yn),
                        cp(part(outs, m, qd, c, 0), 4, yn), cp(part(outs, m, qd, c, 1), 5, xn)],
                passed=[cp(part(outs, m, qx, c, 0), 4, yn), cp(part(outs, m, qy, c, 1), 5, xn)],
                handed=[cp(part(outs, m, qq, c, p), 6 + k, sib)
                        for k, (qq, p) in enumerate([(qx, 0), (qy, 1), (qx, 1), (qy, 0), (qd, 0), (qd, 1)])],
                taken=[cp(part(outs, m, qq, 1 - c, p), 6 + k, sib)
                       for k, (qq, p) in enumerate([(qx, 0), (qy, 1), (qx, 1), (qy, 0), (qd, 0), (qd, 1)])])
        return table

    def start(ins, outs, sems):
        t = copies(outs, sems)
        for m in range(M):
            for cp in t[m]['to']:
                cp.start()

    def finish(ins, outs, sems):
        t = copies(outs, sems)
        for m in range(M):
            for k in range(4):
                t[m]['landed'][k].wait_recv()
                if k < 2:
                    t[m]['passed'][k].start()
                t[m]['handed'][k].start()
        for m in range(M):
            for k in (4, 5):
                t[m]['landed'][k].wait_recv()
                t[m]['handed'][k].start()
        for m in range(M):
            for cp in t[m]['taken']:
                cp.wait_recv()
            for cp in t[m]['to'] + t[m]['passed'] + t[m]['handed']:
                cp.wait_send()

    return _Carry(fulls, [jax.ShapeDtypeStruct(f.shape, f.dtype) for f in fulls], {m: m for m in range(M)},
                  [pltpu.SemaphoreType.DMA((12 * M,)), pltpu.SemaphoreType.DMA((12 * M,))], start, finish)


def _pair_exchange_carry(parts):
    M = len(parts)
    groups = [p.shape[0] // 2 for p in parts]
    base = [sum(groups[:m]) for m in range(M)]
    out_shape = [jax.ShapeDtypeStruct((g,) + p.shape[1:], p.dtype) for g, p in zip(groups, parts)]

    def copies(ins, outs, sems):
        x, y, c = _me()
        return [_remote(ins[m].at[2 * g + 1 - c], outs[m].at[g], sems[0], sems[1], base[m] + g, (x, y, 1 - c))
                for m in range(M) for g in range(groups[m])]

    def start(ins, outs, sems):
        for cp in copies(ins, outs, sems):
            cp.start()

    def finish(ins, outs, sems):
        for cp in copies(ins, outs, sems):
            cp.wait()

    n = sum(groups)
    return _Carry(parts, out_shape, {}, [pltpu.SemaphoreType.DMA((n,)), pltpu.SemaphoreType.DMA((n,))], start, finish)


def _chip_exchange_carry(sums, axes):
    M = len(sums)
    out_shape = []
    for s, ax in zip(sums, axes):
        _, Rh, C = s.shape
        out_shape.append(jax.ShapeDtypeStruct((3, Rh, C // 4 if ax == 1 else C), s.dtype))

    def copies(ins, outs, sems):
        x, y, c = _me()
        cps = []
        for m in range(M):
            for j, (cx, cy) in enumerate(_other_chips(x, y)):
                qj = 2 * cx + cy
                if axes[m] == 1:
                    W = sums[m].shape[2] // 4
                    src = ins[m].at[0, :, pl.ds(pl.multiple_of(qj * W, LANES), W)]
                else:
                    src = ins[m].at[qj]
                cps.append(_remote(src, outs[m].at[j], sems[0], sems[1], 3 * m + j, (cx, cy, c)))
        return cps

    def start(ins, outs, sems):
        for cp in copies(ins, outs, sems):
            cp.start()

    def finish(ins, outs, sems):
        for cp in copies(ins, outs, sems):
            cp.wait()

    return _Carry(sums, out_shape, {}, [pltpu.SemaphoreType.DMA((3 * M,)), pltpu.SemaphoreType.DMA((3 * M,))],
                  start, finish)


def _pair_share_carry(bufs):
    M = len(bufs)

    def start(ins, outs, sems):
        x, y, c = _me()
        for m in range(M):
            _remote(outs[m].at[c], outs[m].at[c], sems[0], sems[1], m, (x, y, 1 - c)).start()

    def finish(ins, outs, sems):
        x, y, c = _me()
        for m in range(M):
            _remote(outs[m].at[c], outs[m].at[c], sems[0], sems[1], m, (x, y, 1 - c)).wait_send()
            _remote(outs[m].at[1 - c], outs[m].at[1 - c], sems[0], sems[1], m, (x, y, 1 - c)).wait_recv()

    return _Carry(bufs, [jax.ShapeDtypeStruct(b.shape, b.dtype) for b in bufs], {m: m for m in range(M)},
                  [pltpu.SemaphoreType.DMA((M,)), pltpu.SemaphoreType.DMA((M,))], start, finish)


def _allreduce_small(v, name):
    R, W = v.shape
    Rh = R // 2

    def body(v_ref, o_ref, sib, quad, send_sems, recv_sems):
        x, y, c = _me()
        q = 2 * x + y
        sibling = (x, y, 1 - c)
        pair = _remote(v_ref, sib, send_sems, recv_sems, 0, sibling)
        pair.start()
        pair.wait()
        mine = pl.ds(pl.multiple_of(c * Rh, SUBLANES), Rh)
        quad[0] = v_ref[mine, :] + sib[mine, :]
        cps = []
        for k in (1, 2, 3):
            peer = (1 - x if k & 2 else x, 1 - y if k & 1 else y, c)
            cps.append(_remote(quad.at[0], quad.at[k], send_sems, recv_sems, k, peer))
            cps[-1].start()
        for cp in cps:
            cp.wait()
        acc = quad[q]
        for p in (1, 2, 3):
            acc = acc + quad[jnp.bitwise_xor(q, p)]
        o_ref[mine, :] = acc
        theirs = pl.ds(pl.multiple_of((1 - c) * Rh, SUBLANES), Rh)
        done = _remote(o_ref.at[mine, :], o_ref.at[mine, :], send_sems, recv_sems, 4, sibling)
        done.start()
        done.wait_send()
        _remote(o_ref.at[theirs, :], o_ref.at[theirs, :], send_sems, recv_sems, 4, sibling).wait_recv()

    vm = pl.BlockSpec(memory_space=pltpu.VMEM)
    return pl.pallas_call(
        body, name=name, in_specs=[vm], out_specs=vm, out_shape=jax.ShapeDtypeStruct((R, W), F32),
        scratch_shapes=[pltpu.VMEM((R, W), F32), pltpu.VMEM((4, Rh, W), F32),
                        pltpu.SemaphoreType.DMA((5,)), pltpu.SemaphoreType.DMA((5,))],
        compiler_params=pltpu.CompilerParams(vmem_limit_bytes=VMEM_LIMIT),
    )(v)


def _pack(pieces):
    flat = []
    for p in pieces:
        p = p.reshape(-1).astype(F32)
        pad = (-p.shape[0]) % PACK_ALIGN
        flat.append(jnp.pad(p, (0, pad)).reshape(-1, PACK_W))
    if sum(f.shape[0] for f in flat) % (2 * SUBLANES):
        flat.append(jnp.zeros((SUBLANES, PACK_W), F32))
    return jnp.concatenate(flat, axis=0)


def _unpack(packed, shapes):
    out, row = [], 0
    for shp in shapes:
        n = math.prod(shp)
        rows = -(-n // PACK_ALIGN) * SUBLANES
        out.append(packed[row:row + rows].reshape(-1)[:n].reshape(shp))
        row += rows
    return out


def kernel(x, ffn1_norm, ffn1_w_gate, ffn1_w_up, ffn1_w_down, mix_norm, w_in, lru_conv_w, lru_conv_b, lru_w_a, lru_b_a, lru_w_i, lru_b_i, lru_lambda, sc_conv_w, lru_out_norm, sc_out_norm, w_out, ffn2_norm, ffn2_w_gate, ffn2_w_up, ffn2_w_down, final_norm, loss_target, m_ffn1_norm, m_ffn1_w_gate, m_ffn1_w_up, m_ffn1_w_down, m_mix_norm, m_w_in, m_lru_conv_w, m_lru_conv_b, m_lru_w_a, m_lru_b_a, m_lru_w_i, m_lru_b_i, m_lru_lambda, m_sc_conv_w, m_lru_out_norm, m_sc_out_norm, m_w_out, m_ffn2_norm, m_ffn2_w_gate, m_ffn2_w_up, m_ffn2_w_down, m_final_norm, v_ffn1_norm, v_ffn1_w_gate, v_ffn1_w_up, v_ffn1_w_down, v_mix_norm, v_w_in, v_lru_conv_w, v_lru_conv_b, v_lru_w_a, v_lru_b_a, v_lru_w_i, v_lru_b_i, v_lru_lambda, v_sc_conv_w, v_lru_out_norm, v_sc_out_norm, v_w_out, v_ffn2_norm, v_ffn2_w_gate, v_ffn2_w_up, v_ffn2_w_down, v_final_norm):
    vals = locals()
    w = {n: vals[n] for n in WEIGHTS}
    mom = {n: vals["m_" + n] for n in WEIGHTS}
    var = {n: vals["v_" + n] for n in WEIGHTS}

    xi, yi, ci = _me()
    qi = 2 * xi + yi
    c_arr = jnp.reshape(ci, (1,)).astype(jnp.int32)
    qc_arr = jnp.stack([qi, ci]).astype(jnp.int32)

    T, D = x.shape[1], x.shape[2]
    xt = x.reshape(T, D)
    target = loss_target.reshape(T, D)
    DL = lru_conv_b.shape[-1]
    NH, HD = lru_w_a.shape[1], lru_w_a.shape[2]
    KL, KS = lru_conv_w.shape[1], sc_conv_w.shape[1]
    DLq = lru_conv_w.shape[2]

    axis_of = dict(zip(BIG, BIG_AXIS))
    first_names = ['ffn1_w_gate', 'ffn1_w_up', 'ffn1_w_down']
    later_names = ['w_in', 'w_out', 'ffn2_w_gate', 'ffn2_w_up', 'ffn2_w_down']

    def unview(n, g):
        return g.reshape(2 * g.shape[1], g.shape[2]) if axis_of[n] == 1 else g.reshape(8 * g.shape[1], g.shape[2])

    first_axes, later_axes = [axis_of[n] for n in first_names], [axis_of[n] for n in later_names]
    placed = _cast_into_full([w[n][0] for n in first_names], qc_arr, first_axes, "cast_ffn1_weights")
    res = _cast_into_full([w[n][0] for n in later_names], qc_arr, later_axes, "cast_later_weights",
                          _gather_two_way_carry(placed, first_axes))
    full = {n: unview(n, g) for n, g in zip(first_names, res[len(later_names):])}
    gather_later = _gather_carry(res[:len(later_names)], later_axes)

    taps = jnp.zeros((2 * SUBLANES, DL), F32)
    taps = lax.dynamic_update_slice(taps, lru_conv_w[0], (0, qi * DLq))
    taps = lax.dynamic_update_slice(taps, sc_conv_w[0], (KL, qi * DLq))
    taps = _allreduce_small(jnp.where(ci == 0, taps, 0.0), "gather_conv_taps")
    cw, sw = taps[0:KL], taps[KL:KL + KS]

    cb = lru_conv_b
    wa, wi = lru_w_a[0].astype(BF16), lru_w_i[0].astype(BF16)
    ba, bi = lru_b_a.reshape(1, DL), lru_b_i.reshape(1, DL)
    mix_args = (cw, cb, wa, ba, wi, bi, lru_lambda, sw, lru_out_norm, sc_out_norm)
    gf = final_norm.reshape(1, D)

    res = _ffn_fwd(xt, ffn1_norm, full['ffn1_w_gate'], full['ffn1_w_up'], full['ffn1_w_down'], "ffn1_fwd", gather_later)
    x1, n1, G1, U1 = res[:4]
    full.update({n: unview(n, g) for n, g in zip(later_names, res[4:])})
    n2, z = _norm_mm(x1, mix_norm, full['w_in'], "mix_in_proj")
    h, ymix = _mix_fwd(z, *mix_args, "mix_fwd")
    x2 = _mm_fullk(ymix, full['w_out'], False, x1, F32, "mix_out_proj")[0]
    x3, n3, G2, U2 = _ffn_fwd(x2, ffn2_norm, full['ffn2_w_gate'], full['ffn2_w_up'], full['ffn2_w_down'], "ffn2_fwd")
    dx3, d3b, sqerr, dgf = _loss_head(x3, gf, target, "loss_head")

    sums, halves, shared = {}, {}, {}

    def dw(n, a, b, name, carry=None):
        res = _mm_tn_pair(a, b, c_arr, axis_of[n], name, carry)
        sums[n] = res[0]
        return res[2:]

    def chip_carry(names):
        return _chip_exchange_carry([sums[n] for n in names], [axis_of[n] for n in names])

    def chip_add(names, recv):
        for n, r in zip(names, recv):
            halves[n] = _quad_add(sums[n], r, qc_arr, axis_of[n], "grad_chip_add_" + n)

    dG2, dU2, H2 = _ffn_bwd_hidden(d3b, G2, U2, full['ffn2_w_down'], "ffn2_bwd_hidden")
    dn3 = _ffn_bwd_input(dG2, dU2, full['ffn2_w_gate'], full['ffn2_w_up'], "ffn2_bwd_input")[0]
    dx2, dx2b, dg_ffn2 = _rms_bwd_res(dn3, x2, ffn2_norm, dx3, 1.0, "ffn2_norm_bwd")
    dw('ffn2_w_gate', n3, dG2, "ffn2_dwg")
    dw('ffn2_w_up', n3, dU2, "ffn2_dwu")
    dw('ffn2_w_down', H2, d3b, "ffn2_dwd")
    dy = _mm_fullk(dx2b, full['w_out'], True, None, BF16, "mix_out_bwd")[0]
    dz, dwa, dwi, vec = _mix_bwd(z, h, dy, *mix_args, "mix_bwd")
    dg_mix, dx1, d1b = _mm_nt_norm_bwd(dz, full['w_in'], x1, mix_norm, dx2, FFN_RESIDUAL_SCALE, "mix_in_bwd")

    def share_carry(names):
        return _pair_share_carry([halves[n] for n in names])

    res = _ffn_bwd_hidden(d1b, G1, U1, full['ffn1_w_down'], "ffn1_bwd_hidden", chip_carry(['ffn2_w_gate']))
    dG1, dU1, H1 = res[:3]
    chip_add(['ffn2_w_gate'], res[3:])
    res = _ffn_bwd_input(dG1, dU1, full['ffn1_w_gate'], full['ffn1_w_up'], "ffn1_bwd_input",
                         chip_carry(['ffn2_w_up', 'ffn2_w_down']))
    dn1 = res[0]
    chip_add(['ffn2_w_up', 'ffn2_w_down'], res[1:])
    dx0, _, dg_ffn1 = _rms_bwd_res(dn1, xt, ffn1_norm, dx1, 1.0, "ffn1_norm_bwd")
    names = ['ffn2_w_gate', 'ffn2_w_up', 'ffn2_w_down']
    shared.update(zip(names, dw('ffn1_w_gate', n1, dG1, "ffn1_dwg", share_carry(names))))
    chip_add(['ffn1_w_gate'], dw('ffn1_w_up', n1, dU1, "ffn1_dwu", chip_carry(['ffn1_w_gate'])))
    chip_add(['ffn1_w_up'], dw('ffn1_w_down', H1, d1b, "ffn1_dwd", chip_carry(['ffn1_w_up'])))
    names = ['ffn1_w_gate', 'ffn1_w_up']
    res = dw('w_in', n2, dz, "mix_dwin", _merge_carries(chip_carry(['ffn1_w_down']), share_carry(names)))
    chip_add(['ffn1_w_down'], res[:1])
    shared.update(zip(names, res[1:]))
    res = dw('w_out', ymix, dx2b, "mix_dwout", _merge_carries(chip_carry(['w_in']), share_carry(['ffn1_w_down'])))
    chip_add(['w_in'], res[:1])
    shared['ffn1_w_down'] = res[1]
    chip_add(['w_out'], _run_carry(chip_carry(['w_out']), "grad_chip_exchange_w_out"))
    names = ['w_in', 'w_out']
    shared.update(zip(names, _run_carry(share_carry(names), "grad_pair_share")))
    out_g, out_d, out_m, out_v = {}, {}, {}, {}
    for n in BIG:
        shp = w[n].shape
        g = shared[n].reshape(shp[1], shp[2]) if axis_of[n] == 1 else shared[n]
        outs = _adamw(w[n][0], g, mom[n][0], var[n][0], "adamw_" + n)
        out_d[n], out_m[n], out_v[n], out_g[n] = (a.reshape(shp) for a in outs)

    small = [n for n in WEIGHTS if n not in BIG]
    local_small = {
        'ffn1_norm': dg_ffn1, 'mix_norm': dg_mix, 'lru_conv_w': vec[V_CW:V_CW + KL], 'lru_conv_b': vec[V_CB],
        'lru_w_a': dwa, 'lru_b_a': vec[V_BA], 'lru_w_i': dwi, 'lru_b_i': vec[V_BI], 'lru_lambda': vec[V_LAM],
        'sc_conv_w': vec[V_SW:V_SW + KS], 'lru_out_norm': vec[V_GLO], 'sc_out_norm': vec[V_GSO],
        'ffn2_norm': dg_ffn2, 'final_norm': dgf,
    }
    full_shapes = [local_small[n].shape for n in small] + [(1,)]
    reduced = _allreduce_small(_pack([local_small[n] for n in small] + [sqerr[0, 0:1]]), "allreduce_small")
    reduced = _unpack(reduced, full_shapes)
    loss = (0.5 / D) * reduced[-1][0]
    gsm = {}
    for n, g in zip(small, reduced[:-1]):
        if n in SMALL_SHARDED:
            g = lax.dynamic_slice(g, (0, qi * DLq), (g.shape[0], DLq))
        gsm[n] = g.reshape(w[n].shape)
    small_shapes = [w[n].shape for n in small]
    d_s, m_s, v_s, _ = _adamw(_pack([w[n] for n in small]), _pack([gsm[n] for n in small]),
                              _pack([mom[n] for n in small]), _pack([var[n] for n in small]), "adamw_small")
    for n, d, mn, vn in zip(small, _unpack(d_s, small_shapes), _unpack(m_s, small_shapes), _unpack(v_s, small_shapes)):
        out_g[n], out_d[n], out_m[n], out_v[n] = gsm[n], d, mn, vn

    return (loss, dx0.reshape(x.shape), *[out_g[n] for n in WEIGHTS], *[out_d[n] for n in WEIGHTS],
            *[out_m[n] for n in WEIGHTS], *[out_v[n] for n in WEIGHTS])
```

```python
import math

import jax
import jax.numpy as jnp
from jax import lax
from jax.experimental import pallas as pl
from jax.experimental.pallas import tpu as pltpu

F32 = jnp.float32
BF16 = jnp.bfloat16
MESH = pl.DeviceIdType.MESH
ANY = pl.BlockSpec(memory_space=pl.ANY)

NORM_EPS = 1e-6
LRU_C = 8.0
FFN_RESIDUAL_SCALE = 0.5
ADAM_LR = 0.001
ADAM_B1 = 0.9
ADAM_B2 = 0.999
ADAM_EPS = 1e-08
ADAM_WD = 0.01
ADAM_STEP = 10

V7X_VMEM_BYTES = 64 * 2**20
VMEM_LIMIT = V7X_VMEM_BYTES - 8 * 2**20
LANES = 128
SUBLANES = 8
PACK_W = LANES
PACK_ALIGN = SUBLANES * PACK_W

WEIGHTS = ['ffn1_norm', 'ffn1_w_gate', 'ffn1_w_up', 'ffn1_w_down', 'mix_norm', 'w_in', 'lru_conv_w', 'lru_conv_b',
           'lru_w_a', 'lru_b_a', 'lru_w_i', 'lru_b_i', 'lru_lambda', 'sc_conv_w', 'lru_out_norm', 'sc_out_norm',
           'w_out', 'ffn2_norm', 'ffn2_w_gate', 'ffn2_w_up', 'ffn2_w_down', 'final_norm']
BIG = ['ffn1_w_gate', 'ffn1_w_up', 'ffn1_w_down', 'w_in', 'w_out', 'ffn2_w_gate', 'ffn2_w_up', 'ffn2_w_down']
BIG_AXIS = [1, 1, 0, 1, 0, 1, 1, 0]
SMALL_SHARDED = ['lru_conv_w', 'sc_conv_w']


def _tile(n, pref, mult):
    if n <= pref:
        return n
    t = (pref // mult) * mult
    while t >= mult:
        if n % t == 0:
            return t
        t -= mult
    return n


def _params(*sem):
    return pltpu.CompilerParams(dimension_semantics=sem, vmem_limit_bytes=VMEM_LIMIT)


def _me():
    return lax.axis_index("x"), lax.axis_index("y"), lax.axis_index("c")


def _sigmoid(v):
    return 1.0 / (1.0 + jnp.exp(-v))


def _rstd(v):
    return lax.rsqrt(jnp.mean(v * v, axis=-1, keepdims=True) + NORM_EPS)


def _rms_bwd(dy, v, gain):
    r = _rstd(v)
    w = gain * dy
    dv = r * w - v * (r * r * r) * jnp.mean(v * w, axis=-1, keepdims=True)
    dgain = jnp.sum(dy * v * r, axis=0, keepdims=True)
    return dv, dgain


def _dot_nt(a, b):
    return lax.dot_general(a, b, (((1,), (1,)), ((), ())), preferred_element_type=F32)


def _dot_tn(a, b):
    return lax.dot_general(a, b, (((0,), (0,)), ((), ())), preferred_element_type=F32)


class _Carry:
    def __init__(self, inputs, out_shape, aliases, sems, start, finish, middle=None, middle_at=0.85):
        self.inputs, self.out_shape, self.aliases, self.sems = list(inputs), list(out_shape), dict(aliases), list(sems)
        self.start, self.finish = start, finish
        self.middle, self.middle_at = middle, middle_at


def _merge_carries(a, b):
    ia, oa, sa = len(a.inputs), len(a.out_shape), len(a.sems)
    aliases = dict(a.aliases)
    aliases.update({ia + i: oa + j for i, j in b.aliases.items()})

    def both(which):
        def run(ins, outs, sems):
            getattr(a, which)(ins[:ia], outs[:oa], sems[:sa])
            getattr(b, which)(ins[ia:], outs[oa:], sems[sa:])
        return run

    return _Carry(a.inputs + b.inputs, a.out_shape + b.out_shape, aliases, a.sems + b.sems, both("start"), both("finish"))


def _call(body, name, grid, in_specs, out_specs, out_shape, scratch_shapes, semantics, args, carry=None, prefetch=()):
    np_ = len(prefetch)
    if carry is None:
        spec = pltpu.PrefetchScalarGridSpec(num_scalar_prefetch=np_, grid=grid, in_specs=in_specs, out_specs=out_specs,
                                            scratch_shapes=scratch_shapes)
        return pl.pallas_call(body, name=name, grid_spec=spec, out_shape=out_shape,
                              compiler_params=_params(*semantics))(*prefetch, *args)
    ni, no, ns = len(in_specs), len(out_specs), len(scratch_shapes)
    ci, co = len(carry.inputs), len(carry.out_shape)

    def carrying(*refs):
        pre, refs = refs[:np_], refs[np_:]
        ins, refs = refs[:ni], refs[ni:]
        cins, refs = refs[:ci], refs[ci:]
        outs, refs = refs[:no], refs[no:]
        couts, refs = refs[:co], refs[co:]
        scratch, csems = refs[:ns], refs[ns:]
        step = pl.program_id(0)
        for ax in range(1, len(grid)):
            step = step * grid[ax] + pl.program_id(ax)
        steps = math.prod(grid)
        first = step == 0
        last = step == steps - 1

        @pl.when(first)
        def _():
            carry.start(cins, couts, csems)

        if carry.middle is not None:
            @pl.when(step == min(int(carry.middle_at * steps), steps - 1))
            def _():
                carry.middle(cins, couts, csems)

        body(*pre, *ins, *outs, *scratch)

        @pl.when(last)
        def _():
            carry.finish(cins, couts, csems)

    spec = pltpu.PrefetchScalarGridSpec(
        num_scalar_prefetch=np_, grid=grid, in_specs=list(in_specs) + [ANY] * ci,
        out_specs=list(out_specs) + [ANY] * co, scratch_shapes=list(scratch_shapes) + carry.sems)
    return pl.pallas_call(
        carrying, name=name, grid_spec=spec, out_shape=list(out_shape) + carry.out_shape,
        input_output_aliases={np_ + ni + i: no + j for i, j in carry.aliases.items()},
        compiler_params=_params(*(["arbitrary"] * len(grid))),
    )(*prefetch, *args, *carry.inputs)


def _run_carry(carry, name):
    ci, co = len(carry.inputs), len(carry.out_shape)

    def body(*refs):
        cins, couts, csems = refs[:ci], refs[ci:ci + co], refs[ci + co:]
        carry.start(cins, couts, csems)
        if carry.middle is not None:
            carry.middle(cins, couts, csems)
        carry.finish(cins, couts, csems)

    return pl.pallas_call(body, name=name, in_specs=[ANY] * ci, out_specs=[ANY] * co, out_shape=carry.out_shape,
                          input_output_aliases=carry.aliases, scratch_shapes=carry.sems)(*carry.inputs)


def _ffn_fwd(x, gain, wg, wu, wd, name, carry=None):
    T, D = x.shape
    FF = wg.shape[1]
    tm = _tile(T, 512, 16)
    tf = _tile(FF, 512, LANES)
    nf = FF // tf

    def body(x_ref, g_ref, wg_ref, wu_ref, wd_ref, xo_ref, n_ref, G_ref, U_ref, acc_ref):
        f = pl.program_id(1)

        @pl.when(f == 0)
        def _():
            xv = x_ref[...]
            n_ref[...] = (xv * _rstd(xv) * g_ref[...]).astype(BF16)
            acc_ref[...] = jnp.zeros_like(acc_ref)

        n = n_ref[...]
        G = jnp.dot(n, wg_ref[...], preferred_element_type=F32)
        U = jnp.dot(n, wu_ref[...], preferred_element_type=F32)
        G_ref[...] = G.astype(BF16)
        U_ref[...] = U.astype(BF16)
        H = (G * _sigmoid(G) * U).astype(BF16)
        acc_ref[...] += jnp.dot(H, wd_ref[...], preferred_element_type=F32)

        @pl.when(f == nf - 1)
        def _():
            xo_ref[...] = x_ref[...] + FFN_RESIDUAL_SCALE * acc_ref[...]

    return _call(
        body, name, (T // tm, nf),
        [pl.BlockSpec((tm, D), lambda i, f: (i, 0)),
         pl.BlockSpec((1, D), lambda i, f: (0, 0)),
         pl.BlockSpec((D, tf), lambda i, f: (0, f)),
         pl.BlockSpec((D, tf), lambda i, f: (0, f)),
         pl.BlockSpec((tf, D), lambda i, f: (f, 0))],
        [pl.BlockSpec((tm, D), lambda i, f: (i, 0)),
         pl.BlockSpec((tm, D), lambda i, f: (i, 0)),
         pl.BlockSpec((tm, tf), lambda i, f: (i, f)),
         pl.BlockSpec((tm, tf), lambda i, f: (i, f))],
        [jax.ShapeDtypeStruct((T, D), F32), jax.ShapeDtypeStruct((T, D), BF16),
         jax.ShapeDtypeStruct((T, FF), BF16), jax.ShapeDtypeStruct((T, FF), BF16)],
        [pltpu.VMEM((tm, D), F32)], ("parallel", "arbitrary"), (x, gain, wg, wu, wd), carry)


def _ffn_bwd_act(db, G, U, wg, wu, wd, name, carry=None):
    T, D = db.shape
    FF = wg.shape[1]
    tm = _tile(T, 512, 16)
    tf = _tile(FF, 512, LANES)

    def body(d_ref, G_ref, U_ref, wg_ref, wu_ref, wd_ref, dG_ref, dU_ref, H_ref, dn_ref):
        f = pl.program_id(1)
        dH = _dot_nt(d_ref[...], wd_ref[...])
        Gv = G_ref[...].astype(F32)
        Uv = U_ref[...].astype(F32)
        s = _sigmoid(Gv)
        sg = Gv * s
        H_ref[...] = (sg * Uv).astype(BF16)
        dU = (dH * sg).astype(BF16)
        dG = (dH * Uv * (s * (1.0 + Gv * (1.0 - s)))).astype(BF16)
        dG_ref[...] = dG
        dU_ref[...] = dU
        contrib = _dot_nt(dG, wg_ref[...]) + _dot_nt(dU, wu_ref[...])

        @pl.when(f == 0)
        def _():
            dn_ref[...] = contrib

        @pl.when(f > 0)
        def _():
            dn_ref[...] += contrib

    return _call(
        body, name, (T // tm, FF // tf),
        [pl.BlockSpec((tm, D), lambda i, f: (i, 0)),
         pl.BlockSpec((tm, tf), lambda i, f: (i, f)),
         pl.BlockSpec((tm, tf), lambda i, f: (i, f)),
         pl.BlockSpec((D, tf), lambda i, f: (0, f)),
         pl.BlockSpec((D, tf), lambda i, f: (0, f)),
         pl.BlockSpec((tf, D), lambda i, f: (f, 0))],
        [pl.BlockSpec((tm, tf), lambda i, f: (i, f)),
         pl.BlockSpec((tm, tf), lambda i, f: (i, f)),
         pl.BlockSpec((tm, tf), lambda i, f: (i, f)),
         pl.BlockSpec((tm, D), lambda i, f: (i, 0))],
        [jax.ShapeDtypeStruct((T, FF), BF16), jax.ShapeDtypeStruct((T, FF), BF16),
         jax.ShapeDtypeStruct((T, FF), BF16), jax.ShapeDtypeStruct((T, D), F32)],
        [], ("parallel", "arbitrary"), (db, G, U, wg, wu, wd), carry)


def _ffn_bwd_hidden(db, G, U, wd, name, carry=None):
    T, D = db.shape
    FF = wd.shape[0]
    tm = _tile(T, 1024, 16)
    tf = _tile(FF, 512, LANES)

    def body(d_ref, G_ref, U_ref, wd_ref, dG_ref, dU_ref, H_ref):
        dH = _dot_nt(d_ref[...], wd_ref[...])
        Gv = G_ref[...].astype(F32)
        Uv = U_ref[...].astype(F32)
        s = _sigmoid(Gv)
        sg = Gv * s
        H_ref[...] = (sg * Uv).astype(BF16)
        dU_ref[...] = (dH * sg).astype(BF16)
        dG_ref[...] = (dH * Uv * (s * (1.0 + Gv * (1.0 - s)))).astype(BF16)

    act = pl.BlockSpec((tm, tf), lambda i, f: (i, f))
    return _call(
        body, name, (T // tm, FF // tf),
        [pl.BlockSpec((tm, D), lambda i, f: (i, 0)), act, act, pl.BlockSpec((tf, D), lambda i, f: (f, 0))],
        [act, act, act], [jax.ShapeDtypeStruct((T, FF), BF16)] * 3,
        [], ("parallel", "arbitrary"), (db, G, U, wd), carry)


def _ffn_bwd_input(dG, dU, wg, wu, name, carry=None):
    T, FF = dG.shape
    D = wg.shape[0]
    tm = _tile(T, 512, 16)
    tn = _tile(D, 512, LANES)

    def body(dG_ref, dU_ref, wg_ref, wu_ref, dn_ref):
        dn_ref[...] = _dot_nt(dG_ref[...], wg_ref[...]) + _dot_nt(dU_ref[...], wu_ref[...])

    act = pl.BlockSpec((tm, FF), lambda i, j: (i, 0))
    wt = pl.BlockSpec((tn, FF), lambda i, j: (j, 0))
    return _call(body, name, (T // tm, D // tn), [act, act, wt, wt], [pl.BlockSpec((tm, tn), lambda i, j: (i, j))],
                 [jax.ShapeDtypeStruct((T, D), F32)], [], ("parallel", "arbitrary"), (dG, dU, wg, wu), carry)


TAIL_ROWS = 128


class _NormBwdTail:
    def __init__(self, T, D, tm, scale):
        self.T, self.D, self.tm, self.scale = T, D, tm, scale
        self.ni = T // tm
        self.scratch = [pltpu.VMEM((tm, D), F32), pltpu.VMEM((tm, D), F32), pltpu.VMEM((tm, D), F32),
                        pltpu.VMEM((tm, D), BF16), pltpu.SemaphoreType.DMA((4,))]
        self.out_shape = [jax.ShapeDtypeStruct((T, D), F32), jax.ShapeDtypeStruct((T, D), BF16)]

    def _rows(self, k):
        return pl.ds(pl.multiple_of(k * self.tm, self.tm), self.tm)

    def _loads(self, k, x_hbm, r_hbm, bufs):
        xbuf, rbuf, _, _, sems = bufs
        return [pltpu.make_async_copy(x_hbm.at[self._rows(k)], xbuf, sems.at[0]),
                pltpu.make_async_copy(r_hbm.at[self._rows(k)], rbuf, sems.at[1])]

    def _stores(self, k, dx_hbm, dxb_hbm, bufs):
        _, _, obuf, obbuf, sems = bufs
        return [pltpu.make_async_copy(obuf, dx_hbm.at[self._rows(k)], sems.at[2]),
                pltpu.make_async_copy(obbuf, dxb_hbm.at[self._rows(k)], sems.at[3])]

    def prefetch(self, i, x_hbm, r_hbm, bufs):
        for cp in self._loads(i, x_hbm, r_hbm, bufs):
            cp.start()

    def run(self, i, acc_ref, g_ref, x_hbm, r_hbm, dx_hbm, dxb_hbm, dg_ref, bufs):
        xbuf, rbuf, obuf, obbuf, _ = bufs
        for cp in self._loads(i, x_hbm, r_hbm, bufs):
            cp.wait()

        @pl.when(i > 0)
        def _():
            for cp in self._stores(i - 1, dx_hbm, dxb_hbm, bufs):
                cp.wait()

        dgain = None
        for r0 in range(0, self.tm, TAIL_ROWS):
            rs = slice(r0, min(r0 + TAIL_ROWS, self.tm))
            dv, dgr = _rms_bwd(acc_ref[rs, :], xbuf[rs, :], g_ref[...])
            dx = rbuf[rs, :] + dv
            obuf[rs, :] = dx
            obbuf[rs, :] = (self.scale * dx).astype(BF16)
            dgain = dgr if dgain is None else dgain + dgr
        for cp in self._stores(i, dx_hbm, dxb_hbm, bufs):
            cp.start()

        @pl.when(i == 0)
        def _():
            dg_ref[...] = dgain

        @pl.when(i > 0)
        def _():
            dg_ref[...] += dgain

        @pl.when(i == self.ni - 1)
        def _():
            for cp in self._stores(i, dx_hbm, dxb_hbm, bufs):
                cp.wait()


def _ffn_bwd_fused(db, G, U, wg, wu, wd, x_in, gain, dres, scale, name):
    T, D = db.shape
    FF = wg.shape[1]
    tm = _tile(T, 512, 16)
    tf = _tile(FF, 512, LANES)
    nf = FF // tf
    tail = _NormBwdTail(T, D, tm, scale)

    def body(d_ref, G_ref, U_ref, wg_ref, wu_ref, wd_ref, g_ref, x_hbm, r_hbm,
             dG_ref, dU_ref, H_ref, dg_ref, dx_hbm, dxb_hbm, acc_ref, *bufs):
        i, f = pl.program_id(0), pl.program_id(1)

        @pl.when(f == 0)
        def _():
            tail.prefetch(i, x_hbm, r_hbm, bufs)

        dH = _dot_nt(d_ref[...], wd_ref[...])
        Gv = G_ref[...].astype(F32)
        Uv = U_ref[...].astype(F32)
        s = _sigmoid(Gv)
        sg = Gv * s
        H_ref[...] = (sg * Uv).astype(BF16)
        dU = (dH * sg).astype(BF16)
        dG = (dH * Uv * (s * (1.0 + Gv * (1.0 - s)))).astype(BF16)
        dG_ref[...] = dG
        dU_ref[...] = dU
        contrib = _dot_nt(dG, wg_ref[...]) + _dot_nt(dU, wu_ref[...])

        @pl.when(f == 0)
        def _():
            acc_ref[...] = contrib

        @pl.when(f > 0)
        def _():
            acc_ref[...] += contrib

        @pl.when(f == nf - 1)
        def _():
            tail.run(i, acc_ref, g_ref, x_hbm, r_hbm, dx_hbm, dxb_hbm, dg_ref, bufs)

    act = pl.BlockSpec((tm, tf), lambda i, f: (i, f))
    return pl.pallas_call(
        body, name=name, grid=(T // tm, nf),
        in_specs=[pl.BlockSpec((tm, D), lambda i, f: (i, 0)), act, act,
                  pl.BlockSpec((D, tf), lambda i, f: (0, f)),
                  pl.BlockSpec((D, tf), lambda i, f: (0, f)),
                  pl.BlockSpec((tf, D), lambda i, f: (f, 0)),
                  pl.BlockSpec((1, D), lambda i, f: (0, 0)), ANY, ANY],
        out_specs=[act, act, act, pl.BlockSpec((1, D), lambda i, f: (0, 0)), ANY, ANY],
        out_shape=[jax.ShapeDtypeStruct((T, FF), BF16)] * 3 + [jax.ShapeDtypeStruct((1, D), F32)] + tail.out_shape,
        scratch_shapes=[pltpu.VMEM((tm, D), F32)] + tail.scratch,
        compiler_params=_params("arbitrary", "arbitrary"),
    )(db, G, U, wg, wu, wd, gain, x_in, dres)


def _mm_nt_norm_bwd(a, w, x_in, gain, dres, scale, name, carry=None):
    T, K = a.shape
    D = w.shape[0]
    tm = _tile(T, 512, 16)
    tk = _tile(K, 1280, LANES)
    nk = K // tk
    tail = _NormBwdTail(T, D, tm, scale)

    def body(a_ref, w_ref, g_ref, x_hbm, r_hbm, dg_ref, dx_hbm, dxb_hbm, acc_ref, *bufs):
        i, k = pl.program_id(0), pl.program_id(1)

        @pl.when(k == 0)
        def _():
            tail.prefetch(i, x_hbm, r_hbm, bufs)

        contrib = _dot_nt(a_ref[...], w_ref[...])

        @pl.when(k == 0)
        def _():
            acc_ref[...] = contrib

        @pl.when(k > 0)
        def _():
            acc_ref[...] += contrib

        @pl.when(k == nk - 1)
        def _():
            tail.run(i, acc_ref, g_ref, x_hbm, r_hbm, dx_hbm, dxb_hbm, dg_ref, bufs)

    return _call(
        body, name, (T // tm, nk),
        [pl.BlockSpec((tm, tk), lambda i, k: (i, k)), pl.BlockSpec((D, tk), lambda i, k: (0, k)),
         pl.BlockSpec((1, D), lambda i, k: (0, 0)), ANY, ANY],
        [pl.BlockSpec((1, D), lambda i, k: (0, 0)), ANY, ANY],
        [jax.ShapeDtypeStruct((1, D), F32)] + tail.out_shape,
        [pltpu.VMEM((tm, D), F32)] + tail.scratch, ("arbitrary", "arbitrary"), (a, w, gain, x_in, dres), carry)


def _rms_bwd_res(dn, x, gain, dres, scale, name, carry=None):
    T, D = x.shape
    tm = _tile(T, 256, 16)

    def body(dn_ref, x_ref, g_ref, dr_ref, dx_ref, dxb_ref, dg_ref):
        i = pl.program_id(0)
        dv, dgain = _rms_bwd(dn_ref[...], x_ref[...], g_ref[...])
        dx = dr_ref[...] + dv
        dx_ref[...] = dx
        dxb_ref[...] = (scale * dx).astype(BF16)

        @pl.when(i == 0)
        def _():
            dg_ref[...] = dgain

        @pl.when(i > 0)
        def _():
            dg_ref[...] += dgain

    row = pl.BlockSpec((tm, D), lambda i: (i, 0))
    vec = pl.BlockSpec((1, D), lambda i: (0, 0))
    return _call(
        body, name, (T // tm,), [row, row, vec, row], [row, row, vec],
        [jax.ShapeDtypeStruct((T, D), F32), jax.ShapeDtypeStruct((T, D), BF16), jax.ShapeDtypeStruct((1, D), F32)],
        [], ("arbitrary",), (dn, x, gain, dres), carry)


def _loss_head(x3, gain, target, name):
    T, D = x3.shape
    tm = _tile(T, 256, 16)

    def body(x_ref, g_ref, t_ref, dx_ref, dxb_ref, ls_ref, dg_ref):
        i = pl.program_id(0)
        xv = x_ref[...]
        err = xv * _rstd(xv) * g_ref[...] - t_ref[...]
        sq = jnp.sum(jnp.sum(err * err, axis=1, keepdims=True), axis=0, keepdims=True)
        dv, dgain = _rms_bwd(err * (1.0 / D), xv, g_ref[...])
        dx_ref[...] = dv
        dxb_ref[...] = (FFN_RESIDUAL_SCALE * dv).astype(BF16)
        sqb = jnp.broadcast_to(sq, (1, LANES))

        @pl.when(i == 0)
        def _():
            dg_ref[...] = dgain
            ls_ref[...] = sqb

        @pl.when(i > 0)
        def _():
            dg_ref[...] += dgain
            ls_ref[...] += sqb

    row = pl.BlockSpec((tm, D), lambda i: (i, 0))
    vec = pl.BlockSpec((1, D), lambda i: (0, 0))
    return pl.pallas_call(
        body, name=name, grid=(T // tm,),
        in_specs=[row, vec, row],
        out_specs=[row, row, pl.BlockSpec((1, LANES), lambda i: (0, 0)), vec],
        out_shape=[jax.ShapeDtypeStruct((T, D), F32), jax.ShapeDtypeStruct((T, D), BF16),
                   jax.ShapeDtypeStruct((1, LANES), F32), jax.ShapeDtypeStruct((1, D), F32)],
        compiler_params=_params("arbitrary"),
    )(x3, gain, target)


def _norm_mm(x, gain, w, name, carry=None):
    T, D = x.shape
    N = w.shape[1]
    tm = _tile(T, 512, 16)
    tn = _tile(N, 2560, LANES)

    def body(x_ref, g_ref, w_ref, n_ref, z_ref):
        @pl.when(pl.program_id(1) == 0)
        def _():
            xv = x_ref[...]
            n_ref[...] = (xv * _rstd(xv) * g_ref[...]).astype(BF16)

        z_ref[...] = jnp.dot(n_ref[...], w_ref[...], preferred_element_type=F32).astype(BF16)

    return _call(
        body, name, (T // tm, N // tn),
        [pl.BlockSpec((tm, D), lambda i, j: (i, 0)),
         pl.BlockSpec((1, D), lambda i, j: (0, 0)),
         pl.BlockSpec((D, tn), lambda i, j: (0, j))],
        [pl.BlockSpec((tm, D), lambda i, j: (i, 0)),
         pl.BlockSpec((tm, tn), lambda i, j: (i, j))],
        [jax.ShapeDtypeStruct((T, D), BF16), jax.ShapeDtypeStruct((T, N), BF16)],
        [], ("parallel", "arbitrary"), (x, gain, w), carry)


def _ffn_up(n, G, wu, name, carry=None):
    T, D = n.shape
    FF = wu.shape[1]
    tm = _tile(T, 512, 16)
    tf = _tile(FF, 1408, LANES)

    def body(n_ref, G_ref, wu_ref, U_ref, H_ref):
        U = jnp.dot(n_ref[...], wu_ref[...], preferred_element_type=F32)
        Gv = G_ref[...].astype(F32)
        U_ref[...] = U.astype(BF16)
        H_ref[...] = (Gv * _sigmoid(Gv) * U).astype(BF16)

    act = pl.BlockSpec((tm, tf), lambda i, f: (i, f))
    return _call(
        body, name, (T // tm, FF // tf),
        [pl.BlockSpec((tm, D), lambda i, f: (i, 0)), act, pl.BlockSpec((D, tf), lambda i, f: (0, f))],
        [act, act], [jax.ShapeDtypeStruct((T, FF), BF16)] * 2, [], ("parallel", "arbitrary"), (n, G, wu), carry)


def _mm_fullk(a, w, trans_w, residual, out_dtype, name, carry=None, scale=1.0):
    T, K = a.shape
    N = w.shape[0] if trans_w else w.shape[1]
    tm = _tile(T, 512, 16)
    tn = _tile(N, 2048 * 2560 // K, LANES)

    def body(*refs):
        if residual is None:
            a_ref, w_ref, o_ref = refs
        else:
            a_ref, w_ref, r_ref, o_ref = refs
        if trans_w:
            acc = _dot_nt(a_ref[...], w_ref[...])
        else:
            acc = jnp.dot(a_ref[...], w_ref[...], preferred_element_type=F32)
        if scale != 1.0:
            acc = scale * acc
        if residual is not None:
            acc = acc + r_ref[...]
        o_ref[...] = acc.astype(out_dtype)

    w_spec = pl.BlockSpec((tn, K), lambda i, j: (j, 0)) if trans_w else pl.BlockSpec((K, tn), lambda i, j: (0, j))
    in_specs = [pl.BlockSpec((tm, K), lambda i, j: (i, 0)), w_spec]
    args = [a, w]
    if residual is not None:
        in_specs.append(pl.BlockSpec((tm, tn), lambda i, j: (i, j)))
        args.append(residual)
    return _call(body, name, (T // tm, N // tn), in_specs, [pl.BlockSpec((tm, tn), lambda i, j: (i, j))],
                 [jax.ShapeDtypeStruct((T, N), out_dtype)], [], ("parallel", "arbitrary"), args, carry)


def _mm_tn(a, b, name, carry=None):
    T, M = a.shape
    N = b.shape[1]
    tmw = _tile(M, 2048, LANES)
    tnw = _tile(N, 2048 * 1408 // tmw, LANES)
    tk = _tile(T, 1024, 16)
    nk = T // tk

    def body(a_ref, b_ref, o_ref, acc_ref):
        k = pl.program_id(2)

        @pl.when(k == 0)
        def _():
            acc_ref[...] = jnp.zeros_like(acc_ref)

        acc_ref[...] += _dot_tn(a_ref[...], b_ref[...])

        @pl.when(k == nk - 1)
        def _():
            o_ref[...] = acc_ref[...].astype(BF16)

    return _call(
        body, name, (M // tmw, N // tnw, nk),
        [pl.BlockSpec((tk, tmw), lambda i, j, k: (k, i)),
         pl.BlockSpec((tk, tnw), lambda i, j, k: (k, j))],
        [pl.BlockSpec((tmw, tnw), lambda i, j, k: (i, j))],
        [jax.ShapeDtypeStruct((M, N), BF16)],
        [pltpu.VMEM((tmw, tnw), F32)], ("parallel", "parallel", "arbitrary"), (a, b), carry)


def _mm_tn_pair(a, b, c_arr, axis, name, carry=None):
    T, M = a.shape
    N = b.shape[1]
    tk = _tile(T, 2048, 16)
    nk = T // tk
    nq = 4
    if axis == 1:
        rows, cols = M // 2, N // nq
        a_spec = pl.BlockSpec((tk, rows), lambda p, j, k, cr: (k, jnp.where(p == 0, 1 - cr[0], cr[0])))
        b_spec = pl.BlockSpec((tk, cols), lambda p, j, k, cr: (k, j))
        s_shape, land_shape = (1, rows, N), (rows, N)
        s_spec = pl.BlockSpec((None, rows, cols), lambda p, j, k, cr: (0, 0, j * p))
    else:
        rows, cols = M // nq, N // 2
        a_spec = pl.BlockSpec((tk, rows), lambda p, j, k, cr: (k, j))
        b_spec = pl.BlockSpec((tk, cols), lambda p, j, k, cr: (k, jnp.where(p == 0, 1 - cr[0], cr[0])))
        s_shape, land_shape = (nq, rows, cols), (nq, rows, cols)
        s_spec = pl.BlockSpec((None, rows, cols), lambda p, j, k, cr: (j * p, 0, 0))

    def body(c_ref, a_ref, b_ref, s_ref, land, acc_ref, stage, got, send_sems, recv_sems, loc_sem):
        p, j, k = pl.program_id(0), pl.program_id(1), pl.program_id(2)
        x, y, c = _me()

        def tile(jj):
            return land.at[:, pl.ds(jj * cols, cols)] if axis == 1 else land.at[jj]

        def send(jj):
            return _remote(stage, tile(jj), send_sems, recv_sems, jj, (x, y, 1 - c))

        @pl.when(k == 0)
        def _():
            acc_ref[...] = jnp.zeros_like(acc_ref)

        acc_ref[...] += _dot_tn(a_ref[...], b_ref[...])

        def fetch(jj):
            return pltpu.make_async_copy(tile(jj), got, loc_sem.at[0])

        for jj in range(nq):
            @pl.when(jnp.logical_and(k == nk - 1, jnp.logical_and(p == 0, j == jj)))
            def _():
                if jj > 0:
                    send(jj - 1).wait_send()
                stage[...] = acc_ref[...].astype(BF16)
                send(jj).start()

            @pl.when(jnp.logical_and(k == max(nk - 2, 0), jnp.logical_and(p == 1, j == jj)))
            def _():
                if jj == 0:
                    send(nq - 1).wait_send()
                send(jj).wait_recv()
                fetch(jj).start()

            @pl.when(jnp.logical_and(k == nk - 1, jnp.logical_and(p == 1, j == jj)))
            def _():
                fetch(jj).wait()
                s_ref[...] = (acc_ref[...] + got[...].astype(F32)).astype(BF16)

    return _call(
        body, name, (2, nq, nk), [a_spec, b_spec], [s_spec, ANY],
        [jax.ShapeDtypeStruct(s_shape, BF16), jax.ShapeDtypeStruct(land_shape, BF16)],
        [pltpu.VMEM((rows, cols), F32), pltpu.VMEM((rows, cols), BF16), pltpu.VMEM((rows, cols), BF16),
         pltpu.SemaphoreType.DMA((nq,)), pltpu.SemaphoreType.DMA((nq,)), pltpu.SemaphoreType.DMA((1,))],
        ("arbitrary", "arbitrary", "arbitrary"), (a, b), carry, (c_arr,))


GELU_K = math.sqrt(2.0 / math.pi)
GELU_C = 0.044715


def _gelu_and_grad(v):
    u = GELU_K * (v + GELU_C * v * v * v)
    th = jnp.tanh(u)
    g = 0.5 * v * (1.0 + th)
    dg = 0.5 * (1.0 + th) + 0.5 * v * (1.0 - th * th) * GELU_K * (1.0 + 3.0 * GELU_C * v * v)
    return g, dg


def _neg_expm1(v):
    poly = v * (1.0 + v * (0.5 + v * (1.0 / 6 + v * (1.0 / 24 + v * (1.0 / 120 + v * (1.0 / 720))))))
    return jnp.where(v > -0.25, -poly, 1.0 - jnp.exp(v))


def _softplus_neg(lam):
    e = jnp.exp(-jnp.abs(lam))
    log1pe = jnp.where(e < 1e-4, e * (1.0 - 0.5 * e), jnp.log(1.0 + e))
    sp = jnp.maximum(-lam, 0.0) + log1pe
    dsp = -1.0 / (1.0 + jnp.exp(lam))
    return sp, dsp


def _earlier(ext, j):
    return pltpu.roll(ext, j, 0)[SUBLANES:, :]


def _later(ext, j):
    n = ext.shape[0]
    return pltpu.roll(ext, n - j, 0)[:n - SUBLANES, :]


def _taps(v, halo, K):
    ext = jnp.concatenate([halo, v], axis=0)
    return [v] + [_earlier(ext, j) for j in range(1, K)]


def _block_diag(vb, w_ref, nh, hd):
    return jnp.concatenate(
        [jnp.dot(vb[:, h * hd:(h + 1) * hd], w_ref[h], preferred_element_type=F32) for h in range(nh)], axis=1)


def _lru_gates(xc, wa_ref, ba_ref, wi_ref, bi_ref, sp, nh, hd):
    xcb = xc.astype(BF16)
    r = _sigmoid(_block_diag(xcb, wa_ref, nh, hd) + ba_ref[...])
    ig = _sigmoid(_block_diag(xcb, wi_ref, nh, hd) + bi_ref[...])
    log_a = -LRU_C * r * sp
    a = jnp.exp(log_a)
    mult = jnp.sqrt(_neg_expm1(2.0 * log_a))
    return xcb, r, ig, a, mult


def _mix_fwd(z, cw, cb, wa, ba, wi, bi, lam, sw, glo, gso, name, carry=None):
    T = z.shape[0]
    DL = cb.shape[1]
    DS = gso.shape[1]
    NH, HD = wa.shape[0], wa.shape[1]
    KL, KS = cw.shape[0], sw.shape[0]
    tt = _tile(T, 128, 16)
    o_g, o_b, o_c, o_x = DL, 2 * DL, 2 * DL + DS, 2 * DL + 2 * DS

    def body(z_ref, cw_ref, cb_ref, wa_ref, ba_ref, wi_ref, bi_ref, lam_ref, sw_ref, glo_ref, gso_ref,
             h_ref, y_ref, cx_ref, cp_ref, ch_ref):
        @pl.when(pl.program_id(0) == 0)
        def _():
            cx_ref[...] = jnp.zeros_like(cx_ref)
            cp_ref[...] = jnp.zeros_like(cp_ref)
            ch_ref[...] = jnp.zeros_like(ch_ref)

        def zcol(o, n):
            return z_ref[:, o:o + n].astype(F32)

        lx = zcol(0, DL)
        xs = _taps(lx, cx_ref[...], KL)
        cx_ref[...] = lx[tt - SUBLANES:, :]
        xc = cb_ref[...] + xs[0] * cw_ref[KL - 1:KL, :]
        for j in range(1, KL):
            xc = xc + xs[j] * cw_ref[KL - 1 - j:KL - j, :]
        sp, _ = _softplus_neg(lam_ref[...])
        _, _, ig, a, mult = _lru_gates(xc, wa_ref, ba_ref, wi_ref, bi_ref, sp, NH, HD)
        b = mult * (ig * xc)
        rows = lax.broadcasted_iota(jnp.int32, (tt, DL), 0)
        s = 1
        while s < tt:
            keep = rows >= s
            b = jnp.where(keep, a * pltpu.roll(b, s, 0) + b, b)
            a = jnp.where(keep, a * pltpu.roll(a, s, 0), a)
            s *= 2
        h = a * ch_ref[SUBLANES - 1:SUBLANES, :] + b
        ch_ref[...] = h[tt - SUBLANES:, :]
        h_ref[...] = h
        ge, _ = _gelu_and_grad(zcol(o_g, DL))
        ylru = h * ge
        y_ref[:, 0:DL] = (ylru * _rstd(ylru) * glo_ref[...]).astype(BF16)

        p = zcol(o_c, DS) * zcol(o_x, DS)
        ps = _taps(p, cp_ref[...], KS)
        cp_ref[...] = p[tt - SUBLANES:, :]
        cv = ps[0] * sw_ref[KS - 1:KS, :]
        for j in range(1, KS):
            cv = cv + ps[j] * sw_ref[KS - 1 - j:KS - j, :]
        ysc = zcol(o_b, DS) * cv
        y_ref[:, DL:DL + DS] = (ysc * _rstd(ysc) * gso_ref[...]).astype(BF16)

    def full(shape):
        return pl.BlockSpec(shape, lambda t: (0,) * len(shape))

    return _call(
        body, name, (T // tt,),
        [pl.BlockSpec((tt, z.shape[1]), lambda t: (t, 0)),
         full(cw.shape), full(cb.shape), full(wa.shape), full(ba.shape), full(wi.shape), full(bi.shape),
         full(lam.shape), full(sw.shape), full(glo.shape), full(gso.shape)],
        [pl.BlockSpec((tt, DL), lambda t: (t, 0)), pl.BlockSpec((tt, DL + DS), lambda t: (t, 0))],
        [jax.ShapeDtypeStruct((T, DL), F32), jax.ShapeDtypeStruct((T, DL + DS), BF16)],
        [pltpu.VMEM((SUBLANES, DL), F32), pltpu.VMEM((SUBLANES, DS), F32), pltpu.VMEM((SUBLANES, DL), F32)],
        ("arbitrary",), (z, cw, cb, wa, ba, wi, bi, lam, sw, glo, gso), carry)


V_BA, V_BI, V_LAM, V_CB, V_CW, V_SW, V_GLO, V_GSO, V_ROWS = 0, 1, 2, 3, 4, 8, 11, 12, 16


def _mix_bwd(z, h, dy, cw, cb, wa, ba, wi, bi, lam, sw, glo, gso, name):
    T = z.shape[0]
    DL = cb.shape[1]
    DS = gso.shape[1]
    NH, HD = wa.shape[0], wa.shape[1]
    KL, KS = cw.shape[0], sw.shape[0]
    tt = _tile(T, 64, 16)
    nt = T // tt
    ZH = 2 * SUBLANES
    o_g, o_b, o_c, o_x = DL, 2 * DL, 2 * DL + DS, 2 * DL + 2 * DS

    def body(z_ref, zh_ref, h_ref, hh_ref, dy_ref, cw_ref, cb_ref, wa_ref, ba_ref, wi_ref, bi_ref, lam_ref,
             sw_ref, glo_ref, gso_ref, dz_ref, dwa_ref, dwi_ref, vec_ref, cdx_ref, cdc_ref, cdh_ref):
        i = pl.program_id(0)
        tr = nt - 1 - i

        @pl.when(i == 0)
        def _():
            dwa_ref[...] = jnp.zeros_like(dwa_ref)
            dwi_ref[...] = jnp.zeros_like(dwi_ref)
            vec_ref[...] = jnp.zeros_like(vec_ref)
            cdx_ref[...] = jnp.zeros_like(cdx_ref)
            cdc_ref[...] = jnp.zeros_like(cdc_ref)
            cdh_ref[...] = jnp.zeros_like(cdh_ref)

        def acc_row(r, v):
            vec_ref[pl.ds(r, 1), :] += jnp.sum(v, axis=0, keepdims=True)

        has_prev = tr > 0
        rows = lax.broadcasted_iota(jnp.int32, (tt, DL), 0)

        def zcol(o, n):
            return z_ref[:, o:o + n].astype(F32)

        def zhalo(o, n):
            return jnp.where(has_prev, zh_ref[:, o:o + n].astype(F32)[SUBLANES:, :], 0.0)

        lx = zcol(0, DL)
        xs = _taps(lx, zhalo(0, DL), KL)
        xc = cb_ref[...] + xs[0] * cw_ref[KL - 1:KL, :]
        for j in range(1, KL):
            xc = xc + xs[j] * cw_ref[KL - 1 - j:KL - j, :]
        sp, dsp = _softplus_neg(lam_ref[...])
        xcb, r, ig, a, mult = _lru_gates(xc, wa_ref, ba_ref, wi_ref, bi_ref, sp, NH, HD)
        hv = h_ref[...]
        hprev = _earlier(jnp.concatenate([jnp.where(has_prev, hh_ref[...], 0.0), hv], axis=0), 1)
        gate = zcol(o_g, DL)
        ge, dge = _gelu_and_grad(gate)
        ylru = hv * ge

        d_ylru, dglo = _rms_bwd(dy_ref[:, 0:DL].astype(F32), ylru, glo_ref[...])
        vec_ref[pl.ds(V_GLO, 1), :] += dglo
        dz_ref[:, o_g:o_g + DL] = (d_ylru * hv * dge).astype(BF16)
        bq = d_ylru * ge
        aq = jnp.where(rows == tt - 1, 1.0, pltpu.roll(a, tt - 1, 0))
        s = 1
        while s < tt:
            keep = rows < tt - s
            bq = jnp.where(keep, aq * pltpu.roll(bq, tt - s, 0) + bq, bq)
            aq = jnp.where(keep, aq * pltpu.roll(aq, tt - s, 0), aq)
            s *= 2
        dhh = bq + aq * cdh_ref[0:1, :]
        cdh_ref[0:1, :] = a[0:1, :] * dhh[0:1, :]

        da = dhh * hprev
        dmult = dhh * (ig * xc)
        d_i = dhh * mult * xc
        dxc = dhh * mult * ig
        dlog = da * a - dmult * (a * a) / mult
        acc_row(V_LAM, dlog * (-LRU_C * r) * dsp)
        dpa = dlog * (-LRU_C * sp) * r * (1.0 - r)
        dpi = d_i * ig * (1.0 - ig)
        acc_row(V_BA, dpa)
        acc_row(V_BI, dpi)
        dpab = dpa.astype(BF16)
        dpib = dpi.astype(BF16)
        back = []
        for hh in range(NH):
            sl = slice(hh * HD, (hh + 1) * HD)
            dwa_ref[hh] += _dot_tn(xcb[:, sl], dpab[:, sl])
            dwi_ref[hh] += _dot_tn(xcb[:, sl], dpib[:, sl])
            back.append(_dot_nt(dpab[:, sl], wa_ref[hh]) + _dot_nt(dpib[:, sl], wi_ref[hh]))
        dxc = dxc + jnp.concatenate(back, axis=1)

        acc_row(V_CB, dxc)
        extd = jnp.concatenate([dxc, cdx_ref[...]], axis=0)
        cdx_ref[...] = dxc[0:SUBLANES, :]
        dlx = dxc * cw_ref[KL - 1:KL, :]
        acc_row(V_CW + KL - 1, dxc * xs[0])
        for j in range(1, KL):
            dlx = dlx + _later(extd, j) * cw_ref[KL - 1 - j:KL - j, :]
            acc_row(V_CW + KL - 1 - j, dxc * xs[j])
        dz_ref[:, 0:DL] = dlx.astype(BF16)

        sb = zcol(o_b, DS)
        sc = zcol(o_c, DS)
        sx = zcol(o_x, DS)
        p = sc * sx
        ps = _taps(p, zhalo(o_c, DS) * zhalo(o_x, DS), KS)
        cv = ps[0] * sw_ref[KS - 1:KS, :]
        for j in range(1, KS):
            cv = cv + ps[j] * sw_ref[KS - 1 - j:KS - j, :]
        d_ysc, dgso = _rms_bwd(dy_ref[:, DL:DL + DS].astype(F32), sb * cv, gso_ref[...])
        vec_ref[pl.ds(V_GSO, 1), :] += dgso
        dz_ref[:, o_b:o_b + DS] = (d_ysc * cv).astype(BF16)
        dcv = d_ysc * sb
        extc = jnp.concatenate([dcv, cdc_ref[...]], axis=0)
        cdc_ref[...] = dcv[0:SUBLANES, :]
        dp = dcv * sw_ref[KS - 1:KS, :]
        acc_row(V_SW + KS - 1, dcv * ps[0])
        for j in range(1, KS):
            dp = dp + _later(extc, j) * sw_ref[KS - 1 - j:KS - j, :]
            acc_row(V_SW + KS - 1 - j, dcv * ps[j])
        dz_ref[:, o_c:o_c + DS] = (dp * sx).astype(BF16)
        dz_ref[:, o_x:o_x + DS] = (dp * sc).astype(BF16)

    def full(shape):
        return pl.BlockSpec(shape, lambda t: (0,) * len(shape))

    def rev(t):
        return nt - 1 - t

    def halo(t, rows):
        return jnp.maximum(rev(t) * (tt // rows) - 1, 0)

    return pl.pallas_call(
        body, name=name, grid=(nt,),
        in_specs=[pl.BlockSpec((tt, z.shape[1]), lambda t: (rev(t), 0)),
                  pl.BlockSpec((ZH, z.shape[1]), lambda t: (halo(t, ZH), 0)),
                  pl.BlockSpec((tt, DL), lambda t: (rev(t), 0)),
                  pl.BlockSpec((SUBLANES, DL), lambda t: (halo(t, SUBLANES), 0)),
                  pl.BlockSpec((tt, DL + DS), lambda t: (rev(t), 0)),
                  full(cw.shape), full(cb.shape), full(wa.shape), full(ba.shape), full(wi.shape), full(bi.shape),
                  full(lam.shape), full(sw.shape), full(glo.shape), full(gso.shape)],
        out_specs=[pl.BlockSpec((tt, z.shape[1]), lambda t: (rev(t), 0)),
                   full(wa.shape), full(wi.shape), full((V_ROWS, DL))],
        out_shape=[jax.ShapeDtypeStruct(z.shape, BF16), jax.ShapeDtypeStruct(wa.shape, F32),
                   jax.ShapeDtypeStruct(wi.shape, F32), jax.ShapeDtypeStruct((V_ROWS, DL), F32)],
        scratch_shapes=[pltpu.VMEM((SUBLANES, DL), F32), pltpu.VMEM((SUBLANES, DS), F32),
                        pltpu.VMEM((SUBLANES, DL), F32)],
        compiler_params=_params("arbitrary"),
    )(z, z, h, h, dy, cw, cb, wa, ba, wi, bi, lam, sw, glo, gso)


def _pair_add(p, r1, c, name):
    G, R, C = r1.shape
    tr = _tile(R, 256, 16)
    tc = _tile(C, 1408, LANES)

    def body(c_ref, p_ref, r_ref, o_ref):
        o_ref[...] = (p_ref[...].astype(F32) + r_ref[...].astype(F32)).astype(BF16)

    blk = (None, tr, tc)
    return pl.pallas_call(
        body, name=name,
        grid_spec=pltpu.PrefetchScalarGridSpec(
            num_scalar_prefetch=1, grid=(G, R // tr, C // tc),
            in_specs=[pl.BlockSpec(blk, lambda g, i, j, cr: (2 * g + cr[0], i, j)),
                      pl.BlockSpec(blk, lambda g, i, j, cr: (g, i, j))],
            out_specs=pl.BlockSpec(blk, lambda g, i, j, cr: (g, i, j))),
        out_shape=jax.ShapeDtypeStruct((G, R, C), BF16),
        compiler_params=_params("parallel", "parallel", "parallel"),
    )(c, p, r1)


def _quad_add(s, r2, qc, axis, name):
    _, R, W = r2.shape
    tr = _tile(R, 256, 16)

    def body(qc_ref, s_ref, r0_ref, r1_ref, r2_ref, o_ref):
        o_ref[...] = ((s_ref[...].astype(F32) + r0_ref[...].astype(F32)) + r1_ref[...].astype(F32)) + r2_ref[...].astype(F32)

    blk = (None, tr, W)
    if axis == 1:
        own = pl.BlockSpec(blk, lambda i, qr: (0, i, qr[0]))
    else:
        own = pl.BlockSpec(blk, lambda i, qr: (qr[0], i, 0))
    return pl.pallas_call(
        body, name=name,
        grid_spec=pltpu.PrefetchScalarGridSpec(
            num_scalar_prefetch=1, grid=(R // tr,),
            in_specs=[own] + [pl.BlockSpec(blk, lambda i, qr, j=j: (j, i, 0)) for j in range(3)],
            out_specs=pl.BlockSpec(blk, lambda i, qr: (qr[1], i, 0))),
        out_shape=jax.ShapeDtypeStruct((2, R, W), F32),
        compiler_params=_params("parallel"),
    )(qc, s, r2, r2, r2)


CAST_BLOCKS = 4


def _cast_into_full(shards, qc, axes, name, carry=None):
    M = len(shards)
    nb = CAST_BLOCKS

    def body(qc_ref, *refs):
        for s_ref, o_ref in zip(refs[:M], refs[M:]):
            o_ref[...] = s_ref[...].astype(BF16)

    in_specs, out_specs, out_shape = [], [], []
    for s, ax in zip(shards, axes):
        R, W = s.shape
        Rh = R // 2
        tr = Rh // nb
        in_specs.append(pl.BlockSpec((tr, W), lambda hf, i, qr: (hf * nb + i, 0)))
        if ax == 1:
            out_shape.append(jax.ShapeDtypeStruct((2, Rh, 4 * W), BF16))
            out_specs.append(pl.BlockSpec((None, tr, W), lambda hf, i, qr: (hf, i, qr[0])))
        else:
            out_shape.append(jax.ShapeDtypeStruct((8, Rh, W), BF16))
            out_specs.append(pl.BlockSpec((None, tr, W), lambda hf, i, qr: (2 * qr[0] + hf, i, 0)))
    return _call(body, name, (2, nb), in_specs, out_specs, out_shape, [], ("parallel", "parallel"), shards, carry, (qc,))


def _adamw(w, g, m, v, name):
    R, C = w.shape
    tr = _tile(R, 256, SUBLANES)
    tc = C // 2 if g.ndim == 3 else _tile(C, 2048, LANES)
    c1 = 1.0 - ADAM_B1 ** ADAM_STEP
    c2 = 1.0 - ADAM_B2 ** ADAM_STEP

    def body(w_ref, g_ref, m_ref, v_ref, d_ref, mo_ref, vo_ref, go_ref):
        gv = g_ref[...]
        go_ref[...] = gv
        mn = ADAM_B1 * m_ref[...] + (1.0 - ADAM_B1) * gv
        vn = ADAM_B2 * v_ref[...] + (1.0 - ADAM_B2) * (gv * gv)
        mo_ref[...] = mn
        vo_ref[...] = vn
        d_ref[...] = -ADAM_LR * ((mn / c1) / (jnp.sqrt(vn / c2) + ADAM_EPS) + ADAM_WD * w_ref[...])

    blk = pl.BlockSpec((tr, tc), lambda i, j: (i, j))
    g_blk = pl.BlockSpec((None, tr, tc), lambda i, j: (j, i, 0)) if g.ndim == 3 else blk
    sh = jax.ShapeDtypeStruct((R, C), F32)
    return pl.pallas_call(
        body, name=name, grid=(R // tr, C // tc),
        in_specs=[blk, g_blk, blk, blk], out_specs=[blk] * 4, out_shape=[sh] * 4,
        compiler_params=_params("parallel", "parallel"),
    )(w, g, m, v)


def _other_chips(x, y):
    return [(1 - x, y), (x, 1 - y), (1 - x, 1 - y)]


def _remote(src, dst, send_sems, recv_sems, idx, dev):
    return pltpu.make_async_remote_copy(src_ref=src, dst_ref=dst, send_sem=send_sems.at[idx], recv_sem=recv_sems.at[idx],
                                        device_id=dev, device_id_type=MESH)


def _gather_carry(fulls, axes):
    M = len(fulls)

    def win(outs, m, qq, cc):
        if axes[m] == 1:
            W = fulls[m].shape[2] // 4
            return outs[m].at[cc, :, pl.ds(pl.multiple_of(qq * W, LANES), W)]
        return outs[m].at[2 * qq + cc]

    def ici(outs, sems, m, j, src_q):
        x, y, c = _me()
        cx, cy = _other_chips(x, y)[j]
        blk = win(outs, m, src_q, c)
        return _remote(blk, blk, sems[0], sems[1], 6 * m + j, (cx, cy, c))

    def d2d(outs, sems, m, j, half):
        x, y, c = _me()
        cx, cy = _other_chips(x, y)[j]
        blk = win(outs, m, 2 * cx + cy, half)
        return _remote(blk, blk, sems[0], sems[1], 6 * m + 3 + j, (x, y, 1 - c))

    def start(ins, outs, sems):
        x, y, c = _me()
        for m in range(M):
            for j in range(3):
                ici(outs, sems, m, j, 2 * x + y).start()

    def middle(ins, outs, sems):
        x, y, c = _me()
        for m in range(M):
            for j, (cx, cy) in enumerate(_other_chips(x, y)):
                ici(outs, sems, m, j, 2 * cx + cy).wait_recv()
                d2d(outs, sems, m, j, c).start()

    def finish(ins, outs, sems):
        x, y, c = _me()
        for m in range(M):
            for j in range(3):
                d2d(outs, sems, m, j, 1 - c).wait_recv()
        for m in range(M):
            for j in range(3):
                ici(outs, sems, m, j, 2 * x + y).wait_send()
                d2d(outs, sems, m, j, c).wait_send()

    return _Carry(fulls, [jax.ShapeDtypeStruct(f.shape, f.dtype) for f in fulls], {m: m for m in range(M)},
                  [pltpu.SemaphoreType.DMA((6 * M,)), pltpu.SemaphoreType.DMA((6 * M,))], start, finish, middle)


def _gather_two_way_carry(fulls, axes):
    M = len(fulls)

    def part(outs, m, qq, cc, p):
        Rp = fulls[m].shape[1] // 2
        rows = pl.ds(p * Rp, Rp)
        if axes[m] == 1:
            W = fulls[m].shape[2] // 4
            return outs[m].at[cc, rows, pl.ds(pl.multiple_of(qq * W, LANES), W)]
        return outs[m].at[2 * qq + cc, rows, :]

    def copies(outs, sems):
        x, y, c = _me()
        q, qx, qy, qd = 2 * x + y, 2 * (1 - x) + y, 2 * x + (1 - y), 2 * (1 - x) + (1 - y)
        xn, yn, sib = (1 - x, y, c), (x, 1 - y, c), (x, y, 1 - c)
        table = {}
        for m in range(M):
            def cp(blk, k, dev):
                return _remote(blk, blk, sems[0], sems[1], 12 * m + k, dev)
            own0, own1 = part(outs, m, q, c, 0), part(outs, m, q, c, 1)
            table[m] = dict(
                to=[cp(own0, 0, xn), cp(own1, 1, yn), cp(own1, 2, xn), cp(own0, 3, yn)],
                landed=[cp(part(outs, m, qx, c, 0), 0, xn), cp(part(outs, m, qy, c, 1), 1, yn),
                        cp(part(outs, m, qx, c, 1), 2, xn), cp(part(outs, m, qy, c, 0), 3, yn),
                        cp(part(outs, m, qd, c, 0), 4, yn), cp(part(outs, m, qd, c, 1), 5, xn)],
                passed=[cp(part(outs, m, qx, c, 0), 4, yn), cp(part(outs, m, qy, c, 1), 5, xn)],
                handed=[cp(part(outs, m, qq, c, p), 6 + k, sib)
                        for k, (qq, p) in enumerate([(qx, 0), (qy, 1), (qx, 1), (qy, 0), (qd, 0), (qd, 1)])],
                taken=[cp(part(outs, m, qq, 1 - c, p), 6 + k, sib)
                       for k, (qq, p) in enumerate([(qx, 0), (qy, 1), (qx, 1), (qy, 0), (qd, 0), (qd, 1)])])
        return table

    def start(ins, outs, sems):
        t = copies(outs, sems)
        for m in range(M):
            for cp in t[m]['to']:
                cp.start()

    def finish(ins, outs, sems):
        t = copies(outs, sems)
        for m in range(M):
            for k in range(4):
                t[m]['landed'][k].wait_recv()
                if k < 2:
                    t[m]['passed'][k].start()
                t[m]['handed'][k].start()
        for m in range(M):
            for k in (4, 5):
                t[m]['landed'][k].wait_recv()
                t[m]['handed'][k].start()
        for m in range(M):
            for cp in t[m]['taken']:
                cp.wait_recv()
            for cp in t[m]['to'] + t[m]['passed'] + t[m]['handed']:
                cp.wait_send()

    return _Carry(fulls, [jax.ShapeDtypeStruct(f.shape, f.dtype) for f in fulls], {m: m for m in range(M)},
                  [pltpu.SemaphoreType.DMA((12 * M,)), pltpu.SemaphoreType.DMA((12 * M,))], start, finish)


def _pair_exchange_carry(parts):
    M = len(parts)
    groups = [p.shape[0] // 2 for p in parts]
    base = [sum(groups[:m]) for m in range(M)]
    out_shape = [jax.ShapeDtypeStruct((g,) + p.shape[1:], p.dtype) for g, p in zip(groups, parts)]

    def copies(ins, outs, sems):
        x, y, c = _me()
        return [_remote(ins[m].at[2 * g + 1 - c], outs[m].at[g], sems[0], sems[1], base[m] + g, (x, y, 1 - c))
                for m in range(M) for g in range(groups[m])]

    def start(ins, outs, sems):
        for cp in copies(ins, outs, sems):
            cp.start()

    def finish(ins, outs, sems):
        for cp in copies(ins, outs, sems):
            cp.wait()

    n = sum(groups)
    return _Carry(parts, out_shape, {}, [pltpu.SemaphoreType.DMA((n,)), pltpu.SemaphoreType.DMA((n,))], start, finish)


def _chip_exchange_carry(sums, axes):
    M = len(sums)
    out_shape = []
    for s, ax in zip(sums, axes):
        _, Rh, C = s.shape
        out_shape.append(jax.ShapeDtypeStruct((3, Rh, C // 4 if ax == 1 else C), s.dtype))

    def copies(ins, outs, sems):
        x, y, c = _me()
        cps = []
        for m in range(M):
            for j, (cx, cy) in enumerate(_other_chips(x, y)):
                qj = 2 * cx + cy
                if axes[m] == 1:
                    W = sums[m].shape[2] // 4
                    src = ins[m].at[0, :, pl.ds(pl.multiple_of(qj * W, LANES), W)]
                else:
                    src = ins[m].at[qj]
                cps.append(_remote(src, outs[m].at[j], sems[0], sems[1], 3 * m + j, (cx, cy, c)))
        return cps

    def start(ins, outs, sems):
        for cp in copies(ins, outs, sems):
            cp.start()

    def finish(ins, outs, sems):
        for cp in copies(ins, outs, sems):
            cp.wait()

    return _Carry(sums, out_shape, {}, [pltpu.SemaphoreType.DMA((3 * M,)), pltpu.SemaphoreType.DMA((3 * M,))],
                  start, finish)


def _pair_share_carry(bufs):
    M = len(bufs)

    def start(ins, outs, sems):
        x, y, c = _me()
        for m in range(M):
            _remote(outs[m].at[c], outs[m].at[c], sems[0], sems[1], m, (x, y, 1 - c)).start()

    def finish(ins, outs, sems):
        x, y, c = _me()
        for m in range(M):
            _remote(outs[m].at[c], outs[m].at[c], sems[0], sems[1], m, (x, y, 1 - c)).wait_send()
            _remote(outs[m].at[1 - c], outs[m].at[1 - c], sems[0], sems[1], m, (x, y, 1 - c)).wait_recv()

    return _Carry(bufs, [jax.ShapeDtypeStruct(b.shape, b.dtype) for b in bufs], {m: m for m in range(M)},
                  [pltpu.SemaphoreType.DMA((M,)), pltpu.SemaphoreType.DMA((M,))], start, finish)


def _allreduce_small(v, name):
    R, W = v.shape
    Rh = R // 2

    def body(v_ref, o_ref, sib, quad, send_sems, recv_sems):
        x, y, c = _me()
        q = 2 * x + y
        sibling = (x, y, 1 - c)
        pair = _remote(v_ref, sib, send_sems, recv_sems, 0, sibling)
        pair.start()
        pair.wait()
        mine = pl.ds(pl.multiple_of(c * Rh, SUBLANES), Rh)
        quad[0] = v_ref[mine, :] + sib[mine, :]
        cps = []
        for k in (1, 2, 3):
            peer = (1 - x if k & 2 else x, 1 - y if k & 1 else y, c)
            cps.append(_remote(quad.at[0], quad.at[k], send_sems, recv_sems, k, peer))
            cps[-1].start()
        for cp in cps:
            cp.wait()
        acc = quad[q]
        for p in (1, 2, 3):
            acc = acc + quad[jnp.bitwise_xor(q, p)]
        o_ref[mine, :] = acc
        theirs = pl.ds(pl.multiple_of((1 - c) * Rh, SUBLANES), Rh)
        done = _remote(o_ref.at[mine, :], o_ref.at[mine, :], send_sems, recv_sems, 4, sibling)
        done.start()
        done.wait_send()
        _remote(o_ref.at[theirs, :], o_ref.at[theirs, :], send_sems, recv_sems, 4, sibling).wait_recv()

    vm = pl.BlockSpec(memory_space=pltpu.VMEM)
    return pl.pallas_call(
        body, name=name, in_specs=[vm], out_specs=vm, out_shape=jax.ShapeDtypeStruct((R, W), F32),
        scratch_shapes=[pltpu.VMEM((R, W), F32), pltpu.VMEM((4, Rh, W), F32),
                        pltpu.SemaphoreType.DMA((5,)), pltpu.SemaphoreType.DMA((5,))],
        compiler_params=pltpu.CompilerParams(vmem_limit_bytes=VMEM_LIMIT),
    )(v)


def _pack(pieces):
    flat = []
    for p in pieces:
        p = p.reshape(-1).astype(F32)
        pad = (-p.shape[0]) % PACK_ALIGN
        flat.append(jnp.pad(p, (0, pad)).reshape(-1, PACK_W))
    if sum(f.shape[0] for f in flat) % (2 * SUBLANES):
        flat.append(jnp.zeros((SUBLANES, PACK_W), F32))
    return jnp.concatenate(flat, axis=0)


def _unpack(packed, shapes):
    out, row = [], 0
    for shp in shapes:
        n = math.prod(shp)
        rows = -(-n // PACK_ALIGN) * SUBLANES
        out.append(packed[row:row + rows].reshape(-1)[:n].reshape(shp))
        row += rows
    return out


def kernel(x, ffn1_norm, ffn1_w_gate, ffn1_w_up, ffn1_w_down, mix_norm, w_in, lru_conv_w, lru_conv_b, lru_w_a, lru_b_a, lru_w_i, lru_b_i, lru_lambda, sc_conv_w, lru_out_norm, sc_out_norm, w_out, ffn2_norm, ffn2_w_gate, ffn2_w_up, ffn2_w_down, final_norm, loss_target, m_ffn1_norm, m_ffn1_w_gate, m_ffn1_w_up, m_ffn1_w_down, m_mix_norm, m_w_in, m_lru_conv_w, m_lru_conv_b, m_lru_w_a, m_lru_b_a, m_lru_w_i, m_lru_b_i, m_lru_lambda, m_sc_conv_w, m_lru_out_norm, m_sc_out_norm, m_w_out, m_ffn2_norm, m_ffn2_w_gate, m_ffn2_w_up, m_ffn2_w_down, m_final_norm, v_ffn1_norm, v_ffn1_w_gate, v_ffn1_w_up, v_ffn1_w_down, v_mix_norm, v_w_in, v_lru_conv_w, v_lru_conv_b, v_lru_w_a, v_lru_b_a, v_lru_w_i, v_lru_b_i, v_lru_lambda, v_sc_conv_w, v_lru_out_norm, v_sc_out_norm, v_w_out, v_ffn2_norm, v_ffn2_w_gate, v_ffn2_w_up, v_ffn2_w_down, v_final_norm):
    vals = locals()
    w = {n: vals[n] for n in WEIGHTS}
    mom = {n: vals["m_" + n] for n in WEIGHTS}
    var = {n: vals["v_" + n] for n in WEIGHTS}

    xi, yi, ci = _me()
    qi = 2 * xi + yi
    c_arr = jnp.reshape(ci, (1,)).astype(jnp.int32)
    qc_arr = jnp.stack([qi, ci]).astype(jnp.int32)

    T, D = x.shape[1], x.shape[2]
    xt = x.reshape(T, D)
    target = loss_target.reshape(T, D)
    DL = lru_conv_b.shape[-1]
    NH, HD = lru_w_a.shape[1], lru_w_a.shape[2]
    KL, KS = lru_conv_w.shape[1], sc_conv_w.shape[1]
    DLq = lru_conv_w.shape[2]

    axis_of = dict(zip(BIG, BIG_AXIS))
    placed, full = {}, {}

    def gather(names):
        return _gather_carry([placed[n] for n in names], [axis_of[n] for n in names])

    def gathered(names, views):
        for n, g in zip(names, views):
            full[n] = g.reshape(2 * g.shape[1], g.shape[2]) if axis_of[n] == 1 else g.reshape(8 * g.shape[1], g.shape[2])

    first, rest = 'ffn1_w_gate', [n for n in BIG if n != 'ffn1_w_gate']
    placed[first] = _cast_into_full([w[first][0]], qc_arr, [axis_of[first]], "cast_first_weight")[0]
    res = _cast_into_full([w[n][0] for n in rest], qc_arr, [axis_of[n] for n in rest], "cast_other_weights",
                          _gather_two_way_carry([placed[first]], [axis_of[first]]))
    placed.update(zip(rest, res[:len(rest)]))
    gathered([first], res[len(rest):])

    taps = jnp.zeros((2 * SUBLANES, DL), F32)
    taps = lax.dynamic_update_slice(taps, lru_conv_w[0], (0, qi * DLq))
    taps = lax.dynamic_update_slice(taps, sc_conv_w[0], (KL, qi * DLq))
    taps = _allreduce_small(jnp.where(ci == 0, taps, 0.0), "gather_conv_taps")
    cw, sw = taps[0:KL], taps[KL:KL + KS]

    cb = lru_conv_b
    wa, wi = lru_w_a[0].astype(BF16), lru_w_i[0].astype(BF16)
    ba, bi = lru_b_a.reshape(1, DL), lru_b_i.reshape(1, DL)
    mix_args = (cw, cb, wa, ba, wi, bi, lru_lambda, sw, lru_out_norm, sc_out_norm)
    gf = final_norm.reshape(1, D)

    names = ['ffn1_w_up']
    res = _norm_mm(xt, ffn1_norm, full['ffn1_w_gate'], "ffn1_gate", gather(names))
    n1, G1 = res[:2]
    gathered(names, res[2:])
    names = ['ffn1_w_down', 'w_out']
    res = _ffn_up(n1, G1, full['ffn1_w_up'], "ffn1_up", gather(names))
    U1, H1f = res[:2]
    gathered(names, res[2:])
    names = ['w_in', 'ffn2_w_down']
    res = _mm_fullk(H1f, full['ffn1_w_down'], False, xt, F32, "ffn1_down", gather(names), FFN_RESIDUAL_SCALE)
    x1 = res[0]
    gathered(names, res[1:])
    names = ['ffn2_w_gate']
    res = _norm_mm(x1, mix_norm, full['w_in'], "mix_in_proj", gather(names))
    n2, z = res[:2]
    gathered(names, res[2:])
    names = ['ffn2_w_up']
    res = _mix_fwd(z, *mix_args, "mix_fwd", gather(names))
    h, ymix = res[:2]
    gathered(names, res[2:])
    x2 = _mm_fullk(ymix, full['w_out'], False, x1, F32, "mix_out_proj")[0]
    x3, n3, G2, U2 = _ffn_fwd(x2, ffn2_norm, full['ffn2_w_gate'], full['ffn2_w_up'], full['ffn2_w_down'], "ffn2_fwd")
    dx3, d3b, sqerr, dgf = _loss_head(x3, gf, target, "loss_head")

    sums, halves, shared = {}, {}, {}

    def dw(n, a, b, name, carry=None):
        res = _mm_tn_pair(a, b, c_arr, axis_of[n], name, carry)
        sums[n] = res[0]
        return res[2:]

    def chip_carry(names):
        return _chip_exchange_carry([sums[n] for n in names], [axis_of[n] for n in names])

    def chip_add(names, recv):
        for n, r in zip(names, recv):
            halves[n] = _quad_add(sums[n], r, qc_arr, axis_of[n], "grad_chip_add_" + n)

    dG2, dU2, H2 = _ffn_bwd_hidden(d3b, G2, U2, full['ffn2_w_down'], "ffn2_bwd_hidden")
    dn3 = _ffn_bwd_input(dG2, dU2, full['ffn2_w_gate'], full['ffn2_w_up'], "ffn2_bwd_input")[0]
    dx2, dx2b, dg_ffn2 = _rms_bwd_res(dn3, x2, ffn2_norm, dx3, 1.0, "ffn2_norm_bwd")
    dw('ffn2_w_gate', n3, dG2, "ffn2_dwg")
    dw('ffn2_w_up', n3, dU2, "ffn2_dwu")
    dw('ffn2_w_down', H2, d3b, "ffn2_dwd")
    dy = _mm_fullk(dx2b, full['w_out'], True, None, BF16, "mix_out_bwd")[0]
    dz, dwa, dwi, vec = _mix_bwd(z, h, dy, *mix_args, "mix_bwd")
    dg_mix, dx1, d1b = _mm_nt_norm_bwd(dz, full['w_in'], x1, mix_norm, dx2, FFN_RESIDUAL_SCALE, "mix_in_bwd")

    def share_carry(names):
        return _pair_share_carry([halves[n] for n in names])

    res = _ffn_bwd_hidden(d1b, G1, U1, full['ffn1_w_down'], "ffn1_bwd_hidden", chip_carry(['ffn2_w_gate']))
    dG1, dU1, H1 = res[:3]
    chip_add(['ffn2_w_gate'], res[3:])
    res = _ffn_bwd_input(dG1, dU1, full['ffn1_w_gate'], full['ffn1_w_up'], "ffn1_bwd_input",
                         chip_carry(['ffn2_w_up', 'ffn2_w_down']))
    dn1 = res[0]
    chip_add(['ffn2_w_up', 'ffn2_w_down'], res[1:])
    dx0, _, dg_ffn1 = _rms_bwd_res(dn1, xt, ffn1_norm, dx1, 1.0, "ffn1_norm_bwd")
    names = ['ffn2_w_gate', 'ffn2_w_up', 'ffn2_w_down']
    shared.update(zip(names, dw('ffn1_w_gate', n1, dG1, "ffn1_dwg", share_carry(names))))
    chip_add(['ffn1_w_gate'], dw('ffn1_w_up', n1, dU1, "ffn1_dwu", chip_carry(['ffn1_w_gate'])))
    chip_add(['ffn1_w_up'], dw('ffn1_w_down', H1, d1b, "ffn1_dwd", chip_carry(['ffn1_w_up'])))
    names = ['ffn1_w_gate', 'ffn1_w_up']
    res = dw('w_in', n2, dz, "mix_dwin", _merge_carries(chip_carry(['ffn1_w_down']), share_carry(names)))
    chip_add(['ffn1_w_down'], res[:1])
    shared.update(zip(names, res[1:]))
    res = dw('w_out', ymix, dx2b, "mix_dwout", _merge_carries(chip_carry(['w_in']), share_carry(['ffn1_w_down'])))
    chip_add(['w_in'], res[:1])
    shared['ffn1_w_down'] = res[1]
    chip_add(['w_out'], _run_carry(chip_carry(['w_out']), "grad_chip_exchange_w_out"))
    names = ['w_in', 'w_out']
    shared.update(zip(names, _run_carry(share_carry(names), "grad_pair_share")))
    out_g, out_d, out_m, out_v = {}, {}, {}, {}
    for n in BIG:
        shp = w[n].shape
        g = shared[n].reshape(shp[1], shp[2]) if axis_of[n] == 1 else shared[n]
        outs = _adamw(w[n][0], g, mom[n][0], var[n][0], "adamw_" + n)
        out_d[n], out_m[n], out_v[n], out_g[n] = (a.reshape(shp) for a in outs)

    small = [n for n in WEIGHTS if n not in BIG]
    local_small = {
        'ffn1_norm': dg_ffn1, 'mix_norm': dg_mix, 'lru_conv_w': vec[V_CW:V_CW + KL], 'lru_conv_b': vec[V_CB],
        'lru_w_a': dwa, 'lru_b_a': vec[V_BA], 'lru_w_i': dwi, 'lru_b_i': vec[V_BI], 'lru_lambda': vec[V_LAM],
        'sc_conv_w': vec[V_SW:V_SW + KS], 'lru_out_norm': vec[V_GLO], 'sc_out_norm': vec[V_GSO],
        'ffn2_norm': dg_ffn2, 'final_norm': dgf,
    }
    full_shapes = [local_small[n].shape for n in small] + [(1,)]
    reduced = _allreduce_small(_pack([local_small[n] for n in small] + [sqerr[0, 0:1]]), "allreduce_small")
    reduced = _unpack(reduced, full_shapes)
    loss = (0.5 / D) * reduced[-1][0]
    gsm = {}
    for n, g in zip(small, reduced[:-1]):
        if n in SMALL_SHARDED:
            g = lax.dynamic_slice(g, (0, qi * DLq), (g.shape[0], DLq))
        gsm[n] = g.reshape(w[n].shape)
    small_shapes = [w[n].shape for n in small]
    d_s, m_s, v_s, _ = _adamw(_pack([w[n] for n in small]), _pack([gsm[n] for n in small]),
                              _pack([mom[n] for n in small]), _pack([var[n] for n in small]), "adamw_small")
    for n, d, mn, vn in zip(small, _unpack(d_s, small_shapes), _unpack(m_s, small_shapes), _unpack(v_s, small_shapes)):
        out_g[n], out_d[n], out_m[n], out_v[n] = gsm[n], d, mn, vn

    return (loss, dx0.reshape(x.shape), *[out_g[n] for n in WEIGHTS], *[out_d[n] for n in WEIGHTS],
            *[out_m[n] for n in WEIGHTS], *[out_v[n] for n in WEIGHTS])
```

```python
import math

import jax
import jax.numpy as jnp
from jax import lax
from jax.experimental import pallas as pl
from jax.experimental.pallas import tpu as pltpu

F32 = jnp.float32
BF16 = jnp.bfloat16
MESH = pl.DeviceIdType.MESH
ANY = pl.BlockSpec(memory_space=pl.ANY)

NORM_EPS = 1e-6
LRU_C = 8.0
FFN_RESIDUAL_SCALE = 0.5
ADAM_LR = 0.001
ADAM_B1 = 0.9
ADAM_B2 = 0.999
ADAM_EPS = 1e-08
ADAM_WD = 0.01
ADAM_STEP = 10

V7X_VMEM_BYTES = 64 * 2**20
VMEM_LIMIT = V7X_VMEM_BYTES - 8 * 2**20
LANES = 128
SUBLANES = 8
PACK_W = LANES
PACK_ALIGN = SUBLANES * PACK_W

WEIGHTS = ['ffn1_norm', 'ffn1_w_gate', 'ffn1_w_up', 'ffn1_w_down', 'mix_norm', 'w_in', 'lru_conv_w', 'lru_conv_b',
           'lru_w_a', 'lru_b_a', 'lru_w_i', 'lru_b_i', 'lru_lambda', 'sc_conv_w', 'lru_out_norm', 'sc_out_norm',
           'w_out', 'ffn2_norm', 'ffn2_w_gate', 'ffn2_w_up', 'ffn2_w_down', 'final_norm']
BIG = ['ffn1_w_gate', 'ffn1_w_up', 'ffn1_w_down', 'w_in', 'w_out', 'ffn2_w_gate', 'ffn2_w_up', 'ffn2_w_down']
BIG_AXIS = [1, 1, 0, 1, 0, 1, 1, 0]
SMALL_SHARDED = ['lru_conv_w', 'sc_conv_w']


def _tile(n, pref, mult):
    if n <= pref:
        return n
    t = (pref // mult) * mult
    while t >= mult:
        if n % t == 0:
            return t
        t -= mult
    return n


def _params(*sem):
    return pltpu.CompilerParams(dimension_semantics=sem, vmem_limit_bytes=VMEM_LIMIT)


def _me():
    return lax.axis_index("x"), lax.axis_index("y"), lax.axis_index("c")


def _sigmoid(v):
    return 1.0 / (1.0 + jnp.exp(-v))


def _rstd(v):
    return lax.rsqrt(jnp.mean(v * v, axis=-1, keepdims=True) + NORM_EPS)


def _rms_bwd(dy, v, gain):
    r = _rstd(v)
    w = gain * dy
    dv = r * w - v * (r * r * r) * jnp.mean(v * w, axis=-1, keepdims=True)
    dgain = jnp.sum(dy * v * r, axis=0, keepdims=True)
    return dv, dgain


def _dot_nt(a, b):
    return lax.dot_general(a, b, (((1,), (1,)), ((), ())), preferred_element_type=F32)


def _dot_tn(a, b):
    return lax.dot_general(a, b, (((0,), (0,)), ((), ())), preferred_element_type=F32)


class _Carry:
    def __init__(self, inputs, out_shape, aliases, sems, start, finish, middle=None, middle_at=0.85):
        self.inputs, self.out_shape, self.aliases, self.sems = list(inputs), list(out_shape), dict(aliases), list(sems)
        self.start, self.finish = start, finish
        self.middle, self.middle_at = middle, middle_at


def _merge_carries(a, b):
    ia, oa, sa = len(a.inputs), len(a.out_shape), len(a.sems)
    aliases = dict(a.aliases)
    aliases.update({ia + i: oa + j for i, j in b.aliases.items()})

    def both(which):
        def run(ins, outs, sems):
            getattr(a, which)(ins[:ia], outs[:oa], sems[:sa])
            getattr(b, which)(ins[ia:], outs[oa:], sems[sa:])
        return run

    return _Carry(a.inputs + b.inputs, a.out_shape + b.out_shape, aliases, a.sems + b.sems, both("start"), both("finish"))


def _call(body, name, grid, in_specs, out_specs, out_shape, scratch_shapes, semantics, args, carry=None, prefetch=()):
    np_ = len(prefetch)
    if carry is None:
        spec = pltpu.PrefetchScalarGridSpec(num_scalar_prefetch=np_, grid=grid, in_specs=in_specs, out_specs=out_specs,
                                            scratch_shapes=scratch_shapes)
        return pl.pallas_call(body, name=name, grid_spec=spec, out_shape=out_shape,
                              compiler_params=_params(*semantics))(*prefetch, *args)
    ni, no, ns = len(in_specs), len(out_specs), len(scratch_shapes)
    ci, co = len(carry.inputs), len(carry.out_shape)

    def carrying(*refs):
        pre, refs = refs[:np_], refs[np_:]
        ins, refs = refs[:ni], refs[ni:]
        cins, refs = refs[:ci], refs[ci:]
        outs, refs = refs[:no], refs[no:]
        couts, refs = refs[:co], refs[co:]
        scratch, csems = refs[:ns], refs[ns:]
        step = pl.program_id(0)
        for ax in range(1, len(grid)):
            step = step * grid[ax] + pl.program_id(ax)
        steps = math.prod(grid)
        first = step == 0
        last = step == steps - 1

        @pl.when(first)
        def _():
            carry.start(cins, couts, csems)

        if carry.middle is not None:
            @pl.when(step == min(int(carry.middle_at * steps), steps - 1))
            def _():
                carry.middle(cins, couts, csems)

        body(*pre, *ins, *outs, *scratch)

        @pl.when(last)
        def _():
            carry.finish(cins, couts, csems)

    spec = pltpu.PrefetchScalarGridSpec(
        num_scalar_prefetch=np_, grid=grid, in_specs=list(in_specs) + [ANY] * ci,
        out_specs=list(out_specs) + [ANY] * co, scratch_shapes=list(scratch_shapes) + carry.sems)
    return pl.pallas_call(
        carrying, name=name, grid_spec=spec, out_shape=list(out_shape) + carry.out_shape,
        input_output_aliases={np_ + ni + i: no + j for i, j in carry.aliases.items()},
        compiler_params=_params(*(["arbitrary"] * len(grid))),
    )(*prefetch, *args, *carry.inputs)


def _run_carry(carry, name):
    ci, co = len(carry.inputs), len(carry.out_shape)

    def body(*refs):
        cins, couts, csems = refs[:ci], refs[ci:ci + co], refs[ci + co:]
        carry.start(cins, couts, csems)
        if carry.middle is not None:
            carry.middle(cins, couts, csems)
        carry.finish(cins, couts, csems)

    return pl.pallas_call(body, name=name, in_specs=[ANY] * ci, out_specs=[ANY] * co, out_shape=carry.out_shape,
                          input_output_aliases=carry.aliases, scratch_shapes=carry.sems)(*carry.inputs)


def _ffn_fwd(x, gain, wg, wu, wd, name, carry=None):
    T, D = x.shape
    FF = wg.shape[1]
    tm = _tile(T, 512, 16)
    tf = _tile(FF, 512, LANES)
    nf = FF // tf

    def body(x_ref, g_ref, wg_ref, wu_ref, wd_ref, xo_ref, n_ref, G_ref, U_ref, acc_ref):
        f = pl.program_id(1)

        @pl.when(f == 0)
        def _():
            xv = x_ref[...]
            n_ref[...] = (xv * _rstd(xv) * g_ref[...]).astype(BF16)
            acc_ref[...] = jnp.zeros_like(acc_ref)

        n = n_ref[...]
        G = jnp.dot(n, wg_ref[...], preferred_element_type=F32)
        U = jnp.dot(n, wu_ref[...], preferred_element_type=F32)
        G_ref[...] = G.astype(BF16)
        U_ref[...] = U.astype(BF16)
        H = (G * _sigmoid(G) * U).astype(BF16)
        acc_ref[...] += jnp.dot(H, wd_ref[...], preferred_element_type=F32)

        @pl.when(f == nf - 1)
        def _():
            xo_ref[...] = x_ref[...] + FFN_RESIDUAL_SCALE * acc_ref[...]

    return _call(
        body, name, (T // tm, nf),
        [pl.BlockSpec((tm, D), lambda i, f: (i, 0)),
         pl.BlockSpec((1, D), lambda i, f: (0, 0)),
         pl.BlockSpec((D, tf), lambda i, f: (0, f)),
         pl.BlockSpec((D, tf), lambda i, f: (0, f)),
         pl.BlockSpec((tf, D), lambda i, f: (f, 0))],
        [pl.BlockSpec((tm, D), lambda i, f: (i, 0)),
         pl.BlockSpec((tm, D), lambda i, f: (i, 0)),
         pl.BlockSpec((tm, tf), lambda i, f: (i, f)),
         pl.BlockSpec((tm, tf), lambda i, f: (i, f))],
        [jax.ShapeDtypeStruct((T, D), F32), jax.ShapeDtypeStruct((T, D), BF16),
         jax.ShapeDtypeStruct((T, FF), BF16), jax.ShapeDtypeStruct((T, FF), BF16)],
        [pltpu.VMEM((tm, D), F32)], ("parallel", "arbitrary"), (x, gain, wg, wu, wd), carry)


def _ffn_bwd_act(db, G, U, wg, wu, wd, name, carry=None):
    T, D = db.shape
    FF = wg.shape[1]
    tm = _tile(T, 512, 16)
    tf = _tile(FF, 512, LANES)

    def body(d_ref, G_ref, U_ref, wg_ref, wu_ref, wd_ref, dG_ref, dU_ref, H_ref, dn_ref):
        f = pl.program_id(1)
        dH = _dot_nt(d_ref[...], wd_ref[...])
        Gv = G_ref[...].astype(F32)
        Uv = U_ref[...].astype(F32)
        s = _sigmoid(Gv)
        sg = Gv * s
        H_ref[...] = (sg * Uv).astype(BF16)
        dU = (dH * sg).astype(BF16)
        dG = (dH * Uv * (s * (1.0 + Gv * (1.0 - s)))).astype(BF16)
        dG_ref[...] = dG
        dU_ref[...] = dU
        contrib = _dot_nt(dG, wg_ref[...]) + _dot_nt(dU, wu_ref[...])

        @pl.when(f == 0)
        def _():
            dn_ref[...] = contrib

        @pl.when(f > 0)
        def _():
            dn_ref[...] += contrib

    return _call(
        body, name, (T // tm, FF // tf),
        [pl.BlockSpec((tm, D), lambda i, f: (i, 0)),
         pl.BlockSpec((tm, tf), lambda i, f: (i, f)),
         pl.BlockSpec((tm, tf), lambda i, f: (i, f)),
         pl.BlockSpec((D, tf), lambda i, f: (0, f)),
         pl.BlockSpec((D, tf), lambda i, f: (0, f)),
         pl.BlockSpec((tf, D), lambda i, f: (f, 0))],
        [pl.BlockSpec((tm, tf), lambda i, f: (i, f)),
         pl.BlockSpec((tm, tf), lambda i, f: (i, f)),
         pl.BlockSpec((tm, tf), lambda i, f: (i, f)),
         pl.BlockSpec((tm, D), lambda i, f: (i, 0))],
        [jax.ShapeDtypeStruct((T, FF), BF16), jax.ShapeDtypeStruct((T, FF), BF16),
         jax.ShapeDtypeStruct((T, FF), BF16), jax.ShapeDtypeStruct((T, D), F32)],
        [], ("parallel", "arbitrary"), (db, G, U, wg, wu, wd), carry)


def _ffn_bwd_hidden(db, G, U, wd, name, carry=None):
    T, D = db.shape
    FF = wd.shape[0]
    tm = _tile(T, 1024, 16)
    tf = _tile(FF, 512, LANES)

    def body(d_ref, G_ref, U_ref, wd_ref, dG_ref, dU_ref, H_ref):
        dH = _dot_nt(d_ref[...], wd_ref[...])
        Gv = G_ref[...].astype(F32)
        Uv = U_ref[...].astype(F32)
        s = _sigmoid(Gv)
        sg = Gv * s
        H_ref[...] = (sg * Uv).astype(BF16)
        dU_ref[...] = (dH * sg).astype(BF16)
        dG_ref[...] = (dH * Uv * (s * (1.0 + Gv * (1.0 - s)))).astype(BF16)

    act = pl.BlockSpec((tm, tf), lambda i, f: (i, f))
    return _call(
        body, name, (T // tm, FF // tf),
        [pl.BlockSpec((tm, D), lambda i, f: (i, 0)), act, act, pl.BlockSpec((tf, D), lambda i, f: (f, 0))],
        [act, act, act], [jax.ShapeDtypeStruct((T, FF), BF16)] * 3,
        [], ("parallel", "arbitrary"), (db, G, U, wd), carry)


def _ffn_bwd_input(dG, dU, wg, wu, name, carry=None):
    T, FF = dG.shape
    D = wg.shape[0]
    tm = _tile(T, 512, 16)
    tn = _tile(D, 512, LANES)

    def body(dG_ref, dU_ref, wg_ref, wu_ref, dn_ref):
        dn_ref[...] = _dot_nt(dG_ref[...], wg_ref[...]) + _dot_nt(dU_ref[...], wu_ref[...])

    act = pl.BlockSpec((tm, FF), lambda i, j: (i, 0))
    wt = pl.BlockSpec((tn, FF), lambda i, j: (j, 0))
    return _call(body, name, (T // tm, D // tn), [act, act, wt, wt], [pl.BlockSpec((tm, tn), lambda i, j: (i, j))],
                 [jax.ShapeDtypeStruct((T, D), F32)], [], ("parallel", "arbitrary"), (dG, dU, wg, wu), carry)


TAIL_ROWS = 128


class _NormBwdTail:
    def __init__(self, T, D, tm, scale):
        self.T, self.D, self.tm, self.scale = T, D, tm, scale
        self.ni = T // tm
        self.scratch = [pltpu.VMEM((tm, D), F32), pltpu.VMEM((tm, D), F32), pltpu.VMEM((tm, D), F32),
                        pltpu.VMEM((tm, D), BF16), pltpu.SemaphoreType.DMA((4,))]
        self.out_shape = [jax.ShapeDtypeStruct((T, D), F32), jax.ShapeDtypeStruct((T, D), BF16)]

    def _rows(self, k):
        return pl.ds(pl.multiple_of(k * self.tm, self.tm), self.tm)

    def _loads(self, k, x_hbm, r_hbm, bufs):
        xbuf, rbuf, _, _, sems = bufs
        return [pltpu.make_async_copy(x_hbm.at[self._rows(k)], xbuf, sems.at[0]),
                pltpu.make_async_copy(r_hbm.at[self._rows(k)], rbuf, sems.at[1])]

    def _stores(self, k, dx_hbm, dxb_hbm, bufs):
        _, _, obuf, obbuf, sems = bufs
        return [pltpu.make_async_copy(obuf, dx_hbm.at[self._rows(k)], sems.at[2]),
                pltpu.make_async_copy(obbuf, dxb_hbm.at[self._rows(k)], sems.at[3])]

    def prefetch(self, i, x_hbm, r_hbm, bufs):
        for cp in self._loads(i, x_hbm, r_hbm, bufs):
            cp.start()

    def run(self, i, acc_ref, g_ref, x_hbm, r_hbm, dx_hbm, dxb_hbm, dg_ref, bufs):
        xbuf, rbuf, obuf, obbuf, _ = bufs
        for cp in self._loads(i, x_hbm, r_hbm, bufs):
            cp.wait()

        @pl.when(i > 0)
        def _():
            for cp in self._stores(i - 1, dx_hbm, dxb_hbm, bufs):
                cp.wait()

        dgain = None
        for r0 in range(0, self.tm, TAIL_ROWS):
            rs = slice(r0, min(r0 + TAIL_ROWS, self.tm))
            dv, dgr = _rms_bwd(acc_ref[rs, :], xbuf[rs, :], g_ref[...])
            dx = rbuf[rs, :] + dv
            obuf[rs, :] = dx
            obbuf[rs, :] = (self.scale * dx).astype(BF16)
            dgain = dgr if dgain is None else dgain + dgr
        for cp in self._stores(i, dx_hbm, dxb_hbm, bufs):
            cp.start()

        @pl.when(i == 0)
        def _():
            dg_ref[...] = dgain

        @pl.when(i > 0)
        def _():
            dg_ref[...] += dgain

        @pl.when(i == self.ni - 1)
        def _():
            for cp in self._stores(i, dx_hbm, dxb_hbm, bufs):
                cp.wait()


def _ffn_bwd_fused(db, G, U, wg, wu, wd, x_in, gain, dres, scale, name):
    T, D = db.shape
    FF = wg.shape[1]
    tm = _tile(T, 512, 16)
    tf = _tile(FF, 512, LANES)
    nf = FF // tf
    tail = _NormBwdTail(T, D, tm, scale)

    def body(d_ref, G_ref, U_ref, wg_ref, wu_ref, wd_ref, g_ref, x_hbm, r_hbm,
             dG_ref, dU_ref, H_ref, dg_ref, dx_hbm, dxb_hbm, acc_ref, *bufs):
        i, f = pl.program_id(0), pl.program_id(1)

        @pl.when(f == 0)
        def _():
            tail.prefetch(i, x_hbm, r_hbm, bufs)

        dH = _dot_nt(d_ref[...], wd_ref[...])
        Gv = G_ref[...].astype(F32)
        Uv = U_ref[...].astype(F32)
        s = _sigmoid(Gv)
        sg = Gv * s
        H_ref[...] = (sg * Uv).astype(BF16)
        dU = (dH * sg).astype(BF16)
        dG = (dH * Uv * (s * (1.0 + Gv * (1.0 - s)))).astype(BF16)
        dG_ref[...] = dG
        dU_ref[...] = dU
        contrib = _dot_nt(dG, wg_ref[...]) + _dot_nt(dU, wu_ref[...])

        @pl.when(f == 0)
        def _():
            acc_ref[...] = contrib

        @pl.when(f > 0)
        def _():
            acc_ref[...] += contrib

        @pl.when(f == nf - 1)
        def _():
            tail.run(i, acc_ref, g_ref, x_hbm, r_hbm, dx_hbm, dxb_hbm, dg_ref, bufs)

    act = pl.BlockSpec((tm, tf), lambda i, f: (i, f))
    return pl.pallas_call(
        body, name=name, grid=(T // tm, nf),
        in_specs=[pl.BlockSpec((tm, D), lambda i, f: (i, 0)), act, act,
                  pl.BlockSpec((D, tf), lambda i, f: (0, f)),
                  pl.BlockSpec((D, tf), lambda i, f: (0, f)),
                  pl.BlockSpec((tf, D), lambda i, f: (f, 0)),
                  pl.BlockSpec((1, D), lambda i, f: (0, 0)), ANY, ANY],
        out_specs=[act, act, act, pl.BlockSpec((1, D), lambda i, f: (0, 0)), ANY, ANY],
        out_shape=[jax.ShapeDtypeStruct((T, FF), BF16)] * 3 + [jax.ShapeDtypeStruct((1, D), F32)] + tail.out_shape,
        scratch_shapes=[pltpu.VMEM((tm, D), F32)] + tail.scratch,
        compiler_params=_params("arbitrary", "arbitrary"),
    )(db, G, U, wg, wu, wd, gain, x_in, dres)


def _mm_nt_norm_bwd(a, w, x_in, gain, dres, scale, name, carry=None):
    T, K = a.shape
    D = w.shape[0]
    tm = _tile(T, 512, 16)
    tk = _tile(K, 1280, LANES)
    nk = K // tk
    tail = _NormBwdTail(T, D, tm, scale)

    def body(a_ref, w_ref, g_ref, x_hbm, r_hbm, dg_ref, dx_hbm, dxb_hbm, acc_ref, *bufs):
        i, k = pl.program_id(0), pl.program_id(1)

        @pl.when(k == 0)
        def _():
            tail.prefetch(i, x_hbm, r_hbm, bufs)

        contrib = _dot_nt(a_ref[...], w_ref[...])

        @pl.when(k == 0)
        def _():
            acc_ref[...] = contrib

        @pl.when(k > 0)
        def _():
            acc_ref[...] += contrib

        @pl.when(k == nk - 1)
        def _():
            tail.run(i, acc_ref, g_ref, x_hbm, r_hbm, dx_hbm, dxb_hbm, dg_ref, bufs)

    return _call(
        body, name, (T // tm, nk),
        [pl.BlockSpec((tm, tk), lambda i, k: (i, k)), pl.BlockSpec((D, tk), lambda i, k: (0, k)),
         pl.BlockSpec((1, D), lambda i, k: (0, 0)), ANY, ANY],
        [pl.BlockSpec((1, D), lambda i, k: (0, 0)), ANY, ANY],
        [jax.ShapeDtypeStruct((1, D), F32)] + tail.out_shape,
        [pltpu.VMEM((tm, D), F32)] + tail.scratch, ("arbitrary", "arbitrary"), (a, w, gain, x_in, dres), carry)


def _rms_bwd_res(dn, x, gain, dres, scale, name, carry=None):
    T, D = x.shape
    tm = _tile(T, 256, 16)

    def body(dn_ref, x_ref, g_ref, dr_ref, dx_ref, dxb_ref, dg_ref):
        i = pl.program_id(0)
        dv, dgain = _rms_bwd(dn_ref[...], x_ref[...], g_ref[...])
        dx = dr_ref[...] + dv
        dx_ref[...] = dx
        dxb_ref[...] = (scale * dx).astype(BF16)

        @pl.when(i == 0)
        def _():
            dg_ref[...] = dgain

        @pl.when(i > 0)
        def _():
            dg_ref[...] += dgain

    row = pl.BlockSpec((tm, D), lambda i: (i, 0))
    vec = pl.BlockSpec((1, D), lambda i: (0, 0))
    return _call(
        body, name, (T // tm,), [row, row, vec, row], [row, row, vec],
        [jax.ShapeDtypeStruct((T, D), F32), jax.ShapeDtypeStruct((T, D), BF16), jax.ShapeDtypeStruct((1, D), F32)],
        [], ("arbitrary",), (dn, x, gain, dres), carry)


def _loss_head(x3, gain, target, name):
    T, D = x3.shape
    tm = _tile(T, 256, 16)

    def body(x_ref, g_ref, t_ref, dx_ref, dxb_ref, ls_ref, dg_ref):
        i = pl.program_id(0)
        xv = x_ref[...]
        err = xv * _rstd(xv) * g_ref[...] - t_ref[...]
        sq = jnp.sum(jnp.sum(err * err, axis=1, keepdims=True), axis=0, keepdims=True)
        dv, dgain = _rms_bwd(err * (1.0 / D), xv, g_ref[...])
        dx_ref[...] = dv
        dxb_ref[...] = (FFN_RESIDUAL_SCALE * dv).astype(BF16)
        sqb = jnp.broadcast_to(sq, (1, LANES))

        @pl.when(i == 0)
        def _():
            dg_ref[...] = dgain
            ls_ref[...] = sqb

        @pl.when(i > 0)
        def _():
            dg_ref[...] += dgain
            ls_ref[...] += sqb

    row = pl.BlockSpec((tm, D), lambda i: (i, 0))
    vec = pl.BlockSpec((1, D), lambda i: (0, 0))
    return pl.pallas_call(
        body, name=name, grid=(T // tm,),
        in_specs=[row, vec, row],
        out_specs=[row, row, pl.BlockSpec((1, LANES), lambda i: (0, 0)), vec],
        out_shape=[jax.ShapeDtypeStruct((T, D), F32), jax.ShapeDtypeStruct((T, D), BF16),
                   jax.ShapeDtypeStruct((1, LANES), F32), jax.ShapeDtypeStruct((1, D), F32)],
        compiler_params=_params("arbitrary"),
    )(x3, gain, target)


def _norm_mm(x, gain, w, name, carry=None):
    T, D = x.shape
    N = w.shape[1]
    tm = _tile(T, 512, 16)
    tn = _tile(N, 2560, LANES)

    def body(x_ref, g_ref, w_ref, n_ref, z_ref):
        @pl.when(pl.program_id(1) == 0)
        def _():
            xv = x_ref[...]
            n_ref[...] = (xv * _rstd(xv) * g_ref[...]).astype(BF16)

        z_ref[...] = jnp.dot(n_ref[...], w_ref[...], preferred_element_type=F32).astype(BF16)

    return _call(
        body, name, (T // tm, N // tn),
        [pl.BlockSpec((tm, D), lambda i, j: (i, 0)),
         pl.BlockSpec((1, D), lambda i, j: (0, 0)),
         pl.BlockSpec((D, tn), lambda i, j: (0, j))],
        [pl.BlockSpec((tm, D), lambda i, j: (i, 0)),
         pl.BlockSpec((tm, tn), lambda i, j: (i, j))],
        [jax.ShapeDtypeStruct((T, D), BF16), jax.ShapeDtypeStruct((T, N), BF16)],
        [], ("parallel", "arbitrary"), (x, gain, w), carry)


def _ffn_up(n, G, wu, name, carry=None):
    T, D = n.shape
    FF = wu.shape[1]
    tm = _tile(T, 512, 16)
    tf = _tile(FF, 1408, LANES)

    def body(n_ref, G_ref, wu_ref, U_ref, H_ref):
        U = jnp.dot(n_ref[...], wu_ref[...], preferred_element_type=F32)
        Gv = G_ref[...].astype(F32)
        U_ref[...] = U.astype(BF16)
        H_ref[...] = (Gv * _sigmoid(Gv) * U).astype(BF16)

    act = pl.BlockSpec((tm, tf), lambda i, f: (i, f))
    return _call(
        body, name, (T // tm, FF // tf),
        [pl.BlockSpec((tm, D), lambda i, f: (i, 0)), act, pl.BlockSpec((D, tf), lambda i, f: (0, f))],
        [act, act], [jax.ShapeDtypeStruct((T, FF), BF16)] * 2, [], ("parallel", "arbitrary"), (n, G, wu), carry)


def _mm_fullk(a, w, trans_w, residual, out_dtype, name, carry=None, scale=1.0):
    T, K = a.shape
    N = w.shape[0] if trans_w else w.shape[1]
    tm = _tile(T, 512, 16)
    tn = _tile(N, 2048 * 2560 // K, LANES)

    def body(*refs):
        if residual is None:
            a_ref, w_ref, o_ref = refs
        else:
            a_ref, w_ref, r_ref, o_ref = refs
        if trans_w:
            acc = _dot_nt(a_ref[...], w_ref[...])
        else:
            acc = jnp.dot(a_ref[...], w_ref[...], preferred_element_type=F32)
        if scale != 1.0:
            acc = scale * acc
        if residual is not None:
            acc = acc + r_ref[...]
        o_ref[...] = acc.astype(out_dtype)

    w_spec = pl.BlockSpec((tn, K), lambda i, j: (j, 0)) if trans_w else pl.BlockSpec((K, tn), lambda i, j: (0, j))
    in_specs = [pl.BlockSpec((tm, K), lambda i, j: (i, 0)), w_spec]
    args = [a, w]
    if residual is not None:
        in_specs.append(pl.BlockSpec((tm, tn), lambda i, j: (i, j)))
        args.append(residual)
    return _call(body, name, (T // tm, N // tn), in_specs, [pl.BlockSpec((tm, tn), lambda i, j: (i, j))],
                 [jax.ShapeDtypeStruct((T, N), out_dtype)], [], ("parallel", "arbitrary"), args, carry)


def _mm_tn(a, b, name, carry=None):
    T, M = a.shape
    N = b.shape[1]
    tmw = _tile(M, 2048, LANES)
    tnw = _tile(N, 2048 * 1408 // tmw, LANES)
    tk = _tile(T, 1024, 16)
    nk = T // tk

    def body(a_ref, b_ref, o_ref, acc_ref):
        k = pl.program_id(2)

        @pl.when(k == 0)
        def _():
            acc_ref[...] = jnp.zeros_like(acc_ref)

        acc_ref[...] += _dot_tn(a_ref[...], b_ref[...])

        @pl.when(k == nk - 1)
        def _():
            o_ref[...] = acc_ref[...].astype(BF16)

    return _call(
        body, name, (M // tmw, N // tnw, nk),
        [pl.BlockSpec((tk, tmw), lambda i, j, k: (k, i)),
         pl.BlockSpec((tk, tnw), lambda i, j, k: (k, j))],
        [pl.BlockSpec((tmw, tnw), lambda i, j, k: (i, j))],
        [jax.ShapeDtypeStruct((M, N), BF16)],
        [pltpu.VMEM((tmw, tnw), F32)], ("parallel", "parallel", "arbitrary"), (a, b), carry)


def _mm_tn_pair(a, b, c_arr, axis, name, carry=None):
    T, M = a.shape
    N = b.shape[1]
    tk = _tile(T, 2048, 16)
    nk = T // tk
    nq = 4
    if axis == 1:
        rows, cols = M // 2, N // nq
        a_spec = pl.BlockSpec((tk, rows), lambda p, j, k, cr: (k, jnp.where(p == 0, 1 - cr[0], cr[0])))
        b_spec = pl.BlockSpec((tk, cols), lambda p, j, k, cr: (k, j))
        s_shape, land_shape = (1, rows, N), (rows, N)
        s_spec = pl.BlockSpec((None, rows, cols), lambda p, j, k, cr: (0, 0, j * p))
    else:
        rows, cols = M // nq, N // 2
        a_spec = pl.BlockSpec((tk, rows), lambda p, j, k, cr: (k, j))
        b_spec = pl.BlockSpec((tk, cols), lambda p, j, k, cr: (k, jnp.where(p == 0, 1 - cr[0], cr[0])))
        s_shape, land_shape = (nq, rows, cols), (nq, rows, cols)
        s_spec = pl.BlockSpec((None, rows, cols), lambda p, j, k, cr: (j * p, 0, 0))

    def body(c_ref, a_ref, b_ref, s_ref, land, acc_ref, stage, got, send_sems, recv_sems, loc_sem):
        p, j, k = pl.program_id(0), pl.program_id(1), pl.program_id(2)
        x, y, c = _me()

        def tile(jj):
            return land.at[:, pl.ds(jj * cols, cols)] if axis == 1 else land.at[jj]

        def send(jj):
            return _remote(stage, tile(jj), send_sems, recv_sems, jj, (x, y, 1 - c))

        @pl.when(k == 0)
        def _():
            acc_ref[...] = jnp.zeros_like(acc_ref)

        acc_ref[...] += _dot_tn(a_ref[...], b_ref[...])

        def fetch(jj):
            return pltpu.make_async_copy(tile(jj), got, loc_sem.at[0])

        for jj in range(nq):
            @pl.when(jnp.logical_and(k == nk - 1, jnp.logical_and(p == 0, j == jj)))
            def _():
                if jj > 0:
                    send(jj - 1).wait_send()
                stage[...] = acc_ref[...].astype(BF16)
                send(jj).start()

            @pl.when(jnp.logical_and(k == max(nk - 2, 0), jnp.logical_and(p == 1, j == jj)))
            def _():
                if jj == 0:
                    send(nq - 1).wait_send()
                send(jj).wait_recv()
                fetch(jj).start()

            @pl.when(jnp.logical_and(k == nk - 1, jnp.logical_and(p == 1, j == jj)))
            def _():
                fetch(jj).wait()
                s_ref[...] = (acc_ref[...] + got[...].astype(F32)).astype(BF16)

    return _call(
        body, name, (2, nq, nk), [a_spec, b_spec], [s_spec, ANY],
        [jax.ShapeDtypeStruct(s_shape, BF16), jax.ShapeDtypeStruct(land_shape, BF16)],
        [pltpu.VMEM((rows, cols), F32), pltpu.VMEM((rows, cols), BF16), pltpu.VMEM((rows, cols), BF16),
         pltpu.SemaphoreType.DMA((nq,)), pltpu.SemaphoreType.DMA((nq,)), pltpu.SemaphoreType.DMA((1,))],
        ("arbitrary", "arbitrary", "arbitrary"), (a, b), carry, (c_arr,))


GELU_K = math.sqrt(2.0 / math.pi)
GELU_C = 0.044715


def _gelu_and_grad(v):
    u = GELU_K * (v + GELU_C * v * v * v)
    th = jnp.tanh(u)
    g = 0.5 * v * (1.0 + th)
    dg = 0.5 * (1.0 + th) + 0.5 * v * (1.0 - th * th) * GELU_K * (1.0 + 3.0 * GELU_C * v * v)
    return g, dg


def _neg_expm1(v):
    poly = v * (1.0 + v * (0.5 + v * (1.0 / 6 + v * (1.0 / 24 + v * (1.0 / 120 + v * (1.0 / 720))))))
    return jnp.where(v > -0.25, -poly, 1.0 - jnp.exp(v))


def _softplus_neg(lam):
    e = jnp.exp(-jnp.abs(lam))
    log1pe = jnp.where(e < 1e-4, e * (1.0 - 0.5 * e), jnp.log(1.0 + e))
    sp = jnp.maximum(-lam, 0.0) + log1pe
    dsp = -1.0 / (1.0 + jnp.exp(lam))
    return sp, dsp


def _earlier(ext, j):
    return pltpu.roll(ext, j, 0)[SUBLANES:, :]


def _later(ext, j):
    n = ext.shape[0]
    return pltpu.roll(ext, n - j, 0)[:n - SUBLANES, :]


def _taps(v, halo, K):
    ext = jnp.concatenate([halo, v], axis=0)
    return [v] + [_earlier(ext, j) for j in range(1, K)]


def _block_diag(vb, w_ref, nh, hd):
    return jnp.concatenate(
        [jnp.dot(vb[:, h * hd:(h + 1) * hd], w_ref[h], preferred_element_type=F32) for h in range(nh)], axis=1)


def _lru_gates(xc, wa_ref, ba_ref, wi_ref, bi_ref, sp, nh, hd):
    xcb = xc.astype(BF16)
    r = _sigmoid(_block_diag(xcb, wa_ref, nh, hd) + ba_ref[...])
    ig = _sigmoid(_block_diag(xcb, wi_ref, nh, hd) + bi_ref[...])
    log_a = -LRU_C * r * sp
    a = jnp.exp(log_a)
    mult = jnp.sqrt(_neg_expm1(2.0 * log_a))
    return xcb, r, ig, a, mult


def _mix_fwd(z, cw, cb, wa, ba, wi, bi, lam, sw, glo, gso, name, carry=None):
    T = z.shape[0]
    DL = cb.shape[1]
    DS = gso.shape[1]
    NH, HD = wa.shape[0], wa.shape[1]
    KL, KS = cw.shape[0], sw.shape[0]
    tt = _tile(T, 128, 16)
    o_g, o_b, o_c, o_x = DL, 2 * DL, 2 * DL + DS, 2 * DL + 2 * DS

    def body(z_ref, cw_ref, cb_ref, wa_ref, ba_ref, wi_ref, bi_ref, lam_ref, sw_ref, glo_ref, gso_ref,
             h_ref, y_ref, cx_ref, cp_ref, ch_ref):
        @pl.when(pl.program_id(0) == 0)
        def _():
            cx_ref[...] = jnp.zeros_like(cx_ref)
            cp_ref[...] = jnp.zeros_like(cp_ref)
            ch_ref[...] = jnp.zeros_like(ch_ref)

        def zcol(o, n):
            return z_ref[:, o:o + n].astype(F32)

        lx = zcol(0, DL)
        xs = _taps(lx, cx_ref[...], KL)
        cx_ref[...] = lx[tt - SUBLANES:, :]
        xc = cb_ref[...] + xs[0] * cw_ref[KL - 1:KL, :]
        for j in range(1, KL):
            xc = xc + xs[j] * cw_ref[KL - 1 - j:KL - j, :]
        sp, _ = _softplus_neg(lam_ref[...])
        _, _, ig, a, mult = _lru_gates(xc, wa_ref, ba_ref, wi_ref, bi_ref, sp, NH, HD)
        b = mult * (ig * xc)
        rows = lax.broadcasted_iota(jnp.int32, (tt, DL), 0)
        s = 1
        while s < tt:
            keep = rows >= s
            b = jnp.where(keep, a * pltpu.roll(b, s, 0) + b, b)
            a = jnp.where(keep, a * pltpu.roll(a, s, 0), a)
            s *= 2
        h = a * ch_ref[SUBLANES - 1:SUBLANES, :] + b
        ch_ref[...] = h[tt - SUBLANES:, :]
        h_ref[...] = h
        ge, _ = _gelu_and_grad(zcol(o_g, DL))
        ylru = h * ge
        y_ref[:, 0:DL] = (ylru * _rstd(ylru) * glo_ref[...]).astype(BF16)

        p = zcol(o_c, DS) * zcol(o_x, DS)
        ps = _taps(p, cp_ref[...], KS)
        cp_ref[...] = p[tt - SUBLANES:, :]
        cv = ps[0] * sw_ref[KS - 1:KS, :]
        for j in range(1, KS):
            cv = cv + ps[j] * sw_ref[KS - 1 - j:KS - j, :]
        ysc = zcol(o_b, DS) * cv
        y_ref[:, DL:DL + DS] = (ysc * _rstd(ysc) * gso_ref[...]).astype(BF16)

    def full(shape):
        return pl.BlockSpec(shape, lambda t: (0,) * len(shape))

    return _call(
        body, name, (T // tt,),
        [pl.BlockSpec((tt, z.shape[1]), lambda t: (t, 0)),
         full(cw.shape), full(cb.shape), full(wa.shape), full(ba.shape), full(wi.shape), full(bi.shape),
         full(lam.shape), full(sw.shape), full(glo.shape), full(gso.shape)],
        [pl.BlockSpec((tt, DL), lambda t: (t, 0)), pl.BlockSpec((tt, DL + DS), lambda t: (t, 0))],
        [jax.ShapeDtypeStruct((T, DL), F32), jax.ShapeDtypeStruct((T, DL + DS), BF16)],
        [pltpu.VMEM((SUBLANES, DL), F32), pltpu.VMEM((SUBLANES, DS), F32), pltpu.VMEM((SUBLANES, DL), F32)],
        ("arbitrary",), (z, cw, cb, wa, ba, wi, bi, lam, sw, glo, gso), carry)


V_BA, V_BI, V_LAM, V_CB, V_CW, V_SW, V_GLO, V_GSO, V_ROWS = 0, 1, 2, 3, 4, 8, 11, 12, 16


def _mix_bwd(z, h, dy, cw, cb, wa, ba, wi, bi, lam, sw, glo, gso, name):
    T = z.shape[0]
    DL = cb.shape[1]
    DS = gso.shape[1]
    NH, HD = wa.shape[0], wa.shape[1]
    KL, KS = cw.shape[0], sw.shape[0]
    tt = _tile(T, 64, 16)
    nt = T // tt
    ZH = 2 * SUBLANES
    o_g, o_b, o_c, o_x = DL, 2 * DL, 2 * DL + DS, 2 * DL + 2 * DS

    def body(z_ref, zh_ref, h_ref, hh_ref, dy_ref, cw_ref, cb_ref, wa_ref, ba_ref, wi_ref, bi_ref, lam_ref,
             sw_ref, glo_ref, gso_ref, dz_ref, dwa_ref, dwi_ref, vec_ref, cdx_ref, cdc_ref, cdh_ref):
        i = pl.program_id(0)
        tr = nt - 1 - i

        @pl.when(i == 0)
        def _():
            dwa_ref[...] = jnp.zeros_like(dwa_ref)
            dwi_ref[...] = jnp.zeros_like(dwi_ref)
            vec_ref[...] = jnp.zeros_like(vec_ref)
            cdx_ref[...] = jnp.zeros_like(cdx_ref)
            cdc_ref[...] = jnp.zeros_like(cdc_ref)
            cdh_ref[...] = jnp.zeros_like(cdh_ref)

        def acc_row(r, v):
            vec_ref[pl.ds(r, 1), :] += jnp.sum(v, axis=0, keepdims=True)

        has_prev = tr > 0
        rows = lax.broadcasted_iota(jnp.int32, (tt, DL), 0)

        def zcol(o, n):
            return z_ref[:, o:o + n].astype(F32)

        def zhalo(o, n):
            return jnp.where(has_prev, zh_ref[:, o:o + n].astype(F32)[SUBLANES:, :], 0.0)

        lx = zcol(0, DL)
        xs = _taps(lx, zhalo(0, DL), KL)
        xc = cb_ref[...] + xs[0] * cw_ref[KL - 1:KL, :]
        for j in range(1, KL):
            xc = xc + xs[j] * cw_ref[KL - 1 - j:KL - j, :]
        sp, dsp = _softplus_neg(lam_ref[...])
        xcb, r, ig, a, mult = _lru_gates(xc, wa_ref, ba_ref, wi_ref, bi_ref, sp, NH, HD)
        hv = h_ref[...]
        hprev = _earlier(jnp.concatenate([jnp.where(has_prev, hh_ref[...], 0.0), hv], axis=0), 1)
        gate = zcol(o_g, DL)
        ge, dge = _gelu_and_grad(gate)
        ylru = hv * ge

        d_ylru, dglo = _rms_bwd(dy_ref[:, 0:DL].astype(F32), ylru, glo_ref[...])
        vec_ref[pl.ds(V_GLO, 1), :] += dglo
        dz_ref[:, o_g:o_g + DL] = (d_ylru * hv * dge).astype(BF16)
        bq = d_ylru * ge
        aq = jnp.where(rows == tt - 1, 1.0, pltpu.roll(a, tt - 1, 0))
        s = 1
        while s < tt:
            keep = rows < tt - s
            bq = jnp.where(keep, aq * pltpu.roll(bq, tt - s, 0) + bq, bq)
            aq = jnp.where(keep, aq * pltpu.roll(aq, tt - s, 0), aq)
            s *= 2
        dhh = bq + aq * cdh_ref[0:1, :]
        cdh_ref[0:1, :] = a[0:1, :] * dhh[0:1, :]

        da = dhh * hprev
        dmult = dhh * (ig * xc)
        d_i = dhh * mult * xc
        dxc = dhh * mult * ig
        dlog = da * a - dmult * (a * a) / mult
        acc_row(V_LAM, dlog * (-LRU_C * r) * dsp)
        dpa = dlog * (-LRU_C * sp) * r * (1.0 - r)
        dpi = d_i * ig * (1.0 - ig)
        acc_row(V_BA, dpa)
        acc_row(V_BI, dpi)
        dpab = dpa.astype(BF16)
        dpib = dpi.astype(BF16)
        back = []
        for hh in range(NH):
            sl = slice(hh * HD, (hh + 1) * HD)
            dwa_ref[hh] += _dot_tn(xcb[:, sl], dpab[:, sl])
            dwi_ref[hh] += _dot_tn(xcb[:, sl], dpib[:, sl])
            back.append(_dot_nt(dpab[:, sl], wa_ref[hh]) + _dot_nt(dpib[:, sl], wi_ref[hh]))
        dxc = dxc + jnp.concatenate(back, axis=1)

        acc_row(V_CB, dxc)
        extd = jnp.concatenate([dxc, cdx_ref[...]], axis=0)
        cdx_ref[...] = dxc[0:SUBLANES, :]
        dlx = dxc * cw_ref[KL - 1:KL, :]
        acc_row(V_CW + KL - 1, dxc * xs[0])
        for j in range(1, KL):
            dlx = dlx + _later(extd, j) * cw_ref[KL - 1 - j:KL - j, :]
            acc_row(V_CW + KL - 1 - j, dxc * xs[j])
        dz_ref[:, 0:DL] = dlx.astype(BF16)

        sb = zcol(o_b, DS)
        sc = zcol(o_c, DS)
        sx = zcol(o_x, DS)
        p = sc * sx
        ps = _taps(p, zhalo(o_c, DS) * zhalo(o_x, DS), KS)
        cv = ps[0] * sw_ref[KS - 1:KS, :]
        for j in range(1, KS):
            cv = cv + ps[j] * sw_ref[KS - 1 - j:KS - j, :]
        d_ysc, dgso = _rms_bwd(dy_ref[:, DL:DL + DS].astype(F32), sb * cv, gso_ref[...])
        vec_ref[pl.ds(V_GSO, 1), :] += dgso
        dz_ref[:, o_b:o_b + DS] = (d_ysc * cv).astype(BF16)
        dcv = d_ysc * sb
        extc = jnp.concatenate([dcv, cdc_ref[...]], axis=0)
        cdc_ref[...] = dcv[0:SUBLANES, :]
        dp = dcv * sw_ref[KS - 1:KS, :]
        acc_row(V_SW + KS - 1, dcv * ps[0])
        for j in range(1, KS):
            dp = dp + _later(extc, j) * sw_ref[KS - 1 - j:KS - j, :]
            acc_row(V_SW + KS - 1 - j, dcv * ps[j])
        dz_ref[:, o_c:o_c + DS] = (dp * sx).astype(BF16)
        dz_ref[:, o_x:o_x + DS] = (dp * sc).astype(BF16)

    def full(shape):
        return pl.BlockSpec(shape, lambda t: (0,) * len(shape))

    def rev(t):
        return nt - 1 - t

    def halo(t, rows):
        return jnp.maximum(rev(t) * (tt // rows) - 1, 0)

    return pl.pallas_call(
        body, name=name, grid=(nt,),
        in_specs=[pl.BlockSpec((tt, z.shape[1]), lambda t: (rev(t), 0)),
                  pl.BlockSpec((ZH, z.shape[1]), lambda t: (halo(t, ZH), 0)),
                  pl.BlockSpec((tt, DL), lambda t: (rev(t), 0)),
                  pl.BlockSpec((SUBLANES, DL), lambda t: (halo(t, SUBLANES), 0)),
                  pl.BlockSpec((tt, DL + DS), lambda t: (rev(t), 0)),
                  full(cw.shape), full(cb.shape), full(wa.shape), full(ba.shape), full(wi.shape), full(bi.shape),
                  full(lam.shape), full(sw.shape), full(glo.shape), full(gso.shape)],
        out_specs=[pl.BlockSpec((tt, z.shape[1]), lambda t: (rev(t), 0)),
                   full(wa.shape), full(wi.shape), full((V_ROWS, DL))],
        out_shape=[jax.ShapeDtypeStruct(z.shape, BF16), jax.ShapeDtypeStruct(wa.shape, F32),
                   jax.ShapeDtypeStruct(wi.shape, F32), jax.ShapeDtypeStruct((V_ROWS, DL), F32)],
        scratch_shapes=[pltpu.VMEM((SUBLANES, DL), F32), pltpu.VMEM((SUBLANES, DS), F32),
                        pltpu.VMEM((SUBLANES, DL), F32)],
        compiler_params=_params("arbitrary"),
    )(z, z, h, h, dy, cw, cb, wa, ba, wi, bi, lam, sw, glo, gso)


def _pair_add(p, r1, c, name):
    G, R, C = r1.shape
    tr = _tile(R, 256, 16)
    tc = _tile(C, 1408, LANES)

    def body(c_ref, p_ref, r_ref, o_ref):
        o_ref[...] = (p_ref[...].astype(F32) + r_ref[...].astype(F32)).astype(BF16)

    blk = (None, tr, tc)
    return pl.pallas_call(
        body, name=name,
        grid_spec=pltpu.PrefetchScalarGridSpec(
            num_scalar_prefetch=1, grid=(G, R // tr, C // tc),
            in_specs=[pl.BlockSpec(blk, lambda g, i, j, cr: (2 * g + cr[0], i, j)),
                      pl.BlockSpec(blk, lambda g, i, j, cr: (g, i, j))],
            out_specs=pl.BlockSpec(blk, lambda g, i, j, cr: (g, i, j))),
        out_shape=jax.ShapeDtypeStruct((G, R, C), BF16),
        compiler_params=_params("parallel", "parallel", "parallel"),
    )(c, p, r1)


def _quad_add(s, r2, qc, axis, name):
    _, R, W = r2.shape
    tr = _tile(R, 256, 16)

    def body(qc_ref, s_ref, r0_ref, r1_ref, r2_ref, o_ref):
        o_ref[...] = ((s_ref[...].astype(F32) + r0_ref[...].astype(F32)) + r1_ref[...].astype(F32)) + r2_ref[...].astype(F32)

    blk = (None, tr, W)
    if axis == 1:
        own = pl.BlockSpec(blk, lambda i, qr: (0, i, qr[0]))
    else:
        own = pl.BlockSpec(blk, lambda i, qr: (qr[0], i, 0))
    return pl.pallas_call(
        body, name=name,
        grid_spec=pltpu.PrefetchScalarGridSpec(
            num_scalar_prefetch=1, grid=(R // tr,),
            in_specs=[own] + [pl.BlockSpec(blk, lambda i, qr, j=j: (j, i, 0)) for j in range(3)],
            out_specs=pl.BlockSpec(blk, lambda i, qr: (qr[1], i, 0))),
        out_shape=jax.ShapeDtypeStruct((2, R, W), F32),
        compiler_params=_params("parallel"),
    )(qc, s, r2, r2, r2)


CAST_BLOCKS = 4


def _cast_into_full(shards, qc, axes, name, carry=None):
    M = len(shards)
    nb = CAST_BLOCKS

    def body(qc_ref, *refs):
        for s_ref, o_ref in zip(refs[:M], refs[M:]):
            o_ref[...] = s_ref[...].astype(BF16)

    in_specs, out_specs, out_shape = [], [], []
    for s, ax in zip(shards, axes):
        R, W = s.shape
        Rh = R // 2
        tr = Rh // nb
        in_specs.append(pl.BlockSpec((tr, W), lambda hf, i, qr: (hf * nb + i, 0)))
        if ax == 1:
            out_shape.append(jax.ShapeDtypeStruct((2, Rh, 4 * W), BF16))
            out_specs.append(pl.BlockSpec((None, tr, W), lambda hf, i, qr: (hf, i, qr[0])))
        else:
            out_shape.append(jax.ShapeDtypeStruct((8, Rh, W), BF16))
            out_specs.append(pl.BlockSpec((None, tr, W), lambda hf, i, qr: (2 * qr[0] + hf, i, 0)))
    return _call(body, name, (2, nb), in_specs, out_specs, out_shape, [], ("parallel", "parallel"), shards, carry, (qc,))


def _adamw(w, g, m, v, name):
    R, C = w.shape
    tr = _tile(R, 256, SUBLANES)
    tc = C // 2 if g.ndim == 3 else _tile(C, 2048, LANES)
    c1 = 1.0 - ADAM_B1 ** ADAM_STEP
    c2 = 1.0 - ADAM_B2 ** ADAM_STEP

    def body(w_ref, g_ref, m_ref, v_ref, d_ref, mo_ref, vo_ref, go_ref):
        gv = g_ref[...]
        go_ref[...] = gv
        mn = ADAM_B1 * m_ref[...] + (1.0 - ADAM_B1) * gv
        vn = ADAM_B2 * v_ref[...] + (1.0 - ADAM_B2) * (gv * gv)
        mo_ref[...] = mn
        vo_ref[...] = vn
        d_ref[...] = -ADAM_LR * ((mn / c1) / (jnp.sqrt(vn / c2) + ADAM_EPS) + ADAM_WD * w_ref[...])

    blk = pl.BlockSpec((tr, tc), lambda i, j: (i, j))
    g_blk = pl.BlockSpec((None, tr, tc), lambda i, j: (j, i, 0)) if g.ndim == 3 else blk
    sh = jax.ShapeDtypeStruct((R, C), F32)
    return pl.pallas_call(
        body, name=name, grid=(R // tr, C // tc),
        in_specs=[blk, g_blk, blk, blk], out_specs=[blk] * 4, out_shape=[sh] * 4,
        compiler_params=_params("parallel", "parallel"),
    )(w, g, m, v)


def _other_chips(x, y):
    return [(1 - x, y), (x, 1 - y), (1 - x, 1 - y)]


def _remote(src, dst, send_sems, recv_sems, idx, dev):
    return pltpu.make_async_remote_copy(src_ref=src, dst_ref=dst, send_sem=send_sems.at[idx], recv_sem=recv_sems.at[idx],
                                        device_id=dev, device_id_type=MESH)


def _gather_carry(fulls, axes, pieces=None):
    M = len(fulls)

    def win(outs, m, qq, cc):
        Rh = fulls[m].shape[1]
        k, n = pieces[m] if pieces is not None and pieces[m] is not None else (0, 1)
        rows = pl.ds(k * (Rh // n), Rh // n)
        if axes[m] == 1:
            W = fulls[m].shape[2] // 4
            return outs[m].at[cc, rows, pl.ds(pl.multiple_of(qq * W, LANES), W)]
        return outs[m].at[2 * qq + cc, rows, :]

    def ici(outs, sems, m, j, src_q):
        x, y, c = _me()
        cx, cy = _other_chips(x, y)[j]
        blk = win(outs, m, src_q, c)
        return _remote(blk, blk, sems[0], sems[1], 6 * m + j, (cx, cy, c))

    def d2d(outs, sems, m, j, half):
        x, y, c = _me()
        cx, cy = _other_chips(x, y)[j]
        blk = win(outs, m, 2 * cx + cy, half)
        return _remote(blk, blk, sems[0], sems[1], 6 * m + 3 + j, (x, y, 1 - c))

    def start(ins, outs, sems):
        x, y, c = _me()
        for m in range(M):
            for j in range(3):
                ici(outs, sems, m, j, 2 * x + y).start()

    def middle(ins, outs, sems):
        x, y, c = _me()
        for m in range(M):
            for j, (cx, cy) in enumerate(_other_chips(x, y)):
                ici(outs, sems, m, j, 2 * cx + cy).wait_recv()
                d2d(outs, sems, m, j, c).start()

    def finish(ins, outs, sems):
        x, y, c = _me()
        for m in range(M):
            for j in range(3):
                d2d(outs, sems, m, j, 1 - c).wait_recv()
        for m in range(M):
            for j in range(3):
                ici(outs, sems, m, j, 2 * x + y).wait_send()
                d2d(outs, sems, m, j, c).wait_send()

    return _Carry(fulls, [jax.ShapeDtypeStruct(f.shape, f.dtype) for f in fulls], {m: m for m in range(M)},
                  [pltpu.SemaphoreType.DMA((6 * M,)), pltpu.SemaphoreType.DMA((6 * M,))], start, finish, middle)


def _gather_two_way_carry(fulls, axes):
    M = len(fulls)

    def part(outs, m, qq, cc, p):
        Rp = fulls[m].shape[1] // 2
        rows = pl.ds(p * Rp, Rp)
        if axes[m] == 1:
            W = fulls[m].shape[2] // 4
            return outs[m].at[cc, rows, pl.ds(pl.multiple_of(qq * W, LANES), W)]
        return outs[m].at[2 * qq + cc, rows, :]

    def copies(outs, sems):
        x, y, c = _me()
        q, qx, qy, qd = 2 * x + y, 2 * (1 - x) + y, 2 * x + (1 - y), 2 * (1 - x) + (1 - y)
        xn, yn, sib = (1 - x, y, c), (x, 1 - y, c), (x, y, 1 - c)
        table = {}
        for m in range(M):
            def cp(blk, k, dev):
                return _remote(blk, blk, sems[0], sems[1], 12 * m + k, dev)
            own0, own1 = part(outs, m, q, c, 0), part(outs, m, q, c, 1)
            table[m] = dict(
                to=[cp(own0, 0, xn), cp(own1, 1, yn), cp(own1, 2, xn), cp(own0, 3, yn)],
                landed=[cp(part(outs, m, qx, c, 0), 0, xn), cp(part(outs, m, qy, c, 1), 1, yn),
                        cp(part(outs, m, qx, c, 1), 2, xn), cp(part(outs, m, qy, c, 0), 3, yn),
                        cp(part(outs, m, qd, c, 0), 4, yn), cp(part(outs, m, qd, c, 1), 5, xn)],
                passed=[cp(part(outs, m, qx, c, 0), 4, yn), cp(part(outs, m, qy, c, 1), 5, xn)],
                handed=[cp(part(outs, m, qq, c, p), 6 + k, sib)
                        for k, (qq, p) in enumerate([(qx, 0), (qy, 1), (qx, 1), (qy, 0), (qd, 0), (qd, 1)])],
                taken=[cp(part(outs, m, qq, 1 - c, p), 6 + k, sib)
                       for k, (qq, p) in enumerate([(qx, 0), (qy, 1), (qx, 1), (qy, 0), (qd, 0), (qd, 1)])])
        return table

    def start(ins, outs, sems):
        t = copies(outs, sems)
        for m in range(M):
            for cp in t[m]['to']:
                cp.start()

    def finish(ins, outs, sems):
        t = copies(outs, sems)
        for m in range(M):
            for k in range(4):
                t[m]['landed'][k].wait_recv()
                if k < 2:
                    t[m]['passed'][k].start()
                t[m]['handed'][k].start()
        for m in range(M):
            for k in (4, 5):
                t[m]['landed'][k].wait_recv()
                t[m]['handed'][k].start()
        for m in range(M):
            for cp in t[m]['taken']:
                cp.wait_recv()
            for cp in t[m]['to'] + t[m]['passed'] + t[m]['handed']:
                cp.wait_send()

    return _Carry(fulls, [jax.ShapeDtypeStruct(f.shape, f.dtype) for f in fulls], {m: m for m in range(M)},
                  [pltpu.SemaphoreType.DMA((12 * M,)), pltpu.SemaphoreType.DMA((12 * M,))], start, finish)


def _pair_exchange_carry(parts):
    M = len(parts)
    groups = [p.shape[0] // 2 for p in parts]
    base = [sum(groups[:m]) for m in range(M)]
    out_shape = [jax.ShapeDtypeStruct((g,) + p.shape[1:], p.dtype) for g, p in zip(groups, parts)]

    def copies(ins, outs, sems):
        x, y, c = _me()
        return [_remote(ins[m].at[2 * g + 1 - c], outs[m].at[g], sems[0], sems[1], base[m] + g, (x, y, 1 - c))
                for m in range(M) for g in range(groups[m])]

    def start(ins, outs, sems):
        for cp in copies(ins, outs, sems):
            cp.start()

    def finish(ins, outs, sems):
        for cp in copies(ins, outs, sems):
            cp.wait()

    n = sum(groups)
    return _Carry(parts, out_shape, {}, [pltpu.SemaphoreType.DMA((n,)), pltpu.SemaphoreType.DMA((n,))], start, finish)


def _chip_exchange_carry(sums, axes):
    M = len(sums)
    out_shape = []
    for s, ax in zip(sums, axes):
        _, Rh, C = s.shape
        out_shape.append(jax.ShapeDtypeStruct((3, Rh, C // 4 if ax == 1 else C), s.dtype))

    def copies(ins, outs, sems):
        x, y, c = _me()
        cps = []
        for m in range(M):
            for j, (cx, cy) in enumerate(_other_chips(x, y)):
                qj = 2 * cx + cy
                if axes[m] == 1:
                    W = sums[m].shape[2] // 4
                    src = ins[m].at[0, :, pl.ds(pl.multiple_of(qj * W, LANES), W)]
                else:
                    src = ins[m].at[qj]
                cps.append(_remote(src, outs[m].at[j], sems[0], sems[1], 3 * m + j, (cx, cy, c)))
        return cps

    def start(ins, outs, sems):
        for cp in copies(ins, outs, sems):
            cp.start()

    def finish(ins, outs, sems):
        for cp in copies(ins, outs, sems):
            cp.wait()

    return _Carry(sums, out_shape, {}, [pltpu.SemaphoreType.DMA((3 * M,)), pltpu.SemaphoreType.DMA((3 * M,))],
                  start, finish)


def _pair_share_carry(bufs):
    M = len(bufs)

    def start(ins, outs, sems):
        x, y, c = _me()
        for m in range(M):
            _remote(outs[m].at[c], outs[m].at[c], sems[0], sems[1], m, (x, y, 1 - c)).start()

    def finish(ins, outs, sems):
        x, y, c = _me()
        for m in range(M):
            _remote(outs[m].at[c], outs[m].at[c], sems[0], sems[1], m, (x, y, 1 - c)).wait_send()
            _remote(outs[m].at[1 - c], outs[m].at[1 - c], sems[0], sems[1], m, (x, y, 1 - c)).wait_recv()

    return _Carry(bufs, [jax.ShapeDtypeStruct(b.shape, b.dtype) for b in bufs], {m: m for m in range(M)},
                  [pltpu.SemaphoreType.DMA((M,)), pltpu.SemaphoreType.DMA((M,))], start, finish)


def _allreduce_small(v, name):
    R, W = v.shape
    Rh = R // 2

    def body(v_ref, o_ref, sib, quad, send_sems, recv_sems):
        x, y, c = _me()
        q = 2 * x + y
        sibling = (x, y, 1 - c)
        pair = _remote(v_ref, sib, send_sems, recv_sems, 0, sibling)
        pair.start()
        pair.wait()
        mine = pl.ds(pl.multiple_of(c * Rh, SUBLANES), Rh)
        quad[0] = v_ref[mine, :] + sib[mine, :]
        cps = []
        for k in (1, 2, 3):
            peer = (1 - x if k & 2 else x, 1 - y if k & 1 else y, c)
            cps.append(_remote(quad.at[0], quad.at[k], send_sems, recv_sems, k, peer))
            cps[-1].start()
        for cp in cps:
            cp.wait()
        acc = quad[q]
        for p in (1, 2, 3):
            acc = acc + quad[jnp.bitwise_xor(q, p)]
        o_ref[mine, :] = acc
        theirs = pl.ds(pl.multiple_of((1 - c) * Rh, SUBLANES), Rh)
        done = _remote(o_ref.at[mine, :], o_ref.at[mine, :], send_sems, recv_sems, 4, sibling)
        done.start()
        done.wait_send()
        _remote(o_ref.at[theirs, :], o_ref.at[theirs, :], send_sems, recv_sems, 4, sibling).wait_recv()

    vm = pl.BlockSpec(memory_space=pltpu.VMEM)
    return pl.pallas_call(
        body, name=name, in_specs=[vm], out_specs=vm, out_shape=jax.ShapeDtypeStruct((R, W), F32),
        scratch_shapes=[pltpu.VMEM((R, W), F32), pltpu.VMEM((4, Rh, W), F32),
                        pltpu.SemaphoreType.DMA((5,)), pltpu.SemaphoreType.DMA((5,))],
        compiler_params=pltpu.CompilerParams(vmem_limit_bytes=VMEM_LIMIT),
    )(v)


def _pack(pieces):
    flat = []
    for p in pieces:
        p = p.reshape(-1).astype(F32)
        pad = (-p.shape[0]) % PACK_ALIGN
        flat.append(jnp.pad(p, (0, pad)).reshape(-1, PACK_W))
    if sum(f.shape[0] for f in flat) % (2 * SUBLANES):
        flat.append(jnp.zeros((SUBLANES, PACK_W), F32))
    return jnp.concatenate(flat, axis=0)


def _unpack(packed, shapes):
    out, row = [], 0
    for shp in shapes:
        n = math.prod(shp)
        rows = -(-n // PACK_ALIGN) * SUBLANES
        out.append(packed[row:row + rows].reshape(-1)[:n].reshape(shp))
        row += rows
    return out


def kernel(x, ffn1_norm, ffn1_w_gate, ffn1_w_up, ffn1_w_down, mix_norm, w_in, lru_conv_w, lru_conv_b, lru_w_a, lru_b_a, lru_w_i, lru_b_i, lru_lambda, sc_conv_w, lru_out_norm, sc_out_norm, w_out, ffn2_norm, ffn2_w_gate, ffn2_w_up, ffn2_w_down, final_norm, loss_target, m_ffn1_norm, m_ffn1_w_gate, m_ffn1_w_up, m_ffn1_w_down, m_mix_norm, m_w_in, m_lru_conv_w, m_lru_conv_b, m_lru_w_a, m_lru_b_a, m_lru_w_i, m_lru_b_i, m_lru_lambda, m_sc_conv_w, m_lru_out_norm, m_sc_out_norm, m_w_out, m_ffn2_norm, m_ffn2_w_gate, m_ffn2_w_up, m_ffn2_w_down, m_final_norm, v_ffn1_norm, v_ffn1_w_gate, v_ffn1_w_up, v_ffn1_w_down, v_mix_norm, v_w_in, v_lru_conv_w, v_lru_conv_b, v_lru_w_a, v_lru_b_a, v_lru_w_i, v_lru_b_i, v_lru_lambda, v_sc_conv_w, v_lru_out_norm, v_sc_out_norm, v_w_out, v_ffn2_norm, v_ffn2_w_gate, v_ffn2_w_up, v_ffn2_w_down, v_final_norm):
    vals = locals()
    w = {n: vals[n] for n in WEIGHTS}
    mom = {n: vals["m_" + n] for n in WEIGHTS}
    var = {n: vals["v_" + n] for n in WEIGHTS}

    xi, yi, ci = _me()
    qi = 2 * xi + yi
    c_arr = jnp.reshape(ci, (1,)).astype(jnp.int32)
    qc_arr = jnp.stack([qi, ci]).astype(jnp.int32)

    T, D = x.shape[1], x.shape[2]
    xt = x.reshape(T, D)
    target = loss_target.reshape(T, D)
    DL = lru_conv_b.shape[-1]
    NH, HD = lru_w_a.shape[1], lru_w_a.shape[2]
    KL, KS = lru_conv_w.shape[1], sc_conv_w.shape[1]
    DLq = lru_conv_w.shape[2]

    axis_of = dict(zip(BIG, BIG_AXIS))
    placed, full = {}, {}

    def gather(names, pieces={}):
        return _gather_carry([placed[n] for n in names], [axis_of[n] for n in names], [pieces.get(n) for n in names])

    def gathered(names, views, unfinished=()):
        for n, g in zip(names, views):
            if n in unfinished:
                placed[n] = g
            else:
                full[n] = (g.reshape(2 * g.shape[1], g.shape[2]) if axis_of[n] == 1
                           else g.reshape(8 * g.shape[1], g.shape[2]))

    first, rest = 'ffn1_w_gate', [n for n in BIG if n != 'ffn1_w_gate']
    placed[first] = _cast_into_full([w[first][0]], qc_arr, [axis_of[first]], "cast_first_weight")[0]
    res = _cast_into_full([w[n][0] for n in rest], qc_arr, [axis_of[n] for n in rest], "cast_other_weights",
                          _gather_two_way_carry([placed[first]], [axis_of[first]]))
    placed.update(zip(rest, res[:len(rest)]))
    gathered([first], res[len(rest):])

    taps = jnp.zeros((2 * SUBLANES, DL), F32)
    taps = lax.dynamic_update_slice(taps, lru_conv_w[0], (0, qi * DLq))
    taps = lax.dynamic_update_slice(taps, sc_conv_w[0], (KL, qi * DLq))
    taps = _allreduce_small(jnp.where(ci == 0, taps, 0.0), "gather_conv_taps")
    cw, sw = taps[0:KL], taps[KL:KL + KS]

    cb = lru_conv_b
    wa, wi = lru_w_a[0].astype(BF16), lru_w_i[0].astype(BF16)
    ba, bi = lru_b_a.reshape(1, DL), lru_b_i.reshape(1, DL)
    mix_args = (cw, cb, wa, ba, wi, bi, lru_lambda, sw, lru_out_norm, sc_out_norm)
    gf = final_norm.reshape(1, D)

    names = ['ffn1_w_up', 'w_out']
    res = _norm_mm(xt, ffn1_norm, full['ffn1_w_gate'], "ffn1_gate", gather(names))
    n1, G1 = res[:2]
    gathered(names, res[2:])
    names = ['ffn1_w_down', 'ffn2_w_down']
    res = _ffn_up(n1, G1, full['ffn1_w_up'], "ffn1_up", gather(names, {'ffn2_w_down': (0, 2)}))
    U1, H1f = res[:2]
    gathered(names, res[2:], unfinished=['ffn2_w_down'])
    names = ['w_in', 'ffn2_w_down']
    res = _mm_fullk(H1f, full['ffn1_w_down'], False, xt, F32, "ffn1_down", gather(names, {'ffn2_w_down': (1, 2)}),
                    FFN_RESIDUAL_SCALE)
    x1 = res[0]
    gathered(names, res[1:])
    names = ['ffn2_w_gate']
    res = _norm_mm(x1, mix_norm, full['w_in'], "mix_in_proj", gather(names))
    n2, z = res[:2]
    gathered(names, res[2:])
    names = ['ffn2_w_up']
    res = _mix_fwd(z, *mix_args, "mix_fwd", gather(names))
    h, ymix = res[:2]
    gathered(names, res[2:])
    x2 = _mm_fullk(ymix, full['w_out'], False, x1, F32, "mix_out_proj")[0]
    x3, n3, G2, U2 = _ffn_fwd(x2, ffn2_norm, full['ffn2_w_gate'], full['ffn2_w_up'], full['ffn2_w_down'], "ffn2_fwd")
    dx3, d3b, sqerr, dgf = _loss_head(x3, gf, target, "loss_head")

    sums, halves, shared = {}, {}, {}

    def dw(n, a, b, name, carry=None):
        res = _mm_tn_pair(a, b, c_arr, axis_of[n], name, carry)
        sums[n] = res[0]
        return res[2:]

    def chip_carry(names):
        return _chip_exchange_carry([sums[n] for n in names], [axis_of[n] for n in names])

    def chip_add(names, recv):
        for n, r in zip(names, recv):
            halves[n] = _quad_add(sums[n], r, qc_arr, axis_of[n], "grad_chip_add_" + n)

    dG2, dU2, H2 = _ffn_bwd_hidden(d3b, G2, U2, full['ffn2_w_down'], "ffn2_bwd_hidden")
    dn3 = _ffn_bwd_input(dG2, dU2, full['ffn2_w_gate'], full['ffn2_w_up'], "ffn2_bwd_input")[0]
    dx2, dx2b, dg_ffn2 = _rms_bwd_res(dn3, x2, ffn2_norm, dx3, 1.0, "ffn2_norm_bwd")
    dw('ffn2_w_gate', n3, dG2, "ffn2_dwg")
    dw('ffn2_w_up', n3, dU2, "ffn2_dwu")
    dw('ffn2_w_down', H2, d3b, "ffn2_dwd")
    dy = _mm_fullk(dx2b, full['w_out'], True, None, BF16, "mix_out_bwd")[0]
    dz, dwa, dwi, vec = _mix_bwd(z, h, dy, *mix_args, "mix_bwd")
    dg_mix, dx1, d1b = _mm_nt_norm_bwd(dz, full['w_in'], x1, mix_norm, dx2, FFN_RESIDUAL_SCALE, "mix_in_bwd")

    def share_carry(names):
        return _pair_share_carry([halves[n] for n in names])

    res = _ffn_bwd_hidden(d1b, G1, U1, full['ffn1_w_down'], "ffn1_bwd_hidden", chip_carry(['ffn2_w_gate']))
    dG1, dU1, H1 = res[:3]
    chip_add(['ffn2_w_gate'], res[3:])
    res = _ffn_bwd_input(dG1, dU1, full['ffn1_w_gate'], full['ffn1_w_up'], "ffn1_bwd_input",
                         chip_carry(['ffn2_w_up', 'ffn2_w_down']))
    dn1 = res[0]
    chip_add(['ffn2_w_up', 'ffn2_w_down'], res[1:])
    dx0, _, dg_ffn1 = _rms_bwd_res(dn1, xt, ffn1_norm, dx1, 1.0, "ffn1_norm_bwd")
    names = ['ffn2_w_gate', 'ffn2_w_up', 'ffn2_w_down']
    shared.update(zip(names, dw('ffn1_w_gate', n1, dG1, "ffn1_dwg", share_carry(names))))
    chip_add(['ffn1_w_gate'], dw('ffn1_w_up', n1, dU1, "ffn1_dwu", chip_carry(['ffn1_w_gate'])))
    chip_add(['ffn1_w_up'], dw('ffn1_w_down', H1, d1b, "ffn1_dwd", chip_carry(['ffn1_w_up'])))
    names = ['ffn1_w_gate', 'ffn1_w_up']
    res = dw('w_in', n2, dz, "mix_dwin", _merge_carries(chip_carry(['ffn1_w_down']), share_carry(names)))
    chip_add(['ffn1_w_down'], res[:1])
    shared.update(zip(names, res[1:]))
    res = dw('w_out', ymix, dx2b, "mix_dwout", _merge_carries(chip_carry(['w_in']), share_carry(['ffn1_w_down'])))
    chip_add(['w_in'], res[:1])
    shared['ffn1_w_down'] = res[1]
    chip_add(['w_out'], _run_carry(chip_carry(['w_out']), "grad_chip_exchange_w_out"))
    names = ['w_in', 'w_out']
    shared.update(zip(names, _run_carry(share_carry(names), "grad_pair_share")))
    out_g, out_d, out_m, out_v = {}, {}, {}, {}
    for n in BIG:
        shp = w[n].shape
        g = shared[n].reshape(shp[1], shp[2]) if axis_of[n] == 1 else shared[n]
        outs = _adamw(w[n][0], g, mom[n][0], var[n][0], "adamw_" + n)
        out_d[n], out_m[n], out_v[n], out_g[n] = (a.reshape(shp) for a in outs)

    small = [n for n in WEIGHTS if n not in BIG]
    local_small = {
        'ffn1_norm': dg_ffn1, 'mix_norm': dg_mix, 'lru_conv_w': vec[V_CW:V_CW + KL], 'lru_conv_b': vec[V_CB],
        'lru_w_a': dwa, 'lru_b_a': vec[V_BA], 'lru_w_i': dwi, 'lru_b_i': vec[V_BI], 'lru_lambda': vec[V_LAM],
        'sc_conv_w': vec[V_SW:V_SW + KS], 'lru_out_norm': vec[V_GLO], 'sc_out_norm': vec[V_GSO],
        'ffn2_norm': dg_ffn2, 'final_norm': dgf,
    }
    full_shapes = [local_small[n].shape for n in small] + [(1,)]
    reduced = _allreduce_small(_pack([local_small[n] for n in small] + [sqerr[0, 0:1]]), "allreduce_small")
    reduced = _unpack(reduced, full_shapes)
    loss = (0.5 / D) * reduced[-1][0]
    gsm = {}
    for n, g in zip(small, reduced[:-1]):
        if n in SMALL_SHARDED:
            g = lax.dynamic_slice(g, (0, qi * DLq), (g.shape[0], DLq))
        gsm[n] = g.reshape(w[n].shape)
    small_shapes = [w[n].shape for n in small]
    d_s, m_s, v_s, _ = _adamw(_pack([w[n] for n in small]), _pack([gsm[n] for n in small]),
                              _pack([mom[n] for n in small]), _pack([var[n] for n in small]), "adamw_small")
    for n, d, mn, vn in zip(small, _unpack(d_s, small_shapes), _unpack(m_s, small_shapes), _unpack(v_s, small_shapes)):
        out_g[n], out_d[n], out_m[n], out_v[n] = gsm[n], d, mn, vn

    return (loss, dx0.reshape(x.shape), *[out_g[n] for n in WEIGHTS], *[out_d[n] for n in WEIGHTS],
            *[out_m[n] for n in WEIGHTS], *[out_v[n] for n in WEIGHTS])
```

```python
import math

import jax
import jax.numpy as jnp
from jax import lax
from jax.experimental import pallas as pl
from jax.experimental.pallas import tpu as pltpu

F32 = jnp.float32
BF16 = jnp.bfloat16
MESH = pl.DeviceIdType.MESH
ANY = pl.BlockSpec(memory_space=pl.ANY)

NORM_EPS = 1e-6
LRU_C = 8.0
FFN_RESIDUAL_SCALE = 0.5
ADAM_LR = 0.001
ADAM_B1 = 0.9
ADAM_B2 = 0.999
ADAM_EPS = 1e-08
ADAM_WD = 0.01
ADAM_STEP = 10

V7X_VMEM_BYTES = 64 * 2**20
VMEM_LIMIT = V7X_VMEM_BYTES - 8 * 2**20
LANES = 128
SUBLANES = 8
PACK_W = LANES
PACK_ALIGN = SUBLANES * PACK_W

WEIGHTS = ['ffn1_norm', 'ffn1_w_gate', 'ffn1_w_up', 'ffn1_w_down', 'mix_norm', 'w_in', 'lru_conv_w', 'lru_conv_b',
           'lru_w_a', 'lru_b_a', 'lru_w_i', 'lru_b_i', 'lru_lambda', 'sc_conv_w', 'lru_out_norm', 'sc_out_norm',
           'w_out', 'ffn2_norm', 'ffn2_w_gate', 'ffn2_w_up', 'ffn2_w_down', 'final_norm']
BIG = ['ffn1_w_gate', 'ffn1_w_up', 'ffn1_w_down', 'w_in', 'w_out', 'ffn2_w_gate', 'ffn2_w_up', 'ffn2_w_down']
BIG_AXIS = [1, 1, 0, 1, 0, 1, 1, 0]
SMALL_SHARDED = ['lru_conv_w', 'sc_conv_w']


def _tile(n, pref, mult):
    if n <= pref:
        return n
    t = (pref // mult) * mult
    while t >= mult:
        if n % t == 0:
            return t
        t -= mult
    return n


def _params(*sem):
    return pltpu.CompilerParams(dimension_semantics=sem, vmem_limit_bytes=VMEM_LIMIT)


def _me():
    return lax.axis_index("x"), lax.axis_index("y"), lax.axis_index("c")


def _sigmoid(v):
    return 1.0 / (1.0 + jnp.exp(-v))


def _rstd(v):
    return lax.rsqrt(jnp.mean(v * v, axis=-1, keepdims=True) + NORM_EPS)


def _rms_bwd(dy, v, gain):
    r = _rstd(v)
    w = gain * dy
    dv = r * w - v * (r * r * r) * jnp.mean(v * w, axis=-1, keepdims=True)
    dgain = jnp.sum(dy * v * r, axis=0, keepdims=True)
    return dv, dgain


def _dot_nt(a, b):
    return lax.dot_general(a, b, (((1,), (1,)), ((), ())), preferred_element_type=F32)


def _dot_tn(a, b):
    return lax.dot_general(a, b, (((0,), (0,)), ((), ())), preferred_element_type=F32)


class _Carry:
    def __init__(self, inputs, out_shape, aliases, sems, start, finish, middle=None, middle_at=0.85):
        self.inputs, self.out_shape, self.aliases, self.sems = list(inputs), list(out_shape), dict(aliases), list(sems)
        self.start, self.finish = start, finish
        self.middle, self.middle_at = middle, middle_at


def _merge_carries(a, b):
    ia, oa, sa = len(a.inputs), len(a.out_shape), len(a.sems)
    aliases = dict(a.aliases)
    aliases.update({ia + i: oa + j for i, j in b.aliases.items()})

    def both(which):
        def run(ins, outs, sems):
            getattr(a, which)(ins[:ia], outs[:oa], sems[:sa])
            getattr(b, which)(ins[ia:], outs[oa:], sems[sa:])
        return run

    return _Carry(a.inputs + b.inputs, a.out_shape + b.out_shape, aliases, a.sems + b.sems, both("start"), both("finish"))


def _call(body, name, grid, in_specs, out_specs, out_shape, scratch_shapes, semantics, args, carry=None, prefetch=()):
    np_ = len(prefetch)
    if carry is None:
        spec = pltpu.PrefetchScalarGridSpec(num_scalar_prefetch=np_, grid=grid, in_specs=in_specs, out_specs=out_specs,
                                            scratch_shapes=scratch_shapes)
        return pl.pallas_call(body, name=name, grid_spec=spec, out_shape=out_shape,
                              compiler_params=_params(*semantics))(*prefetch, *args)
    ni, no, ns = len(in_specs), len(out_specs), len(scratch_shapes)
    ci, co = len(carry.inputs), len(carry.out_shape)

    def carrying(*refs):
        pre, refs = refs[:np_], refs[np_:]
        ins, refs = refs[:ni], refs[ni:]
        cins, refs = refs[:ci], refs[ci:]
        outs, refs = refs[:no], refs[no:]
        couts, refs = refs[:co], refs[co:]
        scratch, csems = refs[:ns], refs[ns:]
        step = pl.program_id(0)
        for ax in range(1, len(grid)):
            step = step * grid[ax] + pl.program_id(ax)
        steps = math.prod(grid)
        first = step == 0
        last = step == steps - 1

        @pl.when(first)
        def _():
            carry.start(cins, couts, csems)

        if carry.middle is not None:
            @pl.when(step == min(int(carry.middle_at * steps), steps - 1))
            def _():
                carry.middle(cins, couts, csems)

        body(*pre, *ins, *outs, *scratch)

        @pl.when(last)
        def _():
            carry.finish(cins, couts, csems)

    spec = pltpu.PrefetchScalarGridSpec(
        num_scalar_prefetch=np_, grid=grid, in_specs=list(in_specs) + [ANY] * ci,
        out_specs=list(out_specs) + [ANY] * co, scratch_shapes=list(scratch_shapes) + carry.sems)
    return pl.pallas_call(
        carrying, name=name, grid_spec=spec, out_shape=list(out_shape) + carry.out_shape,
        input_output_aliases={np_ + ni + i: no + j for i, j in carry.aliases.items()},
        compiler_params=_params(*(["arbitrary"] * len(grid))),
    )(*prefetch, *args, *carry.inputs)


def _run_carry(carry, name):
    ci, co = len(carry.inputs), len(carry.out_shape)

    def body(*refs):
        cins, couts, csems = refs[:ci], refs[ci:ci + co], refs[ci + co:]
        carry.start(cins, couts, csems)
        if carry.middle is not None:
            carry.middle(cins, couts, csems)
        carry.finish(cins, couts, csems)

    return pl.pallas_call(body, name=name, in_specs=[ANY] * ci, out_specs=[ANY] * co, out_shape=carry.out_shape,
                          input_output_aliases=carry.aliases, scratch_shapes=carry.sems)(*carry.inputs)


def _ffn_fwd(x, gain, wg, wu, wd, name, carry=None):
    T, D = x.shape
    FF = wg.shape[1]
    tm = _tile(T, 512, 16)
    tf = _tile(FF, 512, LANES)
    nf = FF // tf

    def body(x_ref, g_ref, wg_ref, wu_ref, wd_ref, xo_ref, n_ref, G_ref, U_ref, acc_ref):
        f = pl.program_id(1)

        @pl.when(f == 0)
        def _():
            xv = x_ref[...]
            n_ref[...] = (xv * _rstd(xv) * g_ref[...]).astype(BF16)
            acc_ref[...] = jnp.zeros_like(acc_ref)

        n = n_ref[...]
        G = jnp.dot(n, wg_ref[...], preferred_element_type=F32)
        U = jnp.dot(n, wu_ref[...], preferred_element_type=F32)
        G_ref[...] = G.astype(BF16)
        U_ref[...] = U.astype(BF16)
        H = (G * _sigmoid(G) * U).astype(BF16)
        acc_ref[...] += jnp.dot(H, wd_ref[...], preferred_element_type=F32)

        @pl.when(f == nf - 1)
        def _():
            xo_ref[...] = x_ref[...] + FFN_RESIDUAL_SCALE * acc_ref[...]

    return _call(
        body, name, (T // tm, nf),
        [pl.BlockSpec((tm, D), lambda i, f: (i, 0)),
         pl.BlockSpec((1, D), lambda i, f: (0, 0)),
         pl.BlockSpec((D, tf), lambda i, f: (0, f)),
         pl.BlockSpec((D, tf), lambda i, f: (0, f)),
         pl.BlockSpec((tf, D), lambda i, f: (f, 0))],
        [pl.BlockSpec((tm, D), lambda i, f: (i, 0)),
         pl.BlockSpec((tm, D), lambda i, f: (i, 0)),
         pl.BlockSpec((tm, tf), lambda i, f: (i, f)),
         pl.BlockSpec((tm, tf), lambda i, f: (i, f))],
        [jax.ShapeDtypeStruct((T, D), F32), jax.ShapeDtypeStruct((T, D), BF16),
         jax.ShapeDtypeStruct((T, FF), BF16), jax.ShapeDtypeStruct((T, FF), BF16)],
        [pltpu.VMEM((tm, D), F32)], ("parallel", "arbitrary"), (x, gain, wg, wu, wd), carry)


def _ffn_bwd_hidden(db, G, U, wd, name, carry=None):
    T, D = db.shape
    FF = wd.shape[0]
    tm = _tile(T, 1024, 16)
    tf = _tile(FF, 512, LANES)

    def body(d_ref, G_ref, U_ref, wd_ref, dG_ref, dU_ref, H_ref):
        dH = _dot_nt(d_ref[...], wd_ref[...])
        Gv = G_ref[...].astype(F32)
        Uv = U_ref[...].astype(F32)
        s = _sigmoid(Gv)
        sg = Gv * s
        H_ref[...] = (sg * Uv).astype(BF16)
        dU_ref[...] = (dH * sg).astype(BF16)
        dG_ref[...] = (dH * Uv * (s * (1.0 + Gv * (1.0 - s)))).astype(BF16)

    act = pl.BlockSpec((tm, tf), lambda i, f: (i, f))
    return _call(
        body, name, (T // tm, FF // tf),
        [pl.BlockSpec((tm, D), lambda i, f: (i, 0)), act, act, pl.BlockSpec((tf, D), lambda i, f: (f, 0))],
        [act, act, act], [jax.ShapeDtypeStruct((T, FF), BF16)] * 3,
        [], ("parallel", "arbitrary"), (db, G, U, wd), carry)


def _ffn_bwd_input(dG, dU, wg, wu, name, carry=None):
    T, FF = dG.shape
    D = wg.shape[0]
    tm = _tile(T, 512, 16)
    tn = _tile(D, 512, LANES)

    def body(dG_ref, dU_ref, wg_ref, wu_ref, dn_ref):
        dn_ref[...] = _dot_nt(dG_ref[...], wg_ref[...]) + _dot_nt(dU_ref[...], wu_ref[...])

    act = pl.BlockSpec((tm, FF), lambda i, j: (i, 0))
    wt = pl.BlockSpec((tn, FF), lambda i, j: (j, 0))
    return _call(body, name, (T // tm, D // tn), [act, act, wt, wt], [pl.BlockSpec((tm, tn), lambda i, j: (i, j))],
                 [jax.ShapeDtypeStruct((T, D), F32)], [], ("parallel", "arbitrary"), (dG, dU, wg, wu), carry)


TAIL_ROWS = 128


class _NormBwdTail:
    def __init__(self, T, D, tm, scale):
        self.T, self.D, self.tm, self.scale = T, D, tm, scale
        self.ni = T // tm
        self.scratch = [pltpu.VMEM((tm, D), F32), pltpu.VMEM((tm, D), F32), pltpu.VMEM((tm, D), F32),
                        pltpu.VMEM((tm, D), BF16), pltpu.SemaphoreType.DMA((4,))]
        self.out_shape = [jax.ShapeDtypeStruct((T, D), F32), jax.ShapeDtypeStruct((T, D), BF16)]

    def _rows(self, k):
        return pl.ds(pl.multiple_of(k * self.tm, self.tm), self.tm)

    def _loads(self, k, x_hbm, r_hbm, bufs):
        xbuf, rbuf, _, _, sems = bufs
        return [pltpu.make_async_copy(x_hbm.at[self._rows(k)], xbuf, sems.at[0]),
                pltpu.make_async_copy(r_hbm.at[self._rows(k)], rbuf, sems.at[1])]

    def _stores(self, k, dx_hbm, dxb_hbm, bufs):
        _, _, obuf, obbuf, sems = bufs
        return [pltpu.make_async_copy(obuf, dx_hbm.at[self._rows(k)], sems.at[2]),
                pltpu.make_async_copy(obbuf, dxb_hbm.at[self._rows(k)], sems.at[3])]

    def prefetch(self, i, x_hbm, r_hbm, bufs):
        for cp in self._loads(i, x_hbm, r_hbm, bufs):
            cp.start()

    def run(self, i, acc_ref, g_ref, x_hbm, r_hbm, dx_hbm, dxb_hbm, dg_ref, bufs):
        xbuf, rbuf, obuf, obbuf, _ = bufs
        for cp in self._loads(i, x_hbm, r_hbm, bufs):
            cp.wait()

        @pl.when(i > 0)
        def _():
            for cp in self._stores(i - 1, dx_hbm, dxb_hbm, bufs):
                cp.wait()

        dgain = None
        for r0 in range(0, self.tm, TAIL_ROWS):
            rs = slice(r0, min(r0 + TAIL_ROWS, self.tm))
            dv, dgr = _rms_bwd(acc_ref[rs, :], xbuf[rs, :], g_ref[...])
            dx = rbuf[rs, :] + dv
            obuf[rs, :] = dx
            obbuf[rs, :] = (self.scale * dx).astype(BF16)
            dgain = dgr if dgain is None else dgain + dgr
        for cp in self._stores(i, dx_hbm, dxb_hbm, bufs):
            cp.start()

        @pl.when(i == 0)
        def _():
            dg_ref[...] = dgain

        @pl.when(i > 0)
        def _():
            dg_ref[...] += dgain

        @pl.when(i == self.ni - 1)
        def _():
            for cp in self._stores(i, dx_hbm, dxb_hbm, bufs):
                cp.wait()


def _mm_nt_norm_bwd(a, w, x_in, gain, dres, scale, name, carry=None):
    T, K = a.shape
    D = w.shape[0]
    whole = D * K * 2 <= 24 * 2**20
    tm = _tile(T, 256 if whole else 512, 16)
    tk = K if whole else _tile(K, 1280, LANES)
    nk = K // tk
    tail = _NormBwdTail(T, D, tm, scale)
    w_spec = (pl.BlockSpec((D, tk), lambda i, k: (0, k), pipeline_mode=pl.Buffered(1)) if whole
              else pl.BlockSpec((D, tk), lambda i, k: (0, k)))

    def body(a_ref, w_ref, g_ref, x_hbm, r_hbm, dg_ref, dx_hbm, dxb_hbm, acc_ref, *bufs):
        i, k = pl.program_id(0), pl.program_id(1)

        @pl.when(k == 0)
        def _():
            tail.prefetch(i, x_hbm, r_hbm, bufs)

        contrib = _dot_nt(a_ref[...], w_ref[...])

        @pl.when(k == 0)
        def _():
            acc_ref[...] = contrib

        @pl.when(k > 0)
        def _():
            acc_ref[...] += contrib

        @pl.when(k == nk - 1)
        def _():
            tail.run(i, acc_ref, g_ref, x_hbm, r_hbm, dx_hbm, dxb_hbm, dg_ref, bufs)

    return _call(
        body, name, (T // tm, nk),
        [pl.BlockSpec((tm, tk), lambda i, k: (i, k)), w_spec, pl.BlockSpec((1, D), lambda i, k: (0, 0)), ANY, ANY],
        [pl.BlockSpec((1, D), lambda i, k: (0, 0)), ANY, ANY],
        [jax.ShapeDtypeStruct((1, D), F32)] + tail.out_shape,
        [pltpu.VMEM((tm, D), F32)] + tail.scratch, ("arbitrary", "arbitrary"), (a, w, gain, x_in, dres), carry)


def _rms_bwd_res(dn, x, gain, dres, scale, name, carry=None):
    T, D = x.shape
    tm = _tile(T, 256, 16)

    def body(dn_ref, x_ref, g_ref, dr_ref, dx_ref, dxb_ref, dg_ref):
        i = pl.program_id(0)
        dv, dgain = _rms_bwd(dn_ref[...], x_ref[...], g_ref[...])
        dx = dr_ref[...] + dv
        dx_ref[...] = dx
        dxb_ref[...] = (scale * dx).astype(BF16)

        @pl.when(i == 0)
        def _():
            dg_ref[...] = dgain

        @pl.when(i > 0)
        def _():
            dg_ref[...] += dgain

    row = pl.BlockSpec((tm, D), lambda i: (i, 0))
    vec = pl.BlockSpec((1, D), lambda i: (0, 0))
    return _call(
        body, name, (T // tm,), [row, row, vec, row], [row, row, vec],
        [jax.ShapeDtypeStruct((T, D), F32), jax.ShapeDtypeStruct((T, D), BF16), jax.ShapeDtypeStruct((1, D), F32)],
        [], ("arbitrary",), (dn, x, gain, dres), carry)


def _loss_head(x3, gain, target, name):
    T, D = x3.shape
    tm = _tile(T, 256, 16)

    def body(x_ref, g_ref, t_ref, dx_ref, dxb_ref, ls_ref, dg_ref):
        i = pl.program_id(0)
        xv = x_ref[...]
        err = xv * _rstd(xv) * g_ref[...] - t_ref[...]
        sq = jnp.sum(jnp.sum(err * err, axis=1, keepdims=True), axis=0, keepdims=True)
        dv, dgain = _rms_bwd(err * (1.0 / D), xv, g_ref[...])
        dx_ref[...] = dv
        dxb_ref[...] = (FFN_RESIDUAL_SCALE * dv).astype(BF16)
        sqb = jnp.broadcast_to(sq, (1, LANES))

        @pl.when(i == 0)
        def _():
            dg_ref[...] = dgain
            ls_ref[...] = sqb

        @pl.when(i > 0)
        def _():
            dg_ref[...] += dgain
            ls_ref[...] += sqb

    row = pl.BlockSpec((tm, D), lambda i: (i, 0))
    vec = pl.BlockSpec((1, D), lambda i: (0, 0))
    return pl.pallas_call(
        body, name=name, grid=(T // tm,),
        in_specs=[row, vec, row],
        out_specs=[row, row, pl.BlockSpec((1, LANES), lambda i: (0, 0)), vec],
        out_shape=[jax.ShapeDtypeStruct((T, D), F32), jax.ShapeDtypeStruct((T, D), BF16),
                   jax.ShapeDtypeStruct((1, LANES), F32), jax.ShapeDtypeStruct((1, D), F32)],
        compiler_params=_params("arbitrary"),
    )(x3, gain, target)


def _norm_mm(x, gain, w, name, carry=None):
    T, D = x.shape
    N = w.shape[1]
    tm = _tile(T, 512, 16)
    tn = _tile(N, 2560, LANES)

    def body(x_ref, g_ref, w_ref, n_ref, z_ref):
        @pl.when(pl.program_id(1) == 0)
        def _():
            xv = x_ref[...]
            n_ref[...] = (xv * _rstd(xv) * g_ref[...]).astype(BF16)

        z_ref[...] = jnp.dot(n_ref[...], w_ref[...], preferred_element_type=F32).astype(BF16)

    return _call(
        body, name, (T // tm, N // tn),
        [pl.BlockSpec((tm, D), lambda i, j: (i, 0)),
         pl.BlockSpec((1, D), lambda i, j: (0, 0)),
         pl.BlockSpec((D, tn), lambda i, j: (0, j))],
        [pl.BlockSpec((tm, D), lambda i, j: (i, 0)),
         pl.BlockSpec((tm, tn), lambda i, j: (i, j))],
        [jax.ShapeDtypeStruct((T, D), BF16), jax.ShapeDtypeStruct((T, N), BF16)],
        [], ("parallel", "arbitrary"), (x, gain, w), carry)


def _ffn_up(n, G, wu, name, carry=None):
    T, D = n.shape
    FF = wu.shape[1]
    tm = _tile(T, 512, 16)
    tf = _tile(FF, 1408, LANES)

    def body(n_ref, G_ref, wu_ref, U_ref, H_ref):
        U = jnp.dot(n_ref[...], wu_ref[...], preferred_element_type=F32)
        Gv = G_ref[...].astype(F32)
        U_ref[...] = U.astype(BF16)
        H_ref[...] = (Gv * _sigmoid(Gv) * U).astype(BF16)

    act = pl.BlockSpec((tm, tf), lambda i, f: (i, f))
    return _call(
        body, name, (T // tm, FF // tf),
        [pl.BlockSpec((tm, D), lambda i, f: (i, 0)), act, pl.BlockSpec((D, tf), lambda i, f: (0, f))],
        [act, act], [jax.ShapeDtypeStruct((T, FF), BF16)] * 2, [], ("parallel", "arbitrary"), (n, G, wu), carry)


def _mm_fullk(a, w, trans_w, residual, out_dtype, name, carry=None, scale=1.0):
    T, K = a.shape
    N = w.shape[0] if trans_w else w.shape[1]
    tm = _tile(T, 512, 16)
    tn = _tile(N, 2048 * 2560 // K, LANES)

    def body(*refs):
        if residual is None:
            a_ref, w_ref, o_ref = refs
        else:
            a_ref, w_ref, r_ref, o_ref = refs
        if trans_w:
            acc = _dot_nt(a_ref[...], w_ref[...])
        else:
            acc = jnp.dot(a_ref[...], w_ref[...], preferred_element_type=F32)
        if scale != 1.0:
            acc = scale * acc
        if residual is not None:
            acc = acc + r_ref[...]
        o_ref[...] = acc.astype(out_dtype)

    w_spec = pl.BlockSpec((tn, K), lambda i, j: (j, 0)) if trans_w else pl.BlockSpec((K, tn), lambda i, j: (0, j))
    in_specs = [pl.BlockSpec((tm, K), lambda i, j: (i, 0)), w_spec]
    args = [a, w]
    if residual is not None:
        in_specs.append(pl.BlockSpec((tm, tn), lambda i, j: (i, j)))
        args.append(residual)
    return _call(body, name, (T // tm, N // tn), in_specs, [pl.BlockSpec((tm, tn), lambda i, j: (i, j))],
                 [jax.ShapeDtypeStruct((T, N), out_dtype)], [], ("parallel", "arbitrary"), args, carry)


def _mm_tn_pair(a, b, c_arr, axis, name, carry=None):
    T, M = a.shape
    N = b.shape[1]
    tk = _tile(T, 2048, 16)
    nk = T // tk
    nq = 4
    if axis == 1:
        rows, cols = M // 2, N // nq
        a_spec = pl.BlockSpec((tk, rows), lambda p, j, k, cr: (k, jnp.where(p == 0, 1 - cr[0], cr[0])))
        b_spec = pl.BlockSpec((tk, cols), lambda p, j, k, cr: (k, j))
        s_shape, land_shape = (1, rows, N), (rows, N)
        s_spec = pl.BlockSpec((None, rows, cols), lambda p, j, k, cr: (0, 0, j * p))
    else:
        rows, cols = M // nq, N // 2
        a_spec = pl.BlockSpec((tk, rows), lambda p, j, k, cr: (k, j))
        b_spec = pl.BlockSpec((tk, cols), lambda p, j, k, cr: (k, jnp.where(p == 0, 1 - cr[0], cr[0])))
        s_shape, land_shape = (nq, rows, cols), (nq, rows, cols)
        s_spec = pl.BlockSpec((None, rows, cols), lambda p, j, k, cr: (j * p, 0, 0))

    def body(c_ref, a_ref, b_ref, s_ref, land, acc_ref, stage, got, send_sems, recv_sems, loc_sem):
        p, j, k = pl.program_id(0), pl.program_id(1), pl.program_id(2)
        x, y, c = _me()

        def tile(jj):
            return land.at[:, pl.ds(jj * cols, cols)] if axis == 1 else land.at[jj]

        def send(jj):
            return _remote(stage, tile(jj), send_sems, recv_sems, jj, (x, y, 1 - c))

        @pl.when(k == 0)
        def _():
            acc_ref[...] = jnp.zeros_like(acc_ref)

        acc_ref[...] += _dot_tn(a_ref[...], b_ref[...])

        def fetch(jj):
            return pltpu.make_async_copy(tile(jj), got, loc_sem.at[0])

        for jj in range(nq):
            @pl.when(jnp.logical_and(k == nk - 1, jnp.logical_and(p == 0, j == jj)))
            def _():
                if jj > 0:
                    send(jj - 1).wait_send()
                stage[...] = acc_ref[...].astype(BF16)
                send(jj).start()

            @pl.when(jnp.logical_and(k == max(nk - 2, 0), jnp.logical_and(p == 1, j == jj)))
            def _():
                if jj == 0:
                    send(nq - 1).wait_send()
                send(jj).wait_recv()
                fetch(jj).start()

            @pl.when(jnp.logical_and(k == nk - 1, jnp.logical_and(p == 1, j == jj)))
            def _():
                fetch(jj).wait()
                s_ref[...] = (acc_ref[...] + got[...].astype(F32)).astype(BF16)

    return _call(
        body, name, (2, nq, nk), [a_spec, b_spec], [s_spec, ANY],
        [jax.ShapeDtypeStruct(s_shape, BF16), jax.ShapeDtypeStruct(land_shape, BF16)],
        [pltpu.VMEM((rows, cols), F32), pltpu.VMEM((rows, cols), BF16), pltpu.VMEM((rows, cols), BF16),
         pltpu.SemaphoreType.DMA((nq,)), pltpu.SemaphoreType.DMA((nq,)), pltpu.SemaphoreType.DMA((1,))],
        ("arbitrary", "arbitrary", "arbitrary"), (a, b), carry, (c_arr,))


GELU_K = math.sqrt(2.0 / math.pi)
GELU_C = 0.044715


def _gelu_and_grad(v):
    u = GELU_K * (v + GELU_C * v * v * v)
    th = jnp.tanh(u)
    g = 0.5 * v * (1.0 + th)
    dg = 0.5 * (1.0 + th) + 0.5 * v * (1.0 - th * th) * GELU_K * (1.0 + 3.0 * GELU_C * v * v)
    return g, dg


def _neg_expm1(v):
    poly = v * (1.0 + v * (0.5 + v * (1.0 / 6 + v * (1.0 / 24 + v * (1.0 / 120 + v * (1.0 / 720))))))
    return jnp.where(v > -0.25, -poly, 1.0 - jnp.exp(v))


def _softplus_neg(lam):
    e = jnp.exp(-jnp.abs(lam))
    log1pe = jnp.where(e < 1e-4, e * (1.0 - 0.5 * e), jnp.log(1.0 + e))
    sp = jnp.maximum(-lam, 0.0) + log1pe
    dsp = -1.0 / (1.0 + jnp.exp(lam))
    return sp, dsp


def _earlier(ext, j):
    return pltpu.roll(ext, j, 0)[SUBLANES:, :]


def _later(ext, j):
    n = ext.shape[0]
    return pltpu.roll(ext, n - j, 0)[:n - SUBLANES, :]


def _taps(v, halo, K):
    ext = jnp.concatenate([halo, v], axis=0)
    return [v] + [_earlier(ext, j) for j in range(1, K)]


def _block_diag(vb, w_ref, nh, hd):
    return jnp.concatenate(
        [jnp.dot(vb[:, h * hd:(h + 1) * hd], w_ref[h], preferred_element_type=F32) for h in range(nh)], axis=1)


def _lru_gates(xc, wa_ref, ba_ref, wi_ref, bi_ref, sp, nh, hd):
    xcb = xc.astype(BF16)
    r = _sigmoid(_block_diag(xcb, wa_ref, nh, hd) + ba_ref[...])
    ig = _sigmoid(_block_diag(xcb, wi_ref, nh, hd) + bi_ref[...])
    log_a = -LRU_C * r * sp
    a = jnp.exp(log_a)
    mult = jnp.sqrt(_neg_expm1(2.0 * log_a))
    return xcb, r, ig, a, mult


def _mix_fwd(z, cw, cb, wa, ba, wi, bi, lam, sw, glo, gso, name, carry=None):
    T = z.shape[0]
    DL = cb.shape[1]
    DS = gso.shape[1]
    NH, HD = wa.shape[0], wa.shape[1]
    KL, KS = cw.shape[0], sw.shape[0]
    tt = _tile(T, 128, 16)
    o_g, o_b, o_c, o_x = DL, 2 * DL, 2 * DL + DS, 2 * DL + 2 * DS

    def body(z_ref, cw_ref, cb_ref, wa_ref, ba_ref, wi_ref, bi_ref, lam_ref, sw_ref, glo_ref, gso_ref,
             h_ref, y_ref, cx_ref, cp_ref, ch_ref):
        @pl.when(pl.program_id(0) == 0)
        def _():
            cx_ref[...] = jnp.zeros_like(cx_ref)
            cp_ref[...] = jnp.zeros_like(cp_ref)
            ch_ref[...] = jnp.zeros_like(ch_ref)

        def zcol(o, n):
            return z_ref[:, o:o + n].astype(F32)

        lx = zcol(0, DL)
        xs = _taps(lx, cx_ref[...], KL)
        cx_ref[...] = lx[tt - SUBLANES:, :]
        xc = cb_ref[...] + xs[0] * cw_ref[KL - 1:KL, :]
        for j in range(1, KL):
            xc = xc + xs[j] * cw_ref[KL - 1 - j:KL - j, :]
        sp, _ = _softplus_neg(lam_ref[...])
        _, _, ig, a, mult = _lru_gates(xc, wa_ref, ba_ref, wi_ref, bi_ref, sp, NH, HD)
        b = mult * (ig * xc)
        rows = lax.broadcasted_iota(jnp.int32, (tt, DL), 0)
        s = 1
        while s < tt:
            keep = rows >= s
            b = jnp.where(keep, a * pltpu.roll(b, s, 0) + b, b)
            a = jnp.where(keep, a * pltpu.roll(a, s, 0), a)
            s *= 2
        h = a * ch_ref[SUBLANES - 1:SUBLANES, :] + b
        ch_ref[...] = h[tt - SUBLANES:, :]
        h_ref[...] = h
        ge, _ = _gelu_and_grad(zcol(o_g, DL))
        ylru = h * ge
        y_ref[:, 0:DL] = (ylru * _rstd(ylru) * glo_ref[...]).astype(BF16)

        p = zcol(o_c, DS) * zcol(o_x, DS)
        ps = _taps(p, cp_ref[...], KS)
        cp_ref[...] = p[tt - SUBLANES:, :]
        cv = ps[0] * sw_ref[KS - 1:KS, :]
        for j in range(1, KS):
            cv = cv + ps[j] * sw_ref[KS - 1 - j:KS - j, :]
        ysc = zcol(o_b, DS) * cv
        y_ref[:, DL:DL + DS] = (ysc * _rstd(ysc) * gso_ref[...]).astype(BF16)

    def full(shape):
        return pl.BlockSpec(shape, lambda t: (0,) * len(shape))

    return _call(
        body, name, (T // tt,),
        [pl.BlockSpec((tt, z.shape[1]), lambda t: (t, 0)),
         full(cw.shape), full(cb.shape), full(wa.shape), full(ba.shape), full(wi.shape), full(bi.shape),
         full(lam.shape), full(sw.shape), full(glo.shape), full(gso.shape)],
        [pl.BlockSpec((tt, DL), lambda t: (t, 0)), pl.BlockSpec((tt, DL + DS), lambda t: (t, 0))],
        [jax.ShapeDtypeStruct((T, DL), F32), jax.ShapeDtypeStruct((T, DL + DS), BF16)],
        [pltpu.VMEM((SUBLANES, DL), F32), pltpu.VMEM((SUBLANES, DS), F32), pltpu.VMEM((SUBLANES, DL), F32)],
        ("arbitrary",), (z, cw, cb, wa, ba, wi, bi, lam, sw, glo, gso), carry)


V_BA, V_BI, V_LAM, V_CB, V_CW, V_SW, V_GLO, V_GSO, V_ROWS = 0, 1, 2, 3, 4, 8, 11, 12, 16


def _mix_bwd(z, h, dy, cw, cb, wa, ba, wi, bi, lam, sw, glo, gso, name):
    T = z.shape[0]
    DL = cb.shape[1]
    DS = gso.shape[1]
    NH, HD = wa.shape[0], wa.shape[1]
    KL, KS = cw.shape[0], sw.shape[0]
    tt = _tile(T, 64, 16)
    nt = T // tt
    ZH = 2 * SUBLANES
    o_g, o_b, o_c, o_x = DL, 2 * DL, 2 * DL + DS, 2 * DL + 2 * DS

    def body(z_ref, zh_ref, h_ref, hh_ref, dy_ref, cw_ref, cb_ref, wa_ref, ba_ref, wi_ref, bi_ref, lam_ref,
             sw_ref, glo_ref, gso_ref, dz_ref, dwa_ref, dwi_ref, vec_ref, cdx_ref, cdc_ref, cdh_ref):
        i = pl.program_id(0)
        tr = nt - 1 - i

        @pl.when(i == 0)
        def _():
            dwa_ref[...] = jnp.zeros_like(dwa_ref)
            dwi_ref[...] = jnp.zeros_like(dwi_ref)
            vec_ref[...] = jnp.zeros_like(vec_ref)
            cdx_ref[...] = jnp.zeros_like(cdx_ref)
            cdc_ref[...] = jnp.zeros_like(cdc_ref)
            cdh_ref[...] = jnp.zeros_like(cdh_ref)

        def acc_row(r, v):
            vec_ref[pl.ds(r, 1), :] += jnp.sum(v, axis=0, keepdims=True)

        has_prev = tr > 0
        rows = lax.broadcasted_iota(jnp.int32, (tt, DL), 0)

        def zcol(o, n):
            return z_ref[:, o:o + n].astype(F32)

        def zhalo(o, n):
            return jnp.where(has_prev, zh_ref[:, o:o + n].astype(F32)[SUBLANES:, :], 0.0)

        lx = zcol(0, DL)
        xs = _taps(lx, zhalo(0, DL), KL)
        xc = cb_ref[...] + xs[0] * cw_ref[KL - 1:KL, :]
        for j in range(1, KL):
            xc = xc + xs[j] * cw_ref[KL - 1 - j:KL - j, :]
        sp, dsp = _softplus_neg(lam_ref[...])
        xcb, r, ig, a, mult = _lru_gates(xc, wa_ref, ba_ref, wi_ref, bi_ref, sp, NH, HD)
        hv = h_ref[...]
        hprev = _earlier(jnp.concatenate([jnp.where(has_prev, hh_ref[...], 0.0), hv], axis=0), 1)
        gate = zcol(o_g, DL)
        ge, dge = _gelu_and_grad(gate)
        ylru = hv * ge

        d_ylru, dglo = _rms_bwd(dy_ref[:, 0:DL].astype(F32), ylru, glo_ref[...])
        vec_ref[pl.ds(V_GLO, 1), :] += dglo
        dz_ref[:, o_g:o_g + DL] = (d_ylru * hv * dge).astype(BF16)
        bq = d_ylru * ge
        aq = jnp.where(rows == tt - 1, 1.0, pltpu.roll(a, tt - 1, 0))
        s = 1
        while s < tt:
            keep = rows < tt - s
            bq = jnp.where(keep, aq * pltpu.roll(bq, tt - s, 0) + bq, bq)
            aq = jnp.where(keep, aq * pltpu.roll(aq, tt - s, 0), aq)
            s *= 2
        dhh = bq + aq * cdh_ref[0:1, :]
        cdh_ref[0:1, :] = a[0:1, :] * dhh[0:1, :]

        da = dhh * hprev
        dmult = dhh * (ig * xc)
        d_i = dhh * mult * xc
        dxc = dhh * mult * ig
        dlog = da * a - dmult * (a * a) / mult
        acc_row(V_LAM, dlog * (-LRU_C * r) * dsp)
        dpa = dlog * (-LRU_C * sp) * r * (1.0 - r)
        dpi = d_i * ig * (1.0 - ig)
        acc_row(V_BA, dpa)
        acc_row(V_BI, dpi)
        dpab = dpa.astype(BF16)
        dpib = dpi.astype(BF16)
        back = []
        for hh in range(NH):
            sl = slice(hh * HD, (hh + 1) * HD)
            dwa_ref[hh] += _dot_tn(xcb[:, sl], dpab[:, sl])
            dwi_ref[hh] += _dot_tn(xcb[:, sl], dpib[:, sl])
            back.append(_dot_nt(dpab[:, sl], wa_ref[hh]) + _dot_nt(dpib[:, sl], wi_ref[hh]))
        dxc = dxc + jnp.concatenate(back, axis=1)

        acc_row(V_CB, dxc)
        extd = jnp.concatenate([dxc, cdx_ref[...]], axis=0)
        cdx_ref[...] = dxc[0:SUBLANES, :]
        dlx = dxc * cw_ref[KL - 1:KL, :]
        acc_row(V_CW + KL - 1, dxc * xs[0])
        for j in range(1, KL):
            dlx = dlx + _later(extd, j) * cw_ref[KL - 1 - j:KL - j, :]
            acc_row(V_CW + KL - 1 - j, dxc * xs[j])
        dz_ref[:, 0:DL] = dlx.astype(BF16)

        sb = zcol(o_b, DS)
        sc = zcol(o_c, DS)
        sx = zcol(o_x, DS)
        p = sc * sx
        ps = _taps(p, zhalo(o_c, DS) * zhalo(o_x, DS), KS)
        cv = ps[0] * sw_ref[KS - 1:KS, :]
        for j in range(1, KS):
            cv = cv + ps[j] * sw_ref[KS - 1 - j:KS - j, :]
        d_ysc, dgso = _rms_bwd(dy_ref[:, DL:DL + DS].astype(F32), sb * cv, gso_ref[...])
        vec_ref[pl.ds(V_GSO, 1), :] += dgso
        dz_ref[:, o_b:o_b + DS] = (d_ysc * cv).astype(BF16)
        dcv = d_ysc * sb
        extc = jnp.concatenate([dcv, cdc_ref[...]], axis=0)
        cdc_ref[...] = dcv[0:SUBLANES, :]
        dp = dcv * sw_ref[KS - 1:KS, :]
        acc_row(V_SW + KS - 1, dcv * ps[0])
        for j in range(1, KS):
            dp = dp + _later(extc, j) * sw_ref[KS - 1 - j:KS - j, :]
            acc_row(V_SW + KS - 1 - j, dcv * ps[j])
        dz_ref[:, o_c:o_c + DS] = (dp * sx).astype(BF16)
        dz_ref[:, o_x:o_x + DS] = (dp * sc).astype(BF16)

    def full(shape):
        return pl.BlockSpec(shape, lambda t: (0,) * len(shape))

    def rev(t):
        return nt - 1 - t

    def halo(t, rows):
        return jnp.maximum(rev(t) * (tt // rows) - 1, 0)

    return pl.pallas_call(
        body, name=name, grid=(nt,),
        in_specs=[pl.BlockSpec((tt, z.shape[1]), lambda t: (rev(t), 0)),
                  pl.BlockSpec((ZH, z.shape[1]), lambda t: (halo(t, ZH), 0)),
                  pl.BlockSpec((tt, DL), lambda t: (rev(t), 0)),
                  pl.BlockSpec((SUBLANES, DL), lambda t: (halo(t, SUBLANES), 0)),
                  pl.BlockSpec((tt, DL + DS), lambda t: (rev(t), 0)),
                  full(cw.shape), full(cb.shape), full(wa.shape), full(ba.shape), full(wi.shape), full(bi.shape),
                  full(lam.shape), full(sw.shape), full(glo.shape), full(gso.shape)],
        out_specs=[pl.BlockSpec((tt, z.shape[1]), lambda t: (rev(t), 0)),
                   full(wa.shape), full(wi.shape), full((V_ROWS, DL))],
        out_shape=[jax.ShapeDtypeStruct(z.shape, BF16), jax.ShapeDtypeStruct(wa.shape, F32),
                   jax.ShapeDtypeStruct(wi.shape, F32), jax.ShapeDtypeStruct((V_ROWS, DL), F32)],
        scratch_shapes=[pltpu.VMEM((SUBLANES, DL), F32), pltpu.VMEM((SUBLANES, DS), F32),
                        pltpu.VMEM((SUBLANES, DL), F32)],
        compiler_params=_params("arbitrary"),
    )(z, z, h, h, dy, cw, cb, wa, ba, wi, bi, lam, sw, glo, gso)


def _quad_add(s, r2, qc, axis, name):
    _, R, W = r2.shape
    tr = _tile(R, 256, 16)

    def body(qc_ref, s_ref, r0_ref, r1_ref, r2_ref, o_ref):
        o_ref[...] = ((s_ref[...].astype(F32) + r0_ref[...].astype(F32)) + r1_ref[...].astype(F32)) + r2_ref[...].astype(F32)

    blk = (None, tr, W)
    if axis == 1:
        own = pl.BlockSpec(blk, lambda i, qr: (0, i, qr[0]))
    else:
        own = pl.BlockSpec(blk, lambda i, qr: (qr[0], i, 0))
    return pl.pallas_call(
        body, name=name,
        grid_spec=pltpu.PrefetchScalarGridSpec(
            num_scalar_prefetch=1, grid=(R // tr,),
            in_specs=[own] + [pl.BlockSpec(blk, lambda i, qr, j=j: (j, i, 0)) for j in range(3)],
            out_specs=pl.BlockSpec(blk, lambda i, qr: (qr[1], i, 0))),
        out_shape=jax.ShapeDtypeStruct((2, R, W), F32),
        compiler_params=_params("parallel"),
    )(qc, s, r2, r2, r2)


CAST_BLOCKS = 4


def _cast_into_full(shards, qc, axes, name, carry=None):
    M = len(shards)
    nb = CAST_BLOCKS

    def body(qc_ref, *refs):
        for s_ref, o_ref in zip(refs[:M], refs[M:]):
            o_ref[...] = s_ref[...].astype(BF16)

    in_specs, out_specs, out_shape = [], [], []
    for s, ax in zip(shards, axes):
        R, W = s.shape
        Rh = R // 2
        tr = Rh // nb
        in_specs.append(pl.BlockSpec((tr, W), lambda hf, i, qr: (hf * nb + i, 0)))
        if ax == 1:
            out_shape.append(jax.ShapeDtypeStruct((2, Rh, 4 * W), BF16))
            out_specs.append(pl.BlockSpec((None, tr, W), lambda hf, i, qr: (hf, i, qr[0])))
        else:
            out_shape.append(jax.ShapeDtypeStruct((8, Rh, W), BF16))
            out_specs.append(pl.BlockSpec((None, tr, W), lambda hf, i, qr: (2 * qr[0] + hf, i, 0)))
    return _call(body, name, (2, nb), in_specs, out_specs, out_shape, [], ("parallel", "parallel"), shards, carry, (qc,))


def _adamw(w, g, m, v, name):
    R, C = w.shape
    tr = _tile(R, 256, SUBLANES)
    tc = C // 2 if g.ndim == 3 else _tile(C, 2048, LANES)
    c1 = 1.0 - ADAM_B1 ** ADAM_STEP
    c2 = 1.0 - ADAM_B2 ** ADAM_STEP

    def body(w_ref, g_ref, m_ref, v_ref, d_ref, mo_ref, vo_ref, go_ref):
        gv = g_ref[...]
        go_ref[...] = gv
        mn = ADAM_B1 * m_ref[...] + (1.0 - ADAM_B1) * gv
        vn = ADAM_B2 * v_ref[...] + (1.0 - ADAM_B2) * (gv * gv)
        mo_ref[...] = mn
        vo_ref[...] = vn
        d_ref[...] = -ADAM_LR * ((mn / c1) / (jnp.sqrt(vn / c2) + ADAM_EPS) + ADAM_WD * w_ref[...])

    blk = pl.BlockSpec((tr, tc), lambda i, j: (i, j))
    g_blk = pl.BlockSpec((None, tr, tc), lambda i, j: (j, i, 0)) if g.ndim == 3 else blk
    sh = jax.ShapeDtypeStruct((R, C), F32)
    return pl.pallas_call(
        body, name=name, grid=(R // tr, C // tc),
        in_specs=[blk, g_blk, blk, blk], out_specs=[blk] * 4, out_shape=[sh] * 4,
        compiler_params=_params("parallel", "parallel"),
    )(w, g, m, v)


def _other_chips(x, y):
    return [(1 - x, y), (x, 1 - y), (1 - x, 1 - y)]


def _remote(src, dst, send_sems, recv_sems, idx, dev):
    return pltpu.make_async_remote_copy(src_ref=src, dst_ref=dst, send_sem=send_sems.at[idx], recv_sem=recv_sems.at[idx],
                                        device_id=dev, device_id_type=MESH)


def _gather_carry(fulls, axes, pieces=None):
    M = len(fulls)

    def win(outs, m, qq, cc):
        Rh = fulls[m].shape[1]
        k, n = pieces[m] if pieces is not None and pieces[m] is not None else (0, 1)
        rows = pl.ds(k * (Rh // n), Rh // n)
        if axes[m] == 1:
            W = fulls[m].shape[2] // 4
            return outs[m].at[cc, rows, pl.ds(pl.multiple_of(qq * W, LANES), W)]
        return outs[m].at[2 * qq + cc, rows, :]

    def ici(outs, sems, m, j, src_q):
        x, y, c = _me()
        cx, cy = _other_chips(x, y)[j]
        blk = win(outs, m, src_q, c)
        return _remote(blk, blk, sems[0], sems[1], 6 * m + j, (cx, cy, c))

    def d2d(outs, sems, m, j, half):
        x, y, c = _me()
        cx, cy = _other_chips(x, y)[j]
        blk = win(outs, m, 2 * cx + cy, half)
        return _remote(blk, blk, sems[0], sems[1], 6 * m + 3 + j, (x, y, 1 - c))

    def start(ins, outs, sems):
        x, y, c = _me()
        for m in range(M):
            for j in range(3):
                ici(outs, sems, m, j, 2 * x + y).start()

    def middle(ins, outs, sems):
        x, y, c = _me()
        for m in range(M):
            for j, (cx, cy) in enumerate(_other_chips(x, y)):
                ici(outs, sems, m, j, 2 * cx + cy).wait_recv()
                d2d(outs, sems, m, j, c).start()

    def finish(ins, outs, sems):
        x, y, c = _me()
        for m in range(M):
            for j in range(3):
                d2d(outs, sems, m, j, 1 - c).wait_recv()
        for m in range(M):
            for j in range(3):
                ici(outs, sems, m, j, 2 * x + y).wait_send()
                d2d(outs, sems, m, j, c).wait_send()

    return _Carry(fulls, [jax.ShapeDtypeStruct(f.shape, f.dtype) for f in fulls], {m: m for m in range(M)},
                  [pltpu.SemaphoreType.DMA((6 * M,)), pltpu.SemaphoreType.DMA((6 * M,))], start, finish, middle)


def _gather_two_way_carry(fulls, axes):
    M = len(fulls)

    def part(outs, m, qq, cc, p):
        Rp = fulls[m].shape[1] // 2
        rows = pl.ds(p * Rp, Rp)
        if axes[m] == 1:
            W = fulls[m].shape[2] // 4
            return outs[m].at[cc, rows, pl.ds(pl.multiple_of(qq * W, LANES), W)]
        return outs[m].at[2 * qq + cc, rows, :]

    def copies(outs, sems):
        x, y, c = _me()
        q, qx, qy, qd = 2 * x + y, 2 * (1 - x) + y, 2 * x + (1 - y), 2 * (1 - x) + (1 - y)
        xn, yn, sib = (1 - x, y, c), (x, 1 - y, c), (x, y, 1 - c)
        table = {}
        for m in range(M):
            def cp(blk, k, dev):
                return _remote(blk, blk, sems[0], sems[1], 12 * m + k, dev)
            own0, own1 = part(outs, m, q, c, 0), part(outs, m, q, c, 1)
            table[m] = dict(
                to=[cp(own0, 0, xn), cp(own1, 1, yn), cp(own1, 2, xn), cp(own0, 3, yn)],
                landed=[cp(part(outs, m, qx, c, 0), 0, xn), cp(part(outs, m, qy, c, 1), 1, yn),
                        cp(part(outs, m, qx, c, 1), 2, xn), cp(part(outs, m, qy, c, 0), 3, yn),
                        cp(part(outs, m, qd, c, 0), 4, yn), cp(part(outs, m, qd, c, 1), 5, xn)],
                passed=[cp(part(outs, m, qx, c, 0), 4, yn), cp(part(outs, m, qy, c, 1), 5, xn)],
                handed=[cp(part(outs, m, qq, c, p), 6 + k, sib)
                        for k, (qq, p) in enumerate([(qx, 0), (qy, 1), (qx, 1), (qy, 0), (qd, 0), (qd, 1)])],
                taken=[cp(part(outs, m, qq, 1 - c, p), 6 + k, sib)
                       for k, (qq, p) in enumerate([(qx, 0), (qy, 1), (qx, 1), (qy, 0), (qd, 0), (qd, 1)])])
        return table

    def start(ins, outs, sems):
        t = copies(outs, sems)
        for m in range(M):
            for cp in t[m]['to']:
                cp.start()

    def finish(ins, outs, sems):
        t = copies(outs, sems)
        for m in range(M):
            for k in range(4):
                t[m]['landed'][k].wait_recv()
                if k < 2:
                    t[m]['passed'][k].start()
                t[m]['handed'][k].start()
        for m in range(M):
            for k in (4, 5):
                t[m]['landed'][k].wait_recv()
                t[m]['handed'][k].start()
        for m in range(M):
            for cp in t[m]['taken']:
                cp.wait_recv()
            for cp in t[m]['to'] + t[m]['passed'] + t[m]['handed']:
                cp.wait_send()

    return _Carry(fulls, [jax.ShapeDtypeStruct(f.shape, f.dtype) for f in fulls], {m: m for m in range(M)},
                  [pltpu.SemaphoreType.DMA((12 * M,)), pltpu.SemaphoreType.DMA((12 * M,))], start, finish)


def _chip_exchange_carry(sums, axes):
    M = len(sums)
    out_shape = []
    for s, ax in zip(sums, axes):
        _, Rh, C = s.shape
        out_shape.append(jax.ShapeDtypeStruct((3, Rh, C // 4 if ax == 1 else C), s.dtype))

    def copies(ins, outs, sems):
        x, y, c = _me()
        cps = []
        for m in range(M):
            for j, (cx, cy) in enumerate(_other_chips(x, y)):
                qj = 2 * cx + cy
                if axes[m] == 1:
                    W = sums[m].shape[2] // 4
                    src = ins[m].at[0, :, pl.ds(pl.multiple_of(qj * W, LANES), W)]
                else:
                    src = ins[m].at[qj]
                cps.append(_remote(src, outs[m].at[j], sems[0], sems[1], 3 * m + j, (cx, cy, c)))
        return cps

    def start(ins, outs, sems):
        for cp in copies(ins, outs, sems):
            cp.start()

    def finish(ins, outs, sems):
        for cp in copies(ins, outs, sems):
            cp.wait()

    return _Carry(sums, out_shape, {}, [pltpu.SemaphoreType.DMA((3 * M,)), pltpu.SemaphoreType.DMA((3 * M,))],
                  start, finish)


def _pair_share_carry(bufs):
    M = len(bufs)

    def start(ins, outs, sems):
        x, y, c = _me()
        for m in range(M):
            _remote(outs[m].at[c], outs[m].at[c], sems[0], sems[1], m, (x, y, 1 - c)).start()

    def finish(ins, outs, sems):
        x, y, c = _me()
        for m in range(M):
            _remote(outs[m].at[c], outs[m].at[c], sems[0], sems[1], m, (x, y, 1 - c)).wait_send()
            _remote(outs[m].at[1 - c], outs[m].at[1 - c], sems[0], sems[1], m, (x, y, 1 - c)).wait_recv()

    return _Carry(bufs, [jax.ShapeDtypeStruct(b.shape, b.dtype) for b in bufs], {m: m for m in range(M)},
                  [pltpu.SemaphoreType.DMA((M,)), pltpu.SemaphoreType.DMA((M,))], start, finish)


def _allreduce_small(v, name):
    R, W = v.shape
    Rh = R // 2

    def body(v_ref, o_ref, sib, quad, send_sems, recv_sems):
        x, y, c = _me()
        q = 2 * x + y
        sibling = (x, y, 1 - c)
        pair = _remote(v_ref, sib, send_sems, recv_sems, 0, sibling)
        pair.start()
        pair.wait()
        mine = pl.ds(pl.multiple_of(c * Rh, SUBLANES), Rh)
        quad[0] = v_ref[mine, :] + sib[mine, :]
        cps = []
        for k in (1, 2, 3):
            peer = (1 - x if k & 2 else x, 1 - y if k & 1 else y, c)
            cps.append(_remote(quad.at[0], quad.at[k], send_sems, recv_sems, k, peer))
            cps[-1].start()
        for cp in cps:
            cp.wait()
        acc = quad[q]
        for p in (1, 2, 3):
            acc = acc + quad[jnp.bitwise_xor(q, p)]
        o_ref[mine, :] = acc
        theirs = pl.ds(pl.multiple_of((1 - c) * Rh, SUBLANES), Rh)
        done = _remote(o_ref.at[mine, :], o_ref.at[mine, :], send_sems, recv_sems, 4, sibling)
        done.start()
        done.wait_send()
        _remote(o_ref.at[theirs, :], o_ref.at[theirs, :], send_sems, recv_sems, 4, sibling).wait_recv()

    vm = pl.BlockSpec(memory_space=pltpu.VMEM)
    return pl.pallas_call(
        body, name=name, in_specs=[vm], out_specs=vm, out_shape=jax.ShapeDtypeStruct((R, W), F32),
        scratch_shapes=[pltpu.VMEM((R, W), F32), pltpu.VMEM((4, Rh, W), F32),
                        pltpu.SemaphoreType.DMA((5,)), pltpu.SemaphoreType.DMA((5,))],
        compiler_params=pltpu.CompilerParams(vmem_limit_bytes=VMEM_LIMIT),
    )(v)


def _pack(pieces):
    flat = []
    for p in pieces:
        p = p.reshape(-1).astype(F32)
        pad = (-p.shape[0]) % PACK_ALIGN
        flat.append(jnp.pad(p, (0, pad)).reshape(-1, PACK_W))
    if sum(f.shape[0] for f in flat) % (2 * SUBLANES):
        flat.append(jnp.zeros((SUBLANES, PACK_W), F32))
    return jnp.concatenate(flat, axis=0)


def _unpack(packed, shapes):
    out, row = [], 0
    for shp in shapes:
        n = math.prod(shp)
        rows = -(-n // PACK_ALIGN) * SUBLANES
        out.append(packed[row:row + rows].reshape(-1)[:n].reshape(shp))
        row += rows
    return out


def kernel(x, ffn1_norm, ffn1_w_gate, ffn1_w_up, ffn1_w_down, mix_norm, w_in, lru_conv_w, lru_conv_b, lru_w_a, lru_b_a, lru_w_i, lru_b_i, lru_lambda, sc_conv_w, lru_out_norm, sc_out_norm, w_out, ffn2_norm, ffn2_w_gate, ffn2_w_up, ffn2_w_down, final_norm, loss_target, m_ffn1_norm, m_ffn1_w_gate, m_ffn1_w_up, m_ffn1_w_down, m_mix_norm, m_w_in, m_lru_conv_w, m_lru_conv_b, m_lru_w_a, m_lru_b_a, m_lru_w_i, m_lru_b_i, m_lru_lambda, m_sc_conv_w, m_lru_out_norm, m_sc_out_norm, m_w_out, m_ffn2_norm, m_ffn2_w_gate, m_ffn2_w_up, m_ffn2_w_down, m_final_norm, v_ffn1_norm, v_ffn1_w_gate, v_ffn1_w_up, v_ffn1_w_down, v_mix_norm, v_w_in, v_lru_conv_w, v_lru_conv_b, v_lru_w_a, v_lru_b_a, v_lru_w_i, v_lru_b_i, v_lru_lambda, v_sc_conv_w, v_lru_out_norm, v_sc_out_norm, v_w_out, v_ffn2_norm, v_ffn2_w_gate, v_ffn2_w_up, v_ffn2_w_down, v_final_norm):
    vals = locals()
    w = {n: vals[n] for n in WEIGHTS}
    mom = {n: vals["m_" + n] for n in WEIGHTS}
    var = {n: vals["v_" + n] for n in WEIGHTS}

    xi, yi, ci = _me()
    qi = 2 * xi + yi
    c_arr = jnp.reshape(ci, (1,)).astype(jnp.int32)
    qc_arr = jnp.stack([qi, ci]).astype(jnp.int32)

    T, D = x.shape[1], x.shape[2]
    xt = x.reshape(T, D)
    target = loss_target.reshape(T, D)
    DL = lru_conv_b.shape[-1]
    NH, HD = lru_w_a.shape[1], lru_w_a.shape[2]
    KL, KS = lru_conv_w.shape[1], sc_conv_w.shape[1]
    DLq = lru_conv_w.shape[2]

    axis_of = dict(zip(BIG, BIG_AXIS))
    placed, full = {}, {}

    def gather(names, pieces={}):
        return _gather_carry([placed[n] for n in names], [axis_of[n] for n in names], [pieces.get(n) for n in names])

    def gathered(names, views, unfinished=()):
        for n, g in zip(names, views):
            if n in unfinished:
                placed[n] = g
            else:
                full[n] = (g.reshape(2 * g.shape[1], g.shape[2]) if axis_of[n] == 1
                           else g.reshape(8 * g.shape[1], g.shape[2]))

    first, rest = 'ffn1_w_gate', [n for n in BIG if n != 'ffn1_w_gate']
    placed[first] = _cast_into_full([w[first][0]], qc_arr, [axis_of[first]], "cast_first_weight")[0]
    res = _cast_into_full([w[n][0] for n in rest], qc_arr, [axis_of[n] for n in rest], "cast_other_weights",
                          _gather_two_way_carry([placed[first]], [axis_of[first]]))
    placed.update(zip(rest, res[:len(rest)]))
    gathered([first], res[len(rest):])

    taps = jnp.zeros((2 * SUBLANES, DL), F32)
    taps = lax.dynamic_update_slice(taps, lru_conv_w[0], (0, qi * DLq))
    taps = lax.dynamic_update_slice(taps, sc_conv_w[0], (KL, qi * DLq))
    taps = _allreduce_small(jnp.where(ci == 0, taps, 0.0), "gather_conv_taps")
    cw, sw = taps[0:KL], taps[KL:KL + KS]

    cb = lru_conv_b
    wa, wi = lru_w_a[0].astype(BF16), lru_w_i[0].astype(BF16)
    ba, bi = lru_b_a.reshape(1, DL), lru_b_i.reshape(1, DL)
    mix_args = (cw, cb, wa, ba, wi, bi, lru_lambda, sw, lru_out_norm, sc_out_norm)
    gf = final_norm.reshape(1, D)

    names = ['ffn1_w_up', 'w_out']
    res = _norm_mm(xt, ffn1_norm, full['ffn1_w_gate'], "ffn1_gate", gather(names))
    n1, G1 = res[:2]
    gathered(names, res[2:])
    names = ['ffn1_w_down', 'ffn2_w_down']
    res = _ffn_up(n1, G1, full['ffn1_w_up'], "ffn1_up", gather(names, {'ffn2_w_down': (0, 2)}))
    U1, H1f = res[:2]
    gathered(names, res[2:], unfinished=['ffn2_w_down'])
    names = ['w_in', 'ffn2_w_down']
    res = _mm_fullk(H1f, full['ffn1_w_down'], False, xt, F32, "ffn1_down", gather(names, {'ffn2_w_down': (1, 2)}),
                    FFN_RESIDUAL_SCALE)
    x1 = res[0]
    gathered(names, res[1:])
    names = ['ffn2_w_gate']
    res = _norm_mm(x1, mix_norm, full['w_in'], "mix_in_proj", gather(names))
    n2, z = res[:2]
    gathered(names, res[2:])
    names = ['ffn2_w_up']
    res = _mix_fwd(z, *mix_args, "mix_fwd", gather(names))
    h, ymix = res[:2]
    gathered(names, res[2:])
    x2 = _mm_fullk(ymix, full['w_out'], False, x1, F32, "mix_out_proj")[0]
    x3, n3, G2, U2 = _ffn_fwd(x2, ffn2_norm, full['ffn2_w_gate'], full['ffn2_w_up'], full['ffn2_w_down'], "ffn2_fwd")
    dx3, d3b, sqerr, dgf = _loss_head(x3, gf, target, "loss_head")

    sums, halves, shared = {}, {}, {}

    def dw(n, a, b, name, carry=None):
        res = _mm_tn_pair(a, b, c_arr, axis_of[n], name, carry)
        sums[n] = res[0]
        return res[2:]

    def chip_carry(names):
        return _chip_exchange_carry([sums[n] for n in names], [axis_of[n] for n in names])

    def chip_add(names, recv):
        for n, r in zip(names, recv):
            halves[n] = _quad_add(sums[n], r, qc_arr, axis_of[n], "grad_chip_add_" + n)

    dG2, dU2, H2 = _ffn_bwd_hidden(d3b, G2, U2, full['ffn2_w_down'], "ffn2_bwd_hidden")
    dn3 = _ffn_bwd_input(dG2, dU2, full['ffn2_w_gate'], full['ffn2_w_up'], "ffn2_bwd_input")[0]
    dx2, dx2b, dg_ffn2 = _rms_bwd_res(dn3, x2, ffn2_norm, dx3, 1.0, "ffn2_norm_bwd")
    dw('ffn2_w_gate', n3, dG2, "ffn2_dwg")
    dw('ffn2_w_up', n3, dU2, "ffn2_dwu")
    dw('ffn2_w_down', H2, d3b, "ffn2_dwd")
    dy = _mm_fullk(dx2b, full['w_out'], True, None, BF16, "mix_out_bwd")[0]
    dz, dwa, dwi, vec = _mix_bwd(z, h, dy, *mix_args, "mix_bwd")
    dg_mix, dx1, d1b = _mm_nt_norm_bwd(dz, full['w_in'], x1, mix_norm, dx2, FFN_RESIDUAL_SCALE, "mix_in_bwd")

    def share_carry(names):
        return _pair_share_carry([halves[n] for n in names])

    res = _ffn_bwd_hidden(d1b, G1, U1, full['ffn1_w_down'], "ffn1_bwd_hidden", chip_carry(['ffn2_w_gate']))
    dG1, dU1, H1 = res[:3]
    chip_add(['ffn2_w_gate'], res[3:])
    res = _ffn_bwd_input(dG1, dU1, full['ffn1_w_gate'], full['ffn1_w_up'], "ffn1_bwd_input",
                         chip_carry(['ffn2_w_up', 'ffn2_w_down']))
    dn1 = res[0]
    chip_add(['ffn2_w_up', 'ffn2_w_down'], res[1:])
    dx0, _, dg_ffn1 = _rms_bwd_res(dn1, xt, ffn1_norm, dx1, 1.0, "ffn1_norm_bwd")
    names = ['ffn2_w_gate', 'ffn2_w_up', 'ffn2_w_down']
    shared.update(zip(names, dw('ffn1_w_gate', n1, dG1, "ffn1_dwg", share_carry(names))))
    chip_add(['ffn1_w_gate'], dw('ffn1_w_up', n1, dU1, "ffn1_dwu", chip_carry(['ffn1_w_gate'])))
    chip_add(['ffn1_w_up'], dw('ffn1_w_down', H1, d1b, "ffn1_dwd", chip_carry(['ffn1_w_up'])))
    names = ['ffn1_w_gate', 'ffn1_w_up']
    res = dw('w_in', n2, dz, "mix_dwin", _merge_carries(chip_carry(['ffn1_w_down']), share_carry(names)))
    chip_add(['ffn1_w_down'], res[:1])
    shared.update(zip(names, res[1:]))
    res = dw('w_out', ymix, dx2b, "mix_dwout", _merge_carries(chip_carry(['w_in']), share_carry(['ffn1_w_down'])))
    chip_add(['w_in'], res[:1])
    shared['ffn1_w_down'] = res[1]
    chip_add(['w_out'], _run_carry(chip_carry(['w_out']), "grad_chip_exchange_w_out"))
    names = ['w_in', 'w_out']
    shared.update(zip(names, _run_carry(share_carry(names), "grad_pair_share")))
    out_g, out_d, out_m, out_v = {}, {}, {}, {}
    for n in BIG:
        shp = w[n].shape
        g = shared[n].reshape(shp[1], shp[2]) if axis_of[n] == 1 else shared[n]
        outs = _adamw(w[n][0], g, mom[n][0], var[n][0], "adamw_" + n)
        out_d[n], out_m[n], out_v[n], out_g[n] = (a.reshape(shp) for a in outs)

    small = [n for n in WEIGHTS if n not in BIG]
    local_small = {
        'ffn1_norm': dg_ffn1, 'mix_norm': dg_mix, 'lru_conv_w': vec[V_CW:V_CW + KL], 'lru_conv_b': vec[V_CB],
        'lru_w_a': dwa, 'lru_b_a': vec[V_BA], 'lru_w_i': dwi, 'lru_b_i': vec[V_BI], 'lru_lambda': vec[V_LAM],
        'sc_conv_w': vec[V_SW:V_SW + KS], 'lru_out_norm': vec[V_GLO], 'sc_out_norm': vec[V_GSO],
        'ffn2_norm': dg_ffn2, 'final_norm': dgf,
    }
    full_shapes = [local_small[n].shape for n in small] + [(1,)]
    reduced = _allreduce_small(_pack([local_small[n] for n in small] + [sqerr[0, 0:1]]), "allreduce_small")
    reduced = _unpack(reduced, full_shapes)
    loss = (0.5 / D) * reduced[-1][0]
    gsm = {}
    for n, g in zip(small, reduced[:-1]):
        if n in SMALL_SHARDED:
            g = lax.dynamic_slice(g, (0, qi * DLq), (g.shape[0], DLq))
        gsm[n] = g.reshape(w[n].shape)
    small_shapes = [w[n].shape for n in small]
    d_s, m_s, v_s, _ = _adamw(_pack([w[n] for n in small]), _pack([gsm[n] for n in small]),
                              _pack([mom[n] for n in small]), _pack([var[n] for n in small]), "adamw_small")
    for n, d, mn, vn in zip(small, _unpack(d_s, small_shapes), _unpack(m_s, small_shapes), _unpack(v_s, small_shapes)):
        out_g[n], out_d[n], out_m[n], out_v[n] = gsm[n], d, mn, vn

    return (loss, dx0.reshape(x.shape), *[out_g[n] for n in WEIGHTS], *[out_d[n] for n in WEIGHTS],
            *[out_m[n] for n in WEIGHTS], *[out_v[n] for n in WEIGHTS])
```

```python
import math

import jax
import jax.numpy as jnp
from jax import lax
from jax.experimental import pallas as pl
from jax.experimental.pallas import tpu as pltpu

F32 = jnp.float32
BF16 = jnp.bfloat16
MESH = pl.DeviceIdType.MESH
ANY = pl.BlockSpec(memory_space=pl.ANY)

NORM_EPS = 1e-6
LRU_C = 8.0
FFN_RESIDUAL_SCALE = 0.5
ADAM_LR = 0.001
ADAM_B1 = 0.9
ADAM_B2 = 0.999
ADAM_EPS = 1e-08
ADAM_WD = 0.01
ADAM_STEP = 10

V7X_VMEM_BYTES = 64 * 2**20
VMEM_LIMIT = V7X_VMEM_BYTES - 8 * 2**20
LANES = 128
SUBLANES = 8
PACK_W = LANES
PACK_ALIGN = SUBLANES * PACK_W

WEIGHTS = ['ffn1_norm', 'ffn1_w_gate', 'ffn1_w_up', 'ffn1_w_down', 'mix_norm', 'w_in', 'lru_conv_w', 'lru_conv_b',
           'lru_w_a', 'lru_b_a', 'lru_w_i', 'lru_b_i', 'lru_lambda', 'sc_conv_w', 'lru_out_norm', 'sc_out_norm',
           'w_out', 'ffn2_norm', 'ffn2_w_gate', 'ffn2_w_up', 'ffn2_w_down', 'final_norm']
BIG = ['ffn1_w_gate', 'ffn1_w_up', 'ffn1_w_down', 'w_in', 'w_out', 'ffn2_w_gate', 'ffn2_w_up', 'ffn2_w_down']
BIG_AXIS = [1, 1, 0, 1, 0, 1, 1, 0]
SMALL_SHARDED = ['lru_conv_w', 'sc_conv_w']


def _tile(n, pref, mult):
    if n <= pref:
        return n
    t = (pref // mult) * mult
    while t >= mult:
        if n % t == 0:
            return t
        t -= mult
    return n


def _params(*sem):
    return pltpu.CompilerParams(dimension_semantics=sem, vmem_limit_bytes=VMEM_LIMIT)


def _me():
    return lax.axis_index("x"), lax.axis_index("y"), lax.axis_index("c")


def _sigmoid(v):
    return 1.0 / (1.0 + jnp.exp(-v))


def _rstd(v):
    return lax.rsqrt(jnp.mean(v * v, axis=-1, keepdims=True) + NORM_EPS)


def _rms_bwd(dy, v, gain):
    r = _rstd(v)
    w = gain * dy
    dv = r * w - v * (r * r * r) * jnp.mean(v * w, axis=-1, keepdims=True)
    dgain = jnp.sum(dy * v * r, axis=0, keepdims=True)
    return dv, dgain


def _dot_nt(a, b):
    return lax.dot_general(a, b, (((1,), (1,)), ((), ())), preferred_element_type=F32)


def _dot_tn(a, b):
    return lax.dot_general(a, b, (((0,), (0,)), ((), ())), preferred_element_type=F32)


class _Carry:
    def __init__(self, inputs, out_shape, aliases, sems, start, finish, middle=None, middle_at=0.85):
        self.inputs, self.out_shape, self.aliases, self.sems = list(inputs), list(out_shape), dict(aliases), list(sems)
        self.start, self.finish = start, finish
        self.middle, self.middle_at = middle, middle_at


def _merge_carries(a, b):
    ia, oa, sa = len(a.inputs), len(a.out_shape), len(a.sems)
    aliases = dict(a.aliases)
    aliases.update({ia + i: oa + j for i, j in b.aliases.items()})

    def both(which):
        def run(ins, outs, sems):
            getattr(a, which)(ins[:ia], outs[:oa], sems[:sa])
            getattr(b, which)(ins[ia:], outs[oa:], sems[sa:])
        return run

    return _Carry(a.inputs + b.inputs, a.out_shape + b.out_shape, aliases, a.sems + b.sems, both("start"), both("finish"))


def _call(body, name, grid, in_specs, out_specs, out_shape, scratch_shapes, semantics, args, carry=None, prefetch=()):
    np_ = len(prefetch)
    if carry is None:
        spec = pltpu.PrefetchScalarGridSpec(num_scalar_prefetch=np_, grid=grid, in_specs=in_specs, out_specs=out_specs,
                                            scratch_shapes=scratch_shapes)
        return pl.pallas_call(body, name=name, grid_spec=spec, out_shape=out_shape,
                              compiler_params=_params(*semantics))(*prefetch, *args)
    ni, no, ns = len(in_specs), len(out_specs), len(scratch_shapes)
    ci, co = len(carry.inputs), len(carry.out_shape)

    def carrying(*refs):
        pre, refs = refs[:np_], refs[np_:]
        ins, refs = refs[:ni], refs[ni:]
        cins, refs = refs[:ci], refs[ci:]
        outs, refs = refs[:no], refs[no:]
        couts, refs = refs[:co], refs[co:]
        scratch, csems = refs[:ns], refs[ns:]
        step = pl.program_id(0)
        for ax in range(1, len(grid)):
            step = step * grid[ax] + pl.program_id(ax)
        steps = math.prod(grid)
        first = step == 0
        last = step == steps - 1

        @pl.when(first)
        def _():
            carry.start(cins, couts, csems)

        if carry.middle is not None:
            @pl.when(step == min(int(carry.middle_at * steps), steps - 1))
            def _():
                carry.middle(cins, couts, csems)

        body(*pre, *ins, *outs, *scratch)

        @pl.when(last)
        def _():
            carry.finish(cins, couts, csems)

    spec = pltpu.PrefetchScalarGridSpec(
        num_scalar_prefetch=np_, grid=grid, in_specs=list(in_specs) + [ANY] * ci,
        out_specs=list(out_specs) + [ANY] * co, scratch_shapes=list(scratch_shapes) + carry.sems)
    return pl.pallas_call(
        carrying, name=name, grid_spec=spec, out_shape=list(out_shape) + carry.out_shape,
        input_output_aliases={np_ + ni + i: no + j for i, j in carry.aliases.items()},
        compiler_params=_params(*(["arbitrary"] * len(grid))),
    )(*prefetch, *args, *carry.inputs)


def _run_carry(carry, name):
    ci, co = len(carry.inputs), len(carry.out_shape)

    def body(*refs):
        cins, couts, csems = refs[:ci], refs[ci:ci + co], refs[ci + co:]
        carry.start(cins, couts, csems)
        if carry.middle is not None:
            carry.middle(cins, couts, csems)
        carry.finish(cins, couts, csems)

    return pl.pallas_call(body, name=name, in_specs=[ANY] * ci, out_specs=[ANY] * co, out_shape=carry.out_shape,
                          input_output_aliases=carry.aliases, scratch_shapes=carry.sems)(*carry.inputs)


def _ffn_fwd(x, gain, wg, wu, wd, name, carry=None):
    T, D = x.shape
    FF = wg.shape[1]
    tm = _tile(T, 512, 16)
    tf = _tile(FF, 512, LANES)
    nf = FF // tf

    def body(x_ref, g_ref, wg_ref, wu_ref, wd_ref, xo_ref, n_ref, G_ref, U_ref, acc_ref):
        f = pl.program_id(1)

        @pl.when(f == 0)
        def _():
            xv = x_ref[...]
            n_ref[...] = (xv * _rstd(xv) * g_ref[...]).astype(BF16)
            acc_ref[...] = jnp.zeros_like(acc_ref)

        n = n_ref[...]
        G = jnp.dot(n, wg_ref[...], preferred_element_type=F32)
        U = jnp.dot(n, wu_ref[...], preferred_element_type=F32)
        G_ref[...] = G.astype(BF16)
        U_ref[...] = U.astype(BF16)
        H = (G * _sigmoid(G) * U).astype(BF16)
        acc_ref[...] += jnp.dot(H, wd_ref[...], preferred_element_type=F32)

        @pl.when(f == nf - 1)
        def _():
            xo_ref[...] = x_ref[...] + FFN_RESIDUAL_SCALE * acc_ref[...]

    return _call(
        body, name, (T // tm, nf),
        [pl.BlockSpec((tm, D), lambda i, f: (i, 0)),
         pl.BlockSpec((1, D), lambda i, f: (0, 0)),
         pl.BlockSpec((D, tf), lambda i, f: (0, f)),
         pl.BlockSpec((D, tf), lambda i, f: (0, f)),
         pl.BlockSpec((tf, D), lambda i, f: (f, 0))],
        [pl.BlockSpec((tm, D), lambda i, f: (i, 0)),
         pl.BlockSpec((tm, D), lambda i, f: (i, 0)),
         pl.BlockSpec((tm, tf), lambda i, f: (i, f)),
         pl.BlockSpec((tm, tf), lambda i, f: (i, f))],
        [jax.ShapeDtypeStruct((T, D), F32), jax.ShapeDtypeStruct((T, D), BF16),
         jax.ShapeDtypeStruct((T, FF), BF16), jax.ShapeDtypeStruct((T, FF), BF16)],
        [pltpu.VMEM((tm, D), F32)], ("parallel", "arbitrary"), (x, gain, wg, wu, wd), carry)


def _ffn_bwd_hidden(db, G, U, wd, name, carry=None):
    T, D = db.shape
    FF = wd.shape[0]
    tm = _tile(T, 1024, 16)
    tf = _tile(FF, 512, LANES)

    def body(d_ref, G_ref, U_ref, wd_ref, dG_ref, dU_ref, H_ref):
        dH = _dot_nt(d_ref[...], wd_ref[...])
        Gv = G_ref[...].astype(F32)
        Uv = U_ref[...].astype(F32)
        s = _sigmoid(Gv)
        sg = Gv * s
        H_ref[...] = (sg * Uv).astype(BF16)
        dU_ref[...] = (dH * sg).astype(BF16)
        dG_ref[...] = (dH * Uv * (s * (1.0 + Gv * (1.0 - s)))).astype(BF16)

    act = pl.BlockSpec((tm, tf), lambda i, f: (i, f))
    return _call(
        body, name, (T // tm, FF // tf),
        [pl.BlockSpec((tm, D), lambda i, f: (i, 0)), act, act, pl.BlockSpec((tf, D), lambda i, f: (f, 0))],
        [act, act, act], [jax.ShapeDtypeStruct((T, FF), BF16)] * 3,
        [], ("parallel", "arbitrary"), (db, G, U, wd), carry)


def _ffn_bwd_input(dG, dU, wg, wu, name, carry=None):
    T, FF = dG.shape
    D = wg.shape[0]
    tm = _tile(T, 512, 16)
    tn = _tile(D, 512, LANES)

    def body(dG_ref, dU_ref, wg_ref, wu_ref, dn_ref):
        dn_ref[...] = _dot_nt(dG_ref[...], wg_ref[...]) + _dot_nt(dU_ref[...], wu_ref[...])

    act = pl.BlockSpec((tm, FF), lambda i, j: (i, 0))
    wt = pl.BlockSpec((tn, FF), lambda i, j: (j, 0))
    return _call(body, name, (T // tm, D // tn), [act, act, wt, wt], [pl.BlockSpec((tm, tn), lambda i, j: (i, j))],
                 [jax.ShapeDtypeStruct((T, D), F32)], [], ("parallel", "arbitrary"), (dG, dU, wg, wu), carry)


TAIL_ROWS = 128


class _NormBwdTail:
    def __init__(self, T, D, tm, scale):
        self.T, self.D, self.tm, self.scale = T, D, tm, scale
        self.ni = T // tm
        self.scratch = [pltpu.VMEM((tm, D), F32), pltpu.VMEM((tm, D), F32), pltpu.VMEM((tm, D), F32),
                        pltpu.VMEM((tm, D), BF16), pltpu.SemaphoreType.DMA((4,))]
        self.out_shape = [jax.ShapeDtypeStruct((T, D), F32), jax.ShapeDtypeStruct((T, D), BF16)]

    def _rows(self, k):
        return pl.ds(pl.multiple_of(k * self.tm, self.tm), self.tm)

    def _loads(self, k, x_hbm, r_hbm, bufs):
        xbuf, rbuf, _, _, sems = bufs
        return [pltpu.make_async_copy(x_hbm.at[self._rows(k)], xbuf, sems.at[0]),
                pltpu.make_async_copy(r_hbm.at[self._rows(k)], rbuf, sems.at[1])]

    def _stores(self, k, dx_hbm, dxb_hbm, bufs):
        _, _, obuf, obbuf, sems = bufs
        return [pltpu.make_async_copy(obuf, dx_hbm.at[self._rows(k)], sems.at[2]),
                pltpu.make_async_copy(obbuf, dxb_hbm.at[self._rows(k)], sems.at[3])]

    def prefetch(self, i, x_hbm, r_hbm, bufs):
        for cp in self._loads(i, x_hbm, r_hbm, bufs):
            cp.start()

    def run(self, i, acc_ref, g_ref, x_hbm, r_hbm, dx_hbm, dxb_hbm, dg_ref, bufs):
        xbuf, rbuf, obuf, obbuf, _ = bufs
        for cp in self._loads(i, x_hbm, r_hbm, bufs):
            cp.wait()

        @pl.when(i > 0)
        def _():
            for cp in self._stores(i - 1, dx_hbm, dxb_hbm, bufs):
                cp.wait()

        dgain = None
        for r0 in range(0, self.tm, TAIL_ROWS):
            rs = slice(r0, min(r0 + TAIL_ROWS, self.tm))
            dv, dgr = _rms_bwd(acc_ref[rs, :], xbuf[rs, :], g_ref[...])
            dx = rbuf[rs, :] + dv
            obuf[rs, :] = dx
            obbuf[rs, :] = (self.scale * dx).astype(BF16)
            dgain = dgr if dgain is None else dgain + dgr
        for cp in self._stores(i, dx_hbm, dxb_hbm, bufs):
            cp.start()

        @pl.when(i == 0)
        def _():
            dg_ref[...] = dgain

        @pl.when(i > 0)
        def _():
            dg_ref[...] += dgain

        @pl.when(i == self.ni - 1)
        def _():
            for cp in self._stores(i, dx_hbm, dxb_hbm, bufs):
                cp.wait()


def _mm_nt_norm_bwd(a, w, x_in, gain, dres, scale, name, carry=None):
    T, K = a.shape
    D = w.shape[0]
    whole = D * K * 2 <= 24 * 2**20
    tm = _tile(T, 256 if whole else 512, 16)
    tk = K if whole else _tile(K, 1280, LANES)
    nk = K // tk
    tail = _NormBwdTail(T, D, tm, scale)
    w_spec = (pl.BlockSpec((D, tk), lambda i, k: (0, k), pipeline_mode=pl.Buffered(1)) if whole
              else pl.BlockSpec((D, tk), lambda i, k: (0, k)))

    def body(a_ref, w_ref, g_ref, x_hbm, r_hbm, dg_ref, dx_hbm, dxb_hbm, acc_ref, *bufs):
        i, k = pl.program_id(0), pl.program_id(1)

        @pl.when(k == 0)
        def _():
            tail.prefetch(i, x_hbm, r_hbm, bufs)

        contrib = _dot_nt(a_ref[...], w_ref[...])

        @pl.when(k == 0)
        def _():
            acc_ref[...] = contrib

        @pl.when(k > 0)
        def _():
            acc_ref[...] += contrib

        @pl.when(k == nk - 1)
        def _():
            tail.run(i, acc_ref, g_ref, x_hbm, r_hbm, dx_hbm, dxb_hbm, dg_ref, bufs)

    return _call(
        body, name, (T // tm, nk),
        [pl.BlockSpec((tm, tk), lambda i, k: (i, k)), w_spec, pl.BlockSpec((1, D), lambda i, k: (0, 0)), ANY, ANY],
        [pl.BlockSpec((1, D), lambda i, k: (0, 0)), ANY, ANY],
        [jax.ShapeDtypeStruct((1, D), F32)] + tail.out_shape,
        [pltpu.VMEM((tm, D), F32)] + tail.scratch, ("arbitrary", "arbitrary"), (a, w, gain, x_in, dres), carry)


def _rms_bwd_res(dn, x, gain, dres, scale, name, carry=None):
    T, D = x.shape
    tm = _tile(T, 256, 16)

    def body(dn_ref, x_ref, g_ref, dr_ref, dx_ref, dxb_ref, dg_ref):
        i = pl.program_id(0)
        dv, dgain = _rms_bwd(dn_ref[...], x_ref[...], g_ref[...])
        dx = dr_ref[...] + dv
        dx_ref[...] = dx
        dxb_ref[...] = (scale * dx).astype(BF16)

        @pl.when(i == 0)
        def _():
            dg_ref[...] = dgain

        @pl.when(i > 0)
        def _():
            dg_ref[...] += dgain

    row = pl.BlockSpec((tm, D), lambda i: (i, 0))
    vec = pl.BlockSpec((1, D), lambda i: (0, 0))
    return _call(
        body, name, (T // tm,), [row, row, vec, row], [row, row, vec],
        [jax.ShapeDtypeStruct((T, D), F32), jax.ShapeDtypeStruct((T, D), BF16), jax.ShapeDtypeStruct((1, D), F32)],
        [], ("arbitrary",), (dn, x, gain, dres), carry)


def _loss_head(x3, gain, target, name):
    T, D = x3.shape
    tm = _tile(T, 256, 16)

    def body(x_ref, g_ref, t_ref, dx_ref, dxb_ref, ls_ref, dg_ref):
        i = pl.program_id(0)
        xv = x_ref[...]
        err = xv * _rstd(xv) * g_ref[...] - t_ref[...]
        sq = jnp.sum(jnp.sum(err * err, axis=1, keepdims=True), axis=0, keepdims=True)
        dv, dgain = _rms_bwd(err * (1.0 / D), xv, g_ref[...])
        dx_ref[...] = dv
        dxb_ref[...] = (FFN_RESIDUAL_SCALE * dv).astype(BF16)
        sqb = jnp.broadcast_to(sq, (1, LANES))

        @pl.when(i == 0)
        def _():
            dg_ref[...] = dgain
            ls_ref[...] = sqb

        @pl.when(i > 0)
        def _():
            dg_ref[...] += dgain
            ls_ref[...] += sqb

    row = pl.BlockSpec((tm, D), lambda i: (i, 0))
    vec = pl.BlockSpec((1, D), lambda i: (0, 0))
    return pl.pallas_call(
        body, name=name, grid=(T // tm,),
        in_specs=[row, vec, row],
        out_specs=[row, row, pl.BlockSpec((1, LANES), lambda i: (0, 0)), vec],
        out_shape=[jax.ShapeDtypeStruct((T, D), F32), jax.ShapeDtypeStruct((T, D), BF16),
                   jax.ShapeDtypeStruct((1, LANES), F32), jax.ShapeDtypeStruct((1, D), F32)],
        compiler_params=_params("arbitrary"),
    )(x3, gain, target)


WHOLE_MIN, WHOLE_MAX = 16 * 2**20, 24 * 2**20


def _fits_whole(w):
    return WHOLE_MIN < w.size * 2 <= WHOLE_MAX


def _weight_spec(block, index_map, whole):
    return pl.BlockSpec(block, index_map, pipeline_mode=pl.Buffered(1)) if whole else pl.BlockSpec(block, index_map)


def _norm_mm(x, gain, w, name, carry=None):
    T, D = x.shape
    N = w.shape[1]
    whole = _fits_whole(w)
    tm = _tile(T, 256 if whole else 512, 16)
    tn = N if whole else _tile(N, 2560, LANES)

    def body(x_ref, g_ref, w_ref, n_ref, z_ref):
        @pl.when(pl.program_id(1) == 0)
        def _():
            xv = x_ref[...]
            n_ref[...] = (xv * _rstd(xv) * g_ref[...]).astype(BF16)

        z_ref[...] = jnp.dot(n_ref[...], w_ref[...], preferred_element_type=F32).astype(BF16)

    return _call(
        body, name, (T // tm, N // tn),
        [pl.BlockSpec((tm, D), lambda i, j: (i, 0)),
         pl.BlockSpec((1, D), lambda i, j: (0, 0)),
         _weight_spec((D, tn), lambda i, j: (0, j), whole)],
        [pl.BlockSpec((tm, D), lambda i, j: (i, 0)),
         pl.BlockSpec((tm, tn), lambda i, j: (i, j))],
        [jax.ShapeDtypeStruct((T, D), BF16), jax.ShapeDtypeStruct((T, N), BF16)],
        [], ("parallel", "arbitrary"), (x, gain, w), carry)


def _ffn_up(n, G, wu, name, carry=None):
    T, D = n.shape
    FF = wu.shape[1]
    whole = _fits_whole(wu)
    tm = _tile(T, 128 if whole else 512, 16)
    tf = FF if whole else _tile(FF, 1408, LANES)

    def body(n_ref, G_ref, wu_ref, U_ref, H_ref):
        U = jnp.dot(n_ref[...], wu_ref[...], preferred_element_type=F32)
        Gv = G_ref[...].astype(F32)
        U_ref[...] = U.astype(BF16)
        H_ref[...] = (Gv * _sigmoid(Gv) * U).astype(BF16)

    act = pl.BlockSpec((tm, tf), lambda i, f: (i, f))
    return _call(
        body, name, (T // tm, FF // tf),
        [pl.BlockSpec((tm, D), lambda i, f: (i, 0)), act, _weight_spec((D, tf), lambda i, f: (0, f), whole)],
        [act, act], [jax.ShapeDtypeStruct((T, FF), BF16)] * 2, [], ("parallel", "arbitrary"), (n, G, wu), carry)


def _mm_fullk(a, w, trans_w, residual, out_dtype, name, carry=None, scale=1.0):
    T, K = a.shape
    N = w.shape[0] if trans_w else w.shape[1]
    whole = _fits_whole(w)
    tm = _tile(T, 256 if whole else 512, 16)
    tn = N if whole else _tile(N, 2048 * 2560 // K, LANES)

    def body(*refs):
        if residual is None:
            a_ref, w_ref, o_ref = refs
        else:
            a_ref, w_ref, r_ref, o_ref = refs
        if trans_w:
            acc = _dot_nt(a_ref[...], w_ref[...])
        else:
            acc = jnp.dot(a_ref[...], w_ref[...], preferred_element_type=F32)
        if scale != 1.0:
            acc = scale * acc
        if residual is not None:
            acc = acc + r_ref[...]
        o_ref[...] = acc.astype(out_dtype)

    w_spec = (_weight_spec((tn, K), lambda i, j: (j, 0), whole) if trans_w
              else _weight_spec((K, tn), lambda i, j: (0, j), whole))
    in_specs = [pl.BlockSpec((tm, K), lambda i, j: (i, 0)), w_spec]
    args = [a, w]
    if residual is not None:
        in_specs.append(pl.BlockSpec((tm, tn), lambda i, j: (i, j)))
        args.append(residual)
    return _call(body, name, (T // tm, N // tn), in_specs, [pl.BlockSpec((tm, tn), lambda i, j: (i, j))],
                 [jax.ShapeDtypeStruct((T, N), out_dtype)], [], ("parallel", "arbitrary"), args, carry)


def _mm_tn_pair(a, b, c_arr, axis, name, carry=None):
    T, M = a.shape
    N = b.shape[1]
    tk = _tile(T, 2048, 16)
    nk = T // tk
    nq = 4
    if axis == 1:
        rows, cols = M // 2, N // nq
        a_spec = pl.BlockSpec((tk, rows), lambda p, j, k, cr: (k, jnp.where(p == 0, 1 - cr[0], cr[0])))
        b_spec = pl.BlockSpec((tk, cols), lambda p, j, k, cr: (k, j))
        s_shape, land_shape = (1, rows, N), (rows, N)
        s_spec = pl.BlockSpec((None, rows, cols), lambda p, j, k, cr: (0, 0, j * p))
    else:
        rows, cols = M // nq, N // 2
        a_spec = pl.BlockSpec((tk, rows), lambda p, j, k, cr: (k, j))
        b_spec = pl.BlockSpec((tk, cols), lambda p, j, k, cr: (k, jnp.where(p == 0, 1 - cr[0], cr[0])))
        s_shape, land_shape = (nq, rows, cols), (nq, rows, cols)
        s_spec = pl.BlockSpec((None, rows, cols), lambda p, j, k, cr: (j * p, 0, 0))

    def body(c_ref, a_ref, b_ref, s_ref, land, acc_ref, stage, got, send_sems, recv_sems, loc_sem):
        p, j, k = pl.program_id(0), pl.program_id(1), pl.program_id(2)
        x, y, c = _me()

        def tile(jj):
            return land.at[:, pl.ds(jj * cols, cols)] if axis == 1 else land.at[jj]

        def send(jj):
            return _remote(stage, tile(jj), send_sems, recv_sems, jj, (x, y, 1 - c))

        @pl.when(k == 0)
        def _():
            acc_ref[...] = jnp.zeros_like(acc_ref)

        acc_ref[...] += _dot_tn(a_ref[...], b_ref[...])

        def fetch(jj):
            return pltpu.make_async_copy(tile(jj), got, loc_sem.at[0])

        for jj in range(nq):
            @pl.when(jnp.logical_and(k == nk - 1, jnp.logical_and(p == 0, j == jj)))
            def _():
                if jj > 0:
                    send(jj - 1).wait_send()
                stage[...] = acc_ref[...].astype(BF16)
                send(jj).start()

            @pl.when(jnp.logical_and(k == max(nk - 2, 0), jnp.logical_and(p == 1, j == jj)))
            def _():
                if jj == 0:
                    send(nq - 1).wait_send()
                send(jj).wait_recv()
                fetch(jj).start()

            @pl.when(jnp.logical_and(k == nk - 1, jnp.logical_and(p == 1, j == jj)))
            def _():
                fetch(jj).wait()
                s_ref[...] = (acc_ref[...] + got[...].astype(F32)).astype(BF16)

    return _call(
        body, name, (2, nq, nk), [a_spec, b_spec], [s_spec, ANY],
        [jax.ShapeDtypeStruct(s_shape, BF16), jax.ShapeDtypeStruct(land_shape, BF16)],
        [pltpu.VMEM((rows, cols), F32), pltpu.VMEM((rows, cols), BF16), pltpu.VMEM((rows, cols), BF16),
         pltpu.SemaphoreType.DMA((nq,)), pltpu.SemaphoreType.DMA((nq,)), pltpu.SemaphoreType.DMA((1,))],
        ("arbitrary", "arbitrary", "arbitrary"), (a, b), carry, (c_arr,))


GELU_K = math.sqrt(2.0 / math.pi)
GELU_C = 0.044715


def _gelu_and_grad(v):
    u = GELU_K * (v + GELU_C * v * v * v)
    th = jnp.tanh(u)
    g = 0.5 * v * (1.0 + th)
    dg = 0.5 * (1.0 + th) + 0.5 * v * (1.0 - th * th) * GELU_K * (1.0 + 3.0 * GELU_C * v * v)
    return g, dg


def _neg_expm1(v):
    poly = v * (1.0 + v * (0.5 + v * (1.0 / 6 + v * (1.0 / 24 + v * (1.0 / 120 + v * (1.0 / 720))))))
    return jnp.where(v > -0.25, -poly, 1.0 - jnp.exp(v))


def _softplus_neg(lam):
    e = jnp.exp(-jnp.abs(lam))
    log1pe = jnp.where(e < 1e-4, e * (1.0 - 0.5 * e), jnp.log(1.0 + e))
    sp = jnp.maximum(-lam, 0.0) + log1pe
    dsp = -1.0 / (1.0 + jnp.exp(lam))
    return sp, dsp


def _earlier(ext, j):
    return pltpu.roll(ext, j, 0)[SUBLANES:, :]


def _later(ext, j):
    n = ext.shape[0]
    return pltpu.roll(ext, n - j, 0)[:n - SUBLANES, :]


def _taps(v, halo, K):
    ext = jnp.concatenate([halo, v], axis=0)
    return [v] + [_earlier(ext, j) for j in range(1, K)]


def _block_diag(vb, w_ref, nh, hd):
    return jnp.concatenate(
        [jnp.dot(vb[:, h * hd:(h + 1) * hd], w_ref[h], preferred_element_type=F32) for h in range(nh)], axis=1)


def _lru_gates(xc, wa_ref, ba_ref, wi_ref, bi_ref, sp, nh, hd):
    xcb = xc.astype(BF16)
    r = _sigmoid(_block_diag(xcb, wa_ref, nh, hd) + ba_ref[...])
    ig = _sigmoid(_block_diag(xcb, wi_ref, nh, hd) + bi_ref[...])
    log_a = -LRU_C * r * sp
    a = jnp.exp(log_a)
    mult = jnp.sqrt(_neg_expm1(2.0 * log_a))
    return xcb, r, ig, a, mult


def _mix_fwd(z, cw, cb, wa, ba, wi, bi, lam, sw, glo, gso, name, carry=None):
    T = z.shape[0]
    DL = cb.shape[1]
    DS = gso.shape[1]
    NH, HD = wa.shape[0], wa.shape[1]
    KL, KS = cw.shape[0], sw.shape[0]
    tt = _tile(T, 128, 16)
    o_g, o_b, o_c, o_x = DL, 2 * DL, 2 * DL + DS, 2 * DL + 2 * DS

    def body(z_ref, cw_ref, cb_ref, wa_ref, ba_ref, wi_ref, bi_ref, lam_ref, sw_ref, glo_ref, gso_ref,
             h_ref, y_ref, cx_ref, cp_ref, ch_ref):
        @pl.when(pl.program_id(0) == 0)
        def _():
            cx_ref[...] = jnp.zeros_like(cx_ref)
            cp_ref[...] = jnp.zeros_like(cp_ref)
            ch_ref[...] = jnp.zeros_like(ch_ref)

        def zcol(o, n):
            return z_ref[:, o:o + n].astype(F32)

        lx = zcol(0, DL)
        xs = _taps(lx, cx_ref[...], KL)
        cx_ref[...] = lx[tt - SUBLANES:, :]
        xc = cb_ref[...] + xs[0] * cw_ref[KL - 1:KL, :]
        for j in range(1, KL):
            xc = xc + xs[j] * cw_ref[KL - 1 - j:KL - j, :]
        sp, _ = _softplus_neg(lam_ref[...])
        _, _, ig, a, mult = _lru_gates(xc, wa_ref, ba_ref, wi_ref, bi_ref, sp, NH, HD)
        b = mult * (ig * xc)
        rows = lax.broadcasted_iota(jnp.int32, (tt, DL), 0)
        s = 1
        while s < tt:
            keep = rows >= s
            b = jnp.where(keep, a * pltpu.roll(b, s, 0) + b, b)
            a = jnp.where(keep, a * pltpu.roll(a, s, 0), a)
            s *= 2
        h = a * ch_ref[SUBLANES - 1:SUBLANES, :] + b
        ch_ref[...] = h[tt - SUBLANES:, :]
        h_ref[...] = h
        ge, _ = _gelu_and_grad(zcol(o_g, DL))
        ylru = h * ge
        y_ref[:, 0:DL] = (ylru * _rstd(ylru) * glo_ref[...]).astype(BF16)

        p = zcol(o_c, DS) * zcol(o_x, DS)
        ps = _taps(p, cp_ref[...], KS)
        cp_ref[...] = p[tt - SUBLANES:, :]
        cv = ps[0] * sw_ref[KS - 1:KS, :]
        for j in range(1, KS):
            cv = cv + ps[j] * sw_ref[KS - 1 - j:KS - j, :]
        ysc = zcol(o_b, DS) * cv
        y_ref[:, DL:DL + DS] = (ysc * _rstd(ysc) * gso_ref[...]).astype(BF16)

    def full(shape):
        return pl.BlockSpec(shape, lambda t: (0,) * len(shape))

    return _call(
        body, name, (T // tt,),
        [pl.BlockSpec((tt, z.shape[1]), lambda t: (t, 0)),
         full(cw.shape), full(cb.shape), full(wa.shape), full(ba.shape), full(wi.shape), full(bi.shape),
         full(lam.shape), full(sw.shape), full(glo.shape), full(gso.shape)],
        [pl.BlockSpec((tt, DL), lambda t: (t, 0)), pl.BlockSpec((tt, DL + DS), lambda t: (t, 0))],
        [jax.ShapeDtypeStruct((T, DL), F32), jax.ShapeDtypeStruct((T, DL + DS), BF16)],
        [pltpu.VMEM((SUBLANES, DL), F32), pltpu.VMEM((SUBLANES, DS), F32), pltpu.VMEM((SUBLANES, DL), F32)],
        ("arbitrary",), (z, cw, cb, wa, ba, wi, bi, lam, sw, glo, gso), carry)


V_BA, V_BI, V_LAM, V_CB, V_CW, V_SW, V_GLO, V_GSO, V_ROWS = 0, 1, 2, 3, 4, 8, 11, 12, 16


def _mix_bwd(z, h, dy, cw, cb, wa, ba, wi, bi, lam, sw, glo, gso, name):
    T = z.shape[0]
    DL = cb.shape[1]
    DS = gso.shape[1]
    NH, HD = wa.shape[0], wa.shape[1]
    KL, KS = cw.shape[0], sw.shape[0]
    tt = _tile(T, 64, 16)
    nt = T // tt
    ZH = 2 * SUBLANES
    o_g, o_b, o_c, o_x = DL, 2 * DL, 2 * DL + DS, 2 * DL + 2 * DS

    def body(z_ref, zh_ref, h_ref, hh_ref, dy_ref, cw_ref, cb_ref, wa_ref, ba_ref, wi_ref, bi_ref, lam_ref,
             sw_ref, glo_ref, gso_ref, dz_ref, dwa_ref, dwi_ref, vec_ref, cdx_ref, cdc_ref, cdh_ref):
        i = pl.program_id(0)
        tr = nt - 1 - i

        @pl.when(i == 0)
        def _():
            dwa_ref[...] = jnp.zeros_like(dwa_ref)
            dwi_ref[...] = jnp.zeros_like(dwi_ref)
            vec_ref[...] = jnp.zeros_like(vec_ref)
            cdx_ref[...] = jnp.zeros_like(cdx_ref)
            cdc_ref[...] = jnp.zeros_like(cdc_ref)
            cdh_ref[...] = jnp.zeros_like(cdh_ref)

        def acc_row(r, v):
            vec_ref[pl.ds(r, 1), :] += jnp.sum(v, axis=0, keepdims=True)

        has_prev = tr > 0
        rows = lax.broadcasted_iota(jnp.int32, (tt, DL), 0)

        def zcol(o, n):
            return z_ref[:, o:o + n].astype(F32)

        def zhalo(o, n):
            return jnp.where(has_prev, zh_ref[:, o:o + n].astype(F32)[SUBLANES:, :], 0.0)

        lx = zcol(0, DL)
        xs = _taps(lx, zhalo(0, DL), KL)
        xc = cb_ref[...] + xs[0] * cw_ref[KL - 1:KL, :]
        for j in range(1, KL):
            xc = xc + xs[j] * cw_ref[KL - 1 - j:KL - j, :]
        sp, dsp = _softplus_neg(lam_ref[...])
        xcb, r, ig, a, mult = _lru_gates(xc, wa_ref, ba_ref, wi_ref, bi_ref, sp, NH, HD)
        hv = h_ref[...]
        hprev = _earlier(jnp.concatenate([jnp.where(has_prev, hh_ref[...], 0.0), hv], axis=0), 1)
        gate = zcol(o_g, DL)
        ge, dge = _gelu_and_grad(gate)
        ylru = hv * ge

        d_ylru, dglo = _rms_bwd(dy_ref[:, 0:DL].astype(F32), ylru, glo_ref[...])
        vec_ref[pl.ds(V_GLO, 1), :] += dglo
        dz_ref[:, o_g:o_g + DL] = (d_ylru * hv * dge).astype(BF16)
        bq = d_ylru * ge
        aq = jnp.where(rows == tt - 1, 1.0, pltpu.roll(a, tt - 1, 0))
        s = 1
        while s < tt:
            keep = rows < tt - s
            bq = jnp.where(keep, aq * pltpu.roll(bq, tt - s, 0) + bq, bq)
            aq = jnp.where(keep, aq * pltpu.roll(aq, tt - s, 0), aq)
            s *= 2
        dhh = bq + aq * cdh_ref[0:1, :]
        cdh_ref[0:1, :] = a[0:1, :] * dhh[0:1, :]

        da = dhh * hprev
        dmult = dhh * (ig * xc)
        d_i = dhh * mult * xc
        dxc = dhh * mult * ig
        dlog = da * a - dmult * (a * a) / mult
        acc_row(V_LAM, dlog * (-LRU_C * r) * dsp)
        dpa = dlog * (-LRU_C * sp) * r * (1.0 - r)
        dpi = d_i * ig * (1.0 - ig)
        acc_row(V_BA, dpa)
        acc_row(V_BI, dpi)
        dpab = dpa.astype(BF16)
        dpib = dpi.astype(BF16)
        back = []
        for hh in range(NH):
            sl = slice(hh * HD, (hh + 1) * HD)
            dwa_ref[hh] += _dot_tn(xcb[:, sl], dpab[:, sl])
            dwi_ref[hh] += _dot_tn(xcb[:, sl], dpib[:, sl])
            back.append(_dot_nt(dpab[:, sl], wa_ref[hh]) + _dot_nt(dpib[:, sl], wi_ref[hh]))
        dxc = dxc + jnp.concatenate(back, axis=1)

        acc_row(V_CB, dxc)
        extd = jnp.concatenate([dxc, cdx_ref[...]], axis=0)
        cdx_ref[...] = dxc[0:SUBLANES, :]
        dlx = dxc * cw_ref[KL - 1:KL, :]
        acc_row(V_CW + KL - 1, dxc * xs[0])
        for j in range(1, KL):
            dlx = dlx + _later(extd, j) * cw_ref[KL - 1 - j:KL - j, :]
            acc_row(V_CW + KL - 1 - j, dxc * xs[j])
        dz_ref[:, 0:DL] = dlx.astype(BF16)

        sb = zcol(o_b, DS)
        sc = zcol(o_c, DS)
        sx = zcol(o_x, DS)
        p = sc * sx
        ps = _taps(p, zhalo(o_c, DS) * zhalo(o_x, DS), KS)
        cv = ps[0] * sw_ref[KS - 1:KS, :]
        for j in range(1, KS):
            cv = cv + ps[j] * sw_ref[KS - 1 - j:KS - j, :]
        d_ysc, dgso = _rms_bwd(dy_ref[:, DL:DL + DS].astype(F32), sb * cv, gso_ref[...])
        vec_ref[pl.ds(V_GSO, 1), :] += dgso
        dz_ref[:, o_b:o_b + DS] = (d_ysc * cv).astype(BF16)
        dcv = d_ysc * sb
        extc = jnp.concatenate([dcv, cdc_ref[...]], axis=0)
        cdc_ref[...] = dcv[0:SUBLANES, :]
        dp = dcv * sw_ref[KS - 1:KS, :]
        acc_row(V_SW + KS - 1, dcv * ps[0])
        for j in range(1, KS):
            dp = dp + _later(extc, j) * sw_ref[KS - 1 - j:KS - j, :]
            acc_row(V_SW + KS - 1 - j, dcv * ps[j])
        dz_ref[:, o_c:o_c + DS] = (dp * sx).astype(BF16)
        dz_ref[:, o_x:o_x + DS] = (dp * sc).astype(BF16)

    def full(shape):
        return pl.BlockSpec(shape, lambda t: (0,) * len(shape))

    def rev(t):
        return nt - 1 - t

    def halo(t, rows):
        return jnp.maximum(rev(t) * (tt // rows) - 1, 0)

    return pl.pallas_call(
        body, name=name, grid=(nt,),
        in_specs=[pl.BlockSpec((tt, z.shape[1]), lambda t: (rev(t), 0)),
                  pl.BlockSpec((ZH, z.shape[1]), lambda t: (halo(t, ZH), 0)),
                  pl.BlockSpec((tt, DL), lambda t: (rev(t), 0)),
                  pl.BlockSpec((SUBLANES, DL), lambda t: (halo(t, SUBLANES), 0)),
                  pl.BlockSpec((tt, DL + DS), lambda t: (rev(t), 0)),
                  full(cw.shape), full(cb.shape), full(wa.shape), full(ba.shape), full(wi.shape), full(bi.shape),
                  full(lam.shape), full(sw.shape), full(glo.shape), full(gso.shape)],
        out_specs=[pl.BlockSpec((tt, z.shape[1]), lambda t: (rev(t), 0)),
                   full(wa.shape), full(wi.shape), full((V_ROWS, DL))],
        out_shape=[jax.ShapeDtypeStruct(z.shape, BF16), jax.ShapeDtypeStruct(wa.shape, F32),
                   jax.ShapeDtypeStruct(wi.shape, F32), jax.ShapeDtypeStruct((V_ROWS, DL), F32)],
        scratch_shapes=[pltpu.VMEM((SUBLANES, DL), F32), pltpu.VMEM((SUBLANES, DS), F32),
                        pltpu.VMEM((SUBLANES, DL), F32)],
        compiler_params=_params("arbitrary"),
    )(z, z, h, h, dy, cw, cb, wa, ba, wi, bi, lam, sw, glo, gso)


def _quad_add(s, r2, qc, axis, name):
    _, R, W = r2.shape
    tr = _tile(R, 256, 16)

    def body(qc_ref, s_ref, r0_ref, r1_ref, r2_ref, o_ref):
        o_ref[...] = ((s_ref[...].astype(F32) + r0_ref[...].astype(F32)) + r1_ref[...].astype(F32)) + r2_ref[...].astype(F32)

    blk = (None, tr, W)
    if axis == 1:
        own = pl.BlockSpec(blk, lambda i, qr: (0, i, qr[0]))
    else:
        own = pl.BlockSpec(blk, lambda i, qr: (qr[0], i, 0))
    return pl.pallas_call(
        body, name=name,
        grid_spec=pltpu.PrefetchScalarGridSpec(
            num_scalar_prefetch=1, grid=(R // tr,),
            in_specs=[own] + [pl.BlockSpec(blk, lambda i, qr, j=j: (j, i, 0)) for j in range(3)],
            out_specs=pl.BlockSpec(blk, lambda i, qr: (qr[1], i, 0))),
        out_shape=jax.ShapeDtypeStruct((2, R, W), F32),
        compiler_params=_params("parallel"),
    )(qc, s, r2, r2, r2)


CAST_BLOCKS = 4


def _cast_into_full(shards, qc, axes, name, carry=None):
    M = len(shards)
    nb = CAST_BLOCKS

    def body(qc_ref, *refs):
        for s_ref, o_ref in zip(refs[:M], refs[M:]):
            o_ref[...] = s_ref[...].astype(BF16)

    in_specs, out_specs, out_shape = [], [], []
    for s, ax in zip(shards, axes):
        R, W = s.shape
        Rh = R // 2
        tr = Rh // nb
        in_specs.append(pl.BlockSpec((tr, W), lambda hf, i, qr: (hf * nb + i, 0)))
        if ax == 1:
            out_shape.append(jax.ShapeDtypeStruct((2, Rh, 4 * W), BF16))
            out_specs.append(pl.BlockSpec((None, tr, W), lambda hf, i, qr: (hf, i, qr[0])))
        else:
            out_shape.append(jax.ShapeDtypeStruct((8, Rh, W), BF16))
            out_specs.append(pl.BlockSpec((None, tr, W), lambda hf, i, qr: (2 * qr[0] + hf, i, 0)))
    return _call(body, name, (2, nb), in_specs, out_specs, out_shape, [], ("parallel", "parallel"), shards, carry, (qc,))


def _adamw(w, g, m, v, name):
    R, C = w.shape
    tr = _tile(R, 256, SUBLANES)
    tc = C // 2 if g.ndim == 3 else _tile(C, 2048, LANES)
    c1 = 1.0 - ADAM_B1 ** ADAM_STEP
    c2 = 1.0 - ADAM_B2 ** ADAM_STEP

    def body(w_ref, g_ref, m_ref, v_ref, d_ref, mo_ref, vo_ref, go_ref):
        gv = g_ref[...]
        go_ref[...] = gv
        mn = ADAM_B1 * m_ref[...] + (1.0 - ADAM_B1) * gv
        vn = ADAM_B2 * v_ref[...] + (1.0 - ADAM_B2) * (gv * gv)
        mo_ref[...] = mn
        vo_ref[...] = vn
        d_ref[...] = -ADAM_LR * ((mn / c1) / (jnp.sqrt(vn / c2) + ADAM_EPS) + ADAM_WD * w_ref[...])

    blk = pl.BlockSpec((tr, tc), lambda i, j: (i, j))
    g_blk = pl.BlockSpec((None, tr, tc), lambda i, j: (j, i, 0)) if g.ndim == 3 else blk
    sh = jax.ShapeDtypeStruct((R, C), F32)
    return pl.pallas_call(
        body, name=name, grid=(R // tr, C // tc),
        in_specs=[blk, g_blk, blk, blk], out_specs=[blk] * 4, out_shape=[sh] * 4,
        compiler_params=_params("parallel", "parallel"),
    )(w, g, m, v)


def _other_chips(x, y):
    return [(1 - x, y), (x, 1 - y), (1 - x, 1 - y)]


def _remote(src, dst, send_sems, recv_sems, idx, dev):
    return pltpu.make_async_remote_copy(src_ref=src, dst_ref=dst, send_sem=send_sems.at[idx], recv_sem=recv_sems.at[idx],
                                        device_id=dev, device_id_type=MESH)


def _gather_carry(fulls, axes, pieces=None):
    M = len(fulls)

    def win(outs, m, qq, cc):
        Rh = fulls[m].shape[1]
        k, n = pieces[m] if pieces is not None and pieces[m] is not None else (0, 1)
        rows = pl.ds(k * (Rh // n), Rh // n)
        if axes[m] == 1:
            W = fulls[m].shape[2] // 4
            return outs[m].at[cc, rows, pl.ds(pl.multiple_of(qq * W, LANES), W)]
        return outs[m].at[2 * qq + cc, rows, :]

    def ici(outs, sems, m, j, src_q):
        x, y, c = _me()
        cx, cy = _other_chips(x, y)[j]
        blk = win(outs, m, src_q, c)
        return _remote(blk, blk, sems[0], sems[1], 6 * m + j, (cx, cy, c))

    def d2d(outs, sems, m, j, half):
        x, y, c = _me()
        cx, cy = _other_chips(x, y)[j]
        blk = win(outs, m, 2 * cx + cy, half)
        return _remote(blk, blk, sems[0], sems[1], 6 * m + 3 + j, (x, y, 1 - c))

    def start(ins, outs, sems):
        x, y, c = _me()
        for m in range(M):
            for j in range(3):
                ici(outs, sems, m, j, 2 * x + y).start()

    def middle(ins, outs, sems):
        x, y, c = _me()
        for m in range(M):
            for j, (cx, cy) in enumerate(_other_chips(x, y)):
                ici(outs, sems, m, j, 2 * cx + cy).wait_recv()
                d2d(outs, sems, m, j, c).start()

    def finish(ins, outs, sems):
        x, y, c = _me()
        for m in range(M):
            for j in range(3):
                d2d(outs, sems, m, j, 1 - c).wait_recv()
        for m in range(M):
            for j in range(3):
                ici(outs, sems, m, j, 2 * x + y).wait_send()
                d2d(outs, sems, m, j, c).wait_send()

    return _Carry(fulls, [jax.ShapeDtypeStruct(f.shape, f.dtype) for f in fulls], {m: m for m in range(M)},
                  [pltpu.SemaphoreType.DMA((6 * M,)), pltpu.SemaphoreType.DMA((6 * M,))], start, finish, middle)


def _gather_two_way_carry(fulls, axes):
    M = len(fulls)

    def part(outs, m, qq, cc, p):
        Rp = fulls[m].shape[1] // 2
        rows = pl.ds(p * Rp, Rp)
        if axes[m] == 1:
            W = fulls[m].shape[2] // 4
            return outs[m].at[cc, rows, pl.ds(pl.multiple_of(qq * W, LANES), W)]
        return outs[m].at[2 * qq + cc, rows, :]

    def copies(outs, sems):
        x, y, c = _me()
        q, qx, qy, qd = 2 * x + y, 2 * (1 - x) + y, 2 * x + (1 - y), 2 * (1 - x) + (1 - y)
        xn, yn, sib = (1 - x, y, c), (x, 1 - y, c), (x, y, 1 - c)
        table = {}
        for m in range(M):
            def cp(blk, k, dev):
                return _remote(blk, blk, sems[0], sems[1], 12 * m + k, dev)
            own0, own1 = part(outs, m, q, c, 0), part(outs, m, q, c, 1)
            table[m] = dict(
                to=[cp(own0, 0, xn), cp(own1, 1, yn), cp(own1, 2, xn), cp(own0, 3, yn)],
                landed=[cp(part(outs, m, qx, c, 0), 0, xn), cp(part(outs, m, qy, c, 1), 1, yn),
                        cp(part(outs, m, qx, c, 1), 2, xn), cp(part(outs, m, qy, c, 0), 3, yn),
                        cp(part(outs, m, qd, c, 0), 4, yn), cp(part(outs, m, qd, c, 1), 5, xn)],
                passed=[cp(part(outs, m, qx, c, 0), 4, yn), cp(part(outs, m, qy, c, 1), 5, xn)],
                handed=[cp(part(outs, m, qq, c, p), 6 + k, sib)
                        for k, (qq, p) in enumerate([(qx, 0), (qy, 1), (qx, 1), (qy, 0), (qd, 0), (qd, 1)])],
                taken=[cp(part(outs, m, qq, 1 - c, p), 6 + k, sib)
                       for k, (qq, p) in enumerate([(qx, 0), (qy, 1), (qx, 1), (qy, 0), (qd, 0), (qd, 1)])])
        return table

    def start(ins, outs, sems):
        t = copies(outs, sems)
        for m in range(M):
            for cp in t[m]['to']:
                cp.start()

    def finish(ins, outs, sems):
        t = copies(outs, sems)
        for m in range(M):
            for k in range(4):
                t[m]['landed'][k].wait_recv()
                if k < 2:
                    t[m]['passed'][k].start()
                t[m]['handed'][k].start()
        for m in range(M):
            for k in (4, 5):
                t[m]['landed'][k].wait_recv()
                t[m]['handed'][k].start()
        for m in range(M):
            for cp in t[m]['taken']:
                cp.wait_recv()
            for cp in t[m]['to'] + t[m]['passed'] + t[m]['handed']:
                cp.wait_send()

    return _Carry(fulls, [jax.ShapeDtypeStruct(f.shape, f.dtype) for f in fulls], {m: m for m in range(M)},
                  [pltpu.SemaphoreType.DMA((12 * M,)), pltpu.SemaphoreType.DMA((12 * M,))], start, finish)


def _chip_exchange_carry(sums, axes):
    M = len(sums)
    out_shape = []
    for s, ax in zip(sums, axes):
        _, Rh, C = s.shape
        out_shape.append(jax.ShapeDtypeStruct((3, Rh, C // 4 if ax == 1 else C), s.dtype))

    def copies(ins, outs, sems):
        x, y, c = _me()
        cps = []
        for m in range(M):
            for j, (cx, cy) in enumerate(_other_chips(x, y)):
                qj = 2 * cx + cy
                if axes[m] == 1:
                    W = sums[m].shape[2] // 4
                    src = ins[m].at[0, :, pl.ds(pl.multiple_of(qj * W, LANES), W)]
                else:
                    src = ins[m].at[qj]
                cps.append(_remote(src, outs[m].at[j], sems[0], sems[1], 3 * m + j, (cx, cy, c)))
        return cps

    def start(ins, outs, sems):
        for cp in copies(ins, outs, sems):
            cp.start()

    def finish(ins, outs, sems):
        for cp in copies(ins, outs, sems):
            cp.wait()

    return _Carry(sums, out_shape, {}, [pltpu.SemaphoreType.DMA((3 * M,)), pltpu.SemaphoreType.DMA((3 * M,))],
                  start, finish)


def _pair_share_carry(bufs):
    M = len(bufs)

    def start(ins, outs, sems):
        x, y, c = _me()
        for m in range(M):
            _remote(outs[m].at[c], outs[m].at[c], sems[0], sems[1], m, (x, y, 1 - c)).start()

    def finish(ins, outs, sems):
        x, y, c = _me()
        for m in range(M):
            _remote(outs[m].at[c], outs[m].at[c], sems[0], sems[1], m, (x, y, 1 - c)).wait_send()
            _remote(outs[m].at[1 - c], outs[m].at[1 - c], sems[0], sems[1], m, (x, y, 1 - c)).wait_recv()

    return _Carry(bufs, [jax.ShapeDtypeStruct(b.shape, b.dtype) for b in bufs], {m: m for m in range(M)},
                  [pltpu.SemaphoreType.DMA((M,)), pltpu.SemaphoreType.DMA((M,))], start, finish)


def _allreduce_small(v, name):
    R, W = v.shape
    Rh = R // 2

    def body(v_ref, o_ref, sib, quad, send_sems, recv_sems):
        x, y, c = _me()
        q = 2 * x + y
        sibling = (x, y, 1 - c)
        pair = _remote(v_ref, sib, send_sems, recv_sems, 0, sibling)
        pair.start()
        pair.wait()
        mine = pl.ds(pl.multiple_of(c * Rh, SUBLANES), Rh)
        quad[0] = v_ref[mine, :] + sib[mine, :]
        cps = []
        for k in (1, 2, 3):
            peer = (1 - x if k & 2 else x, 1 - y if k & 1 else y, c)
            cps.append(_remote(quad.at[0], quad.at[k], send_sems, recv_sems, k, peer))
            cps[-1].start()
        for cp in cps:
            cp.wait()
        acc = quad[q]
        for p in (1, 2, 3):
            acc = acc + quad[jnp.bitwise_xor(q, p)]
        o_ref[mine, :] = acc
        theirs = pl.ds(pl.multiple_of((1 - c) * Rh, SUBLANES), Rh)
        done = _remote(o_ref.at[mine, :], o_ref.at[mine, :], send_sems, recv_sems, 4, sibling)
        done.start()
        done.wait_send()
        _remote(o_ref.at[theirs, :], o_ref.at[theirs, :], send_sems, recv_sems, 4, sibling).wait_recv()

    vm = pl.BlockSpec(memory_space=pltpu.VMEM)
    return pl.pallas_call(
        body, name=name, in_specs=[vm], out_specs=vm, out_shape=jax.ShapeDtypeStruct((R, W), F32),
        scratch_shapes=[pltpu.VMEM((R, W), F32), pltpu.VMEM((4, Rh, W), F32),
                        pltpu.SemaphoreType.DMA((5,)), pltpu.SemaphoreType.DMA((5,))],
        compiler_params=pltpu.CompilerParams(vmem_limit_bytes=VMEM_LIMIT),
    )(v)


def _pack(pieces):
    flat = []
    for p in pieces:
        p = p.reshape(-1).astype(F32)
        pad = (-p.shape[0]) % PACK_ALIGN
        flat.append(jnp.pad(p, (0, pad)).reshape(-1, PACK_W))
    if sum(f.shape[0] for f in flat) % (2 * SUBLANES):
        flat.append(jnp.zeros((SUBLANES, PACK_W), F32))
    return jnp.concatenate(flat, axis=0)


def _unpack(packed, shapes):
    out, row = [], 0
    for shp in shapes:
        n = math.prod(shp)
        rows = -(-n // PACK_ALIGN) * SUBLANES
        out.append(packed[row:row + rows].reshape(-1)[:n].reshape(shp))
        row += rows
    return out


def kernel(x, ffn1_norm, ffn1_w_gate, ffn1_w_up, ffn1_w_down, mix_norm, w_in, lru_conv_w, lru_conv_b, lru_w_a, lru_b_a, lru_w_i, lru_b_i, lru_lambda, sc_conv_w, lru_out_norm, sc_out_norm, w_out, ffn2_norm, ffn2_w_gate, ffn2_w_up, ffn2_w_down, final_norm, loss_target, m_ffn1_norm, m_ffn1_w_gate, m_ffn1_w_up, m_ffn1_w_down, m_mix_norm, m_w_in, m_lru_conv_w, m_lru_conv_b, m_lru_w_a, m_lru_b_a, m_lru_w_i, m_lru_b_i, m_lru_lambda, m_sc_conv_w, m_lru_out_norm, m_sc_out_norm, m_w_out, m_ffn2_norm, m_ffn2_w_gate, m_ffn2_w_up, m_ffn2_w_down, m_final_norm, v_ffn1_norm, v_ffn1_w_gate, v_ffn1_w_up, v_ffn1_w_down, v_mix_norm, v_w_in, v_lru_conv_w, v_lru_conv_b, v_lru_w_a, v_lru_b_a, v_lru_w_i, v_lru_b_i, v_lru_lambda, v_sc_conv_w, v_lru_out_norm, v_sc_out_norm, v_w_out, v_ffn2_norm, v_ffn2_w_gate, v_ffn2_w_up, v_ffn2_w_down, v_final_norm):
    vals = locals()
    w = {n: vals[n] for n in WEIGHTS}
    mom = {n: vals["m_" + n] for n in WEIGHTS}
    var = {n: vals["v_" + n] for n in WEIGHTS}

    xi, yi, ci = _me()
    qi = 2 * xi + yi
    c_arr = jnp.reshape(ci, (1,)).astype(jnp.int32)
    qc_arr = jnp.stack([qi, ci]).astype(jnp.int32)

    T, D = x.shape[1], x.shape[2]
    xt = x.reshape(T, D)
    target = loss_target.reshape(T, D)
    DL = lru_conv_b.shape[-1]
    NH, HD = lru_w_a.shape[1], lru_w_a.shape[2]
    KL, KS = lru_conv_w.shape[1], sc_conv_w.shape[1]
    DLq = lru_conv_w.shape[2]

    axis_of = dict(zip(BIG, BIG_AXIS))
    placed, full = {}, {}

    def gather(names, pieces={}):
        return _gather_carry([placed[n] for n in names], [axis_of[n] for n in names], [pieces.get(n) for n in names])

    def gathered(names, views, unfinished=()):
        for n, g in zip(names, views):
            if n in unfinished:
                placed[n] = g
            else:
                full[n] = (g.reshape(2 * g.shape[1], g.shape[2]) if axis_of[n] == 1
                           else g.reshape(8 * g.shape[1], g.shape[2]))

    first, rest = 'ffn1_w_gate', [n for n in BIG if n != 'ffn1_w_gate']
    placed[first] = _cast_into_full([w[first][0]], qc_arr, [axis_of[first]], "cast_first_weight")[0]
    res = _cast_into_full([w[n][0] for n in rest], qc_arr, [axis_of[n] for n in rest], "cast_other_weights",
                          _gather_two_way_carry([placed[first]], [axis_of[first]]))
    placed.update(zip(rest, res[:len(rest)]))
    gathered([first], res[len(rest):])

    taps = jnp.zeros((2 * SUBLANES, DL), F32)
    taps = lax.dynamic_update_slice(taps, lru_conv_w[0], (0, qi * DLq))
    taps = lax.dynamic_update_slice(taps, sc_conv_w[0], (KL, qi * DLq))
    taps = _allreduce_small(jnp.where(ci == 0, taps, 0.0), "gather_conv_taps")
    cw, sw = taps[0:KL], taps[KL:KL + KS]

    cb = lru_conv_b
    wa, wi = lru_w_a[0].astype(BF16), lru_w_i[0].astype(BF16)
    ba, bi = lru_b_a.reshape(1, DL), lru_b_i.reshape(1, DL)
    mix_args = (cw, cb, wa, ba, wi, bi, lru_lambda, sw, lru_out_norm, sc_out_norm)
    gf = final_norm.reshape(1, D)

    names = ['ffn1_w_up', 'w_out']
    res = _norm_mm(xt, ffn1_norm, full['ffn1_w_gate'], "ffn1_gate", gather(names))
    n1, G1 = res[:2]
    gathered(names, res[2:])
    names = ['ffn1_w_down', 'ffn2_w_down']
    res = _ffn_up(n1, G1, full['ffn1_w_up'], "ffn1_up", gather(names, {'ffn2_w_down': (0, 2)}))
    U1, H1f = res[:2]
    gathered(names, res[2:], unfinished=['ffn2_w_down'])
    names = ['w_in', 'ffn2_w_down']
    res = _mm_fullk(H1f, full['ffn1_w_down'], False, xt, F32, "ffn1_down", gather(names, {'ffn2_w_down': (1, 2)}),
                    FFN_RESIDUAL_SCALE)
    x1 = res[0]
    gathered(names, res[1:])
    names = ['ffn2_w_gate']
    res = _norm_mm(x1, mix_norm, full['w_in'], "mix_in_proj", gather(names))
    n2, z = res[:2]
    gathered(names, res[2:])
    names = ['ffn2_w_up']
    res = _mix_fwd(z, *mix_args, "mix_fwd", gather(names))
    h, ymix = res[:2]
    gathered(names, res[2:])
    x2 = _mm_fullk(ymix, full['w_out'], False, x1, F32, "mix_out_proj")[0]
    x3, n3, G2, U2 = _ffn_fwd(x2, ffn2_norm, full['ffn2_w_gate'], full['ffn2_w_up'], full['ffn2_w_down'], "ffn2_fwd")
    dx3, d3b, sqerr, dgf = _loss_head(x3, gf, target, "loss_head")

    sums, halves, shared = {}, {}, {}

    def dw(n, a, b, name, carry=None):
        res = _mm_tn_pair(a, b, c_arr, axis_of[n], name, carry)
        sums[n] = res[0]
        return res[2:]

    def chip_carry(names):
        return _chip_exchange_carry([sums[n] for n in names], [axis_of[n] for n in names])

    def chip_add(names, recv):
        for n, r in zip(names, recv):
            halves[n] = _quad_add(sums[n], r, qc_arr, axis_of[n], "grad_chip_add_" + n)

    dG2, dU2, H2 = _ffn_bwd_hidden(d3b, G2, U2, full['ffn2_w_down'], "ffn2_bwd_hidden")
    dn3 = _ffn_bwd_input(dG2, dU2, full['ffn2_w_gate'], full['ffn2_w_up'], "ffn2_bwd_input")[0]
    dx2, dx2b, dg_ffn2 = _rms_bwd_res(dn3, x2, ffn2_norm, dx3, 1.0, "ffn2_norm_bwd")
    dw('ffn2_w_gate', n3, dG2, "ffn2_dwg")
    dw('ffn2_w_up', n3, dU2, "ffn2_dwu")
    dw('ffn2_w_down', H2, d3b, "ffn2_dwd")
    dy = _mm_fullk(dx2b, full['w_out'], True, None, BF16, "mix_out_bwd")[0]
    dz, dwa, dwi, vec = _mix_bwd(z, h, dy, *mix_args, "mix_bwd")
    dg_mix, dx1, d1b = _mm_nt_norm_bwd(dz, full['w_in'], x1, mix_norm, dx2, FFN_RESIDUAL_SCALE, "mix_in_bwd")

    def share_carry(names):
        return _pair_share_carry([halves[n] for n in names])

    res = _ffn_bwd_hidden(d1b, G1, U1, full['ffn1_w_down'], "ffn1_bwd_hidden", chip_carry(['ffn2_w_gate']))
    dG1, dU1, H1 = res[:3]
    chip_add(['ffn2_w_gate'], res[3:])
    res = _ffn_bwd_input(dG1, dU1, full['ffn1_w_gate'], full['ffn1_w_up'], "ffn1_bwd_input",
                         chip_carry(['ffn2_w_up', 'ffn2_w_down']))
    dn1 = res[0]
    chip_add(['ffn2_w_up', 'ffn2_w_down'], res[1:])
    dx0, _, dg_ffn1 = _rms_bwd_res(dn1, xt, ffn1_norm, dx1, 1.0, "ffn1_norm_bwd")
    names = ['ffn2_w_gate', 'ffn2_w_up', 'ffn2_w_down']
    shared.update(zip(names, dw('ffn1_w_gate', n1, dG1, "ffn1_dwg", share_carry(names))))
    chip_add(['ffn1_w_gate'], dw('ffn1_w_up', n1, dU1, "ffn1_dwu", chip_carry(['ffn1_w_gate'])))
    chip_add(['ffn1_w_up'], dw('ffn1_w_down', H1, d1b, "ffn1_dwd", chip_carry(['ffn1_w_up'])))
    names = ['ffn1_w_gate', 'ffn1_w_up']
    res = dw('w_in', n2, dz, "mix_dwin", _merge_carries(chip_carry(['ffn1_w_down']), share_carry(names)))
    chip_add(['ffn1_w_down'], res[:1])
    shared.update(zip(names, res[1:]))
    res = dw('w_out', ymix, dx2b, "mix_dwout", _merge_carries(chip_carry(['w_in']), share_carry(['ffn1_w_down'])))
    chip_add(['w_in'], res[:1])
    shared['ffn1_w_down'] = res[1]
    chip_add(['w_out'], _run_carry(chip_carry(['w_out']), "grad_chip_exchange_w_out"))
    names = ['w_in', 'w_out']
    shared.update(zip(names, _run_carry(share_carry(names), "grad_pair_share")))
    out_g, out_d, out_m, out_v = {}, {}, {}, {}
    for n in BIG:
        shp = w[n].shape
        g = shared[n].reshape(shp[1], shp[2]) if axis_of[n] == 1 else shared[n]
        outs = _adamw(w[n][0], g, mom[n][0], var[n][0], "adamw_" + n)
        out_d[n], out_m[n], out_v[n], out_g[n] = (a.reshape(shp) for a in outs)

    small = [n for n in WEIGHTS if n not in BIG]
    local_small = {
        'ffn1_norm': dg_ffn1, 'mix_norm': dg_mix, 'lru_conv_w': vec[V_CW:V_CW + KL], 'lru_conv_b': vec[V_CB],
        'lru_w_a': dwa, 'lru_b_a': vec[V_BA], 'lru_w_i': dwi, 'lru_b_i': vec[V_BI], 'lru_lambda': vec[V_LAM],
        'sc_conv_w': vec[V_SW:V_SW + KS], 'lru_out_norm': vec[V_GLO], 'sc_out_norm': vec[V_GSO],
        'ffn2_norm': dg_ffn2, 'final_norm': dgf,
    }
    full_shapes = [local_small[n].shape for n in small] + [(1,)]
    reduced = _allreduce_small(_pack([local_small[n] for n in small] + [sqerr[0, 0:1]]), "allreduce_small")
    reduced = _unpack(reduced, full_shapes)
    loss = (0.5 / D) * reduced[-1][0]
    gsm = {}
    for n, g in zip(small, reduced[:-1]):
        if n in SMALL_SHARDED:
            g = lax.dynamic_slice(g, (0, qi * DLq), (g.shape[0], DLq))
        gsm[n] = g.reshape(w[n].shape)
    small_shapes = [w[n].shape for n in small]
    d_s, m_s, v_s, _ = _adamw(_pack([w[n] for n in small]), _pack([gsm[n] for n in small]),
                              _pack([mom[n] for n in small]), _pack([var[n] for n in small]), "adamw_small")
    for n, d, mn, vn in zip(small, _unpack(d_s, small_shapes), _unpack(m_s, small_shapes), _unpack(v_s, small_shapes)):
        out_g[n], out_d[n], out_m[n], out_v[n] = gsm[n], d, mn, vn

    return (loss, dx0.reshape(x.shape), *[out_g[n] for n in WEIGHTS], *[out_d[n] for n in WEIGHTS],
            *[out_m[n] for n in WEIGHTS], *[out_v[n] for n in WEIGHTS])
```

```python
import math

import jax
import jax.numpy as jnp
from jax import lax
from jax.experimental import pallas as pl
from jax.experimental.pallas import tpu as pltpu

F32 = jnp.float32
BF16 = jnp.bfloat16
MESH = pl.DeviceIdType.MESH
ANY = pl.BlockSpec(memory_space=pl.ANY)

NORM_EPS = 1e-6
LRU_C = 8.0
FFN_RESIDUAL_SCALE = 0.5
ADAM_LR = 0.001
ADAM_B1 = 0.9
ADAM_B2 = 0.999
ADAM_EPS = 1e-08
ADAM_WD = 0.01
ADAM_STEP = 10

V7X_VMEM_BYTES = 64 * 2**20
VMEM_LIMIT = V7X_VMEM_BYTES - 8 * 2**20
LANES = 128
SUBLANES = 8
PACK_W = LANES
PACK_ALIGN = SUBLANES * PACK_W

WEIGHTS = ['ffn1_norm', 'ffn1_w_gate', 'ffn1_w_up', 'ffn1_w_down', 'mix_norm', 'w_in', 'lru_conv_w', 'lru_conv_b',
           'lru_w_a', 'lru_b_a', 'lru_w_i', 'lru_b_i', 'lru_lambda', 'sc_conv_w', 'lru_out_norm', 'sc_out_norm',
           'w_out', 'ffn2_norm', 'ffn2_w_gate', 'ffn2_w_up', 'ffn2_w_down', 'final_norm']
BIG = ['ffn1_w_gate', 'ffn1_w_up', 'ffn1_w_down', 'w_in', 'w_out', 'ffn2_w_gate', 'ffn2_w_up', 'ffn2_w_down']
BIG_AXIS = [1, 1, 0, 1, 0, 1, 1, 0]
SMALL_SHARDED = ['lru_conv_w', 'sc_conv_w']


def _tile(n, pref, mult):
    if n <= pref:
        return n
    t = (pref // mult) * mult
    while t >= mult:
        if n % t == 0:
            return t
        t -= mult
    return n


def _params(*sem):
    return pltpu.CompilerParams(dimension_semantics=sem, vmem_limit_bytes=VMEM_LIMIT)


def _me():
    return lax.axis_index("x"), lax.axis_index("y"), lax.axis_index("c")


def _sigmoid(v):
    return 1.0 / (1.0 + jnp.exp(-v))


def _rstd(v):
    return lax.rsqrt(jnp.mean(v * v, axis=-1, keepdims=True) + NORM_EPS)


def _rms_bwd(dy, v, gain):
    r = _rstd(v)
    w = gain * dy
    dv = r * w - v * (r * r * r) * jnp.mean(v * w, axis=-1, keepdims=True)
    dgain = jnp.sum(dy * v * r, axis=0, keepdims=True)
    return dv, dgain


def _dot_nt(a, b):
    return lax.dot_general(a, b, (((1,), (1,)), ((), ())), preferred_element_type=F32)


def _dot_tn(a, b):
    return lax.dot_general(a, b, (((0,), (0,)), ((), ())), preferred_element_type=F32)


class _Carry:
    def __init__(self, inputs, out_shape, aliases, sems, start, finish, middle=None, middle_at=0.85):
        self.inputs, self.out_shape, self.aliases, self.sems = list(inputs), list(out_shape), dict(aliases), list(sems)
        self.start, self.finish = start, finish
        self.middle, self.middle_at = middle, middle_at


def _merge_carries(a, b):
    ia, oa, sa = len(a.inputs), len(a.out_shape), len(a.sems)
    aliases = dict(a.aliases)
    aliases.update({ia + i: oa + j for i, j in b.aliases.items()})

    def both(which):
        def run(ins, outs, sems):
            getattr(a, which)(ins[:ia], outs[:oa], sems[:sa])
            getattr(b, which)(ins[ia:], outs[oa:], sems[sa:])
        return run

    return _Carry(a.inputs + b.inputs, a.out_shape + b.out_shape, aliases, a.sems + b.sems, both("start"), both("finish"))


def _call(body, name, grid, in_specs, out_specs, out_shape, scratch_shapes, semantics, args, carry=None, prefetch=()):
    np_ = len(prefetch)
    if carry is None:
        spec = pltpu.PrefetchScalarGridSpec(num_scalar_prefetch=np_, grid=grid, in_specs=in_specs, out_specs=out_specs,
                                            scratch_shapes=scratch_shapes)
        return pl.pallas_call(body, name=name, grid_spec=spec, out_shape=out_shape,
                              compiler_params=_params(*semantics))(*prefetch, *args)
    ni, no, ns = len(in_specs), len(out_specs), len(scratch_shapes)
    ci, co = len(carry.inputs), len(carry.out_shape)

    def carrying(*refs):
        pre, refs = refs[:np_], refs[np_:]
        ins, refs = refs[:ni], refs[ni:]
        cins, refs = refs[:ci], refs[ci:]
        outs, refs = refs[:no], refs[no:]
        couts, refs = refs[:co], refs[co:]
        scratch, csems = refs[:ns], refs[ns:]
        step = pl.program_id(0)
        for ax in range(1, len(grid)):
            step = step * grid[ax] + pl.program_id(ax)
        steps = math.prod(grid)
        first = step == 0
        last = step == steps - 1

        @pl.when(first)
        def _():
            carry.start(cins, couts, csems)

        if carry.middle is not None:
            @pl.when(step == min(int(carry.middle_at * steps), steps - 1))
            def _():
                carry.middle(cins, couts, csems)

        body(*pre, *ins, *outs, *scratch)

        @pl.when(last)
        def _():
            carry.finish(cins, couts, csems)

    spec = pltpu.PrefetchScalarGridSpec(
        num_scalar_prefetch=np_, grid=grid, in_specs=list(in_specs) + [ANY] * ci,
        out_specs=list(out_specs) + [ANY] * co, scratch_shapes=list(scratch_shapes) + carry.sems)
    return pl.pallas_call(
        carrying, name=name, grid_spec=spec, out_shape=list(out_shape) + carry.out_shape,
        input_output_aliases={np_ + ni + i: no + j for i, j in carry.aliases.items()},
        compiler_params=_params(*(["arbitrary"] * len(grid))),
    )(*prefetch, *args, *carry.inputs)


def _run_carry(carry, name):
    ci, co = len(carry.inputs), len(carry.out_shape)

    def body(*refs):
        cins, couts, csems = refs[:ci], refs[ci:ci + co], refs[ci + co:]
        carry.start(cins, couts, csems)
        if carry.middle is not None:
            carry.middle(cins, couts, csems)
        carry.finish(cins, couts, csems)

    return pl.pallas_call(body, name=name, in_specs=[ANY] * ci, out_specs=[ANY] * co, out_shape=carry.out_shape,
                          input_output_aliases=carry.aliases, scratch_shapes=carry.sems)(*carry.inputs)


def _ffn_bwd_hidden(db, G, U, wd, name, carry=None):
    T, D = db.shape
    FF = wd.shape[0]
    tm = _tile(T, 1024, 16)
    tf = _tile(FF, 512, LANES)

    def body(d_ref, G_ref, U_ref, wd_ref, dG_ref, dU_ref):
        dH = _dot_nt(d_ref[...], wd_ref[...])
        Gv = G_ref[...].astype(F32)
        Uv = U_ref[...].astype(F32)
        s = _sigmoid(Gv)
        dU_ref[...] = (dH * (Gv * s)).astype(BF16)
        dG_ref[...] = (dH * Uv * (s * (1.0 + Gv * (1.0 - s)))).astype(BF16)

    act = pl.BlockSpec((tm, tf), lambda i, f: (i, f))
    return _call(
        body, name, (T // tm, FF // tf),
        [pl.BlockSpec((tm, D), lambda i, f: (i, 0)), act, act, pl.BlockSpec((tf, D), lambda i, f: (f, 0))],
        [act, act], [jax.ShapeDtypeStruct((T, FF), BF16)] * 2,
        [], ("parallel", "arbitrary"), (db, G, U, wd), carry)


def _ffn_bwd_input(dG, dU, wg, wu, name, carry=None):
    T, FF = dG.shape
    D = wg.shape[0]
    tm = _tile(T, 512, 16)
    tn = _tile(D, 512, LANES)

    def body(dG_ref, dU_ref, wg_ref, wu_ref, dn_ref):
        dn_ref[...] = (_dot_nt(dG_ref[...], wg_ref[...]) + _dot_nt(dU_ref[...], wu_ref[...])).astype(BF16)

    act = pl.BlockSpec((tm, FF), lambda i, j: (i, 0))
    wt = pl.BlockSpec((tn, FF), lambda i, j: (j, 0))
    return _call(body, name, (T // tm, D // tn), [act, act, wt, wt], [pl.BlockSpec((tm, tn), lambda i, j: (i, j))],
                 [jax.ShapeDtypeStruct((T, D), BF16)], [], ("parallel", "arbitrary"), (dG, dU, wg, wu), carry)


TAIL_ROWS = 128


class _NormBwdTail:
    def __init__(self, T, D, tm, scale):
        self.T, self.D, self.tm, self.scale = T, D, tm, scale
        self.ni = T // tm
        self.scratch = [pltpu.VMEM((tm, D), F32), pltpu.VMEM((tm, D), F32), pltpu.VMEM((tm, D), F32),
                        pltpu.VMEM((tm, D), BF16), pltpu.SemaphoreType.DMA((4,))]
        self.out_shape = [jax.ShapeDtypeStruct((T, D), F32), jax.ShapeDtypeStruct((T, D), BF16)]

    def _rows(self, k):
        return pl.ds(pl.multiple_of(k * self.tm, self.tm), self.tm)

    def _loads(self, k, x_hbm, r_hbm, bufs):
        xbuf, rbuf, _, _, sems = bufs
        return [pltpu.make_async_copy(x_hbm.at[self._rows(k)], xbuf, sems.at[0]),
                pltpu.make_async_copy(r_hbm.at[self._rows(k)], rbuf, sems.at[1])]

    def _stores(self, k, dx_hbm, dxb_hbm, bufs):
        _, _, obuf, obbuf, sems = bufs
        return [pltpu.make_async_copy(obuf, dx_hbm.at[self._rows(k)], sems.at[2]),
                pltpu.make_async_copy(obbuf, dxb_hbm.at[self._rows(k)], sems.at[3])]

    def prefetch(self, i, x_hbm, r_hbm, bufs):
        for cp in self._loads(i, x_hbm, r_hbm, bufs):
            cp.start()

    def run(self, i, acc_ref, g_ref, x_hbm, r_hbm, dx_hbm, dxb_hbm, dg_ref, bufs):
        xbuf, rbuf, obuf, obbuf, _ = bufs
        for cp in self._loads(i, x_hbm, r_hbm, bufs):
            cp.wait()

        @pl.when(i > 0)
        def _():
            for cp in self._stores(i - 1, dx_hbm, dxb_hbm, bufs):
                cp.wait()

        dgain = None
        for r0 in range(0, self.tm, TAIL_ROWS):
            rs = slice(r0, min(r0 + TAIL_ROWS, self.tm))
            dv, dgr = _rms_bwd(acc_ref[rs, :], xbuf[rs, :], g_ref[...])
            dx = rbuf[rs, :] + dv
            obuf[rs, :] = dx
            obbuf[rs, :] = (self.scale * dx).astype(BF16)
            dgain = dgr if dgain is None else dgain + dgr
        for cp in self._stores(i, dx_hbm, dxb_hbm, bufs):
            cp.start()

        @pl.when(i == 0)
        def _():
            dg_ref[...] = dgain

        @pl.when(i > 0)
        def _():
            dg_ref[...] += dgain

        @pl.when(i == self.ni - 1)
        def _():
            for cp in self._stores(i, dx_hbm, dxb_hbm, bufs):
                cp.wait()


def _mm_nt_norm_bwd(a, w, x_in, gain, dres, scale, name, carry=None):
    T, K = a.shape
    D = w.shape[0]
    whole = D * K * 2 <= 24 * 2**20
    tm = _tile(T, 256 if whole else 512, 16)
    tk = K if whole else _tile(K, 1280, LANES)
    nk = K // tk
    tail = _NormBwdTail(T, D, tm, scale)
    w_spec = (pl.BlockSpec((D, tk), lambda i, k: (0, k), pipeline_mode=pl.Buffered(1)) if whole
              else pl.BlockSpec((D, tk), lambda i, k: (0, k)))

    def body(a_ref, w_ref, g_ref, x_hbm, r_hbm, dg_ref, dx_hbm, dxb_hbm, acc_ref, *bufs):
        i, k = pl.program_id(0), pl.program_id(1)

        @pl.when(k == 0)
        def _():
            tail.prefetch(i, x_hbm, r_hbm, bufs)

        contrib = _dot_nt(a_ref[...], w_ref[...])

        @pl.when(k == 0)
        def _():
            acc_ref[...] = contrib

        @pl.when(k > 0)
        def _():
            acc_ref[...] += contrib

        @pl.when(k == nk - 1)
        def _():
            tail.run(i, acc_ref, g_ref, x_hbm, r_hbm, dx_hbm, dxb_hbm, dg_ref, bufs)

    return _call(
        body, name, (T // tm, nk),
        [pl.BlockSpec((tm, tk), lambda i, k: (i, k)), w_spec, pl.BlockSpec((1, D), lambda i, k: (0, 0)), ANY, ANY],
        [pl.BlockSpec((1, D), lambda i, k: (0, 0)), ANY, ANY],
        [jax.ShapeDtypeStruct((1, D), F32)] + tail.out_shape,
        [pltpu.VMEM((tm, D), F32)] + tail.scratch, ("arbitrary", "arbitrary"), (a, w, gain, x_in, dres), carry)


def _rms_bwd_res(dn, x, gain, dres, scale, name, carry=None):
    T, D = x.shape
    tm = _tile(T, 256, 16)

    def body(dn_ref, x_ref, g_ref, dr_ref, dx_ref, dxb_ref, dg_ref):
        i = pl.program_id(0)
        dv, dgain = _rms_bwd(dn_ref[...].astype(F32), x_ref[...], g_ref[...])
        dx = dr_ref[...] + dv
        dx_ref[...] = dx
        dxb_ref[...] = (scale * dx).astype(BF16)

        @pl.when(i == 0)
        def _():
            dg_ref[...] = dgain

        @pl.when(i > 0)
        def _():
            dg_ref[...] += dgain

    row = pl.BlockSpec((tm, D), lambda i: (i, 0))
    vec = pl.BlockSpec((1, D), lambda i: (0, 0))
    return _call(
        body, name, (T // tm,), [row, row, vec, row], [row, row, vec],
        [jax.ShapeDtypeStruct((T, D), F32), jax.ShapeDtypeStruct((T, D), BF16), jax.ShapeDtypeStruct((1, D), F32)],
        [], ("arbitrary",), (dn, x, gain, dres), carry)


def _loss_head(x3, gain, target, name):
    T, D = x3.shape
    tm = _tile(T, 256, 16)

    def body(x_ref, g_ref, t_ref, dx_ref, dxb_ref, ls_ref, dg_ref):
        i = pl.program_id(0)
        xv = x_ref[...]
        err = xv * _rstd(xv) * g_ref[...] - t_ref[...]
        sq = jnp.sum(jnp.sum(err * err, axis=1, keepdims=True), axis=0, keepdims=True)
        dv, dgain = _rms_bwd(err * (1.0 / D), xv, g_ref[...])
        dx_ref[...] = dv
        dxb_ref[...] = (FFN_RESIDUAL_SCALE * dv).astype(BF16)
        sqb = jnp.broadcast_to(sq, (1, LANES))

        @pl.when(i == 0)
        def _():
            dg_ref[...] = dgain
            ls_ref[...] = sqb

        @pl.when(i > 0)
        def _():
            dg_ref[...] += dgain
            ls_ref[...] += sqb

    row = pl.BlockSpec((tm, D), lambda i: (i, 0))
    vec = pl.BlockSpec((1, D), lambda i: (0, 0))
    return pl.pallas_call(
        body, name=name, grid=(T // tm,),
        in_specs=[row, vec, row],
        out_specs=[row, row, pl.BlockSpec((1, LANES), lambda i: (0, 0)), vec],
        out_shape=[jax.ShapeDtypeStruct((T, D), F32), jax.ShapeDtypeStruct((T, D), BF16),
                   jax.ShapeDtypeStruct((1, LANES), F32), jax.ShapeDtypeStruct((1, D), F32)],
        compiler_params=_params("arbitrary"),
    )(x3, gain, target)


WHOLE_MIN, WHOLE_MAX = 16 * 2**20, 24 * 2**20


def _fits_whole(w):
    return WHOLE_MIN < w.size * 2 <= WHOLE_MAX


def _weight_spec(block, index_map, whole):
    return pl.BlockSpec(block, index_map, pipeline_mode=pl.Buffered(1)) if whole else pl.BlockSpec(block, index_map)


def _norm_mm(x, gain, w, name, carry=None):
    T, D = x.shape
    N = w.shape[1]
    whole = _fits_whole(w)
    tm = _tile(T, 256 if whole else 512, 16)
    tn = N if whole else _tile(N, 2560, LANES)

    def body(x_ref, g_ref, w_ref, n_ref, z_ref):
        @pl.when(pl.program_id(1) == 0)
        def _():
            xv = x_ref[...]
            n_ref[...] = (xv * _rstd(xv) * g_ref[...]).astype(BF16)

        z_ref[...] = jnp.dot(n_ref[...], w_ref[...], preferred_element_type=F32).astype(BF16)

    return _call(
        body, name, (T // tm, N // tn),
        [pl.BlockSpec((tm, D), lambda i, j: (i, 0)),
         pl.BlockSpec((1, D), lambda i, j: (0, 0)),
         _weight_spec((D, tn), lambda i, j: (0, j), whole)],
        [pl.BlockSpec((tm, D), lambda i, j: (i, 0)),
         pl.BlockSpec((tm, tn), lambda i, j: (i, j))],
        [jax.ShapeDtypeStruct((T, D), BF16), jax.ShapeDtypeStruct((T, N), BF16)],
        [], ("parallel", "arbitrary"), (x, gain, w), carry)


def _ffn_up(n, G, wu, name, carry=None):
    T, D = n.shape
    FF = wu.shape[1]
    whole = _fits_whole(wu)
    tm = _tile(T, 128 if whole else 512, 16)
    tf = FF if whole else _tile(FF, 1408, LANES)

    def body(n_ref, G_ref, wu_ref, U_ref, H_ref):
        U = jnp.dot(n_ref[...], wu_ref[...], preferred_element_type=F32)
        Gv = G_ref[...].astype(F32)
        U_ref[...] = U.astype(BF16)
        H_ref[...] = (Gv * _sigmoid(Gv) * U).astype(BF16)

    act = pl.BlockSpec((tm, tf), lambda i, f: (i, f))
    return _call(
        body, name, (T // tm, FF // tf),
        [pl.BlockSpec((tm, D), lambda i, f: (i, 0)), act, _weight_spec((D, tf), lambda i, f: (0, f), whole)],
        [act, act], [jax.ShapeDtypeStruct((T, FF), BF16)] * 2, [], ("parallel", "arbitrary"), (n, G, wu), carry)


def _mm_fullk(a, w, trans_w, residual, out_dtype, name, carry=None, scale=1.0):
    T, K = a.shape
    N = w.shape[0] if trans_w else w.shape[1]
    whole = _fits_whole(w)
    tm = _tile(T, 256 if whole else 512, 16)
    tn = N if whole else _tile(N, 2048 * 2560 // K, LANES)

    def body(*refs):
        if residual is None:
            a_ref, w_ref, o_ref = refs
        else:
            a_ref, w_ref, r_ref, o_ref = refs
        if trans_w:
            acc = _dot_nt(a_ref[...], w_ref[...])
        else:
            acc = jnp.dot(a_ref[...], w_ref[...], preferred_element_type=F32)
        if scale != 1.0:
            acc = scale * acc
        if residual is not None:
            acc = acc + r_ref[...]
        o_ref[...] = acc.astype(out_dtype)

    w_spec = (_weight_spec((tn, K), lambda i, j: (j, 0), whole) if trans_w
              else _weight_spec((K, tn), lambda i, j: (0, j), whole))
    in_specs = [pl.BlockSpec((tm, K), lambda i, j: (i, 0)), w_spec]
    args = [a, w]
    if residual is not None:
        in_specs.append(pl.BlockSpec((tm, tn), lambda i, j: (i, j)))
        args.append(residual)
    return _call(body, name, (T // tm, N // tn), in_specs, [pl.BlockSpec((tm, tn), lambda i, j: (i, j))],
                 [jax.ShapeDtypeStruct((T, N), out_dtype)], [], ("parallel", "arbitrary"), args, carry)


def _mm_tn_pair(a, b, c_arr, axis, name, carry=None):
    T, M = a.shape
    N = b.shape[1]
    tk = _tile(T, 2048, 16)
    nk = T // tk
    nq = 4
    if axis == 1:
        rows, cols = M // 2, N // nq
        a_spec = pl.BlockSpec((tk, rows), lambda p, j, k, cr: (k, jnp.where(p == 0, 1 - cr[0], cr[0])))
        b_spec = pl.BlockSpec((tk, cols), lambda p, j, k, cr: (k, j))
        s_shape, land_shape = (1, rows, N), (rows, N)
        s_spec = pl.BlockSpec((None, rows, cols), lambda p, j, k, cr: (0, 0, j * p))
    else:
        rows, cols = M // nq, N // 2
        a_spec = pl.BlockSpec((tk, rows), lambda p, j, k, cr: (k, j))
        b_spec = pl.BlockSpec((tk, cols), lambda p, j, k, cr: (k, jnp.where(p == 0, 1 - cr[0], cr[0])))
        s_shape, land_shape = (nq, rows, cols), (nq, rows, cols)
        s_spec = pl.BlockSpec((None, rows, cols), lambda p, j, k, cr: (j * p, 0, 0))

    def body(c_ref, a_ref, b_ref, s_ref, land, acc_ref, stage, got, send_sems, recv_sems, loc_sem):
        p, j, k = pl.program_id(0), pl.program_id(1), pl.program_id(2)
        x, y, c = _me()

        def tile(jj):
            return land.at[:, pl.ds(jj * cols, cols)] if axis == 1 else land.at[jj]

        def send(jj):
            return _remote(stage, tile(jj), send_sems, recv_sems, jj, (x, y, 1 - c))

        @pl.when(k == 0)
        def _():
            acc_ref[...] = jnp.zeros_like(acc_ref)

        acc_ref[...] += _dot_tn(a_ref[...], b_ref[...])

        def fetch(jj):
            return pltpu.make_async_copy(tile(jj), got, loc_sem.at[0])

        for jj in range(nq):
            @pl.when(jnp.logical_and(k == nk - 1, jnp.logical_and(p == 0, j == jj)))
            def _():
                if jj > 0:
                    send(jj - 1).wait_send()
                stage[...] = acc_ref[...].astype(BF16)
                send(jj).start()

            @pl.when(jnp.logical_and(k == max(nk - 2, 0), jnp.logical_and(p == 1, j == jj)))
            def _():
                if jj == 0:
                    send(nq - 1).wait_send()
                send(jj).wait_recv()
                fetch(jj).start()

            @pl.when(jnp.logical_and(k == nk - 1, jnp.logical_and(p == 1, j == jj)))
            def _():
                fetch(jj).wait()
                s_ref[...] = (acc_ref[...] + got[...].astype(F32)).astype(BF16)

    return _call(
        body, name, (2, nq, nk), [a_spec, b_spec], [s_spec, ANY],
        [jax.ShapeDtypeStruct(s_shape, BF16), jax.ShapeDtypeStruct(land_shape, BF16)],
        [pltpu.VMEM((rows, cols), F32), pltpu.VMEM((rows, cols), BF16), pltpu.VMEM((rows, cols), BF16),
         pltpu.SemaphoreType.DMA((nq,)), pltpu.SemaphoreType.DMA((nq,)), pltpu.SemaphoreType.DMA((1,))],
        ("arbitrary", "arbitrary", "arbitrary"), (a, b), carry, (c_arr,))


GELU_K = math.sqrt(2.0 / math.pi)
GELU_C = 0.044715


def _gelu_and_grad(v):
    u = GELU_K * (v + GELU_C * v * v * v)
    th = jnp.tanh(u)
    g = 0.5 * v * (1.0 + th)
    dg = 0.5 * (1.0 + th) + 0.5 * v * (1.0 - th * th) * GELU_K * (1.0 + 3.0 * GELU_C * v * v)
    return g, dg


def _neg_expm1(v):
    poly = v * (1.0 + v * (0.5 + v * (1.0 / 6 + v * (1.0 / 24 + v * (1.0 / 120 + v * (1.0 / 720))))))
    return jnp.where(v > -0.25, -poly, 1.0 - jnp.exp(v))


def _softplus_neg(lam):
    e = jnp.exp(-jnp.abs(lam))
    log1pe = jnp.where(e < 1e-4, e * (1.0 - 0.5 * e), jnp.log(1.0 + e))
    sp = jnp.maximum(-lam, 0.0) + log1pe
    dsp = -1.0 / (1.0 + jnp.exp(lam))
    return sp, dsp


def _earlier(ext, j):
    return pltpu.roll(ext, j, 0)[SUBLANES:, :]


def _later(ext, j):
    n = ext.shape[0]
    return pltpu.roll(ext, n - j, 0)[:n - SUBLANES, :]


def _taps(v, halo, K):
    ext = jnp.concatenate([halo, v], axis=0)
    return [v] + [_earlier(ext, j) for j in range(1, K)]


def _block_diag(vb, w_ref, nh, hd):
    return jnp.concatenate(
        [jnp.dot(vb[:, h * hd:(h + 1) * hd], w_ref[h], preferred_element_type=F32) for h in range(nh)], axis=1)


def _lru_gates(xc, wa_ref, ba_ref, wi_ref, bi_ref, sp, nh, hd):
    xcb = xc.astype(BF16)
    r = _sigmoid(_block_diag(xcb, wa_ref, nh, hd) + ba_ref[...])
    ig = _sigmoid(_block_diag(xcb, wi_ref, nh, hd) + bi_ref[...])
    log_a = -LRU_C * r * sp
    a = jnp.exp(log_a)
    mult = jnp.sqrt(_neg_expm1(2.0 * log_a))
    return xcb, r, ig, a, mult


def _mix_fwd(z, cw, cb, wa, ba, wi, bi, lam, sw, glo, gso, name, carry=None):
    T = z.shape[0]
    DL = cb.shape[1]
    DS = gso.shape[1]
    NH, HD = wa.shape[0], wa.shape[1]
    KL, KS = cw.shape[0], sw.shape[0]
    tt = _tile(T, 128, 16)
    o_g, o_b, o_c, o_x = DL, 2 * DL, 2 * DL + DS, 2 * DL + 2 * DS

    def body(z_ref, cw_ref, cb_ref, wa_ref, ba_ref, wi_ref, bi_ref, lam_ref, sw_ref, glo_ref, gso_ref,
             h_ref, y_ref, cx_ref, cp_ref, ch_ref):
        @pl.when(pl.program_id(0) == 0)
        def _():
            cx_ref[...] = jnp.zeros_like(cx_ref)
            cp_ref[...] = jnp.zeros_like(cp_ref)
            ch_ref[...] = jnp.zeros_like(ch_ref)

        def zcol(o, n):
            return z_ref[:, o:o + n].astype(F32)

        lx = zcol(0, DL)
        xs = _taps(lx, cx_ref[...], KL)
        cx_ref[...] = lx[tt - SUBLANES:, :]
        xc = cb_ref[...] + xs[0] * cw_ref[KL - 1:KL, :]
        for j in range(1, KL):
            xc = xc + xs[j] * cw_ref[KL - 1 - j:KL - j, :]
        sp, _ = _softplus_neg(lam_ref[...])
        _, _, ig, a, mult = _lru_gates(xc, wa_ref, ba_ref, wi_ref, bi_ref, sp, NH, HD)
        b = mult * (ig * xc)
        rows = lax.broadcasted_iota(jnp.int32, (tt, DL), 0)
        s = 1
        while s < tt:
            keep = rows >= s
            b = jnp.where(keep, a * pltpu.roll(b, s, 0) + b, b)
            a = jnp.where(keep, a * pltpu.roll(a, s, 0), a)
            s *= 2
        h = a * ch_ref[SUBLANES - 1:SUBLANES, :] + b
        ch_ref[...] = h[tt - SUBLANES:, :]
        h_ref[...] = h
        ge, _ = _gelu_and_grad(zcol(o_g, DL))
        ylru = h * ge
        y_ref[:, 0:DL] = (ylru * _rstd(ylru) * glo_ref[...]).astype(BF16)

        p = zcol(o_c, DS) * zcol(o_x, DS)
        ps = _taps(p, cp_ref[...], KS)
        cp_ref[...] = p[tt - SUBLANES:, :]
        cv = ps[0] * sw_ref[KS - 1:KS, :]
        for j in range(1, KS):
            cv = cv + ps[j] * sw_ref[KS - 1 - j:KS - j, :]
        ysc = zcol(o_b, DS) * cv
        y_ref[:, DL:DL + DS] = (ysc * _rstd(ysc) * gso_ref[...]).astype(BF16)

    def full(shape):
        return pl.BlockSpec(shape, lambda t: (0,) * len(shape))

    return _call(
        body, name, (T // tt,),
        [pl.BlockSpec((tt, z.shape[1]), lambda t: (t, 0)),
         full(cw.shape), full(cb.shape), full(wa.shape), full(ba.shape), full(wi.shape), full(bi.shape),
         full(lam.shape), full(sw.shape), full(glo.shape), full(gso.shape)],
        [pl.BlockSpec((tt, DL), lambda t: (t, 0)), pl.BlockSpec((tt, DL + DS), lambda t: (t, 0))],
        [jax.ShapeDtypeStruct((T, DL), F32), jax.ShapeDtypeStruct((T, DL + DS), BF16)],
        [pltpu.VMEM((SUBLANES, DL), F32), pltpu.VMEM((SUBLANES, DS), F32), pltpu.VMEM((SUBLANES, DL), F32)],
        ("arbitrary",), (z, cw, cb, wa, ba, wi, bi, lam, sw, glo, gso), carry)


V_BA, V_BI, V_LAM, V_CB, V_CW, V_SW, V_GLO, V_GSO, V_ROWS = 0, 1, 2, 3, 4, 8, 11, 12, 16


def _mix_bwd(z, h, dy, cw, cb, wa, ba, wi, bi, lam, sw, glo, gso, name):
    T = z.shape[0]
    DL = cb.shape[1]
    DS = gso.shape[1]
    NH, HD = wa.shape[0], wa.shape[1]
    KL, KS = cw.shape[0], sw.shape[0]
    tt = _tile(T, 64, 16)
    nt = T // tt
    ZH = 2 * SUBLANES
    o_g, o_b, o_c, o_x = DL, 2 * DL, 2 * DL + DS, 2 * DL + 2 * DS

    def body(z_ref, zh_ref, h_ref, hh_ref, dy_ref, cw_ref, cb_ref, wa_ref, ba_ref, wi_ref, bi_ref, lam_ref,
             sw_ref, glo_ref, gso_ref, dz_ref, dwa_ref, dwi_ref, vec_ref, cdx_ref, cdc_ref, cdh_ref):
        i = pl.program_id(0)
        tr = nt - 1 - i

        @pl.when(i == 0)
        def _():
            dwa_ref[...] = jnp.zeros_like(dwa_ref)
            dwi_ref[...] = jnp.zeros_like(dwi_ref)
            vec_ref[...] = jnp.zeros_like(vec_ref)
            cdx_ref[...] = jnp.zeros_like(cdx_ref)
            cdc_ref[...] = jnp.zeros_like(cdc_ref)
            cdh_ref[...] = jnp.zeros_like(cdh_ref)

        def acc_row(r, v):
            vec_ref[pl.ds(r, 1), :] += jnp.sum(v, axis=0, keepdims=True)

        has_prev = tr > 0
        rows = lax.broadcasted_iota(jnp.int32, (tt, DL), 0)

        def zcol(o, n):
            return z_ref[:, o:o + n].astype(F32)

        def zhalo(o, n):
            return jnp.where(has_prev, zh_ref[:, o:o + n].astype(F32)[SUBLANES:, :], 0.0)

        lx = zcol(0, DL)
        xs = _taps(lx, zhalo(0, DL), KL)
        xc = cb_ref[...] + xs[0] * cw_ref[KL - 1:KL, :]
        for j in range(1, KL):
            xc = xc + xs[j] * cw_ref[KL - 1 - j:KL - j, :]
        sp, dsp = _softplus_neg(lam_ref[...])
        xcb, r, ig, a, mult = _lru_gates(xc, wa_ref, ba_ref, wi_ref, bi_ref, sp, NH, HD)
        hv = h_ref[...]
        hprev = _earlier(jnp.concatenate([jnp.where(has_prev, hh_ref[...], 0.0), hv], axis=0), 1)
        gate = zcol(o_g, DL)
        ge, dge = _gelu_and_grad(gate)
        ylru = hv * ge

        d_ylru, dglo = _rms_bwd(dy_ref[:, 0:DL].astype(F32), ylru, glo_ref[...])
        vec_ref[pl.ds(V_GLO, 1), :] += dglo
        dz_ref[:, o_g:o_g + DL] = (d_ylru * hv * dge).astype(BF16)
        bq = d_ylru * ge
        aq = jnp.where(rows == tt - 1, 1.0, pltpu.roll(a, tt - 1, 0))
        s = 1
        while s < tt:
            keep = rows < tt - s
            bq = jnp.where(keep, aq * pltpu.roll(bq, tt - s, 0) + bq, bq)
            aq = jnp.where(keep, aq * pltpu.roll(aq, tt - s, 0), aq)
            s *= 2
        dhh = bq + aq * cdh_ref[0:1, :]
        cdh_ref[0:1, :] = a[0:1, :] * dhh[0:1, :]

        da = dhh * hprev
        dmult = dhh * (ig * xc)
        d_i = dhh * mult * xc
        dxc = dhh * mult * ig
        dlog = da * a - dmult * (a * a) / mult
        acc_row(V_LAM, dlog * (-LRU_C * r) * dsp)
        dpa = dlog * (-LRU_C * sp) * r * (1.0 - r)
        dpi = d_i * ig * (1.0 - ig)
        acc_row(V_BA, dpa)
        acc_row(V_BI, dpi)
        dpab = dpa.astype(BF16)
        dpib = dpi.astype(BF16)
        back = []
        for hh in range(NH):
            sl = slice(hh * HD, (hh + 1) * HD)
            dwa_ref[hh] += _dot_tn(xcb[:, sl], dpab[:, sl])
            dwi_ref[hh] += _dot_tn(xcb[:, sl], dpib[:, sl])
            back.append(_dot_nt(dpab[:, sl], wa_ref[hh]) + _dot_nt(dpib[:, sl], wi_ref[hh]))
        dxc = dxc + jnp.concatenate(back, axis=1)

        acc_row(V_CB, dxc)
        extd = jnp.concatenate([dxc, cdx_ref[...]], axis=0)
        cdx_ref[...] = dxc[0:SUBLANES, :]
        dlx = dxc * cw_ref[KL - 1:KL, :]
        acc_row(V_CW + KL - 1, dxc * xs[0])
        for j in range(1, KL):
            dlx = dlx + _later(extd, j) * cw_ref[KL - 1 - j:KL - j, :]
            acc_row(V_CW + KL - 1 - j, dxc * xs[j])
        dz_ref[:, 0:DL] = dlx.astype(BF16)

        sb = zcol(o_b, DS)
        sc = zcol(o_c, DS)
        sx = zcol(o_x, DS)
        p = sc * sx
        ps = _taps(p, zhalo(o_c, DS) * zhalo(o_x, DS), KS)
        cv = ps[0] * sw_ref[KS - 1:KS, :]
        for j in range(1, KS):
            cv = cv + ps[j] * sw_ref[KS - 1 - j:KS - j, :]
        d_ysc, dgso = _rms_bwd(dy_ref[:, DL:DL + DS].astype(F32), sb * cv, gso_ref[...])
        vec_ref[pl.ds(V_GSO, 1), :] += dgso
        dz_ref[:, o_b:o_b + DS] = (d_ysc * cv).astype(BF16)
        dcv = d_ysc * sb
        extc = jnp.concatenate([dcv, cdc_ref[...]], axis=0)
        cdc_ref[...] = dcv[0:SUBLANES, :]
        dp = dcv * sw_ref[KS - 1:KS, :]
        acc_row(V_SW + KS - 1, dcv * ps[0])
        for j in range(1, KS):
            dp = dp + _later(extc, j) * sw_ref[KS - 1 - j:KS - j, :]
            acc_row(V_SW + KS - 1 - j, dcv * ps[j])
        dz_ref[:, o_c:o_c + DS] = (dp * sx).astype(BF16)
        dz_ref[:, o_x:o_x + DS] = (dp * sc).astype(BF16)

    def full(shape):
        return pl.BlockSpec(shape, lambda t: (0,) * len(shape))

    def rev(t):
        return nt - 1 - t

    def halo(t, rows):
        return jnp.maximum(rev(t) * (tt // rows) - 1, 0)

    return pl.pallas_call(
        body, name=name, grid=(nt,),
        in_specs=[pl.BlockSpec((tt, z.shape[1]), lambda t: (rev(t), 0)),
                  pl.BlockSpec((ZH, z.shape[1]), lambda t: (halo(t, ZH), 0)),
                  pl.BlockSpec((tt, DL), lambda t: (rev(t), 0)),
                  pl.BlockSpec((SUBLANES, DL), lambda t: (halo(t, SUBLANES), 0)),
                  pl.BlockSpec((tt, DL + DS), lambda t: (rev(t), 0)),
                  full(cw.shape), full(cb.shape), full(wa.shape), full(ba.shape), full(wi.shape), full(bi.shape),
                  full(lam.shape), full(sw.shape), full(glo.shape), full(gso.shape)],
        out_specs=[pl.BlockSpec((tt, z.shape[1]), lambda t: (rev(t), 0)),
                   full(wa.shape), full(wi.shape), full((V_ROWS, DL))],
        out_shape=[jax.ShapeDtypeStruct(z.shape, BF16), jax.ShapeDtypeStruct(wa.shape, F32),
                   jax.ShapeDtypeStruct(wi.shape, F32), jax.ShapeDtypeStruct((V_ROWS, DL), F32)],
        scratch_shapes=[pltpu.VMEM((SUBLANES, DL), F32), pltpu.VMEM((SUBLANES, DS), F32),
                        pltpu.VMEM((SUBLANES, DL), F32)],
        compiler_params=_params("arbitrary"),
    )(z, z, h, h, dy, cw, cb, wa, ba, wi, bi, lam, sw, glo, gso)


def _quad_add(s, r2, qc, axis, name):
    _, R, W = r2.shape
    tr = _tile(R, 256, 16)

    def body(qc_ref, s_ref, r0_ref, r1_ref, r2_ref, o_ref):
        o_ref[...] = ((s_ref[...].astype(F32) + r0_ref[...].astype(F32)) + r1_ref[...].astype(F32)) + r2_ref[...].astype(F32)

    blk = (None, tr, W)
    if axis == 1:
        own = pl.BlockSpec(blk, lambda i, qr: (0, i, qr[0]))
    else:
        own = pl.BlockSpec(blk, lambda i, qr: (qr[0], i, 0))
    return pl.pallas_call(
        body, name=name,
        grid_spec=pltpu.PrefetchScalarGridSpec(
            num_scalar_prefetch=1, grid=(R // tr,),
            in_specs=[own] + [pl.BlockSpec(blk, lambda i, qr, j=j: (j, i, 0)) for j in range(3)],
            out_specs=pl.BlockSpec(blk, lambda i, qr: (qr[1], i, 0))),
        out_shape=jax.ShapeDtypeStruct((2, R, W), F32),
        compiler_params=_params("parallel"),
    )(qc, s, r2, r2, r2)


CAST_BLOCKS = 4


def _cast_into_full(shards, qc, axes, name, carry=None):
    M = len(shards)
    nb = CAST_BLOCKS

    def body(qc_ref, *refs):
        for s_ref, o_ref in zip(refs[:M], refs[M:]):
            o_ref[...] = s_ref[...].astype(BF16)

    in_specs, out_specs, out_shape = [], [], []
    for s, ax in zip(shards, axes):
        R, W = s.shape
        Rh = R // 2
        tr = Rh // nb
        in_specs.append(pl.BlockSpec((tr, W), lambda hf, i, qr: (hf * nb + i, 0)))
        if ax == 1:
            out_shape.append(jax.ShapeDtypeStruct((2, Rh, 4 * W), BF16))
            out_specs.append(pl.BlockSpec((None, tr, W), lambda hf, i, qr: (hf, i, qr[0])))
        else:
            out_shape.append(jax.ShapeDtypeStruct((8, Rh, W), BF16))
            out_specs.append(pl.BlockSpec((None, tr, W), lambda hf, i, qr: (2 * qr[0] + hf, i, 0)))
    return _call(body, name, (2, nb), in_specs, out_specs, out_shape, [], ("parallel", "parallel"), shards, carry, (qc,))


def _adamw(w, g, m, v, name):
    R, C = w.shape
    tr = _tile(R, 256, SUBLANES)
    tc = C // 2 if g.ndim == 3 else _tile(C, 2048, LANES)
    c1 = 1.0 - ADAM_B1 ** ADAM_STEP
    c2 = 1.0 - ADAM_B2 ** ADAM_STEP

    def body(w_ref, g_ref, m_ref, v_ref, d_ref, mo_ref, vo_ref, go_ref):
        gv = g_ref[...]
        go_ref[...] = gv
        mn = ADAM_B1 * m_ref[...] + (1.0 - ADAM_B1) * gv
        vn = ADAM_B2 * v_ref[...] + (1.0 - ADAM_B2) * (gv * gv)
        mo_ref[...] = mn
        vo_ref[...] = vn
        d_ref[...] = -ADAM_LR * ((mn / c1) / (jnp.sqrt(vn / c2) + ADAM_EPS) + ADAM_WD * w_ref[...])

    blk = pl.BlockSpec((tr, tc), lambda i, j: (i, j))
    g_blk = pl.BlockSpec((None, tr, tc), lambda i, j: (j, i, 0)) if g.ndim == 3 else blk
    sh = jax.ShapeDtypeStruct((R, C), F32)
    return pl.pallas_call(
        body, name=name, grid=(R // tr, C // tc),
        in_specs=[blk, g_blk, blk, blk], out_specs=[blk] * 4, out_shape=[sh] * 4,
        compiler_params=_params("parallel", "parallel"),
    )(w, g, m, v)


def _other_chips(x, y):
    return [(1 - x, y), (x, 1 - y), (1 - x, 1 - y)]


def _remote(src, dst, send_sems, recv_sems, idx, dev):
    return pltpu.make_async_remote_copy(src_ref=src, dst_ref=dst, send_sem=send_sems.at[idx], recv_sem=recv_sems.at[idx],
                                        device_id=dev, device_id_type=MESH)


def _gather_carry(fulls, axes, pieces=None):
    M = len(fulls)

    def win(outs, m, qq, cc):
        Rh = fulls[m].shape[1]
        k, n = pieces[m] if pieces is not None and pieces[m] is not None else (0, 1)
        rows = pl.ds(k * (Rh // n), Rh // n)
        if axes[m] == 1:
            W = fulls[m].shape[2] // 4
            return outs[m].at[cc, rows, pl.ds(pl.multiple_of(qq * W, LANES), W)]
        return outs[m].at[2 * qq + cc, rows, :]

    def ici(outs, sems, m, j, src_q):
        x, y, c = _me()
        cx, cy = _other_chips(x, y)[j]
        blk = win(outs, m, src_q, c)
        return _remote(blk, blk, sems[0], sems[1], 6 * m + j, (cx, cy, c))

    def d2d(outs, sems, m, j, half):
        x, y, c = _me()
        cx, cy = _other_chips(x, y)[j]
        blk = win(outs, m, 2 * cx + cy, half)
        return _remote(blk, blk, sems[0], sems[1], 6 * m + 3 + j, (x, y, 1 - c))

    def start(ins, outs, sems):
        x, y, c = _me()
        for m in range(M):
            for j in range(3):
                ici(outs, sems, m, j, 2 * x + y).start()

    def middle(ins, outs, sems):
        x, y, c = _me()
        for m in range(M):
            for j, (cx, cy) in enumerate(_other_chips(x, y)):
                ici(outs, sems, m, j, 2 * cx + cy).wait_recv()
                d2d(outs, sems, m, j, c).start()

    def finish(ins, outs, sems):
        x, y, c = _me()
        for m in range(M):
            for j in range(3):
                d2d(outs, sems, m, j, 1 - c).wait_recv()
        for m in range(M):
            for j in range(3):
                ici(outs, sems, m, j, 2 * x + y).wait_send()
                d2d(outs, sems, m, j, c).wait_send()

    return _Carry(fulls, [jax.ShapeDtypeStruct(f.shape, f.dtype) for f in fulls], {m: m for m in range(M)},
                  [pltpu.SemaphoreType.DMA((6 * M,)), pltpu.SemaphoreType.DMA((6 * M,))], start, finish, middle)


def _gather_two_way_carry(fulls, axes):
    M = len(fulls)

    def part(outs, m, qq, cc, p):
        Rp = fulls[m].shape[1] // 2
        rows = pl.ds(p * Rp, Rp)
        if axes[m] == 1:
            W = fulls[m].shape[2] // 4
            return outs[m].at[cc, rows, pl.ds(pl.multiple_of(qq * W, LANES), W)]
        return outs[m].at[2 * qq + cc, rows, :]

    def copies(outs, sems):
        x, y, c = _me()
        q, qx, qy, qd = 2 * x + y, 2 * (1 - x) + y, 2 * x + (1 - y), 2 * (1 - x) + (1 - y)
        xn, yn, sib = (1 - x, y, c), (x, 1 - y, c), (x, y, 1 - c)
        table = {}
        for m in range(M):
            def cp(blk, k, dev):
                return _remote(blk, blk, sems[0], sems[1], 12 * m + k, dev)
            own0, own1 = part(outs, m, q, c, 0), part(outs, m, q, c, 1)
            table[m] = dict(
                to=[cp(own0, 0, xn), cp(own1, 1, yn), cp(own1, 2, xn), cp(own0, 3, yn)],
                landed=[cp(part(outs, m, qx, c, 0), 0, xn), cp(part(outs, m, qy, c, 1), 1, yn),
                        cp(part(outs, m, qx, c, 1), 2, xn), cp(part(outs, m, qy, c, 0), 3, yn),
                        cp(part(outs, m, qd, c, 0), 4, yn), cp(part(outs, m, qd, c, 1), 5, xn)],
                passed=[cp(part(outs, m, qx, c, 0), 4, yn), cp(part(outs, m, qy, c, 1), 5, xn)],
                handed=[cp(part(outs, m, qq, c, p), 6 + k, sib)
                        for k, (qq, p) in enumerate([(qx, 0), (qy, 1), (qx, 1), (qy, 0), (qd, 0), (qd, 1)])],
                taken=[cp(part(outs, m, qq, 1 - c, p), 6 + k, sib)
                       for k, (qq, p) in enumerate([(qx, 0), (qy, 1), (qx, 1), (qy, 0), (qd, 0), (qd, 1)])])
        return table

    def start(ins, outs, sems):
        t = copies(outs, sems)
        for m in range(M):
            for cp in t[m]['to']:
                cp.start()

    def finish(ins, outs, sems):
        t = copies(outs, sems)
        for m in range(M):
            for k in range(4):
                t[m]['landed'][k].wait_recv()
                if k < 2:
                    t[m]['passed'][k].start()
                t[m]['handed'][k].start()
        for m in range(M):
            for k in (4, 5):
                t[m]['landed'][k].wait_recv()
                t[m]['handed'][k].start()
        for m in range(M):
            for cp in t[m]['taken']:
                cp.wait_recv()
            for cp in t[m]['to'] + t[m]['passed'] + t[m]['handed']:
                cp.wait_send()

    return _Carry(fulls, [jax.ShapeDtypeStruct(f.shape, f.dtype) for f in fulls], {m: m for m in range(M)},
                  [pltpu.SemaphoreType.DMA((12 * M,)), pltpu.SemaphoreType.DMA((12 * M,))], start, finish)


def _chip_exchange_carry(sums, axes):
    M = len(sums)
    out_shape = []
    for s, ax in zip(sums, axes):
        _, Rh, C = s.shape
        out_shape.append(jax.ShapeDtypeStruct((3, Rh, C // 4 if ax == 1 else C), s.dtype))

    def copies(ins, outs, sems):
        x, y, c = _me()
        cps = []
        for m in range(M):
            for j, (cx, cy) in enumerate(_other_chips(x, y)):
                qj = 2 * cx + cy
                if axes[m] == 1:
                    W = sums[m].shape[2] // 4
                    src = ins[m].at[0, :, pl.ds(pl.multiple_of(qj * W, LANES), W)]
                else:
                    src = ins[m].at[qj]
                cps.append(_remote(src, outs[m].at[j], sems[0], sems[1], 3 * m + j, (cx, cy, c)))
        return cps

    def start(ins, outs, sems):
        for cp in copies(ins, outs, sems):
            cp.start()

    def finish(ins, outs, sems):
        for cp in copies(ins, outs, sems):
            cp.wait()

    return _Carry(sums, out_shape, {}, [pltpu.SemaphoreType.DMA((3 * M,)), pltpu.SemaphoreType.DMA((3 * M,))],
                  start, finish)


def _pair_share_carry(bufs):
    M = len(bufs)

    def start(ins, outs, sems):
        x, y, c = _me()
        for m in range(M):
            _remote(outs[m].at[c], outs[m].at[c], sems[0], sems[1], m, (x, y, 1 - c)).start()

    def finish(ins, outs, sems):
        x, y, c = _me()
        for m in range(M):
            _remote(outs[m].at[c], outs[m].at[c], sems[0], sems[1], m, (x, y, 1 - c)).wait_send()
            _remote(outs[m].at[1 - c], outs[m].at[1 - c], sems[0], sems[1], m, (x, y, 1 - c)).wait_recv()

    return _Carry(bufs, [jax.ShapeDtypeStruct(b.shape, b.dtype) for b in bufs], {m: m for m in range(M)},
                  [pltpu.SemaphoreType.DMA((M,)), pltpu.SemaphoreType.DMA((M,))], start, finish)


def _allreduce_small(v, name):
    R, W = v.shape
    Rh = R // 2

    def body(v_ref, o_ref, sib, quad, send_sems, recv_sems):
        x, y, c = _me()
        q = 2 * x + y
        sibling = (x, y, 1 - c)
        pair = _remote(v_ref, sib, send_sems, recv_sems, 0, sibling)
        pair.start()
        pair.wait()
        mine = pl.ds(pl.multiple_of(c * Rh, SUBLANES), Rh)
        quad[0] = v_ref[mine, :] + sib[mine, :]
        cps = []
        for k in (1, 2, 3):
            peer = (1 - x if k & 2 else x, 1 - y if k & 1 else y, c)
            cps.append(_remote(quad.at[0], quad.at[k], send_sems, recv_sems, k, peer))
            cps[-1].start()
        for cp in cps:
            cp.wait()
        acc = quad[q]
        for p in (1, 2, 3):
            acc = acc + quad[jnp.bitwise_xor(q, p)]
        o_ref[mine, :] = acc
        theirs = pl.ds(pl.multiple_of((1 - c) * Rh, SUBLANES), Rh)
        done = _remote(o_ref.at[mine, :], o_ref.at[mine, :], send_sems, recv_sems, 4, sibling)
        done.start()
        done.wait_send()
        _remote(o_ref.at[theirs, :], o_ref.at[theirs, :], send_sems, recv_sems, 4, sibling).wait_recv()

    vm = pl.BlockSpec(memory_space=pltpu.VMEM)
    return pl.pallas_call(
        body, name=name, in_specs=[vm], out_specs=vm, out_shape=jax.ShapeDtypeStruct((R, W), F32),
        scratch_shapes=[pltpu.VMEM((R, W), F32), pltpu.VMEM((4, Rh, W), F32),
                        pltpu.SemaphoreType.DMA((5,)), pltpu.SemaphoreType.DMA((5,))],
        compiler_params=pltpu.CompilerParams(vmem_limit_bytes=VMEM_LIMIT),
    )(v)


def _pack(pieces):
    flat = []
    for p in pieces:
        p = p.reshape(-1).astype(F32)
        pad = (-p.shape[0]) % PACK_ALIGN
        flat.append(jnp.pad(p, (0, pad)).reshape(-1, PACK_W))
    if sum(f.shape[0] for f in flat) % (2 * SUBLANES):
        flat.append(jnp.zeros((SUBLANES, PACK_W), F32))
    return jnp.concatenate(flat, axis=0)


def _unpack(packed, shapes):
    out, row = [], 0
    for shp in shapes:
        n = math.prod(shp)
        rows = -(-n // PACK_ALIGN) * SUBLANES
        out.append(packed[row:row + rows].reshape(-1)[:n].reshape(shp))
        row += rows
    return out


def kernel(x, ffn1_norm, ffn1_w_gate, ffn1_w_up, ffn1_w_down, mix_norm, w_in, lru_conv_w, lru_conv_b, lru_w_a, lru_b_a, lru_w_i, lru_b_i, lru_lambda, sc_conv_w, lru_out_norm, sc_out_norm, w_out, ffn2_norm, ffn2_w_gate, ffn2_w_up, ffn2_w_down, final_norm, loss_target, m_ffn1_norm, m_ffn1_w_gate, m_ffn1_w_up, m_ffn1_w_down, m_mix_norm, m_w_in, m_lru_conv_w, m_lru_conv_b, m_lru_w_a, m_lru_b_a, m_lru_w_i, m_lru_b_i, m_lru_lambda, m_sc_conv_w, m_lru_out_norm, m_sc_out_norm, m_w_out, m_ffn2_norm, m_ffn2_w_gate, m_ffn2_w_up, m_ffn2_w_down, m_final_norm, v_ffn1_norm, v_ffn1_w_gate, v_ffn1_w_up, v_ffn1_w_down, v_mix_norm, v_w_in, v_lru_conv_w, v_lru_conv_b, v_lru_w_a, v_lru_b_a, v_lru_w_i, v_lru_b_i, v_lru_lambda, v_sc_conv_w, v_lru_out_norm, v_sc_out_norm, v_w_out, v_ffn2_norm, v_ffn2_w_gate, v_ffn2_w_up, v_ffn2_w_down, v_final_norm):
    vals = locals()
    w = {n: vals[n] for n in WEIGHTS}
    mom = {n: vals["m_" + n] for n in WEIGHTS}
    var = {n: vals["v_" + n] for n in WEIGHTS}

    xi, yi, ci = _me()
    qi = 2 * xi + yi
    c_arr = jnp.reshape(ci, (1,)).astype(jnp.int32)
    qc_arr = jnp.stack([qi, ci]).astype(jnp.int32)

    T, D = x.shape[1], x.shape[2]
    xt = x.reshape(T, D)
    target = loss_target.reshape(T, D)
    DL = lru_conv_b.shape[-1]
    NH, HD = lru_w_a.shape[1], lru_w_a.shape[2]
    KL, KS = lru_conv_w.shape[1], sc_conv_w.shape[1]
    DLq = lru_conv_w.shape[2]

    axis_of = dict(zip(BIG, BIG_AXIS))
    placed, full = {}, {}

    def gather(names, pieces={}):
        return _gather_carry([placed[n] for n in names], [axis_of[n] for n in names], [pieces.get(n) for n in names])

    def gathered(names, views, unfinished=()):
        for n, g in zip(names, views):
            if n in unfinished:
                placed[n] = g
            else:
                full[n] = (g.reshape(2 * g.shape[1], g.shape[2]) if axis_of[n] == 1
                           else g.reshape(8 * g.shape[1], g.shape[2]))

    first, rest = 'ffn1_w_gate', [n for n in BIG if n != 'ffn1_w_gate']
    placed[first] = _cast_into_full([w[first][0]], qc_arr, [axis_of[first]], "cast_first_weight")[0]
    res = _cast_into_full([w[n][0] for n in rest], qc_arr, [axis_of[n] for n in rest], "cast_other_weights",
                          _gather_two_way_carry([placed[first]], [axis_of[first]]))
    placed.update(zip(rest, res[:len(rest)]))
    gathered([first], res[len(rest):])

    taps = jnp.zeros((2 * SUBLANES, DL), F32)
    taps = lax.dynamic_update_slice(taps, lru_conv_w[0], (0, qi * DLq))
    taps = lax.dynamic_update_slice(taps, sc_conv_w[0], (KL, qi * DLq))
    taps = _allreduce_small(jnp.where(ci == 0, taps, 0.0), "gather_conv_taps")
    cw, sw = taps[0:KL], taps[KL:KL + KS]

    cb = lru_conv_b
    wa, wi = lru_w_a[0].astype(BF16), lru_w_i[0].astype(BF16)
    ba, bi = lru_b_a.reshape(1, DL), lru_b_i.reshape(1, DL)
    mix_args = (cw, cb, wa, ba, wi, bi, lru_lambda, sw, lru_out_norm, sc_out_norm)
    gf = final_norm.reshape(1, D)

    names = ['ffn1_w_up', 'w_out']
    res = _norm_mm(xt, ffn1_norm, full['ffn1_w_gate'], "ffn1_gate", gather(names))
    n1, G1 = res[:2]
    gathered(names, res[2:])
    names = ['ffn1_w_down', 'ffn2_w_down']
    res = _ffn_up(n1, G1, full['ffn1_w_up'], "ffn1_up", gather(names, {'ffn2_w_down': (0, 2)}))
    U1, H1 = res[:2]
    gathered(names, res[2:], unfinished=['ffn2_w_down'])
    names = ['w_in', 'ffn2_w_down']
    res = _mm_fullk(H1, full['ffn1_w_down'], False, xt, F32, "ffn1_down", gather(names, {'ffn2_w_down': (1, 2)}),
                    FFN_RESIDUAL_SCALE)
    x1 = res[0]
    gathered(names, res[1:])
    names = ['ffn2_w_gate']
    res = _norm_mm(x1, mix_norm, full['w_in'], "mix_in_proj", gather(names))
    n2, z = res[:2]
    gathered(names, res[2:])
    names = ['ffn2_w_up']
    res = _mix_fwd(z, *mix_args, "mix_fwd", gather(names))
    h, ymix = res[:2]
    gathered(names, res[2:])
    x2 = _mm_fullk(ymix, full['w_out'], False, x1, F32, "mix_out_proj")[0]
    n3, G2 = _norm_mm(x2, ffn2_norm, full['ffn2_w_gate'], "ffn2_gate")
    U2, H2 = _ffn_up(n3, G2, full['ffn2_w_up'], "ffn2_up")
    x3 = _mm_fullk(H2, full['ffn2_w_down'], False, x2, F32, "ffn2_down", None, FFN_RESIDUAL_SCALE)[0]
    dx3, d3b, sqerr, dgf = _loss_head(x3, gf, target, "loss_head")

    sums, halves, shared = {}, {}, {}

    def dw(n, a, b, name, carry=None):
        res = _mm_tn_pair(a, b, c_arr, axis_of[n], name, carry)
        sums[n] = res[0]
        return res[2:]

    def chip_carry(names):
        return _chip_exchange_carry([sums[n] for n in names], [axis_of[n] for n in names])

    def chip_add(names, recv):
        for n, r in zip(names, recv):
            halves[n] = _quad_add(sums[n], r, qc_arr, axis_of[n], "grad_chip_add_" + n)

    dG2, dU2 = _ffn_bwd_hidden(d3b, G2, U2, full['ffn2_w_down'], "ffn2_bwd_hidden")
    dn3 = _ffn_bwd_input(dG2, dU2, full['ffn2_w_gate'], full['ffn2_w_up'], "ffn2_bwd_input")[0]
    dx2, dx2b, dg_ffn2 = _rms_bwd_res(dn3, x2, ffn2_norm, dx3, 1.0, "ffn2_norm_bwd")
    dw('ffn2_w_gate', n3, dG2, "ffn2_dwg")
    dw('ffn2_w_up', n3, dU2, "ffn2_dwu")
    dw('ffn2_w_down', H2, d3b, "ffn2_dwd")
    dy = _mm_fullk(dx2b, full['w_out'], True, None, BF16, "mix_out_bwd")[0]
    dz, dwa, dwi, vec = _mix_bwd(z, h, dy, *mix_args, "mix_bwd")
    dg_mix, dx1, d1b = _mm_nt_norm_bwd(dz, full['w_in'], x1, mix_norm, dx2, FFN_RESIDUAL_SCALE, "mix_in_bwd")

    def share_carry(names):
        return _pair_share_carry([halves[n] for n in names])

    res = _ffn_bwd_hidden(d1b, G1, U1, full['ffn1_w_down'], "ffn1_bwd_hidden", chip_carry(['ffn2_w_gate']))
    dG1, dU1 = res[:2]
    chip_add(['ffn2_w_gate'], res[2:])
    res = _ffn_bwd_input(dG1, dU1, full['ffn1_w_gate'], full['ffn1_w_up'], "ffn1_bwd_input",
                         chip_carry(['ffn2_w_up', 'ffn2_w_down']))
    dn1 = res[0]
    chip_add(['ffn2_w_up', 'ffn2_w_down'], res[1:])
    dx0, _, dg_ffn1 = _rms_bwd_res(dn1, xt, ffn1_norm, dx1, 1.0, "ffn1_norm_bwd")
    names = ['ffn2_w_gate', 'ffn2_w_up', 'ffn2_w_down']
    shared.update(zip(names, dw('ffn1_w_gate', n1, dG1, "ffn1_dwg", share_carry(names))))
    chip_add(['ffn1_w_gate'], dw('ffn1_w_up', n1, dU1, "ffn1_dwu", chip_carry(['ffn1_w_gate'])))
    chip_add(['ffn1_w_up'], dw('ffn1_w_down', H1, d1b, "ffn1_dwd", chip_carry(['ffn1_w_up'])))
    names = ['ffn1_w_gate', 'ffn1_w_up']
    res = dw('w_in', n2, dz, "mix_dwin", _merge_carries(chip_carry(['ffn1_w_down']), share_carry(names)))
    chip_add(['ffn1_w_down'], res[:1])
    shared.update(zip(names, res[1:]))
    res = dw('w_out', ymix, dx2b, "mix_dwout", _merge_carries(chip_carry(['w_in']), share_carry(['ffn1_w_down'])))
    chip_add(['w_in'], res[:1])
    shared['ffn1_w_down'] = res[1]
    chip_add(['w_out'], _run_carry(chip_carry(['w_out']), "grad_chip_exchange_w_out"))
    names = ['w_in', 'w_out']
    shared.update(zip(names, _run_carry(share_carry(names), "grad_pair_share")))
    out_g, out_d, out_m, out_v = {}, {}, {}, {}
    for n in BIG:
        shp = w[n].shape
        g = shared[n].reshape(shp[1], shp[2]) if axis_of[n] == 1 else shared[n]
        outs = _adamw(w[n][0], g, mom[n][0], var[n][0], "adamw_" + n)
        out_d[n], out_m[n], out_v[n], out_g[n] = (a.reshape(shp) for a in outs)

    small = [n for n in WEIGHTS if n not in BIG]
    local_small = {
        'ffn1_norm': dg_ffn1, 'mix_norm': dg_mix, 'lru_conv_w': vec[V_CW:V_CW + KL], 'lru_conv_b': vec[V_CB],
        'lru_w_a': dwa, 'lru_b_a': vec[V_BA], 'lru_w_i': dwi, 'lru_b_i': vec[V_BI], 'lru_lambda': vec[V_LAM],
        'sc_conv_w': vec[V_SW:V_SW + KS], 'lru_out_norm': vec[V_GLO], 'sc_out_norm': vec[V_GSO],
        'ffn2_norm': dg_ffn2, 'final_norm': dgf,
    }
    full_shapes = [local_small[n].shape for n in small] + [(1,)]
    reduced = _allreduce_small(_pack([local_small[n] for n in small] + [sqerr[0, 0:1]]), "allreduce_small")
    reduced = _unpack(reduced, full_shapes)
    loss = (0.5 / D) * reduced[-1][0]
    gsm = {}
    for n, g in zip(small, reduced[:-1]):
        if n in SMALL_SHARDED:
            g = lax.dynamic_slice(g, (0, qi * DLq), (g.shape[0], DLq))
        gsm[n] = g.reshape(w[n].shape)
    small_shapes = [w[n].shape for n in small]
    d_s, m_s, v_s, _ = _adamw(_pack([w[n] for n in small]), _pack([gsm[n] for n in small]),
                              _pack([mom[n] for n in small]), _pack([var[n] for n in small]), "adamw_small")
    for n, d, mn, vn in zip(small, _unpack(d_s, small_shapes), _unpack(m_s, small_shapes), _unpack(v_s, small_shapes)):
        out_g[n], out_d[n], out_m[n], out_v[n] = gsm[n], d, mn, vn

    return (loss, dx0.reshape(x.shape), *[out_g[n] for n in WEIGHTS], *[out_d[n] for n in WEIGHTS],
            *[out_m[n] for n in WEIGHTS], *[out_v[n] for n in WEIGHTS])
```

```python
import math

import jax
import jax.numpy as jnp
from jax import lax
from jax.experimental import pallas as pl
from jax.experimental.pallas import tpu as pltpu

F32 = jnp.float32
BF16 = jnp.bfloat16
MESH = pl.DeviceIdType.MESH
ANY = pl.BlockSpec(memory_space=pl.ANY)

NORM_EPS = 1e-6
LRU_C = 8.0
FFN_RESIDUAL_SCALE = 0.5
ADAM_LR = 0.001
ADAM_B1 = 0.9
ADAM_B2 = 0.999
ADAM_EPS = 1e-08
ADAM_WD = 0.01
ADAM_STEP = 10

V7X_VMEM_BYTES = 64 * 2**20
VMEM_LIMIT = V7X_VMEM_BYTES - 8 * 2**20
LANES = 128
SUBLANES = 8
PACK_W = LANES
PACK_ALIGN = SUBLANES * PACK_W

WEIGHTS = ['ffn1_norm', 'ffn1_w_gate', 'ffn1_w_up', 'ffn1_w_down', 'mix_norm', 'w_in', 'lru_conv_w', 'lru_conv_b',
           'lru_w_a', 'lru_b_a', 'lru_w_i', 'lru_b_i', 'lru_lambda', 'sc_conv_w', 'lru_out_norm', 'sc_out_norm',
           'w_out', 'ffn2_norm', 'ffn2_w_gate', 'ffn2_w_up', 'ffn2_w_down', 'final_norm']
BIG = ['ffn1_w_gate', 'ffn1_w_up', 'ffn1_w_down', 'w_in', 'w_out', 'ffn2_w_gate', 'ffn2_w_up', 'ffn2_w_down']
BIG_AXIS = [1, 1, 0, 1, 0, 1, 1, 0]
SMALL_SHARDED = ['lru_conv_w', 'sc_conv_w']


def _tile(n, pref, mult):
    if n <= pref:
        return n
    t = (pref // mult) * mult
    while t >= mult:
        if n % t == 0:
            return t
        t -= mult
    return n


def _params(*sem):
    return pltpu.CompilerParams(dimension_semantics=sem, vmem_limit_bytes=VMEM_LIMIT)


def _me():
    return lax.axis_index("x"), lax.axis_index("y"), lax.axis_index("c")


def _sigmoid(v):
    return 1.0 / (1.0 + jnp.exp(-v))


def _rstd(v):
    return lax.rsqrt(jnp.mean(v * v, axis=-1, keepdims=True) + NORM_EPS)


def _rms_bwd(dy, v, gain):
    r = _rstd(v)
    w = gain * dy
    dv = r * w - v * (r * r * r) * jnp.mean(v * w, axis=-1, keepdims=True)
    dgain = jnp.sum(dy * v * r, axis=0, keepdims=True)
    return dv, dgain


def _dot_nt(a, b):
    return lax.dot_general(a, b, (((1,), (1,)), ((), ())), preferred_element_type=F32)


def _dot_tn(a, b):
    return lax.dot_general(a, b, (((0,), (0,)), ((), ())), preferred_element_type=F32)


class _Carry:
    def __init__(self, inputs, out_shape, aliases, sems, start, finish, middle=None, middle_at=0.85):
        self.inputs, self.out_shape, self.aliases, self.sems = list(inputs), list(out_shape), dict(aliases), list(sems)
        self.start, self.finish = start, finish
        self.middle, self.middle_at = middle, middle_at


def _merge_carries(a, b):
    ia, oa, sa = len(a.inputs), len(a.out_shape), len(a.sems)
    aliases = dict(a.aliases)
    aliases.update({ia + i: oa + j for i, j in b.aliases.items()})

    def both(which):
        def run(ins, outs, sems):
            getattr(a, which)(ins[:ia], outs[:oa], sems[:sa])
            getattr(b, which)(ins[ia:], outs[oa:], sems[sa:])
        return run

    return _Carry(a.inputs + b.inputs, a.out_shape + b.out_shape, aliases, a.sems + b.sems, both("start"), both("finish"))


def _call(body, name, grid, in_specs, out_specs, out_shape, scratch_shapes, semantics, args, carry=None, prefetch=()):
    np_ = len(prefetch)
    if carry is None:
        spec = pltpu.PrefetchScalarGridSpec(num_scalar_prefetch=np_, grid=grid, in_specs=in_specs, out_specs=out_specs,
                                            scratch_shapes=scratch_shapes)
        return pl.pallas_call(body, name=name, grid_spec=spec, out_shape=out_shape,
                              compiler_params=_params(*semantics))(*prefetch, *args)
    ni, no, ns = len(in_specs), len(out_specs), len(scratch_shapes)
    ci, co = len(carry.inputs), len(carry.out_shape)

    def carrying(*refs):
        pre, refs = refs[:np_], refs[np_:]
        ins, refs = refs[:ni], refs[ni:]
        cins, refs = refs[:ci], refs[ci:]
        outs, refs = refs[:no], refs[no:]
        couts, refs = refs[:co], refs[co:]
        scratch, csems = refs[:ns], refs[ns:]
        step = pl.program_id(0)
        for ax in range(1, len(grid)):
            step = step * grid[ax] + pl.program_id(ax)
        steps = math.prod(grid)
        first = step == 0
        last = step == steps - 1

        @pl.when(first)
        def _():
            carry.start(cins, couts, csems)

        if carry.middle is not None:
            @pl.when(step == min(int(carry.middle_at * steps), steps - 1))
            def _():
                carry.middle(cins, couts, csems)

        body(*pre, *ins, *outs, *scratch)

        @pl.when(last)
        def _():
            carry.finish(cins, couts, csems)

    spec = pltpu.PrefetchScalarGridSpec(
        num_scalar_prefetch=np_, grid=grid, in_specs=list(in_specs) + [ANY] * ci,
        out_specs=list(out_specs) + [ANY] * co, scratch_shapes=list(scratch_shapes) + carry.sems)
    return pl.pallas_call(
        carrying, name=name, grid_spec=spec, out_shape=list(out_shape) + carry.out_shape,
        input_output_aliases={np_ + ni + i: no + j for i, j in carry.aliases.items()},
        compiler_params=_params(*(["arbitrary"] * len(grid))),
    )(*prefetch, *args, *carry.inputs)


def _run_carry(carry, name):
    ci, co = len(carry.inputs), len(carry.out_shape)

    def body(*refs):
        cins, couts, csems = refs[:ci], refs[ci:ci + co], refs[ci + co:]
        carry.start(cins, couts, csems)
        if carry.middle is not None:
            carry.middle(cins, couts, csems)
        carry.finish(cins, couts, csems)

    return pl.pallas_call(body, name=name, in_specs=[ANY] * ci, out_specs=[ANY] * co, out_shape=carry.out_shape,
                          input_output_aliases=carry.aliases, scratch_shapes=carry.sems)(*carry.inputs)


def _ffn_bwd_hidden(db, A, B, wd, name, carry=None):
    T, D = db.shape
    FF = wd.shape[0]
    tm = _tile(T, 1024, 16)
    tf = _tile(FF, 512, LANES)

    def body(d_ref, A_ref, B_ref, wd_ref, dG_ref, dU_ref):
        dH = _dot_nt(d_ref[...], wd_ref[...])
        dU_ref[...] = (dH * A_ref[...].astype(F32)).astype(BF16)
        dG_ref[...] = (dH * B_ref[...].astype(F32)).astype(BF16)

    act = pl.BlockSpec((tm, tf), lambda i, f: (i, f))
    return _call(
        body, name, (T // tm, FF // tf),
        [pl.BlockSpec((tm, D), lambda i, f: (i, 0)), act, act, pl.BlockSpec((tf, D), lambda i, f: (f, 0))],
        [act, act], [jax.ShapeDtypeStruct((T, FF), BF16)] * 2,
        [], ("parallel", "arbitrary"), (db, A, B, wd), carry)


def _ffn_bwd_input(dG, dU, wg, wu, name, carry=None):
    T, FF = dG.shape
    D = wg.shape[0]
    tm = _tile(T, 512, 16)
    tn = _tile(D, 512, LANES)

    def body(dG_ref, dU_ref, wg_ref, wu_ref, dn_ref):
        dn_ref[...] = (_dot_nt(dG_ref[...], wg_ref[...]) + _dot_nt(dU_ref[...], wu_ref[...])).astype(BF16)

    act = pl.BlockSpec((tm, FF), lambda i, j: (i, 0))
    wt = pl.BlockSpec((tn, FF), lambda i, j: (j, 0))
    return _call(body, name, (T // tm, D // tn), [act, act, wt, wt], [pl.BlockSpec((tm, tn), lambda i, j: (i, j))],
                 [jax.ShapeDtypeStruct((T, D), BF16)], [], ("parallel", "arbitrary"), (dG, dU, wg, wu), carry)


TAIL_ROWS = 128


class _NormBwdTail:
    def __init__(self, T, D, tm, scale):
        self.T, self.D, self.tm, self.scale = T, D, tm, scale
        self.ni = T // tm
        self.scratch = [pltpu.VMEM((tm, D), F32), pltpu.VMEM((tm, D), F32), pltpu.VMEM((tm, D), F32),
                        pltpu.VMEM((tm, D), BF16), pltpu.SemaphoreType.DMA((4,))]
        self.out_shape = [jax.ShapeDtypeStruct((T, D), F32), jax.ShapeDtypeStruct((T, D), BF16)]

    def _rows(self, k):
        return pl.ds(pl.multiple_of(k * self.tm, self.tm), self.tm)

    def _loads(self, k, x_hbm, r_hbm, bufs):
        xbuf, rbuf, _, _, sems = bufs
        return [pltpu.make_async_copy(x_hbm.at[self._rows(k)], xbuf, sems.at[0]),
                pltpu.make_async_copy(r_hbm.at[self._rows(k)], rbuf, sems.at[1])]

    def _stores(self, k, dx_hbm, dxb_hbm, bufs):
        _, _, obuf, obbuf, sems = bufs
        return [pltpu.make_async_copy(obuf, dx_hbm.at[self._rows(k)], sems.at[2]),
                pltpu.make_async_copy(obbuf, dxb_hbm.at[self._rows(k)], sems.at[3])]

    def prefetch(self, i, x_hbm, r_hbm, bufs):
        for cp in self._loads(i, x_hbm, r_hbm, bufs):
            cp.start()

    def run(self, i, acc_ref, g_ref, x_hbm, r_hbm, dx_hbm, dxb_hbm, dg_ref, bufs):
        xbuf, rbuf, obuf, obbuf, _ = bufs
        for cp in self._loads(i, x_hbm, r_hbm, bufs):
            cp.wait()

        @pl.when(i > 0)
        def _():
            for cp in self._stores(i - 1, dx_hbm, dxb_hbm, bufs):
                cp.wait()

        dgain = None
        for r0 in range(0, self.tm, TAIL_ROWS):
            rs = slice(r0, min(r0 + TAIL_ROWS, self.tm))
            dv, dgr = _rms_bwd(acc_ref[rs, :], xbuf[rs, :], g_ref[...])
            dx = rbuf[rs, :] + dv
            obuf[rs, :] = dx
            obbuf[rs, :] = (self.scale * dx).astype(BF16)
            dgain = dgr if dgain is None else dgain + dgr
        for cp in self._stores(i, dx_hbm, dxb_hbm, bufs):
            cp.start()

        @pl.when(i == 0)
        def _():
            dg_ref[...] = dgain

        @pl.when(i > 0)
        def _():
            dg_ref[...] += dgain

        @pl.when(i == self.ni - 1)
        def _():
            for cp in self._stores(i, dx_hbm, dxb_hbm, bufs):
                cp.wait()


def _mm_nt_norm_bwd(a, w, x_in, gain, dres, scale, name, carry=None):
    T, K = a.shape
    D = w.shape[0]
    whole = D * K * 2 <= 24 * 2**20
    tm = _tile(T, 256 if whole else 512, 16)
    tk = K if whole else _tile(K, 1280, LANES)
    nk = K // tk
    tail = _NormBwdTail(T, D, tm, scale)
    w_spec = (pl.BlockSpec((D, tk), lambda i, k: (0, k), pipeline_mode=pl.Buffered(1)) if whole
              else pl.BlockSpec((D, tk), lambda i, k: (0, k)))

    def body(a_ref, w_ref, g_ref, x_hbm, r_hbm, dg_ref, dx_hbm, dxb_hbm, acc_ref, *bufs):
        i, k = pl.program_id(0), pl.program_id(1)

        @pl.when(k == 0)
        def _():
            tail.prefetch(i, x_hbm, r_hbm, bufs)

        contrib = _dot_nt(a_ref[...], w_ref[...])

        @pl.when(k == 0)
        def _():
            acc_ref[...] = contrib

        @pl.when(k > 0)
        def _():
            acc_ref[...] += contrib

        @pl.when(k == nk - 1)
        def _():
            tail.run(i, acc_ref, g_ref, x_hbm, r_hbm, dx_hbm, dxb_hbm, dg_ref, bufs)

    return _call(
        body, name, (T // tm, nk),
        [pl.BlockSpec((tm, tk), lambda i, k: (i, k)), w_spec, pl.BlockSpec((1, D), lambda i, k: (0, 0)), ANY, ANY],
        [pl.BlockSpec((1, D), lambda i, k: (0, 0)), ANY, ANY],
        [jax.ShapeDtypeStruct((1, D), F32)] + tail.out_shape,
        [pltpu.VMEM((tm, D), F32)] + tail.scratch, ("arbitrary", "arbitrary"), (a, w, gain, x_in, dres), carry)


def _rms_bwd_res(dn, x, gain, dres, scale, name, carry=None):
    T, D = x.shape
    tm = _tile(T, 256, 16)

    def body(dn_ref, x_ref, g_ref, dr_ref, dx_ref, dxb_ref, dg_ref):
        i = pl.program_id(0)
        dv, dgain = _rms_bwd(dn_ref[...].astype(F32), x_ref[...], g_ref[...])
        dx = dr_ref[...] + dv
        dx_ref[...] = dx
        dxb_ref[...] = (scale * dx).astype(BF16)

        @pl.when(i == 0)
        def _():
            dg_ref[...] = dgain

        @pl.when(i > 0)
        def _():
            dg_ref[...] += dgain

    row = pl.BlockSpec((tm, D), lambda i: (i, 0))
    vec = pl.BlockSpec((1, D), lambda i: (0, 0))
    return _call(
        body, name, (T // tm,), [row, row, vec, row], [row, row, vec],
        [jax.ShapeDtypeStruct((T, D), F32), jax.ShapeDtypeStruct((T, D), BF16), jax.ShapeDtypeStruct((1, D), F32)],
        [], ("arbitrary",), (dn, x, gain, dres), carry)


def _loss_head(x3, gain, target, name):
    T, D = x3.shape
    tm = _tile(T, 256, 16)

    def body(x_ref, g_ref, t_ref, dx_ref, dxb_ref, ls_ref, dg_ref):
        i = pl.program_id(0)
        xv = x_ref[...]
        err = xv * _rstd(xv) * g_ref[...] - t_ref[...]
        sq = jnp.sum(jnp.sum(err * err, axis=1, keepdims=True), axis=0, keepdims=True)
        dv, dgain = _rms_bwd(err * (1.0 / D), xv, g_ref[...])
        dx_ref[...] = dv
        dxb_ref[...] = (FFN_RESIDUAL_SCALE * dv).astype(BF16)
        sqb = jnp.broadcast_to(sq, (1, LANES))

        @pl.when(i == 0)
        def _():
            dg_ref[...] = dgain
            ls_ref[...] = sqb

        @pl.when(i > 0)
        def _():
            dg_ref[...] += dgain
            ls_ref[...] += sqb

    row = pl.BlockSpec((tm, D), lambda i: (i, 0))
    vec = pl.BlockSpec((1, D), lambda i: (0, 0))
    return pl.pallas_call(
        body, name=name, grid=(T // tm,),
        in_specs=[row, vec, row],
        out_specs=[row, row, pl.BlockSpec((1, LANES), lambda i: (0, 0)), vec],
        out_shape=[jax.ShapeDtypeStruct((T, D), F32), jax.ShapeDtypeStruct((T, D), BF16),
                   jax.ShapeDtypeStruct((1, LANES), F32), jax.ShapeDtypeStruct((1, D), F32)],
        compiler_params=_params("arbitrary"),
    )(x3, gain, target)


WHOLE_MIN, WHOLE_MAX = 16 * 2**20, 24 * 2**20


def _fits_whole(w):
    return WHOLE_MIN < w.size * 2 <= WHOLE_MAX


def _weight_spec(block, index_map, whole):
    return pl.BlockSpec(block, index_map, pipeline_mode=pl.Buffered(1)) if whole else pl.BlockSpec(block, index_map)


def _norm_mm(x, gain, w, name, carry=None):
    T, D = x.shape
    N = w.shape[1]
    whole = _fits_whole(w)
    tm = _tile(T, 256 if whole else 512, 16)
    tn = N if whole else _tile(N, 2560, LANES)

    def body(x_ref, g_ref, w_ref, n_ref, z_ref):
        @pl.when(pl.program_id(1) == 0)
        def _():
            xv = x_ref[...]
            n_ref[...] = (xv * _rstd(xv) * g_ref[...]).astype(BF16)

        z_ref[...] = jnp.dot(n_ref[...], w_ref[...], preferred_element_type=F32).astype(BF16)

    return _call(
        body, name, (T // tm, N // tn),
        [pl.BlockSpec((tm, D), lambda i, j: (i, 0)),
         pl.BlockSpec((1, D), lambda i, j: (0, 0)),
         _weight_spec((D, tn), lambda i, j: (0, j), whole)],
        [pl.BlockSpec((tm, D), lambda i, j: (i, 0)),
         pl.BlockSpec((tm, tn), lambda i, j: (i, j))],
        [jax.ShapeDtypeStruct((T, D), BF16), jax.ShapeDtypeStruct((T, N), BF16)],
        [], ("parallel", "arbitrary"), (x, gain, w), carry)


def _ffn_up(n, G, wu, name, carry=None):
    T, D = n.shape
    FF = wu.shape[1]
    whole = _fits_whole(wu)
    tm = _tile(T, 128 if whole else 512, 16)
    tf = FF if whole else _tile(FF, 1408, LANES)

    def body(n_ref, G_ref, wu_ref, H_ref, A_ref, B_ref):
        U = jnp.dot(n_ref[...], wu_ref[...], preferred_element_type=F32)
        Gv = G_ref[...].astype(F32)
        s = _sigmoid(Gv)
        sg = Gv * s
        A_ref[...] = sg.astype(BF16)
        B_ref[...] = (U * (s * (1.0 + Gv * (1.0 - s)))).astype(BF16)
        H_ref[...] = (sg * U).astype(BF16)

    act = pl.BlockSpec((tm, tf), lambda i, f: (i, f))
    return _call(
        body, name, (T // tm, FF // tf),
        [pl.BlockSpec((tm, D), lambda i, f: (i, 0)), act, _weight_spec((D, tf), lambda i, f: (0, f), whole)],
        [act, act, act], [jax.ShapeDtypeStruct((T, FF), BF16)] * 3, [], ("parallel", "arbitrary"), (n, G, wu), carry)


def _mm_fullk(a, w, trans_w, residual, out_dtype, name, carry=None, scale=1.0):
    T, K = a.shape
    N = w.shape[0] if trans_w else w.shape[1]
    whole = _fits_whole(w)
    tm = _tile(T, 256 if whole else 512, 16)
    tn = N if whole else _tile(N, 2048 * 2560 // K, LANES)

    def body(*refs):
        if residual is None:
            a_ref, w_ref, o_ref = refs
        else:
            a_ref, w_ref, r_ref, o_ref = refs
        if trans_w:
            acc = _dot_nt(a_ref[...], w_ref[...])
        else:
            acc = jnp.dot(a_ref[...], w_ref[...], preferred_element_type=F32)
        if scale != 1.0:
            acc = scale * acc
        if residual is not None:
            acc = acc + r_ref[...]
        o_ref[...] = acc.astype(out_dtype)

    w_spec = (_weight_spec((tn, K), lambda i, j: (j, 0), whole) if trans_w
              else _weight_spec((K, tn), lambda i, j: (0, j), whole))
    in_specs = [pl.BlockSpec((tm, K), lambda i, j: (i, 0)), w_spec]
    args = [a, w]
    if residual is not None:
        in_specs.append(pl.BlockSpec((tm, tn), lambda i, j: (i, j)))
        args.append(residual)
    return _call(body, name, (T // tm, N // tn), in_specs, [pl.BlockSpec((tm, tn), lambda i, j: (i, j))],
                 [jax.ShapeDtypeStruct((T, N), out_dtype)], [], ("parallel", "arbitrary"), args, carry)


def _mm_tn_pair(a, b, c_arr, axis, name, carry=None):
    T, M = a.shape
    N = b.shape[1]
    tk = _tile(T, 2048, 16)
    nk = T // tk
    nq = 4
    if axis == 1:
        rows, cols = M // 2, N // nq
        a_spec = pl.BlockSpec((tk, rows), lambda p, j, k, cr: (k, jnp.where(p == 0, 1 - cr[0], cr[0])))
        b_spec = pl.BlockSpec((tk, cols), lambda p, j, k, cr: (k, j))
        s_shape, land_shape = (1, rows, N), (rows, N)
        s_spec = pl.BlockSpec((None, rows, cols), lambda p, j, k, cr: (0, 0, j * p))
    else:
        rows, cols = M // nq, N // 2
        a_spec = pl.BlockSpec((tk, rows), lambda p, j, k, cr: (k, j))
        b_spec = pl.BlockSpec((tk, cols), lambda p, j, k, cr: (k, jnp.where(p == 0, 1 - cr[0], cr[0])))
        s_shape, land_shape = (nq, rows, cols), (nq, rows, cols)
        s_spec = pl.BlockSpec((None, rows, cols), lambda p, j, k, cr: (j * p, 0, 0))

    def body(c_ref, a_ref, b_ref, s_ref, land, acc_ref, stage, got, send_sems, recv_sems, loc_sem):
        p, j, k = pl.program_id(0), pl.program_id(1), pl.program_id(2)
        x, y, c = _me()

        def tile(jj):
            return land.at[:, pl.ds(jj * cols, cols)] if axis == 1 else land.at[jj]

        def send(jj):
            return _remote(stage, tile(jj), send_sems, recv_sems, jj, (x, y, 1 - c))

        @pl.when(k == 0)
        def _():
            acc_ref[...] = jnp.zeros_like(acc_ref)

        acc_ref[...] += _dot_tn(a_ref[...], b_ref[...])

        def fetch(jj):
            return pltpu.make_async_copy(tile(jj), got, loc_sem.at[0])

        for jj in range(nq):
            @pl.when(jnp.logical_and(k == nk - 1, jnp.logical_and(p == 0, j == jj)))
            def _():
                if jj > 0:
                    send(jj - 1).wait_send()
                stage[...] = acc_ref[...].astype(BF16)
                send(jj).start()

            @pl.when(jnp.logical_and(k == max(nk - 2, 0), jnp.logical_and(p == 1, j == jj)))
            def _():
                if jj == 0:
                    send(nq - 1).wait_send()
                send(jj).wait_recv()
                fetch(jj).start()

            @pl.when(jnp.logical_and(k == nk - 1, jnp.logical_and(p == 1, j == jj)))
            def _():
                fetch(jj).wait()
                s_ref[...] = (acc_ref[...] + got[...].astype(F32)).astype(BF16)

    return _call(
        body, name, (2, nq, nk), [a_spec, b_spec], [s_spec, ANY],
        [jax.ShapeDtypeStruct(s_shape, BF16), jax.ShapeDtypeStruct(land_shape, BF16)],
        [pltpu.VMEM((rows, cols), F32), pltpu.VMEM((rows, cols), BF16), pltpu.VMEM((rows, cols), BF16),
         pltpu.SemaphoreType.DMA((nq,)), pltpu.SemaphoreType.DMA((nq,)), pltpu.SemaphoreType.DMA((1,))],
        ("arbitrary", "arbitrary", "arbitrary"), (a, b), carry, (c_arr,))


GELU_K = math.sqrt(2.0 / math.pi)
GELU_C = 0.044715


def _gelu_and_grad(v):
    u = GELU_K * (v + GELU_C * v * v * v)
    th = jnp.tanh(u)
    g = 0.5 * v * (1.0 + th)
    dg = 0.5 * (1.0 + th) + 0.5 * v * (1.0 - th * th) * GELU_K * (1.0 + 3.0 * GELU_C * v * v)
    return g, dg


def _neg_expm1(v):
    poly = v * (1.0 + v * (0.5 + v * (1.0 / 6 + v * (1.0 / 24 + v * (1.0 / 120 + v * (1.0 / 720))))))
    return jnp.where(v > -0.25, -poly, 1.0 - jnp.exp(v))


def _softplus_neg(lam):
    e = jnp.exp(-jnp.abs(lam))
    log1pe = jnp.where(e < 1e-4, e * (1.0 - 0.5 * e), jnp.log(1.0 + e))
    sp = jnp.maximum(-lam, 0.0) + log1pe
    dsp = -1.0 / (1.0 + jnp.exp(lam))
    return sp, dsp


def _earlier(ext, j):
    return pltpu.roll(ext, j, 0)[SUBLANES:, :]


def _later(ext, j):
    n = ext.shape[0]
    return pltpu.roll(ext, n - j, 0)[:n - SUBLANES, :]


def _taps(v, halo, K):
    ext = jnp.concatenate([halo, v], axis=0)
    return [v] + [_earlier(ext, j) for j in range(1, K)]


def _block_diag(vb, w_ref, nh, hd):
    return jnp.concatenate(
        [jnp.dot(vb[:, h * hd:(h + 1) * hd], w_ref[h], preferred_element_type=F32) for h in range(nh)], axis=1)


def _lru_gates(xc, wa_ref, ba_ref, wi_ref, bi_ref, sp, nh, hd):
    xcb = xc.astype(BF16)
    r = _sigmoid(_block_diag(xcb, wa_ref, nh, hd) + ba_ref[...])
    ig = _sigmoid(_block_diag(xcb, wi_ref, nh, hd) + bi_ref[...])
    log_a = -LRU_C * r * sp
    a = jnp.exp(log_a)
    mult = jnp.sqrt(_neg_expm1(2.0 * log_a))
    return xcb, r, ig, a, mult


def _mix_fwd(z, cw, cb, wa, ba, wi, bi, lam, sw, glo, gso, name, carry=None):
    T = z.shape[0]
    DL = cb.shape[1]
    DS = gso.shape[1]
    NH, HD = wa.shape[0], wa.shape[1]
    KL, KS = cw.shape[0], sw.shape[0]
    tt = _tile(T, 128, 16)
    o_g, o_b, o_c, o_x = DL, 2 * DL, 2 * DL + DS, 2 * DL + 2 * DS

    def body(z_ref, cw_ref, cb_ref, wa_ref, ba_ref, wi_ref, bi_ref, lam_ref, sw_ref, glo_ref, gso_ref,
             h_ref, y_ref, cx_ref, cp_ref, ch_ref):
        @pl.when(pl.program_id(0) == 0)
        def _():
            cx_ref[...] = jnp.zeros_like(cx_ref)
            cp_ref[...] = jnp.zeros_like(cp_ref)
            ch_ref[...] = jnp.zeros_like(ch_ref)

        def zcol(o, n):
            return z_ref[:, o:o + n].astype(F32)

        lx = zcol(0, DL)
        xs = _taps(lx, cx_ref[...], KL)
        cx_ref[...] = lx[tt - SUBLANES:, :]
        xc = cb_ref[...] + xs[0] * cw_ref[KL - 1:KL, :]
        for j in range(1, KL):
            xc = xc + xs[j] * cw_ref[KL - 1 - j:KL - j, :]
        sp, _ = _softplus_neg(lam_ref[...])
        _, _, ig, a, mult = _lru_gates(xc, wa_ref, ba_ref, wi_ref, bi_ref, sp, NH, HD)
        b = mult * (ig * xc)
        rows = lax.broadcasted_iota(jnp.int32, (tt, DL), 0)
        s = 1
        while s < tt:
            keep = rows >= s
            b = jnp.where(keep, a * pltpu.roll(b, s, 0) + b, b)
            a = jnp.where(keep, a * pltpu.roll(a, s, 0), a)
            s *= 2
        h = a * ch_ref[SUBLANES - 1:SUBLANES, :] + b
        ch_ref[...] = h[tt - SUBLANES:, :]
        h_ref[...] = h
        ge, _ = _gelu_and_grad(zcol(o_g, DL))
        ylru = h * ge
        y_ref[:, 0:DL] = (ylru * _rstd(ylru) * glo_ref[...]).astype(BF16)

        p = zcol(o_c, DS) * zcol(o_x, DS)
        ps = _taps(p, cp_ref[...], KS)
        cp_ref[...] = p[tt - SUBLANES:, :]
        cv = ps[0] * sw_ref[KS - 1:KS, :]
        for j in range(1, KS):
            cv = cv + ps[j] * sw_ref[KS - 1 - j:KS - j, :]
        ysc = zcol(o_b, DS) * cv
        y_ref[:, DL:DL + DS] = (ysc * _rstd(ysc) * gso_ref[...]).astype(BF16)

    def full(shape):
        return pl.BlockSpec(shape, lambda t: (0,) * len(shape))

    return _call(
        body, name, (T // tt,),
        [pl.BlockSpec((tt, z.shape[1]), lambda t: (t, 0)),
         full(cw.shape), full(cb.shape), full(wa.shape), full(ba.shape), full(wi.shape), full(bi.shape),
         full(lam.shape), full(sw.shape), full(glo.shape), full(gso.shape)],
        [pl.BlockSpec((tt, DL), lambda t: (t, 0)), pl.BlockSpec((tt, DL + DS), lambda t: (t, 0))],
        [jax.ShapeDtypeStruct((T, DL), F32), jax.ShapeDtypeStruct((T, DL + DS), BF16)],
        [pltpu.VMEM((SUBLANES, DL), F32), pltpu.VMEM((SUBLANES, DS), F32), pltpu.VMEM((SUBLANES, DL), F32)],
        ("arbitrary",), (z, cw, cb, wa, ba, wi, bi, lam, sw, glo, gso), carry)


V_BA, V_BI, V_LAM, V_CB, V_CW, V_SW, V_GLO, V_GSO, V_ROWS = 0, 1, 2, 3, 4, 8, 11, 12, 16


def _mix_bwd(z, h, dy, cw, cb, wa, ba, wi, bi, lam, sw, glo, gso, name):
    T = z.shape[0]
    DL = cb.shape[1]
    DS = gso.shape[1]
    NH, HD = wa.shape[0], wa.shape[1]
    KL, KS = cw.shape[0], sw.shape[0]
    tt = _tile(T, 64, 16)
    nt = T // tt
    ZH = 2 * SUBLANES
    o_g, o_b, o_c, o_x = DL, 2 * DL, 2 * DL + DS, 2 * DL + 2 * DS

    def body(z_ref, zh_ref, h_ref, hh_ref, dy_ref, cw_ref, cb_ref, wa_ref, ba_ref, wi_ref, bi_ref, lam_ref,
             sw_ref, glo_ref, gso_ref, dz_ref, dwa_ref, dwi_ref, vec_ref, cdx_ref, cdc_ref, cdh_ref):
        i = pl.program_id(0)
        tr = nt - 1 - i

        @pl.when(i == 0)
        def _():
            dwa_ref[...] = jnp.zeros_like(dwa_ref)
            dwi_ref[...] = jnp.zeros_like(dwi_ref)
            vec_ref[...] = jnp.zeros_like(vec_ref)
            cdx_ref[...] = jnp.zeros_like(cdx_ref)
            cdc_ref[...] = jnp.zeros_like(cdc_ref)
            cdh_ref[...] = jnp.zeros_like(cdh_ref)

        def acc_row(r, v):
            vec_ref[pl.ds(r, 1), :] += jnp.sum(v, axis=0, keepdims=True)

        has_prev = tr > 0
        rows = lax.broadcasted_iota(jnp.int32, (tt, DL), 0)

        def zcol(o, n):
            return z_ref[:, o:o + n].astype(F32)

        def zhalo(o, n):
            return jnp.where(has_prev, zh_ref[:, o:o + n].astype(F32)[SUBLANES:, :], 0.0)

        lx = zcol(0, DL)
        xs = _taps(lx, zhalo(0, DL), KL)
        xc = cb_ref[...] + xs[0] * cw_ref[KL - 1:KL, :]
        for j in range(1, KL):
            xc = xc + xs[j] * cw_ref[KL - 1 - j:KL - j, :]
        sp, dsp = _softplus_neg(lam_ref[...])
        xcb, r, ig, a, mult = _lru_gates(xc, wa_ref, ba_ref, wi_ref, bi_ref, sp, NH, HD)
        hv = h_ref[...]
        hprev = _earlier(jnp.concatenate([jnp.where(has_prev, hh_ref[...], 0.0), hv], axis=0), 1)
        gate = zcol(o_g, DL)
        ge, dge = _gelu_and_grad(gate)
        ylru = hv * ge

        d_ylru, dglo = _rms_bwd(dy_ref[:, 0:DL].astype(F32), ylru, glo_ref[...])
        vec_ref[pl.ds(V_GLO, 1), :] += dglo
        dz_ref[:, o_g:o_g + DL] = (d_ylru * hv * dge).astype(BF16)
        bq = d_ylru * ge
        aq = jnp.where(rows == tt - 1, 1.0, pltpu.roll(a, tt - 1, 0))
        s = 1
        while s < tt:
            keep = rows < tt - s
            bq = jnp.where(keep, aq * pltpu.roll(bq, tt - s, 0) + bq, bq)
            aq = jnp.where(keep, aq * pltpu.roll(aq, tt - s, 0), aq)
            s *= 2
        dhh = bq + aq * cdh_ref[0:1, :]
        cdh_ref[0:1, :] = a[0:1, :] * dhh[0:1, :]

        da = dhh * hprev
        dmult = dhh * (ig * xc)
        d_i = dhh * mult * xc
        dxc = dhh * mult * ig
        dlog = da * a - dmult * (a * a) / mult
        acc_row(V_LAM, dlog * (-LRU_C * r) * dsp)
        dpa = dlog * (-LRU_C * sp) * r * (1.0 - r)
        dpi = d_i * ig * (1.0 - ig)
        acc_row(V_BA, dpa)
        acc_row(V_BI, dpi)
        dpab = dpa.astype(BF16)
        dpib = dpi.astype(BF16)
        back = []
        for hh in range(NH):
            sl = slice(hh * HD, (hh + 1) * HD)
            dwa_ref[hh] += _dot_tn(xcb[:, sl], dpab[:, sl])
            dwi_ref[hh] += _dot_tn(xcb[:, sl], dpib[:, sl])
            back.append(_dot_nt(dpab[:, sl], wa_ref[hh]) + _dot_nt(dpib[:, sl], wi_ref[hh]))
        dxc = dxc + jnp.concatenate(back, axis=1)

        acc_row(V_CB, dxc)
        extd = jnp.concatenate([dxc, cdx_ref[...]], axis=0)
        cdx_ref[...] = dxc[0:SUBLANES, :]
        dlx = dxc * cw_ref[KL - 1:KL, :]
        acc_row(V_CW + KL - 1, dxc * xs[0])
        for j in range(1, KL):
            dlx = dlx + _later(extd, j) * cw_ref[KL - 1 - j:KL - j, :]
            acc_row(V_CW + KL - 1 - j, dxc * xs[j])
        dz_ref[:, 0:DL] = dlx.astype(BF16)

        sb = zcol(o_b, DS)
        sc = zcol(o_c, DS)
        sx = zcol(o_x, DS)
        p = sc * sx
        ps = _taps(p, zhalo(o_c, DS) * zhalo(o_x, DS), KS)
        cv = ps[0] * sw_ref[KS - 1:KS, :]
        for j in range(1, KS):
            cv = cv + ps[j] * sw_ref[KS - 1 - j:KS - j, :]
        d_ysc, dgso = _rms_bwd(dy_ref[:, DL:DL + DS].astype(F32), sb * cv, gso_ref[...])
        vec_ref[pl.ds(V_GSO, 1), :] += dgso
        dz_ref[:, o_b:o_b + DS] = (d_ysc * cv).astype(BF16)
        dcv = d_ysc * sb
        extc = jnp.concatenate([dcv, cdc_ref[...]], axis=0)
        cdc_ref[...] = dcv[0:SUBLANES, :]
        dp = dcv * sw_ref[KS - 1:KS, :]
        acc_row(V_SW + KS - 1, dcv * ps[0])
        for j in range(1, KS):
            dp = dp + _later(extc, j) * sw_ref[KS - 1 - j:KS - j, :]
            acc_row(V_SW + KS - 1 - j, dcv * ps[j])
        dz_ref[:, o_c:o_c + DS] = (dp * sx).astype(BF16)
        dz_ref[:, o_x:o_x + DS] = (dp * sc).astype(BF16)

    def full(shape):
        return pl.BlockSpec(shape, lambda t: (0,) * len(shape))

    def rev(t):
        return nt - 1 - t

    def halo(t, rows):
        return jnp.maximum(rev(t) * (tt // rows) - 1, 0)

    return pl.pallas_call(
        body, name=name, grid=(nt,),
        in_specs=[pl.BlockSpec((tt, z.shape[1]), lambda t: (rev(t), 0)),
                  pl.BlockSpec((ZH, z.shape[1]), lambda t: (halo(t, ZH), 0)),
                  pl.BlockSpec((tt, DL), lambda t: (rev(t), 0)),
                  pl.BlockSpec((SUBLANES, DL), lambda t: (halo(t, SUBLANES), 0)),
                  pl.BlockSpec((tt, DL + DS), lambda t: (rev(t), 0)),
                  full(cw.shape), full(cb.shape), full(wa.shape), full(ba.shape), full(wi.shape), full(bi.shape),
                  full(lam.shape), full(sw.shape), full(glo.shape), full(gso.shape)],
        out_specs=[pl.BlockSpec((tt, z.shape[1]), lambda t: (rev(t), 0)),
                   full(wa.shape), full(wi.shape), full((V_ROWS, DL))],
        out_shape=[jax.ShapeDtypeStruct(z.shape, BF16), jax.ShapeDtypeStruct(wa.shape, F32),
                   jax.ShapeDtypeStruct(wi.shape, F32), jax.ShapeDtypeStruct((V_ROWS, DL), F32)],
        scratch_shapes=[pltpu.VMEM((SUBLANES, DL), F32), pltpu.VMEM((SUBLANES, DS), F32),
                        pltpu.VMEM((SUBLANES, DL), F32)],
        compiler_params=_params("arbitrary"),
    )(z, z, h, h, dy, cw, cb, wa, ba, wi, bi, lam, sw, glo, gso)


def _quad_add(s, r2, qc, axis, name):
    _, R, W = r2.shape
    tr = _tile(R, 256, 16)

    def body(qc_ref, s_ref, r0_ref, r1_ref, r2_ref, o_ref):
        o_ref[...] = ((s_ref[...].astype(F32) + r0_ref[...].astype(F32)) + r1_ref[...].astype(F32)) + r2_ref[...].astype(F32)

    blk = (None, tr, W)
    if axis == 1:
        own = pl.BlockSpec(blk, lambda i, qr: (0, i, qr[0]))
    else:
        own = pl.BlockSpec(blk, lambda i, qr: (qr[0], i, 0))
    return pl.pallas_call(
        body, name=name,
        grid_spec=pltpu.PrefetchScalarGridSpec(
            num_scalar_prefetch=1, grid=(R // tr,),
            in_specs=[own] + [pl.BlockSpec(blk, lambda i, qr, j=j: (j, i, 0)) for j in range(3)],
            out_specs=pl.BlockSpec(blk, lambda i, qr: (qr[1], i, 0))),
        out_shape=jax.ShapeDtypeStruct((2, R, W), F32),
        compiler_params=_params("parallel"),
    )(qc, s, r2, r2, r2)


CAST_BLOCKS = 4


def _cast_into_full(shards, qc, axes, name, carry=None):
    M = len(shards)
    nb = CAST_BLOCKS

    def body(qc_ref, *refs):
        for s_ref, o_ref in zip(refs[:M], refs[M:]):
            o_ref[...] = s_ref[...].astype(BF16)

    in_specs, out_specs, out_shape = [], [], []
    for s, ax in zip(shards, axes):
        R, W = s.shape
        Rh = R // 2
        tr = Rh // nb
        in_specs.append(pl.BlockSpec((tr, W), lambda hf, i, qr: (hf * nb + i, 0)))
        if ax == 1:
            out_shape.append(jax.ShapeDtypeStruct((2, Rh, 4 * W), BF16))
            out_specs.append(pl.BlockSpec((None, tr, W), lambda hf, i, qr: (hf, i, qr[0])))
        else:
            out_shape.append(jax.ShapeDtypeStruct((8, Rh, W), BF16))
            out_specs.append(pl.BlockSpec((None, tr, W), lambda hf, i, qr: (2 * qr[0] + hf, i, 0)))
    return _call(body, name, (2, nb), in_specs, out_specs, out_shape, [], ("parallel", "parallel"), shards, carry, (qc,))


def _adamw(w, g, m, v, name):
    R, C = w.shape
    tr = _tile(R, 256, SUBLANES)
    tc = C // 2 if g.ndim == 3 else _tile(C, 2048, LANES)
    c1 = 1.0 - ADAM_B1 ** ADAM_STEP
    c2 = 1.0 - ADAM_B2 ** ADAM_STEP

    def body(w_ref, g_ref, m_ref, v_ref, d_ref, mo_ref, vo_ref, go_ref):
        gv = g_ref[...]
        go_ref[...] = gv
        mn = ADAM_B1 * m_ref[...] + (1.0 - ADAM_B1) * gv
        vn = ADAM_B2 * v_ref[...] + (1.0 - ADAM_B2) * (gv * gv)
        mo_ref[...] = mn
        vo_ref[...] = vn
        d_ref[...] = -ADAM_LR * ((mn / c1) / (jnp.sqrt(vn / c2) + ADAM_EPS) + ADAM_WD * w_ref[...])

    blk = pl.BlockSpec((tr, tc), lambda i, j: (i, j))
    g_blk = pl.BlockSpec((None, tr, tc), lambda i, j: (j, i, 0)) if g.ndim == 3 else blk
    sh = jax.ShapeDtypeStruct((R, C), F32)
    return pl.pallas_call(
        body, name=name, grid=(R // tr, C // tc),
        in_specs=[blk, g_blk, blk, blk], out_specs=[blk] * 4, out_shape=[sh] * 4,
        compiler_params=_params("parallel", "parallel"),
    )(w, g, m, v)


def _other_chips(x, y):
    return [(1 - x, y), (x, 1 - y), (1 - x, 1 - y)]


def _remote(src, dst, send_sems, recv_sems, idx, dev):
    return pltpu.make_async_remote_copy(src_ref=src, dst_ref=dst, send_sem=send_sems.at[idx], recv_sem=recv_sems.at[idx],
                                        device_id=dev, device_id_type=MESH)


def _gather_carry(fulls, axes, pieces=None):
    M = len(fulls)

    def win(outs, m, qq, cc):
        Rh = fulls[m].shape[1]
        k, n = pieces[m] if pieces is not None and pieces[m] is not None else (0, 1)
        rows = pl.ds(k * (Rh // n), Rh // n)
        if axes[m] == 1:
            W = fulls[m].shape[2] // 4
            return outs[m].at[cc, rows, pl.ds(pl.multiple_of(qq * W, LANES), W)]
        return outs[m].at[2 * qq + cc, rows, :]

    def ici(outs, sems, m, j, src_q):
        x, y, c = _me()
        cx, cy = _other_chips(x, y)[j]
        blk = win(outs, m, src_q, c)
        return _remote(blk, blk, sems[0], sems[1], 6 * m + j, (cx, cy, c))

    def d2d(outs, sems, m, j, half):
        x, y, c = _me()
        cx, cy = _other_chips(x, y)[j]
        blk = win(outs, m, 2 * cx + cy, half)
        return _remote(blk, blk, sems[0], sems[1], 6 * m + 3 + j, (x, y, 1 - c))

    def start(ins, outs, sems):
        x, y, c = _me()
        for m in range(M):
            for j in range(3):
                ici(outs, sems, m, j, 2 * x + y).start()

    def middle(ins, outs, sems):
        x, y, c = _me()
        for m in range(M):
            for j, (cx, cy) in enumerate(_other_chips(x, y)):
                ici(outs, sems, m, j, 2 * cx + cy).wait_recv()
                d2d(outs, sems, m, j, c).start()

    def finish(ins, outs, sems):
        x, y, c = _me()
        for m in range(M):
            for j in range(3):
                d2d(outs, sems, m, j, 1 - c).wait_recv()
        for m in range(M):
            for j in range(3):
                ici(outs, sems, m, j, 2 * x + y).wait_send()
                d2d(outs, sems, m, j, c).wait_send()

    return _Carry(fulls, [jax.ShapeDtypeStruct(f.shape, f.dtype) for f in fulls], {m: m for m in range(M)},
                  [pltpu.SemaphoreType.DMA((6 * M,)), pltpu.SemaphoreType.DMA((6 * M,))], start, finish, middle)


def _gather_two_way_carry(fulls, axes):
    M = len(fulls)

    def part(outs, m, qq, cc, p):
        Rp = fulls[m].shape[1] // 2
        rows = pl.ds(p * Rp, Rp)
        if axes[m] == 1:
            W = fulls[m].shape[2] // 4
            return outs[m].at[cc, rows, pl.ds(pl.multiple_of(qq * W, LANES), W)]
        return outs[m].at[2 * qq + cc, rows, :]

    def copies(outs, sems):
        x, y, c = _me()
        q, qx, qy, qd = 2 * x + y, 2 * (1 - x) + y, 2 * x + (1 - y), 2 * (1 - x) + (1 - y)
        xn, yn, sib = (1 - x, y, c), (x, 1 - y, c), (x, y, 1 - c)
        table = {}
        for m in range(M):
            def cp(blk, k, dev):
                return _remote(blk, blk, sems[0], sems[1], 12 * m + k, dev)
            own0, own1 = part(outs, m, q, c, 0), part(outs, m, q, c, 1)
            table[m] = dict(
                to=[cp(own0, 0, xn), cp(own1, 1, yn), cp(own1, 2, xn), cp(own0, 3, yn)],
                landed=[cp(part(outs, m, qx, c, 0), 0, xn), cp(part(outs, m, qy, c, 1), 1, yn),
                        cp(part(outs, m, qx, c, 1), 2, xn), cp(part(outs, m, qy, c, 0), 3, yn),
                        cp(part(outs, m, qd, c, 0), 4, yn), cp(part(outs, m, qd, c, 1), 5, xn)],
                passed=[cp(part(outs, m, qx, c, 0), 4, yn), cp(part(outs, m, qy, c, 1), 5, xn)],
                handed=[cp(part(outs, m, qq, c, p), 6 + k, sib)
                        for k, (qq, p) in enumerate([(qx, 0), (qy, 1), (qx, 1), (qy, 0), (qd, 0), (qd, 1)])],
                taken=[cp(part(outs, m, qq, 1 - c, p), 6 + k, sib)
                       for k, (qq, p) in enumerate([(qx, 0), (qy, 1), (qx, 1), (qy, 0), (qd, 0), (qd, 1)])])
        return table

    def start(ins, outs, sems):
        t = copies(outs, sems)
        for m in range(M):
            for cp in t[m]['to']:
                cp.start()

    def finish(ins, outs, sems):
        t = copies(outs, sems)
        for m in range(M):
            for k in range(4):
                t[m]['landed'][k].wait_recv()
                if k < 2:
                    t[m]['passed'][k].start()
                t[m]['handed'][k].start()
        for m in range(M):
            for k in (4, 5):
                t[m]['landed'][k].wait_recv()
                t[m]['handed'][k].start()
        for m in range(M):
            for cp in t[m]['taken']:
                cp.wait_recv()
            for cp in t[m]['to'] + t[m]['passed'] + t[m]['handed']:
                cp.wait_send()

    return _Carry(fulls, [jax.ShapeDtypeStruct(f.shape, f.dtype) for f in fulls], {m: m for m in range(M)},
                  [pltpu.SemaphoreType.DMA((12 * M,)), pltpu.SemaphoreType.DMA((12 * M,))], start, finish)


def _chip_exchange_carry(sums, axes):
    M = len(sums)
    out_shape = []
    for s, ax in zip(sums, axes):
        _, Rh, C = s.shape
        out_shape.append(jax.ShapeDtypeStruct((3, Rh, C // 4 if ax == 1 else C), s.dtype))

    def copies(ins, outs, sems):
        x, y, c = _me()
        cps = []
        for m in range(M):
            for j, (cx, cy) in enumerate(_other_chips(x, y)):
                qj = 2 * cx + cy
                if axes[m] == 1:
                    W = sums[m].shape[2] // 4
                    src = ins[m].at[0, :, pl.ds(pl.multiple_of(qj * W, LANES), W)]
                else:
                    src = ins[m].at[qj]
                cps.append(_remote(src, outs[m].at[j], sems[0], sems[1], 3 * m + j, (cx, cy, c)))
        return cps

    def start(ins, outs, sems):
        for cp in copies(ins, outs, sems):
            cp.start()

    def finish(ins, outs, sems):
        for cp in copies(ins, outs, sems):
            cp.wait()

    return _Carry(sums, out_shape, {}, [pltpu.SemaphoreType.DMA((3 * M,)), pltpu.SemaphoreType.DMA((3 * M,))],
                  start, finish)


def _pair_share_carry(bufs):
    M = len(bufs)

    def start(ins, outs, sems):
        x, y, c = _me()
        for m in range(M):
            _remote(outs[m].at[c], outs[m].at[c], sems[0], sems[1], m, (x, y, 1 - c)).start()

    def finish(ins, outs, sems):
        x, y, c = _me()
        for m in range(M):
            _remote(outs[m].at[c], outs[m].at[c], sems[0], sems[1], m, (x, y, 1 - c)).wait_send()
            _remote(outs[m].at[1 - c], outs[m].at[1 - c], sems[0], sems[1], m, (x, y, 1 - c)).wait_recv()

    return _Carry(bufs, [jax.ShapeDtypeStruct(b.shape, b.dtype) for b in bufs], {m: m for m in range(M)},
                  [pltpu.SemaphoreType.DMA((M,)), pltpu.SemaphoreType.DMA((M,))], start, finish)


def _allreduce_small(v, name):
    R, W = v.shape
    Rh = R // 2

    def body(v_ref, o_ref, sib, quad, send_sems, recv_sems):
        x, y, c = _me()
        q = 2 * x + y
        sibling = (x, y, 1 - c)
        pair = _remote(v_ref, sib, send_sems, recv_sems, 0, sibling)
        pair.start()
        pair.wait()
        mine = pl.ds(pl.multiple_of(c * Rh, SUBLANES), Rh)
        quad[0] = v_ref[mine, :] + sib[mine, :]
        cps = []
        for k in (1, 2, 3):
            peer = (1 - x if k & 2 else x, 1 - y if k & 1 else y, c)
            cps.append(_remote(quad.at[0], quad.at[k], send_sems, recv_sems, k, peer))
            cps[-1].start()
        for cp in cps:
            cp.wait()
        acc = quad[q]
        for p in (1, 2, 3):
            acc = acc + quad[jnp.bitwise_xor(q, p)]
        o_ref[mine, :] = acc
        theirs = pl.ds(pl.multiple_of((1 - c) * Rh, SUBLANES), Rh)
        done = _remote(o_ref.at[mine, :], o_ref.at[mine, :], send_sems, recv_sems, 4, sibling)
        done.start()
        done.wait_send()
        _remote(o_ref.at[theirs, :], o_ref.at[theirs, :], send_sems, recv_sems, 4, sibling).wait_recv()

    vm = pl.BlockSpec(memory_space=pltpu.VMEM)
    return pl.pallas_call(
        body, name=name, in_specs=[vm], out_specs=vm, out_shape=jax.ShapeDtypeStruct((R, W), F32),
        scratch_shapes=[pltpu.VMEM((R, W), F32), pltpu.VMEM((4, Rh, W), F32),
                        pltpu.SemaphoreType.DMA((5,)), pltpu.SemaphoreType.DMA((5,))],
        compiler_params=pltpu.CompilerParams(vmem_limit_bytes=VMEM_LIMIT),
    )(v)


def _pack(pieces):
    flat = []
    for p in pieces:
        p = p.reshape(-1).astype(F32)
        pad = (-p.shape[0]) % PACK_ALIGN
        flat.append(jnp.pad(p, (0, pad)).reshape(-1, PACK_W))
    if sum(f.shape[0] for f in flat) % (2 * SUBLANES):
        flat.append(jnp.zeros((SUBLANES, PACK_W), F32))
    return jnp.concatenate(flat, axis=0)


def _unpack(packed, shapes):
    out, row = [], 0
    for shp in shapes:
        n = math.prod(shp)
        rows = -(-n // PACK_ALIGN) * SUBLANES
        out.append(packed[row:row + rows].reshape(-1)[:n].reshape(shp))
        row += rows
    return out


def kernel(x, ffn1_norm, ffn1_w_gate, ffn1_w_up, ffn1_w_down, mix_norm, w_in, lru_conv_w, lru_conv_b, lru_w_a, lru_b_a, lru_w_i, lru_b_i, lru_lambda, sc_conv_w, lru_out_norm, sc_out_norm, w_out, ffn2_norm, ffn2_w_gate, ffn2_w_up, ffn2_w_down, final_norm, loss_target, m_ffn1_norm, m_ffn1_w_gate, m_ffn1_w_up, m_ffn1_w_down, m_mix_norm, m_w_in, m_lru_conv_w, m_lru_conv_b, m_lru_w_a, m_lru_b_a, m_lru_w_i, m_lru_b_i, m_lru_lambda, m_sc_conv_w, m_lru_out_norm, m_sc_out_norm, m_w_out, m_ffn2_norm, m_ffn2_w_gate, m_ffn2_w_up, m_ffn2_w_down, m_final_norm, v_ffn1_norm, v_ffn1_w_gate, v_ffn1_w_up, v_ffn1_w_down, v_mix_norm, v_w_in, v_lru_conv_w, v_lru_conv_b, v_lru_w_a, v_lru_b_a, v_lru_w_i, v_lru_b_i, v_lru_lambda, v_sc_conv_w, v_lru_out_norm, v_sc_out_norm, v_w_out, v_ffn2_norm, v_ffn2_w_gate, v_ffn2_w_up, v_ffn2_w_down, v_final_norm):
    vals = locals()
    w = {n: vals[n] for n in WEIGHTS}
    mom = {n: vals["m_" + n] for n in WEIGHTS}
    var = {n: vals["v_" + n] for n in WEIGHTS}

    xi, yi, ci = _me()
    qi = 2 * xi + yi
    c_arr = jnp.reshape(ci, (1,)).astype(jnp.int32)
    qc_arr = jnp.stack([qi, ci]).astype(jnp.int32)

    T, D = x.shape[1], x.shape[2]
    xt = x.reshape(T, D)
    target = loss_target.reshape(T, D)
    DL = lru_conv_b.shape[-1]
    NH, HD = lru_w_a.shape[1], lru_w_a.shape[2]
    KL, KS = lru_conv_w.shape[1], sc_conv_w.shape[1]
    DLq = lru_conv_w.shape[2]

    axis_of = dict(zip(BIG, BIG_AXIS))
    placed, full = {}, {}

    def gather(names, pieces={}):
        return _gather_carry([placed[n] for n in names], [axis_of[n] for n in names], [pieces.get(n) for n in names])

    def gathered(names, views, unfinished=()):
        for n, g in zip(names, views):
            if n in unfinished:
                placed[n] = g
            else:
                full[n] = (g.reshape(2 * g.shape[1], g.shape[2]) if axis_of[n] == 1
                           else g.reshape(8 * g.shape[1], g.shape[2]))

    first, rest = 'ffn1_w_gate', [n for n in BIG if n != 'ffn1_w_gate']
    placed[first] = _cast_into_full([w[first][0]], qc_arr, [axis_of[first]], "cast_first_weight")[0]
    res = _cast_into_full([w[n][0] for n in rest], qc_arr, [axis_of[n] for n in rest], "cast_other_weights",
                          _gather_two_way_carry([placed[first]], [axis_of[first]]))
    placed.update(zip(rest, res[:len(rest)]))
    gathered([first], res[len(rest):])

    taps = jnp.zeros((2 * SUBLANES, DL), F32)
    taps = lax.dynamic_update_slice(taps, lru_conv_w[0], (0, qi * DLq))
    taps = lax.dynamic_update_slice(taps, sc_conv_w[0], (KL, qi * DLq))
    taps = _allreduce_small(jnp.where(ci == 0, taps, 0.0), "gather_conv_taps")
    cw, sw = taps[0:KL], taps[KL:KL + KS]

    cb = lru_conv_b
    wa, wi = lru_w_a[0].astype(BF16), lru_w_i[0].astype(BF16)
    ba, bi = lru_b_a.reshape(1, DL), lru_b_i.reshape(1, DL)
    mix_args = (cw, cb, wa, ba, wi, bi, lru_lambda, sw, lru_out_norm, sc_out_norm)
    gf = final_norm.reshape(1, D)

    names = ['ffn1_w_up', 'w_out']
    res = _norm_mm(xt, ffn1_norm, full['ffn1_w_gate'], "ffn1_gate", gather(names))
    n1, G1 = res[:2]
    gathered(names, res[2:])
    names = ['ffn1_w_down', 'ffn2_w_down']
    res = _ffn_up(n1, G1, full['ffn1_w_up'], "ffn1_up", gather(names, {'ffn2_w_down': (0, 2)}))
    H1, A1, B1 = res[:3]
    gathered(names, res[3:], unfinished=['ffn2_w_down'])
    names = ['w_in', 'ffn2_w_down']
    res = _mm_fullk(H1, full['ffn1_w_down'], False, xt, F32, "ffn1_down", gather(names, {'ffn2_w_down': (1, 2)}),
                    FFN_RESIDUAL_SCALE)
    x1 = res[0]
    gathered(names, res[1:])
    names = ['ffn2_w_gate']
    res = _norm_mm(x1, mix_norm, full['w_in'], "mix_in_proj", gather(names))
    n2, z = res[:2]
    gathered(names, res[2:])
    names = ['ffn2_w_up']
    res = _mix_fwd(z, *mix_args, "mix_fwd", gather(names))
    h, ymix = res[:2]
    gathered(names, res[2:])
    x2 = _mm_fullk(ymix, full['w_out'], False, x1, F32, "mix_out_proj")[0]
    n3, G2 = _norm_mm(x2, ffn2_norm, full['ffn2_w_gate'], "ffn2_gate")
    H2, A2, B2 = _ffn_up(n3, G2, full['ffn2_w_up'], "ffn2_up")
    x3 = _mm_fullk(H2, full['ffn2_w_down'], False, x2, F32, "ffn2_down", None, FFN_RESIDUAL_SCALE)[0]
    dx3, d3b, sqerr, dgf = _loss_head(x3, gf, target, "loss_head")

    sums, halves, shared = {}, {}, {}

    def dw(n, a, b, name, carry=None):
        res = _mm_tn_pair(a, b, c_arr, axis_of[n], name, carry)
        sums[n] = res[0]
        return res[2:]

    def chip_carry(names):
        return _chip_exchange_carry([sums[n] for n in names], [axis_of[n] for n in names])

    def chip_add(names, recv):
        for n, r in zip(names, recv):
            halves[n] = _quad_add(sums[n], r, qc_arr, axis_of[n], "grad_chip_add_" + n)

    dG2, dU2 = _ffn_bwd_hidden(d3b, A2, B2, full['ffn2_w_down'], "ffn2_bwd_hidden")
    dn3 = _ffn_bwd_input(dG2, dU2, full['ffn2_w_gate'], full['ffn2_w_up'], "ffn2_bwd_input")[0]
    dx2, dx2b, dg_ffn2 = _rms_bwd_res(dn3, x2, ffn2_norm, dx3, 1.0, "ffn2_norm_bwd")
    dw('ffn2_w_gate', n3, dG2, "ffn2_dwg")
    dw('ffn2_w_up', n3, dU2, "ffn2_dwu")
    dw('ffn2_w_down', H2, d3b, "ffn2_dwd")
    dy = _mm_fullk(dx2b, full['w_out'], True, None, BF16, "mix_out_bwd")[0]
    dz, dwa, dwi, vec = _mix_bwd(z, h, dy, *mix_args, "mix_bwd")
    dg_mix, dx1, d1b = _mm_nt_norm_bwd(dz, full['w_in'], x1, mix_norm, dx2, FFN_RESIDUAL_SCALE, "mix_in_bwd")

    def share_carry(names):
        return _pair_share_carry([halves[n] for n in names])

    res = _ffn_bwd_hidden(d1b, A1, B1, full['ffn1_w_down'], "ffn1_bwd_hidden", chip_carry(['ffn2_w_gate']))
    dG1, dU1 = res[:2]
    chip_add(['ffn2_w_gate'], res[2:])
    res = _ffn_bwd_input(dG1, dU1, full['ffn1_w_gate'], full['ffn1_w_up'], "ffn1_bwd_input",
                         chip_carry(['ffn2_w_up', 'ffn2_w_down']))
    dn1 = res[0]
    chip_add(['ffn2_w_up', 'ffn2_w_down'], res[1:])
    dx0, _, dg_ffn1 = _rms_bwd_res(dn1, xt, ffn1_norm, dx1, 1.0, "ffn1_norm_bwd")
    names = ['ffn2_w_gate', 'ffn2_w_up', 'ffn2_w_down']
    shared.update(zip(names, dw('ffn1_w_gate', n1, dG1, "ffn1_dwg", share_carry(names))))
    chip_add(['ffn1_w_gate'], dw('ffn1_w_up', n1, dU1, "ffn1_dwu", chip_carry(['ffn1_w_gate'])))
    chip_add(['ffn1_w_up'], dw('ffn1_w_down', H1, d1b, "ffn1_dwd", chip_carry(['ffn1_w_up'])))
    names = ['ffn1_w_gate', 'ffn1_w_up']
    res = dw('w_in', n2, dz, "mix_dwin", _merge_carries(chip_carry(['ffn1_w_down']), share_carry(names)))
    chip_add(['ffn1_w_down'], res[:1])
    shared.update(zip(names, res[1:]))
    res = dw('w_out', ymix, dx2b, "mix_dwout", _merge_carries(chip_carry(['w_in']), share_carry(['ffn1_w_down'])))
    chip_add(['w_in'], res[:1])
    shared['ffn1_w_down'] = res[1]
    chip_add(['w_out'], _run_carry(chip_carry(['w_out']), "grad_chip_exchange_w_out"))
    names = ['w_in', 'w_out']
    shared.update(zip(names, _run_carry(share_carry(names), "grad_pair_share")))
    out_g, out_d, out_m, out_v = {}, {}, {}, {}
    for n in BIG:
        shp = w[n].shape
        g = shared[n].reshape(shp[1], shp[2]) if axis_of[n] == 1 else shared[n]
        outs = _adamw(w[n][0], g, mom[n][0], var[n][0], "adamw_" + n)
        out_d[n], out_m[n], out_v[n], out_g[n] = (a.reshape(shp) for a in outs)

    small = [n for n in WEIGHTS if n not in BIG]
    local_small = {
        'ffn1_norm': dg_ffn1, 'mix_norm': dg_mix, 'lru_conv_w': vec[V_CW:V_CW + KL], 'lru_conv_b': vec[V_CB],
        'lru_w_a': dwa, 'lru_b_a': vec[V_BA], 'lru_w_i': dwi, 'lru_b_i': vec[V_BI], 'lru_lambda': vec[V_LAM],
        'sc_conv_w': vec[V_SW:V_SW + KS], 'lru_out_norm': vec[V_GLO], 'sc_out_norm': vec[V_GSO],
        'ffn2_norm': dg_ffn2, 'final_norm': dgf,
    }
    full_shapes = [local_small[n].shape for n in small] + [(1,)]
    reduced = _allreduce_small(_pack([local_small[n] for n in small] + [sqerr[0, 0:1]]), "allreduce_small")
    reduced = _unpack(reduced, full_shapes)
    loss = (0.5 / D) * reduced[-1][0]
    gsm = {}
    for n, g in zip(small, reduced[:-1]):
        if n in SMALL_SHARDED:
            g = lax.dynamic_slice(g, (0, qi * DLq), (g.shape[0], DLq))
        gsm[n] = g.reshape(w[n].shape)
    small_shapes = [w[n].shape for n in small]
    d_s, m_s, v_s, _ = _adamw(_pack([w[n] for n in small]), _pack([gsm[n] for n in small]),
                              _pack([mom[n] for n in small]), _pack([var[n] for n in small]), "adamw_small")
    for n, d, mn, vn in zip(small, _unpack(d_s, small_shapes), _unpack(m_s, small_shapes), _unpack(v_s, small_shapes)):
        out_g[n], out_d[n], out_m[n], out_v[n] = gsm[n], d, mn, vn

    return (loss, dx0.reshape(x.shape), *[out_g[n] for n in WEIGHTS], *[out_d[n] for n in WEIGHTS],
            *[out_m[n] for n in WEIGHTS], *[out_v[n] for n in WEIGHTS])
```

```python
import math

import jax
import jax.numpy as jnp
from jax import lax
from jax.experimental import pallas as pl
from jax.experimental.pallas import tpu as pltpu

F32 = jnp.float32
BF16 = jnp.bfloat16
MESH = pl.DeviceIdType.MESH
ANY = pl.BlockSpec(memory_space=pl.ANY)

NORM_EPS = 1e-6
LRU_C = 8.0
FFN_RESIDUAL_SCALE = 0.5
ADAM_LR = 0.001
ADAM_B1 = 0.9
ADAM_B2 = 0.999
ADAM_EPS = 1e-08
ADAM_WD = 0.01
ADAM_STEP = 10

V7X_VMEM_BYTES = 64 * 2**20
VMEM_LIMIT = V7X_VMEM_BYTES - 8 * 2**20
LANES = 128
SUBLANES = 8
PACK_W = LANES
PACK_ALIGN = SUBLANES * PACK_W

WEIGHTS = ['ffn1_norm', 'ffn1_w_gate', 'ffn1_w_up', 'ffn1_w_down', 'mix_norm', 'w_in', 'lru_conv_w', 'lru_conv_b',
           'lru_w_a', 'lru_b_a', 'lru_w_i', 'lru_b_i', 'lru_lambda', 'sc_conv_w', 'lru_out_norm', 'sc_out_norm',
           'w_out', 'ffn2_norm', 'ffn2_w_gate', 'ffn2_w_up', 'ffn2_w_down', 'final_norm']
BIG = ['ffn1_w_gate', 'ffn1_w_up', 'ffn1_w_down', 'w_in', 'w_out', 'ffn2_w_gate', 'ffn2_w_up', 'ffn2_w_down']
BIG_AXIS = [1, 1, 0, 1, 0, 1, 1, 0]
SMALL_SHARDED = ['lru_conv_w', 'sc_conv_w']


def _tile(n, pref, mult):
    if n <= pref:
        return n
    t = (pref // mult) * mult
    while t >= mult:
        if n % t == 0:
            return t
        t -= mult
    return n


def _params(*sem):
    return pltpu.CompilerParams(dimension_semantics=sem, vmem_limit_bytes=VMEM_LIMIT)


def _me():
    return lax.axis_index("x"), lax.axis_index("y"), lax.axis_index("c")


def _sigmoid(v):
    return 1.0 / (1.0 + jnp.exp(-v))


def _rstd(v):
    return lax.rsqrt(jnp.mean(v * v, axis=-1, keepdims=True) + NORM_EPS)


def _rms_bwd(dy, v, gain):
    r = _rstd(v)
    w = gain * dy
    dv = r * w - v * (r * r * r) * jnp.mean(v * w, axis=-1, keepdims=True)
    dgain = jnp.sum(dy * v * r, axis=0, keepdims=True)
    return dv, dgain


def _dot_nt(a, b):
    return lax.dot_general(a, b, (((1,), (1,)), ((), ())), preferred_element_type=F32)


def _dot_tn(a, b):
    return lax.dot_general(a, b, (((0,), (0,)), ((), ())), preferred_element_type=F32)


class _Carry:
    def __init__(self, inputs, out_shape, aliases, sems, start, finish, middle=None, middle_at=0.85):
        self.inputs, self.out_shape, self.aliases, self.sems = list(inputs), list(out_shape), dict(aliases), list(sems)
        self.start, self.finish = start, finish
        self.middle, self.middle_at = middle, middle_at


def _merge_carries(a, b):
    ia, oa, sa = len(a.inputs), len(a.out_shape), len(a.sems)
    aliases = dict(a.aliases)
    aliases.update({ia + i: oa + j for i, j in b.aliases.items()})

    def both(which):
        def run(ins, outs, sems):
            getattr(a, which)(ins[:ia], outs[:oa], sems[:sa])
            getattr(b, which)(ins[ia:], outs[oa:], sems[sa:])
        return run

    return _Carry(a.inputs + b.inputs, a.out_shape + b.out_shape, aliases, a.sems + b.sems, both("start"), both("finish"))


def _call(body, name, grid, in_specs, out_specs, out_shape, scratch_shapes, semantics, args, carry=None, prefetch=()):
    np_ = len(prefetch)
    if carry is None:
        spec = pltpu.PrefetchScalarGridSpec(num_scalar_prefetch=np_, grid=grid, in_specs=in_specs, out_specs=out_specs,
                                            scratch_shapes=scratch_shapes)
        return pl.pallas_call(body, name=name, grid_spec=spec, out_shape=out_shape,
                              compiler_params=_params(*semantics))(*prefetch, *args)
    ni, no, ns = len(in_specs), len(out_specs), len(scratch_shapes)
    ci, co = len(carry.inputs), len(carry.out_shape)

    def carrying(*refs):
        pre, refs = refs[:np_], refs[np_:]
        ins, refs = refs[:ni], refs[ni:]
        cins, refs = refs[:ci], refs[ci:]
        outs, refs = refs[:no], refs[no:]
        couts, refs = refs[:co], refs[co:]
        scratch, csems = refs[:ns], refs[ns:]
        step = pl.program_id(0)
        for ax in range(1, len(grid)):
            step = step * grid[ax] + pl.program_id(ax)
        steps = math.prod(grid)
        first = step == 0
        last = step == steps - 1

        @pl.when(first)
        def _():
            carry.start(cins, couts, csems)

        if carry.middle is not None:
            @pl.when(step == min(int(carry.middle_at * steps), steps - 1))
            def _():
                carry.middle(cins, couts, csems)

        body(*pre, *ins, *outs, *scratch)

        @pl.when(last)
        def _():
            carry.finish(cins, couts, csems)

    spec = pltpu.PrefetchScalarGridSpec(
        num_scalar_prefetch=np_, grid=grid, in_specs=list(in_specs) + [ANY] * ci,
        out_specs=list(out_specs) + [ANY] * co, scratch_shapes=list(scratch_shapes) + carry.sems)
    return pl.pallas_call(
        carrying, name=name, grid_spec=spec, out_shape=list(out_shape) + carry.out_shape,
        input_output_aliases={np_ + ni + i: no + j for i, j in carry.aliases.items()},
        compiler_params=_params(*(["arbitrary"] * len(grid))),
    )(*prefetch, *args, *carry.inputs)


def _run_carry(carry, name):
    ci, co = len(carry.inputs), len(carry.out_shape)

    def body(*refs):
        cins, couts, csems = refs[:ci], refs[ci:ci + co], refs[ci + co:]
        carry.start(cins, couts, csems)
        if carry.middle is not None:
            carry.middle(cins, couts, csems)
        carry.finish(cins, couts, csems)

    return pl.pallas_call(body, name=name, in_specs=[ANY] * ci, out_specs=[ANY] * co, out_shape=carry.out_shape,
                          input_output_aliases=carry.aliases, scratch_shapes=carry.sems)(*carry.inputs)


def _ffn_bwd_hidden(db, A, B, wd, name, carry=None):
    T, D = db.shape
    FF = wd.shape[0]
    tm = _tile(T, 1024, 16)
    tf = _tile(FF, 512, LANES)

    def body(d_ref, A_ref, B_ref, wd_ref, dG_ref, dU_ref):
        dH = _dot_nt(d_ref[...], wd_ref[...])
        dU_ref[...] = (dH * A_ref[...].astype(F32)).astype(BF16)
        dG_ref[...] = (dH * B_ref[...].astype(F32)).astype(BF16)

    act = pl.BlockSpec((tm, tf), lambda i, f: (i, f))
    return _call(
        body, name, (T // tm, FF // tf),
        [pl.BlockSpec((tm, D), lambda i, f: (i, 0)), act, act, pl.BlockSpec((tf, D), lambda i, f: (f, 0))],
        [act, act], [jax.ShapeDtypeStruct((T, FF), BF16)] * 2,
        [], ("parallel", "arbitrary"), (db, A, B, wd), carry)


TAIL_ROWS = 128


class _NormBwdTail:
    def __init__(self, T, D, tm, scale):
        self.T, self.D, self.tm, self.scale = T, D, tm, scale
        self.ni = T // tm
        self.scratch = [pltpu.VMEM((tm, D), F32), pltpu.VMEM((tm, D), F32), pltpu.VMEM((tm, D), F32),
                        pltpu.VMEM((tm, D), BF16), pltpu.SemaphoreType.DMA((4,))]
        self.out_shape = [jax.ShapeDtypeStruct((T, D), F32), jax.ShapeDtypeStruct((T, D), BF16)]

    def _rows(self, k):
        return pl.ds(pl.multiple_of(k * self.tm, self.tm), self.tm)

    def _loads(self, k, x_hbm, r_hbm, bufs):
        xbuf, rbuf, _, _, sems = bufs
        return [pltpu.make_async_copy(x_hbm.at[self._rows(k)], xbuf, sems.at[0]),
                pltpu.make_async_copy(r_hbm.at[self._rows(k)], rbuf, sems.at[1])]

    def _stores(self, k, dx_hbm, dxb_hbm, bufs):
        _, _, obuf, obbuf, sems = bufs
        return [pltpu.make_async_copy(obuf, dx_hbm.at[self._rows(k)], sems.at[2]),
                pltpu.make_async_copy(obbuf, dxb_hbm.at[self._rows(k)], sems.at[3])]

    def prefetch(self, i, x_hbm, r_hbm, bufs):
        for cp in self._loads(i, x_hbm, r_hbm, bufs):
            cp.start()

    def run(self, i, acc_ref, g_ref, x_hbm, r_hbm, dx_hbm, dxb_hbm, dg_ref, bufs):
        xbuf, rbuf, obuf, obbuf, _ = bufs
        for cp in self._loads(i, x_hbm, r_hbm, bufs):
            cp.wait()

        @pl.when(i > 0)
        def _():
            for cp in self._stores(i - 1, dx_hbm, dxb_hbm, bufs):
                cp.wait()

        dgain = None
        for r0 in range(0, self.tm, TAIL_ROWS):
            rs = slice(r0, min(r0 + TAIL_ROWS, self.tm))
            dv, dgr = _rms_bwd(acc_ref[rs, :], xbuf[rs, :], g_ref[...])
            dx = rbuf[rs, :] + dv
            obuf[rs, :] = dx
            obbuf[rs, :] = (self.scale * dx).astype(BF16)
            dgain = dgr if dgain is None else dgain + dgr
        for cp in self._stores(i, dx_hbm, dxb_hbm, bufs):
            cp.start()

        @pl.when(i == 0)
        def _():
            dg_ref[...] = dgain

        @pl.when(i > 0)
        def _():
            dg_ref[...] += dgain

        @pl.when(i == self.ni - 1)
        def _():
            for cp in self._stores(i, dx_hbm, dxb_hbm, bufs):
                cp.wait()


def _mm_nt_norm_bwd(a, w, x_in, gain, dres, scale, name, carry=None):
    T, K = a.shape
    D = w.shape[0]
    whole = D * K * 2 <= 24 * 2**20
    tm = _tile(T, 256 if whole else 512, 16)
    tk = K if whole else _tile(K, 1280, LANES)
    nk = K // tk
    tail = _NormBwdTail(T, D, tm, scale)
    w_spec = (pl.BlockSpec((D, tk), lambda i, k: (0, k), pipeline_mode=pl.Buffered(1)) if whole
              else pl.BlockSpec((D, tk), lambda i, k: (0, k)))

    def body(a_ref, w_ref, g_ref, x_hbm, r_hbm, dg_ref, dx_hbm, dxb_hbm, acc_ref, *bufs):
        i, k = pl.program_id(0), pl.program_id(1)

        @pl.when(k == 0)
        def _():
            tail.prefetch(i, x_hbm, r_hbm, bufs)

        contrib = _dot_nt(a_ref[...], w_ref[...])

        @pl.when(k == 0)
        def _():
            acc_ref[...] = contrib

        @pl.when(k > 0)
        def _():
            acc_ref[...] += contrib

        @pl.when(k == nk - 1)
        def _():
            tail.run(i, acc_ref, g_ref, x_hbm, r_hbm, dx_hbm, dxb_hbm, dg_ref, bufs)

    return _call(
        body, name, (T // tm, nk),
        [pl.BlockSpec((tm, tk), lambda i, k: (i, k)), w_spec, pl.BlockSpec((1, D), lambda i, k: (0, 0)), ANY, ANY],
        [pl.BlockSpec((1, D), lambda i, k: (0, 0)), ANY, ANY],
        [jax.ShapeDtypeStruct((1, D), F32)] + tail.out_shape,
        [pltpu.VMEM((tm, D), F32)] + tail.scratch, ("arbitrary", "arbitrary"), (a, w, gain, x_in, dres), carry)


def _rms_bwd_res(dn, x, gain, dres, scale, name, carry=None):
    T, D = x.shape
    tm = _tile(T, 256, 16)

    def body(dn_ref, x_ref, g_ref, dr_ref, dx_ref, dxb_ref, dg_ref):
        i = pl.program_id(0)
        dv, dgain = _rms_bwd(dn_ref[...].astype(F32), x_ref[...], g_ref[...])
        dx = dr_ref[...] + dv
        dx_ref[...] = dx
        dxb_ref[...] = (scale * dx).astype(BF16)

        @pl.when(i == 0)
        def _():
            dg_ref[...] = dgain

        @pl.when(i > 0)
        def _():
            dg_ref[...] += dgain

    row = pl.BlockSpec((tm, D), lambda i: (i, 0))
    vec = pl.BlockSpec((1, D), lambda i: (0, 0))
    return _call(
        body, name, (T // tm,), [row, row, vec, row], [row, row, vec],
        [jax.ShapeDtypeStruct((T, D), F32), jax.ShapeDtypeStruct((T, D), BF16), jax.ShapeDtypeStruct((1, D), F32)],
        [], ("arbitrary",), (dn, x, gain, dres), carry)


def _loss_head(x3, gain, target, name):
    T, D = x3.shape
    tm = _tile(T, 256, 16)

    def body(x_ref, g_ref, t_ref, dx_ref, dxb_ref, ls_ref, dg_ref):
        i = pl.program_id(0)
        xv = x_ref[...]
        err = xv * _rstd(xv) * g_ref[...] - t_ref[...]
        sq = jnp.sum(jnp.sum(err * err, axis=1, keepdims=True), axis=0, keepdims=True)
        dv, dgain = _rms_bwd(err * (1.0 / D), xv, g_ref[...])
        dx_ref[...] = dv
        dxb_ref[...] = (FFN_RESIDUAL_SCALE * dv).astype(BF16)
        sqb = jnp.broadcast_to(sq, (1, LANES))

        @pl.when(i == 0)
        def _():
            dg_ref[...] = dgain
            ls_ref[...] = sqb

        @pl.when(i > 0)
        def _():
            dg_ref[...] += dgain
            ls_ref[...] += sqb

    row = pl.BlockSpec((tm, D), lambda i: (i, 0))
    vec = pl.BlockSpec((1, D), lambda i: (0, 0))
    return pl.pallas_call(
        body, name=name, grid=(T // tm,),
        in_specs=[row, vec, row],
        out_specs=[row, row, pl.BlockSpec((1, LANES), lambda i: (0, 0)), vec],
        out_shape=[jax.ShapeDtypeStruct((T, D), F32), jax.ShapeDtypeStruct((T, D), BF16),
                   jax.ShapeDtypeStruct((1, LANES), F32), jax.ShapeDtypeStruct((1, D), F32)],
        compiler_params=_params("arbitrary"),
    )(x3, gain, target)


WHOLE_MIN, WHOLE_MAX = 16 * 2**20, 24 * 2**20


def _fits_whole(w):
    return WHOLE_MIN < w.size * 2 <= WHOLE_MAX


def _weight_spec(block, index_map, whole):
    return pl.BlockSpec(block, index_map, pipeline_mode=pl.Buffered(1)) if whole else pl.BlockSpec(block, index_map)


def _norm_mm(x, gain, w, name, carry=None):
    T, D = x.shape
    N = w.shape[1]
    whole = _fits_whole(w)
    tm = _tile(T, 256 if whole else 512, 16)
    tn = N if whole else _tile(N, 2560, LANES)

    def body(x_ref, g_ref, w_ref, n_ref, z_ref):
        @pl.when(pl.program_id(1) == 0)
        def _():
            xv = x_ref[...]
            n_ref[...] = (xv * _rstd(xv) * g_ref[...]).astype(BF16)

        z_ref[...] = jnp.dot(n_ref[...], w_ref[...], preferred_element_type=F32).astype(BF16)

    return _call(
        body, name, (T // tm, N // tn),
        [pl.BlockSpec((tm, D), lambda i, j: (i, 0)),
         pl.BlockSpec((1, D), lambda i, j: (0, 0)),
         _weight_spec((D, tn), lambda i, j: (0, j), whole)],
        [pl.BlockSpec((tm, D), lambda i, j: (i, 0)),
         pl.BlockSpec((tm, tn), lambda i, j: (i, j))],
        [jax.ShapeDtypeStruct((T, D), BF16), jax.ShapeDtypeStruct((T, N), BF16)],
        [], ("parallel", "arbitrary"), (x, gain, w), carry)


def _ffn_up(n, G, wu, name, carry=None):
    T, D = n.shape
    FF = wu.shape[1]
    whole = _fits_whole(wu)
    tm = _tile(T, 128 if whole else 512, 16)
    tf = FF if whole else _tile(FF, 1408, LANES)

    def body(n_ref, G_ref, wu_ref, H_ref, A_ref, B_ref):
        U = jnp.dot(n_ref[...], wu_ref[...], preferred_element_type=F32)
        Gv = G_ref[...].astype(F32)
        s = _sigmoid(Gv)
        sg = Gv * s
        A_ref[...] = sg.astype(BF16)
        B_ref[...] = (U * (s * (1.0 + Gv * (1.0 - s)))).astype(BF16)
        H_ref[...] = (sg * U).astype(BF16)

    act = pl.BlockSpec((tm, tf), lambda i, f: (i, f))
    return _call(
        body, name, (T // tm, FF // tf),
        [pl.BlockSpec((tm, D), lambda i, f: (i, 0)), act, _weight_spec((D, tf), lambda i, f: (0, f), whole)],
        [act, act, act], [jax.ShapeDtypeStruct((T, FF), BF16)] * 3, [], ("parallel", "arbitrary"), (n, G, wu), carry)


def _mm_fullk(a, w, trans_w, residual, out_dtype, name, carry=None, scale=1.0):
    T, K = a.shape
    N = w.shape[0] if trans_w else w.shape[1]
    whole = _fits_whole(w)
    tm = _tile(T, 256 if whole else 512, 16)
    tn = N if whole else _tile(N, 2048 * 2560 // K, LANES)

    def body(*refs):
        if residual is None:
            a_ref, w_ref, o_ref = refs
        else:
            a_ref, w_ref, r_ref, o_ref = refs
        if trans_w:
            acc = _dot_nt(a_ref[...], w_ref[...])
        else:
            acc = jnp.dot(a_ref[...], w_ref[...], preferred_element_type=F32)
        if scale != 1.0:
            acc = scale * acc
        if residual is not None:
            acc = acc + r_ref[...]
        o_ref[...] = acc.astype(out_dtype)

    w_spec = (_weight_spec((tn, K), lambda i, j: (j, 0), whole) if trans_w
              else _weight_spec((K, tn), lambda i, j: (0, j), whole))
    in_specs = [pl.BlockSpec((tm, K), lambda i, j: (i, 0)), w_spec]
    args = [a, w]
    if residual is not None:
        in_specs.append(pl.BlockSpec((tm, tn), lambda i, j: (i, j)))
        args.append(residual)
    return _call(body, name, (T // tm, N // tn), in_specs, [pl.BlockSpec((tm, tn), lambda i, j: (i, j))],
                 [jax.ShapeDtypeStruct((T, N), out_dtype)], [], ("parallel", "arbitrary"), args, carry)


def _mm_tn_pair(a, b, c_arr, axis, name, carry=None):
    T, M = a.shape
    N = b.shape[1]
    tk = _tile(T, 2048, 16)
    nk = T // tk
    nq = 4
    if axis == 1:
        rows, cols = M // 2, N // nq
        a_spec = pl.BlockSpec((tk, rows), lambda p, j, k, cr: (k, jnp.where(p == 0, 1 - cr[0], cr[0])))
        b_spec = pl.BlockSpec((tk, cols), lambda p, j, k, cr: (k, j))
        s_shape, land_shape = (1, rows, N), (rows, N)
        s_spec = pl.BlockSpec((None, rows, cols), lambda p, j, k, cr: (0, 0, j * p))
    else:
        rows, cols = M // nq, N // 2
        a_spec = pl.BlockSpec((tk, rows), lambda p, j, k, cr: (k, j))
        b_spec = pl.BlockSpec((tk, cols), lambda p, j, k, cr: (k, jnp.where(p == 0, 1 - cr[0], cr[0])))
        s_shape, land_shape = (nq, rows, cols), (nq, rows, cols)
        s_spec = pl.BlockSpec((None, rows, cols), lambda p, j, k, cr: (j * p, 0, 0))

    def body(c_ref, a_ref, b_ref, s_ref, land, acc_ref, stage, got, send_sems, recv_sems, loc_sem):
        p, j, k = pl.program_id(0), pl.program_id(1), pl.program_id(2)
        x, y, c = _me()

        def tile(jj):
            return land.at[:, pl.ds(jj * cols, cols)] if axis == 1 else land.at[jj]

        def send(jj):
            return _remote(stage, tile(jj), send_sems, recv_sems, jj, (x, y, 1 - c))

        @pl.when(k == 0)
        def _():
            acc_ref[...] = jnp.zeros_like(acc_ref)

        acc_ref[...] += _dot_tn(a_ref[...], b_ref[...])

        def fetch(jj):
            return pltpu.make_async_copy(tile(jj), got, loc_sem.at[0])

        for jj in range(nq):
            @pl.when(jnp.logical_and(k == nk - 1, jnp.logical_and(p == 0, j == jj)))
            def _():
                if jj > 0:
                    send(jj - 1).wait_send()
                stage[...] = acc_ref[...].astype(BF16)
                send(jj).start()

            @pl.when(jnp.logical_and(k == max(nk - 2, 0), jnp.logical_and(p == 1, j == jj)))
            def _():
                if jj == 0:
                    send(nq - 1).wait_send()
                send(jj).wait_recv()
                fetch(jj).start()

            @pl.when(jnp.logical_and(k == nk - 1, jnp.logical_and(p == 1, j == jj)))
            def _():
                fetch(jj).wait()
                s_ref[...] = (acc_ref[...] + got[...].astype(F32)).astype(BF16)

    return _call(
        body, name, (2, nq, nk), [a_spec, b_spec], [s_spec, ANY],
        [jax.ShapeDtypeStruct(s_shape, BF16), jax.ShapeDtypeStruct(land_shape, BF16)],
        [pltpu.VMEM((rows, cols), F32), pltpu.VMEM((rows, cols), BF16), pltpu.VMEM((rows, cols), BF16),
         pltpu.SemaphoreType.DMA((nq,)), pltpu.SemaphoreType.DMA((nq,)), pltpu.SemaphoreType.DMA((1,))],
        ("arbitrary", "arbitrary", "arbitrary"), (a, b), carry, (c_arr,))


GELU_K = math.sqrt(2.0 / math.pi)
GELU_C = 0.044715


def _gelu_and_grad(v):
    u = GELU_K * (v + GELU_C * v * v * v)
    th = jnp.tanh(u)
    g = 0.5 * v * (1.0 + th)
    dg = 0.5 * (1.0 + th) + 0.5 * v * (1.0 - th * th) * GELU_K * (1.0 + 3.0 * GELU_C * v * v)
    return g, dg


def _neg_expm1(v):
    poly = v * (1.0 + v * (0.5 + v * (1.0 / 6 + v * (1.0 / 24 + v * (1.0 / 120 + v * (1.0 / 720))))))
    return jnp.where(v > -0.25, -poly, 1.0 - jnp.exp(v))


def _softplus_neg(lam):
    e = jnp.exp(-jnp.abs(lam))
    log1pe = jnp.where(e < 1e-4, e * (1.0 - 0.5 * e), jnp.log(1.0 + e))
    sp = jnp.maximum(-lam, 0.0) + log1pe
    dsp = -1.0 / (1.0 + jnp.exp(lam))
    return sp, dsp


def _earlier(ext, j):
    return pltpu.roll(ext, j, 0)[SUBLANES:, :]


def _later(ext, j):
    n = ext.shape[0]
    return pltpu.roll(ext, n - j, 0)[:n - SUBLANES, :]


def _taps(v, halo, K):
    ext = jnp.concatenate([halo, v], axis=0)
    return [v] + [_earlier(ext, j) for j in range(1, K)]


def _block_diag(vb, w_ref, nh, hd):
    return jnp.concatenate(
        [jnp.dot(vb[:, h * hd:(h + 1) * hd], w_ref[h], preferred_element_type=F32) for h in range(nh)], axis=1)


def _lru_gates(xc, wa_ref, ba_ref, wi_ref, bi_ref, sp, nh, hd):
    xcb = xc.astype(BF16)
    r = _sigmoid(_block_diag(xcb, wa_ref, nh, hd) + ba_ref[...])
    ig = _sigmoid(_block_diag(xcb, wi_ref, nh, hd) + bi_ref[...])
    log_a = -LRU_C * r * sp
    a = jnp.exp(log_a)
    mult = jnp.sqrt(_neg_expm1(2.0 * log_a))
    return xcb, r, ig, a, mult


def _mix_fwd(z, cw, cb, wa, ba, wi, bi, lam, sw, glo, gso, name, carry=None):
    T = z.shape[0]
    DL = cb.shape[1]
    DS = gso.shape[1]
    NH, HD = wa.shape[0], wa.shape[1]
    KL, KS = cw.shape[0], sw.shape[0]
    tt = _tile(T, 128, 16)
    o_g, o_b, o_c, o_x = DL, 2 * DL, 2 * DL + DS, 2 * DL + 2 * DS

    def body(z_ref, cw_ref, cb_ref, wa_ref, ba_ref, wi_ref, bi_ref, lam_ref, sw_ref, glo_ref, gso_ref,
             h_ref, y_ref, cx_ref, cp_ref, ch_ref):
        @pl.when(pl.program_id(0) == 0)
        def _():
            cx_ref[...] = jnp.zeros_like(cx_ref)
            cp_ref[...] = jnp.zeros_like(cp_ref)
            ch_ref[...] = jnp.zeros_like(ch_ref)

        def zcol(o, n):
            return z_ref[:, o:o + n].astype(F32)

        lx = zcol(0, DL)
        xs = _taps(lx, cx_ref[...], KL)
        cx_ref[...] = lx[tt - SUBLANES:, :]
        xc = cb_ref[...] + xs[0] * cw_ref[KL - 1:KL, :]
        for j in range(1, KL):
            xc = xc + xs[j] * cw_ref[KL - 1 - j:KL - j, :]
        sp, _ = _softplus_neg(lam_ref[...])
        _, _, ig, a, mult = _lru_gates(xc, wa_ref, ba_ref, wi_ref, bi_ref, sp, NH, HD)
        b = mult * (ig * xc)
        rows = lax.broadcasted_iota(jnp.int32, (tt, DL), 0)
        s = 1
        while s < tt:
            keep = rows >= s
            b = jnp.where(keep, a * pltpu.roll(b, s, 0) + b, b)
            a = jnp.where(keep, a * pltpu.roll(a, s, 0), a)
            s *= 2
        h = a * ch_ref[SUBLANES - 1:SUBLANES, :] + b
        ch_ref[...] = h[tt - SUBLANES:, :]
        h_ref[...] = h
        ge, _ = _gelu_and_grad(zcol(o_g, DL))
        ylru = h * ge
        y_ref[:, 0:DL] = (ylru * _rstd(ylru) * glo_ref[...]).astype(BF16)

        p = zcol(o_c, DS) * zcol(o_x, DS)
        ps = _taps(p, cp_ref[...], KS)
        cp_ref[...] = p[tt - SUBLANES:, :]
        cv = ps[0] * sw_ref[KS - 1:KS, :]
        for j in range(1, KS):
            cv = cv + ps[j] * sw_ref[KS - 1 - j:KS - j, :]
        ysc = zcol(o_b, DS) * cv
        y_ref[:, DL:DL + DS] = (ysc * _rstd(ysc) * gso_ref[...]).astype(BF16)

    def full(shape):
        return pl.BlockSpec(shape, lambda t: (0,) * len(shape))

    return _call(
        body, name, (T // tt,),
        [pl.BlockSpec((tt, z.shape[1]), lambda t: (t, 0)),
         full(cw.shape), full(cb.shape), full(wa.shape), full(ba.shape), full(wi.shape), full(bi.shape),
         full(lam.shape), full(sw.shape), full(glo.shape), full(gso.shape)],
        [pl.BlockSpec((tt, DL), lambda t: (t, 0)), pl.BlockSpec((tt, DL + DS), lambda t: (t, 0))],
        [jax.ShapeDtypeStruct((T, DL), F32), jax.ShapeDtypeStruct((T, DL + DS), BF16)],
        [pltpu.VMEM((SUBLANES, DL), F32), pltpu.VMEM((SUBLANES, DS), F32), pltpu.VMEM((SUBLANES, DL), F32)],
        ("arbitrary",), (z, cw, cb, wa, ba, wi, bi, lam, sw, glo, gso), carry)


V_BA, V_BI, V_LAM, V_CB, V_CW, V_SW, V_GLO, V_GSO, V_ROWS = 0, 1, 2, 3, 4, 8, 11, 12, 16


def _mix_bwd(z, h, dy, cw, cb, wa, ba, wi, bi, lam, sw, glo, gso, name):
    T = z.shape[0]
    DL = cb.shape[1]
    DS = gso.shape[1]
    NH, HD = wa.shape[0], wa.shape[1]
    KL, KS = cw.shape[0], sw.shape[0]
    tt = _tile(T, 64, 16)
    nt = T // tt
    ZH = 2 * SUBLANES
    o_g, o_b, o_c, o_x = DL, 2 * DL, 2 * DL + DS, 2 * DL + 2 * DS

    def body(z_ref, zh_ref, h_ref, hh_ref, dy_ref, cw_ref, cb_ref, wa_ref, ba_ref, wi_ref, bi_ref, lam_ref,
             sw_ref, glo_ref, gso_ref, dz_ref, dwa_ref, dwi_ref, vec_ref, cdx_ref, cdc_ref, cdh_ref):
        i = pl.program_id(0)
        tr = nt - 1 - i

        @pl.when(i == 0)
        def _():
            dwa_ref[...] = jnp.zeros_like(dwa_ref)
            dwi_ref[...] = jnp.zeros_like(dwi_ref)
            vec_ref[...] = jnp.zeros_like(vec_ref)
            cdx_ref[...] = jnp.zeros_like(cdx_ref)
            cdc_ref[...] = jnp.zeros_like(cdc_ref)
            cdh_ref[...] = jnp.zeros_like(cdh_ref)

        def acc_row(r, v):
            vec_ref[pl.ds(r, 1), :] += jnp.sum(v, axis=0, keepdims=True)

        has_prev = tr > 0
        rows = lax.broadcasted_iota(jnp.int32, (tt, DL), 0)

        def zcol(o, n):
            return z_ref[:, o:o + n].astype(F32)

        def zhalo(o, n):
            return jnp.where(has_prev, zh_ref[:, o:o + n].astype(F32)[SUBLANES:, :], 0.0)

        lx = zcol(0, DL)
        xs = _taps(lx, zhalo(0, DL), KL)
        xc = cb_ref[...] + xs[0] * cw_ref[KL - 1:KL, :]
        for j in range(1, KL):
            xc = xc + xs[j] * cw_ref[KL - 1 - j:KL - j, :]
        sp, dsp = _softplus_neg(lam_ref[...])
        xcb, r, ig, a, mult = _lru_gates(xc, wa_ref, ba_ref, wi_ref, bi_ref, sp, NH, HD)
        hv = h_ref[...]
        hprev = _earlier(jnp.concatenate([jnp.where(has_prev, hh_ref[...], 0.0), hv], axis=0), 1)
        gate = zcol(o_g, DL)
        ge, dge = _gelu_and_grad(gate)
        ylru = hv * ge

        d_ylru, dglo = _rms_bwd(dy_ref[:, 0:DL].astype(F32), ylru, glo_ref[...])
        vec_ref[pl.ds(V_GLO, 1), :] += dglo
        dz_ref[:, o_g:o_g + DL] = (d_ylru * hv * dge).astype(BF16)
        bq = d_ylru * ge
        aq = jnp.where(rows == tt - 1, 1.0, pltpu.roll(a, tt - 1, 0))
        s = 1
        while s < tt:
            keep = rows < tt - s
            bq = jnp.where(keep, aq * pltpu.roll(bq, tt - s, 0) + bq, bq)
            aq = jnp.where(keep, aq * pltpu.roll(aq, tt - s, 0), aq)
            s *= 2
        dhh = bq + aq * cdh_ref[0:1, :]
        cdh_ref[0:1, :] = a[0:1, :] * dhh[0:1, :]

        da = dhh * hprev
        dmult = dhh * (ig * xc)
        d_i = dhh * mult * xc
        dxc = dhh * mult * ig
        dlog = da * a - dmult * (a * a) / mult
        acc_row(V_LAM, dlog * (-LRU_C * r) * dsp)
        dpa = dlog * (-LRU_C * sp) * r * (1.0 - r)
        dpi = d_i * ig * (1.0 - ig)
        acc_row(V_BA, dpa)
        acc_row(V_BI, dpi)
        dpab = dpa.astype(BF16)
        dpib = dpi.astype(BF16)
        back = []
        for hh in range(NH):
            sl = slice(hh * HD, (hh + 1) * HD)
            dwa_ref[hh] += _dot_tn(xcb[:, sl], dpab[:, sl])
            dwi_ref[hh] += _dot_tn(xcb[:, sl], dpib[:, sl])
            back.append(_dot_nt(dpab[:, sl], wa_ref[hh]) + _dot_nt(dpib[:, sl], wi_ref[hh]))
        dxc = dxc + jnp.concatenate(back, axis=1)

        acc_row(V_CB, dxc)
        extd = jnp.concatenate([dxc, cdx_ref[...]], axis=0)
        cdx_ref[...] = dxc[0:SUBLANES, :]
        dlx = dxc * cw_ref[KL - 1:KL, :]
        acc_row(V_CW + KL - 1, dxc * xs[0])
        for j in range(1, KL):
            dlx = dlx + _later(extd, j) * cw_ref[KL - 1 - j:KL - j, :]
            acc_row(V_CW + KL - 1 - j, dxc * xs[j])
        dz_ref[:, 0:DL] = dlx.astype(BF16)

        sb = zcol(o_b, DS)
        sc = zcol(o_c, DS)
        sx = zcol(o_x, DS)
        p = sc * sx
        ps = _taps(p, zhalo(o_c, DS) * zhalo(o_x, DS), KS)
        cv = ps[0] * sw_ref[KS - 1:KS, :]
        for j in range(1, KS):
            cv = cv + ps[j] * sw_ref[KS - 1 - j:KS - j, :]
        d_ysc, dgso = _rms_bwd(dy_ref[:, DL:DL + DS].astype(F32), sb * cv, gso_ref[...])
        vec_ref[pl.ds(V_GSO, 1), :] += dgso
        dz_ref[:, o_b:o_b + DS] = (d_ysc * cv).astype(BF16)
        dcv = d_ysc * sb
        extc = jnp.concatenate([dcv, cdc_ref[...]], axis=0)
        cdc_ref[...] = dcv[0:SUBLANES, :]
        dp = dcv * sw_ref[KS - 1:KS, :]
        acc_row(V_SW + KS - 1, dcv * ps[0])
        for j in range(1, KS):
            dp = dp + _later(extc, j) * sw_ref[KS - 1 - j:KS - j, :]
            acc_row(V_SW + KS - 1 - j, dcv * ps[j])
        dz_ref[:, o_c:o_c + DS] = (dp * sx).astype(BF16)
        dz_ref[:, o_x:o_x + DS] = (dp * sc).astype(BF16)

    def full(shape):
        return pl.BlockSpec(shape, lambda t: (0,) * len(shape))

    def rev(t):
        return nt - 1 - t

    def halo(t, rows):
        return jnp.maximum(rev(t) * (tt // rows) - 1, 0)

    return pl.pallas_call(
        body, name=name, grid=(nt,),
        in_specs=[pl.BlockSpec((tt, z.shape[1]), lambda t: (rev(t), 0)),
                  pl.BlockSpec((ZH, z.shape[1]), lambda t: (halo(t, ZH), 0)),
                  pl.BlockSpec((tt, DL), lambda t: (rev(t), 0)),
                  pl.BlockSpec((SUBLANES, DL), lambda t: (halo(t, SUBLANES), 0)),
                  pl.BlockSpec((tt, DL + DS), lambda t: (rev(t), 0)),
                  full(cw.shape), full(cb.shape), full(wa.shape), full(ba.shape), full(wi.shape), full(bi.shape),
                  full(lam.shape), full(sw.shape), full(glo.shape), full(gso.shape)],
        out_specs=[pl.BlockSpec((tt, z.shape[1]), lambda t: (rev(t), 0)),
                   full(wa.shape), full(wi.shape), full((V_ROWS, DL))],
        out_shape=[jax.ShapeDtypeStruct(z.shape, BF16), jax.ShapeDtypeStruct(wa.shape, F32),
                   jax.ShapeDtypeStruct(wi.shape, F32), jax.ShapeDtypeStruct((V_ROWS, DL), F32)],
        scratch_shapes=[pltpu.VMEM((SUBLANES, DL), F32), pltpu.VMEM((SUBLANES, DS), F32),
                        pltpu.VMEM((SUBLANES, DL), F32)],
        compiler_params=_params("arbitrary"),
    )(z, z, h, h, dy, cw, cb, wa, ba, wi, bi, lam, sw, glo, gso)


def _quad_add(s, r2, qc, axis, name):
    _, R, W = r2.shape
    tr = _tile(R, 256, 16)

    def body(qc_ref, s_ref, r0_ref, r1_ref, r2_ref, o_ref):
        o_ref[...] = ((s_ref[...].astype(F32) + r0_ref[...].astype(F32)) + r1_ref[...].astype(F32)) + r2_ref[...].astype(F32)

    blk = (None, tr, W)
    if axis == 1:
        own = pl.BlockSpec(blk, lambda i, qr: (0, i, qr[0]))
    else:
        own = pl.BlockSpec(blk, lambda i, qr: (qr[0], i, 0))
    return pl.pallas_call(
        body, name=name,
        grid_spec=pltpu.PrefetchScalarGridSpec(
            num_scalar_prefetch=1, grid=(R // tr,),
            in_specs=[own] + [pl.BlockSpec(blk, lambda i, qr, j=j: (j, i, 0)) for j in range(3)],
            out_specs=pl.BlockSpec(blk, lambda i, qr: (qr[1], i, 0))),
        out_shape=jax.ShapeDtypeStruct((2, R, W), F32),
        compiler_params=_params("parallel"),
    )(qc, s, r2, r2, r2)


CAST_BLOCKS = 4


def _cast_into_full(shards, qc, axes, name, carry=None):
    M = len(shards)
    nb = CAST_BLOCKS

    def body(qc_ref, *refs):
        for s_ref, o_ref in zip(refs[:M], refs[M:]):
            o_ref[...] = s_ref[...].astype(BF16)

    in_specs, out_specs, out_shape = [], [], []
    for s, ax in zip(shards, axes):
        R, W = s.shape
        Rh = R // 2
        tr = Rh // nb
        in_specs.append(pl.BlockSpec((tr, W), lambda hf, i, qr: (hf * nb + i, 0)))
        if ax == 1:
            out_shape.append(jax.ShapeDtypeStruct((2, Rh, 4 * W), BF16))
            out_specs.append(pl.BlockSpec((None, tr, W), lambda hf, i, qr: (hf, i, qr[0])))
        else:
            out_shape.append(jax.ShapeDtypeStruct((8, Rh, W), BF16))
            out_specs.append(pl.BlockSpec((None, tr, W), lambda hf, i, qr: (2 * qr[0] + hf, i, 0)))
    return _call(body, name, (2, nb), in_specs, out_specs, out_shape, [], ("parallel", "parallel"), shards, carry, (qc,))


def _adamw(w, g, m, v, name):
    R, C = w.shape
    tr = _tile(R, 256, SUBLANES)
    tc = C // 2 if g.ndim == 3 else _tile(C, 2048, LANES)
    c1 = 1.0 - ADAM_B1 ** ADAM_STEP
    c2 = 1.0 - ADAM_B2 ** ADAM_STEP

    def body(w_ref, g_ref, m_ref, v_ref, d_ref, mo_ref, vo_ref, go_ref):
        gv = g_ref[...]
        go_ref[...] = gv
        mn = ADAM_B1 * m_ref[...] + (1.0 - ADAM_B1) * gv
        vn = ADAM_B2 * v_ref[...] + (1.0 - ADAM_B2) * (gv * gv)
        mo_ref[...] = mn
        vo_ref[...] = vn
        d_ref[...] = -ADAM_LR * ((mn / c1) / (jnp.sqrt(vn / c2) + ADAM_EPS) + ADAM_WD * w_ref[...])

    blk = pl.BlockSpec((tr, tc), lambda i, j: (i, j))
    g_blk = pl.BlockSpec((None, tr, tc), lambda i, j: (j, i, 0)) if g.ndim == 3 else blk
    sh = jax.ShapeDtypeStruct((R, C), F32)
    return pl.pallas_call(
        body, name=name, grid=(R // tr, C // tc),
        in_specs=[blk, g_blk, blk, blk], out_specs=[blk] * 4, out_shape=[sh] * 4,
        compiler_params=_params("parallel", "parallel"),
    )(w, g, m, v)


def _other_chips(x, y):
    return [(1 - x, y), (x, 1 - y), (1 - x, 1 - y)]


def _remote(src, dst, send_sems, recv_sems, idx, dev):
    return pltpu.make_async_remote_copy(src_ref=src, dst_ref=dst, send_sem=send_sems.at[idx], recv_sem=recv_sems.at[idx],
                                        device_id=dev, device_id_type=MESH)


def _gather_carry(fulls, axes, pieces=None):
    M = len(fulls)

    def win(outs, m, qq, cc):
        Rh = fulls[m].shape[1]
        k, n = pieces[m] if pieces is not None and pieces[m] is not None else (0, 1)
        rows = pl.ds(k * (Rh // n), Rh // n)
        if axes[m] == 1:
            W = fulls[m].shape[2] // 4
            return outs[m].at[cc, rows, pl.ds(pl.multiple_of(qq * W, LANES), W)]
        return outs[m].at[2 * qq + cc, rows, :]

    def ici(outs, sems, m, j, src_q):
        x, y, c = _me()
        cx, cy = _other_chips(x, y)[j]
        blk = win(outs, m, src_q, c)
        return _remote(blk, blk, sems[0], sems[1], 6 * m + j, (cx, cy, c))

    def d2d(outs, sems, m, j, half):
        x, y, c = _me()
        cx, cy = _other_chips(x, y)[j]
        blk = win(outs, m, 2 * cx + cy, half)
        return _remote(blk, blk, sems[0], sems[1], 6 * m + 3 + j, (x, y, 1 - c))

    def start(ins, outs, sems):
        x, y, c = _me()
        for m in range(M):
            for j in range(3):
                ici(outs, sems, m, j, 2 * x + y).start()

    def middle(ins, outs, sems):
        x, y, c = _me()
        for m in range(M):
            for j, (cx, cy) in enumerate(_other_chips(x, y)):
                ici(outs, sems, m, j, 2 * cx + cy).wait_recv()
                d2d(outs, sems, m, j, c).start()

    def finish(ins, outs, sems):
        x, y, c = _me()
        for m in range(M):
            for j in range(3):
                d2d(outs, sems, m, j, 1 - c).wait_recv()
        for m in range(M):
            for j in range(3):
                ici(outs, sems, m, j, 2 * x + y).wait_send()
                d2d(outs, sems, m, j, c).wait_send()

    return _Carry(fulls, [jax.ShapeDtypeStruct(f.shape, f.dtype) for f in fulls], {m: m for m in range(M)},
                  [pltpu.SemaphoreType.DMA((6 * M,)), pltpu.SemaphoreType.DMA((6 * M,))], start, finish, middle)


def _gather_two_way_carry(fulls, axes):
    M = len(fulls)

    def part(outs, m, qq, cc, p):
        Rp = fulls[m].shape[1] // 2
        rows = pl.ds(p * Rp, Rp)
        if axes[m] == 1:
            W = fulls[m].shape[2] // 4
            return outs[m].at[cc, rows, pl.ds(pl.multiple_of(qq * W, LANES), W)]
        return outs[m].at[2 * qq + cc, rows, :]

    def copies(outs, sems):
        x, y, c = _me()
        q, qx, qy, qd = 2 * x + y, 2 * (1 - x) + y, 2 * x + (1 - y), 2 * (1 - x) + (1 - y)
        xn, yn, sib = (1 - x, y, c), (x, 1 - y, c), (x, y, 1 - c)
        table = {}
        for m in range(M):
            def cp(blk, k, dev):
                return _remote(blk, blk, sems[0], sems[1], 12 * m + k, dev)
            own0, own1 = part(outs, m, q, c, 0), part(outs, m, q, c, 1)
            table[m] = dict(
                to=[cp(own0, 0, xn), cp(own1, 1, yn), cp(own1, 2, xn), cp(own0, 3, yn)],
                landed=[cp(part(outs, m, qx, c, 0), 0, xn), cp(part(outs, m, qy, c, 1), 1, yn),
                        cp(part(outs, m, qx, c, 1), 2, xn), cp(part(outs, m, qy, c, 0), 3, yn),
                        cp(part(outs, m, qd, c, 0), 4, yn), cp(part(outs, m, qd, c, 1), 5, xn)],
                passed=[cp(part(outs, m, qx, c, 0), 4, yn), cp(part(outs, m, qy, c, 1), 5, xn)],
                handed=[cp(part(outs, m, qq, c, p), 6 + k, sib)
                        for k, (qq, p) in enumerate([(qx, 0), (qy, 1), (qx, 1), (qy, 0), (qd, 0), (qd, 1)])],
                taken=[cp(part(outs, m, qq, 1 - c, p), 6 + k, sib)
                       for k, (qq, p) in enumerate([(qx, 0), (qy, 1), (qx, 1), (qy, 0), (qd, 0), (qd, 1)])])
        return table

    def start(ins, outs, sems):
        t = copies(outs, sems)
        for m in range(M):
            for cp in t[m]['to']:
                cp.start()

    def finish(ins, outs, sems):
        t = copies(outs, sems)
        for m in range(M):
            for k in range(4):
                t[m]['landed'][k].wait_recv()
                if k < 2:
                    t[m]['passed'][k].start()
                t[m]['handed'][k].start()
        for m in range(M):
            for k in (4, 5):
                t[m]['landed'][k].wait_recv()
                t[m]['handed'][k].start()
        for m in range(M):
            for cp in t[m]['taken']:
                cp.wait_recv()
            for cp in t[m]['to'] + t[m]['passed'] + t[m]['handed']:
                cp.wait_send()

    return _Carry(fulls, [jax.ShapeDtypeStruct(f.shape, f.dtype) for f in fulls], {m: m for m in range(M)},
                  [pltpu.SemaphoreType.DMA((12 * M,)), pltpu.SemaphoreType.DMA((12 * M,))], start, finish)


def _chip_exchange_carry(sums, axes):
    M = len(sums)
    out_shape = []
    for s, ax in zip(sums, axes):
        _, Rh, C = s.shape
        out_shape.append(jax.ShapeDtypeStruct((3, Rh, C // 4 if ax == 1 else C), s.dtype))

    def copies(ins, outs, sems):
        x, y, c = _me()
        cps = []
        for m in range(M):
            for j, (cx, cy) in enumerate(_other_chips(x, y)):
                qj = 2 * cx + cy
                if axes[m] == 1:
                    W = sums[m].shape[2] // 4
                    src = ins[m].at[0, :, pl.ds(pl.multiple_of(qj * W, LANES), W)]
                else:
                    src = ins[m].at[qj]
                cps.append(_remote(src, outs[m].at[j], sems[0], sems[1], 3 * m + j, (cx, cy, c)))
        return cps

    def start(ins, outs, sems):
        for cp in copies(ins, outs, sems):
            cp.start()

    def finish(ins, outs, sems):
        for cp in copies(ins, outs, sems):
            cp.wait()

    return _Carry(sums, out_shape, {}, [pltpu.SemaphoreType.DMA((3 * M,)), pltpu.SemaphoreType.DMA((3 * M,))],
                  start, finish)


def _pair_share_carry(bufs):
    M = len(bufs)

    def start(ins, outs, sems):
        x, y, c = _me()
        for m in range(M):
            _remote(outs[m].at[c], outs[m].at[c], sems[0], sems[1], m, (x, y, 1 - c)).start()

    def finish(ins, outs, sems):
        x, y, c = _me()
        for m in range(M):
            _remote(outs[m].at[c], outs[m].at[c], sems[0], sems[1], m, (x, y, 1 - c)).wait_send()
            _remote(outs[m].at[1 - c], outs[m].at[1 - c], sems[0], sems[1], m, (x, y, 1 - c)).wait_recv()

    return _Carry(bufs, [jax.ShapeDtypeStruct(b.shape, b.dtype) for b in bufs], {m: m for m in range(M)},
                  [pltpu.SemaphoreType.DMA((M,)), pltpu.SemaphoreType.DMA((M,))], start, finish)


def _allreduce_small(v, name):
    R, W = v.shape
    Rh = R // 2

    def body(v_ref, o_ref, sib, quad, send_sems, recv_sems):
        x, y, c = _me()
        q = 2 * x + y
        sibling = (x, y, 1 - c)
        pair = _remote(v_ref, sib, send_sems, recv_sems, 0, sibling)
        pair.start()
        pair.wait()
        mine = pl.ds(pl.multiple_of(c * Rh, SUBLANES), Rh)
        quad[0] = v_ref[mine, :] + sib[mine, :]
        cps = []
        for k in (1, 2, 3):
            peer = (1 - x if k & 2 else x, 1 - y if k & 1 else y, c)
            cps.append(_remote(quad.at[0], quad.at[k], send_sems, recv_sems, k, peer))
            cps[-1].start()
        for cp in cps:
            cp.wait()
        acc = quad[q]
        for p in (1, 2, 3):
            acc = acc + quad[jnp.bitwise_xor(q, p)]
        o_ref[mine, :] = acc
        theirs = pl.ds(pl.multiple_of((1 - c) * Rh, SUBLANES), Rh)
        done = _remote(o_ref.at[mine, :], o_ref.at[mine, :], send_sems, recv_sems, 4, sibling)
        done.start()
        done.wait_send()
        _remote(o_ref.at[theirs, :], o_ref.at[theirs, :], send_sems, recv_sems, 4, sibling).wait_recv()

    vm = pl.BlockSpec(memory_space=pltpu.VMEM)
    return pl.pallas_call(
        body, name=name, in_specs=[vm], out_specs=vm, out_shape=jax.ShapeDtypeStruct((R, W), F32),
        scratch_shapes=[pltpu.VMEM((R, W), F32), pltpu.VMEM((4, Rh, W), F32),
                        pltpu.SemaphoreType.DMA((5,)), pltpu.SemaphoreType.DMA((5,))],
        compiler_params=pltpu.CompilerParams(vmem_limit_bytes=VMEM_LIMIT),
    )(v)


def _pack(pieces):
    flat = []
    for p in pieces:
        p = p.reshape(-1).astype(F32)
        pad = (-p.shape[0]) % PACK_ALIGN
        flat.append(jnp.pad(p, (0, pad)).reshape(-1, PACK_W))
    if sum(f.shape[0] for f in flat) % (2 * SUBLANES):
        flat.append(jnp.zeros((SUBLANES, PACK_W), F32))
    return jnp.concatenate(flat, axis=0)


def _unpack(packed, shapes):
    out, row = [], 0
    for shp in shapes:
        n = math.prod(shp)
        rows = -(-n // PACK_ALIGN) * SUBLANES
        out.append(packed[row:row + rows].reshape(-1)[:n].reshape(shp))
        row += rows
    return out


def kernel(x, ffn1_norm, ffn1_w_gate, ffn1_w_up, ffn1_w_down, mix_norm, w_in, lru_conv_w, lru_conv_b, lru_w_a, lru_b_a, lru_w_i, lru_b_i, lru_lambda, sc_conv_w, lru_out_norm, sc_out_norm, w_out, ffn2_norm, ffn2_w_gate, ffn2_w_up, ffn2_w_down, final_norm, loss_target, m_ffn1_norm, m_ffn1_w_gate, m_ffn1_w_up, m_ffn1_w_down, m_mix_norm, m_w_in, m_lru_conv_w, m_lru_conv_b, m_lru_w_a, m_lru_b_a, m_lru_w_i, m_lru_b_i, m_lru_lambda, m_sc_conv_w, m_lru_out_norm, m_sc_out_norm, m_w_out, m_ffn2_norm, m_ffn2_w_gate, m_ffn2_w_up, m_ffn2_w_down, m_final_norm, v_ffn1_norm, v_ffn1_w_gate, v_ffn1_w_up, v_ffn1_w_down, v_mix_norm, v_w_in, v_lru_conv_w, v_lru_conv_b, v_lru_w_a, v_lru_b_a, v_lru_w_i, v_lru_b_i, v_lru_lambda, v_sc_conv_w, v_lru_out_norm, v_sc_out_norm, v_w_out, v_ffn2_norm, v_ffn2_w_gate, v_ffn2_w_up, v_ffn2_w_down, v_final_norm):
    vals = locals()
    w = {n: vals[n] for n in WEIGHTS}
    mom = {n: vals["m_" + n] for n in WEIGHTS}
    var = {n: vals["v_" + n] for n in WEIGHTS}

    xi, yi, ci = _me()
    qi = 2 * xi + yi
    c_arr = jnp.reshape(ci, (1,)).astype(jnp.int32)
    qc_arr = jnp.stack([qi, ci]).astype(jnp.int32)

    T, D = x.shape[1], x.shape[2]
    xt = x.reshape(T, D)
    target = loss_target.reshape(T, D)
    DL = lru_conv_b.shape[-1]
    NH, HD = lru_w_a.shape[1], lru_w_a.shape[2]
    KL, KS = lru_conv_w.shape[1], sc_conv_w.shape[1]
    DLq = lru_conv_w.shape[2]

    axis_of = dict(zip(BIG, BIG_AXIS))
    placed, full = {}, {}

    def gather(names, pieces={}):
        return _gather_carry([placed[n] for n in names], [axis_of[n] for n in names], [pieces.get(n) for n in names])

    def gathered(names, views, unfinished=()):
        for n, g in zip(names, views):
            if n in unfinished:
                placed[n] = g
            else:
                full[n] = (g.reshape(2 * g.shape[1], g.shape[2]) if axis_of[n] == 1
                           else g.reshape(8 * g.shape[1], g.shape[2]))

    first, rest = 'ffn1_w_gate', [n for n in BIG if n != 'ffn1_w_gate']
    placed[first] = _cast_into_full([w[first][0]], qc_arr, [axis_of[first]], "cast_first_weight")[0]
    res = _cast_into_full([w[n][0] for n in rest], qc_arr, [axis_of[n] for n in rest], "cast_other_weights",
                          _gather_two_way_carry([placed[first]], [axis_of[first]]))
    placed.update(zip(rest, res[:len(rest)]))
    gathered([first], res[len(rest):])

    taps = jnp.zeros((2 * SUBLANES, DL), F32)
    taps = lax.dynamic_update_slice(taps, lru_conv_w[0], (0, qi * DLq))
    taps = lax.dynamic_update_slice(taps, sc_conv_w[0], (KL, qi * DLq))
    taps = _allreduce_small(jnp.where(ci == 0, taps, 0.0), "gather_conv_taps")
    cw, sw = taps[0:KL], taps[KL:KL + KS]

    cb = lru_conv_b
    wa, wi = lru_w_a[0].astype(BF16), lru_w_i[0].astype(BF16)
    ba, bi = lru_b_a.reshape(1, DL), lru_b_i.reshape(1, DL)
    mix_args = (cw, cb, wa, ba, wi, bi, lru_lambda, sw, lru_out_norm, sc_out_norm)
    gf = final_norm.reshape(1, D)

    names = ['ffn1_w_up', 'w_out']
    res = _norm_mm(xt, ffn1_norm, full['ffn1_w_gate'], "ffn1_gate", gather(names))
    n1, G1 = res[:2]
    gathered(names, res[2:])
    names = ['ffn1_w_down', 'ffn2_w_down']
    res = _ffn_up(n1, G1, full['ffn1_w_up'], "ffn1_up", gather(names, {'ffn2_w_down': (0, 2)}))
    H1, A1, B1 = res[:3]
    gathered(names, res[3:], unfinished=['ffn2_w_down'])
    names = ['w_in', 'ffn2_w_down']
    res = _mm_fullk(H1, full['ffn1_w_down'], False, xt, F32, "ffn1_down", gather(names, {'ffn2_w_down': (1, 2)}),
                    FFN_RESIDUAL_SCALE)
    x1 = res[0]
    gathered(names, res[1:])
    names = ['ffn2_w_gate']
    res = _norm_mm(x1, mix_norm, full['w_in'], "mix_in_proj", gather(names))
    n2, z = res[:2]
    gathered(names, res[2:])
    names = ['ffn2_w_up']
    res = _mix_fwd(z, *mix_args, "mix_fwd", gather(names))
    h, ymix = res[:2]
    gathered(names, res[2:])
    x2 = _mm_fullk(ymix, full['w_out'], False, x1, F32, "mix_out_proj")[0]
    n3, G2 = _norm_mm(x2, ffn2_norm, full['ffn2_w_gate'], "ffn2_gate")
    H2, A2, B2 = _ffn_up(n3, G2, full['ffn2_w_up'], "ffn2_up")
    x3 = _mm_fullk(H2, full['ffn2_w_down'], False, x2, F32, "ffn2_down", None, FFN_RESIDUAL_SCALE)[0]
    dx3, d3b, sqerr, dgf = _loss_head(x3, gf, target, "loss_head")

    sums, halves, shared = {}, {}, {}

    def dw(n, a, b, name, carry=None):
        res = _mm_tn_pair(a, b, c_arr, axis_of[n], name, carry)
        sums[n] = res[0]
        return res[2:]

    def chip_carry(names):
        return _chip_exchange_carry([sums[n] for n in names], [axis_of[n] for n in names])

    def chip_add(names, recv):
        for n, r in zip(names, recv):
            halves[n] = _quad_add(sums[n], r, qc_arr, axis_of[n], "grad_chip_add_" + n)

    dG2, dU2 = _ffn_bwd_hidden(d3b, A2, B2, full['ffn2_w_down'], "ffn2_bwd_hidden")
    dn3 = _mm_fullk(dG2, full['ffn2_w_gate'], True, None, BF16, "ffn2_bwd_input_gate")[0]
    dn3 = _mm_fullk(dU2, full['ffn2_w_up'], True, dn3, BF16, "ffn2_bwd_input_up")[0]
    dx2, dx2b, dg_ffn2 = _rms_bwd_res(dn3, x2, ffn2_norm, dx3, 1.0, "ffn2_norm_bwd")
    dw('ffn2_w_gate', n3, dG2, "ffn2_dwg")
    dw('ffn2_w_up', n3, dU2, "ffn2_dwu")
    dw('ffn2_w_down', H2, d3b, "ffn2_dwd")
    dy = _mm_fullk(dx2b, full['w_out'], True, None, BF16, "mix_out_bwd")[0]
    dz, dwa, dwi, vec = _mix_bwd(z, h, dy, *mix_args, "mix_bwd")
    dg_mix, dx1, d1b = _mm_nt_norm_bwd(dz, full['w_in'], x1, mix_norm, dx2, FFN_RESIDUAL_SCALE, "mix_in_bwd")

    def share_carry(names):
        return _pair_share_carry([halves[n] for n in names])

    res = _ffn_bwd_hidden(d1b, A1, B1, full['ffn1_w_down'], "ffn1_bwd_hidden", chip_carry(['ffn2_w_gate']))
    dG1, dU1 = res[:2]
    chip_add(['ffn2_w_gate'], res[2:])
    res = _mm_fullk(dG1, full['ffn1_w_gate'], True, None, BF16, "ffn1_bwd_input_gate", chip_carry(['ffn2_w_up']))
    chip_add(['ffn2_w_up'], res[1:])
    res = _mm_fullk(dU1, full['ffn1_w_up'], True, res[0], BF16, "ffn1_bwd_input_up", chip_carry(['ffn2_w_down']))
    dn1 = res[0]
    chip_add(['ffn2_w_down'], res[1:])
    dx0, _, dg_ffn1 = _rms_bwd_res(dn1, xt, ffn1_norm, dx1, 1.0, "ffn1_norm_bwd")
    names = ['ffn2_w_gate', 'ffn2_w_up', 'ffn2_w_down']
    shared.update(zip(names, dw('ffn1_w_gate', n1, dG1, "ffn1_dwg", share_carry(names))))
    chip_add(['ffn1_w_gate'], dw('ffn1_w_up', n1, dU1, "ffn1_dwu", chip_carry(['ffn1_w_gate'])))
    chip_add(['ffn1_w_up'], dw('ffn1_w_down', H1, d1b, "ffn1_dwd", chip_carry(['ffn1_w_up'])))
    names = ['ffn1_w_gate', 'ffn1_w_up']
    res = dw('w_in', n2, dz, "mix_dwin", _merge_carries(chip_carry(['ffn1_w_down']), share_carry(names)))
    chip_add(['ffn1_w_down'], res[:1])
    shared.update(zip(names, res[1:]))
    res = dw('w_out', ymix, dx2b, "mix_dwout", _merge_carries(chip_carry(['w_in']), share_carry(['ffn1_w_down'])))
    chip_add(['w_in'], res[:1])
    shared['ffn1_w_down'] = res[1]
    chip_add(['w_out'], _run_carry(chip_carry(['w_out']), "grad_chip_exchange_w_out"))
    names = ['w_in', 'w_out']
    shared.update(zip(names, _run_carry(share_carry(names), "grad_pair_share")))
    out_g, out_d, out_m, out_v = {}, {}, {}, {}
    for n in BIG:
        shp = w[n].shape
        g = shared[n].reshape(shp[1], shp[2]) if axis_of[n] == 1 else shared[n]
        outs = _adamw(w[n][0], g, mom[n][0], var[n][0], "adamw_" + n)
        out_d[n], out_m[n], out_v[n], out_g[n] = (a.reshape(shp) for a in outs)

    small = [n for n in WEIGHTS if n not in BIG]
    local_small = {
        'ffn1_norm': dg_ffn1, 'mix_norm': dg_mix, 'lru_conv_w': vec[V_CW:V_CW + KL], 'lru_conv_b': vec[V_CB],
        'lru_w_a': dwa, 'lru_b_a': vec[V_BA], 'lru_w_i': dwi, 'lru_b_i': vec[V_BI], 'lru_lambda': vec[V_LAM],
        'sc_conv_w': vec[V_SW:V_SW + KS], 'lru_out_norm': vec[V_GLO], 'sc_out_norm': vec[V_GSO],
        'ffn2_norm': dg_ffn2, 'final_norm': dgf,
    }
    full_shapes = [local_small[n].shape for n in small] + [(1,)]
    reduced = _allreduce_small(_pack([local_small[n] for n in small] + [sqerr[0, 0:1]]), "allreduce_small")
    reduced = _unpack(reduced, full_shapes)
    loss = (0.5 / D) * reduced[-1][0]
    gsm = {}
    for n, g in zip(small, reduced[:-1]):
        if n in SMALL_SHARDED:
            g = lax.dynamic_slice(g, (0, qi * DLq), (g.shape[0], DLq))
        gsm[n] = g.reshape(w[n].shape)
    small_shapes = [w[n].shape for n in small]
    d_s, m_s, v_s, _ = _adamw(_pack([w[n] for n in small]), _pack([gsm[n] for n in small]),
                              _pack([mom[n] for n in small]), _pack([var[n] for n in small]), "adamw_small")
    for n, d, mn, vn in zip(small, _unpack(d_s, small_shapes), _unpack(m_s, small_shapes), _unpack(v_s, small_shapes)):
        out_g[n], out_d[n], out_m[n], out_v[n] = gsm[n], d, mn, vn

    return (loss, dx0.reshape(x.shape), *[out_g[n] for n in WEIGHTS], *[out_d[n] for n in WEIGHTS],
            *[out_m[n] for n in WEIGHTS], *[out_v[n] for n in WEIGHTS])
```

```python
import math

import jax
import jax.numpy as jnp
from jax import lax
from jax.experimental import pallas as pl
from jax.experimental.pallas import tpu as pltpu

F32 = jnp.float32
BF16 = jnp.bfloat16
MESH = pl.DeviceIdType.MESH
ANY = pl.BlockSpec(memory_space=pl.ANY)

NORM_EPS = 1e-6
LRU_C = 8.0
FFN_RESIDUAL_SCALE = 0.5
ADAM_LR = 0.001
ADAM_B1 = 0.9
ADAM_B2 = 0.999
ADAM_EPS = 1e-08
ADAM_WD = 0.01
ADAM_STEP = 10

V7X_VMEM_BYTES = 64 * 2**20
VMEM_LIMIT = V7X_VMEM_BYTES - 8 * 2**20
LANES = 128
SUBLANES = 8
PACK_W = LANES
PACK_ALIGN = SUBLANES * PACK_W

WEIGHTS = ['ffn1_norm', 'ffn1_w_gate', 'ffn1_w_up', 'ffn1_w_down', 'mix_norm', 'w_in', 'lru_conv_w', 'lru_conv_b',
           'lru_w_a', 'lru_b_a', 'lru_w_i', 'lru_b_i', 'lru_lambda', 'sc_conv_w', 'lru_out_norm', 'sc_out_norm',
           'w_out', 'ffn2_norm', 'ffn2_w_gate', 'ffn2_w_up', 'ffn2_w_down', 'final_norm']
BIG = ['ffn1_w_gate', 'ffn1_w_up', 'ffn1_w_down', 'w_in', 'w_out', 'ffn2_w_gate', 'ffn2_w_up', 'ffn2_w_down']
BIG_AXIS = [1, 1, 0, 1, 0, 1, 1, 0]
SMALL_SHARDED = ['lru_conv_w', 'sc_conv_w']


def _tile(n, pref, mult):
    if n <= pref:
        return n
    t = (pref // mult) * mult
    while t >= mult:
        if n % t == 0:
            return t
        t -= mult
    return n


def _params(*sem):
    return pltpu.CompilerParams(dimension_semantics=sem, vmem_limit_bytes=VMEM_LIMIT)


def _me():
    return lax.axis_index("x"), lax.axis_index("y"), lax.axis_index("c")


def _sigmoid(v):
    return 1.0 / (1.0 + jnp.exp(-v))


def _rstd(v):
    return lax.rsqrt(jnp.mean(v * v, axis=-1, keepdims=True) + NORM_EPS)


def _rms_bwd(dy, v, gain):
    r = _rstd(v)
    w = gain * dy
    dv = r * w - v * (r * r * r) * jnp.mean(v * w, axis=-1, keepdims=True)
    dgain = jnp.sum(dy * v * r, axis=0, keepdims=True)
    return dv, dgain


def _dot_nt(a, b):
    return lax.dot_general(a, b, (((1,), (1,)), ((), ())), preferred_element_type=F32)


def _dot_tn(a, b):
    return lax.dot_general(a, b, (((0,), (0,)), ((), ())), preferred_element_type=F32)


class _Carry:
    def __init__(self, inputs, out_shape, aliases, sems, start, finish, middle=None, middle_at=0.85):
        self.inputs, self.out_shape, self.aliases, self.sems = list(inputs), list(out_shape), dict(aliases), list(sems)
        self.start, self.finish = start, finish
        self.middle, self.middle_at = middle, middle_at


def _merge_carries(a, b):
    ia, oa, sa = len(a.inputs), len(a.out_shape), len(a.sems)
    aliases = dict(a.aliases)
    aliases.update({ia + i: oa + j for i, j in b.aliases.items()})

    def both(which):
        def run(ins, outs, sems):
            getattr(a, which)(ins[:ia], outs[:oa], sems[:sa])
            getattr(b, which)(ins[ia:], outs[oa:], sems[sa:])
        return run

    return _Carry(a.inputs + b.inputs, a.out_shape + b.out_shape, aliases, a.sems + b.sems, both("start"), both("finish"))


def _call(body, name, grid, in_specs, out_specs, out_shape, scratch_shapes, semantics, args, carry=None, prefetch=()):
    np_ = len(prefetch)
    if carry is None:
        spec = pltpu.PrefetchScalarGridSpec(num_scalar_prefetch=np_, grid=grid, in_specs=in_specs, out_specs=out_specs,
                                            scratch_shapes=scratch_shapes)
        return pl.pallas_call(body, name=name, grid_spec=spec, out_shape=out_shape,
                              compiler_params=_params(*semantics))(*prefetch, *args)
    ni, no, ns = len(in_specs), len(out_specs), len(scratch_shapes)
    ci, co = len(carry.inputs), len(carry.out_shape)

    def carrying(*refs):
        pre, refs = refs[:np_], refs[np_:]
        ins, refs = refs[:ni], refs[ni:]
        cins, refs = refs[:ci], refs[ci:]
        outs, refs = refs[:no], refs[no:]
        couts, refs = refs[:co], refs[co:]
        scratch, csems = refs[:ns], refs[ns:]
        step = pl.program_id(0)
        for ax in range(1, len(grid)):
            step = step * grid[ax] + pl.program_id(ax)
        steps = math.prod(grid)
        first = step == 0
        last = step == steps - 1

        @pl.when(first)
        def _():
            carry.start(cins, couts, csems)

        if carry.middle is not None:
            @pl.when(step == min(int(carry.middle_at * steps), steps - 1))
            def _():
                carry.middle(cins, couts, csems)

        body(*pre, *ins, *outs, *scratch)

        @pl.when(last)
        def _():
            carry.finish(cins, couts, csems)

    spec = pltpu.PrefetchScalarGridSpec(
        num_scalar_prefetch=np_, grid=grid, in_specs=list(in_specs) + [ANY] * ci,
        out_specs=list(out_specs) + [ANY] * co, scratch_shapes=list(scratch_shapes) + carry.sems)
    return pl.pallas_call(
        carrying, name=name, grid_spec=spec, out_shape=list(out_shape) + carry.out_shape,
        input_output_aliases={np_ + ni + i: no + j for i, j in carry.aliases.items()},
        compiler_params=_params(*(["arbitrary"] * len(grid))),
    )(*prefetch, *args, *carry.inputs)


def _run_carry(carry, name):
    ci, co = len(carry.inputs), len(carry.out_shape)

    def body(*refs):
        cins, couts, csems = refs[:ci], refs[ci:ci + co], refs[ci + co:]
        carry.start(cins, couts, csems)
        if carry.middle is not None:
            carry.middle(cins, couts, csems)
        carry.finish(cins, couts, csems)

    return pl.pallas_call(body, name=name, in_specs=[ANY] * ci, out_specs=[ANY] * co, out_shape=carry.out_shape,
                          input_output_aliases=carry.aliases, scratch_shapes=carry.sems)(*carry.inputs)


def _ffn_bwd_hidden(db, A, B, wd, name, carry=None):
    T, D = db.shape
    FF = wd.shape[0]
    tm = _tile(T, 1024, 16)
    tf = _tile(FF, 512, LANES)
    ni, nf = T // tm, FF // tf
    steps = ni * nf
    SLOTS = 3

    def body(d_ref, A_ref, B_ref, wd_hbm, dG_ref, dU_ref, ring, sems):
        s = pl.program_id(0) * nf + pl.program_id(1)

        def fetch(step):
            rows = pl.ds(pl.multiple_of((step % nf) * tf, tf), tf)
            return pltpu.make_async_copy(wd_hbm.at[rows], ring.at[step % SLOTS], sems.at[step % SLOTS])

        @pl.when(s == 0)
        def _():
            fetch(0).start()
            if steps > 1:
                fetch(1).start()

        @pl.when(s + 2 < steps)
        def _():
            fetch(s + 2).start()

        fetch(s).wait()
        dH = _dot_nt(d_ref[...], ring[s % SLOTS])
        dU_ref[...] = (dH * A_ref[...].astype(F32)).astype(BF16)
        dG_ref[...] = (dH * B_ref[...].astype(F32)).astype(BF16)

    act = pl.BlockSpec((tm, tf), lambda i, f: (i, f))
    return _call(
        body, name, (ni, nf), [pl.BlockSpec((tm, D), lambda i, f: (i, 0)), act, act, ANY],
        [act, act], [jax.ShapeDtypeStruct((T, FF), BF16)] * 2,
        [pltpu.VMEM((SLOTS, tf, D), BF16), pltpu.SemaphoreType.DMA((SLOTS,))],
        ("arbitrary", "arbitrary"), (db, A, B, wd), carry)


TAIL_ROWS = 128


class _NormBwdTail:
    def __init__(self, T, D, tm, scale):
        self.T, self.D, self.tm, self.scale = T, D, tm, scale
        self.ni = T // tm
        self.scratch = [pltpu.VMEM((tm, D), F32), pltpu.VMEM((tm, D), F32), pltpu.VMEM((tm, D), F32),
                        pltpu.VMEM((tm, D), BF16), pltpu.SemaphoreType.DMA((4,))]
        self.out_shape = [jax.ShapeDtypeStruct((T, D), F32), jax.ShapeDtypeStruct((T, D), BF16)]

    def _rows(self, k):
        return pl.ds(pl.multiple_of(k * self.tm, self.tm), self.tm)

    def _loads(self, k, x_hbm, r_hbm, bufs):
        xbuf, rbuf, _, _, sems = bufs
        return [pltpu.make_async_copy(x_hbm.at[self._rows(k)], xbuf, sems.at[0]),
                pltpu.make_async_copy(r_hbm.at[self._rows(k)], rbuf, sems.at[1])]

    def _stores(self, k, dx_hbm, dxb_hbm, bufs):
        _, _, obuf, obbuf, sems = bufs
        return [pltpu.make_async_copy(obuf, dx_hbm.at[self._rows(k)], sems.at[2]),
                pltpu.make_async_copy(obbuf, dxb_hbm.at[self._rows(k)], sems.at[3])]

    def prefetch(self, i, x_hbm, r_hbm, bufs):
        for cp in self._loads(i, x_hbm, r_hbm, bufs):
            cp.start()

    def run(self, i, acc_ref, g_ref, x_hbm, r_hbm, dx_hbm, dxb_hbm, dg_ref, bufs):
        xbuf, rbuf, obuf, obbuf, _ = bufs
        for cp in self._loads(i, x_hbm, r_hbm, bufs):
            cp.wait()

        @pl.when(i > 0)
        def _():
            for cp in self._stores(i - 1, dx_hbm, dxb_hbm, bufs):
                cp.wait()

        dgain = None
        for r0 in range(0, self.tm, TAIL_ROWS):
            rs = slice(r0, min(r0 + TAIL_ROWS, self.tm))
            dv, dgr = _rms_bwd(acc_ref[rs, :], xbuf[rs, :], g_ref[...])
            dx = rbuf[rs, :] + dv
            obuf[rs, :] = dx
            obbuf[rs, :] = (self.scale * dx).astype(BF16)
            dgain = dgr if dgain is None else dgain + dgr
        for cp in self._stores(i, dx_hbm, dxb_hbm, bufs):
            cp.start()

        @pl.when(i == 0)
        def _():
            dg_ref[...] = dgain

        @pl.when(i > 0)
        def _():
            dg_ref[...] += dgain

        @pl.when(i == self.ni - 1)
        def _():
            for cp in self._stores(i, dx_hbm, dxb_hbm, bufs):
                cp.wait()


def _mm_nt_norm_bwd(a, w, x_in, gain, dres, scale, name, carry=None):
    T, K = a.shape
    D = w.shape[0]
    whole = D * K * 2 <= 24 * 2**20
    tm = _tile(T, 256 if whole else 512, 16)
    tk = K if whole else _tile(K, 1280, LANES)
    nk = K // tk
    tail = _NormBwdTail(T, D, tm, scale)
    w_spec = (pl.BlockSpec((D, tk), lambda i, k: (0, k), pipeline_mode=pl.Buffered(1)) if whole
              else pl.BlockSpec((D, tk), lambda i, k: (0, k)))

    def body(a_ref, w_ref, g_ref, x_hbm, r_hbm, dg_ref, dx_hbm, dxb_hbm, acc_ref, *bufs):
        i, k = pl.program_id(0), pl.program_id(1)

        @pl.when(k == 0)
        def _():
            tail.prefetch(i, x_hbm, r_hbm, bufs)

        contrib = _dot_nt(a_ref[...], w_ref[...])

        @pl.when(k == 0)
        def _():
            acc_ref[...] = contrib

        @pl.when(k > 0)
        def _():
            acc_ref[...] += contrib

        @pl.when(k == nk - 1)
        def _():
            tail.run(i, acc_ref, g_ref, x_hbm, r_hbm, dx_hbm, dxb_hbm, dg_ref, bufs)

    return _call(
        body, name, (T // tm, nk),
        [pl.BlockSpec((tm, tk), lambda i, k: (i, k)), w_spec, pl.BlockSpec((1, D), lambda i, k: (0, 0)), ANY, ANY],
        [pl.BlockSpec((1, D), lambda i, k: (0, 0)), ANY, ANY],
        [jax.ShapeDtypeStruct((1, D), F32)] + tail.out_shape,
        [pltpu.VMEM((tm, D), F32)] + tail.scratch, ("arbitrary", "arbitrary"), (a, w, gain, x_in, dres), carry)


def _rms_bwd_res(dn, x, gain, dres, scale, name, carry=None):
    T, D = x.shape
    tm = _tile(T, 256, 16)

    def body(dn_ref, x_ref, g_ref, dr_ref, dx_ref, dxb_ref, dg_ref):
        i = pl.program_id(0)
        dv, dgain = _rms_bwd(dn_ref[...].astype(F32), x_ref[...], g_ref[...])
        dx = dr_ref[...] + dv
        dx_ref[...] = dx
        dxb_ref[...] = (scale * dx).astype(BF16)

        @pl.when(i == 0)
        def _():
            dg_ref[...] = dgain

        @pl.when(i > 0)
        def _():
            dg_ref[...] += dgain

    row = pl.BlockSpec((tm, D), lambda i: (i, 0))
    vec = pl.BlockSpec((1, D), lambda i: (0, 0))
    return _call(
        body, name, (T // tm,), [row, row, vec, row], [row, row, vec],
        [jax.ShapeDtypeStruct((T, D), F32), jax.ShapeDtypeStruct((T, D), BF16), jax.ShapeDtypeStruct((1, D), F32)],
        [], ("arbitrary",), (dn, x, gain, dres), carry)


def _loss_head(x3, gain, target, name):
    T, D = x3.shape
    tm = _tile(T, 256, 16)

    def body(x_ref, g_ref, t_ref, dx_ref, dxb_ref, ls_ref, dg_ref):
        i = pl.program_id(0)
        xv = x_ref[...]
        err = xv * _rstd(xv) * g_ref[...] - t_ref[...]
        sq = jnp.sum(jnp.sum(err * err, axis=1, keepdims=True), axis=0, keepdims=True)
        dv, dgain = _rms_bwd(err * (1.0 / D), xv, g_ref[...])
        dx_ref[...] = dv
        dxb_ref[...] = (FFN_RESIDUAL_SCALE * dv).astype(BF16)
        sqb = jnp.broadcast_to(sq, (1, LANES))

        @pl.when(i == 0)
        def _():
            dg_ref[...] = dgain
            ls_ref[...] = sqb

        @pl.when(i > 0)
        def _():
            dg_ref[...] += dgain
            ls_ref[...] += sqb

    row = pl.BlockSpec((tm, D), lambda i: (i, 0))
    vec = pl.BlockSpec((1, D), lambda i: (0, 0))
    return pl.pallas_call(
        body, name=name, grid=(T // tm,),
        in_specs=[row, vec, row],
        out_specs=[row, row, pl.BlockSpec((1, LANES), lambda i: (0, 0)), vec],
        out_shape=[jax.ShapeDtypeStruct((T, D), F32), jax.ShapeDtypeStruct((T, D), BF16),
                   jax.ShapeDtypeStruct((1, LANES), F32), jax.ShapeDtypeStruct((1, D), F32)],
        compiler_params=_params("arbitrary"),
    )(x3, gain, target)


WHOLE_MIN, WHOLE_MAX = 16 * 2**20, 24 * 2**20


def _fits_whole(w):
    return WHOLE_MIN < w.size * 2 <= WHOLE_MAX


def _weight_spec(block, index_map, whole):
    return pl.BlockSpec(block, index_map, pipeline_mode=pl.Buffered(1)) if whole else pl.BlockSpec(block, index_map)


def _norm_mm(x, gain, w, name, carry=None):
    T, D = x.shape
    N = w.shape[1]
    whole = _fits_whole(w)
    tm = _tile(T, 256 if whole else 512, 16)
    tn = N if whole else _tile(N, 2560, LANES)

    def body(x_ref, g_ref, w_ref, n_ref, z_ref):
        @pl.when(pl.program_id(1) == 0)
        def _():
            xv = x_ref[...]
            n_ref[...] = (xv * _rstd(xv) * g_ref[...]).astype(BF16)

        z_ref[...] = jnp.dot(n_ref[...], w_ref[...], preferred_element_type=F32).astype(BF16)

    return _call(
        body, name, (T // tm, N // tn),
        [pl.BlockSpec((tm, D), lambda i, j: (i, 0)),
         pl.BlockSpec((1, D), lambda i, j: (0, 0)),
         _weight_spec((D, tn), lambda i, j: (0, j), whole)],
        [pl.BlockSpec((tm, D), lambda i, j: (i, 0)),
         pl.BlockSpec((tm, tn), lambda i, j: (i, j))],
        [jax.ShapeDtypeStruct((T, D), BF16), jax.ShapeDtypeStruct((T, N), BF16)],
        [], ("parallel", "arbitrary"), (x, gain, w), carry)


def _ffn_up(n, G, wu, name, carry=None):
    T, D = n.shape
    FF = wu.shape[1]
    whole = _fits_whole(wu)
    tm = _tile(T, 128 if whole else 512, 16)
    tf = FF if whole else _tile(FF, 1408, LANES)

    def body(n_ref, G_ref, wu_ref, H_ref, A_ref, B_ref):
        U = jnp.dot(n_ref[...], wu_ref[...], preferred_element_type=F32)
        Gv = G_ref[...].astype(F32)
        s = _sigmoid(Gv)
        sg = Gv * s
        A_ref[...] = sg.astype(BF16)
        B_ref[...] = (U * (s * (1.0 + Gv * (1.0 - s)))).astype(BF16)
        H_ref[...] = (sg * U).astype(BF16)

    act = pl.BlockSpec((tm, tf), lambda i, f: (i, f))
    return _call(
        body, name, (T // tm, FF // tf),
        [pl.BlockSpec((tm, D), lambda i, f: (i, 0)), act, _weight_spec((D, tf), lambda i, f: (0, f), whole)],
        [act, act, act], [jax.ShapeDtypeStruct((T, FF), BF16)] * 3, [], ("parallel", "arbitrary"), (n, G, wu), carry)


def _mm_fullk(a, w, trans_w, residual, out_dtype, name, carry=None, scale=1.0):
    T, K = a.shape
    N = w.shape[0] if trans_w else w.shape[1]
    whole = _fits_whole(w)
    tm = _tile(T, 256 if whole else 512, 16)
    tn = N if whole else _tile(N, 2048 * 2560 // K, LANES)

    def body(*refs):
        if residual is None:
            a_ref, w_ref, o_ref = refs
        else:
            a_ref, w_ref, r_ref, o_ref = refs
        if trans_w:
            acc = _dot_nt(a_ref[...], w_ref[...])
        else:
            acc = jnp.dot(a_ref[...], w_ref[...], preferred_element_type=F32)
        if scale != 1.0:
            acc = scale * acc
        if residual is not None:
            acc = acc + r_ref[...]
        o_ref[...] = acc.astype(out_dtype)

    w_spec = (_weight_spec((tn, K), lambda i, j: (j, 0), whole) if trans_w
              else _weight_spec((K, tn), lambda i, j: (0, j), whole))
    in_specs = [pl.BlockSpec((tm, K), lambda i, j: (i, 0)), w_spec]
    args = [a, w]
    if residual is not None:
        in_specs.append(pl.BlockSpec((tm, tn), lambda i, j: (i, j)))
        args.append(residual)
    return _call(body, name, (T // tm, N // tn), in_specs, [pl.BlockSpec((tm, tn), lambda i, j: (i, j))],
                 [jax.ShapeDtypeStruct((T, N), out_dtype)], [], ("parallel", "arbitrary"), args, carry)


def _mm_tn_pair(a, b, c_arr, axis, name, carry=None):
    T, M = a.shape
    N = b.shape[1]
    tk = _tile(T, 2048, 16)
    nk = T // tk
    nq = 4
    if axis == 1:
        rows, cols = M // 2, N // nq
        a_spec = pl.BlockSpec((tk, rows), lambda p, j, k, cr: (k, jnp.where(p == 0, 1 - cr[0], cr[0])))
        b_spec = pl.BlockSpec((tk, cols), lambda p, j, k, cr: (k, j))
        s_shape, land_shape = (1, rows, N), (rows, N)
        s_spec = pl.BlockSpec((None, rows, cols), lambda p, j, k, cr: (0, 0, j * p))
    else:
        rows, cols = M // nq, N // 2
        a_spec = pl.BlockSpec((tk, rows), lambda p, j, k, cr: (k, j))
        b_spec = pl.BlockSpec((tk, cols), lambda p, j, k, cr: (k, jnp.where(p == 0, 1 - cr[0], cr[0])))
        s_shape, land_shape = (nq, rows, cols), (nq, rows, cols)
        s_spec = pl.BlockSpec((None, rows, cols), lambda p, j, k, cr: (j * p, 0, 0))

    def body(c_ref, a_ref, b_ref, s_ref, land, acc_ref, stage, got, send_sems, recv_sems, loc_sem):
        p, j, k = pl.program_id(0), pl.program_id(1), pl.program_id(2)
        x, y, c = _me()

        def tile(jj):
            return land.at[:, pl.ds(jj * cols, cols)] if axis == 1 else land.at[jj]

        def send(jj):
            return _remote(stage, tile(jj), send_sems, recv_sems, jj, (x, y, 1 - c))

        @pl.when(k == 0)
        def _():
            acc_ref[...] = jnp.zeros_like(acc_ref)

        acc_ref[...] += _dot_tn(a_ref[...], b_ref[...])

        def fetch(jj):
            return pltpu.make_async_copy(tile(jj), got, loc_sem.at[0])

        for jj in range(nq):
            @pl.when(jnp.logical_and(k == nk - 1, jnp.logical_and(p == 0, j == jj)))
            def _():
                if jj > 0:
                    send(jj - 1).wait_send()
                stage[...] = acc_ref[...].astype(BF16)
                send(jj).start()

            @pl.when(jnp.logical_and(k == max(nk - 2, 0), jnp.logical_and(p == 1, j == jj)))
            def _():
                if jj == 0:
                    send(nq - 1).wait_send()
                send(jj).wait_recv()
                fetch(jj).start()

            @pl.when(jnp.logical_and(k == nk - 1, jnp.logical_and(p == 1, j == jj)))
            def _():
                fetch(jj).wait()
                s_ref[...] = (acc_ref[...] + got[...].astype(F32)).astype(BF16)

    return _call(
        body, name, (2, nq, nk), [a_spec, b_spec], [s_spec, ANY],
        [jax.ShapeDtypeStruct(s_shape, BF16), jax.ShapeDtypeStruct(land_shape, BF16)],
        [pltpu.VMEM((rows, cols), F32), pltpu.VMEM((rows, cols), BF16), pltpu.VMEM((rows, cols), BF16),
         pltpu.SemaphoreType.DMA((nq,)), pltpu.SemaphoreType.DMA((nq,)), pltpu.SemaphoreType.DMA((1,))],
        ("arbitrary", "arbitrary", "arbitrary"), (a, b), carry, (c_arr,))


GELU_K = math.sqrt(2.0 / math.pi)
GELU_C = 0.044715


def _gelu_and_grad(v):
    u = GELU_K * (v + GELU_C * v * v * v)
    th = jnp.tanh(u)
    g = 0.5 * v * (1.0 + th)
    dg = 0.5 * (1.0 + th) + 0.5 * v * (1.0 - th * th) * GELU_K * (1.0 + 3.0 * GELU_C * v * v)
    return g, dg


def _neg_expm1(v):
    poly = v * (1.0 + v * (0.5 + v * (1.0 / 6 + v * (1.0 / 24 + v * (1.0 / 120 + v * (1.0 / 720))))))
    return jnp.where(v > -0.25, -poly, 1.0 - jnp.exp(v))


def _softplus_neg(lam):
    e = jnp.exp(-jnp.abs(lam))
    log1pe = jnp.where(e < 1e-4, e * (1.0 - 0.5 * e), jnp.log(1.0 + e))
    sp = jnp.maximum(-lam, 0.0) + log1pe
    dsp = -1.0 / (1.0 + jnp.exp(lam))
    return sp, dsp


def _earlier(ext, j):
    return pltpu.roll(ext, j, 0)[SUBLANES:, :]


def _later(ext, j):
    n = ext.shape[0]
    return pltpu.roll(ext, n - j, 0)[:n - SUBLANES, :]


def _taps(v, halo, K):
    ext = jnp.concatenate([halo, v], axis=0)
    return [v] + [_earlier(ext, j) for j in range(1, K)]


def _block_diag(vb, w_ref, nh, hd):
    return jnp.concatenate(
        [jnp.dot(vb[:, h * hd:(h + 1) * hd], w_ref[h], preferred_element_type=F32) for h in range(nh)], axis=1)


def _lru_gates(xc, wa_ref, ba_ref, wi_ref, bi_ref, sp, nh, hd):
    xcb = xc.astype(BF16)
    r = _sigmoid(_block_diag(xcb, wa_ref, nh, hd) + ba_ref[...])
    ig = _sigmoid(_block_diag(xcb, wi_ref, nh, hd) + bi_ref[...])
    log_a = -LRU_C * r * sp
    a = jnp.exp(log_a)
    mult = jnp.sqrt(_neg_expm1(2.0 * log_a))
    return xcb, r, ig, a, mult


def _mix_fwd(z, cw, cb, wa, ba, wi, bi, lam, sw, glo, gso, name, carry=None):
    T = z.shape[0]
    DL = cb.shape[1]
    DS = gso.shape[1]
    NH, HD = wa.shape[0], wa.shape[1]
    KL, KS = cw.shape[0], sw.shape[0]
    tt = _tile(T, 128, 16)
    o_g, o_b, o_c, o_x = DL, 2 * DL, 2 * DL + DS, 2 * DL + 2 * DS

    def body(z_ref, cw_ref, cb_ref, wa_ref, ba_ref, wi_ref, bi_ref, lam_ref, sw_ref, glo_ref, gso_ref,
             h_ref, y_ref, cx_ref, cp_ref, ch_ref):
        @pl.when(pl.program_id(0) == 0)
        def _():
            cx_ref[...] = jnp.zeros_like(cx_ref)
            cp_ref[...] = jnp.zeros_like(cp_ref)
            ch_ref[...] = jnp.zeros_like(ch_ref)

        def zcol(o, n):
            return z_ref[:, o:o + n].astype(F32)

        lx = zcol(0, DL)
        xs = _taps(lx, cx_ref[...], KL)
        cx_ref[...] = lx[tt - SUBLANES:, :]
        xc = cb_ref[...] + xs[0] * cw_ref[KL - 1:KL, :]
        for j in range(1, KL):
            xc = xc + xs[j] * cw_ref[KL - 1 - j:KL - j, :]
        sp, _ = _softplus_neg(lam_ref[...])
        _, _, ig, a, mult = _lru_gates(xc, wa_ref, ba_ref, wi_ref, bi_ref, sp, NH, HD)
        b = mult * (ig * xc)
        rows = lax.broadcasted_iota(jnp.int32, (tt, DL), 0)
        s = 1
        while s < tt:
            keep = rows >= s
            b = jnp.where(keep, a * pltpu.roll(b, s, 0) + b, b)
            a = jnp.where(keep, a * pltpu.roll(a, s, 0), a)
            s *= 2
        h = a * ch_ref[SUBLANES - 1:SUBLANES, :] + b
        ch_ref[...] = h[tt - SUBLANES:, :]
        h_ref[...] = h
        ge, _ = _gelu_and_grad(zcol(o_g, DL))
        ylru = h * ge
        y_ref[:, 0:DL] = (ylru * _rstd(ylru) * glo_ref[...]).astype(BF16)

        p = zcol(o_c, DS) * zcol(o_x, DS)
        ps = _taps(p, cp_ref[...], KS)
        cp_ref[...] = p[tt - SUBLANES:, :]
        cv = ps[0] * sw_ref[KS - 1:KS, :]
        for j in range(1, KS):
            cv = cv + ps[j] * sw_ref[KS - 1 - j:KS - j, :]
        ysc = zcol(o_b, DS) * cv
        y_ref[:, DL:DL + DS] = (ysc * _rstd(ysc) * gso_ref[...]).astype(BF16)

    def full(shape):
        return pl.BlockSpec(shape, lambda t: (0,) * len(shape))

    return _call(
        body, name, (T // tt,),
        [pl.BlockSpec((tt, z.shape[1]), lambda t: (t, 0)),
         full(cw.shape), full(cb.shape), full(wa.shape), full(ba.shape), full(wi.shape), full(bi.shape),
         full(lam.shape), full(sw.shape), full(glo.shape), full(gso.shape)],
        [pl.BlockSpec((tt, DL), lambda t: (t, 0)), pl.BlockSpec((tt, DL + DS), lambda t: (t, 0))],
        [jax.ShapeDtypeStruct((T, DL), F32), jax.ShapeDtypeStruct((T, DL + DS), BF16)],
        [pltpu.VMEM((SUBLANES, DL), F32), pltpu.VMEM((SUBLANES, DS), F32), pltpu.VMEM((SUBLANES, DL), F32)],
        ("arbitrary",), (z, cw, cb, wa, ba, wi, bi, lam, sw, glo, gso), carry)


V_BA, V_BI, V_LAM, V_CB, V_CW, V_SW, V_GLO, V_GSO, V_ROWS = 0, 1, 2, 3, 4, 8, 11, 12, 16


def _mix_bwd(z, h, dy, cw, cb, wa, ba, wi, bi, lam, sw, glo, gso, name):
    T = z.shape[0]
    DL = cb.shape[1]
    DS = gso.shape[1]
    NH, HD = wa.shape[0], wa.shape[1]
    KL, KS = cw.shape[0], sw.shape[0]
    tt = _tile(T, 64, 16)
    nt = T // tt
    ZH = 2 * SUBLANES
    o_g, o_b, o_c, o_x = DL, 2 * DL, 2 * DL + DS, 2 * DL + 2 * DS

    def body(z_ref, zh_ref, h_ref, hh_ref, dy_ref, cw_ref, cb_ref, wa_ref, ba_ref, wi_ref, bi_ref, lam_ref,
             sw_ref, glo_ref, gso_ref, dz_ref, dwa_ref, dwi_ref, vec_ref, cdx_ref, cdc_ref, cdh_ref):
        i = pl.program_id(0)
        tr = nt - 1 - i

        @pl.when(i == 0)
        def _():
            dwa_ref[...] = jnp.zeros_like(dwa_ref)
            dwi_ref[...] = jnp.zeros_like(dwi_ref)
            vec_ref[...] = jnp.zeros_like(vec_ref)
            cdx_ref[...] = jnp.zeros_like(cdx_ref)
            cdc_ref[...] = jnp.zeros_like(cdc_ref)
            cdh_ref[...] = jnp.zeros_like(cdh_ref)

        def acc_row(r, v):
            vec_ref[pl.ds(r, 1), :] += jnp.sum(v, axis=0, keepdims=True)

        has_prev = tr > 0
        rows = lax.broadcasted_iota(jnp.int32, (tt, DL), 0)

        def zcol(o, n):
            return z_ref[:, o:o + n].astype(F32)

        def zhalo(o, n):
            return jnp.where(has_prev, zh_ref[:, o:o + n].astype(F32)[SUBLANES:, :], 0.0)

        lx = zcol(0, DL)
        xs = _taps(lx, zhalo(0, DL), KL)
        xc = cb_ref[...] + xs[0] * cw_ref[KL - 1:KL, :]
        for j in range(1, KL):
            xc = xc + xs[j] * cw_ref[KL - 1 - j:KL - j, :]
        sp, dsp = _softplus_neg(lam_ref[...])
        xcb, r, ig, a, mult = _lru_gates(xc, wa_ref, ba_ref, wi_ref, bi_ref, sp, NH, HD)
        hv = h_ref[...]
        hprev = _earlier(jnp.concatenate([jnp.where(has_prev, hh_ref[...], 0.0), hv], axis=0), 1)
        gate = zcol(o_g, DL)
        ge, dge = _gelu_and_grad(gate)
        ylru = hv * ge

        d_ylru, dglo = _rms_bwd(dy_ref[:, 0:DL].astype(F32), ylru, glo_ref[...])
        vec_ref[pl.ds(V_GLO, 1), :] += dglo
        dz_ref[:, o_g:o_g + DL] = (d_ylru * hv * dge).astype(BF16)
        bq = d_ylru * ge
        aq = jnp.where(rows == tt - 1, 1.0, pltpu.roll(a, tt - 1, 0))
        s = 1
        while s < tt:
            keep = rows < tt - s
            bq = jnp.where(keep, aq * pltpu.roll(bq, tt - s, 0) + bq, bq)
            aq = jnp.where(keep, aq * pltpu.roll(aq, tt - s, 0), aq)
            s *= 2
        dhh = bq + aq * cdh_ref[0:1, :]
        cdh_ref[0:1, :] = a[0:1, :] * dhh[0:1, :]

        da = dhh * hprev
        dmult = dhh * (ig * xc)
        d_i = dhh * mult * xc
        dxc = dhh * mult * ig
        dlog = da * a - dmult * (a * a) / mult
        acc_row(V_LAM, dlog * (-LRU_C * r) * dsp)
        dpa = dlog * (-LRU_C * sp) * r * (1.0 - r)
        dpi = d_i * ig * (1.0 - ig)
        acc_row(V_BA, dpa)
        acc_row(V_BI, dpi)
        dpab = dpa.astype(BF16)
        dpib = dpi.astype(BF16)
        back = []
        for hh in range(NH):
            sl = slice(hh * HD, (hh + 1) * HD)
            dwa_ref[hh] += _dot_tn(xcb[:, sl], dpab[:, sl])
            dwi_ref[hh] += _dot_tn(xcb[:, sl], dpib[:, sl])
            back.append(_dot_nt(dpab[:, sl], wa_ref[hh]) + _dot_nt(dpib[:, sl], wi_ref[hh]))
        dxc = dxc + jnp.concatenate(back, axis=1)

        acc_row(V_CB, dxc)
        extd = jnp.concatenate([dxc, cdx_ref[...]], axis=0)
        cdx_ref[...] = dxc[0:SUBLANES, :]
        dlx = dxc * cw_ref[KL - 1:KL, :]
        acc_row(V_CW + KL - 1, dxc * xs[0])
        for j in range(1, KL):
            dlx = dlx + _later(extd, j) * cw_ref[KL - 1 - j:KL - j, :]
            acc_row(V_CW + KL - 1 - j, dxc * xs[j])
        dz_ref[:, 0:DL] = dlx.astype(BF16)

        sb = zcol(o_b, DS)
        sc = zcol(o_c, DS)
        sx = zcol(o_x, DS)
        p = sc * sx
        ps = _taps(p, zhalo(o_c, DS) * zhalo(o_x, DS), KS)
        cv = ps[0] * sw_ref[KS - 1:KS, :]
        for j in range(1, KS):
            cv = cv + ps[j] * sw_ref[KS - 1 - j:KS - j, :]
        d_ysc, dgso = _rms_bwd(dy_ref[:, DL:DL + DS].astype(F32), sb * cv, gso_ref[...])
        vec_ref[pl.ds(V_GSO, 1), :] += dgso
        dz_ref[:, o_b:o_b + DS] = (d_ysc * cv).astype(BF16)
        dcv = d_ysc * sb
        extc = jnp.concatenate([dcv, cdc_ref[...]], axis=0)
        cdc_ref[...] = dcv[0:SUBLANES, :]
        dp = dcv * sw_ref[KS - 1:KS, :]
        acc_row(V_SW + KS - 1, dcv * ps[0])
        for j in range(1, KS):
            dp = dp + _later(extc, j) * sw_ref[KS - 1 - j:KS - j, :]
            acc_row(V_SW + KS - 1 - j, dcv * ps[j])
        dz_ref[:, o_c:o_c + DS] = (dp * sx).astype(BF16)
        dz_ref[:, o_x:o_x + DS] = (dp * sc).astype(BF16)

    def full(shape):
        return pl.BlockSpec(shape, lambda t: (0,) * len(shape))

    def rev(t):
        return nt - 1 - t

    def halo(t, rows):
        return jnp.maximum(rev(t) * (tt // rows) - 1, 0)

    return pl.pallas_call(
        body, name=name, grid=(nt,),
        in_specs=[pl.BlockSpec((tt, z.shape[1]), lambda t: (rev(t), 0)),
                  pl.BlockSpec((ZH, z.shape[1]), lambda t: (halo(t, ZH), 0)),
                  pl.BlockSpec((tt, DL), lambda t: (rev(t), 0)),
                  pl.BlockSpec((SUBLANES, DL), lambda t: (halo(t, SUBLANES), 0)),
                  pl.BlockSpec((tt, DL + DS), lambda t: (rev(t), 0)),
                  full(cw.shape), full(cb.shape), full(wa.shape), full(ba.shape), full(wi.shape), full(bi.shape),
                  full(lam.shape), full(sw.shape), full(glo.shape), full(gso.shape)],
        out_specs=[pl.BlockSpec((tt, z.shape[1]), lambda t: (rev(t), 0)),
                   full(wa.shape), full(wi.shape), full((V_ROWS, DL))],
        out_shape=[jax.ShapeDtypeStruct(z.shape, BF16), jax.ShapeDtypeStruct(wa.shape, F32),
                   jax.ShapeDtypeStruct(wi.shape, F32), jax.ShapeDtypeStruct((V_ROWS, DL), F32)],
        scratch_shapes=[pltpu.VMEM((SUBLANES, DL), F32), pltpu.VMEM((SUBLANES, DS), F32),
                        pltpu.VMEM((SUBLANES, DL), F32)],
        compiler_params=_params("arbitrary"),
    )(z, z, h, h, dy, cw, cb, wa, ba, wi, bi, lam, sw, glo, gso)


def _quad_add(s, r2, qc, axis, name):
    _, R, W = r2.shape
    tr = _tile(R, 256, 16)

    def body(qc_ref, s_ref, r0_ref, r1_ref, r2_ref, o_ref):
        o_ref[...] = ((s_ref[...].astype(F32) + r0_ref[...].astype(F32)) + r1_ref[...].astype(F32)) + r2_ref[...].astype(F32)

    blk = (None, tr, W)
    if axis == 1:
        own = pl.BlockSpec(blk, lambda i, qr: (0, i, qr[0]))
    else:
        own = pl.BlockSpec(blk, lambda i, qr: (qr[0], i, 0))
    return pl.pallas_call(
        body, name=name,
        grid_spec=pltpu.PrefetchScalarGridSpec(
            num_scalar_prefetch=1, grid=(R // tr,),
            in_specs=[own] + [pl.BlockSpec(blk, lambda i, qr, j=j: (j, i, 0)) for j in range(3)],
            out_specs=pl.BlockSpec(blk, lambda i, qr: (qr[1], i, 0))),
        out_shape=jax.ShapeDtypeStruct((2, R, W), F32),
        compiler_params=_params("parallel"),
    )(qc, s, r2, r2, r2)


CAST_BLOCKS = 4


def _cast_into_full(shards, qc, axes, name, carry=None):
    M = len(shards)
    nb = CAST_BLOCKS

    def body(qc_ref, *refs):
        for s_ref, o_ref in zip(refs[:M], refs[M:]):
            o_ref[...] = s_ref[...].astype(BF16)

    in_specs, out_specs, out_shape = [], [], []
    for s, ax in zip(shards, axes):
        R, W = s.shape
        Rh = R // 2
        tr = Rh // nb
        in_specs.append(pl.BlockSpec((tr, W), lambda hf, i, qr: (hf * nb + i, 0)))
        if ax == 1:
            out_shape.append(jax.ShapeDtypeStruct((2, Rh, 4 * W), BF16))
            out_specs.append(pl.BlockSpec((None, tr, W), lambda hf, i, qr: (hf, i, qr[0])))
        else:
            out_shape.append(jax.ShapeDtypeStruct((8, Rh, W), BF16))
            out_specs.append(pl.BlockSpec((None, tr, W), lambda hf, i, qr: (2 * qr[0] + hf, i, 0)))
    return _call(body, name, (2, nb), in_specs, out_specs, out_shape, [], ("parallel", "parallel"), shards, carry, (qc,))


def _adamw(w, g, m, v, name):
    R, C = w.shape
    tr = _tile(R, 256, SUBLANES)
    tc = C // 2 if g.ndim == 3 else _tile(C, 2048, LANES)
    c1 = 1.0 - ADAM_B1 ** ADAM_STEP
    c2 = 1.0 - ADAM_B2 ** ADAM_STEP

    def body(w_ref, g_ref, m_ref, v_ref, d_ref, mo_ref, vo_ref, go_ref):
        gv = g_ref[...]
        go_ref[...] = gv
        mn = ADAM_B1 * m_ref[...] + (1.0 - ADAM_B1) * gv
        vn = ADAM_B2 * v_ref[...] + (1.0 - ADAM_B2) * (gv * gv)
        mo_ref[...] = mn
        vo_ref[...] = vn
        d_ref[...] = -ADAM_LR * ((mn / c1) / (jnp.sqrt(vn / c2) + ADAM_EPS) + ADAM_WD * w_ref[...])

    blk = pl.BlockSpec((tr, tc), lambda i, j: (i, j))
    g_blk = pl.BlockSpec((None, tr, tc), lambda i, j: (j, i, 0)) if g.ndim == 3 else blk
    sh = jax.ShapeDtypeStruct((R, C), F32)
    return pl.pallas_call(
        body, name=name, grid=(R // tr, C // tc),
        in_specs=[blk, g_blk, blk, blk], out_specs=[blk] * 4, out_shape=[sh] * 4,
        compiler_params=_params("parallel", "parallel"),
    )(w, g, m, v)


def _other_chips(x, y):
    return [(1 - x, y), (x, 1 - y), (1 - x, 1 - y)]


def _remote(src, dst, send_sems, recv_sems, idx, dev):
    return pltpu.make_async_remote_copy(src_ref=src, dst_ref=dst, send_sem=send_sems.at[idx], recv_sem=recv_sems.at[idx],
                                        device_id=dev, device_id_type=MESH)


def _gather_carry(fulls, axes, pieces=None):
    M = len(fulls)

    def win(outs, m, qq, cc):
        Rh = fulls[m].shape[1]
        k, n = pieces[m] if pieces is not None and pieces[m] is not None else (0, 1)
        rows = pl.ds(k * (Rh // n), Rh // n)
        if axes[m] == 1:
            W = fulls[m].shape[2] // 4
            return outs[m].at[cc, rows, pl.ds(pl.multiple_of(qq * W, LANES), W)]
        return outs[m].at[2 * qq + cc, rows, :]

    def ici(outs, sems, m, j, src_q):
        x, y, c = _me()
        cx, cy = _other_chips(x, y)[j]
        blk = win(outs, m, src_q, c)
        return _remote(blk, blk, sems[0], sems[1], 6 * m + j, (cx, cy, c))

    def d2d(outs, sems, m, j, half):
        x, y, c = _me()
        cx, cy = _other_chips(x, y)[j]
        blk = win(outs, m, 2 * cx + cy, half)
        return _remote(blk, blk, sems[0], sems[1], 6 * m + 3 + j, (x, y, 1 - c))

    def start(ins, outs, sems):
        x, y, c = _me()
        for m in range(M):
            for j in range(3):
                ici(outs, sems, m, j, 2 * x + y).start()

    def middle(ins, outs, sems):
        x, y, c = _me()
        for m in range(M):
            for j, (cx, cy) in enumerate(_other_chips(x, y)):
                ici(outs, sems, m, j, 2 * cx + cy).wait_recv()
                d2d(outs, sems, m, j, c).start()

    def finish(ins, outs, sems):
        x, y, c = _me()
        for m in range(M):
            for j in range(3):
                d2d(outs, sems, m, j, 1 - c).wait_recv()
        for m in range(M):
            for j in range(3):
                ici(outs, sems, m, j, 2 * x + y).wait_send()
                d2d(outs, sems, m, j, c).wait_send()

    return _Carry(fulls, [jax.ShapeDtypeStruct(f.shape, f.dtype) for f in fulls], {m: m for m in range(M)},
                  [pltpu.SemaphoreType.DMA((6 * M,)), pltpu.SemaphoreType.DMA((6 * M,))], start, finish, middle)


def _gather_two_way_carry(fulls, axes):
    M = len(fulls)

    def part(outs, m, qq, cc, p):
        Rp = fulls[m].shape[1] // 2
        rows = pl.ds(p * Rp, Rp)
        if axes[m] == 1:
            W = fulls[m].shape[2] // 4
            return outs[m].at[cc, rows, pl.ds(pl.multiple_of(qq * W, LANES), W)]
        return outs[m].at[2 * qq + cc, rows, :]

    def copies(outs, sems):
        x, y, c = _me()
        q, qx, qy, qd = 2 * x + y, 2 * (1 - x) + y, 2 * x + (1 - y), 2 * (1 - x) + (1 - y)
        xn, yn, sib = (1 - x, y, c), (x, 1 - y, c), (x, y, 1 - c)
        table = {}
        for m in range(M):
            def cp(blk, k, dev):
                return _remote(blk, blk, sems[0], sems[1], 12 * m + k, dev)
            own0, own1 = part(outs, m, q, c, 0), part(outs, m, q, c, 1)
            table[m] = dict(
                to=[cp(own0, 0, xn), cp(own1, 1, yn), cp(own1, 2, xn), cp(own0, 3, yn)],
                landed=[cp(part(outs, m, qx, c, 0), 0, xn), cp(part(outs, m, qy, c, 1), 1, yn),
                        cp(part(outs, m, qx, c, 1), 2, xn), cp(part(outs, m, qy, c, 0), 3, yn),
                        cp(part(outs, m, qd, c, 0), 4, yn), cp(part(outs, m, qd, c, 1), 5, xn)],
                passed=[cp(part(outs, m, qx, c, 0), 4, yn), cp(part(outs, m, qy, c, 1), 5, xn)],
                handed=[cp(part(outs, m, qq, c, p), 6 + k, sib)
                        for k, (qq, p) in enumerate([(qx, 0), (qy, 1), (qx, 1), (qy, 0), (qd, 0), (qd, 1)])],
                taken=[cp(part(outs, m, qq, 1 - c, p), 6 + k, sib)
                       for k, (qq, p) in enumerate([(qx, 0), (qy, 1), (qx, 1), (qy, 0), (qd, 0), (qd, 1)])])
        return table

    def start(ins, outs, sems):
        t = copies(outs, sems)
        for m in range(M):
            for cp in t[m]['to']:
                cp.start()

    def finish(ins, outs, sems):
        t = copies(outs, sems)
        for m in range(M):
            for k in range(4):
                t[m]['landed'][k].wait_recv()
                if k < 2:
                    t[m]['passed'][k].start()
                t[m]['handed'][k].start()
        for m in range(M):
            for k in (4, 5):
                t[m]['landed'][k].wait_recv()
                t[m]['handed'][k].start()
        for m in range(M):
            for cp in t[m]['taken']:
                cp.wait_recv()
            for cp in t[m]['to'] + t[m]['passed'] + t[m]['handed']:
                cp.wait_send()

    return _Carry(fulls, [jax.ShapeDtypeStruct(f.shape, f.dtype) for f in fulls], {m: m for m in range(M)},
                  [pltpu.SemaphoreType.DMA((12 * M,)), pltpu.SemaphoreType.DMA((12 * M,))], start, finish)


def _chip_exchange_carry(sums, axes):
    M = len(sums)
    out_shape = []
    for s, ax in zip(sums, axes):
        _, Rh, C = s.shape
        out_shape.append(jax.ShapeDtypeStruct((3, Rh, C // 4 if ax == 1 else C), s.dtype))

    def copies(ins, outs, sems):
        x, y, c = _me()
        cps = []
        for m in range(M):
            for j, (cx, cy) in enumerate(_other_chips(x, y)):
                qj = 2 * cx + cy
                if axes[m] == 1:
                    W = sums[m].shape[2] // 4
                    src = ins[m].at[0, :, pl.ds(pl.multiple_of(qj * W, LANES), W)]
                else:
                    src = ins[m].at[qj]
                cps.append(_remote(src, outs[m].at[j], sems[0], sems[1], 3 * m + j, (cx, cy, c)))
        return cps

    def start(ins, outs, sems):
        for cp in copies(ins, outs, sems):
            cp.start()

    def finish(ins, outs, sems):
        for cp in copies(ins, outs, sems):
            cp.wait()

    return _Carry(sums, out_shape, {}, [pltpu.SemaphoreType.DMA((3 * M,)), pltpu.SemaphoreType.DMA((3 * M,))],
                  start, finish)


def _pair_share_carry(bufs):
    M = len(bufs)

    def start(ins, outs, sems):
        x, y, c = _me()
        for m in range(M):
            _remote(outs[m].at[c], outs[m].at[c], sems[0], sems[1], m, (x, y, 1 - c)).start()

    def finish(ins, outs, sems):
        x, y, c = _me()
        for m in range(M):
            _remote(outs[m].at[c], outs[m].at[c], sems[0], sems[1], m, (x, y, 1 - c)).wait_send()
            _remote(outs[m].at[1 - c], outs[m].at[1 - c], sems[0], sems[1], m, (x, y, 1 - c)).wait_recv()

    return _Carry(bufs, [jax.ShapeDtypeStruct(b.shape, b.dtype) for b in bufs], {m: m for m in range(M)},
                  [pltpu.SemaphoreType.DMA((M,)), pltpu.SemaphoreType.DMA((M,))], start, finish)


def _allreduce_small(v, name):
    R, W = v.shape
    Rh = R // 2

    def body(v_ref, o_ref, sib, quad, send_sems, recv_sems):
        x, y, c = _me()
        q = 2 * x + y
        sibling = (x, y, 1 - c)
        pair = _remote(v_ref, sib, send_sems, recv_sems, 0, sibling)
        pair.start()
        pair.wait()
        mine = pl.ds(pl.multiple_of(c * Rh, SUBLANES), Rh)
        quad[0] = v_ref[mine, :] + sib[mine, :]
        cps = []
        for k in (1, 2, 3):
            peer = (1 - x if k & 2 else x, 1 - y if k & 1 else y, c)
            cps.append(_remote(quad.at[0], quad.at[k], send_sems, recv_sems, k, peer))
            cps[-1].start()
        for cp in cps:
            cp.wait()
        acc = quad[q]
        for p in (1, 2, 3):
            acc = acc + quad[jnp.bitwise_xor(q, p)]
        o_ref[mine, :] = acc
        theirs = pl.ds(pl.multiple_of((1 - c) * Rh, SUBLANES), Rh)
        done = _remote(o_ref.at[mine, :], o_ref.at[mine, :], send_sems, recv_sems, 4, sibling)
        done.start()
        done.wait_send()
        _remote(o_ref.at[theirs, :], o_ref.at[theirs, :], send_sems, recv_sems, 4, sibling).wait_recv()

    vm = pl.BlockSpec(memory_space=pltpu.VMEM)
    return pl.pallas_call(
        body, name=name, in_specs=[vm], out_specs=vm, out_shape=jax.ShapeDtypeStruct((R, W), F32),
        scratch_shapes=[pltpu.VMEM((R, W), F32), pltpu.VMEM((4, Rh, W), F32),
                        pltpu.SemaphoreType.DMA((5,)), pltpu.SemaphoreType.DMA((5,))],
        compiler_params=pltpu.CompilerParams(vmem_limit_bytes=VMEM_LIMIT),
    )(v)


def _pack(pieces):
    flat = []
    for p in pieces:
        p = p.reshape(-1).astype(F32)
        pad = (-p.shape[0]) % PACK_ALIGN
        flat.append(jnp.pad(p, (0, pad)).reshape(-1, PACK_W))
    if sum(f.shape[0] for f in flat) % (2 * SUBLANES):
        flat.append(jnp.zeros((SUBLANES, PACK_W), F32))
    return jnp.concatenate(flat, axis=0)


def _unpack(packed, shapes):
    out, row = [], 0
    for shp in shapes:
        n = math.prod(shp)
        rows = -(-n // PACK_ALIGN) * SUBLANES
        out.append(packed[row:row + rows].reshape(-1)[:n].reshape(shp))
        row += rows
    return out


def kernel(x, ffn1_norm, ffn1_w_gate, ffn1_w_up, ffn1_w_down, mix_norm, w_in, lru_conv_w, lru_conv_b, lru_w_a, lru_b_a, lru_w_i, lru_b_i, lru_lambda, sc_conv_w, lru_out_norm, sc_out_norm, w_out, ffn2_norm, ffn2_w_gate, ffn2_w_up, ffn2_w_down, final_norm, loss_target, m_ffn1_norm, m_ffn1_w_gate, m_ffn1_w_up, m_ffn1_w_down, m_mix_norm, m_w_in, m_lru_conv_w, m_lru_conv_b, m_lru_w_a, m_lru_b_a, m_lru_w_i, m_lru_b_i, m_lru_lambda, m_sc_conv_w, m_lru_out_norm, m_sc_out_norm, m_w_out, m_ffn2_norm, m_ffn2_w_gate, m_ffn2_w_up, m_ffn2_w_down, m_final_norm, v_ffn1_norm, v_ffn1_w_gate, v_ffn1_w_up, v_ffn1_w_down, v_mix_norm, v_w_in, v_lru_conv_w, v_lru_conv_b, v_lru_w_a, v_lru_b_a, v_lru_w_i, v_lru_b_i, v_lru_lambda, v_sc_conv_w, v_lru_out_norm, v_sc_out_norm, v_w_out, v_ffn2_norm, v_ffn2_w_gate, v_ffn2_w_up, v_ffn2_w_down, v_final_norm):
    vals = locals()
    w = {n: vals[n] for n in WEIGHTS}
    mom = {n: vals["m_" + n] for n in WEIGHTS}
    var = {n: vals["v_" + n] for n in WEIGHTS}

    xi, yi, ci = _me()
    qi = 2 * xi + yi
    c_arr = jnp.reshape(ci, (1,)).astype(jnp.int32)
    qc_arr = jnp.stack([qi, ci]).astype(jnp.int32)

    T, D = x.shape[1], x.shape[2]
    xt = x.reshape(T, D)
    target = loss_target.reshape(T, D)
    DL = lru_conv_b.shape[-1]
    NH, HD = lru_w_a.shape[1], lru_w_a.shape[2]
    KL, KS = lru_conv_w.shape[1], sc_conv_w.shape[1]
    DLq = lru_conv_w.shape[2]

    axis_of = dict(zip(BIG, BIG_AXIS))
    placed, full = {}, {}

    def gather(names, pieces={}):
        return _gather_carry([placed[n] for n in names], [axis_of[n] for n in names], [pieces.get(n) for n in names])

    def gathered(names, views, unfinished=()):
        for n, g in zip(names, views):
            if n in unfinished:
                placed[n] = g
            else:
                full[n] = (g.reshape(2 * g.shape[1], g.shape[2]) if axis_of[n] == 1
                           else g.reshape(8 * g.shape[1], g.shape[2]))

    first, rest = 'ffn1_w_gate', [n for n in BIG if n != 'ffn1_w_gate']
    placed[first] = _cast_into_full([w[first][0]], qc_arr, [axis_of[first]], "cast_first_weight")[0]
    res = _cast_into_full([w[n][0] for n in rest], qc_arr, [axis_of[n] for n in rest], "cast_other_weights",
                          _gather_two_way_carry([placed[first]], [axis_of[first]]))
    placed.update(zip(rest, res[:len(rest)]))
    gathered([first], res[len(rest):])

    taps = jnp.zeros((2 * SUBLANES, DL), F32)
    taps = lax.dynamic_update_slice(taps, lru_conv_w[0], (0, qi * DLq))
    taps = lax.dynamic_update_slice(taps, sc_conv_w[0], (KL, qi * DLq))
    taps = _allreduce_small(jnp.where(ci == 0, taps, 0.0), "gather_conv_taps")
    cw, sw = taps[0:KL], taps[KL:KL + KS]

    cb = lru_conv_b
    wa, wi = lru_w_a[0].astype(BF16), lru_w_i[0].astype(BF16)
    ba, bi = lru_b_a.reshape(1, DL), lru_b_i.reshape(1, DL)
    mix_args = (cw, cb, wa, ba, wi, bi, lru_lambda, sw, lru_out_norm, sc_out_norm)
    gf = final_norm.reshape(1, D)

    names = ['ffn1_w_up', 'w_out']
    res = _norm_mm(xt, ffn1_norm, full['ffn1_w_gate'], "ffn1_gate", gather(names))
    n1, G1 = res[:2]
    gathered(names, res[2:])
    names = ['ffn1_w_down', 'ffn2_w_down']
    res = _ffn_up(n1, G1, full['ffn1_w_up'], "ffn1_up", gather(names, {'ffn2_w_down': (0, 2)}))
    H1, A1, B1 = res[:3]
    gathered(names, res[3:], unfinished=['ffn2_w_down'])
    names = ['w_in', 'ffn2_w_down']
    res = _mm_fullk(H1, full['ffn1_w_down'], False, xt, F32, "ffn1_down", gather(names, {'ffn2_w_down': (1, 2)}),
                    FFN_RESIDUAL_SCALE)
    x1 = res[0]
    gathered(names, res[1:])
    names = ['ffn2_w_gate']
    res = _norm_mm(x1, mix_norm, full['w_in'], "mix_in_proj", gather(names))
    n2, z = res[:2]
    gathered(names, res[2:])
    names = ['ffn2_w_up']
    res = _mix_fwd(z, *mix_args, "mix_fwd", gather(names))
    h, ymix = res[:2]
    gathered(names, res[2:])
    x2 = _mm_fullk(ymix, full['w_out'], False, x1, F32, "mix_out_proj")[0]
    n3, G2 = _norm_mm(x2, ffn2_norm, full['ffn2_w_gate'], "ffn2_gate")
    H2, A2, B2 = _ffn_up(n3, G2, full['ffn2_w_up'], "ffn2_up")
    x3 = _mm_fullk(H2, full['ffn2_w_down'], False, x2, F32, "ffn2_down", None, FFN_RESIDUAL_SCALE)[0]
    dx3, d3b, sqerr, dgf = _loss_head(x3, gf, target, "loss_head")

    sums, halves, shared = {}, {}, {}

    def dw(n, a, b, name, carry=None):
        res = _mm_tn_pair(a, b, c_arr, axis_of[n], name, carry)
        sums[n] = res[0]
        return res[2:]

    def chip_carry(names):
        return _chip_exchange_carry([sums[n] for n in names], [axis_of[n] for n in names])

    def chip_add(names, recv):
        for n, r in zip(names, recv):
            halves[n] = _quad_add(sums[n], r, qc_arr, axis_of[n], "grad_chip_add_" + n)

    dG2, dU2 = _ffn_bwd_hidden(d3b, A2, B2, full['ffn2_w_down'], "ffn2_bwd_hidden")
    dn3 = _mm_fullk(dG2, full['ffn2_w_gate'], True, None, BF16, "ffn2_bwd_input_gate")[0]
    dn3 = _mm_fullk(dU2, full['ffn2_w_up'], True, dn3, BF16, "ffn2_bwd_input_up")[0]
    dx2, dx2b, dg_ffn2 = _rms_bwd_res(dn3, x2, ffn2_norm, dx3, 1.0, "ffn2_norm_bwd")
    dw('ffn2_w_gate', n3, dG2, "ffn2_dwg")
    dw('ffn2_w_up', n3, dU2, "ffn2_dwu")
    dw('ffn2_w_down', H2, d3b, "ffn2_dwd")
    dy = _mm_fullk(dx2b, full['w_out'], True, None, BF16, "mix_out_bwd")[0]
    dz, dwa, dwi, vec = _mix_bwd(z, h, dy, *mix_args, "mix_bwd")
    dg_mix, dx1, d1b = _mm_nt_norm_bwd(dz, full['w_in'], x1, mix_norm, dx2, FFN_RESIDUAL_SCALE, "mix_in_bwd")

    def share_carry(names):
        return _pair_share_carry([halves[n] for n in names])

    res = _ffn_bwd_hidden(d1b, A1, B1, full['ffn1_w_down'], "ffn1_bwd_hidden", chip_carry(['ffn2_w_gate']))
    dG1, dU1 = res[:2]
    chip_add(['ffn2_w_gate'], res[2:])
    res = _mm_fullk(dG1, full['ffn1_w_gate'], True, None, BF16, "ffn1_bwd_input_gate", chip_carry(['ffn2_w_up']))
    chip_add(['ffn2_w_up'], res[1:])
    res = _mm_fullk(dU1, full['ffn1_w_up'], True, res[0], BF16, "ffn1_bwd_input_up", chip_carry(['ffn2_w_down']))
    dn1 = res[0]
    chip_add(['ffn2_w_down'], res[1:])
    dx0, _, dg_ffn1 = _rms_bwd_res(dn1, xt, ffn1_norm, dx1, 1.0, "ffn1_norm_bwd")
    names = ['ffn2_w_gate', 'ffn2_w_up', 'ffn2_w_down']
    shared.update(zip(names, dw('ffn1_w_gate', n1, dG1, "ffn1_dwg", share_carry(names))))
    chip_add(['ffn1_w_gate'], dw('ffn1_w_up', n1, dU1, "ffn1_dwu", chip_carry(['ffn1_w_gate'])))
    chip_add(['ffn1_w_up'], dw('ffn1_w_down', H1, d1b, "ffn1_dwd", chip_carry(['ffn1_w_up'])))
    names = ['ffn1_w_gate', 'ffn1_w_up']
    res = dw('w_in', n2, dz, "mix_dwin", _merge_carries(chip_carry(['ffn1_w_down']), share_carry(names)))
    chip_add(['ffn1_w_down'], res[:1])
    shared.update(zip(names, res[1:]))
    res = dw('w_out', ymix, dx2b, "mix_dwout", _merge_carries(chip_carry(['w_in']), share_carry(['ffn1_w_down'])))
    chip_add(['w_in'], res[:1])
    shared['ffn1_w_down'] = res[1]
    chip_add(['w_out'], _run_carry(chip_carry(['w_out']), "grad_chip_exchange_w_out"))
    names = ['w_in', 'w_out']
    shared.update(zip(names, _run_carry(share_carry(names), "grad_pair_share")))
    out_g, out_d, out_m, out_v = {}, {}, {}, {}
    for n in BIG:
        shp = w[n].shape
        g = shared[n].reshape(shp[1], shp[2]) if axis_of[n] == 1 else shared[n]
        outs = _adamw(w[n][0], g, mom[n][0], var[n][0], "adamw_" + n)
        out_d[n], out_m[n], out_v[n], out_g[n] = (a.reshape(shp) for a in outs)

    small = [n for n in WEIGHTS if n not in BIG]
    local_small = {
        'ffn1_norm': dg_ffn1, 'mix_norm': dg_mix, 'lru_conv_w': vec[V_CW:V_CW + KL], 'lru_conv_b': vec[V_CB],
        'lru_w_a': dwa, 'lru_b_a': vec[V_BA], 'lru_w_i': dwi, 'lru_b_i': vec[V_BI], 'lru_lambda': vec[V_LAM],
        'sc_conv_w': vec[V_SW:V_SW + KS], 'lru_out_norm': vec[V_GLO], 'sc_out_norm': vec[V_GSO],
        'ffn2_norm': dg_ffn2, 'final_norm': dgf,
    }
    full_shapes = [local_small[n].shape for n in small] + [(1,)]
    reduced = _allreduce_small(_pack([local_small[n] for n in small] + [sqerr[0, 0:1]]), "allreduce_small")
    reduced = _unpack(reduced, full_shapes)
    loss = (0.5 / D) * reduced[-1][0]
    gsm = {}
    for n, g in zip(small, reduced[:-1]):
        if n in SMALL_SHARDED:
            g = lax.dynamic_slice(g, (0, qi * DLq), (g.shape[0], DLq))
        gsm[n] = g.reshape(w[n].shape)
    small_shapes = [w[n].shape for n in small]
    d_s, m_s, v_s, _ = _adamw(_pack([w[n] for n in small]), _pack([gsm[n] for n in small]),
                              _pack([mom[n] for n in small]), _pack([var[n] for n in small]), "adamw_small")
    for n, d, mn, vn in zip(small, _unpack(d_s, small_shapes), _unpack(m_s, small_shapes), _unpack(v_s, small_shapes)):
        out_g[n], out_d[n], out_m[n], out_v[n] = gsm[n], d, mn, vn

    return (loss, dx0.reshape(x.shape), *[out_g[n] for n in WEIGHTS], *[out_d[n] for n in WEIGHTS],
            *[out_m[n] for n in WEIGHTS], *[out_v[n] for n in WEIGHTS])
```

```python
import math

import jax
import jax.numpy as jnp
from jax import lax
from jax.experimental import pallas as pl
from jax.experimental.pallas import tpu as pltpu

F32 = jnp.float32
BF16 = jnp.bfloat16
MESH = pl.DeviceIdType.MESH
ANY = pl.BlockSpec(memory_space=pl.ANY)

NORM_EPS = 1e-6
LRU_C = 8.0
FFN_RESIDUAL_SCALE = 0.5
ADAM_LR = 0.001
ADAM_B1 = 0.9
ADAM_B2 = 0.999
ADAM_EPS = 1e-08
ADAM_WD = 0.01
ADAM_STEP = 10

V7X_VMEM_BYTES = 64 * 2**20
VMEM_LIMIT = V7X_VMEM_BYTES - 8 * 2**20
LANES = 128
SUBLANES = 8
PACK_W = LANES
PACK_ALIGN = SUBLANES * PACK_W

WEIGHTS = ['ffn1_norm', 'ffn1_w_gate', 'ffn1_w_up', 'ffn1_w_down', 'mix_norm', 'w_in', 'lru_conv_w', 'lru_conv_b',
           'lru_w_a', 'lru_b_a', 'lru_w_i', 'lru_b_i', 'lru_lambda', 'sc_conv_w', 'lru_out_norm', 'sc_out_norm',
           'w_out', 'ffn2_norm', 'ffn2_w_gate', 'ffn2_w_up', 'ffn2_w_down', 'final_norm']
BIG = ['ffn1_w_gate', 'ffn1_w_up', 'ffn1_w_down', 'w_in', 'w_out', 'ffn2_w_gate', 'ffn2_w_up', 'ffn2_w_down']
BIG_AXIS = [1, 1, 0, 1, 0, 1, 1, 0]
SMALL_SHARDED = ['lru_conv_w', 'sc_conv_w']


def _tile(n, pref, mult):
    if n <= pref:
        return n
    t = (pref // mult) * mult
    while t >= mult:
        if n % t == 0:
            return t
        t -= mult
    return n


def _params(*sem):
    return pltpu.CompilerParams(dimension_semantics=sem, vmem_limit_bytes=VMEM_LIMIT)


def _me():
    return lax.axis_index("x"), lax.axis_index("y"), lax.axis_index("c")


def _sigmoid(v):
    return 1.0 / (1.0 + jnp.exp(-v))


def _rstd(v):
    return lax.rsqrt(jnp.mean(v * v, axis=-1, keepdims=True) + NORM_EPS)


def _rms_bwd(dy, v, gain):
    r = _rstd(v)
    w = gain * dy
    dv = r * w - v * (r * r * r) * jnp.mean(v * w, axis=-1, keepdims=True)
    dgain = jnp.sum(dy * v * r, axis=0, keepdims=True)
    return dv, dgain


def _dot_nt(a, b):
    return lax.dot_general(a, b, (((1,), (1,)), ((), ())), preferred_element_type=F32)


def _dot_tn(a, b):
    return lax.dot_general(a, b, (((0,), (0,)), ((), ())), preferred_element_type=F32)


class _Carry:
    def __init__(self, inputs, out_shape, aliases, sems, start, finish, middle=None, middle_at=0.85):
        self.inputs, self.out_shape, self.aliases, self.sems = list(inputs), list(out_shape), dict(aliases), list(sems)
        self.start, self.finish = start, finish
        self.middle, self.middle_at = middle, middle_at


def _merge_carries(a, b):
    ia, oa, sa = len(a.inputs), len(a.out_shape), len(a.sems)
    aliases = dict(a.aliases)
    aliases.update({ia + i: oa + j for i, j in b.aliases.items()})

    def both(which):
        def run(ins, outs, sems):
            getattr(a, which)(ins[:ia], outs[:oa], sems[:sa])
            getattr(b, which)(ins[ia:], outs[oa:], sems[sa:])
        return run

    return _Carry(a.inputs + b.inputs, a.out_shape + b.out_shape, aliases, a.sems + b.sems, both("start"), both("finish"))


def _call(body, name, grid, in_specs, out_specs, out_shape, scratch_shapes, semantics, args, carry=None, prefetch=()):
    np_ = len(prefetch)
    if carry is None:
        spec = pltpu.PrefetchScalarGridSpec(num_scalar_prefetch=np_, grid=grid, in_specs=in_specs, out_specs=out_specs,
                                            scratch_shapes=scratch_shapes)
        return pl.pallas_call(body, name=name, grid_spec=spec, out_shape=out_shape,
                              compiler_params=_params(*semantics))(*prefetch, *args)
    ni, no, ns = len(in_specs), len(out_specs), len(scratch_shapes)
    ci, co = len(carry.inputs), len(carry.out_shape)

    def carrying(*refs):
        pre, refs = refs[:np_], refs[np_:]
        ins, refs = refs[:ni], refs[ni:]
        cins, refs = refs[:ci], refs[ci:]
        outs, refs = refs[:no], refs[no:]
        couts, refs = refs[:co], refs[co:]
        scratch, csems = refs[:ns], refs[ns:]
        step = pl.program_id(0)
        for ax in range(1, len(grid)):
            step = step * grid[ax] + pl.program_id(ax)
        steps = math.prod(grid)
        first = step == 0
        last = step == steps - 1

        @pl.when(first)
        def _():
            carry.start(cins, couts, csems)

        if carry.middle is not None:
            @pl.when(step == min(int(carry.middle_at * steps), steps - 1))
            def _():
                carry.middle(cins, couts, csems)

        body(*pre, *ins, *outs, *scratch)

        @pl.when(last)
        def _():
            carry.finish(cins, couts, csems)

    spec = pltpu.PrefetchScalarGridSpec(
        num_scalar_prefetch=np_, grid=grid, in_specs=list(in_specs) + [ANY] * ci,
        out_specs=list(out_specs) + [ANY] * co, scratch_shapes=list(scratch_shapes) + carry.sems)
    return pl.pallas_call(
        carrying, name=name, grid_spec=spec, out_shape=list(out_shape) + carry.out_shape,
        input_output_aliases={np_ + ni + i: no + j for i, j in carry.aliases.items()},
        compiler_params=_params(*(["arbitrary"] * len(grid))),
    )(*prefetch, *args, *carry.inputs)


def _run_carry(carry, name):
    ci, co = len(carry.inputs), len(carry.out_shape)

    def body(*refs):
        cins, couts, csems = refs[:ci], refs[ci:ci + co], refs[ci + co:]
        carry.start(cins, couts, csems)
        if carry.middle is not None:
            carry.middle(cins, couts, csems)
        carry.finish(cins, couts, csems)

    return pl.pallas_call(body, name=name, in_specs=[ANY] * ci, out_specs=[ANY] * co, out_shape=carry.out_shape,
                          input_output_aliases=carry.aliases, scratch_shapes=carry.sems)(*carry.inputs)


def _ffn_bwd_hidden(db, A, B, wd, name, carry=None):
    T, D = db.shape
    FF = wd.shape[0]
    tm = _tile(T, 1024, 16)
    tf = _tile(FF, 512, LANES)
    ni, nf = T // tm, FF // tf
    steps = ni * nf
    SLOTS = 3

    def body(d_ref, A_hbm, B_hbm, wd_hbm, dG_ref, dU_ref, ring, ring_a, ring_b, sems):
        s = pl.program_id(0) * nf + pl.program_id(1)

        def fetch(step):
            slot = step % SLOTS
            rows = pl.ds(pl.multiple_of((step % nf) * tf, tf), tf)
            toks = pl.ds(pl.multiple_of((step // nf) * tm, tm), tm)
            cols = pl.ds(pl.multiple_of((step % nf) * tf, LANES), tf)
            return [pltpu.make_async_copy(wd_hbm.at[rows], ring.at[slot], sems.at[slot]),
                    pltpu.make_async_copy(A_hbm.at[toks, cols], ring_a.at[slot], sems.at[SLOTS + slot]),
                    pltpu.make_async_copy(B_hbm.at[toks, cols], ring_b.at[slot], sems.at[2 * SLOTS + slot])]

        @pl.when(s == 0)
        def _():
            for cp in fetch(0) + (fetch(1) if steps > 1 else []):
                cp.start()

        @pl.when(s + 2 < steps)
        def _():
            for cp in fetch(s + 2):
                cp.start()

        for cp in fetch(s):
            cp.wait()
        dH = _dot_nt(d_ref[...], ring[s % SLOTS])
        dU_ref[...] = (dH * ring_a[s % SLOTS].astype(F32)).astype(BF16)
        dG_ref[...] = (dH * ring_b[s % SLOTS].astype(F32)).astype(BF16)

    act = pl.BlockSpec((tm, tf), lambda i, f: (i, f))
    return _call(
        body, name, (ni, nf), [pl.BlockSpec((tm, D), lambda i, f: (i, 0)), ANY, ANY, ANY],
        [act, act], [jax.ShapeDtypeStruct((T, FF), BF16)] * 2,
        [pltpu.VMEM((SLOTS, tf, D), BF16), pltpu.VMEM((SLOTS, tm, tf), BF16), pltpu.VMEM((SLOTS, tm, tf), BF16),
         pltpu.SemaphoreType.DMA((3 * SLOTS,))],
        ("arbitrary", "arbitrary"), (db, A, B, wd), carry)


TAIL_ROWS = 128


class _NormBwdTail:
    def __init__(self, T, D, tm, scale):
        self.T, self.D, self.tm, self.scale = T, D, tm, scale
        self.ni = T // tm
        self.scratch = [pltpu.VMEM((tm, D), F32), pltpu.VMEM((tm, D), F32), pltpu.VMEM((tm, D), F32),
                        pltpu.VMEM((tm, D), BF16), pltpu.SemaphoreType.DMA((4,))]
        self.out_shape = [jax.ShapeDtypeStruct((T, D), F32), jax.ShapeDtypeStruct((T, D), BF16)]

    def _rows(self, k):
        return pl.ds(pl.multiple_of(k * self.tm, self.tm), self.tm)

    def _loads(self, k, x_hbm, r_hbm, bufs):
        xbuf, rbuf, _, _, sems = bufs
        return [pltpu.make_async_copy(x_hbm.at[self._rows(k)], xbuf, sems.at[0]),
                pltpu.make_async_copy(r_hbm.at[self._rows(k)], rbuf, sems.at[1])]

    def _stores(self, k, dx_hbm, dxb_hbm, bufs):
        _, _, obuf, obbuf, sems = bufs
        return [pltpu.make_async_copy(obuf, dx_hbm.at[self._rows(k)], sems.at[2]),
                pltpu.make_async_copy(obbuf, dxb_hbm.at[self._rows(k)], sems.at[3])]

    def prefetch(self, i, x_hbm, r_hbm, bufs):
        for cp in self._loads(i, x_hbm, r_hbm, bufs):
            cp.start()

    def run(self, i, acc_ref, g_ref, x_hbm, r_hbm, dx_hbm, dxb_hbm, dg_ref, bufs):
        xbuf, rbuf, obuf, obbuf, _ = bufs
        for cp in self._loads(i, x_hbm, r_hbm, bufs):
            cp.wait()

        @pl.when(i > 0)
        def _():
            for cp in self._stores(i - 1, dx_hbm, dxb_hbm, bufs):
                cp.wait()

        dgain = None
        for r0 in range(0, self.tm, TAIL_ROWS):
            rs = slice(r0, min(r0 + TAIL_ROWS, self.tm))
            dv, dgr = _rms_bwd(acc_ref[rs, :], xbuf[rs, :], g_ref[...])
            dx = rbuf[rs, :] + dv
            obuf[rs, :] = dx
            obbuf[rs, :] = (self.scale * dx).astype(BF16)
            dgain = dgr if dgain is None else dgain + dgr
        for cp in self._stores(i, dx_hbm, dxb_hbm, bufs):
            cp.start()

        @pl.when(i == 0)
        def _():
            dg_ref[...] = dgain

        @pl.when(i > 0)
        def _():
            dg_ref[...] += dgain

        @pl.when(i == self.ni - 1)
        def _():
            for cp in self._stores(i, dx_hbm, dxb_hbm, bufs):
                cp.wait()


def _mm_nt_norm_bwd(a, w, x_in, gain, dres, scale, name, carry=None):
    T, K = a.shape
    D = w.shape[0]
    whole = D * K * 2 <= 24 * 2**20
    tm = _tile(T, 256 if whole else 512, 16)
    tk = K if whole else _tile(K, 1280, LANES)
    nk = K // tk
    tail = _NormBwdTail(T, D, tm, scale)
    w_spec = (pl.BlockSpec((D, tk), lambda i, k: (0, k), pipeline_mode=pl.Buffered(1)) if whole
              else pl.BlockSpec((D, tk), lambda i, k: (0, k)))

    def body(a_ref, w_ref, g_ref, x_hbm, r_hbm, dg_ref, dx_hbm, dxb_hbm, acc_ref, *bufs):
        i, k = pl.program_id(0), pl.program_id(1)

        @pl.when(k == 0)
        def _():
            tail.prefetch(i, x_hbm, r_hbm, bufs)

        contrib = _dot_nt(a_ref[...], w_ref[...])

        @pl.when(k == 0)
        def _():
            acc_ref[...] = contrib

        @pl.when(k > 0)
        def _():
            acc_ref[...] += contrib

        @pl.when(k == nk - 1)
        def _():
            tail.run(i, acc_ref, g_ref, x_hbm, r_hbm, dx_hbm, dxb_hbm, dg_ref, bufs)

    return _call(
        body, name, (T // tm, nk),
        [pl.BlockSpec((tm, tk), lambda i, k: (i, k)), w_spec, pl.BlockSpec((1, D), lambda i, k: (0, 0)), ANY, ANY],
        [pl.BlockSpec((1, D), lambda i, k: (0, 0)), ANY, ANY],
        [jax.ShapeDtypeStruct((1, D), F32)] + tail.out_shape,
        [pltpu.VMEM((tm, D), F32)] + tail.scratch, ("arbitrary", "arbitrary"), (a, w, gain, x_in, dres), carry)


def _rms_bwd_res(dn, x, gain, dres, scale, name, carry=None):
    T, D = x.shape
    tm = _tile(T, 256, 16)

    def body(dn_ref, x_ref, g_ref, dr_ref, dx_ref, dxb_ref, dg_ref):
        i = pl.program_id(0)
        dv, dgain = _rms_bwd(dn_ref[...].astype(F32), x_ref[...], g_ref[...])
        dx = dr_ref[...] + dv
        dx_ref[...] = dx
        dxb_ref[...] = (scale * dx).astype(BF16)

        @pl.when(i == 0)
        def _():
            dg_ref[...] = dgain

        @pl.when(i > 0)
        def _():
            dg_ref[...] += dgain

    row = pl.BlockSpec((tm, D), lambda i: (i, 0))
    vec = pl.BlockSpec((1, D), lambda i: (0, 0))
    return _call(
        body, name, (T // tm,), [row, row, vec, row], [row, row, vec],
        [jax.ShapeDtypeStruct((T, D), F32), jax.ShapeDtypeStruct((T, D), BF16), jax.ShapeDtypeStruct((1, D), F32)],
        [], ("arbitrary",), (dn, x, gain, dres), carry)


def _loss_head(x3, gain, target, name):
    T, D = x3.shape
    tm = _tile(T, 256, 16)

    def body(x_ref, g_ref, t_ref, dx_ref, dxb_ref, ls_ref, dg_ref):
        i = pl.program_id(0)
        xv = x_ref[...]
        err = xv * _rstd(xv) * g_ref[...] - t_ref[...]
        sq = jnp.sum(jnp.sum(err * err, axis=1, keepdims=True), axis=0, keepdims=True)
        dv, dgain = _rms_bwd(err * (1.0 / D), xv, g_ref[...])
        dx_ref[...] = dv
        dxb_ref[...] = (FFN_RESIDUAL_SCALE * dv).astype(BF16)
        sqb = jnp.broadcast_to(sq, (1, LANES))

        @pl.when(i == 0)
        def _():
            dg_ref[...] = dgain
            ls_ref[...] = sqb

        @pl.when(i > 0)
        def _():
            dg_ref[...] += dgain
            ls_ref[...] += sqb

    row = pl.BlockSpec((tm, D), lambda i: (i, 0))
    vec = pl.BlockSpec((1, D), lambda i: (0, 0))
    return pl.pallas_call(
        body, name=name, grid=(T // tm,),
        in_specs=[row, vec, row],
        out_specs=[row, row, pl.BlockSpec((1, LANES), lambda i: (0, 0)), vec],
        out_shape=[jax.ShapeDtypeStruct((T, D), F32), jax.ShapeDtypeStruct((T, D), BF16),
                   jax.ShapeDtypeStruct((1, LANES), F32), jax.ShapeDtypeStruct((1, D), F32)],
        compiler_params=_params("arbitrary"),
    )(x3, gain, target)


WHOLE_MIN, WHOLE_MAX = 16 * 2**20, 24 * 2**20


def _fits_whole(w):
    return WHOLE_MIN < w.size * 2 <= WHOLE_MAX


def _weight_spec(block, index_map, whole):
    return pl.BlockSpec(block, index_map, pipeline_mode=pl.Buffered(1)) if whole else pl.BlockSpec(block, index_map)


def _norm_mm(x, gain, w, name, carry=None):
    T, D = x.shape
    N = w.shape[1]
    whole = _fits_whole(w)
    tm = _tile(T, 256 if whole else 512, 16)
    tn = N if whole else _tile(N, 2560, LANES)

    def body(x_ref, g_ref, w_ref, n_ref, z_ref):
        @pl.when(pl.program_id(1) == 0)
        def _():
            xv = x_ref[...]
            n_ref[...] = (xv * _rstd(xv) * g_ref[...]).astype(BF16)

        z_ref[...] = jnp.dot(n_ref[...], w_ref[...], preferred_element_type=F32).astype(BF16)

    return _call(
        body, name, (T // tm, N // tn),
        [pl.BlockSpec((tm, D), lambda i, j: (i, 0)),
         pl.BlockSpec((1, D), lambda i, j: (0, 0)),
         _weight_spec((D, tn), lambda i, j: (0, j), whole)],
        [pl.BlockSpec((tm, D), lambda i, j: (i, 0)),
         pl.BlockSpec((tm, tn), lambda i, j: (i, j))],
        [jax.ShapeDtypeStruct((T, D), BF16), jax.ShapeDtypeStruct((T, N), BF16)],
        [], ("parallel", "arbitrary"), (x, gain, w), carry)


def _ffn_up(n, G, wu, name, carry=None):
    T, D = n.shape
    FF = wu.shape[1]
    whole = _fits_whole(wu)
    tm = _tile(T, 128 if whole else 512, 16)
    tf = FF if whole else _tile(FF, 1408, LANES)

    def body(n_ref, G_ref, wu_ref, H_ref, A_ref, B_ref):
        U = jnp.dot(n_ref[...], wu_ref[...], preferred_element_type=F32)
        Gv = G_ref[...].astype(F32)
        s = _sigmoid(Gv)
        sg = Gv * s
        A_ref[...] = sg.astype(BF16)
        B_ref[...] = (U * (s * (1.0 + Gv * (1.0 - s)))).astype(BF16)
        H_ref[...] = (sg * U).astype(BF16)

    act = pl.BlockSpec((tm, tf), lambda i, f: (i, f))
    return _call(
        body, name, (T // tm, FF // tf),
        [pl.BlockSpec((tm, D), lambda i, f: (i, 0)), act, _weight_spec((D, tf), lambda i, f: (0, f), whole)],
        [act, act, act], [jax.ShapeDtypeStruct((T, FF), BF16)] * 3, [], ("parallel", "arbitrary"), (n, G, wu), carry)


def _mm_fullk(a, w, trans_w, residual, out_dtype, name, carry=None, scale=1.0):
    T, K = a.shape
    N = w.shape[0] if trans_w else w.shape[1]
    whole = _fits_whole(w)
    tm = _tile(T, 256 if whole else 512, 16)
    tn = N if whole else _tile(N, 2048 * 2560 // K, LANES)

    def body(*refs):
        if residual is None:
            a_ref, w_ref, o_ref = refs
        else:
            a_ref, w_ref, r_ref, o_ref = refs
        if trans_w:
            acc = _dot_nt(a_ref[...], w_ref[...])
        else:
            acc = jnp.dot(a_ref[...], w_ref[...], preferred_element_type=F32)
        if scale != 1.0:
            acc = scale * acc
        if residual is not None:
            acc = acc + r_ref[...]
        o_ref[...] = acc.astype(out_dtype)

    w_spec = (_weight_spec((tn, K), lambda i, j: (j, 0), whole) if trans_w
              else _weight_spec((K, tn), lambda i, j: (0, j), whole))
    in_specs = [pl.BlockSpec((tm, K), lambda i, j: (i, 0)), w_spec]
    args = [a, w]
    if residual is not None:
        in_specs.append(pl.BlockSpec((tm, tn), lambda i, j: (i, j)))
        args.append(residual)
    return _call(body, name, (T // tm, N // tn), in_specs, [pl.BlockSpec((tm, tn), lambda i, j: (i, j))],
                 [jax.ShapeDtypeStruct((T, N), out_dtype)], [], ("parallel", "arbitrary"), args, carry)


def _mm_tn_pair(a, b, c_arr, axis, name, carry=None):
    T, M = a.shape
    N = b.shape[1]
    tk = _tile(T, 2048, 16)
    nk = T // tk
    nq = 4
    if axis == 1:
        rows, cols = M // 2, N // nq
        a_spec = pl.BlockSpec((tk, rows), lambda p, j, k, cr: (k, jnp.where(p == 0, 1 - cr[0], cr[0])))
        b_spec = pl.BlockSpec((tk, cols), lambda p, j, k, cr: (k, j))
        s_shape, land_shape = (1, rows, N), (rows, N)
        s_spec = pl.BlockSpec((None, rows, cols), lambda p, j, k, cr: (0, 0, j * p))
    else:
        rows, cols = M // nq, N // 2
        a_spec = pl.BlockSpec((tk, rows), lambda p, j, k, cr: (k, j))
        b_spec = pl.BlockSpec((tk, cols), lambda p, j, k, cr: (k, jnp.where(p == 0, 1 - cr[0], cr[0])))
        s_shape, land_shape = (nq, rows, cols), (nq, rows, cols)
        s_spec = pl.BlockSpec((None, rows, cols), lambda p, j, k, cr: (j * p, 0, 0))

    def body(c_ref, a_ref, b_ref, s_ref, land, acc_ref, stage, got, send_sems, recv_sems, loc_sem):
        p, j, k = pl.program_id(0), pl.program_id(1), pl.program_id(2)
        x, y, c = _me()

        def tile(jj):
            return land.at[:, pl.ds(jj * cols, cols)] if axis == 1 else land.at[jj]

        def send(jj):
            return _remote(stage, tile(jj), send_sems, recv_sems, jj, (x, y, 1 - c))

        @pl.when(k == 0)
        def _():
            acc_ref[...] = jnp.zeros_like(acc_ref)

        acc_ref[...] += _dot_tn(a_ref[...], b_ref[...])

        def fetch(jj):
            return pltpu.make_async_copy(tile(jj), got, loc_sem.at[0])

        for jj in range(nq):
            @pl.when(jnp.logical_and(k == nk - 1, jnp.logical_and(p == 0, j == jj)))
            def _():
                if jj > 0:
                    send(jj - 1).wait_send()
                stage[...] = acc_ref[...].astype(BF16)
                send(jj).start()

            @pl.when(jnp.logical_and(k == max(nk - 2, 0), jnp.logical_and(p == 1, j == jj)))
            def _():
                if jj == 0:
                    send(nq - 1).wait_send()
                send(jj).wait_recv()
                fetch(jj).start()

            @pl.when(jnp.logical_and(k == nk - 1, jnp.logical_and(p == 1, j == jj)))
            def _():
                fetch(jj).wait()
                s_ref[...] = (acc_ref[...] + got[...].astype(F32)).astype(BF16)

    return _call(
        body, name, (2, nq, nk), [a_spec, b_spec], [s_spec, ANY],
        [jax.ShapeDtypeStruct(s_shape, BF16), jax.ShapeDtypeStruct(land_shape, BF16)],
        [pltpu.VMEM((rows, cols), F32), pltpu.VMEM((rows, cols), BF16), pltpu.VMEM((rows, cols), BF16),
         pltpu.SemaphoreType.DMA((nq,)), pltpu.SemaphoreType.DMA((nq,)), pltpu.SemaphoreType.DMA((1,))],
        ("arbitrary", "arbitrary", "arbitrary"), (a, b), carry, (c_arr,))


GELU_K = math.sqrt(2.0 / math.pi)
GELU_C = 0.044715


def _gelu_and_grad(v):
    u = GELU_K * (v + GELU_C * v * v * v)
    th = jnp.tanh(u)
    g = 0.5 * v * (1.0 + th)
    dg = 0.5 * (1.0 + th) + 0.5 * v * (1.0 - th * th) * GELU_K * (1.0 + 3.0 * GELU_C * v * v)
    return g, dg


def _neg_expm1(v):
    poly = v * (1.0 + v * (0.5 + v * (1.0 / 6 + v * (1.0 / 24 + v * (1.0 / 120 + v * (1.0 / 720))))))
    return jnp.where(v > -0.25, -poly, 1.0 - jnp.exp(v))


def _softplus_neg(lam):
    e = jnp.exp(-jnp.abs(lam))
    log1pe = jnp.where(e < 1e-4, e * (1.0 - 0.5 * e), jnp.log(1.0 + e))
    sp = jnp.maximum(-lam, 0.0) + log1pe
    dsp = -1.0 / (1.0 + jnp.exp(lam))
    return sp, dsp


def _earlier(ext, j):
    return pltpu.roll(ext, j, 0)[SUBLANES:, :]


def _later(ext, j):
    n = ext.shape[0]
    return pltpu.roll(ext, n - j, 0)[:n - SUBLANES, :]


def _taps(v, halo, K):
    ext = jnp.concatenate([halo, v], axis=0)
    return [v] + [_earlier(ext, j) for j in range(1, K)]


def _block_diag(vb, w_ref, nh, hd):
    return jnp.concatenate(
        [jnp.dot(vb[:, h * hd:(h + 1) * hd], w_ref[h], preferred_element_type=F32) for h in range(nh)], axis=1)


def _lru_gates(xc, wa_ref, ba_ref, wi_ref, bi_ref, sp, nh, hd):
    xcb = xc.astype(BF16)
    r = _sigmoid(_block_diag(xcb, wa_ref, nh, hd) + ba_ref[...])
    ig = _sigmoid(_block_diag(xcb, wi_ref, nh, hd) + bi_ref[...])
    log_a = -LRU_C * r * sp
    a = jnp.exp(log_a)
    mult = jnp.sqrt(_neg_expm1(2.0 * log_a))
    return xcb, r, ig, a, mult


def _mix_fwd(z, cw, cb, wa, ba, wi, bi, lam, sw, glo, gso, name, carry=None):
    T = z.shape[0]
    DL = cb.shape[1]
    DS = gso.shape[1]
    NH, HD = wa.shape[0], wa.shape[1]
    KL, KS = cw.shape[0], sw.shape[0]
    tt = _tile(T, 128, 16)
    o_g, o_b, o_c, o_x = DL, 2 * DL, 2 * DL + DS, 2 * DL + 2 * DS

    def body(z_ref, cw_ref, cb_ref, wa_ref, ba_ref, wi_ref, bi_ref, lam_ref, sw_ref, glo_ref, gso_ref,
             h_ref, y_ref, cx_ref, cp_ref, ch_ref):
        @pl.when(pl.program_id(0) == 0)
        def _():
            cx_ref[...] = jnp.zeros_like(cx_ref)
            cp_ref[...] = jnp.zeros_like(cp_ref)
            ch_ref[...] = jnp.zeros_like(ch_ref)

        def zcol(o, n):
            return z_ref[:, o:o + n].astype(F32)

        lx = zcol(0, DL)
        xs = _taps(lx, cx_ref[...], KL)
        cx_ref[...] = lx[tt - SUBLANES:, :]
        xc = cb_ref[...] + xs[0] * cw_ref[KL - 1:KL, :]
        for j in range(1, KL):
            xc = xc + xs[j] * cw_ref[KL - 1 - j:KL - j, :]
        sp, _ = _softplus_neg(lam_ref[...])
        _, _, ig, a, mult = _lru_gates(xc, wa_ref, ba_ref, wi_ref, bi_ref, sp, NH, HD)
        b = mult * (ig * xc)
        rows = lax.broadcasted_iota(jnp.int32, (tt, DL), 0)
        s = 1
        while s < tt:
            keep = rows >= s
            b = jnp.where(keep, a * pltpu.roll(b, s, 0) + b, b)
            a = jnp.where(keep, a * pltpu.roll(a, s, 0), a)
            s *= 2
        h = a * ch_ref[SUBLANES - 1:SUBLANES, :] + b
        ch_ref[...] = h[tt - SUBLANES:, :]
        h_ref[...] = h
        ge, _ = _gelu_and_grad(zcol(o_g, DL))
        ylru = h * ge
        y_ref[:, 0:DL] = (ylru * _rstd(ylru) * glo_ref[...]).astype(BF16)

        p = zcol(o_c, DS) * zcol(o_x, DS)
        ps = _taps(p, cp_ref[...], KS)
        cp_ref[...] = p[tt - SUBLANES:, :]
        cv = ps[0] * sw_ref[KS - 1:KS, :]
        for j in range(1, KS):
            cv = cv + ps[j] * sw_ref[KS - 1 - j:KS - j, :]
        ysc = zcol(o_b, DS) * cv
        y_ref[:, DL:DL + DS] = (ysc * _rstd(ysc) * gso_ref[...]).astype(BF16)

    def full(shape):
        return pl.BlockSpec(shape, lambda t: (0,) * len(shape))

    return _call(
        body, name, (T // tt,),
        [pl.BlockSpec((tt, z.shape[1]), lambda t: (t, 0)),
         full(cw.shape), full(cb.shape), full(wa.shape), full(ba.shape), full(wi.shape), full(bi.shape),
         full(lam.shape), full(sw.shape), full(glo.shape), full(gso.shape)],
        [pl.BlockSpec((tt, DL), lambda t: (t, 0)), pl.BlockSpec((tt, DL + DS), lambda t: (t, 0))],
        [jax.ShapeDtypeStruct((T, DL), F32), jax.ShapeDtypeStruct((T, DL + DS), BF16)],
        [pltpu.VMEM((SUBLANES, DL), F32), pltpu.VMEM((SUBLANES, DS), F32), pltpu.VMEM((SUBLANES, DL), F32)],
        ("arbitrary",), (z, cw, cb, wa, ba, wi, bi, lam, sw, glo, gso), carry)


V_BA, V_BI, V_LAM, V_CB, V_CW, V_SW, V_GLO, V_GSO, V_ROWS = 0, 1, 2, 3, 4, 8, 11, 12, 16


def _mix_bwd(z, h, dy, cw, cb, wa, ba, wi, bi, lam, sw, glo, gso, name):
    T = z.shape[0]
    DL = cb.shape[1]
    DS = gso.shape[1]
    NH, HD = wa.shape[0], wa.shape[1]
    KL, KS = cw.shape[0], sw.shape[0]
    tt = _tile(T, 64, 16)
    nt = T // tt
    ZH = 2 * SUBLANES
    o_g, o_b, o_c, o_x = DL, 2 * DL, 2 * DL + DS, 2 * DL + 2 * DS

    def body(z_ref, zh_ref, h_ref, hh_ref, dy_ref, cw_ref, cb_ref, wa_ref, ba_ref, wi_ref, bi_ref, lam_ref,
             sw_ref, glo_ref, gso_ref, dz_ref, dwa_ref, dwi_ref, vec_ref, cdx_ref, cdc_ref, cdh_ref):
        i = pl.program_id(0)
        tr = nt - 1 - i

        @pl.when(i == 0)
        def _():
            dwa_ref[...] = jnp.zeros_like(dwa_ref)
            dwi_ref[...] = jnp.zeros_like(dwi_ref)
            vec_ref[...] = jnp.zeros_like(vec_ref)
            cdx_ref[...] = jnp.zeros_like(cdx_ref)
            cdc_ref[...] = jnp.zeros_like(cdc_ref)
            cdh_ref[...] = jnp.zeros_like(cdh_ref)

        def acc_row(r, v):
            vec_ref[pl.ds(r, 1), :] += jnp.sum(v, axis=0, keepdims=True)

        has_prev = tr > 0
        rows = lax.broadcasted_iota(jnp.int32, (tt, DL), 0)

        def zcol(o, n):
            return z_ref[:, o:o + n].astype(F32)

        def zhalo(o, n):
            return jnp.where(has_prev, zh_ref[:, o:o + n].astype(F32)[SUBLANES:, :], 0.0)

        lx = zcol(0, DL)
        xs = _taps(lx, zhalo(0, DL), KL)
        xc = cb_ref[...] + xs[0] * cw_ref[KL - 1:KL, :]
        for j in range(1, KL):
            xc = xc + xs[j] * cw_ref[KL - 1 - j:KL - j, :]
        sp, dsp = _softplus_neg(lam_ref[...])
        xcb, r, ig, a, mult = _lru_gates(xc, wa_ref, ba_ref, wi_ref, bi_ref, sp, NH, HD)
        hv = h_ref[...]
        hprev = _earlier(jnp.concatenate([jnp.where(has_prev, hh_ref[...], 0.0), hv], axis=0), 1)
        gate = zcol(o_g, DL)
        ge, dge = _gelu_and_grad(gate)
        ylru = hv * ge

        d_ylru, dglo = _rms_bwd(dy_ref[:, 0:DL].astype(F32), ylru, glo_ref[...])
        vec_ref[pl.ds(V_GLO, 1), :] += dglo
        dz_ref[:, o_g:o_g + DL] = (d_ylru * hv * dge).astype(BF16)
        bq = d_ylru * ge
        aq = jnp.where(rows == tt - 1, 1.0, pltpu.roll(a, tt - 1, 0))
        s = 1
        while s < tt:
            keep = rows < tt - s
            bq = jnp.where(keep, aq * pltpu.roll(bq, tt - s, 0) + bq, bq)
            aq = jnp.where(keep, aq * pltpu.roll(aq, tt - s, 0), aq)
            s *= 2
        dhh = bq + aq * cdh_ref[0:1, :]
        cdh_ref[0:1, :] = a[0:1, :] * dhh[0:1, :]

        da = dhh * hprev
        dmult = dhh * (ig * xc)
        d_i = dhh * mult * xc
        dxc = dhh * mult * ig
        dlog = da * a - dmult * (a * a) / mult
        acc_row(V_LAM, dlog * (-LRU_C * r) * dsp)
        dpa = dlog * (-LRU_C * sp) * r * (1.0 - r)
        dpi = d_i * ig * (1.0 - ig)
        acc_row(V_BA, dpa)
        acc_row(V_BI, dpi)
        dpab = dpa.astype(BF16)
        dpib = dpi.astype(BF16)
        back = []
        for hh in range(NH):
            sl = slice(hh * HD, (hh + 1) * HD)
            dwa_ref[hh] += _dot_tn(xcb[:, sl], dpab[:, sl])
            dwi_ref[hh] += _dot_tn(xcb[:, sl], dpib[:, sl])
            back.append(_dot_nt(dpab[:, sl], wa_ref[hh]) + _dot_nt(dpib[:, sl], wi_ref[hh]))
        dxc = dxc + jnp.concatenate(back, axis=1)

        acc_row(V_CB, dxc)
        extd = jnp.concatenate([dxc, cdx_ref[...]], axis=0)
        cdx_ref[...] = dxc[0:SUBLANES, :]
        dlx = dxc * cw_ref[KL - 1:KL, :]
        acc_row(V_CW + KL - 1, dxc * xs[0])
        for j in range(1, KL):
            dlx = dlx + _later(extd, j) * cw_ref[KL - 1 - j:KL - j, :]
            acc_row(V_CW + KL - 1 - j, dxc * xs[j])
        dz_ref[:, 0:DL] = dlx.astype(BF16)

        sb = zcol(o_b, DS)
        sc = zcol(o_c, DS)
        sx = zcol(o_x, DS)
        p = sc * sx
        ps = _taps(p, zhalo(o_c, DS) * zhalo(o_x, DS), KS)
        cv = ps[0] * sw_ref[KS - 1:KS, :]
        for j in range(1, KS):
            cv = cv + ps[j] * sw_ref[KS - 1 - j:KS - j, :]
        d_ysc, dgso = _rms_bwd(dy_ref[:, DL:DL + DS].astype(F32), sb * cv, gso_ref[...])
        vec_ref[pl.ds(V_GSO, 1), :] += dgso
        dz_ref[:, o_b:o_b + DS] = (d_ysc * cv).astype(BF16)
        dcv = d_ysc * sb
        extc = jnp.concatenate([dcv, cdc_ref[...]], axis=0)
        cdc_ref[...] = dcv[0:SUBLANES, :]
        dp = dcv * sw_ref[KS - 1:KS, :]
        acc_row(V_SW + KS - 1, dcv * ps[0])
        for j in range(1, KS):
            dp = dp + _later(extc, j) * sw_ref[KS - 1 - j:KS - j, :]
            acc_row(V_SW + KS - 1 - j, dcv * ps[j])
        dz_ref[:, o_c:o_c + DS] = (dp * sx).astype(BF16)
        dz_ref[:, o_x:o_x + DS] = (dp * sc).astype(BF16)

    def full(shape):
        return pl.BlockSpec(shape, lambda t: (0,) * len(shape))

    def rev(t):
        return nt - 1 - t

    def halo(t, rows):
        return jnp.maximum(rev(t) * (tt // rows) - 1, 0)

    return pl.pallas_call(
        body, name=name, grid=(nt,),
        in_specs=[pl.BlockSpec((tt, z.shape[1]), lambda t: (rev(t), 0)),
                  pl.BlockSpec((ZH, z.shape[1]), lambda t: (halo(t, ZH), 0)),
                  pl.BlockSpec((tt, DL), lambda t: (rev(t), 0)),
                  pl.BlockSpec((SUBLANES, DL), lambda t: (halo(t, SUBLANES), 0)),
                  pl.BlockSpec((tt, DL + DS), lambda t: (rev(t), 0)),
                  full(cw.shape), full(cb.shape), full(wa.shape), full(ba.shape), full(wi.shape), full(bi.shape),
                  full(lam.shape), full(sw.shape), full(glo.shape), full(gso.shape)],
        out_specs=[pl.BlockSpec((tt, z.shape[1]), lambda t: (rev(t), 0)),
                   full(wa.shape), full(wi.shape), full((V_ROWS, DL))],
        out_shape=[jax.ShapeDtypeStruct(z.shape, BF16), jax.ShapeDtypeStruct(wa.shape, F32),
                   jax.ShapeDtypeStruct(wi.shape, F32), jax.ShapeDtypeStruct((V_ROWS, DL), F32)],
        scratch_shapes=[pltpu.VMEM((SUBLANES, DL), F32), pltpu.VMEM((SUBLANES, DS), F32),
                        pltpu.VMEM((SUBLANES, DL), F32)],
        compiler_params=_params("arbitrary"),
    )(z, z, h, h, dy, cw, cb, wa, ba, wi, bi, lam, sw, glo, gso)


def _quad_add(s, r2, qc, axis, name):
    _, R, W = r2.shape
    tr = _tile(R, 256, 16)

    def body(qc_ref, s_ref, r0_ref, r1_ref, r2_ref, o_ref):
        o_ref[...] = ((s_ref[...].astype(F32) + r0_ref[...].astype(F32)) + r1_ref[...].astype(F32)) + r2_ref[...].astype(F32)

    blk = (None, tr, W)
    if axis == 1:
        own = pl.BlockSpec(blk, lambda i, qr: (0, i, qr[0]))
    else:
        own = pl.BlockSpec(blk, lambda i, qr: (qr[0], i, 0))
    return pl.pallas_call(
        body, name=name,
        grid_spec=pltpu.PrefetchScalarGridSpec(
            num_scalar_prefetch=1, grid=(R // tr,),
            in_specs=[own] + [pl.BlockSpec(blk, lambda i, qr, j=j: (j, i, 0)) for j in range(3)],
            out_specs=pl.BlockSpec(blk, lambda i, qr: (qr[1], i, 0))),
        out_shape=jax.ShapeDtypeStruct((2, R, W), F32),
        compiler_params=_params("parallel"),
    )(qc, s, r2, r2, r2)


CAST_BLOCKS = 4


def _cast_into_full(shards, qc, axes, name, carry=None):
    M = len(shards)
    nb = CAST_BLOCKS

    def body(qc_ref, *refs):
        for s_ref, o_ref in zip(refs[:M], refs[M:]):
            o_ref[...] = s_ref[...].astype(BF16)

    in_specs, out_specs, out_shape = [], [], []
    for s, ax in zip(shards, axes):
        R, W = s.shape
        Rh = R // 2
        tr = Rh // nb
        in_specs.append(pl.BlockSpec((tr, W), lambda hf, i, qr: (hf * nb + i, 0)))
        if ax == 1:
            out_shape.append(jax.ShapeDtypeStruct((2, Rh, 4 * W), BF16))
            out_specs.append(pl.BlockSpec((None, tr, W), lambda hf, i, qr: (hf, i, qr[0])))
        else:
            out_shape.append(jax.ShapeDtypeStruct((8, Rh, W), BF16))
            out_specs.append(pl.BlockSpec((None, tr, W), lambda hf, i, qr: (2 * qr[0] + hf, i, 0)))
    return _call(body, name, (2, nb), in_specs, out_specs, out_shape, [], ("parallel", "parallel"), shards, carry, (qc,))


def _adamw(w, g, m, v, name):
    R, C = w.shape
    tr = _tile(R, 256, SUBLANES)
    tc = C // 2 if g.ndim == 3 else _tile(C, 2048, LANES)
    c1 = 1.0 - ADAM_B1 ** ADAM_STEP
    c2 = 1.0 - ADAM_B2 ** ADAM_STEP

    def body(w_ref, g_ref, m_ref, v_ref, d_ref, mo_ref, vo_ref, go_ref):
        gv = g_ref[...]
        go_ref[...] = gv
        mn = ADAM_B1 * m_ref[...] + (1.0 - ADAM_B1) * gv
        vn = ADAM_B2 * v_ref[...] + (1.0 - ADAM_B2) * (gv * gv)
        mo_ref[...] = mn
        vo_ref[...] = vn
        d_ref[...] = -ADAM_LR * ((mn / c1) / (jnp.sqrt(vn / c2) + ADAM_EPS) + ADAM_WD * w_ref[...])

    blk = pl.BlockSpec((tr, tc), lambda i, j: (i, j))
    g_blk = pl.BlockSpec((None, tr, tc), lambda i, j: (j, i, 0)) if g.ndim == 3 else blk
    sh = jax.ShapeDtypeStruct((R, C), F32)
    return pl.pallas_call(
        body, name=name, grid=(R // tr, C // tc),
        in_specs=[blk, g_blk, blk, blk], out_specs=[blk] * 4, out_shape=[sh] * 4,
        compiler_params=_params("parallel", "parallel"),
    )(w, g, m, v)


def _other_chips(x, y):
    return [(1 - x, y), (x, 1 - y), (1 - x, 1 - y)]


def _remote(src, dst, send_sems, recv_sems, idx, dev):
    return pltpu.make_async_remote_copy(src_ref=src, dst_ref=dst, send_sem=send_sems.at[idx], recv_sem=recv_sems.at[idx],
                                        device_id=dev, device_id_type=MESH)


def _gather_carry(fulls, axes, pieces=None):
    M = len(fulls)

    def win(outs, m, qq, cc):
        Rh = fulls[m].shape[1]
        k, n = pieces[m] if pieces is not None and pieces[m] is not None else (0, 1)
        rows = pl.ds(k * (Rh // n), Rh // n)
        if axes[m] == 1:
            W = fulls[m].shape[2] // 4
            return outs[m].at[cc, rows, pl.ds(pl.multiple_of(qq * W, LANES), W)]
        return outs[m].at[2 * qq + cc, rows, :]

    def ici(outs, sems, m, j, src_q):
        x, y, c = _me()
        cx, cy = _other_chips(x, y)[j]
        blk = win(outs, m, src_q, c)
        return _remote(blk, blk, sems[0], sems[1], 6 * m + j, (cx, cy, c))

    def d2d(outs, sems, m, j, half):
        x, y, c = _me()
        cx, cy = _other_chips(x, y)[j]
        blk = win(outs, m, 2 * cx + cy, half)
        return _remote(blk, blk, sems[0], sems[1], 6 * m + 3 + j, (x, y, 1 - c))

    def start(ins, outs, sems):
        x, y, c = _me()
        for m in range(M):
            for j in range(3):
                ici(outs, sems, m, j, 2 * x + y).start()

    def middle(ins, outs, sems):
        x, y, c = _me()
        for m in range(M):
            for j, (cx, cy) in enumerate(_other_chips(x, y)):
                ici(outs, sems, m, j, 2 * cx + cy).wait_recv()
                d2d(outs, sems, m, j, c).start()

    def finish(ins, outs, sems):
        x, y, c = _me()
        for m in range(M):
            for j in range(3):
                d2d(outs, sems, m, j, 1 - c).wait_recv()
        for m in range(M):
            for j in range(3):
                ici(outs, sems, m, j, 2 * x + y).wait_send()
                d2d(outs, sems, m, j, c).wait_send()

    return _Carry(fulls, [jax.ShapeDtypeStruct(f.shape, f.dtype) for f in fulls], {m: m for m in range(M)},
                  [pltpu.SemaphoreType.DMA((6 * M,)), pltpu.SemaphoreType.DMA((6 * M,))], start, finish, middle)


def _gather_two_way_carry(fulls, axes):
    M = len(fulls)

    def part(outs, m, qq, cc, p):
        Rp = fulls[m].shape[1] // 2
        rows = pl.ds(p * Rp, Rp)
        if axes[m] == 1:
            W = fulls[m].shape[2] // 4
            return outs[m].at[cc, rows, pl.ds(pl.multiple_of(qq * W, LANES), W)]
        return outs[m].at[2 * qq + cc, rows, :]

    def copies(outs, sems):
        x, y, c = _me()
        q, qx, qy, qd = 2 * x + y, 2 * (1 - x) + y, 2 * x + (1 - y), 2 * (1 - x) + (1 - y)
        xn, yn, sib = (1 - x, y, c), (x, 1 - y, c), (x, y, 1 - c)
        table = {}
        for m in range(M):
            def cp(blk, k, dev):
                return _remote(blk, blk, sems[0], sems[1], 12 * m + k, dev)
            own0, own1 = part(outs, m, q, c, 0), part(outs, m, q, c, 1)
            table[m] = dict(
                to=[cp(own0, 0, xn), cp(own1, 1, yn), cp(own1, 2, xn), cp(own0, 3, yn)],
                landed=[cp(part(outs, m, qx, c, 0), 0, xn), cp(part(outs, m, qy, c, 1), 1, yn),
                        cp(part(outs, m, qx, c, 1), 2, xn), cp(part(outs, m, qy, c, 0), 3, yn),
                        cp(part(outs, m, qd, c, 0), 4, yn), cp(part(outs, m, qd, c, 1), 5, xn)],
                passed=[cp(part(outs, m, qx, c, 0), 4, yn), cp(part(outs, m, qy, c, 1), 5, xn)],
                handed=[cp(part(outs, m, qq, c, p), 6 + k, sib)
                        for k, (qq, p) in enumerate([(qx, 0), (qy, 1), (qx, 1), (qy, 0), (qd, 0), (qd, 1)])],
                taken=[cp(part(outs, m, qq, 1 - c, p), 6 + k, sib)
                       for k, (qq, p) in enumerate([(qx, 0), (qy, 1), (qx, 1), (qy, 0), (qd, 0), (qd, 1)])])
        return table

    def start(ins, outs, sems):
        t = copies(outs, sems)
        for m in range(M):
            for cp in t[m]['to']:
                cp.start()

    def finish(ins, outs, sems):
        t = copies(outs, sems)
        for m in range(M):
            for k in range(4):
                t[m]['landed'][k].wait_recv()
                if k < 2:
                    t[m]['passed'][k].start()
                t[m]['handed'][k].start()
        for m in range(M):
            for k in (4, 5):
                t[m]['landed'][k].wait_recv()
                t[m]['handed'][k].start()
        for m in range(M):
            for cp in t[m]['taken']:
                cp.wait_recv()
            for cp in t[m]['to'] + t[m]['passed'] + t[m]['handed']:
                cp.wait_send()

    return _Carry(fulls, [jax.ShapeDtypeStruct(f.shape, f.dtype) for f in fulls], {m: m for m in range(M)},
                  [pltpu.SemaphoreType.DMA((12 * M,)), pltpu.SemaphoreType.DMA((12 * M,))], start, finish)


def _chip_exchange_carry(sums, axes):
    M = len(sums)
    out_shape = []
    for s, ax in zip(sums, axes):
        _, Rh, C = s.shape
        out_shape.append(jax.ShapeDtypeStruct((3, Rh, C // 4 if ax == 1 else C), s.dtype))

    def copies(ins, outs, sems):
        x, y, c = _me()
        cps = []
        for m in range(M):
            for j, (cx, cy) in enumerate(_other_chips(x, y)):
                qj = 2 * cx + cy
                if axes[m] == 1:
                    W = sums[m].shape[2] // 4
                    src = ins[m].at[0, :, pl.ds(pl.multiple_of(qj * W, LANES), W)]
                else:
                    src = ins[m].at[qj]
                cps.append(_remote(src, outs[m].at[j], sems[0], sems[1], 3 * m + j, (cx, cy, c)))
        return cps

    def start(ins, outs, sems):
        for cp in copies(ins, outs, sems):
            cp.start()

    def finish(ins, outs, sems):
        for cp in copies(ins, outs, sems):
            cp.wait()

    return _Carry(sums, out_shape, {}, [pltpu.SemaphoreType.DMA((3 * M,)), pltpu.SemaphoreType.DMA((3 * M,))],
                  start, finish)


def _pair_share_carry(bufs):
    M = len(bufs)

    def start(ins, outs, sems):
        x, y, c = _me()
        for m in range(M):
            _remote(outs[m].at[c], outs[m].at[c], sems[0], sems[1], m, (x, y, 1 - c)).start()

    def finish(ins, outs, sems):
        x, y, c = _me()
        for m in range(M):
            _remote(outs[m].at[c], outs[m].at[c], sems[0], sems[1], m, (x, y, 1 - c)).wait_send()
            _remote(outs[m].at[1 - c], outs[m].at[1 - c], sems[0], sems[1], m, (x, y, 1 - c)).wait_recv()

    return _Carry(bufs, [jax.ShapeDtypeStruct(b.shape, b.dtype) for b in bufs], {m: m for m in range(M)},
                  [pltpu.SemaphoreType.DMA((M,)), pltpu.SemaphoreType.DMA((M,))], start, finish)


def _allreduce_small(v, name):
    R, W = v.shape
    Rh = R // 2

    def body(v_ref, o_ref, sib, quad, send_sems, recv_sems):
        x, y, c = _me()
        q = 2 * x + y
        sibling = (x, y, 1 - c)
        pair = _remote(v_ref, sib, send_sems, recv_sems, 0, sibling)
        pair.start()
        pair.wait()
        mine = pl.ds(pl.multiple_of(c * Rh, SUBLANES), Rh)
        quad[0] = v_ref[mine, :] + sib[mine, :]
        cps = []
        for k in (1, 2, 3):
            peer = (1 - x if k & 2 else x, 1 - y if k & 1 else y, c)
            cps.append(_remote(quad.at[0], quad.at[k], send_sems, recv_sems, k, peer))
            cps[-1].start()
        for cp in cps:
            cp.wait()
        acc = quad[q]
        for p in (1, 2, 3):
            acc = acc + quad[jnp.bitwise_xor(q, p)]
        o_ref[mine, :] = acc
        theirs = pl.ds(pl.multiple_of((1 - c) * Rh, SUBLANES), Rh)
        done = _remote(o_ref.at[mine, :], o_ref.at[mine, :], send_sems, recv_sems, 4, sibling)
        done.start()
        done.wait_send()
        _remote(o_ref.at[theirs, :], o_ref.at[theirs, :], send_sems, recv_sems, 4, sibling).wait_recv()

    vm = pl.BlockSpec(memory_space=pltpu.VMEM)
    return pl.pallas_call(
        body, name=name, in_specs=[vm], out_specs=vm, out_shape=jax.ShapeDtypeStruct((R, W), F32),
        scratch_shapes=[pltpu.VMEM((R, W), F32), pltpu.VMEM((4, Rh, W), F32),
                        pltpu.SemaphoreType.DMA((5,)), pltpu.SemaphoreType.DMA((5,))],
        compiler_params=pltpu.CompilerParams(vmem_limit_bytes=VMEM_LIMIT),
    )(v)


def _pack(pieces):
    flat = []
    for p in pieces:
        p = p.reshape(-1).astype(F32)
        pad = (-p.shape[0]) % PACK_ALIGN
        flat.append(jnp.pad(p, (0, pad)).reshape(-1, PACK_W))
    if sum(f.shape[0] for f in flat) % (2 * SUBLANES):
        flat.append(jnp.zeros((SUBLANES, PACK_W), F32))
    return jnp.concatenate(flat, axis=0)


def _unpack(packed, shapes):
    out, row = [], 0
    for shp in shapes:
        n = math.prod(shp)
        rows = -(-n // PACK_ALIGN) * SUBLANES
        out.append(packed[row:row + rows].reshape(-1)[:n].reshape(shp))
        row += rows
    return out


def kernel(x, ffn1_norm, ffn1_w_gate, ffn1_w_up, ffn1_w_down, mix_norm, w_in, lru_conv_w, lru_conv_b, lru_w_a, lru_b_a, lru_w_i, lru_b_i, lru_lambda, sc_conv_w, lru_out_norm, sc_out_norm, w_out, ffn2_norm, ffn2_w_gate, ffn2_w_up, ffn2_w_down, final_norm, loss_target, m_ffn1_norm, m_ffn1_w_gate, m_ffn1_w_up, m_ffn1_w_down, m_mix_norm, m_w_in, m_lru_conv_w, m_lru_conv_b, m_lru_w_a, m_lru_b_a, m_lru_w_i, m_lru_b_i, m_lru_lambda, m_sc_conv_w, m_lru_out_norm, m_sc_out_norm, m_w_out, m_ffn2_norm, m_ffn2_w_gate, m_ffn2_w_up, m_ffn2_w_down, m_final_norm, v_ffn1_norm, v_ffn1_w_gate, v_ffn1_w_up, v_ffn1_w_down, v_mix_norm, v_w_in, v_lru_conv_w, v_lru_conv_b, v_lru_w_a, v_lru_b_a, v_lru_w_i, v_lru_b_i, v_lru_lambda, v_sc_conv_w, v_lru_out_norm, v_sc_out_norm, v_w_out, v_ffn2_norm, v_ffn2_w_gate, v_ffn2_w_up, v_ffn2_w_down, v_final_norm):
    vals = locals()
    w = {n: vals[n] for n in WEIGHTS}
    mom = {n: vals["m_" + n] for n in WEIGHTS}
    var = {n: vals["v_" + n] for n in WEIGHTS}

    xi, yi, ci = _me()
    qi = 2 * xi + yi
    c_arr = jnp.reshape(ci, (1,)).astype(jnp.int32)
    qc_arr = jnp.stack([qi, ci]).astype(jnp.int32)

    T, D = x.shape[1], x.shape[2]
    xt = x.reshape(T, D)
    target = loss_target.reshape(T, D)
    DL = lru_conv_b.shape[-1]
    NH, HD = lru_w_a.shape[1], lru_w_a.shape[2]
    KL, KS = lru_conv_w.shape[1], sc_conv_w.shape[1]
    DLq = lru_conv_w.shape[2]

    axis_of = dict(zip(BIG, BIG_AXIS))
    placed, full = {}, {}

    def gather(names, pieces={}):
        return _gather_carry([placed[n] for n in names], [axis_of[n] for n in names], [pieces.get(n) for n in names])

    def gathered(names, views, unfinished=()):
        for n, g in zip(names, views):
            if n in unfinished:
                placed[n] = g
            else:
                full[n] = (g.reshape(2 * g.shape[1], g.shape[2]) if axis_of[n] == 1
                           else g.reshape(8 * g.shape[1], g.shape[2]))

    first, rest = 'ffn1_w_gate', [n for n in BIG if n != 'ffn1_w_gate']
    placed[first] = _cast_into_full([w[first][0]], qc_arr, [axis_of[first]], "cast_first_weight")[0]
    res = _cast_into_full([w[n][0] for n in rest], qc_arr, [axis_of[n] for n in rest], "cast_other_weights",
                          _gather_two_way_carry([placed[first]], [axis_of[first]]))
    placed.update(zip(rest, res[:len(rest)]))
    gathered([first], res[len(rest):])

    taps = jnp.zeros((2 * SUBLANES, DL), F32)
    taps = lax.dynamic_update_slice(taps, lru_conv_w[0], (0, qi * DLq))
    taps = lax.dynamic_update_slice(taps, sc_conv_w[0], (KL, qi * DLq))
    taps = _allreduce_small(jnp.where(ci == 0, taps, 0.0), "gather_conv_taps")
    cw, sw = taps[0:KL], taps[KL:KL + KS]

    cb = lru_conv_b
    wa, wi = lru_w_a[0].astype(BF16), lru_w_i[0].astype(BF16)
    ba, bi = lru_b_a.reshape(1, DL), lru_b_i.reshape(1, DL)
    mix_args = (cw, cb, wa, ba, wi, bi, lru_lambda, sw, lru_out_norm, sc_out_norm)
    gf = final_norm.reshape(1, D)

    names = ['ffn1_w_up', 'w_out']
    res = _norm_mm(xt, ffn1_norm, full['ffn1_w_gate'], "ffn1_gate", gather(names))
    n1, G1 = res[:2]
    gathered(names, res[2:])
    names = ['ffn1_w_down', 'ffn2_w_down']
    res = _ffn_up(n1, G1, full['ffn1_w_up'], "ffn1_up", gather(names, {'ffn2_w_down': (0, 2)}))
    H1, A1, B1 = res[:3]
    gathered(names, res[3:], unfinished=['ffn2_w_down'])
    names = ['w_in', 'ffn2_w_down']
    res = _mm_fullk(H1, full['ffn1_w_down'], False, xt, F32, "ffn1_down", gather(names, {'ffn2_w_down': (1, 2)}),
                    FFN_RESIDUAL_SCALE)
    x1 = res[0]
    gathered(names, res[1:])
    names = ['ffn2_w_gate']
    res = _norm_mm(x1, mix_norm, full['w_in'], "mix_in_proj", gather(names))
    n2, z = res[:2]
    gathered(names, res[2:])
    names = ['ffn2_w_up']
    res = _mix_fwd(z, *mix_args, "mix_fwd", gather(names))
    h, ymix = res[:2]
    gathered(names, res[2:])
    x2 = _mm_fullk(ymix, full['w_out'], False, x1, F32, "mix_out_proj")[0]
    n3, G2 = _norm_mm(x2, ffn2_norm, full['ffn2_w_gate'], "ffn2_gate")
    H2, A2, B2 = _ffn_up(n3, G2, full['ffn2_w_up'], "ffn2_up")
    x3 = _mm_fullk(H2, full['ffn2_w_down'], False, x2, F32, "ffn2_down", None, FFN_RESIDUAL_SCALE)[0]
    dx3, d3b, sqerr, dgf = _loss_head(x3, gf, target, "loss_head")

    sums, halves, shared = {}, {}, {}

    def dw(n, a, b, name, carry=None):
        res = _mm_tn_pair(a, b, c_arr, axis_of[n], name, carry)
        sums[n] = res[0]
        return res[2:]

    def chip_carry(names):
        return _chip_exchange_carry([sums[n] for n in names], [axis_of[n] for n in names])

    def chip_add(names, recv):
        for n, r in zip(names, recv):
            halves[n] = _quad_add(sums[n], r, qc_arr, axis_of[n], "grad_chip_add_" + n)

    dG2, dU2 = _ffn_bwd_hidden(d3b, A2, B2, full['ffn2_w_down'], "ffn2_bwd_hidden")
    dn3 = _mm_fullk(dG2, full['ffn2_w_gate'], True, None, BF16, "ffn2_bwd_input_gate")[0]
    dn3 = _mm_fullk(dU2, full['ffn2_w_up'], True, dn3, BF16, "ffn2_bwd_input_up")[0]
    dx2, dx2b, dg_ffn2 = _rms_bwd_res(dn3, x2, ffn2_norm, dx3, 1.0, "ffn2_norm_bwd")
    dw('ffn2_w_gate', n3, dG2, "ffn2_dwg")
    dw('ffn2_w_up', n3, dU2, "ffn2_dwu")
    dw('ffn2_w_down', H2, d3b, "ffn2_dwd")
    dy = _mm_fullk(dx2b, full['w_out'], True, None, BF16, "mix_out_bwd")[0]
    dz, dwa, dwi, vec = _mix_bwd(z, h, dy, *mix_args, "mix_bwd")
    dg_mix, dx1, d1b = _mm_nt_norm_bwd(dz, full['w_in'], x1, mix_norm, dx2, FFN_RESIDUAL_SCALE, "mix_in_bwd")

    def share_carry(names):
        return _pair_share_carry([halves[n] for n in names])

    res = _ffn_bwd_hidden(d1b, A1, B1, full['ffn1_w_down'], "ffn1_bwd_hidden", chip_carry(['ffn2_w_gate']))
    dG1, dU1 = res[:2]
    chip_add(['ffn2_w_gate'], res[2:])
    res = _mm_fullk(dG1, full['ffn1_w_gate'], True, None, BF16, "ffn1_bwd_input_gate", chip_carry(['ffn2_w_up']))
    chip_add(['ffn2_w_up'], res[1:])
    res = _mm_fullk(dU1, full['ffn1_w_up'], True, res[0], BF16, "ffn1_bwd_input_up", chip_carry(['ffn2_w_down']))
    dn1 = res[0]
    chip_add(['ffn2_w_down'], res[1:])
    dx0, _, dg_ffn1 = _rms_bwd_res(dn1, xt, ffn1_norm, dx1, 1.0, "ffn1_norm_bwd")
    names = ['ffn2_w_gate', 'ffn2_w_up', 'ffn2_w_down']
    shared.update(zip(names, dw('ffn1_w_gate', n1, dG1, "ffn1_dwg", share_carry(names))))
    chip_add(['ffn1_w_gate'], dw('ffn1_w_up', n1, dU1, "ffn1_dwu", chip_carry(['ffn1_w_gate'])))
    chip_add(['ffn1_w_up'], dw('ffn1_w_down', H1, d1b, "ffn1_dwd", chip_carry(['ffn1_w_up'])))
    names = ['ffn1_w_gate', 'ffn1_w_up']
    res = dw('w_in', n2, dz, "mix_dwin", _merge_carries(chip_carry(['ffn1_w_down']), share_carry(names)))
    chip_add(['ffn1_w_down'], res[:1])
    shared.update(zip(names, res[1:]))
    res = dw('w_out', ymix, dx2b, "mix_dwout", _merge_carries(chip_carry(['w_in']), share_carry(['ffn1_w_down'])))
    chip_add(['w_in'], res[:1])
    shared['ffn1_w_down'] = res[1]
    chip_add(['w_out'], _run_carry(chip_carry(['w_out']), "grad_chip_exchange_w_out"))
    names = ['w_in', 'w_out']
    shared.update(zip(names, _run_carry(share_carry(names), "grad_pair_share")))
    out_g, out_d, out_m, out_v = {}, {}, {}, {}
    for n in BIG:
        shp = w[n].shape
        g = shared[n].reshape(shp[1], shp[2]) if axis_of[n] == 1 else shared[n]
        outs = _adamw(w[n][0], g, mom[n][0], var[n][0], "adamw_" + n)
        out_d[n], out_m[n], out_v[n], out_g[n] = (a.reshape(shp) for a in outs)

    small = [n for n in WEIGHTS if n not in BIG]
    local_small = {
        'ffn1_norm': dg_ffn1, 'mix_norm': dg_mix, 'lru_conv_w': vec[V_CW:V_CW + KL], 'lru_conv_b': vec[V_CB],
        'lru_w_a': dwa, 'lru_b_a': vec[V_BA], 'lru_w_i': dwi, 'lru_b_i': vec[V_BI], 'lru_lambda': vec[V_LAM],
        'sc_conv_w': vec[V_SW:V_SW + KS], 'lru_out_norm': vec[V_GLO], 'sc_out_norm': vec[V_GSO],
        'ffn2_norm': dg_ffn2, 'final_norm': dgf,
    }
    full_shapes = [local_small[n].shape for n in small] + [(1,)]
    reduced = _allreduce_small(_pack([local_small[n] for n in small] + [sqerr[0, 0:1]]), "allreduce_small")
    reduced = _unpack(reduced, full_shapes)
    loss = (0.5 / D) * reduced[-1][0]
    gsm = {}
    for n, g in zip(small, reduced[:-1]):
        if n in SMALL_SHARDED:
            g = lax.dynamic_slice(g, (0, qi * DLq), (g.shape[0], DLq))
        gsm[n] = g.reshape(w[n].shape)
    small_shapes = [w[n].shape for n in small]
    d_s, m_s, v_s, _ = _adamw(_pack([w[n] for n in small]), _pack([gsm[n] for n in small]),
                              _pack([mom[n] for n in small]), _pack([var[n] for n in small]), "adamw_small")
    for n, d, mn, vn in zip(small, _unpack(d_s, small_shapes), _unpack(m_s, small_shapes), _unpack(v_s, small_shapes)):
        out_g[n], out_d[n], out_m[n], out_v[n] = gsm[n], d, mn, vn

    return (loss, dx0.reshape(x.shape), *[out_g[n] for n in WEIGHTS], *[out_d[n] for n in WEIGHTS],
            *[out_m[n] for n in WEIGHTS], *[out_v[n] for n in WEIGHTS])
```
